```python
import math
import jax, jax.numpy as jnp
from jax import lax
import numpy as np

D_MODEL = 1024
BATCH = 8
SEQ = 16384
DEPTH = 4

CHUNK = 64
Q_BLOCK = 128
EPS = 1e-6
N_A_LAYERS = DEPTH // 2
N_B_LAYERS = DEPTH - N_A_LAYERS

A_HEADS = 6
A_HEAD_DIM = 128
A_WIDTH = A_HEADS * A_HEAD_DIM
CONV_K = 4

B_HEADS = 6
QK_NOPE = 128
QK_ROPE = 64
V_HEAD = 128
Q_LORA = 256
KV_LORA = 256
B_WIDTH = B_HEADS * V_HEAD
ROPE_THETA = 10000.0

N_MEM = 256
MEM_HEADS = 4
MEM_HEAD_DIM = 64
MEM_WIDTH = MEM_HEADS * MEM_HEAD_DIM

D_FF = 2816

MIX_WIDTH = A_WIDTH + MEM_WIDTH
A_IN = 4 * A_WIDTH + 2 * A_HEADS + MEM_WIDTH
B_IN = Q_LORA + MEM_WIDTH

kernel_name = "hybrid_gdn_mla_yoco_macaron"


def rms_norm(x, g):
    xf = x.astype(jnp.float32)
    y = xf * lax.rsqrt(jnp.mean(xf * xf, axis=-1, keepdims=True) + EPS)
    return (y * g.astype(jnp.float32)).astype(x.dtype)


def l2_norm(x):
    xf = x.astype(jnp.float32)
    return (xf * lax.rsqrt(jnp.sum(xf * xf, axis=-1, keepdims=True) + EPS)).astype(x.dtype)


def swiglu(x, w_gu, w_down):
    gate, up = jnp.split(x @ w_gu, 2, axis=-1)
    return (jax.nn.silu(gate) * up) @ w_down


def rope_tables(positions, dim):
    inv = ROPE_THETA ** (-jnp.arange(0, dim, 2, dtype=jnp.float32) / dim)
    ang = positions.astype(jnp.float32)[..., None] * inv
    return jnp.cos(ang), jnp.sin(ang)


def apply_rope(x, cos, sin):
    xf = x.astype(jnp.float32)
    x1, x2 = jnp.split(xf, 2, axis=-1)
    c, s = cos[:, :, None, :], sin[:, :, None, :]
    return jnp.concatenate([x1 * c - x2 * s, x1 * s + x2 * c], axis=-1).astype(x.dtype)


def causal_depthwise_conv(x, w):
    k = w.shape[0]
    return lax.conv_general_dilated(
        x, w[:, None, :].astype(x.dtype), window_strides=(1,), padding=[(k - 1, 0)],
        dimension_numbers=("NWC", "WIO", "NWC"), feature_group_count=x.shape[-1])


def chunked_gated_delta_rule(q, k, v, beta, g):
    b, s, h, dk = q.shape
    dv = v.shape[-1]
    n = s // CHUNK
    f32 = jnp.float32

    def chunks(t):
        return t.astype(f32).reshape(b, n, CHUNK, h, -1).transpose(0, 3, 1, 2, 4)

    q, k, v = chunks(q), chunks(k), chunks(v)
    beta = chunks(beta[..., None])[..., 0]
    g_cum = jnp.cumsum(chunks(g[..., None])[..., 0], axis=-1)
    causal = jnp.tril(jnp.ones((CHUNK, CHUNK), dtype=bool))
    strict = jnp.tril(jnp.ones((CHUNK, CHUNK), dtype=bool), -1)
    diff = g_cum[..., :, None] - g_cum[..., None, :]
    decay = jnp.where(causal, jnp.exp(jnp.where(causal, diff, 0.0)), 0.0)
    k_beta = k * beta[..., None]
    t_mat = jnp.where(strict, jnp.einsum("bhnid,bhnjd->bhnij", k_beta, k) * decay, 0.0) \
        + jnp.eye(CHUNK, dtype=f32)
    u = lax.linalg.triangular_solve(t_mat, v * beta[..., None], left_side=True,
                                    lower=True, unit_diagonal=True)
    w = lax.linalg.triangular_solve(t_mat, k_beta * jnp.exp(g_cum)[..., None], left_side=True,
                                    lower=True, unit_diagonal=True)
    qk = jnp.einsum("bhnid,bhnjd->bhnij", q, k) * decay
    g_last = g_cum[..., -1:]
    q_dec = q * jnp.exp(g_cum)[..., None]
    k_dec = k * jnp.exp(g_last - g_cum)[..., None]
    chunk_decay = jnp.exp(g_last[..., 0])
    xs = tuple(jnp.moveaxis(t, 2, 0) for t in (q_dec, k_dec, w, u, qk, chunk_decay))

    def step(state, inp):
        q_c, k_c, w_c, u_c, qk_c, d_c = inp
        v_new = u_c - jnp.einsum("bhcd,bhde->bhce", w_c, state)
        out = jnp.einsum("bhcd,bhde->bhce", q_c, state) + jnp.einsum("bhij,bhje->bhie", qk_c, v_new)
        state = state * d_c[..., None, None] + jnp.einsum("bhcd,bhce->bhde", k_c, v_new)
        return state, out

    _, o = lax.scan(step, jnp.zeros((b, h, dk, dv), f32), xs)
    return o.transpose(1, 0, 3, 2, 4).reshape(b, s, h, dv)


def gated_deltanet(qkv, gate, b_raw, a_raw, conv_w, A_log, dt_bias, out_gain):
    b, s, _ = qkv.shape
    qkv = jax.nn.silu(causal_depthwise_conv(qkv, conv_w))
    q, k, v = (t.reshape(b, s, A_HEADS, A_HEAD_DIM) for t in jnp.split(qkv, 3, axis=-1))
    q = l2_norm(q) * A_HEAD_DIM ** -0.5
    k = l2_norm(k)
    beta = jax.nn.sigmoid(b_raw.astype(jnp.float32))
    g = -jnp.exp(A_log.astype(jnp.float32)) * jax.nn.softplus(
        a_raw.astype(jnp.float32) + dt_bias.astype(jnp.float32))
    o = chunked_gated_delta_rule(q, k, v, beta, g).astype(qkv.dtype)
    o = rms_norm(o, out_gain) * jax.nn.silu(gate.reshape(b, s, A_HEADS, A_HEAD_DIM))
    return o.reshape(b, s, A_WIDTH)


def mla_attention(q_nope, q_rope, k_nope, k_rope, v):
    b, s, h, _ = q_nope.shape
    nb = s // Q_BLOCK
    scale = (QK_NOPE + QK_ROPE) ** -0.5
    key_chunk = jnp.arange(s) // CHUNK
    qn = q_nope.reshape(b, nb, Q_BLOCK, h, QK_NOPE).swapaxes(0, 1)
    qr = q_rope.reshape(b, nb, Q_BLOCK, h, QK_ROPE).swapaxes(0, 1)

    def block(args):
        i, qn_b, qr_b = args
        sc = (jnp.einsum("bqhd,bkhd->bhqk", qn_b, k_nope)
              + jnp.einsum("bqhd,bkd->bhqk", qr_b, k_rope)).astype(jnp.float32) * scale
        q_chunk = (i * Q_BLOCK + jnp.arange(Q_BLOCK)) // CHUNK
        mask = key_chunk[None, :] <= q_chunk[:, None]
        p = jax.nn.softmax(jnp.where(mask, sc, -jnp.inf), axis=-1).astype(v.dtype)
        return jnp.einsum("bhqk,bkhd->bqhd", p, v)

    out = lax.map(block, (jnp.arange(nb), qn, qr))
    return out.swapaxes(0, 1).reshape(b, s, h * V_HEAD)


def memory_attention(q, mem_kv):
    b, s, _ = q.shape
    q = q.reshape(b, s, MEM_HEADS, MEM_HEAD_DIM)
    k, v = (t.reshape(b, N_MEM, MEM_HEADS, MEM_HEAD_DIM) for t in jnp.split(mem_kv, 2, axis=-1))
    sc = jnp.einsum("bqhd,bmhd->bhqm", q, k).astype(jnp.float32) * MEM_HEAD_DIM ** -0.5
    p = jax.nn.softmax(sc, axis=-1).astype(v.dtype)
    return jnp.einsum("bhqm,bmhd->bqhd", p, v).reshape(b, s, MEM_WIDTH)


def _fwd_setup_inputs(seed: int = 0) -> dict:
    key = jax.random.key(seed)
    ks = iter(jax.random.split(key, 40))
    f32 = jnp.float32

    def dense(shape, fan_in):
        return jax.random.normal(next(ks), shape, f32) * fan_in ** -0.5

    def gain(shape):
        return 1.0 + 0.02 * jax.random.normal(next(ks), shape, f32)

    x = jax.random.normal(next(ks), (BATCH, SEQ, D_MODEL), f32)
    mem = jax.random.normal(next(ks), (BATCH, N_MEM, D_MODEL), f32)
    offset = jax.random.randint(next(ks), (BATCH, 1), 0, 64, dtype=jnp.int32) * CHUNK
    positions = (offset + jnp.arange(SEQ, dtype=jnp.int32)[None, :]).astype(jnp.int32)

    ffn1_norm = gain((DEPTH, D_MODEL))
    ffn1_w_gu = dense((DEPTH, D_MODEL, 2 * D_FF), D_MODEL)
    ffn1_w_down = dense((DEPTH, D_FF, D_MODEL), D_FF)
    mix_norm = gain((DEPTH, D_MODEL))
    ffn2_norm = gain((DEPTH, D_MODEL))
    ffn2_w_gu = dense((DEPTH, D_MODEL, 2 * D_FF), D_MODEL)
    ffn2_w_down = dense((DEPTH, D_FF, D_MODEL), D_FF)
    w_out = dense((DEPTH, MIX_WIDTH, D_MODEL), MIX_WIDTH)
    mem_norm = gain((D_MODEL,))
    w_mem_kv = dense((DEPTH, D_MODEL, 2 * MEM_WIDTH), D_MODEL)

    a_w_in = dense((N_A_LAYERS, D_MODEL, A_IN), D_MODEL)
    a_conv = dense((N_A_LAYERS, CONV_K, 3 * A_WIDTH), CONV_K)
    a_A_log = jnp.log(jax.random.uniform(next(ks), (N_A_LAYERS, A_HEADS), f32, 1.0, 16.0))
    dt = jnp.exp(jax.random.uniform(next(ks), (N_A_LAYERS, A_HEADS), f32,
                                    math.log(1e-3), math.log(1e-1)))
    a_dt_bias = dt + jnp.log(-jnp.expm1(-dt))
    a_out_norm = gain((N_A_LAYERS, A_HEAD_DIM))

    b_w_in = dense((N_B_LAYERS, D_MODEL, B_IN), D_MODEL)
    b_q_norm = gain((N_B_LAYERS, Q_LORA))
    b_w_uq = dense((N_B_LAYERS, Q_LORA, B_HEADS * (QK_NOPE + QK_ROPE)), Q_LORA)

    kv_in_norm = gain((D_MODEL,))
    w_dkv = dense((D_MODEL, KV_LORA + QK_ROPE), D_MODEL)
    kv_lat_norm = gain((KV_LORA,))
    w_ukv = dense((KV_LORA, B_HEADS * (QK_NOPE + V_HEAD)), KV_LORA)
    final_norm = gain((D_MODEL,))

    return {"x": x, "mem": mem, "positions": positions,
            "ffn1_norm": ffn1_norm, "ffn1_w_gu": ffn1_w_gu, "ffn1_w_down": ffn1_w_down,
            "mix_norm": mix_norm,
            "ffn2_norm": ffn2_norm, "ffn2_w_gu": ffn2_w_gu, "ffn2_w_down": ffn2_w_down,
            "w_out": w_out, "mem_norm": mem_norm, "w_mem_kv": w_mem_kv,
            "a_w_in": a_w_in, "a_conv": a_conv, "a_A_log": a_A_log, "a_dt_bias": a_dt_bias,
            "a_out_norm": a_out_norm,
            "b_w_in": b_w_in, "b_q_norm": b_q_norm, "b_w_uq": b_w_uq,
            "kv_in_norm": kv_in_norm, "w_dkv": w_dkv, "kv_lat_norm": kv_lat_norm, "w_ukv": w_ukv,
            "final_norm": final_norm}


def _fwd_reference(x, mem, positions, ffn1_norm, ffn1_w_gu, ffn1_w_down, mix_norm,
              ffn2_norm, ffn2_w_gu, ffn2_w_down, w_out, mem_norm, w_mem_kv,
              a_w_in, a_conv, a_A_log, a_dt_bias, a_out_norm,
              b_w_in, b_q_norm, b_w_uq, kv_in_norm, w_dkv, kv_lat_norm, w_ukv, final_norm):
    b, s, _ = x.shape
    mem_n = rms_norm(mem, mem_norm)
    cos, sin = rope_tables(positions, QK_ROPE)

    for i in range(N_A_LAYERS):
        l = i
        x = x + 0.5 * swiglu(rms_norm(x, ffn1_norm[l]), ffn1_w_gu[l], ffn1_w_down[l])
        h = rms_norm(x, mix_norm[l]) @ a_w_in[i]
        qkv, gate, b_raw, a_raw, q_mem = jnp.split(
            h, [3 * A_WIDTH, 4 * A_WIDTH, 4 * A_WIDTH + A_HEADS, 4 * A_WIDTH + 2 * A_HEADS], axis=-1)
        o_a = gated_deltanet(qkv, gate, b_raw, a_raw, a_conv[i], a_A_log[i], a_dt_bias[i], a_out_norm[i])
        o_m = memory_attention(q_mem, mem_n @ w_mem_kv[l])
        x = x + jnp.concatenate([o_a, o_m], axis=-1) @ w_out[l]
        x = x + 0.5 * swiglu(rms_norm(x, ffn2_norm[l]), ffn2_w_gu[l], ffn2_w_down[l])

    ckr = rms_norm(x, kv_in_norm) @ w_dkv
    c_kv = rms_norm(ckr[..., :KV_LORA], kv_lat_norm)
    k_rope = apply_rope(ckr[..., None, KV_LORA:], cos, sin)[:, :, 0]
    k_nope, v_mla = jnp.split((c_kv @ w_ukv).reshape(b, s, B_HEADS, QK_NOPE + V_HEAD), [QK_NOPE], axis=-1)

    for j in range(N_B_LAYERS):
        l = N_A_LAYERS + j
        x = x + 0.5 * swiglu(rms_norm(x, ffn1_norm[l]), ffn1_w_gu[l], ffn1_w_down[l])
        h = rms_norm(x, mix_norm[l]) @ b_w_in[j]
        cq, q_mem = jnp.split(h, [Q_LORA], axis=-1)
        q = (rms_norm(cq, b_q_norm[j]) @ b_w_uq[j]).reshape(b, s, B_HEADS, QK_NOPE + QK_ROPE)
        q_nope, q_rope = jnp.split(q, [QK_NOPE], axis=-1)
        q_rope = apply_rope(q_rope, cos, sin)
        o_b = mla_attention(q_nope, q_rope, k_nope, k_rope, v_mla)
        o_m = memory_attention(q_mem, mem_n @ w_mem_kv[l])
        x = x + jnp.concatenate([o_b, o_m], axis=-1) @ w_out[l]
        x = x + 0.5 * swiglu(rms_norm(x, ffn2_norm[l]), ffn2_w_gu[l], ffn2_w_down[l])

    return rms_norm(x, final_norm)


import jax as _jax
import jax.numpy as _jnp

TWIN_FORMAT = 'train_step'
FWD_PARAMS = ['x', 'mem', 'positions', 'ffn1_norm', 'ffn1_w_gu', 'ffn1_w_down', 'mix_norm', 'ffn2_norm', 'ffn2_w_gu', 'ffn2_w_down', 'w_out', 'mem_norm', 'w_mem_kv', 'a_w_in', 'a_conv', 'a_A_log', 'a_dt_bias', 'a_out_norm', 'b_w_in', 'b_q_norm', 'b_w_uq', 'kv_in_norm', 'w_dkv', 'kv_lat_norm', 'w_ukv', 'final_norm']
TWIN_WEIGHTS = ['ffn1_norm', 'ffn1_w_gu', 'ffn1_w_down', 'mix_norm', 'ffn2_norm', 'ffn2_w_gu', 'ffn2_w_down', 'w_out', 'mem_norm', 'w_mem_kv', 'a_w_in', 'a_conv', 'a_A_log', 'a_dt_bias', 'a_out_norm', 'b_w_in', 'b_q_norm', 'b_w_uq', 'kv_in_norm', 'w_dkv', 'kv_lat_norm', 'w_ukv', 'final_norm']
TWIN_DIFF_INPUT = 'x'
TWIN_INPUTS = ['x', 'mem', 'positions', 'ffn1_norm', 'ffn1_w_gu', 'ffn1_w_down', 'mix_norm', 'ffn2_norm', 'ffn2_w_gu', 'ffn2_w_down', 'w_out', 'mem_norm', 'w_mem_kv', 'a_w_in', 'a_conv', 'a_A_log', 'a_dt_bias', 'a_out_norm', 'b_w_in', 'b_q_norm', 'b_w_uq', 'kv_in_norm', 'w_dkv', 'kv_lat_norm', 'w_ukv', 'final_norm', 'loss_target', 'm_ffn1_norm', 'm_ffn1_w_gu', 'm_ffn1_w_down', 'm_mix_norm', 'm_ffn2_norm', 'm_ffn2_w_gu', 'm_ffn2_w_down', 'm_w_out', 'm_mem_norm', 'm_w_mem_kv', 'm_a_w_in', 'm_a_conv', 'm_a_A_log', 'm_a_dt_bias', 'm_a_out_norm', 'm_b_w_in', 'm_b_q_norm', 'm_b_w_uq', 'm_kv_in_norm', 'm_w_dkv', 'm_kv_lat_norm', 'm_w_ukv', 'm_final_norm', 'v_ffn1_norm', 'v_ffn1_w_gu', 'v_ffn1_w_down', 'v_mix_norm', 'v_ffn2_norm', 'v_ffn2_w_gu', 'v_ffn2_w_down', 'v_w_out', 'v_mem_norm', 'v_w_mem_kv', 'v_a_w_in', 'v_a_conv', 'v_a_A_log', 'v_a_dt_bias', 'v_a_out_norm', 'v_b_w_in', 'v_b_q_norm', 'v_b_w_uq', 'v_kv_in_norm', 'v_w_dkv', 'v_kv_lat_norm', 'v_w_ukv', 'v_final_norm']
TWIN_OUTPUTS = ['loss', 'grad_x', 'grad_ffn1_norm', 'grad_ffn1_w_gu', 'grad_ffn1_w_down', 'grad_mix_norm', 'grad_ffn2_norm', 'grad_ffn2_w_gu', 'grad_ffn2_w_down', 'grad_w_out', 'grad_mem_norm', 'grad_w_mem_kv', 'grad_a_w_in', 'grad_a_conv', 'grad_a_A_log', 'grad_a_dt_bias', 'grad_a_out_norm', 'grad_b_w_in', 'grad_b_q_norm', 'grad_b_w_uq', 'grad_kv_in_norm', 'grad_w_dkv', 'grad_kv_lat_norm', 'grad_w_ukv', 'grad_final_norm', 'delta_ffn1_norm', 'delta_ffn1_w_gu', 'delta_ffn1_w_down', 'delta_mix_norm', 'delta_ffn2_norm', 'delta_ffn2_w_gu', 'delta_ffn2_w_down', 'delta_w_out', 'delta_mem_norm', 'delta_w_mem_kv', 'delta_a_w_in', 'delta_a_conv', 'delta_a_A_log', 'delta_a_dt_bias', 'delta_a_out_norm', 'delta_b_w_in', 'delta_b_q_norm', 'delta_b_w_uq', 'delta_kv_in_norm', 'delta_w_dkv', 'delta_kv_lat_norm', 'delta_w_ukv', 'delta_final_norm', 'new_m_ffn1_norm', 'new_m_ffn1_w_gu', 'new_m_ffn1_w_down', 'new_m_mix_norm', 'new_m_ffn2_norm', 'new_m_ffn2_w_gu', 'new_m_ffn2_w_down', 'new_m_w_out', 'new_m_mem_norm', 'new_m_w_mem_kv', 'new_m_a_w_in', 'new_m_a_conv', 'new_m_a_A_log', 'new_m_a_dt_bias', 'new_m_a_out_norm', 'new_m_b_w_in', 'new_m_b_q_norm', 'new_m_b_w_uq', 'new_m_kv_in_norm', 'new_m_w_dkv', 'new_m_kv_lat_norm', 'new_m_w_ukv', 'new_m_final_norm', 'new_v_ffn1_norm', 'new_v_ffn1_w_gu', 'new_v_ffn1_w_down', 'new_v_mix_norm', 'new_v_ffn2_norm', 'new_v_ffn2_w_gu', 'new_v_ffn2_w_down', 'new_v_w_out', 'new_v_mem_norm', 'new_v_w_mem_kv', 'new_v_a_w_in', 'new_v_a_conv', 'new_v_a_A_log', 'new_v_a_dt_bias', 'new_v_a_out_norm', 'new_v_b_w_in', 'new_v_b_q_norm', 'new_v_b_w_uq', 'new_v_kv_in_norm', 'new_v_w_dkv', 'new_v_kv_lat_norm', 'new_v_w_ukv', 'new_v_final_norm']
TWIN_LEAF_KINDS = {'loss': 'loss', 'grad_x': 'grad_x', 'grad_ffn1_norm': 'grad_w', 'grad_ffn1_w_gu': 'grad_w', 'grad_ffn1_w_down': 'grad_w', 'grad_mix_norm': 'grad_w', 'grad_ffn2_norm': 'grad_w', 'grad_ffn2_w_gu': 'grad_w', 'grad_ffn2_w_down': 'grad_w', 'grad_w_out': 'grad_w', 'grad_mem_norm': 'grad_w', 'grad_w_mem_kv': 'grad_w', 'grad_a_w_in': 'grad_w', 'grad_a_conv': 'grad_w', 'grad_a_A_log': 'grad_w', 'grad_a_dt_bias': 'grad_w', 'grad_a_out_norm': 'grad_w', 'grad_b_w_in': 'grad_w', 'grad_b_q_norm': 'grad_w', 'grad_b_w_uq': 'grad_w', 'grad_kv_in_norm': 'grad_w', 'grad_w_dkv': 'grad_w', 'grad_kv_lat_norm': 'grad_w', 'grad_w_ukv': 'grad_w', 'grad_final_norm': 'grad_w', 'delta_ffn1_norm': 'delta_w', 'delta_ffn1_w_gu': 'delta_w', 'delta_ffn1_w_down': 'delta_w', 'delta_mix_norm': 'delta_w', 'delta_ffn2_norm': 'delta_w', 'delta_ffn2_w_gu': 'delta_w', 'delta_ffn2_w_down': 'delta_w', 'delta_w_out': 'delta_w', 'delta_mem_norm': 'delta_w', 'delta_w_mem_kv': 'delta_w', 'delta_a_w_in': 'delta_w', 'delta_a_conv': 'delta_w', 'delta_a_A_log': 'delta_w', 'delta_a_dt_bias': 'delta_w', 'delta_a_out_norm': 'delta_w', 'delta_b_w_in': 'delta_w', 'delta_b_q_norm': 'delta_w', 'delta_b_w_uq': 'delta_w', 'delta_kv_in_norm': 'delta_w', 'delta_w_dkv': 'delta_w', 'delta_kv_lat_norm': 'delta_w', 'delta_w_ukv': 'delta_w', 'delta_final_norm': 'delta_w', 'new_m_ffn1_norm': 'new_m', 'new_m_ffn1_w_gu': 'new_m', 'new_m_ffn1_w_down': 'new_m', 'new_m_mix_norm': 'new_m', 'new_m_ffn2_norm': 'new_m', 'new_m_ffn2_w_gu': 'new_m', 'new_m_ffn2_w_down': 'new_m', 'new_m_w_out': 'new_m', 'new_m_mem_norm': 'new_m', 'new_m_w_mem_kv': 'new_m', 'new_m_a_w_in': 'new_m', 'new_m_a_conv': 'new_m', 'new_m_a_A_log': 'new_m', 'new_m_a_dt_bias': 'new_m', 'new_m_a_out_norm': 'new_m', 'new_m_b_w_in': 'new_m', 'new_m_b_q_norm': 'new_m', 'new_m_b_w_uq': 'new_m', 'new_m_kv_in_norm': 'new_m', 'new_m_w_dkv': 'new_m', 'new_m_kv_lat_norm': 'new_m', 'new_m_w_ukv': 'new_m', 'new_m_final_norm': 'new_m', 'new_v_ffn1_norm': 'new_v', 'new_v_ffn1_w_gu': 'new_v', 'new_v_ffn1_w_down': 'new_v', 'new_v_mix_norm': 'new_v', 'new_v_ffn2_norm': 'new_v', 'new_v_ffn2_w_gu': 'new_v', 'new_v_ffn2_w_down': 'new_v', 'new_v_w_out': 'new_v', 'new_v_mem_norm': 'new_v', 'new_v_w_mem_kv': 'new_v', 'new_v_a_w_in': 'new_v', 'new_v_a_conv': 'new_v', 'new_v_a_A_log': 'new_v', 'new_v_a_dt_bias': 'new_v', 'new_v_a_out_norm': 'new_v', 'new_v_b_w_in': 'new_v', 'new_v_b_q_norm': 'new_v', 'new_v_b_w_uq': 'new_v', 'new_v_kv_in_norm': 'new_v', 'new_v_w_dkv': 'new_v', 'new_v_kv_lat_norm': 'new_v', 'new_v_w_ukv': 'new_v', 'new_v_final_norm': 'new_v'}


def _forward(args):
    return _fwd_reference(*[args[k] for k in FWD_PARAMS])


def _output_shape():
    def fwd():
        inp = _fwd_setup_inputs(0)
        return _fwd_reference(*[inp[k] for k in FWD_PARAMS])
    out = _jax.eval_shape(fwd)
    return out.shape, out.dtype

N_MICROBATCH = 1
ADAM_LR = 0.001
ADAM_B1 = 0.9
ADAM_B2 = 0.999
ADAM_EPS = 1e-08
ADAM_WD = 0.01
ADAM_STEP = 10
PER_EXAMPLE_BATCH_AXIS = {'x': 0, 'mem': 0, 'positions': 0, 'loss_target': 0}
SHARED_INPUTS = []
_WEIGHT_DTYPES = {'ffn1_norm': _jnp.float32, 'ffn1_w_gu': _jnp.float32, 'ffn1_w_down': _jnp.float32, 'mix_norm': _jnp.float32, 'ffn2_norm': _jnp.float32, 'ffn2_w_gu': _jnp.float32, 'ffn2_w_down': _jnp.float32, 'w_out': _jnp.float32, 'mem_norm': _jnp.float32, 'w_mem_kv': _jnp.float32, 'a_w_in': _jnp.float32, 'a_conv': _jnp.float32, 'a_A_log': _jnp.float32, 'a_dt_bias': _jnp.float32, 'a_out_norm': _jnp.float32, 'b_w_in': _jnp.float32, 'b_q_norm': _jnp.float32, 'b_w_uq': _jnp.float32, 'kv_in_norm': _jnp.float32, 'w_dkv': _jnp.float32, 'kv_lat_norm': _jnp.float32, 'w_ukv': _jnp.float32, 'final_norm': _jnp.float32}
MOMENT_SCALE = {'ffn1_norm': 1.610796e-01, 'ffn1_w_gu': 6.775832e-02, 'ffn1_w_down': 1.104903e-01, 'mix_norm': 2.186098e-01, 'ffn2_norm': 1.340431e-01, 'ffn2_w_gu': 5.431243e-02, 'ffn2_w_down': 8.870276e-02, 'w_out': 1.335369e-01, 'mem_norm': 5.389078e-02, 'w_mem_kv': 3.440268e-02, 'a_w_in': 1.699111e-01, 'a_conv': 1.641254e-01, 'a_A_log': 8.339469e-01, 'a_dt_bias': 8.010430e-01, 'a_out_norm': 6.006784e-01, 'b_w_in': 4.966307e-02, 'b_q_norm': 6.529199e-02, 'b_w_uq': 3.051240e-02, 'kv_in_norm': 7.139258e-02, 'w_dkv': 1.295788e-01, 'kv_lat_norm': 1.474019e-01, 'w_ukv': 5.844171e-02, 'final_norm': 1.281085e+02}


def _to_microbatches(a, axis):
    t = _jnp.moveaxis(a, axis, 0)
    t = t.reshape((N_MICROBATCH, t.shape[0] // N_MICROBATCH) + t.shape[1:])
    return _jnp.moveaxis(t, 1, axis + 1)


def setup_inputs(seed: int = 0) -> dict:
    inp = _fwd_setup_inputs(seed)
    key = _jax.random.fold_in(_jax.random.key(seed), 7919)
    shape, _ = _output_shape()
    out = dict(inp)
    out["loss_target"] = _jax.random.normal(_jax.random.fold_in(key, 0), shape, _jnp.float32)
    for i, name in enumerate(TWIN_WEIGHTS):
        w = inp[name].astype(_jnp.float32)
        if MOMENT_SCALE is None:
            s = _jnp.sqrt(_jnp.mean(_jnp.square(w)) + 1e-30)
        else:
            s = MOMENT_SCALE[name]
        km, kv = _jax.random.split(_jax.random.fold_in(key, i + 1))
        out[name] = w
        out["m_" + name] = s * _jax.random.normal(km, w.shape, _jnp.float32)
        out["v_" + name] = (s * s) * _jax.random.uniform(kv, w.shape, _jnp.float32, 0.5, 1.5)
    if N_MICROBATCH > 1:
        for name, axis in PER_EXAMPLE_BATCH_AXIS.items():
            out[name] = _to_microbatches(out[name], axis)
    return {'x': out['x'], 'mem': out['mem'], 'positions': out['positions'], 'ffn1_norm': out['ffn1_norm'], 'ffn1_w_gu': out['ffn1_w_gu'], 'ffn1_w_down': out['ffn1_w_down'], 'mix_norm': out['mix_norm'], 'ffn2_norm': out['ffn2_norm'], 'ffn2_w_gu': out['ffn2_w_gu'], 'ffn2_w_down': out['ffn2_w_down'], 'w_out': out['w_out'], 'mem_norm': out['mem_norm'], 'w_mem_kv': out['w_mem_kv'], 'a_w_in': out['a_w_in'], 'a_conv': out['a_conv'], 'a_A_log': out['a_A_log'], 'a_dt_bias': out['a_dt_bias'], 'a_out_norm': out['a_out_norm'], 'b_w_in': out['b_w_in'], 'b_q_norm': out['b_q_norm'], 'b_w_uq': out['b_w_uq'], 'kv_in_norm': out['kv_in_norm'], 'w_dkv': out['w_dkv'], 'kv_lat_norm': out['kv_lat_norm'], 'w_ukv': out['w_ukv'], 'final_norm': out['final_norm'], 'loss_target': out['loss_target'], 'm_ffn1_norm': out['m_ffn1_norm'], 'm_ffn1_w_gu': out['m_ffn1_w_gu'], 'm_ffn1_w_down': out['m_ffn1_w_down'], 'm_mix_norm': out['m_mix_norm'], 'm_ffn2_norm': out['m_ffn2_norm'], 'm_ffn2_w_gu': out['m_ffn2_w_gu'], 'm_ffn2_w_down': out['m_ffn2_w_down'], 'm_w_out': out['m_w_out'], 'm_mem_norm': out['m_mem_norm'], 'm_w_mem_kv': out['m_w_mem_kv'], 'm_a_w_in': out['m_a_w_in'], 'm_a_conv': out['m_a_conv'], 'm_a_A_log': out['m_a_A_log'], 'm_a_dt_bias': out['m_a_dt_bias'], 'm_a_out_norm': out['m_a_out_norm'], 'm_b_w_in': out['m_b_w_in'], 'm_b_q_norm': out['m_b_q_norm'], 'm_b_w_uq': out['m_b_w_uq'], 'm_kv_in_norm': out['m_kv_in_norm'], 'm_w_dkv': out['m_w_dkv'], 'm_kv_lat_norm': out['m_kv_lat_norm'], 'm_w_ukv': out['m_w_ukv'], 'm_final_norm': out['m_final_norm'], 'v_ffn1_norm': out['v_ffn1_norm'], 'v_ffn1_w_gu': out['v_ffn1_w_gu'], 'v_ffn1_w_down': out['v_ffn1_w_down'], 'v_mix_norm': out['v_mix_norm'], 'v_ffn2_norm': out['v_ffn2_norm'], 'v_ffn2_w_gu': out['v_ffn2_w_gu'], 'v_ffn2_w_down': out['v_ffn2_w_down'], 'v_w_out': out['v_w_out'], 'v_mem_norm': out['v_mem_norm'], 'v_w_mem_kv': out['v_w_mem_kv'], 'v_a_w_in': out['v_a_w_in'], 'v_a_conv': out['v_a_conv'], 'v_a_A_log': out['v_a_A_log'], 'v_a_dt_bias': out['v_a_dt_bias'], 'v_a_out_norm': out['v_a_out_norm'], 'v_b_w_in': out['v_b_w_in'], 'v_b_q_norm': out['v_b_q_norm'], 'v_b_w_uq': out['v_b_w_uq'], 'v_kv_in_norm': out['v_kv_in_norm'], 'v_w_dkv': out['v_w_dkv'], 'v_kv_lat_norm': out['v_kv_lat_norm'], 'v_w_ukv': out['v_w_ukv'], 'v_final_norm': out['v_final_norm']}


def _loss(weights, diff, rest, loss_target):
    with _jax.named_scope("forward"):
        args = {**rest, TWIN_DIFF_INPUT: diff, **{k: w.astype(_WEIGHT_DTYPES[k]) for k, w in weights.items()}}
        y = _forward(args)
    with _jax.named_scope("loss_head"):
        err = _jnp.square(y.astype(_jnp.float32) - loss_target)
        return 0.5 * _jnp.sum(_jnp.mean(err, axis=-1)) if err.ndim else 0.5 * err


def _adamw(w, g, m, v):
    m = ADAM_B1 * m + (1.0 - ADAM_B1) * g
    v = ADAM_B2 * v + (1.0 - ADAM_B2) * _jnp.square(g)
    m_hat = m / (1.0 - ADAM_B1 ** ADAM_STEP)
    v_hat = v / (1.0 - ADAM_B2 ** ADAM_STEP)
    delta = -ADAM_LR * (m_hat / (_jnp.sqrt(v_hat) + ADAM_EPS) + ADAM_WD * w)
    return delta, m, v


def reference(x, mem, positions, ffn1_norm, ffn1_w_gu, ffn1_w_down, mix_norm, ffn2_norm, ffn2_w_gu, ffn2_w_down, w_out, mem_norm, w_mem_kv, a_w_in, a_conv, a_A_log, a_dt_bias, a_out_norm, b_w_in, b_q_norm, b_w_uq, kv_in_norm, w_dkv, kv_lat_norm, w_ukv, final_norm, loss_target, m_ffn1_norm, m_ffn1_w_gu, m_ffn1_w_down, m_mix_norm, m_ffn2_norm, m_ffn2_w_gu, m_ffn2_w_down, m_w_out, m_mem_norm, m_w_mem_kv, m_a_w_in, m_a_conv, m_a_A_log, m_a_dt_bias, m_a_out_norm, m_b_w_in, m_b_q_norm, m_b_w_uq, m_kv_in_norm, m_w_dkv, m_kv_lat_norm, m_w_ukv, m_final_norm, v_ffn1_norm, v_ffn1_w_gu, v_ffn1_w_down, v_mix_norm, v_ffn2_norm, v_ffn2_w_gu, v_ffn2_w_down, v_w_out, v_mem_norm, v_w_mem_kv, v_a_w_in, v_a_conv, v_a_A_log, v_a_dt_bias, v_a_out_norm, v_b_w_in, v_b_q_norm, v_b_w_uq, v_kv_in_norm, v_w_dkv, v_kv_lat_norm, v_w_ukv, v_final_norm):
    given = dict(x=x, mem=mem, positions=positions, ffn1_norm=ffn1_norm, ffn1_w_gu=ffn1_w_gu, ffn1_w_down=ffn1_w_down, mix_norm=mix_norm, ffn2_norm=ffn2_norm, ffn2_w_gu=ffn2_w_gu, ffn2_w_down=ffn2_w_down, w_out=w_out, mem_norm=mem_norm, w_mem_kv=w_mem_kv, a_w_in=a_w_in, a_conv=a_conv, a_A_log=a_A_log, a_dt_bias=a_dt_bias, a_out_norm=a_out_norm, b_w_in=b_w_in, b_q_norm=b_q_norm, b_w_uq=b_w_uq, kv_in_norm=kv_in_norm, w_dkv=w_dkv, kv_lat_norm=kv_lat_norm, w_ukv=w_ukv, final_norm=final_norm, loss_target=loss_target, m_ffn1_norm=m_ffn1_norm, m_ffn1_w_gu=m_ffn1_w_gu, m_ffn1_w_down=m_ffn1_w_down, m_mix_norm=m_mix_norm, m_ffn2_norm=m_ffn2_norm, m_ffn2_w_gu=m_ffn2_w_gu, m_ffn2_w_down=m_ffn2_w_down, m_w_out=m_w_out, m_mem_norm=m_mem_norm, m_w_mem_kv=m_w_mem_kv, m_a_w_in=m_a_w_in, m_a_conv=m_a_conv, m_a_A_log=m_a_A_log, m_a_dt_bias=m_a_dt_bias, m_a_out_norm=m_a_out_norm, m_b_w_in=m_b_w_in, m_b_q_norm=m_b_q_norm, m_b_w_uq=m_b_w_uq, m_kv_in_norm=m_kv_in_norm, m_w_dkv=m_w_dkv, m_kv_lat_norm=m_kv_lat_norm, m_w_ukv=m_w_ukv, m_final_norm=m_final_norm, v_ffn1_norm=v_ffn1_norm, v_ffn1_w_gu=v_ffn1_w_gu, v_ffn1_w_down=v_ffn1_w_down, v_mix_norm=v_mix_norm, v_ffn2_norm=v_ffn2_norm, v_ffn2_w_gu=v_ffn2_w_gu, v_ffn2_w_down=v_ffn2_w_down, v_w_out=v_w_out, v_mem_norm=v_mem_norm, v_w_mem_kv=v_w_mem_kv, v_a_w_in=v_a_w_in, v_a_conv=v_a_conv, v_a_A_log=v_a_A_log, v_a_dt_bias=v_a_dt_bias, v_a_out_norm=v_a_out_norm, v_b_w_in=v_b_w_in, v_b_q_norm=v_b_q_norm, v_b_w_uq=v_b_w_uq, v_kv_in_norm=v_kv_in_norm, v_w_dkv=v_w_dkv, v_kv_lat_norm=v_kv_lat_norm, v_w_ukv=v_w_ukv, v_final_norm=v_final_norm)
    weights = {n: given[n] for n in TWIN_WEIGHTS}
    shared = {n: given[n] for n in SHARED_INPUTS}
    per_example = {n: given[n] for n in ['x', 'mem', 'positions']}
    grad_fn = _jax.value_and_grad(_loss, argnums=(0, 1))

    def one_microbatch(ex, loss_target):
        ex = dict(ex)
        diff = ex.pop(TWIN_DIFF_INPUT)
        return grad_fn(weights, diff, {**shared, **ex}, loss_target)

    if N_MICROBATCH == 1:
        loss, (grad_w, grad_x) = one_microbatch(per_example, given["loss_target"])
    else:
        def body(carry, xs):
            loss_sum, grad_sum = carry
            l_k, (gw_k, gx_k) = one_microbatch(xs[0], xs[1])
            with _jax.named_scope("update"):
                return (loss_sum + l_k, _jax.tree.map(_jnp.add, grad_sum, gw_k)), gx_k

        init = (_jnp.zeros((), _jnp.float32), _jax.tree.map(_jnp.zeros_like, weights))
        (loss, grad_w), grad_x = _jax.lax.scan(body, init, (per_example, given["loss_target"]))
    with _jax.named_scope("update"):
        delta_w, new_m, new_v = {}, {}, {}
        for n in TWIN_WEIGHTS:
            delta_w[n], new_m[n], new_v[n] = _adamw(weights[n], grad_w[n], given["m_" + n], given["v_" + n])
    return (loss, grad_x, *[grad_w[n] for n in TWIN_WEIGHTS], *[delta_w[n] for n in TWIN_WEIGHTS],
            *[new_m[n] for n in TWIN_WEIGHTS], *[new_v[n] for n in TWIN_WEIGHTS])
```

```python
import functools

import numpy as np
import jax
import jax.numpy as jnp
from jax import lax
from jax.experimental import pallas as pl
from jax.experimental.pallas import tpu as pltpu

F32 = jnp.float32
BF16 = jnp.bfloat16

N_DEV = 8
D = 1024
D_FF = 2816
FF_SHARD = 2 * D_FF // N_DEV
DEPTH = 4
N_A = 2
N_B = 2
EPS = 1e-6
CHUNK = 64
GROUP = 256
A_HEADS = 6
HD = 128
A_WIDTH = A_HEADS * HD
B_HEADS = 6
QK_NOPE = 128
QK_ROPE = 64
QK_CAT = 256
Q_LORA = 256
KV_LORA = 256
MEM_HEADS = 4
MEM_HD = 64
MEM_W = 256
N_MEM = 256
ROPE_THETA = 10000.0
A_IN = 4 * A_WIDTH + 2 * A_HEADS + MEM_W
A_MQ_BLK = 4 * A_WIDTH // MEM_W
A_BA_BLK = (4 * A_WIDTH + MEM_W) // 128

ADAM_LR = 0.001
ADAM_B1 = 0.9
ADAM_B2 = 0.999
ADAM_EPS = 1e-08
ADAM_WD = 0.01
ADAM_STEP = 10

VMEM_LIMIT = 56 * 1024 * 1024
TM = 512
TMM = 1024
ATT_BQ = 512
ATT_BK = 512
PACK_W = 1024
PACK_ROWS = 32


def _cparams(sem):
    return pltpu.CompilerParams(dimension_semantics=sem, vmem_limit_bytes=VMEM_LIMIT)


_DIMS = {"nn": ((1,), (0,)), "nt": ((1,), (1,)), "tn": ((0,), (0,))}


def _dot(a, b, dims="nn"):
    return lax.dot_general(a.astype(BF16), b.astype(BF16), (_DIMS[dims], ((), ())),
                           preferred_element_type=F32)


def _matmul(a, b, *, dims, grid, a_spec, b_spec, o_spec, out_shape, name, scale=1.0,
            res=None, res_spec=None):
    nk = grid[-1]
    kax = len(grid) - 1
    acc_shape = tuple(s for s in o_spec.block_shape if s is not None)

    def body(*refs):
        if res is None:
            a_ref, b_ref, o_ref, acc = refs
            r_ref = None
        else:
            a_ref, b_ref, r_ref, o_ref, acc = refs
        k = pl.program_id(kax)

        @pl.when(k == 0)
        def _():
            acc[...] = jnp.zeros_like(acc)

        acc[...] += _dot(a_ref[...], b_ref[...], dims)

        @pl.when(k == nk - 1)
        def _():
            y = acc[...] * scale
            if r_ref is not None:
                y = y + r_ref[...].astype(F32)
            o_ref[...] = y.astype(o_ref.dtype)

    args = [a, b] + ([res] if res is not None else [])
    specs = [a_spec, b_spec] + ([res_spec] if res is not None else [])
    sem = ("parallel",) * kax + ("arbitrary",)
    return pl.pallas_call(
        body, out_shape=out_shape, grid=grid, in_specs=specs, out_specs=o_spec,
        scratch_shapes=[pltpu.VMEM(acc_shape, F32)], name=name, compiler_params=_cparams(sem))(*args)


def _tile(n, cap):
    if n <= cap:
        return n
    t = cap - cap % 128
    while t >= 128:
        if n % t == 0:
            return t
        t -= 128
    raise ValueError(f"no tile for {n}")


def _mm(a, b, dims, out_dtype, name, scale=1.0, res=None):
    if dims == "tn":
        kk, m = a.shape
        n = b.shape[1]
        tk, tn = _tile(kk, TMM), _tile(n, 1152)
        return _matmul(a, b, dims=dims, grid=(1, n // tn, kk // tk),
                       a_spec=pl.BlockSpec((tk, m), lambda i, j, k: (k, 0)),
                       b_spec=pl.BlockSpec((tk, tn), lambda i, j, k: (k, j)),
                       o_spec=pl.BlockSpec((m, tn), lambda i, j, k: (0, j)),
                       out_shape=jax.ShapeDtypeStruct((m, n), out_dtype), name=name, scale=scale)
    m, kk = a.shape
    n = b.shape[1] if dims == "nn" else b.shape[0]
    tm, tn, tk = _tile(m, TMM), _tile(n, 1152), _tile(kk, 1536)
    if dims == "nn":
        b_spec = pl.BlockSpec((tk, tn), lambda i, j, k: (k, j))
    else:
        b_spec = pl.BlockSpec((tn, tk), lambda i, j, k: (j, k))
    o_spec = pl.BlockSpec((tm, tn), lambda i, j, k: (i, j))
    return _matmul(a, b, dims=dims, grid=(m // tm, n // tn, kk // tk),
                   a_spec=pl.BlockSpec((tm, tk), lambda i, j, k: (i, k)), b_spec=b_spec, o_spec=o_spec,
                   out_shape=jax.ShapeDtypeStruct((m, n), out_dtype), name=name, scale=scale,
                   res=res, res_spec=o_spec if res is not None else None)


def _rowcall(fn, args, in_specs, out_shapes, out_specs, grid, name, n_acc=0):
    n_in, n_out = len(args), len(out_shapes)

    def body(*refs):
        outs = fn(*[r[...] for r in refs[:n_in]])
        if not isinstance(outs, (tuple, list)):
            outs = (outs,)
        first = pl.program_id(0) == 0
        for ax in range(1, len(grid)):
            first = jnp.logical_and(first, pl.program_id(ax) == 0)
        for idx, (o_ref, val) in enumerate(zip(refs[n_in:], outs)):
            if idx >= n_out - n_acc:
                @pl.when(first)
                def _(o_ref=o_ref):
                    o_ref[...] = jnp.zeros_like(o_ref)

                o_ref[...] += val.astype(o_ref.dtype)
            else:
                o_ref[...] = val.astype(o_ref.dtype)

    sem = (("arbitrary",) if n_acc else ("parallel",)) * len(grid)
    res = pl.pallas_call(body, out_shape=tuple(out_shapes), grid=grid, in_specs=list(in_specs),
                         out_specs=tuple(out_specs), name=name, compiler_params=_cparams(sem))(*args)
    return res


def _vjp_fn(fn, n_in, wrt):
    def bwd(*blocks):
        ins = [b.astype(F32) for b in blocks[:n_in]]
        cts = [c.astype(F32) for c in blocks[n_in:]]
        outs, vjp = jax.vjp(fn, *ins)
        if isinstance(outs, (tuple, list)):
            grads = vjp(tuple(cts))
        else:
            grads = vjp(cts[0])
        return tuple(grads[i] for i in wrt)
    return bwd


def _sds(shape, dtype):
    return jax.ShapeDtypeStruct(tuple(shape), dtype)


def _rows(tm, w, col=0):
    return pl.BlockSpec((tm, w), lambda i, *_: (i, col))


def _shared(shape):
    nd = len(shape)
    return pl.BlockSpec(tuple(shape), lambda *_: (0,) * nd)


def _rms(x, g):
    return x * lax.rsqrt(jnp.mean(x * x, axis=-1, keepdims=True) + EPS) * g


def _silu(x):
    return x * jax.nn.sigmoid(x)


def _swiglu(g, u):
    return _silu(g) * u


def _gdn_prep(q, k, v):
    q, k, v = _silu(q), _silu(k), _silu(v)
    q = q * lax.rsqrt(jnp.sum(q * q, axis=-1, keepdims=True) + EPS) * (HD ** -0.5)
    k = k * lax.rsqrt(jnp.sum(k * k, axis=-1, keepdims=True) + EPS)
    return q, k, v


def _gates(ba, a_log, dt_bias):
    lane = lax.broadcasted_iota(jnp.int32, ba.shape, 1)
    beta = jax.nn.sigmoid(ba)
    z = ba + dt_bias
    softplus = jnp.maximum(z, 0.0) + jnp.log(1.0 + jnp.exp(-jnp.abs(z)))
    g = -jnp.exp(a_log) * softplus
    return jnp.where(lane < A_HEADS, beta, jnp.where(lane < 2 * A_HEADS, g, 0.0))


def _outnorm_gate(o, gate, gain):
    return _rms(o, gain) * _silu(gate)


def _memattn(q, k, v):
    lane = lax.shift_right_logical(lax.broadcasted_iota(jnp.int32, (1, MEM_W), 1), 6)
    out = jnp.zeros(q.shape, F32)
    for h in range(MEM_HEADS):
        mh = (lane == h).astype(F32)
        s = _dot(q * mh, k, "nt") * (MEM_HD ** -0.5)
        s = s - lax.stop_gradient(jnp.max(s, axis=-1, keepdims=True))
        p = jnp.exp(s)
        p = p / jnp.sum(p, axis=-1, keepdims=True)
        out = out + _dot(p, v * mh)
    return out


def _rope_mix(a, a_sw, c, s):
    return a * c + a_sw * s


def _kcat(kn, kr, kr_sw, c, s):
    return kn + kr * c + kr_sw * s


def _gdn_local(q, k, v, beta, gcol, grow):
    n = GROUP
    ri = lax.broadcasted_iota(jnp.int32, (n, n), 0)
    ci = lax.broadcasted_iota(jnp.int32, (n, n), 1)
    same = lax.shift_right_logical(ri, 6) == lax.shift_right_logical(ci, 6)
    lower = jnp.logical_and(same, ci <= ri)
    strict = jnp.logical_and(same, ci < ri)
    gc_col = jnp.sum(lower.astype(F32) * grow, axis=1, keepdims=True)
    gc_row = jnp.sum(jnp.logical_and(same, ri <= ci).astype(F32) * gcol, axis=0, keepdims=True)
    glast = jnp.sum(same.astype(F32) * grow, axis=1, keepdims=True)
    decay = jnp.where(lower, jnp.exp(jnp.where(lower, gc_col - gc_row, 0.0)), 0.0)
    kb = k * beta
    nmat = -jnp.where(strict, _dot(kb, k, "nt") * decay, 0.0)
    pinv = (ri == ci).astype(F32) + nmat
    npow = nmat
    for _ in range(5):
        npow = _dot(npow, npow)
        pinv = pinv + _dot(pinv, npow)
    e_gc = jnp.exp(gc_col)
    u = _dot(pinv, v * beta)
    w = _dot(pinv, kb * e_gc)
    qk = _dot(q, k, "nt") * decay
    fold = (jnp.bitwise_and(lax.broadcasted_iota(jnp.int32, (n, CHUNK), 0), CHUNK - 1)
            == lax.broadcasted_iota(jnp.int32, (n, CHUNK), 1)).astype(F32)
    qk_c = _dot(qk, fold)
    q_dec = q * e_gc
    k_dec = k * jnp.exp(glast - gc_col)
    dmat = jnp.exp(glast) * jnp.ones((1, HD), F32)
    return u, w, q_dec, k_dec, qk_c, dmat


def _gdn_step(s, w_c, u_c, qd_c, kd_c, qk_c, d_c):
    v_new = u_c - _dot(w_c, s)
    out = _dot(qd_c, s) + _dot(qk_c, v_new)
    d_row = jnp.mean(d_c, axis=0, keepdims=True)
    s_new = s * d_row + _dot(kd_c, v_new, "tn")
    return s_new, out


def _rmsnorm_fwd(x, gain, name, col=0, width=None):
    t = x.shape[0]
    w = width or x.shape[1]
    (n,) = _rowcall(_rms, [x, gain.reshape(1, w)], [_rows(TM, w, col), _shared((1, w))],
                    [_sds((t, w), BF16)], [_rows(TM, w)], (t // TM,), name)
    return n


def _rmsnorm_bwd(x, gain, dn, dres, name):
    t, w = x.shape
    fn = _vjp_fn(_rms, 2, (0, 1))

    def bwd(xb, gb, dnb, drb):
        dx, dg = fn(xb, gb, dnb)
        return dx + drb, dg

    dx, dg = _rowcall(bwd, [x, gain.reshape(1, w), dn, dres],
                      [_rows(TM, w), _shared((1, w)), _rows(TM, w), _rows(TM, w)],
                      [_sds((t, w), F32), _sds((1, w), F32)], [_rows(TM, w), _shared((1, w))],
                      (t // TM,), name, n_acc=1)
    return dx, dg[0]


def _ffn_fwd(x, gain, wgu8, wd4, tag):
    t = x.shape[0]
    nt = t // TMM
    n = _rmsnorm_fwd(x, gain, tag + "_norm")
    gu = _matmul(n, wgu8, dims="nn", grid=(N_DEV, nt, 1),
                 a_spec=pl.BlockSpec((TMM, D), lambda j, i, k: (i, 0)),
                 b_spec=pl.BlockSpec((None, D, FF_SHARD), lambda j, i, k: (j, 0, 0)),
                 o_spec=pl.BlockSpec((None, TMM, FF_SHARD), lambda j, i, k: (j, i, 0)),
                 out_shape=_sds((N_DEV, t, FF_SHARD), BF16), name=tag + "_gu")
    gu = gu.reshape(2, 4, t, FF_SHARD)
    gu_spec = pl.BlockSpec((2, None, TM, FF_SHARD), lambda i, j: (0, j, i, 0))
    h_spec = pl.BlockSpec((None, TM, FF_SHARD), lambda i, j: (j, i, 0))
    (h,) = _rowcall(lambda b: _swiglu(b[0].astype(F32), b[1].astype(F32)), [gu], [gu_spec], [_sds((4, t, FF_SHARD), BF16)], [h_spec],
                    (t // TM, 4), tag + "_act")
    y = _matmul(h, wd4, dims="nn", grid=(nt, 1, 4),
                a_spec=pl.BlockSpec((None, TMM, FF_SHARD), lambda i, j, k: (k, i, 0)),
                b_spec=pl.BlockSpec((None, FF_SHARD, D), lambda i, j, k: (k, 0, 0)),
                o_spec=pl.BlockSpec((TMM, D), lambda i, j, k: (i, 0)),
                out_shape=_sds((t, D), F32), name=tag + "_down", scale=0.5,
                res=x, res_spec=pl.BlockSpec((TMM, D), lambda i, j, k: (i, 0)))
    return y, (x, n, gu, h)


def _ffn_bwd(d, saved, gain, wgu8, wd4, tag):
    x, n, gu, h = saved
    t = x.shape[0]
    nt = t // TMM
    dh = _matmul(d, wd4, dims="nt", grid=(4, nt, 1),
                 a_spec=pl.BlockSpec((TMM, D), lambda j, i, k: (i, 0)),
                 b_spec=pl.BlockSpec((None, FF_SHARD, D), lambda j, i, k: (j, 0, 0)),
                 o_spec=pl.BlockSpec((None, TMM, FF_SHARD), lambda j, i, k: (j, i, 0)),
                 out_shape=_sds((4, t, FF_SHARD), BF16), name=tag + "_dh", scale=0.5)
    dwd4 = _matmul(h, d, dims="tn", grid=(4, 1, nt),
                   a_spec=pl.BlockSpec((None, TMM, FF_SHARD), lambda j, i, k: (j, k, 0)),
                   b_spec=pl.BlockSpec((TMM, D), lambda j, i, k: (k, 0)),
                   o_spec=pl.BlockSpec((None, FF_SHARD, D), lambda j, i, k: (j, 0, 0)),
                   out_shape=_sds((4, FF_SHARD, D), F32), name=tag + "_dwd", scale=0.5)
    gu_spec = pl.BlockSpec((2, None, TM, FF_SHARD), lambda i, j: (0, j, i, 0))
    h_spec = pl.BlockSpec((None, TM, FF_SHARD), lambda i, j: (j, i, 0))
    def dact(b, dhb):
        return jnp.stack(_vjp_fn(_swiglu, 2, (0, 1))(b[0], b[1], dhb))

    (dgu,) = _rowcall(dact, [gu, dh], [gu_spec, h_spec],
                      [_sds((2, 4, t, FF_SHARD), BF16)], [gu_spec], (t // TM, 4), tag + "_dact")
    dgu = dgu.reshape(N_DEV, t, FF_SHARD)
    dwgu8 = _matmul(n, dgu, dims="tn", grid=(N_DEV, 1, nt),
                    a_spec=pl.BlockSpec((TMM, D), lambda j, i, k: (k, 0)),
                    b_spec=pl.BlockSpec((None, TMM, FF_SHARD), lambda j, i, k: (j, k, 0)),
                    o_spec=pl.BlockSpec((None, D, FF_SHARD), lambda j, i, k: (j, 0, 0)),
                    out_shape=_sds((N_DEV, D, FF_SHARD), F32), name=tag + "_dwgu")
    dn = _matmul(dgu, wgu8, dims="nt", grid=(nt, 1, N_DEV),
                 a_spec=pl.BlockSpec((None, TMM, FF_SHARD), lambda i, j, k: (k, i, 0)),
                 b_spec=pl.BlockSpec((None, D, FF_SHARD), lambda i, j, k: (k, 0, 0)),
                 o_spec=pl.BlockSpec((TMM, D), lambda i, j, k: (i, 0)),
                 out_shape=_sds((t, D), BF16), name=tag + "_dn")
    dx, dgain = _rmsnorm_bwd(x, gain, dn, d, tag + "_dnorm")
    return dx, dgain, dwgu8, dwd4


CONV_TC = 768
CONV_K = 4


def _conv_fwd(ha, w, name):
    t = ha.shape[0]
    nb = TM // 8

    def body(prev_ref, cur_ref, w_ref, o_ref):
        i = pl.program_id(0)
        cur = cur_ref[...].astype(F32)
        prev = prev_ref[...].astype(F32) * (i > 0).astype(F32)
        ext = jnp.concatenate([prev, cur], axis=0)
        wv = w_ref[...]
        acc = cur * wv[3:4]
        for k in range(1, CONV_K):
            acc = acc + pltpu.roll(ext, k, axis=0)[8:] * wv[3 - k:4 - k]
        o_ref[...] = acc.astype(o_ref.dtype)

    return pl.pallas_call(
        body, out_shape=_sds((3, t, CONV_TC), BF16), grid=(t // TM, 3),
        in_specs=[pl.BlockSpec((8, CONV_TC), lambda i, c: (jnp.maximum(i * nb - 1, 0), c)),
                  pl.BlockSpec((TM, CONV_TC), lambda i, c: (i, c)),
                  pl.BlockSpec((CONV_K, CONV_TC), lambda i, c: (0, c))],
        out_specs=pl.BlockSpec((None, TM, CONV_TC), lambda i, c: (c, i, 0)),
        name=name, compiler_params=_cparams(("parallel", "parallel")))(ha, ha, w)


def _conv_bwd(ha, w, dy3, name):
    t = ha.shape[0]
    nb = TM // 8
    nt = t // TM

    def body(prev_ref, cur_ref, dy_ref, nxt_ref, w_ref, dx_ref, dw_ref):
        i = pl.program_id(1)
        cur = cur_ref[...].astype(F32)
        prev = prev_ref[...].astype(F32) * (i > 0).astype(F32)
        ext = jnp.concatenate([prev, cur], axis=0)
        dy = dy_ref[...].astype(F32)
        nxt = nxt_ref[...].astype(F32) * (i < nt - 1).astype(F32)
        dext = jnp.concatenate([dy, nxt], axis=0)
        wv = w_ref[...]
        dx = dy * wv[3:4]
        dws = [None] * CONV_K
        dws[3] = jnp.sum(dy * cur, axis=0, keepdims=True)
        for k in range(1, CONV_K):
            dx = dx + pltpu.roll(dext, TM + 8 - k, axis=0)[:TM] * wv[3 - k:4 - k]
            dws[3 - k] = jnp.sum(dy * pltpu.roll(ext, k, axis=0)[8:], axis=0, keepdims=True)
        dx_ref[...] = dx.astype(dx_ref.dtype)

        @pl.when(i == 0)
        def _():
            dw_ref[...] = jnp.zeros_like(dw_ref)

        dw_ref[...] += jnp.concatenate(dws, axis=0)

    return pl.pallas_call(
        body, out_shape=(_sds((t, 3 * CONV_TC), BF16), _sds((CONV_K, 3 * CONV_TC), F32)), grid=(3, nt),
        in_specs=[pl.BlockSpec((8, CONV_TC), lambda c, i: (jnp.maximum(i * nb - 1, 0), c)),
                  pl.BlockSpec((TM, CONV_TC), lambda c, i: (i, c)),
                  pl.BlockSpec((None, TM, CONV_TC), lambda c, i: (c, i, 0)),
                  pl.BlockSpec((None, 8, CONV_TC), lambda c, i: (c, jnp.minimum((i + 1) * nb, t // 8 - 1), 0)),
                  pl.BlockSpec((CONV_K, CONV_TC), lambda c, i: (0, c))],
        out_specs=(pl.BlockSpec((TM, CONV_TC), lambda c, i: (i, c)),
                   pl.BlockSpec((CONV_K, CONV_TC), lambda c, i: (0, c))),
        name=name, compiler_params=_cparams(("parallel", "arbitrary")))(ha, ha, dy3, dy3, w)


def _gdn_specs(t, rev):
    ng = t // GROUP

    def gi(g):
        return ng - 1 - g if rev else g

    qkv = pl.BlockSpec((3, GROUP, HD), lambda h, g: (0, gi(g), h))
    col = pl.BlockSpec((None, GROUP, 1), lambda h, g: (h, gi(g), 0))
    row = pl.BlockSpec((None, None, 1, GROUP), lambda h, g: (h, gi(g), 0, 0))
    o = pl.BlockSpec((GROUP, HD), lambda h, g: (gi(g), h))
    st = pl.BlockSpec((None, None, HD, HD), lambda h, g: (h, gi(g), 0, 0))
    return qkv, col, row, o, st


def _gdn_fwd(qkv3, beta_col, g_col, g_row, name):
    t = qkv3.shape[1]
    ng = t // GROUP
    qkv_s, col_s, row_s, o_s, st_s = _gdn_specs(t, False)

    def body(qkv_ref, b_ref, gc_ref, gr_ref, o_ref, st_ref, s_scr):
        @pl.when(pl.program_id(1) == 0)
        def _():
            s_scr[...] = jnp.zeros_like(s_scr)

        s = s_scr[...]
        st_ref[...] = s
        u, w, qd, kd, qkc, dm = _gdn_local(qkv_ref[0].astype(F32), qkv_ref[1].astype(F32),
                                           qkv_ref[2].astype(F32), b_ref[...], gc_ref[...], gr_ref[...])
        for a in range(GROUP // CHUNK):
            sl = slice(a * CHUNK, (a + 1) * CHUNK)
            s, out = _gdn_step(s, w[sl], u[sl], qd[sl], kd[sl], qkc[sl], dm[sl])
            o_ref[sl, :] = out.astype(o_ref.dtype)
        s_scr[...] = s

    return pl.pallas_call(
        body, out_shape=(_sds((t, A_WIDTH), BF16), _sds((A_HEADS, ng, HD, HD), F32)), grid=(A_HEADS, ng),
        in_specs=[qkv_s, col_s, col_s, row_s], out_specs=(o_s, st_s),
        scratch_shapes=[pltpu.VMEM((HD, HD), F32)], name=name,
        compiler_params=_cparams(("parallel", "arbitrary")))(qkv3, beta_col, g_col, g_row)


def _gdn_bwd(qkv3, beta_col, g_col, g_row, states, do, name):
    t = qkv3.shape[1]
    ng = t // GROUP
    qkv_s, col_s, row_s, o_s, st_s = _gdn_specs(t, True)
    nc = GROUP // CHUNK

    def body(qkv_ref, b_ref, gc_ref, gr_ref, st_ref, do_ref, dqkv_ref, db_ref, dgc_ref, dgr_ref, ds_scr):
        @pl.when(pl.program_id(1) == 0)
        def _():
            ds_scr[...] = jnp.zeros_like(ds_scr)

        loc, vjp_local = jax.vjp(_gdn_local, qkv_ref[0].astype(F32), qkv_ref[1].astype(F32),
                                 qkv_ref[2].astype(F32), b_ref[...], gc_ref[...], gr_ref[...])
        u, w, qd, kd, qkc, dm = loc
        s = st_ref[...]
        starts = []
        for a in range(nc):
            sl = slice(a * CHUNK, (a + 1) * CHUNK)
            starts.append(s)
            if a < nc - 1:
                s, _ = _gdn_step(s, w[sl], u[sl], qd[sl], kd[sl], qkc[sl], dm[sl])
        ds = ds_scr[...]
        parts = [None] * nc
        for a in reversed(range(nc)):
            sl = slice(a * CHUNK, (a + 1) * CHUNK)
            _, vjp_step = jax.vjp(_gdn_step, starts[a], w[sl], u[sl], qd[sl], kd[sl], qkc[sl], dm[sl])
            grads = vjp_step((ds, do_ref[sl, :].astype(F32)))
            ds = grads[0]
            parts[a] = grads[1:]
        ds_scr[...] = ds
        dw, du, dqd, dkd, dqk, ddm = [jnp.concatenate([parts[a][j] for a in range(nc)], axis=0)
                                      for j in range(6)]
        dq, dk, dv, db, dgc, dgr = vjp_local((du, dw, dqd, dkd, dqk, ddm))
        dqkv_ref[0] = dq.astype(dqkv_ref.dtype)
        dqkv_ref[1] = dk.astype(dqkv_ref.dtype)
        dqkv_ref[2] = dv.astype(dqkv_ref.dtype)
        db_ref[...] = db
        dgc_ref[...] = dgc
        dgr_ref[...] = dgr

    return pl.pallas_call(
        body,
        out_shape=(_sds((3, t, A_WIDTH), BF16), _sds((A_HEADS, t, 1), F32), _sds((A_HEADS, t, 1), F32),
                   _sds((A_HEADS, ng, 1, GROUP), F32)),
        grid=(A_HEADS, ng), in_specs=[qkv_s, col_s, col_s, row_s, st_s, o_s],
        out_specs=(qkv_s, col_s, col_s, row_s), scratch_shapes=[pltpu.VMEM((HD, HD), F32)], name=name,
        compiler_params=_cparams(("parallel", "arbitrary")))(qkv3, beta_col, g_col, g_row, states, do)


ATT_SCALE = (QK_NOPE + QK_ROPE) ** -0.5
NEG = -1e30


def _att_scores(q, k, qb, kb):
    s = _dot(q, k, "nt") * ATT_SCALE
    r = qb * ATT_BQ + lax.broadcasted_iota(jnp.int32, s.shape, 0)
    c = kb * ATT_BK + lax.broadcasted_iota(jnp.int32, s.shape, 1)
    visible = lax.shift_right_logical(c, 6) <= lax.shift_right_logical(r, 6)
    return jnp.where(visible, s, NEG)


def _att_fwd(qc, kc, v, name):
    t = qc.shape[0]
    nq, nk = t // ATT_BQ, t // ATT_BK

    def body(q_ref, k_ref, v_ref, o_ref, lse_ref, m_scr, l_scr, acc_scr):
        qb, kb = pl.program_id(1), pl.program_id(2)

        @pl.when(kb == 0)
        def _():
            m_scr[...] = jnp.full_like(m_scr, NEG)
            l_scr[...] = jnp.zeros_like(l_scr)
            acc_scr[...] = jnp.zeros_like(acc_scr)

        @pl.when(kb <= qb)
        def _():
            s = _att_scores(q_ref[...], k_ref[...], qb, kb)
            m_old = m_scr[...]
            m_new = jnp.maximum(m_old, jnp.max(s, axis=1, keepdims=True))
            alpha = jnp.exp(m_old - m_new)
            p = jnp.exp(s - m_new)
            l_scr[...] = alpha * l_scr[...] + jnp.sum(p, axis=1, keepdims=True)
            acc_scr[...] = alpha * acc_scr[...] + _dot(p, v_ref[...])
            m_scr[...] = m_new

        @pl.when(kb == nk - 1)
        def _():
            o_ref[...] = (acc_scr[...] / l_scr[...]).astype(o_ref.dtype)
            lse_ref[...] = m_scr[...] + jnp.log(l_scr[...])

    kmap = lambda h, i, j: (jnp.minimum(j, i), h)
    return pl.pallas_call(
        body, out_shape=(_sds((t, B_HEADS * HD), BF16), _sds((B_HEADS, t, 1), F32)), grid=(B_HEADS, nq, nk),
        in_specs=[pl.BlockSpec((ATT_BQ, QK_CAT), lambda h, i, j: (i, h)),
                  pl.BlockSpec((ATT_BK, QK_CAT), kmap), pl.BlockSpec((ATT_BK, HD), kmap)],
        out_specs=(pl.BlockSpec((ATT_BQ, HD), lambda h, i, j: (i, h)),
                   pl.BlockSpec((None, ATT_BQ, 1), lambda h, i, j: (h, i, 0))),
        scratch_shapes=[pltpu.VMEM((ATT_BQ, 1), F32), pltpu.VMEM((ATT_BQ, 1), F32), pltpu.VMEM((ATT_BQ, HD), F32)],
        name=name, compiler_params=_cparams(("parallel", "parallel", "arbitrary")))(qc, kc, v)


def _att_bwd(qc, kc, v, o, lse, do, name):
    t = qc.shape[0]
    nq, nk = t // ATT_BQ, t // ATT_BK

    def delta_fn(ob, dob):
        return jnp.sum(ob * dob, axis=1, keepdims=True)

    (delta,) = _rowcall(delta_fn, [o, do],
                        [pl.BlockSpec((TM, HD), lambda i, h: (i, h))] * 2, [_sds((B_HEADS, t, 1), F32)],
                        [pl.BlockSpec((None, TM, 1), lambda i, h: (h, i, 0))], (t // TM, B_HEADS), name + "_delta")

    def p_ds(q, k, vv, dob, lse_b, dl_b, qb, kb):
        s = _att_scores(q, k, qb, kb)
        p = jnp.exp(s - lse_b)
        dp = _dot(dob, vv, "nt")
        return p, p * (dp - dl_b) * ATT_SCALE

    def dq_body(q_ref, k_ref, v_ref, do_ref, lse_ref, dl_ref, dq_ref, acc):
        qb, kb = pl.program_id(1), pl.program_id(2)

        @pl.when(kb == 0)
        def _():
            acc[...] = jnp.zeros_like(acc)

        @pl.when(kb <= qb)
        def _():
            _, ds = p_ds(q_ref[...], k_ref[...], v_ref[...], do_ref[...], lse_ref[...], dl_ref[...], qb, kb)
            acc[...] += _dot(ds, k_ref[...])

        @pl.when(kb == nk - 1)
        def _():
            dq_ref[...] = acc[...].astype(dq_ref.dtype)

    kmap = lambda h, i, j: (jnp.minimum(j, i), h)
    qmap = lambda h, i, j: (i, h)
    colq = pl.BlockSpec((None, ATT_BQ, 1), lambda h, i, j: (h, i, 0))
    dq = pl.pallas_call(
        dq_body, out_shape=_sds((t, B_HEADS * QK_CAT), BF16), grid=(B_HEADS, nq, nk),
        in_specs=[pl.BlockSpec((ATT_BQ, QK_CAT), qmap), pl.BlockSpec((ATT_BK, QK_CAT), kmap),
                  pl.BlockSpec((ATT_BK, HD), kmap), pl.BlockSpec((ATT_BQ, HD), qmap), colq, colq],
        out_specs=pl.BlockSpec((ATT_BQ, QK_CAT), qmap),
        scratch_shapes=[pltpu.VMEM((ATT_BQ, QK_CAT), F32)], name=name + "_dq",
        compiler_params=_cparams(("parallel", "parallel", "arbitrary")))(qc, kc, v, do, lse, delta)

    def dkv_body(q_ref, k_ref, v_ref, do_ref, lse_ref, dl_ref, dk_ref, dv_ref, dk_acc, dv_acc):
        kb, qb = pl.program_id(1), pl.program_id(2)

        @pl.when(qb == 0)
        def _():
            dk_acc[...] = jnp.zeros_like(dk_acc)
            dv_acc[...] = jnp.zeros_like(dv_acc)

        @pl.when(qb >= kb)
        def _():
            p, ds = p_ds(q_ref[...], k_ref[...], v_ref[...], do_ref[...], lse_ref[...], dl_ref[...], qb, kb)
            dv_acc[...] += _dot(p, do_ref[...], "tn")
            dk_acc[...] += _dot(ds, q_ref[...], "tn")

        @pl.when(qb == nq - 1)
        def _():
            dk_ref[...] = dk_acc[...].astype(dk_ref.dtype)
            dv_ref[...] = dv_acc[...].astype(dv_ref.dtype)

    qmap2 = lambda h, j, i: (jnp.maximum(i, j), h)
    kmap2 = lambda h, j, i: (j, h)
    colq2 = pl.BlockSpec((None, ATT_BQ, 1), lambda h, j, i: (h, jnp.maximum(i, j), 0))
    dk, dv = pl.pallas_call(
        dkv_body, out_shape=(_sds((t, B_HEADS * QK_CAT), BF16), _sds((t, B_HEADS * HD), BF16)),
        grid=(B_HEADS, nk, nq),
        in_specs=[pl.BlockSpec((ATT_BQ, QK_CAT), qmap2), pl.BlockSpec((ATT_BK, QK_CAT), kmap2),
                  pl.BlockSpec((ATT_BK, HD), kmap2), pl.BlockSpec((ATT_BQ, HD), qmap2), colq2, colq2],
        out_specs=(pl.BlockSpec((ATT_BK, QK_CAT), kmap2), pl.BlockSpec((ATT_BK, HD), kmap2)),
        scratch_shapes=[pltpu.VMEM((ATT_BK, QK_CAT), F32), pltpu.VMEM((ATT_BK, HD), F32)], name=name + "_dkv",
        compiler_params=_cparams(("parallel", "parallel", "arbitrary")))(qc, kc, v, do, lse, delta)
    return dq, dk, dv


def _mem_fwd(hx, col, mkv, name):
    t = hx.shape[0]
    (o,) = _rowcall(_memattn, [hx, mkv, mkv],
                    [_rows(TM, MEM_W, col), pl.BlockSpec((N_MEM, MEM_W), lambda i: (0, 0)),
                     pl.BlockSpec((N_MEM, MEM_W), lambda i: (0, 1))],
                    [_sds((t, MEM_W), BF16)], [_rows(TM, MEM_W)], (t // TM,), name)
    return o


def _mem_bwd(hx, col, mkv, do, do_col, name):
    t = hx.shape[0]
    dq, dk, dv = _rowcall(_vjp_fn(_memattn, 3, (0, 1, 2)), [hx, mkv, mkv, do],
                          [_rows(TM, MEM_W, col), pl.BlockSpec((N_MEM, MEM_W), lambda i: (0, 0)),
                           pl.BlockSpec((N_MEM, MEM_W), lambda i: (0, 1)), _rows(TM, MEM_W, do_col)],
                          [_sds((t, MEM_W), BF16), _sds((N_MEM, MEM_W), F32), _sds((N_MEM, MEM_W), F32)],
                          [_rows(TM, MEM_W), _shared((N_MEM, MEM_W)), _shared((N_MEM, MEM_W))],
                          (t // TM,), name, n_acc=2)
    return dq, jnp.concatenate([dk, dv], axis=1)


def _a_in_ext(w):
    nb = 4 * A_WIDTH
    ba = jnp.pad(w[:, nb:nb + 2 * A_HEADS], ((0, 0), (0, 128 - 2 * A_HEADS)))
    return jnp.concatenate([w[:, :nb], w[:, nb + 2 * A_HEADS:], ba], axis=1)


def _swap_halves(w):
    return jnp.concatenate([w[..., QK_ROPE // 2:], w[..., :QK_ROPE // 2]], axis=-1)


def _uq_ext(w):
    w = w.reshape(Q_LORA, B_HEADS, QK_NOPE + QK_ROPE)
    nope, rope = w[..., :QK_NOPE], w[..., QK_NOPE:]
    z64 = jnp.zeros((Q_LORA, B_HEADS, QK_CAT - QK_NOPE - QK_ROPE), w.dtype)
    z128 = jnp.zeros((Q_LORA, B_HEADS, QK_NOPE), w.dtype)
    a = jnp.concatenate([nope, rope, z64], axis=-1).reshape(Q_LORA, B_HEADS * QK_CAT)
    b = jnp.concatenate([z128, _swap_halves(rope), z64], axis=-1).reshape(Q_LORA, B_HEADS * QK_CAT)
    return jnp.concatenate([a, b], axis=1)


def _dkv_ext(w):
    ckv, kr = w[:, :KV_LORA], w[:, KV_LORA:]
    z128 = jnp.zeros((D, QK_NOPE), w.dtype)
    z64 = jnp.zeros((D, QK_CAT - QK_NOPE - QK_ROPE), w.dtype)
    return jnp.concatenate([ckv, z128, kr, z64, z128, _swap_halves(kr), z64], axis=1)


def _ukv_ext(w):
    w = w.reshape(KV_LORA, B_HEADS, QK_NOPE + HD)
    kn, vv = w[..., :QK_NOPE], w[..., QK_NOPE:]
    z = jnp.zeros((KV_LORA, B_HEADS, QK_CAT - QK_NOPE), w.dtype)
    a = jnp.concatenate([kn, z], axis=-1).reshape(KV_LORA, B_HEADS * QK_CAT)
    return jnp.concatenate([a, vv.reshape(KV_LORA, B_HEADS * HD)], axis=1)


def _ext_and_back(fn, w):
    ext, back = jax.vjp(fn, w.astype(F32))
    return ext.astype(BF16), lambda g: back(g.astype(F32))[0]


def _rope_tables(pos_col):
    t = pos_col.shape[0]
    inv = (ROPE_THETA ** (-np.arange(0, QK_ROPE, 2, dtype=np.float32) / QK_ROPE)).astype(np.float32)
    inv_row = np.zeros((1, QK_CAT), np.float32)
    inv_row[0, QK_NOPE:QK_NOPE + QK_ROPE] = np.concatenate([inv, inv])
    sign = np.zeros((1, QK_CAT), np.float32)
    sign[0, QK_NOPE:QK_NOPE + QK_ROPE // 2] = -1.0
    sign[0, QK_NOPE + QK_ROPE // 2:QK_NOPE + QK_ROPE] = 1.0
    is_rope = np.abs(sign)
    is_nope = np.zeros((1, QK_CAT), np.float32)
    is_nope[0, :QK_NOPE] = 1.0

    def fn(p, inv_b, sign_b, rope_b, nope_b):
        ang = p.astype(F32) * inv_b
        return jnp.cos(ang) * rope_b + nope_b, jnp.sin(ang) * sign_b

    consts = [jnp.asarray(a) for a in (inv_row, sign, is_rope, is_nope)]
    return _rowcall(fn, [pos_col] + consts, [_rows(TM, 1)] + [_shared((1, QK_CAT))] * 4,
                    [_sds((t, QK_CAT), F32)] * 2, [_rows(TM, QK_CAT)] * 2, (t // TM,), "rope_tables")


def _gate_layouts(bg):
    t = bg.shape[0]
    beta_col = bg[:, :A_HEADS].T.reshape(A_HEADS, t, 1)
    g_t = bg[:, A_HEADS:2 * A_HEADS].T
    return beta_col, g_t.reshape(A_HEADS, t, 1), g_t.reshape(A_HEADS, t // GROUP, 1, GROUP)


def _local_step(x, mem, pos, target, w):
    t = x.shape[0]
    g = {}
    head6 = (t // TM, A_HEADS)

    mem_n = _rmsnorm_fwd_small(mem, w["mem_norm"])
    rope_c, rope_s = _rope_tables(pos.reshape(t, 1))
    mkv = [_mm(mem_n, w["w_mem_kv"][l], "nn", BF16, f"mkv{l}") for l in range(DEPTH)]

    saved = []
    for l in range(DEPTH):
        sv = {}
        x, sv["ffn1"] = _ffn_fwd(x, w["ffn1_norm"][l], w["ffn1_w_gu"][l], w["ffn1_w_down"][l], "ffn1")
        sv["x1"] = x
        n2 = _rmsnorm_fwd(x, w["mix_norm"][l], "mix_norm")
        sv["n2"] = n2
        if l < N_A:
            ha = _mm(n2, w["a_w_in"][l], "nn", BF16, "a_in")
            yc3 = _conv_fwd(ha, w["a_conv"][l], "a_conv")
            blk3 = pl.BlockSpec((3, TM, HD), lambda i, h: (0, i, h))
            (qkv3,) = _rowcall(lambda b: jnp.stack(_gdn_prep(b[0].astype(F32), b[1].astype(F32), b[2].astype(F32))),
                               [yc3], [blk3], [_sds((3, t, A_WIDTH), BF16)], [blk3], head6, "a_prep")
            (bg,) = _rowcall(_gates, [ha, _pad128(w["a_A_log"][l], A_HEADS), _pad128(w["a_dt_bias"][l], A_HEADS)],
                             [_rows(TM, 128, A_BA_BLK), _shared((1, 128)), _shared((1, 128))],
                             [_sds((t, 128), F32)], [_rows(TM, 128)], (t // TM,), "a_gates")
            beta_col, g_col, g_row = _gate_layouts(bg)
            o_gdn, states = _gdn_fwd(qkv3, beta_col, g_col, g_row, "a_gdn")
            (o_a,) = _rowcall(_outnorm_gate, [o_gdn, ha, w["a_out_norm"][l].reshape(1, HD)],
                              [pl.BlockSpec((TM, HD), lambda i, h: (i, h)),
                               pl.BlockSpec((TM, HD), lambda i, h: (i, 3 * A_HEADS + h)), _shared((1, HD))],
                              [_sds((t, A_WIDTH), BF16)], [pl.BlockSpec((TM, HD), lambda i, h: (i, h))],
                              head6, "a_outnorm")
            o_m = _mem_fwd(ha, A_MQ_BLK, mkv[l], "mem_attn_a")
            sv.update(ha=ha, yc3=yc3, qkv3=qkv3, gates=(beta_col, g_col, g_row), states=states, o_gdn=o_gdn)
            cat = jnp.concatenate([o_a, o_m], axis=1)
        else:
            j = l - N_A
            hb = _mm(n2, w["b_w_in"][j], "nn", BF16, "b_in")
            cqn = _rmsnorm_fwd(hb, w["b_q_norm"][j], "b_qnorm", 0, Q_LORA)
            qq = _mm(cqn, w["b_w_uq"][j], "nn", BF16, "b_uq")
            (qc,) = _rowcall(_rope_mix, [qq, qq, rope_c, rope_s],
                             [pl.BlockSpec((TM, QK_CAT), lambda i, h: (i, h)),
                              pl.BlockSpec((TM, QK_CAT), lambda i, h: (i, B_HEADS + h)),
                              pl.BlockSpec((TM, QK_CAT), lambda i, h: (i, 0)),
                              pl.BlockSpec((TM, QK_CAT), lambda i, h: (i, 0))],
                             [_sds((t, B_HEADS * QK_CAT), BF16)], [pl.BlockSpec((TM, QK_CAT), lambda i, h: (i, h))],
                             head6, "b_qrope")
            o_b, lse = _att_fwd(qc, kcat, vmla, "b_attn")
            o_m = _mem_fwd(hb, 1, mkv[l], "mem_attn_b")
            sv.update(hb=hb, cqn=cqn, qc=qc, o_b=o_b, lse=lse)
            cat = jnp.concatenate([o_b, o_m], axis=1)
        sv["cat"] = cat
        x = _mm(cat, w["w_out"][l], "nn", F32, "w_out", res=x)
        x, sv["ffn2"] = _ffn_fwd(x, w["ffn2_norm"][l], w["ffn2_w_gu"][l], w["ffn2_w_down"][l], "ffn2")
        saved.append(sv)
        if l == N_A - 1:
            x_kv = x
            nkv = _rmsnorm_fwd(x, w["kv_in_norm"], "kv_in_norm")
            ckr = _mm(nkv, w["w_dkv"], "nn", BF16, "kv_down")
            ckv_n = _rmsnorm_fwd(ckr, w["kv_lat_norm"], "kv_lat_norm", 0, KV_LORA)
            kvu = _mm(ckv_n, w["w_ukv"], "nn", BF16, "kv_up")
            vmla = kvu[:, B_HEADS * QK_CAT:]
            (kcat,) = _rowcall(_kcat, [kvu, ckr, ckr, rope_c, rope_s],
                               [pl.BlockSpec((TM, QK_CAT), lambda i, h: (i, h)),
                                pl.BlockSpec((TM, QK_CAT), lambda i, h: (i, 1)),
                                pl.BlockSpec((TM, QK_CAT), lambda i, h: (i, 2)),
                                pl.BlockSpec((TM, QK_CAT), lambda i, h: (i, 0)),
                                pl.BlockSpec((TM, QK_CAT), lambda i, h: (i, 0))],
                               [_sds((t, B_HEADS * QK_CAT), BF16)],
                               [pl.BlockSpec((TM, QK_CAT), lambda i, h: (i, h))], head6, "kv_cat")

    def loss_fn(xb, gb, tb):
        def f(xx, gg):
            e = _rms(xx, gg) - tb
            return 0.5 * jnp.sum(jnp.mean(e * e, axis=-1, keepdims=True), axis=0, keepdims=True)
        val, vjp = jax.vjp(f, xb, gb)
        dx, dg = vjp(jnp.ones((1, 1), F32))
        return dx, dg, val * jnp.ones((1, 128), F32)

    d, dfin, loss = _rowcall(loss_fn, [x, w["final_norm"].reshape(1, D), target],
                             [_rows(TM, D), _shared((1, D)), _rows(TM, D)],
                             [_sds((t, D), F32), _sds((1, D), F32), _sds((1, 128), F32)],
                             [_rows(TM, D), _shared((1, D)), _shared((1, 128))], (t // TM,), "loss_head", n_acc=2)
    g["final_norm"] = dfin[0]
    loss = loss[0, 0]

    for name in ("ffn1_norm", "ffn1_w_gu", "ffn1_w_down", "mix_norm", "ffn2_norm", "ffn2_w_gu", "ffn2_w_down",
                 "w_out", "w_mem_kv"):
        g[name] = [None] * DEPTH
    for name in ("a_w_in", "a_conv", "a_A_log", "a_dt_bias", "a_out_norm"):
        g[name] = [None] * N_A
    for name in ("b_w_in", "b_q_norm", "b_w_uq"):
        g[name] = [None] * N_B
    dmkv = [None] * DEPTH
    dkcat = []
    dvmla = []

    for l in reversed(range(DEPTH)):
        sv = saved[l]
        if l == N_A - 1:
            kq = pl.BlockSpec((TM, QK_CAT), lambda i, h: (i, h))
            tab = pl.BlockSpec((TM, QK_CAT), lambda i, h: (i, 0))

            def dk_fn(c, s, d0, d1):
                dk = d0.astype(F32) + d1.astype(F32)
                return dk, dk * c, dk * s

            dkn, dkr_h, dkrs_h = _rowcall(dk_fn, [rope_c, rope_s, dkcat[0], dkcat[1]], [tab, tab, kq, kq],
                                          [_sds((t, B_HEADS * QK_CAT), BF16)] + [_sds((B_HEADS, t, QK_CAT), BF16)] * 2,
                                          [kq] + [pl.BlockSpec((None, TM, QK_CAT), lambda i, h: (h, i, 0))] * 2,
                                          head6, "kv_dcat")

            def sum6(a, b):
                return jnp.sum(a.astype(F32), axis=0), jnp.sum(b.astype(F32), axis=0)

            h6 = pl.BlockSpec((B_HEADS, TM, QK_CAT), lambda i: (0, i, 0))
            dkr, dkrs = _rowcall(sum6, [dkr_h, dkrs_h], [h6, h6], [_sds((t, QK_CAT), BF16)] * 2,
                                 [_rows(TM, QK_CAT)] * 2, (t // TM,), "kv_dkr")

            def addv(a, b):
                return a.astype(F32) + b.astype(F32)

            (dv,) = _rowcall(addv, dvmla, [_rows(TM, B_HEADS * HD)] * 2, [_sds((t, B_HEADS * HD), BF16)],
                             [_rows(TM, B_HEADS * HD)], (t // TM,), "kv_dv")
            dkvu = jnp.concatenate([dkn, dv], axis=1)
            g["w_ukv"] = _mm(ckv_n, dkvu, "tn", F32, "kv_up_dw")
            dckv_n = _mm(dkvu, w["w_ukv"], "nt", BF16, "kv_up_dx")

            def lat_bwd(cb, gb, dnb):
                return _vjp_fn(_rms, 2, (0, 1))(cb, gb, dnb)

            dckv, g["kv_lat_norm"] = _rowcall(lat_bwd, [ckr, w["kv_lat_norm"].reshape(1, KV_LORA), dckv_n],
                                              [_rows(TM, KV_LORA), _shared((1, KV_LORA)), _rows(TM, KV_LORA)],
                                              [_sds((t, KV_LORA), BF16), _sds((1, KV_LORA), F32)],
                                              [_rows(TM, KV_LORA), _shared((1, KV_LORA))], (t // TM,),
                                              "kv_lat_dnorm", n_acc=1)
            g["kv_lat_norm"] = g["kv_lat_norm"][0]
            dckr = jnp.concatenate([dckv, dkr, dkrs], axis=1)
            g["w_dkv"] = _mm(nkv, dckr, "tn", F32, "kv_down_dw")
            dnkv = _mm(dckr, w["w_dkv"], "nt", BF16, "kv_down_dx")
            d, g["kv_in_norm"] = _rmsnorm_bwd(x_kv, w["kv_in_norm"], dnkv, d, "kv_in_dnorm")

        d, g["ffn2_norm"][l], g["ffn2_w_gu"][l], g["ffn2_w_down"][l] = _ffn_bwd(
            d, sv["ffn2"], w["ffn2_norm"][l], w["ffn2_w_gu"][l], w["ffn2_w_down"][l], "ffn2b")
        g["w_out"][l] = _mm(sv["cat"], d, "tn", F32, "w_out_dw")
        dcat = _mm(d, w["w_out"][l], "nt", BF16, "w_out_dx")
        if l < N_A:
            ha, yc3, qkv3, states, o_gdn = sv["ha"], sv["yc3"], sv["qkv3"], sv["states"], sv["o_gdn"]
            beta_col, g_col, g_row = sv["gates"]
            dmq, dmkv[l] = _mem_bwd(ha, A_MQ_BLK, mkv[l], dcat, 3, "mem_attn_a_bwd")
            hblk = pl.BlockSpec((TM, HD), lambda i, h: (i, h))
            do_gdn, dgate, dgain = _rowcall(
                _vjp_fn(_outnorm_gate, 3, (0, 1, 2)), [o_gdn, ha, w["a_out_norm"][l].reshape(1, HD), dcat],
                [hblk, pl.BlockSpec((TM, HD), lambda i, h: (i, 3 * A_HEADS + h)), _shared((1, HD)), hblk],
                [_sds((t, A_WIDTH), BF16), _sds((t, A_WIDTH), BF16), _sds((1, HD), F32)],
                [hblk, hblk, _shared((1, HD))], head6, "a_outnorm_bwd", n_acc=1)
            g["a_out_norm"][l] = dgain[0]
            dqkv3, dbeta_col, dg_col, dg_row = _gdn_bwd(qkv3, beta_col, g_col, g_row, states, do_gdn, "a_gdn_bwd")
            dbg = jnp.concatenate([dbeta_col.reshape(A_HEADS, t).T,
                                   (dg_col.reshape(A_HEADS, t) + dg_row.reshape(A_HEADS, t)).T,
                                   jnp.zeros((t, 128 - 2 * A_HEADS), F32)], axis=1)
            dba, dalog, ddt = _rowcall(
                _vjp_fn(_gates, 3, (0, 1, 2)),
                [ha, _pad128(w["a_A_log"][l], A_HEADS), _pad128(w["a_dt_bias"][l], A_HEADS), dbg],
                [_rows(TM, 128, A_BA_BLK), _shared((1, 128)), _shared((1, 128)), _rows(TM, 128)],
                [_sds((t, 128), BF16), _sds((1, 128), F32), _sds((1, 128), F32)],
                [_rows(TM, 128), _shared((1, 128)), _shared((1, 128))], (t // TM,), "a_gates_bwd", n_acc=2)
            g["a_A_log"][l] = dalog[0, A_HEADS:2 * A_HEADS]
            g["a_dt_bias"][l] = ddt[0, A_HEADS:2 * A_HEADS]
            blk3 = pl.BlockSpec((3, TM, HD), lambda i, h: (0, i, h))
            def dprep(b, db):
                return jnp.stack(_vjp_fn(_gdn_prep, 3, (0, 1, 2))(b[0], b[1], b[2], db[0], db[1], db[2]))

            (dyc3,) = _rowcall(dprep, [yc3, dqkv3], [blk3, blk3],
                               [_sds((3, t, A_WIDTH), BF16)], [blk3], head6, "a_prep_bwd")
            dqkv_in, g["a_conv"][l] = _conv_bwd(ha, w["a_conv"][l], dyc3, "a_conv_bwd")
            dha = jnp.concatenate([dqkv_in, dgate, dmq, dba], axis=1)
            g["a_w_in"][l] = _mm(sv["n2"], dha, "tn", F32, "a_in_dw")
            dn2 = _mm(dha, w["a_w_in"][l], "nt", BF16, "a_in_dx")
        else:
            j = l - N_A
            hb, cqn, qc, o_b, lse = sv["hb"], sv["cqn"], sv["qc"], sv["o_b"], sv["lse"]
            dmq, dmkv[l] = _mem_bwd(hb, 1, mkv[l], dcat, 3, "mem_attn_b_bwd")
            dqc, dkc, dvv = _att_bwd(qc, kcat, vmla, o_b, lse, dcat, "b_attn_bwd")
            dkcat.append(dkc)
            dvmla.append(dvv)
            kq = pl.BlockSpec((TM, QK_CAT), lambda i, h: (i, h))
            tab = pl.BlockSpec((TM, QK_CAT), lambda i, h: (i, 0))

            def dq_fn(c, s, dq):
                dq = dq.astype(F32)
                return jnp.stack([dq * c, dq * s])

            (dqq,) = _rowcall(dq_fn, [rope_c, rope_s, dqc], [tab, tab, kq],
                              [_sds((2, t, B_HEADS * QK_CAT), BF16)],
                              [pl.BlockSpec((2, TM, QK_CAT), lambda i, h: (0, i, h))], head6, "b_qrope_bwd")
            dqq = jnp.concatenate([dqq[0], dqq[1]], axis=1)
            g["b_w_uq"][j] = _mm(cqn, dqq, "tn", F32, "b_uq_dw")
            dcqn = _mm(dqq, w["b_w_uq"][j], "nt", BF16, "b_uq_dx")
            dcq, dqg = _rowcall(_vjp_fn(_rms, 2, (0, 1)), [hb, w["b_q_norm"][j].reshape(1, Q_LORA), dcqn],
                                [_rows(TM, Q_LORA), _shared((1, Q_LORA)), _rows(TM, Q_LORA)],
                                [_sds((t, Q_LORA), BF16), _sds((1, Q_LORA), F32)],
                                [_rows(TM, Q_LORA), _shared((1, Q_LORA))], (t // TM,), "b_qnorm_bwd", n_acc=1)
            g["b_q_norm"][j] = dqg[0]
            dhb = jnp.concatenate([dcq, dmq], axis=1)
            g["b_w_in"][j] = _mm(sv["n2"], dhb, "tn", F32, "b_in_dw")
            dn2 = _mm(dhb, w["b_w_in"][j], "nt", BF16, "b_in_dx")
        d, g["mix_norm"][l] = _rmsnorm_bwd(sv["x1"], w["mix_norm"][l], dn2, d, "mix_dnorm")
        d, g["ffn1_norm"][l], g["ffn1_w_gu"][l], g["ffn1_w_down"][l] = _ffn_bwd(
            d, sv["ffn1"], w["ffn1_norm"][l], w["ffn1_w_gu"][l], w["ffn1_w_down"][l], "ffn1b")

    dmem_n = None
    for l in range(DEPTH):
        g["w_mem_kv"][l] = _mm(mem_n, dmkv[l], "tn", F32, f"mkv_dw{l}")
        dmem_n = _mm(dmkv[l], w["w_mem_kv"][l], "nt", F32, f"mkv_dx{l}", res=dmem_n)
    (_, gmn) = _rowcall(_vjp_fn(_rms, 2, (0, 1)), [mem, w["mem_norm"].reshape(1, D), dmem_n],
                        [_shared((N_MEM, D)), _shared((1, D)), _shared((N_MEM, D))],
                        [_sds((N_MEM, D), F32), _sds((1, D), F32)], [_shared((N_MEM, D)), _shared((1, D))],
                        (1,), "mem_dnorm")
    g["mem_norm"] = gmn[0]
    return loss, d, g


def _pad128(v, offset):
    return jnp.zeros((1, 128), F32).at[0, offset:offset + v.shape[0]].set(v.astype(F32))


def _rmsnorm_fwd_small(x, gain):
    r, w = x.shape
    (n,) = _rowcall(_rms, [x, gain.reshape(1, w)], [_shared((r, w)), _shared((1, w))],
                    [_sds((r, w), BF16)], [_shared((r, w))], (1,), "mem_norm")
    return n


def _exchange(src, gather, name):
    blk = src.shape if gather else src.shape[1:]

    def body(src_ref, out_ref, send_sems, recv_sems, local_sem):
        x, y, c = lax.axis_index("x"), lax.axis_index("y"), lax.axis_index("c")
        me = 4 * x + 2 * y + c
        copies = []
        for k in range(1, N_DEV):
            px = (x + (k >> 2 & 1)) % 2
            py = (y + (k >> 1 & 1)) % 2
            pc = (c + (k & 1)) % 2
            peer = 4 * px + 2 * py + pc
            cp = pltpu.make_async_remote_copy(
                src_ref=src_ref if gather else src_ref.at[peer], dst_ref=out_ref.at[me],
                send_sem=send_sems.at[k - 1], recv_sem=recv_sems.at[k - 1],
                device_id=(px, py, pc), device_id_type=pl.DeviceIdType.MESH)
            cp.start()
            copies.append(cp)
        mine = pltpu.make_async_copy(src_ref if gather else src_ref.at[me], out_ref.at[me], local_sem)
        mine.start()
        for cp in copies:
            cp.wait()
        mine.wait()

    return pl.pallas_call(
        body, out_shape=_sds((N_DEV,) + tuple(blk), src.dtype),
        in_specs=[pl.BlockSpec(memory_space=pl.ANY)], out_specs=pl.BlockSpec(memory_space=pl.ANY),
        scratch_shapes=[pltpu.SemaphoreType.DMA((N_DEV - 1,)), pltpu.SemaphoreType.DMA((N_DEV - 1,)),
                        pltpu.SemaphoreType.DMA],
        name=name)(src)


def _reduce_adamw(parts, wp, mp, vp, name):
    r = wp.shape[0]
    tr = _tile_rows(r)
    c1 = 1.0 - ADAM_B1 ** ADAM_STEP
    c2 = 1.0 - ADAM_B2 ** ADAM_STEP

    def fn(pb, wb, mb, vb):
        gsum = pb[0].astype(F32)
        for j in range(1, N_DEV):
            gsum = gsum + pb[j].astype(F32)
        m_new = ADAM_B1 * mb + (1.0 - ADAM_B1) * gsum
        v_new = ADAM_B2 * vb + (1.0 - ADAM_B2) * (gsum * gsum)
        delta = -ADAM_LR * ((m_new / c1) / (jnp.sqrt(v_new / c2) + ADAM_EPS) + ADAM_WD * wb)
        return gsum, delta, m_new, v_new

    row = _rows(tr, PACK_W)
    return _rowcall(fn, [parts, wp, mp, vp],
                    [pl.BlockSpec((N_DEV, tr, PACK_W), lambda i: (0, i, 0)), row, row, row],
                    [_sds((r, PACK_W), F32)] * 4, [row] * 4, (r // tr,), name)


def _tile_rows(r):
    for t in (512, 256, 128, 64, 32):
        if r % t == 0:
            return t
    raise ValueError(r)


def _pack(arrs, dtype, lead=None):
    if lead is None:
        flat = jnp.concatenate([a.reshape(-1).astype(dtype) for a in arrs])
        n = flat.shape[0]
        unit = PACK_W * PACK_ROWS
        tot = -(-n // unit) * unit
        return jnp.pad(flat, (0, tot - n)).reshape(tot // PACK_W, PACK_W)
    flat = jnp.concatenate([a.reshape(lead, -1).astype(dtype) for a in arrs], axis=1)
    n = flat.shape[1]
    unit = PACK_W * PACK_ROWS
    tot = -(-n // unit) * unit
    return jnp.pad(flat, ((0, 0), (0, tot - n))).reshape(lead, tot // PACK_W, PACK_W)


def _unpack(buf, shapes, lead=None):
    out, off = [], 0
    flat = buf.reshape(-1) if lead is None else buf.reshape(lead, -1)
    for s in shapes:
        n = int(np.prod(s))
        if lead is None:
            out.append(flat[off:off + n].reshape(s))
        else:
            out.append(flat[:, off:off + n].reshape((lead,) + tuple(s)))
        off += n
    return out


_SHARDED = ["ffn1_w_gu", "ffn1_w_down", "ffn2_w_gu", "ffn2_w_down", "w_out", "w_mem_kv", "a_w_in", "a_conv",
            "b_w_in", "b_w_uq", "w_dkv", "w_ukv"]
_COL_SHARDED = {"ffn1_w_gu", "ffn2_w_gu", "a_conv", "b_w_uq", "w_ukv"}
_LAYERED = {"ffn1_w_gu": DEPTH, "ffn1_w_down": DEPTH, "ffn2_w_gu": DEPTH, "ffn2_w_down": DEPTH, "w_out": DEPTH,
            "w_mem_kv": DEPTH, "a_w_in": N_A, "a_conv": N_A, "b_w_in": N_B, "b_w_uq": N_B}
_REPLICATED = ["ffn1_norm", "mix_norm", "ffn2_norm", "mem_norm", "a_A_log", "a_dt_bias", "a_out_norm", "b_q_norm",
               "kv_in_norm", "kv_lat_norm", "final_norm"]
_WEIGHTS = ["ffn1_norm", "ffn1_w_gu", "ffn1_w_down", "mix_norm", "ffn2_norm", "ffn2_w_gu", "ffn2_w_down", "w_out",
            "mem_norm", "w_mem_kv", "a_w_in", "a_conv", "a_A_log", "a_dt_bias", "a_out_norm", "b_w_in", "b_q_norm",
            "b_w_uq", "kv_in_norm", "w_dkv", "kv_lat_norm", "w_ukv", "final_norm"]


def _full_from_shards(name, sh):
    if name in ("ffn1_w_gu", "ffn2_w_gu"):
        return sh
    if name in ("ffn1_w_down", "ffn2_w_down"):
        return sh.reshape(4, FF_SHARD, D)
    if name in _COL_SHARDED:
        return jnp.moveaxis(sh, 0, -2).reshape(sh.shape[1:-1] + (N_DEV * sh.shape[-1],))
    return sh.reshape((N_DEV * sh.shape[1],) + sh.shape[2:])


def _shards_from_full(name, full):
    if name in ("ffn1_w_gu", "ffn2_w_gu"):
        return full
    if name in ("ffn1_w_down", "ffn2_w_down"):
        return full.reshape(N_DEV, D_FF // N_DEV, D)
    if name in _COL_SHARDED:
        r, cc = full.shape
        return jnp.moveaxis(full.reshape(r, N_DEV, cc // N_DEV), 1, 0)
    return full.reshape((N_DEV, full.shape[0] // N_DEV) + full.shape[1:])


def kernel(x, mem, positions, ffn1_norm, ffn1_w_gu, ffn1_w_down, mix_norm, ffn2_norm, ffn2_w_gu, ffn2_w_down, w_out, mem_norm, w_mem_kv, a_w_in, a_conv, a_A_log, a_dt_bias, a_out_norm, b_w_in, b_q_norm, b_w_uq, kv_in_norm, w_dkv, kv_lat_norm, w_ukv, final_norm, loss_target, m_ffn1_norm, m_ffn1_w_gu, m_ffn1_w_down, m_mix_norm, m_ffn2_norm, m_ffn2_w_gu, m_ffn2_w_down, m_w_out, m_mem_norm, m_w_mem_kv, m_a_w_in, m_a_conv, m_a_A_log, m_a_dt_bias, m_a_out_norm, m_b_w_in, m_b_q_norm, m_b_w_uq, m_kv_in_norm, m_w_dkv, m_kv_lat_norm, m_w_ukv, m_final_norm, v_ffn1_norm, v_ffn1_w_gu, v_ffn1_w_down, v_mix_norm, v_ffn2_norm, v_ffn2_w_gu, v_ffn2_w_down, v_w_out, v_mem_norm, v_w_mem_kv, v_a_w_in, v_a_conv, v_a_A_log, v_a_dt_bias, v_a_out_norm, v_b_w_in, v_b_q_norm, v_b_w_uq, v_kv_in_norm, v_w_dkv, v_kv_lat_norm, v_w_ukv, v_final_norm):
    loc = dict(locals())
    wl = {n: loc[n] for n in _WEIGHTS}
    ml = {n: loc["m_" + n] for n in _WEIGHTS}
    vl = {n: loc["v_" + n] for n in _WEIGHTS}

    shard_shapes = [wl[n].shape for n in _SHARDED]
    gathered = _exchange(_pack([wl[n] for n in _SHARDED], BF16), True, "gather_weights")
    pieces = _unpack(gathered, shard_shapes, lead=N_DEV)
    w = {}
    back = {}
    for n, p in zip(_SHARDED, pieces):
        if n in _LAYERED:
            w[n] = [_full_from_shards(n, p[:, l]) for l in range(_LAYERED[n])]
        else:
            w[n] = _full_from_shards(n, p)
    for n in _REPLICATED:
        w[n] = wl[n]
    for l in range(N_A):
        w["a_w_in"][l], back[("a_w_in", l)] = _ext_and_back(_a_in_ext, w["a_w_in"][l])
        w["a_conv"][l] = w["a_conv"][l].astype(F32)
    for j in range(N_B):
        w["b_w_uq"][j], back[("b_w_uq", j)] = _ext_and_back(_uq_ext, w["b_w_uq"][j])
    w["w_dkv"], back["w_dkv"] = _ext_and_back(_dkv_ext, w["w_dkv"])
    w["w_ukv"], back["w_ukv"] = _ext_and_back(_ukv_ext, w["w_ukv"])

    loss, dx, g = _local_step(x[0], mem[0], positions[0], loss_target[0], w)
    loss = lax.psum(loss, ("x", "y", "c"))

    for l in range(N_A):
        g["a_w_in"][l] = back[("a_w_in", l)](g["a_w_in"][l])
    for j in range(N_B):
        g["b_w_uq"][j] = back[("b_w_uq", j)](g["b_w_uq"][j])
    g["w_dkv"] = back["w_dkv"](g["w_dkv"])
    g["w_ukv"] = back["w_ukv"](g["w_ukv"])

    gsh = []
    for n in _SHARDED:
        if n in _LAYERED:
            gsh.append(jnp.stack([_shards_from_full(n, g[n][l]) for l in range(_LAYERED[n])], axis=1))
        else:
            gsh.append(_shards_from_full(n, g[n]))
    parts = _exchange(_pack(gsh, BF16, lead=N_DEV), False, "scatter_grads")
    res = _reduce_adamw(parts, _pack([wl[n] for n in _SHARDED], F32), _pack([ml[n] for n in _SHARDED], F32),
                        _pack([vl[n] for n in _SHARDED], F32), "adamw_sharded")
    out = {}
    for kind, buf in zip(("grad", "delta", "new_m", "new_v"), res):
        for n, a in zip(_SHARDED, _unpack(buf, shard_shapes)):
            out[(kind, n)] = a

    rep_shapes = [wl[n].shape for n in _REPLICATED]
    grep = [jnp.stack(g[n]) if isinstance(g[n], list) else g[n] for n in _REPLICATED]
    rparts = _exchange(_pack(grep, F32), True, "gather_small_grads")
    res = _reduce_adamw(rparts, _pack([wl[n] for n in _REPLICATED], F32), _pack([ml[n] for n in _REPLICATED], F32),
                        _pack([vl[n] for n in _REPLICATED], F32), "adamw_replicated")
    for kind, buf in zip(("grad", "delta", "new_m", "new_v"), res):
        for n, a in zip(_REPLICATED, _unpack(buf, rep_shapes)):
            out[(kind, n)] = a

    return (loss, dx[None], *[out[("grad", n)] for n in _WEIGHTS], *[out[("delta", n)] for n in _WEIGHTS],
            *[out[("new_m", n)] for n in _WEIGHTS], *[out[("new_v", n)] for n in _WEIGHTS])
```

```python
import functools

import numpy as np
import jax
import jax.numpy as jnp
from jax import lax
from jax.experimental import pallas as pl
from jax.experimental.pallas import tpu as pltpu

F32 = jnp.float32
BF16 = jnp.bfloat16

N_DEV = 8
D = 1024
D_FF = 2816
FF_SHARD = 2 * D_FF // N_DEV
DEPTH = 4
N_A = 2
N_B = 2
EPS = 1e-6
CHUNK = 64
GROUP = 256
A_HEADS = 6
HD = 128
A_WIDTH = A_HEADS * HD
B_HEADS = 6
QK_NOPE = 128
QK_ROPE = 64
QK_CAT = 256
Q_LORA = 256
KV_LORA = 256
MEM_HEADS = 4
MEM_HD = 64
MEM_W = 256
N_MEM = 256
ROPE_THETA = 10000.0
A_IN = 4 * A_WIDTH + 2 * A_HEADS + MEM_W
A_MQ_BLK = 4 * A_WIDTH // MEM_W
A_BA_BLK = (4 * A_WIDTH + MEM_W) // 128

ADAM_LR = 0.001
ADAM_B1 = 0.9
ADAM_B2 = 0.999
ADAM_EPS = 1e-08
ADAM_WD = 0.01
ADAM_STEP = 10

VMEM_LIMIT = 56 * 1024 * 1024
TM = 512
TMM = 1024
ATT_BQ = 512
PACK_W = 1024
PACK_ROWS = 32


def _cparams(sem):
    return pltpu.CompilerParams(dimension_semantics=sem, vmem_limit_bytes=VMEM_LIMIT)


_DIMS = {"nn": ((1,), (0,)), "nt": ((1,), (1,)), "tn": ((0,), (0,))}


def _dot(a, b, dims="nn"):
    return lax.dot_general(a.astype(BF16), b.astype(BF16), (_DIMS[dims], ((), ())),
                           preferred_element_type=F32)


def _matmul(a, b, *, dims, grid, a_spec, b_spec, o_spec, out_shape, name, scale=1.0,
            res=None, res_spec=None):
    nk = grid[-1]
    kax = len(grid) - 1
    acc_shape = tuple(s for s in o_spec.block_shape if s is not None)

    def body(*refs):
        if res is None:
            a_ref, b_ref, o_ref, acc = refs
            r_ref = None
        else:
            a_ref, b_ref, r_ref, o_ref, acc = refs
        k = pl.program_id(kax)

        @pl.when(k == 0)
        def _():
            acc[...] = jnp.zeros_like(acc)

        acc[...] += _dot(a_ref[...], b_ref[...], dims)

        @pl.when(k == nk - 1)
        def _():
            y = acc[...] * scale
            if r_ref is not None:
                y = y + r_ref[...].astype(F32)
            o_ref[...] = y.astype(o_ref.dtype)

    args = [a, b] + ([res] if res is not None else [])
    specs = [a_spec, b_spec] + ([res_spec] if res is not None else [])
    sem = ("parallel",) * kax + ("arbitrary",)
    return pl.pallas_call(
        body, out_shape=out_shape, grid=grid, in_specs=specs, out_specs=o_spec,
        scratch_shapes=[pltpu.VMEM(acc_shape, F32)], name=name, compiler_params=_cparams(sem))(*args)


def _tile(n, cap):
    if n <= cap:
        return n
    t = cap - cap % 128
    while t >= 128:
        if n % t == 0:
            return t
        t -= 128
    raise ValueError(f"no tile for {n}")


def _mm(a, b, dims, out_dtype, name, scale=1.0, res=None):
    if dims == "tn":
        kk, m = a.shape
        n = b.shape[1]
        tk, tn = _tile(kk, TMM), _tile(n, 1152)
        return _matmul(a, b, dims=dims, grid=(1, n // tn, kk // tk),
                       a_spec=pl.BlockSpec((tk, m), lambda i, j, k: (k, 0)),
                       b_spec=pl.BlockSpec((tk, tn), lambda i, j, k: (k, j)),
                       o_spec=pl.BlockSpec((m, tn), lambda i, j, k: (0, j)),
                       out_shape=jax.ShapeDtypeStruct((m, n), out_dtype), name=name, scale=scale)
    m, kk = a.shape
    n = b.shape[1] if dims == "nn" else b.shape[0]
    tm, tn, tk = _tile(m, TMM), _tile(n, 1152), _tile(kk, 1536)
    if dims == "nn":
        b_spec = pl.BlockSpec((tk, tn), lambda i, j, k: (k, j))
    else:
        b_spec = pl.BlockSpec((tn, tk), lambda i, j, k: (j, k))
    o_spec = pl.BlockSpec((tm, tn), lambda i, j, k: (i, j))
    return _matmul(a, b, dims=dims, grid=(m // tm, n // tn, kk // tk),
                   a_spec=pl.BlockSpec((tm, tk), lambda i, j, k: (i, k)), b_spec=b_spec, o_spec=o_spec,
                   out_shape=jax.ShapeDtypeStruct((m, n), out_dtype), name=name, scale=scale,
                   res=res, res_spec=o_spec if res is not None else None)


def _rowcall(fn, args, in_specs, out_shapes, out_specs, grid, name, n_acc=0):
    n_in, n_out = len(args), len(out_shapes)

    def body(*refs):
        outs = fn(*[r[...] for r in refs[:n_in]])
        if not isinstance(outs, (tuple, list)):
            outs = (outs,)
        first = pl.program_id(0) == 0
        for ax in range(1, len(grid)):
            first = jnp.logical_and(first, pl.program_id(ax) == 0)
        for idx, (o_ref, val) in enumerate(zip(refs[n_in:], outs)):
            if idx >= n_out - n_acc:
                @pl.when(first)
                def _(o_ref=o_ref):
                    o_ref[...] = jnp.zeros_like(o_ref)

                o_ref[...] += val.astype(o_ref.dtype)
            else:
                o_ref[...] = val.astype(o_ref.dtype)

    sem = (("arbitrary",) if n_acc else ("parallel",)) * len(grid)
    res = pl.pallas_call(body, out_shape=tuple(out_shapes), grid=grid, in_specs=list(in_specs),
                         out_specs=tuple(out_specs), name=name, compiler_params=_cparams(sem))(*args)
    return res


def _vjp_fn(fn, n_in, wrt):
    def bwd(*blocks):
        ins = [b.astype(F32) for b in blocks[:n_in]]
        cts = [c.astype(F32) for c in blocks[n_in:]]
        outs, vjp = jax.vjp(fn, *ins)
        if isinstance(outs, (tuple, list)):
            grads = vjp(tuple(cts))
        else:
            grads = vjp(cts[0])
        return tuple(grads[i] for i in wrt)
    return bwd


def _sds(shape, dtype):
    return jax.ShapeDtypeStruct(tuple(shape), dtype)


def _rows(tm, w, col=0):
    return pl.BlockSpec((tm, w), lambda i, *_: (i, col))


def _shared(shape):
    nd = len(shape)
    return pl.BlockSpec(tuple(shape), lambda *_: (0,) * nd)


def _rms(x, g):
    return x * lax.rsqrt(jnp.mean(x * x, axis=-1, keepdims=True) + EPS) * g


def _silu(x):
    return x * jax.nn.sigmoid(x)


def _swiglu(g, u):
    return _silu(g) * u


def _gdn_prep(q, k, v):
    q, k, v = _silu(q), _silu(k), _silu(v)
    q = q * lax.rsqrt(jnp.sum(q * q, axis=-1, keepdims=True) + EPS) * (HD ** -0.5)
    k = k * lax.rsqrt(jnp.sum(k * k, axis=-1, keepdims=True) + EPS)
    return q, k, v


def _gates(ba, a_log, dt_bias):
    lane = lax.broadcasted_iota(jnp.int32, ba.shape, 1)
    beta = jax.nn.sigmoid(ba)
    z = ba + dt_bias
    softplus = jnp.maximum(z, 0.0) + jnp.log(1.0 + jnp.exp(-jnp.abs(z)))
    g = -jnp.exp(a_log) * softplus
    return jnp.where(lane < A_HEADS, beta, jnp.where(lane < 2 * A_HEADS, g, 0.0))


def _outnorm_gate(o, gate, gain):
    return _rms(o, gain) * _silu(gate)


def _memattn(q, k, v):
    lane = lax.shift_right_logical(lax.broadcasted_iota(jnp.int32, (1, MEM_W), 1), 6)
    out = jnp.zeros(q.shape, F32)
    for h in range(MEM_HEADS):
        mh = (lane == h).astype(F32)
        s = _dot(q * mh, k, "nt") * (MEM_HD ** -0.5)
        s = s - lax.stop_gradient(jnp.max(s, axis=-1, keepdims=True))
        p = jnp.exp(s)
        p = p / jnp.sum(p, axis=-1, keepdims=True)
        out = out + _dot(p, v * mh)
    return out


def _rope_mix(a, a_sw, c, s):
    return a * c + a_sw * s


def _kcat(kn, kr, kr_sw, c, s):
    return kn + kr * c + kr_sw * s


def _gdn_local(q, k, v, beta, gcol, grow):
    n = GROUP
    ri = lax.broadcasted_iota(jnp.int32, (n, n), 0)
    ci = lax.broadcasted_iota(jnp.int32, (n, n), 1)
    same = lax.shift_right_logical(ri, 6) == lax.shift_right_logical(ci, 6)
    lower = jnp.logical_and(same, ci <= ri)
    strict = jnp.logical_and(same, ci < ri)
    gc_col = jnp.sum(lower.astype(F32) * grow, axis=1, keepdims=True)
    gc_row = jnp.sum(jnp.logical_and(same, ri <= ci).astype(F32) * gcol, axis=0, keepdims=True)
    glast = jnp.sum(same.astype(F32) * grow, axis=1, keepdims=True)
    decay = jnp.where(lower, jnp.exp(jnp.where(lower, gc_col - gc_row, 0.0)), 0.0)
    kb = k * beta
    nmat = -jnp.where(strict, _dot(kb, k, "nt") * decay, 0.0)
    pinv = (ri == ci).astype(F32) + nmat
    npow = nmat
    for _ in range(5):
        npow = _dot(npow, npow)
        pinv = pinv + _dot(pinv, npow)
    e_gc = jnp.exp(gc_col)
    u = _dot(pinv, v * beta)
    w = _dot(pinv, kb * e_gc)
    qk = _dot(q, k, "nt") * decay
    fold = (jnp.bitwise_and(lax.broadcasted_iota(jnp.int32, (n, CHUNK), 0), CHUNK - 1)
            == lax.broadcasted_iota(jnp.int32, (n, CHUNK), 1)).astype(F32)
    qk_c = _dot(qk, fold)
    q_dec = q * e_gc
    k_dec = k * jnp.exp(glast - gc_col)
    dmat = jnp.exp(glast) * jnp.ones((1, HD), F32)
    return u, w, q_dec, k_dec, qk_c, dmat


def _gdn_step(s, w_c, u_c, qd_c, kd_c, qk_c, d_c):
    v_new = u_c - _dot(w_c, s)
    out = _dot(qd_c, s) + _dot(qk_c, v_new)
    d_row = jnp.mean(d_c, axis=0, keepdims=True)
    s_new = s * d_row + _dot(kd_c, v_new, "tn")
    return s_new, out


def _rmsnorm_fwd(x, gain, name, col=0, width=None):
    t = x.shape[0]
    w = width or x.shape[1]
    (n,) = _rowcall(_rms, [x, gain.reshape(1, w)], [_rows(TM, w, col), _shared((1, w))],
                    [_sds((t, w), BF16)], [_rows(TM, w)], (t // TM,), name)
    return n


def _rmsnorm_bwd(x, gain, dn, dres, name):
    t, w = x.shape
    fn = _vjp_fn(_rms, 2, (0, 1))

    def bwd(xb, gb, dnb, drb):
        dx, dg = fn(xb, gb, dnb)
        return dx + drb, dg

    dx, dg = _rowcall(bwd, [x, gain.reshape(1, w), dn, dres],
                      [_rows(TM, w), _shared((1, w)), _rows(TM, w), _rows(TM, w)],
                      [_sds((t, w), F32), _sds((1, w), F32)], [_rows(TM, w), _shared((1, w))],
                      (t // TM,), name, n_acc=1)
    return dx, dg[0]


def _ffn_fwd(x, gain, wgu8, wd4, tag):
    t = x.shape[0]
    nt = t // TMM
    n = _rmsnorm_fwd(x, gain, tag + "_norm")
    gu = _matmul(n, wgu8, dims="nn", grid=(N_DEV, nt, 1),
                 a_spec=pl.BlockSpec((TMM, D), lambda j, i, k: (i, 0)),
                 b_spec=pl.BlockSpec((None, D, FF_SHARD), lambda j, i, k: (j, 0, 0)),
                 o_spec=pl.BlockSpec((None, TMM, FF_SHARD), lambda j, i, k: (j, i, 0)),
                 out_shape=_sds((N_DEV, t, FF_SHARD), BF16), name=tag + "_gu")
    gu = gu.reshape(2, 4, t, FF_SHARD)
    gu_spec = pl.BlockSpec((2, None, TM, FF_SHARD), lambda i, j: (0, j, i, 0))
    h_spec = pl.BlockSpec((None, TM, FF_SHARD), lambda i, j: (j, i, 0))
    (h,) = _rowcall(lambda b: _swiglu(b[0].astype(F32), b[1].astype(F32)), [gu], [gu_spec], [_sds((4, t, FF_SHARD), BF16)], [h_spec],
                    (t // TM, 4), tag + "_act")
    y = _matmul(h, wd4, dims="nn", grid=(nt, 1, 4),
                a_spec=pl.BlockSpec((None, TMM, FF_SHARD), lambda i, j, k: (k, i, 0)),
                b_spec=pl.BlockSpec((None, FF_SHARD, D), lambda i, j, k: (k, 0, 0)),
                o_spec=pl.BlockSpec((TMM, D), lambda i, j, k: (i, 0)),
                out_shape=_sds((t, D), F32), name=tag + "_down", scale=0.5,
                res=x, res_spec=pl.BlockSpec((TMM, D), lambda i, j, k: (i, 0)))
    return y, (x, n, gu, h)


def _ffn_bwd(d, saved, gain, wgu8, wd4, tag):
    x, n, gu, h = saved
    t = x.shape[0]
    nt = t // TMM
    dh = _matmul(d, wd4, dims="nt", grid=(4, nt, 1),
                 a_spec=pl.BlockSpec((TMM, D), lambda j, i, k: (i, 0)),
                 b_spec=pl.BlockSpec((None, FF_SHARD, D), lambda j, i, k: (j, 0, 0)),
                 o_spec=pl.BlockSpec((None, TMM, FF_SHARD), lambda j, i, k: (j, i, 0)),
                 out_shape=_sds((4, t, FF_SHARD), BF16), name=tag + "_dh", scale=0.5)
    dwd4 = _matmul(h, d, dims="tn", grid=(4, 1, nt),
                   a_spec=pl.BlockSpec((None, TMM, FF_SHARD), lambda j, i, k: (j, k, 0)),
                   b_spec=pl.BlockSpec((TMM, D), lambda j, i, k: (k, 0)),
                   o_spec=pl.BlockSpec((None, FF_SHARD, D), lambda j, i, k: (j, 0, 0)),
                   out_shape=_sds((4, FF_SHARD, D), F32), name=tag + "_dwd", scale=0.5)
    gu_spec = pl.BlockSpec((2, None, TM, FF_SHARD), lambda i, j: (0, j, i, 0))
    h_spec = pl.BlockSpec((None, TM, FF_SHARD), lambda i, j: (j, i, 0))
    def dact(b, dhb):
        return jnp.stack(_vjp_fn(_swiglu, 2, (0, 1))(b[0], b[1], dhb))

    (dgu,) = _rowcall(dact, [gu, dh], [gu_spec, h_spec],
                      [_sds((2, 4, t, FF_SHARD), BF16)], [gu_spec], (t // TM, 4), tag + "_dact")
    dgu = dgu.reshape(N_DEV, t, FF_SHARD)
    dwgu8 = _matmul(n, dgu, dims="tn", grid=(N_DEV, 1, nt),
                    a_spec=pl.BlockSpec((TMM, D), lambda j, i, k: (k, 0)),
                    b_spec=pl.BlockSpec((None, TMM, FF_SHARD), lambda j, i, k: (j, k, 0)),
                    o_spec=pl.BlockSpec((None, D, FF_SHARD), lambda j, i, k: (j, 0, 0)),
                    out_shape=_sds((N_DEV, D, FF_SHARD), F32), name=tag + "_dwgu")
    dn = _matmul(dgu, wgu8, dims="nt", grid=(nt, 1, N_DEV),
                 a_spec=pl.BlockSpec((None, TMM, FF_SHARD), lambda i, j, k: (k, i, 0)),
                 b_spec=pl.BlockSpec((None, D, FF_SHARD), lambda i, j, k: (k, 0, 0)),
                 o_spec=pl.BlockSpec((TMM, D), lambda i, j, k: (i, 0)),
                 out_shape=_sds((t, D), BF16), name=tag + "_dn")
    dx, dgain = _rmsnorm_bwd(x, gain, dn, d, tag + "_dnorm")
    return dx, dgain, dwgu8, dwd4


CONV_TC = 768
CONV_K = 4


def _conv_fwd(ha, w, name):
    t = ha.shape[0]
    nb = TM // 8

    def body(prev_ref, cur_ref, w_ref, o_ref):
        i = pl.program_id(0)
        cur = cur_ref[...].astype(F32)
        prev = prev_ref[...].astype(F32) * (i > 0).astype(F32)
        ext = jnp.concatenate([prev, cur], axis=0)
        wv = w_ref[...]
        acc = cur * wv[3:4]
        for k in range(1, CONV_K):
            acc = acc + pltpu.roll(ext, k, axis=0)[8:] * wv[3 - k:4 - k]
        o_ref[...] = acc.astype(o_ref.dtype)

    return pl.pallas_call(
        body, out_shape=_sds((3, t, CONV_TC), BF16), grid=(t // TM, 3),
        in_specs=[pl.BlockSpec((8, CONV_TC), lambda i, c: (jnp.maximum(i * nb - 1, 0), c)),
                  pl.BlockSpec((TM, CONV_TC), lambda i, c: (i, c)),
                  pl.BlockSpec((CONV_K, CONV_TC), lambda i, c: (0, c))],
        out_specs=pl.BlockSpec((None, TM, CONV_TC), lambda i, c: (c, i, 0)),
        name=name, compiler_params=_cparams(("parallel", "parallel")))(ha, ha, w)


def _conv_bwd(ha, w, dy3, name):
    t = ha.shape[0]
    nb = TM // 8
    nt = t // TM

    def body(prev_ref, cur_ref, dy_ref, nxt_ref, w_ref, dx_ref, dw_ref):
        i = pl.program_id(1)
        cur = cur_ref[...].astype(F32)
        prev = prev_ref[...].astype(F32) * (i > 0).astype(F32)
        ext = jnp.concatenate([prev, cur], axis=0)
        dy = dy_ref[...].astype(F32)
        nxt = nxt_ref[...].astype(F32) * (i < nt - 1).astype(F32)
        dext = jnp.concatenate([dy, nxt], axis=0)
        wv = w_ref[...]
        dx = dy * wv[3:4]
        dws = [None] * CONV_K
        dws[3] = jnp.sum(dy * cur, axis=0, keepdims=True)
        for k in range(1, CONV_K):
            dx = dx + pltpu.roll(dext, TM + 8 - k, axis=0)[:TM] * wv[3 - k:4 - k]
            dws[3 - k] = jnp.sum(dy * pltpu.roll(ext, k, axis=0)[8:], axis=0, keepdims=True)
        dx_ref[...] = dx.astype(dx_ref.dtype)

        @pl.when(i == 0)
        def _():
            dw_ref[...] = jnp.zeros_like(dw_ref)

        dw_ref[...] += jnp.concatenate(dws, axis=0)

    return pl.pallas_call(
        body, out_shape=(_sds((t, 3 * CONV_TC), BF16), _sds((CONV_K, 3 * CONV_TC), F32)), grid=(3, nt),
        in_specs=[pl.BlockSpec((8, CONV_TC), lambda c, i: (jnp.maximum(i * nb - 1, 0), c)),
                  pl.BlockSpec((TM, CONV_TC), lambda c, i: (i, c)),
                  pl.BlockSpec((None, TM, CONV_TC), lambda c, i: (c, i, 0)),
                  pl.BlockSpec((None, 8, CONV_TC), lambda c, i: (c, jnp.minimum((i + 1) * nb, t // 8 - 1), 0)),
                  pl.BlockSpec((CONV_K, CONV_TC), lambda c, i: (0, c))],
        out_specs=(pl.BlockSpec((TM, CONV_TC), lambda c, i: (i, c)),
                   pl.BlockSpec((CONV_K, CONV_TC), lambda c, i: (0, c))),
        name=name, compiler_params=_cparams(("parallel", "arbitrary")))(ha, ha, dy3, dy3, w)


def _gdn_specs(t, rev):
    ng = t // GROUP

    def gi(g):
        return ng - 1 - g if rev else g

    qkv = pl.BlockSpec((3, GROUP, HD), lambda h, g: (0, gi(g), h))
    bg = pl.BlockSpec((GROUP, 128), lambda h, g: (gi(g), 0))
    dbg = pl.BlockSpec((None, GROUP, 128), lambda h, g: (h, gi(g), 0))
    o = pl.BlockSpec((GROUP, HD), lambda h, g: (gi(g), h))
    st = pl.BlockSpec((None, None, HD, HD), lambda h, g: (h, gi(g), 0, 0))
    return qkv, bg, dbg, o, st


def _head_gates(bg, h):
    lane = lax.broadcasted_iota(jnp.int32, (1, 128), 1)
    beta = jnp.sum(jnp.where(lane == h, bg, 0.0), axis=1, keepdims=True)
    gcol = jnp.sum(jnp.where(lane == h + A_HEADS, bg, 0.0), axis=1, keepdims=True)
    return beta, gcol, _col_to_row(gcol)


def _gdn_fwd(qkv3, bg, name):
    t = qkv3.shape[1]
    ng = t // GROUP
    qkv_s, bg_s, _, o_s, st_s = _gdn_specs(t, False)

    def body(qkv_ref, bg_ref, o_ref, st_ref, s_scr):
        @pl.when(pl.program_id(1) == 0)
        def _():
            s_scr[...] = jnp.zeros_like(s_scr)

        s = s_scr[...]
        st_ref[...] = s
        beta, gcol, grow = _head_gates(bg_ref[...], pl.program_id(0))
        u, w, qd, kd, qkc, dm = _gdn_local(qkv_ref[0].astype(F32), qkv_ref[1].astype(F32),
                                           qkv_ref[2].astype(F32), beta, gcol, grow)
        for a in range(GROUP // CHUNK):
            sl = slice(a * CHUNK, (a + 1) * CHUNK)
            s, out = _gdn_step(s, w[sl], u[sl], qd[sl], kd[sl], qkc[sl], dm[sl])
            o_ref[sl, :] = out.astype(o_ref.dtype)
        s_scr[...] = s

    return pl.pallas_call(
        body, out_shape=(_sds((t, A_WIDTH), BF16), _sds((A_HEADS, ng, HD, HD), F32)), grid=(A_HEADS, ng),
        in_specs=[qkv_s, bg_s], out_specs=(o_s, st_s),
        scratch_shapes=[pltpu.VMEM((HD, HD), F32)], name=name,
        compiler_params=_cparams(("parallel", "arbitrary")))(qkv3, bg)


def _gdn_bwd(qkv3, bg, states, do, name):
    t = qkv3.shape[1]
    ng = t // GROUP
    qkv_s, bg_s, dbg_s, o_s, st_s = _gdn_specs(t, True)
    nc = GROUP // CHUNK

    def body(qkv_ref, bg_ref, st_ref, do_ref, dqkv_ref, dbg_ref, ds_scr):
        @pl.when(pl.program_id(1) == 0)
        def _():
            ds_scr[...] = jnp.zeros_like(ds_scr)

        h = pl.program_id(0)
        beta, gcol, grow = _head_gates(bg_ref[...], h)
        loc, vjp_local = jax.vjp(_gdn_local, qkv_ref[0].astype(F32), qkv_ref[1].astype(F32),
                                 qkv_ref[2].astype(F32), beta, gcol, grow)
        u, w, qd, kd, qkc, dm = loc
        s = st_ref[...]
        starts = []
        for a in range(nc):
            sl = slice(a * CHUNK, (a + 1) * CHUNK)
            starts.append(s)
            if a < nc - 1:
                s, _ = _gdn_step(s, w[sl], u[sl], qd[sl], kd[sl], qkc[sl], dm[sl])
        ds = ds_scr[...]
        parts = [None] * nc
        for a in reversed(range(nc)):
            sl = slice(a * CHUNK, (a + 1) * CHUNK)
            _, vjp_step = jax.vjp(_gdn_step, starts[a], w[sl], u[sl], qd[sl], kd[sl], qkc[sl], dm[sl])
            grads = vjp_step((ds, do_ref[sl, :].astype(F32)))
            ds = grads[0]
            parts[a] = grads[1:]
        ds_scr[...] = ds
        dw, du, dqd, dkd, dqk, ddm = [jnp.concatenate([parts[a][j] for a in range(nc)], axis=0)
                                      for j in range(6)]
        dq, dk, dv, db, dgc, dgr = vjp_local((du, dw, dqd, dkd, dqk, ddm))
        dqkv_ref[0] = dq.astype(dqkv_ref.dtype)
        dqkv_ref[1] = dk.astype(dqkv_ref.dtype)
        dqkv_ref[2] = dv.astype(dqkv_ref.dtype)
        lane = lax.broadcasted_iota(jnp.int32, (1, 128), 1)
        dbg_ref[...] = (jnp.where(lane == h, db, 0.0)
                        + jnp.where(lane == h + A_HEADS, dgc + _row_to_col(dgr), 0.0))

    return pl.pallas_call(
        body, out_shape=(_sds((3, t, A_WIDTH), BF16), _sds((A_HEADS, t, 128), F32)),
        grid=(A_HEADS, ng), in_specs=[qkv_s, bg_s, st_s, o_s],
        out_specs=(qkv_s, dbg_s), scratch_shapes=[pltpu.VMEM((HD, HD), F32)], name=name,
        compiler_params=_cparams(("parallel", "arbitrary")))(qkv3, bg, states, do)


ATT_SCALE = (QK_NOPE + QK_ROPE) ** -0.5
NEG = -1e30


def _diag_mask(shape, q_axis):
    qi = lax.shift_right_logical(lax.broadcasted_iota(jnp.int32, shape, q_axis), 6)
    ki = lax.shift_right_logical(lax.broadcasted_iota(jnp.int32, shape, 1 - q_axis), 6)
    return ki <= qi


def _col_to_row(col):
    n = col.shape[0]
    eye = lax.broadcasted_iota(jnp.int32, (n, n), 0) == lax.broadcasted_iota(jnp.int32, (n, n), 1)
    return jnp.sum(jnp.where(eye, col, 0.0), axis=0, keepdims=True)


def _row_to_col(row):
    n = row.shape[1]
    eye = lax.broadcasted_iota(jnp.int32, (n, n), 0) == lax.broadcasted_iota(jnp.int32, (n, n), 1)
    return jnp.sum(jnp.where(eye, row, 0.0), axis=1, keepdims=True)


def _blk(ref, i):
    return ref[pl.ds(pl.multiple_of(i * ATT_BQ, ATT_BQ), ATT_BQ), :]


def _att_fwd(qc, kc, v, name):
    t = qc.shape[0]
    nq = t // ATT_BQ

    def body(q_ref, k_ref, v_ref, o_ref, lse_ref, lser_ref, m_scr, l_scr, acc_scr):
        qb = pl.program_id(1)
        q = q_ref[...]
        m_scr[...] = jnp.full_like(m_scr, NEG)
        l_scr[...] = jnp.zeros_like(l_scr)
        acc_scr[...] = jnp.zeros_like(acc_scr)

        def step(kb, diag):
            s = _dot(q, _blk(k_ref, kb), "nt") * ATT_SCALE
            if diag:
                s = jnp.where(_diag_mask(s.shape, 0), s, NEG)
            m_old = m_scr[...]
            m_new = jnp.maximum(m_old, jnp.max(s, axis=1, keepdims=True))
            alpha = jnp.exp(m_old - m_new)
            p = jnp.exp(s - m_new)
            l_scr[...] = alpha * l_scr[...] + jnp.sum(p, axis=1, keepdims=True)
            acc_scr[...] = alpha * acc_scr[...] + _dot(p, _blk(v_ref, kb))
            m_scr[...] = m_new

        def loop_body(kb, carry):
            step(kb, False)
            return carry

        lax.fori_loop(0, qb, loop_body, 0)
        step(qb, True)
        o_ref[...] = (acc_scr[...] / l_scr[...]).astype(o_ref.dtype)
        lse = m_scr[...] + jnp.log(l_scr[...])
        lse_ref[...] = lse
        lser_ref[...] = _col_to_row(lse)

    return pl.pallas_call(
        body, out_shape=(_sds((t, B_HEADS * HD), BF16), _sds((B_HEADS, t, 1), F32),
                         _sds((B_HEADS, nq, 1, ATT_BQ), F32)), grid=(B_HEADS, nq),
        in_specs=[pl.BlockSpec((ATT_BQ, QK_CAT), lambda h, i: (i, h)),
                  pl.BlockSpec((t, QK_CAT), lambda h, i: (0, h)), pl.BlockSpec((t, HD), lambda h, i: (0, h))],
        out_specs=(pl.BlockSpec((ATT_BQ, HD), lambda h, i: (i, h)),
                   pl.BlockSpec((None, ATT_BQ, 1), lambda h, i: (h, i, 0)),
                   pl.BlockSpec((None, None, 1, ATT_BQ), lambda h, i: (h, i, 0, 0))),
        scratch_shapes=[pltpu.VMEM((ATT_BQ, 1), F32), pltpu.VMEM((ATT_BQ, 1), F32), pltpu.VMEM((ATT_BQ, HD), F32)],
        name=name, compiler_params=_cparams(("parallel", "arbitrary")))(qc, kc, v)


def _att_bwd(qc, kc, v, o, lse, lse_row, do, name):
    t = qc.shape[0]
    nq = t // ATT_BQ

    def delta_fn(ob, dob):
        dl = jnp.sum(ob.astype(F32) * dob.astype(F32), axis=1, keepdims=True)
        return dl, _col_to_row(dl)

    delta, delta_row = _rowcall(
        delta_fn, [o, do], [pl.BlockSpec((ATT_BQ, HD), lambda i, h: (i, h))] * 2,
        [_sds((B_HEADS, t, 1), F32), _sds((B_HEADS, nq, 1, ATT_BQ), F32)],
        [pl.BlockSpec((None, ATT_BQ, 1), lambda i, h: (h, i, 0)),
         pl.BlockSpec((None, None, 1, ATT_BQ), lambda i, h: (h, i, 0, 0))], (nq, B_HEADS), name + "_delta")

    def dq_body(q_ref, k_ref, v_ref, do_ref, lse_ref, dl_ref, dq_ref, acc):
        qb = pl.program_id(1)
        q, dob, lse_b, dl_b = q_ref[...], do_ref[...], lse_ref[...], dl_ref[...]
        acc[...] = jnp.zeros_like(acc)

        def step(kb, diag):
            k = _blk(k_ref, kb)
            s = _dot(q, k, "nt") * ATT_SCALE
            if diag:
                s = jnp.where(_diag_mask(s.shape, 0), s, NEG)
            p = jnp.exp(s - lse_b)
            ds = p * (_dot(dob, _blk(v_ref, kb), "nt") - dl_b) * ATT_SCALE
            acc[...] += _dot(ds, k)

        def loop_body(kb, carry):
            step(kb, False)
            return carry

        lax.fori_loop(0, qb, loop_body, 0)
        step(qb, True)
        dq_ref[...] = acc[...].astype(dq_ref.dtype)

    qmap = lambda h, i: (i, h)
    colq = pl.BlockSpec((None, ATT_BQ, 1), lambda h, i: (h, i, 0))
    dq = pl.pallas_call(
        dq_body, out_shape=_sds((t, B_HEADS * QK_CAT), BF16), grid=(B_HEADS, nq),
        in_specs=[pl.BlockSpec((ATT_BQ, QK_CAT), qmap), pl.BlockSpec((t, QK_CAT), lambda h, i: (0, h)),
                  pl.BlockSpec((t, HD), lambda h, i: (0, h)), pl.BlockSpec((ATT_BQ, HD), qmap), colq, colq],
        out_specs=pl.BlockSpec((ATT_BQ, QK_CAT), qmap),
        scratch_shapes=[pltpu.VMEM((ATT_BQ, QK_CAT), F32)], name=name + "_dq",
        compiler_params=_cparams(("parallel", "arbitrary")))(qc, kc, v, do, lse, delta)

    def dkv_body(k_ref, v_ref, q_ref, do_ref, lser_ref, dlr_ref, dk_ref, dv_ref, dk_acc, dv_acc):
        kb = pl.program_id(1)
        k, vv = k_ref[...], v_ref[...]
        dk_acc[...] = jnp.zeros_like(dk_acc)
        dv_acc[...] = jnp.zeros_like(dv_acc)

        def step(qb, diag):
            q, dob = _blk(q_ref, qb), _blk(do_ref, qb)
            st = _dot(k, q, "nt") * ATT_SCALE
            if diag:
                st = jnp.where(_diag_mask(st.shape, 1), st, NEG)
            pt = jnp.exp(st - lser_ref[qb])
            dst = pt * (_dot(vv, dob, "nt") - dlr_ref[qb]) * ATT_SCALE
            dv_acc[...] += _dot(pt, dob)
            dk_acc[...] += _dot(dst, q)

        def loop_body(qb, carry):
            step(qb, False)
            return carry

        step(kb, True)
        lax.fori_loop(kb + 1, nq, loop_body, 0)
        dk_ref[...] = dk_acc[...].astype(dk_ref.dtype)
        dv_ref[...] = dv_acc[...].astype(dv_ref.dtype)

    kmap = lambda h, j: (j, h)
    rowq = pl.BlockSpec((None, nq, 1, ATT_BQ), lambda h, j: (h, 0, 0, 0))
    dk, dv = pl.pallas_call(
        dkv_body, out_shape=(_sds((t, B_HEADS * QK_CAT), BF16), _sds((t, B_HEADS * HD), BF16)),
        grid=(B_HEADS, nq),
        in_specs=[pl.BlockSpec((ATT_BQ, QK_CAT), kmap), pl.BlockSpec((ATT_BQ, HD), kmap),
                  pl.BlockSpec((t, QK_CAT), lambda h, j: (0, h)), pl.BlockSpec((t, HD), lambda h, j: (0, h)),
                  rowq, rowq],
        out_specs=(pl.BlockSpec((ATT_BQ, QK_CAT), kmap), pl.BlockSpec((ATT_BQ, HD), kmap)),
        scratch_shapes=[pltpu.VMEM((ATT_BQ, QK_CAT), F32), pltpu.VMEM((ATT_BQ, HD), F32)], name=name + "_dkv",
        compiler_params=_cparams(("parallel", "arbitrary")))(kc, v, qc, do, lse_row, delta_row)
    return dq, dk, dv


def _mem_fwd(hx, col, mkv, name):
    t = hx.shape[0]
    (o,) = _rowcall(_memattn, [hx, mkv, mkv],
                    [_rows(TM, MEM_W, col), pl.BlockSpec((N_MEM, MEM_W), lambda i: (0, 0)),
                     pl.BlockSpec((N_MEM, MEM_W), lambda i: (0, 1))],
                    [_sds((t, MEM_W), BF16)], [_rows(TM, MEM_W)], (t // TM,), name)
    return o


def _mem_bwd(hx, col, mkv, do, do_col, name):
    t = hx.shape[0]
    dq, dk, dv = _rowcall(_vjp_fn(_memattn, 3, (0, 1, 2)), [hx, mkv, mkv, do],
                          [_rows(TM, MEM_W, col), pl.BlockSpec((N_MEM, MEM_W), lambda i: (0, 0)),
                           pl.BlockSpec((N_MEM, MEM_W), lambda i: (0, 1)), _rows(TM, MEM_W, do_col)],
                          [_sds((t, MEM_W), BF16), _sds((N_MEM, MEM_W), F32), _sds((N_MEM, MEM_W), F32)],
                          [_rows(TM, MEM_W), _shared((N_MEM, MEM_W)), _shared((N_MEM, MEM_W))],
                          (t // TM,), name, n_acc=2)
    return dq, jnp.concatenate([dk, dv], axis=1)


def _a_in_ext(w):
    nb = 4 * A_WIDTH
    ba = jnp.pad(w[:, nb:nb + 2 * A_HEADS], ((0, 0), (0, 128 - 2 * A_HEADS)))
    return jnp.concatenate([w[:, :nb], w[:, nb + 2 * A_HEADS:], ba], axis=1)


def _swap_halves(w):
    return jnp.concatenate([w[..., QK_ROPE // 2:], w[..., :QK_ROPE // 2]], axis=-1)


def _uq_ext(w):
    w = w.reshape(Q_LORA, B_HEADS, QK_NOPE + QK_ROPE)
    nope, rope = w[..., :QK_NOPE], w[..., QK_NOPE:]
    z64 = jnp.zeros((Q_LORA, B_HEADS, QK_CAT - QK_NOPE - QK_ROPE), w.dtype)
    z128 = jnp.zeros((Q_LORA, B_HEADS, QK_NOPE), w.dtype)
    a = jnp.concatenate([nope, rope, z64], axis=-1).reshape(Q_LORA, B_HEADS * QK_CAT)
    b = jnp.concatenate([z128, _swap_halves(rope), z64], axis=-1).reshape(Q_LORA, B_HEADS * QK_CAT)
    return jnp.concatenate([a, b], axis=1)


def _dkv_ext(w):
    ckv, kr = w[:, :KV_LORA], w[:, KV_LORA:]
    z128 = jnp.zeros((D, QK_NOPE), w.dtype)
    z64 = jnp.zeros((D, QK_CAT - QK_NOPE - QK_ROPE), w.dtype)
    return jnp.concatenate([ckv, z128, kr, z64, z128, _swap_halves(kr), z64], axis=1)


def _ukv_ext(w):
    w = w.reshape(KV_LORA, B_HEADS, QK_NOPE + HD)
    kn, vv = w[..., :QK_NOPE], w[..., QK_NOPE:]
    z = jnp.zeros((KV_LORA, B_HEADS, QK_CAT - QK_NOPE), w.dtype)
    a = jnp.concatenate([kn, z], axis=-1).reshape(KV_LORA, B_HEADS * QK_CAT)
    return jnp.concatenate([a, vv.reshape(KV_LORA, B_HEADS * HD)], axis=1)


def _ext_and_back(fn, w):
    ext, back = jax.vjp(fn, w.astype(F32))
    return ext.astype(BF16), lambda g: back(g.astype(F32))[0]


def _rope_tables(pos_col):
    t = pos_col.shape[0]
    inv = (ROPE_THETA ** (-np.arange(0, QK_ROPE, 2, dtype=np.float32) / QK_ROPE)).astype(np.float32)
    inv_row = np.zeros((1, QK_CAT), np.float32)
    inv_row[0, QK_NOPE:QK_NOPE + QK_ROPE] = np.concatenate([inv, inv])
    sign = np.zeros((1, QK_CAT), np.float32)
    sign[0, QK_NOPE:QK_NOPE + QK_ROPE // 2] = -1.0
    sign[0, QK_NOPE + QK_ROPE // 2:QK_NOPE + QK_ROPE] = 1.0
    is_rope = np.abs(sign)
    is_nope = np.zeros((1, QK_CAT), np.float32)
    is_nope[0, :QK_NOPE] = 1.0

    def fn(p, inv_b, sign_b, rope_b, nope_b):
        ang = p.astype(F32) * inv_b
        return jnp.cos(ang) * rope_b + nope_b, jnp.sin(ang) * sign_b

    consts = [jnp.asarray(a) for a in (inv_row, sign, is_rope, is_nope)]
    return _rowcall(fn, [pos_col] + consts, [_rows(TM, 1)] + [_shared((1, QK_CAT))] * 4,
                    [_sds((t, QK_CAT), F32)] * 2, [_rows(TM, QK_CAT)] * 2, (t // TM,), "rope_tables")


def _local_step(x, mem, pos, target, w):
    t = x.shape[0]
    g = {}
    head6 = (t // TM, A_HEADS)

    mem_n = _rmsnorm_fwd_small(mem, w["mem_norm"])
    rope_c, rope_s = _rope_tables(pos.reshape(t, 1))
    mkv = [_mm(mem_n, w["w_mem_kv"][l], "nn", BF16, f"mkv{l}") for l in range(DEPTH)]

    saved = []
    for l in range(DEPTH):
        sv = {}
        x, sv["ffn1"] = _ffn_fwd(x, w["ffn1_norm"][l], w["ffn1_w_gu"][l], w["ffn1_w_down"][l], "ffn1")
        sv["x1"] = x
        n2 = _rmsnorm_fwd(x, w["mix_norm"][l], "mix_norm")
        sv["n2"] = n2
        if l < N_A:
            ha = _mm(n2, w["a_w_in"][l], "nn", BF16, "a_in")
            yc3 = _conv_fwd(ha, w["a_conv"][l], "a_conv")
            blk3 = pl.BlockSpec((3, TM, HD), lambda i, h: (0, i, h))
            (qkv3,) = _rowcall(lambda b: jnp.stack(_gdn_prep(b[0].astype(F32), b[1].astype(F32), b[2].astype(F32))),
                               [yc3], [blk3], [_sds((3, t, A_WIDTH), BF16)], [blk3], head6, "a_prep")
            (bg,) = _rowcall(_gates, [ha, _pad128(w["a_A_log"][l], A_HEADS), _pad128(w["a_dt_bias"][l], A_HEADS)],
                             [_rows(TM, 128, A_BA_BLK), _shared((1, 128)), _shared((1, 128))],
                             [_sds((t, 128), F32)], [_rows(TM, 128)], (t // TM,), "a_gates")
            o_gdn, states = _gdn_fwd(qkv3, bg, "a_gdn")
            (o_a,) = _rowcall(_outnorm_gate, [o_gdn, ha, w["a_out_norm"][l].reshape(1, HD)],
                              [pl.BlockSpec((TM, HD), lambda i, h: (i, h)),
                               pl.BlockSpec((TM, HD), lambda i, h: (i, 3 * A_HEADS + h)), _shared((1, HD))],
                              [_sds((t, A_WIDTH), BF16)], [pl.BlockSpec((TM, HD), lambda i, h: (i, h))],
                              head6, "a_outnorm")
            o_m = _mem_fwd(ha, A_MQ_BLK, mkv[l], "mem_attn_a")
            sv.update(ha=ha, yc3=yc3, qkv3=qkv3, bg=bg, states=states, o_gdn=o_gdn)
            cat = jnp.concatenate([o_a, o_m], axis=1)
        else:
            j = l - N_A
            hb = _mm(n2, w["b_w_in"][j], "nn", BF16, "b_in")
            cqn = _rmsnorm_fwd(hb, w["b_q_norm"][j], "b_qnorm", 0, Q_LORA)
            qq = _mm(cqn, w["b_w_uq"][j], "nn", BF16, "b_uq")
            (qc,) = _rowcall(_rope_mix, [qq, qq, rope_c, rope_s],
                             [pl.BlockSpec((TM, QK_CAT), lambda i, h: (i, h)),
                              pl.BlockSpec((TM, QK_CAT), lambda i, h: (i, B_HEADS + h)),
                              pl.BlockSpec((TM, QK_CAT), lambda i, h: (i, 0)),
                              pl.BlockSpec((TM, QK_CAT), lambda i, h: (i, 0))],
                             [_sds((t, B_HEADS * QK_CAT), BF16)], [pl.BlockSpec((TM, QK_CAT), lambda i, h: (i, h))],
                             head6, "b_qrope")
            o_b, lse, lse_row = _att_fwd(qc, kcat, vmla, "b_attn")
            o_m = _mem_fwd(hb, 1, mkv[l], "mem_attn_b")
            sv.update(hb=hb, cqn=cqn, qc=qc, o_b=o_b, lse=(lse, lse_row))
            cat = jnp.concatenate([o_b, o_m], axis=1)
        sv["cat"] = cat
        x = _mm(cat, w["w_out"][l], "nn", F32, "w_out", res=x)
        x, sv["ffn2"] = _ffn_fwd(x, w["ffn2_norm"][l], w["ffn2_w_gu"][l], w["ffn2_w_down"][l], "ffn2")
        saved.append(sv)
        if l == N_A - 1:
            x_kv = x
            nkv = _rmsnorm_fwd(x, w["kv_in_norm"], "kv_in_norm")
            ckr = _mm(nkv, w["w_dkv"], "nn", BF16, "kv_down")
            ckv_n = _rmsnorm_fwd(ckr, w["kv_lat_norm"], "kv_lat_norm", 0, KV_LORA)
            kvu = _mm(ckv_n, w["w_ukv"], "nn", BF16, "kv_up")
            vmla = kvu[:, B_HEADS * QK_CAT:]
            (kcat,) = _rowcall(_kcat, [kvu, ckr, ckr, rope_c, rope_s],
                               [pl.BlockSpec((TM, QK_CAT), lambda i, h: (i, h)),
                                pl.BlockSpec((TM, QK_CAT), lambda i, h: (i, 1)),
                                pl.BlockSpec((TM, QK_CAT), lambda i, h: (i, 2)),
                                pl.BlockSpec((TM, QK_CAT), lambda i, h: (i, 0)),
                                pl.BlockSpec((TM, QK_CAT), lambda i, h: (i, 0))],
                               [_sds((t, B_HEADS * QK_CAT), BF16)],
                               [pl.BlockSpec((TM, QK_CAT), lambda i, h: (i, h))], head6, "kv_cat")

    def loss_fn(xb, gb, tb):
        def f(xx, gg):
            e = _rms(xx, gg) - tb
            return 0.5 * jnp.sum(jnp.mean(e * e, axis=-1, keepdims=True), axis=0, keepdims=True)
        val, vjp = jax.vjp(f, xb, gb)
        dx, dg = vjp(jnp.ones((1, 1), F32))
        return dx, dg, val * jnp.ones((1, 128), F32)

    d, dfin, loss = _rowcall(loss_fn, [x, w["final_norm"].reshape(1, D), target],
                             [_rows(TM, D), _shared((1, D)), _rows(TM, D)],
                             [_sds((t, D), F32), _sds((1, D), F32), _sds((1, 128), F32)],
                             [_rows(TM, D), _shared((1, D)), _shared((1, 128))], (t // TM,), "loss_head", n_acc=2)
    g["final_norm"] = dfin[0]
    loss = loss[0, 0]

    for name in ("ffn1_norm", "ffn1_w_gu", "ffn1_w_down", "mix_norm", "ffn2_norm", "ffn2_w_gu", "ffn2_w_down",
                 "w_out", "w_mem_kv"):
        g[name] = [None] * DEPTH
    for name in ("a_w_in", "a_conv", "a_A_log", "a_dt_bias", "a_out_norm"):
        g[name] = [None] * N_A
    for name in ("b_w_in", "b_q_norm", "b_w_uq"):
        g[name] = [None] * N_B
    dmkv = [None] * DEPTH
    dkcat = []
    dvmla = []

    for l in reversed(range(DEPTH)):
        sv = saved[l]
        if l == N_A - 1:
            kq = pl.BlockSpec((TM, QK_CAT), lambda i, h: (i, h))
            tab = pl.BlockSpec((TM, QK_CAT), lambda i, h: (i, 0))

            def dk_fn(c, s, d0, d1):
                dk = d0.astype(F32) + d1.astype(F32)
                return dk, dk * c, dk * s

            dkn, dkr_h, dkrs_h = _rowcall(dk_fn, [rope_c, rope_s, dkcat[0], dkcat[1]], [tab, tab, kq, kq],
                                          [_sds((t, B_HEADS * QK_CAT), BF16)] + [_sds((B_HEADS, t, QK_CAT), BF16)] * 2,
                                          [kq] + [pl.BlockSpec((None, TM, QK_CAT), lambda i, h: (h, i, 0))] * 2,
                                          head6, "kv_dcat")

            def sum6(a, b):
                return jnp.sum(a.astype(F32), axis=0), jnp.sum(b.astype(F32), axis=0)

            h6 = pl.BlockSpec((B_HEADS, TM, QK_CAT), lambda i: (0, i, 0))
            dkr, dkrs = _rowcall(sum6, [dkr_h, dkrs_h], [h6, h6], [_sds((t, QK_CAT), BF16)] * 2,
                                 [_rows(TM, QK_CAT)] * 2, (t // TM,), "kv_dkr")

            def addv(a, b):
                return a.astype(F32) + b.astype(F32)

            (dv,) = _rowcall(addv, dvmla, [_rows(TM, B_HEADS * HD)] * 2, [_sds((t, B_HEADS * HD), BF16)],
                             [_rows(TM, B_HEADS * HD)], (t // TM,), "kv_dv")
            dkvu = jnp.concatenate([dkn, dv], axis=1)
            g["w_ukv"] = _mm(ckv_n, dkvu, "tn", F32, "kv_up_dw")
            dckv_n = _mm(dkvu, w["w_ukv"], "nt", BF16, "kv_up_dx")

            def lat_bwd(cb, gb, dnb):
                return _vjp_fn(_rms, 2, (0, 1))(cb, gb, dnb)

            dckv, g["kv_lat_norm"] = _rowcall(lat_bwd, [ckr, w["kv_lat_norm"].reshape(1, KV_LORA), dckv_n],
                                              [_rows(TM, KV_LORA), _shared((1, KV_LORA)), _rows(TM, KV_LORA)],
                                              [_sds((t, KV_LORA), BF16), _sds((1, KV_LORA), F32)],
                                              [_rows(TM, KV_LORA), _shared((1, KV_LORA))], (t // TM,),
                                              "kv_lat_dnorm", n_acc=1)
            g["kv_lat_norm"] = g["kv_lat_norm"][0]
            dckr = jnp.concatenate([dckv, dkr, dkrs], axis=1)
            g["w_dkv"] = _mm(nkv, dckr, "tn", F32, "kv_down_dw")
            dnkv = _mm(dckr, w["w_dkv"], "nt", BF16, "kv_down_dx")
            d, g["kv_in_norm"] = _rmsnorm_bwd(x_kv, w["kv_in_norm"], dnkv, d, "kv_in_dnorm")

        d, g["ffn2_norm"][l], g["ffn2_w_gu"][l], g["ffn2_w_down"][l] = _ffn_bwd(
            d, sv["ffn2"], w["ffn2_norm"][l], w["ffn2_w_gu"][l], w["ffn2_w_down"][l], "ffn2b")
        g["w_out"][l] = _mm(sv["cat"], d, "tn", F32, "w_out_dw")
        dcat = _mm(d, w["w_out"][l], "nt", BF16, "w_out_dx")
        if l < N_A:
            ha, yc3, qkv3, states, o_gdn = sv["ha"], sv["yc3"], sv["qkv3"], sv["states"], sv["o_gdn"]
            dmq, dmkv[l] = _mem_bwd(ha, A_MQ_BLK, mkv[l], dcat, 3, "mem_attn_a_bwd")
            hblk = pl.BlockSpec((TM, HD), lambda i, h: (i, h))
            do_gdn, dgate, dgain = _rowcall(
                _vjp_fn(_outnorm_gate, 3, (0, 1, 2)), [o_gdn, ha, w["a_out_norm"][l].reshape(1, HD), dcat],
                [hblk, pl.BlockSpec((TM, HD), lambda i, h: (i, 3 * A_HEADS + h)), _shared((1, HD)), hblk],
                [_sds((t, A_WIDTH), BF16), _sds((t, A_WIDTH), BF16), _sds((1, HD), F32)],
                [hblk, hblk, _shared((1, HD))], head6, "a_outnorm_bwd", n_acc=1)
            g["a_out_norm"][l] = dgain[0]
            dqkv3, dbg6 = _gdn_bwd(qkv3, sv["bg"], states, do_gdn, "a_gdn_bwd")

            def dgates(bab, alb, dtb, d6):
                return _vjp_fn(_gates, 3, (0, 1, 2))(bab, alb, dtb, jnp.sum(d6, axis=0))

            dba, dalog, ddt = _rowcall(
                dgates, [ha, _pad128(w["a_A_log"][l], A_HEADS), _pad128(w["a_dt_bias"][l], A_HEADS), dbg6],
                [_rows(TM, 128, A_BA_BLK), _shared((1, 128)), _shared((1, 128)),
                 pl.BlockSpec((A_HEADS, TM, 128), lambda i: (0, i, 0))],
                [_sds((t, 128), BF16), _sds((1, 128), F32), _sds((1, 128), F32)],
                [_rows(TM, 128), _shared((1, 128)), _shared((1, 128))], (t // TM,), "a_gates_bwd", n_acc=2)
            g["a_A_log"][l] = dalog[0, A_HEADS:2 * A_HEADS]
            g["a_dt_bias"][l] = ddt[0, A_HEADS:2 * A_HEADS]
            blk3 = pl.BlockSpec((3, TM, HD), lambda i, h: (0, i, h))
            def dprep(b, db):
                return jnp.stack(_vjp_fn(_gdn_prep, 3, (0, 1, 2))(b[0], b[1], b[2], db[0], db[1], db[2]))

            (dyc3,) = _rowcall(dprep, [yc3, dqkv3], [blk3, blk3],
                               [_sds((3, t, A_WIDTH), BF16)], [blk3], head6, "a_prep_bwd")
            dqkv_in, g["a_conv"][l] = _conv_bwd(ha, w["a_conv"][l], dyc3, "a_conv_bwd")
            dha = jnp.concatenate([dqkv_in, dgate, dmq, dba], axis=1)
            g["a_w_in"][l] = _mm(sv["n2"], dha, "tn", F32, "a_in_dw")
            dn2 = _mm(dha, w["a_w_in"][l], "nt", BF16, "a_in_dx")
        else:
            j = l - N_A
            hb, cqn, qc, o_b, lse = sv["hb"], sv["cqn"], sv["qc"], sv["o_b"], sv["lse"]
            dmq, dmkv[l] = _mem_bwd(hb, 1, mkv[l], dcat, 3, "mem_attn_b_bwd")
            dqc, dkc, dvv = _att_bwd(qc, kcat, vmla, o_b, lse[0], lse[1], dcat, "b_attn_bwd")
            dkcat.append(dkc)
            dvmla.append(dvv)
            kq = pl.BlockSpec((TM, QK_CAT), lambda i, h: (i, h))
            tab = pl.BlockSpec((TM, QK_CAT), lambda i, h: (i, 0))

            def dq_fn(c, s, dq):
                dq = dq.astype(F32)
                return jnp.stack([dq * c, dq * s])

            (dqq,) = _rowcall(dq_fn, [rope_c, rope_s, dqc], [tab, tab, kq],
                              [_sds((2, t, B_HEADS * QK_CAT), BF16)],
                              [pl.BlockSpec((2, TM, QK_CAT), lambda i, h: (0, i, h))], head6, "b_qrope_bwd")
            dqq = jnp.concatenate([dqq[0], dqq[1]], axis=1)
            g["b_w_uq"][j] = _mm(cqn, dqq, "tn", F32, "b_uq_dw")
            dcqn = _mm(dqq, w["b_w_uq"][j], "nt", BF16, "b_uq_dx")
            dcq, dqg = _rowcall(_vjp_fn(_rms, 2, (0, 1)), [hb, w["b_q_norm"][j].reshape(1, Q_LORA), dcqn],
                                [_rows(TM, Q_LORA), _shared((1, Q_LORA)), _rows(TM, Q_LORA)],
                                [_sds((t, Q_LORA), BF16), _sds((1, Q_LORA), F32)],
                                [_rows(TM, Q_LORA), _shared((1, Q_LORA))], (t // TM,), "b_qnorm_bwd", n_acc=1)
            g["b_q_norm"][j] = dqg[0]
            dhb = jnp.concatenate([dcq, dmq], axis=1)
            g["b_w_in"][j] = _mm(sv["n2"], dhb, "tn", F32, "b_in_dw")
            dn2 = _mm(dhb, w["b_w_in"][j], "nt", BF16, "b_in_dx")
        d, g["mix_norm"][l] = _rmsnorm_bwd(sv["x1"], w["mix_norm"][l], dn2, d, "mix_dnorm")
        d, g["ffn1_norm"][l], g["ffn1_w_gu"][l], g["ffn1_w_down"][l] = _ffn_bwd(
            d, sv["ffn1"], w["ffn1_norm"][l], w["ffn1_w_gu"][l], w["ffn1_w_down"][l], "ffn1b")

    dmem_n = None
    for l in range(DEPTH):
        g["w_mem_kv"][l] = _mm(mem_n, dmkv[l], "tn", F32, f"mkv_dw{l}")
        dmem_n = _mm(dmkv[l], w["w_mem_kv"][l], "nt", F32, f"mkv_dx{l}", res=dmem_n)
    (_, gmn) = _rowcall(_vjp_fn(_rms, 2, (0, 1)), [mem, w["mem_norm"].reshape(1, D), dmem_n],
                        [_shared((N_MEM, D)), _shared((1, D)), _shared((N_MEM, D))],
                        [_sds((N_MEM, D), F32), _sds((1, D), F32)], [_shared((N_MEM, D)), _shared((1, D))],
                        (1,), "mem_dnorm")
    g["mem_norm"] = gmn[0]
    return loss, d, g


def _pad128(v, offset):
    return jnp.pad(v.astype(F32).reshape(1, -1), ((0, 0), (offset, 128 - offset - v.shape[0])))


def _rmsnorm_fwd_small(x, gain):
    r, w = x.shape
    (n,) = _rowcall(_rms, [x, gain.reshape(1, w)], [_shared((r, w)), _shared((1, w))],
                    [_sds((r, w), BF16)], [_shared((r, w))], (1,), "mem_norm")
    return n


def _exchange(srcs, gather, name):
    n = len(srcs)
    blks = [tuple(s.shape) if gather else tuple(s.shape[1:]) for s in srcs]

    def body(*refs):
        src_refs, out_refs = refs[:n], refs[n:2 * n]
        send_sems, recv_sems, local_sems = refs[2 * n:]
        x, y, c = lax.axis_index("x"), lax.axis_index("y"), lax.axis_index("c")
        me = 4 * x + 2 * y + c
        copies = []
        for k in range(1, N_DEV):
            px = (x + (k >> 2 & 1)) % 2
            py = (y + (k >> 1 & 1)) % 2
            pc = (c + (k & 1)) % 2
            peer = 4 * px + 2 * py + pc
            for a in range(n):
                cp = pltpu.make_async_remote_copy(
                    src_ref=src_refs[a] if gather else src_refs[a].at[peer], dst_ref=out_refs[a].at[me],
                    send_sem=send_sems.at[a, k - 1], recv_sem=recv_sems.at[a, k - 1],
                    device_id=(px, py, pc), device_id_type=pl.DeviceIdType.MESH)
                cp.start()
                copies.append(cp)
        for a in range(n):
            cp = pltpu.make_async_copy(src_refs[a] if gather else src_refs[a].at[me], out_refs[a].at[me],
                                       local_sems.at[a])
            cp.start()
            copies.append(cp)
        for cp in copies:
            cp.wait()

    return pl.pallas_call(
        body, out_shape=tuple(_sds((N_DEV,) + b, s.dtype) for b, s in zip(blks, srcs)),
        in_specs=[pl.BlockSpec(memory_space=pl.ANY)] * n, out_specs=tuple([pl.BlockSpec(memory_space=pl.ANY)] * n),
        scratch_shapes=[pltpu.SemaphoreType.DMA((n, N_DEV - 1)), pltpu.SemaphoreType.DMA((n, N_DEV - 1)),
                        pltpu.SemaphoreType.DMA((n,))],
        name=name)(*srcs)


def _reduce_adamw(parts, wp, mp, vp, name):
    r, cols = wp.shape
    tr = _tile_rows(r, cols)
    c1 = 1.0 - ADAM_B1 ** ADAM_STEP
    c2 = 1.0 - ADAM_B2 ** ADAM_STEP

    def fn(pb, wb, mb, vb):
        gsum = pb[0].astype(F32)
        for j in range(1, N_DEV):
            gsum = gsum + pb[j].astype(F32)
        m_new = ADAM_B1 * mb + (1.0 - ADAM_B1) * gsum
        v_new = ADAM_B2 * vb + (1.0 - ADAM_B2) * (gsum * gsum)
        delta = -ADAM_LR * ((m_new / c1) / (jnp.sqrt(v_new / c2) + ADAM_EPS) + ADAM_WD * wb)
        return gsum, delta, m_new, v_new

    row = _rows(tr, cols)
    return _rowcall(fn, [parts, wp, mp, vp],
                    [pl.BlockSpec((N_DEV, tr, cols), lambda i: (0, i, 0)), row, row, row],
                    [_sds((r, cols), F32)] * 4, [row] * 4, (r // tr,), name)


def _tile_rows(r, cols):
    for t in (512, 256, 128, 64, 32, 16):
        if r % t == 0 and t * cols <= 160 * 1024:
            return t
    return r


def _pack(arrs):
    flat = jnp.concatenate([a.reshape(-1).astype(F32) for a in arrs])
    n = flat.shape[0]
    unit = PACK_W * PACK_ROWS
    tot = -(-n // unit) * unit
    return jnp.pad(flat, (0, tot - n)).reshape(tot // PACK_W, PACK_W)


def _unpack(buf, shapes):
    out, off = [], 0
    flat = buf.reshape(-1)
    for s in shapes:
        n = int(np.prod(s))
        out.append(flat[off:off + n].reshape(s))
        off += n
    return out


def _as2d(a):
    return a.reshape(-1, a.shape[-1])


_SHARDED = ["ffn1_w_gu", "ffn1_w_down", "ffn2_w_gu", "ffn2_w_down", "w_out", "w_mem_kv", "a_w_in", "a_conv",
            "b_w_in", "b_w_uq", "w_dkv", "w_ukv"]
_COL_SHARDED = {"ffn1_w_gu", "ffn2_w_gu", "a_conv", "b_w_uq", "w_ukv"}
_LAYERED = {"ffn1_w_gu": DEPTH, "ffn1_w_down": DEPTH, "ffn2_w_gu": DEPTH, "ffn2_w_down": DEPTH, "w_out": DEPTH,
            "w_mem_kv": DEPTH, "a_w_in": N_A, "a_conv": N_A, "b_w_in": N_B, "b_w_uq": N_B}
_REPLICATED = ["ffn1_norm", "mix_norm", "ffn2_norm", "mem_norm", "a_A_log", "a_dt_bias", "a_out_norm", "b_q_norm",
               "kv_in_norm", "kv_lat_norm", "final_norm"]
_WEIGHTS = ["ffn1_norm", "ffn1_w_gu", "ffn1_w_down", "mix_norm", "ffn2_norm", "ffn2_w_gu", "ffn2_w_down", "w_out",
            "mem_norm", "w_mem_kv", "a_w_in", "a_conv", "a_A_log", "a_dt_bias", "a_out_norm", "b_w_in", "b_q_norm",
            "b_w_uq", "kv_in_norm", "w_dkv", "kv_lat_norm", "w_ukv", "final_norm"]


def _full_from_shards(name, sh):
    if name in ("ffn1_w_gu", "ffn2_w_gu"):
        return sh
    if name in ("ffn1_w_down", "ffn2_w_down"):
        return sh.reshape(4, FF_SHARD, D)
    if name in _COL_SHARDED:
        return jnp.moveaxis(sh, 0, -2).reshape(sh.shape[1:-1] + (N_DEV * sh.shape[-1],))
    return sh.reshape((N_DEV * sh.shape[1],) + sh.shape[2:])


def _shards_from_full(name, full):
    if name in ("ffn1_w_gu", "ffn2_w_gu"):
        return full
    if name in ("ffn1_w_down", "ffn2_w_down"):
        return full.reshape(N_DEV, D_FF // N_DEV, D)
    if name in _COL_SHARDED:
        r, cc = full.shape
        return jnp.moveaxis(full.reshape(r, N_DEV, cc // N_DEV), 1, 0)
    return full.reshape((N_DEV, full.shape[0] // N_DEV) + full.shape[1:])


def kernel(x, mem, positions, ffn1_norm, ffn1_w_gu, ffn1_w_down, mix_norm, ffn2_norm, ffn2_w_gu, ffn2_w_down, w_out, mem_norm, w_mem_kv, a_w_in, a_conv, a_A_log, a_dt_bias, a_out_norm, b_w_in, b_q_norm, b_w_uq, kv_in_norm, w_dkv, kv_lat_norm, w_ukv, final_norm, loss_target, m_ffn1_norm, m_ffn1_w_gu, m_ffn1_w_down, m_mix_norm, m_ffn2_norm, m_ffn2_w_gu, m_ffn2_w_down, m_w_out, m_mem_norm, m_w_mem_kv, m_a_w_in, m_a_conv, m_a_A_log, m_a_dt_bias, m_a_out_norm, m_b_w_in, m_b_q_norm, m_b_w_uq, m_kv_in_norm, m_w_dkv, m_kv_lat_norm, m_w_ukv, m_final_norm, v_ffn1_norm, v_ffn1_w_gu, v_ffn1_w_down, v_mix_norm, v_ffn2_norm, v_ffn2_w_gu, v_ffn2_w_down, v_w_out, v_mem_norm, v_w_mem_kv, v_a_w_in, v_a_conv, v_a_A_log, v_a_dt_bias, v_a_out_norm, v_b_w_in, v_b_q_norm, v_b_w_uq, v_kv_in_norm, v_w_dkv, v_kv_lat_norm, v_w_ukv, v_final_norm):
    loc = dict(locals())
    wl = {n: loc[n] for n in _WEIGHTS}
    ml = {n: loc["m_" + n] for n in _WEIGHTS}
    vl = {n: loc["v_" + n] for n in _WEIGHTS}

    pieces = _exchange([wl[n].astype(BF16) for n in _SHARDED], True, "gather_weights")
    w = {}
    back = {}
    for n, p in zip(_SHARDED, pieces):
        if n in _LAYERED:
            w[n] = [_full_from_shards(n, p[:, l]) for l in range(_LAYERED[n])]
        else:
            w[n] = _full_from_shards(n, p)
    for n in _REPLICATED:
        w[n] = wl[n]
    for l in range(N_A):
        w["a_w_in"][l], back[("a_w_in", l)] = _ext_and_back(_a_in_ext, w["a_w_in"][l])
        w["a_conv"][l] = w["a_conv"][l].astype(F32)
    for j in range(N_B):
        w["b_w_uq"][j], back[("b_w_uq", j)] = _ext_and_back(_uq_ext, w["b_w_uq"][j])
    w["w_dkv"], back["w_dkv"] = _ext_and_back(_dkv_ext, w["w_dkv"])
    w["w_ukv"], back["w_ukv"] = _ext_and_back(_ukv_ext, w["w_ukv"])

    loss, dx, g = _local_step(x[0], mem[0], positions[0], loss_target[0], w)
    loss = lax.psum(loss, ("x", "y", "c"))

    for l in range(N_A):
        g["a_w_in"][l] = back[("a_w_in", l)](g["a_w_in"][l])
    for j in range(N_B):
        g["b_w_uq"][j] = back[("b_w_uq", j)](g["b_w_uq"][j])
    g["w_dkv"] = back["w_dkv"](g["w_dkv"])
    g["w_ukv"] = back["w_ukv"](g["w_ukv"])

    gsh = []
    for n in _SHARDED:
        if n in _LAYERED:
            gsh.append(jnp.stack([_shards_from_full(n, g[n][l]).astype(BF16) for l in range(_LAYERED[n])], axis=1))
        else:
            gsh.append(_shards_from_full(n, g[n]).astype(BF16))
    parts = _exchange(gsh, False, "scatter_grads")
    out = {}
    for n, p in zip(_SHARDED, parts):
        shape = wl[n].shape
        res = _reduce_adamw(p.reshape(N_DEV, -1, shape[-1]), _as2d(wl[n]), _as2d(ml[n]), _as2d(vl[n]), "adamw_" + n)
        for kind, buf in zip(("grad", "delta", "new_m", "new_v"), res):
            out[(kind, n)] = buf.reshape(shape)

    rep_shapes = [wl[n].shape for n in _REPLICATED]
    grep = [jnp.stack(g[n]) if isinstance(g[n], list) else g[n] for n in _REPLICATED]
    (rparts,) = _exchange([_pack(grep)], True, "gather_small_grads")
    res = _reduce_adamw(rparts, _pack([wl[n] for n in _REPLICATED]), _pack([ml[n] for n in _REPLICATED]),
                        _pack([vl[n] for n in _REPLICATED]), "adamw_replicated")
    for kind, buf in zip(("grad", "delta", "new_m", "new_v"), res):
        for n, a in zip(_REPLICATED, _unpack(buf, rep_shapes)):
            out[(kind, n)] = a

    return (loss, dx[None], *[out[("grad", n)] for n in _WEIGHTS], *[out[("delta", n)] for n in _WEIGHTS],
            *[out[("new_m", n)] for n in _WEIGHTS], *[out[("new_v", n)] for n in _WEIGHTS])
```

```python
import functools

import numpy as np
import jax
import jax.numpy as jnp
from jax import lax
from jax.experimental import pallas as pl
from jax.experimental.pallas import tpu as pltpu

F32 = jnp.float32
BF16 = jnp.bfloat16

N_DEV = 8
D = 1024
D_FF = 2816
FF_SHARD = 2 * D_FF // N_DEV
DEPTH = 4
N_A = 2
N_B = 2
EPS = 1e-6
CHUNK = 64
GROUP = 256
GDN_HPS = 3
A_HEADS = 6
HD = 128
A_WIDTH = A_HEADS * HD
B_HEADS = 6
QK_NOPE = 128
QK_ROPE = 64
QK_CAT = 256
Q_LORA = 256
KV_LORA = 256
MEM_HEADS = 4
MEM_HD = 64
MEM_W = 256
N_MEM = 256
ROPE_THETA = 10000.0
ATT_SCALE = (QK_NOPE + QK_ROPE) ** -0.5
LN2 = 0.6931471805599453
Q_PRESCALE = ATT_SCALE / LN2
A_IN = 4 * A_WIDTH + 2 * A_HEADS + MEM_W
A_MQ_BLK = 4 * A_WIDTH // MEM_W
A_BA_BLK = (4 * A_WIDTH + MEM_W) // 128

ADAM_LR = 0.001
ADAM_B1 = 0.9
ADAM_B2 = 0.999
ADAM_EPS = 1e-08
ADAM_WD = 0.01
ADAM_STEP = 10

VMEM_LIMIT = 56 * 1024 * 1024
TM = 512
TMM = 1024
ATT_BQ = 1024
PACK_W = 1024
PACK_ROWS = 32


def _cparams(sem):
    return pltpu.CompilerParams(dimension_semantics=sem, vmem_limit_bytes=VMEM_LIMIT)


_DIMS = {"nn": ((1,), (0,)), "nt": ((1,), (1,)), "tn": ((0,), (0,))}


def _dot(a, b, dims="nn"):
    return lax.dot_general(a.astype(BF16), b.astype(BF16), (_DIMS[dims], ((), ())),
                           preferred_element_type=F32)


def _matmul(a, b, *, dims, grid, a_spec, b_spec, o_spec, out_shape, name, scale=1.0,
            res=None, res_spec=None):
    nk = grid[-1]
    kax = len(grid) - 1
    acc_shape = tuple(s for s in o_spec.block_shape if s is not None)

    def body(*refs):
        if res is None:
            a_ref, b_ref, o_ref, acc = refs
            r_ref = None
        else:
            a_ref, b_ref, r_ref, o_ref, acc = refs
        k = pl.program_id(kax)

        @pl.when(k == 0)
        def _():
            acc[...] = jnp.zeros_like(acc)

        acc[...] += _dot(a_ref[...], b_ref[...], dims)

        @pl.when(k == nk - 1)
        def _():
            y = acc[...] * scale
            if r_ref is not None:
                y = y + r_ref[...].astype(F32)
            o_ref[...] = y.astype(o_ref.dtype)

    args = [a, b] + ([res] if res is not None else [])
    specs = [a_spec, b_spec] + ([res_spec] if res is not None else [])
    sem = ("parallel",) * kax + ("arbitrary",)
    return pl.pallas_call(
        body, out_shape=out_shape, grid=grid, in_specs=specs, out_specs=o_spec,
        scratch_shapes=[pltpu.VMEM(acc_shape, F32)], name=name, compiler_params=_cparams(sem))(*args)


def _tile(n, cap):
    if n <= cap:
        return n
    t = cap - cap % 128
    while t >= 128:
        if n % t == 0:
            return t
        t -= 128
    raise ValueError(f"no tile for {n}")


def _mm(a, b, dims, out_dtype, name, scale=1.0, res=None):
    if dims == "tn":
        kk, m = a.shape
        n = b.shape[1]
        tk, tn = _tile(kk, TMM), _tile(n, 1152)
        return _matmul(a, b, dims=dims, grid=(1, n // tn, kk // tk),
                       a_spec=pl.BlockSpec((tk, m), lambda i, j, k: (k, 0)),
                       b_spec=pl.BlockSpec((tk, tn), lambda i, j, k: (k, j)),
                       o_spec=pl.BlockSpec((m, tn), lambda i, j, k: (0, j)),
                       out_shape=jax.ShapeDtypeStruct((m, n), out_dtype), name=name, scale=scale)
    m, kk = a.shape
    n = b.shape[1] if dims == "nn" else b.shape[0]
    tm, tn, tk = _tile(m, TMM), _tile(n, 1152), _tile(kk, 1536)
    if dims == "nn":
        b_spec = pl.BlockSpec((tk, tn), lambda i, j, k: (k, j))
    else:
        b_spec = pl.BlockSpec((tn, tk), lambda i, j, k: (j, k))
    o_spec = pl.BlockSpec((tm, tn), lambda i, j, k: (i, j))
    return _matmul(a, b, dims=dims, grid=(m // tm, n // tn, kk // tk),
                   a_spec=pl.BlockSpec((tm, tk), lambda i, j, k: (i, k)), b_spec=b_spec, o_spec=o_spec,
                   out_shape=jax.ShapeDtypeStruct((m, n), out_dtype), name=name, scale=scale,
                   res=res, res_spec=o_spec if res is not None else None)


def _rowcall(fn, args, in_specs, out_shapes, out_specs, grid, name, n_acc=0):
    n_in, n_out = len(args), len(out_shapes)

    def body(*refs):
        outs = fn(*[r[...] for r in refs[:n_in]])
        if not isinstance(outs, (tuple, list)):
            outs = (outs,)
        first = pl.program_id(0) == 0
        for ax in range(1, len(grid)):
            first = jnp.logical_and(first, pl.program_id(ax) == 0)
        for idx, (o_ref, val) in enumerate(zip(refs[n_in:], outs)):
            if idx >= n_out - n_acc:
                @pl.when(first)
                def _(o_ref=o_ref):
                    o_ref[...] = jnp.zeros_like(o_ref)

                o_ref[...] += val.astype(o_ref.dtype)
            else:
                o_ref[...] = val.astype(o_ref.dtype)

    sem = (("arbitrary",) if n_acc else ("parallel",)) * len(grid)
    res = pl.pallas_call(body, out_shape=tuple(out_shapes), grid=grid, in_specs=list(in_specs),
                         out_specs=tuple(out_specs), name=name, compiler_params=_cparams(sem))(*args)
    return res


def _vjp_fn(fn, n_in, wrt):
    def bwd(*blocks):
        ins = [b.astype(F32) for b in blocks[:n_in]]
        cts = [c.astype(F32) for c in blocks[n_in:]]
        outs, vjp = jax.vjp(fn, *ins)
        if isinstance(outs, (tuple, list)):
            grads = vjp(tuple(cts))
        else:
            grads = vjp(cts[0])
        return tuple(grads[i] for i in wrt)
    return bwd


def _sds(shape, dtype):
    return jax.ShapeDtypeStruct(tuple(shape), dtype)


def _rows(tm, w, col=0):
    return pl.BlockSpec((tm, w), lambda i, *_: (i, col))


def _shared(shape):
    nd = len(shape)
    return pl.BlockSpec(tuple(shape), lambda *_: (0,) * nd)


def _rms(x, g):
    return x * lax.rsqrt(jnp.mean(x * x, axis=-1, keepdims=True) + EPS) * g


def _silu(x):
    return x * jax.nn.sigmoid(x)


def _swiglu(g, u):
    return _silu(g) * u


def _gdn_prep(q, k, v):
    q, k, v = _silu(q), _silu(k), _silu(v)
    q = q * lax.rsqrt(jnp.sum(q * q, axis=-1, keepdims=True) + EPS) * (HD ** -0.5)
    k = k * lax.rsqrt(jnp.sum(k * k, axis=-1, keepdims=True) + EPS)
    return q, k, v


def _gates(ba, a_log, dt_bias):
    lane = lax.broadcasted_iota(jnp.int32, ba.shape, 1)
    beta = jax.nn.sigmoid(ba)
    z = ba + dt_bias
    softplus = jnp.maximum(z, 0.0) + jnp.log(1.0 + jnp.exp(-jnp.abs(z)))
    g = -jnp.exp(a_log) * softplus
    return jnp.where(lane < A_HEADS, beta, jnp.where(lane < 2 * A_HEADS, g, 0.0))


def _outnorm_gate(o, gate, gain):
    return _rms(o, gain) * _silu(gate)


def _memattn(q, k, v):
    lane = lax.shift_right_logical(lax.broadcasted_iota(jnp.int32, (1, MEM_W), 1), 6)
    out = jnp.zeros(q.shape, F32)
    for h in range(MEM_HEADS):
        mh = (lane == h).astype(F32)
        s = _dot(q * mh, k, "nt") * (MEM_HD ** -0.5)
        s = s - lax.stop_gradient(jnp.max(s, axis=-1, keepdims=True))
        p = jnp.exp(s)
        p = p / jnp.sum(p, axis=-1, keepdims=True)
        out = out + _dot(p, v * mh)
    return out


def _rope_mix(a, a_sw, c, s):
    return (a * c + a_sw * s) * Q_PRESCALE


def _kcat(kn, kr, kr_sw, c, s):
    return kn + kr * c + kr_sw * s


def _gdn_local(q, k, v, beta, gcol, grow):
    n = GROUP
    ri = lax.broadcasted_iota(jnp.int32, (n, n), 0)
    ci = lax.broadcasted_iota(jnp.int32, (n, n), 1)
    same = lax.shift_right_logical(ri, 6) == lax.shift_right_logical(ci, 6)
    lower = jnp.logical_and(same, ci <= ri)
    strict = jnp.logical_and(same, ci < ri)
    gc_col = jnp.sum(lower.astype(F32) * grow, axis=1, keepdims=True)
    gc_row = jnp.sum(jnp.logical_and(same, ri <= ci).astype(F32) * gcol, axis=0, keepdims=True)
    glast = jnp.sum(same.astype(F32) * grow, axis=1, keepdims=True)
    decay = jnp.where(lower, jnp.exp(jnp.where(lower, gc_col - gc_row, 0.0)), 0.0)
    kb = k * beta
    nmat = -jnp.where(strict, _dot(kb, k, "nt") * decay, 0.0)
    pinv = (ri == ci).astype(F32) + nmat
    npow = nmat
    for _ in range(5):
        npow = _dot(npow, npow)
        pinv = pinv + _dot(pinv, npow)
    e_gc = jnp.exp(gc_col)
    u = _dot(pinv, v * beta)
    w = _dot(pinv, kb * e_gc)
    qk = _dot(q, k, "nt") * decay
    fold = (jnp.bitwise_and(lax.broadcasted_iota(jnp.int32, (n, CHUNK), 0), CHUNK - 1)
            == lax.broadcasted_iota(jnp.int32, (n, CHUNK), 1)).astype(F32)
    qk_c = _dot(qk, fold)
    q_dec = q * e_gc
    k_dec = k * jnp.exp(glast - gc_col)
    dmat = jnp.exp(glast) * jnp.ones((1, HD), F32)
    return u, w, q_dec, k_dec, qk_c, dmat


def _gdn_step(s, w_c, u_c, qd_c, kd_c, qk_c, d_c):
    v_new = u_c - _dot(w_c, s)
    out = _dot(qd_c, s) + _dot(qk_c, v_new)
    d_row = jnp.mean(d_c, axis=0, keepdims=True)
    s_new = s * d_row + _dot(kd_c, v_new, "tn")
    return s_new, out


def _rmsnorm_fwd(x, gain, name, col=0, width=None):
    t = x.shape[0]
    w = width or x.shape[1]
    (n,) = _rowcall(_rms, [x, gain.reshape(1, w)], [_rows(TM, w, col), _shared((1, w))],
                    [_sds((t, w), BF16)], [_rows(TM, w)], (t // TM,), name)
    return n


def _rmsnorm_bwd(x, gain, dn, dres, name):
    t, w = x.shape
    fn = _vjp_fn(_rms, 2, (0, 1))

    def bwd(xb, gb, dnb, drb):
        dx, dg = fn(xb, gb, dnb)
        return dx + drb, dg

    dx, dg = _rowcall(bwd, [x, gain.reshape(1, w), dn, dres],
                      [_rows(TM, w), _shared((1, w)), _rows(TM, w), _rows(TM, w)],
                      [_sds((t, w), F32), _sds((1, w), F32)], [_rows(TM, w), _shared((1, w))],
                      (t // TM,), name, n_acc=1)
    return dx, dg[0]


def _ffn_fwd(x, gain, wgu8, wd4, tag):
    t = x.shape[0]
    nt = t // TMM
    n = _rmsnorm_fwd(x, gain, tag + "_norm")
    gu = _matmul(n, wgu8, dims="nn", grid=(N_DEV, nt, 1),
                 a_spec=pl.BlockSpec((TMM, D), lambda j, i, k: (i, 0)),
                 b_spec=pl.BlockSpec((None, D, FF_SHARD), lambda j, i, k: (j, 0, 0)),
                 o_spec=pl.BlockSpec((None, TMM, FF_SHARD), lambda j, i, k: (j, i, 0)),
                 out_shape=_sds((N_DEV, t, FF_SHARD), BF16), name=tag + "_gu")
    gu = gu.reshape(2, 4, t, FF_SHARD)
    gu_spec = pl.BlockSpec((2, None, TM, FF_SHARD), lambda i, j: (0, j, i, 0))
    h_spec = pl.BlockSpec((None, TM, FF_SHARD), lambda i, j: (j, i, 0))
    (h,) = _rowcall(lambda b: _swiglu(b[0].astype(F32), b[1].astype(F32)), [gu], [gu_spec], [_sds((4, t, FF_SHARD), BF16)], [h_spec],
                    (t // TM, 4), tag + "_act")
    y = _matmul(h, wd4, dims="nn", grid=(nt, 1, 4),
                a_spec=pl.BlockSpec((None, TMM, FF_SHARD), lambda i, j, k: (k, i, 0)),
                b_spec=pl.BlockSpec((None, FF_SHARD, D), lambda i, j, k: (k, 0, 0)),
                o_spec=pl.BlockSpec((TMM, D), lambda i, j, k: (i, 0)),
                out_shape=_sds((t, D), F32), name=tag + "_down", scale=0.5,
                res=x, res_spec=pl.BlockSpec((TMM, D), lambda i, j, k: (i, 0)))
    return y, (x, n, gu, h)


def _ffn_bwd(d, saved, gain, wgu8, wd4, tag):
    x, n, gu, h = saved
    t = x.shape[0]
    nt = t // TMM
    dh = _matmul(d, wd4, dims="nt", grid=(4, nt, 1),
                 a_spec=pl.BlockSpec((TMM, D), lambda j, i, k: (i, 0)),
                 b_spec=pl.BlockSpec((None, FF_SHARD, D), lambda j, i, k: (j, 0, 0)),
                 o_spec=pl.BlockSpec((None, TMM, FF_SHARD), lambda j, i, k: (j, i, 0)),
                 out_shape=_sds((4, t, FF_SHARD), BF16), name=tag + "_dh", scale=0.5)
    dwd4 = _matmul(h, d, dims="tn", grid=(4, 1, nt),
                   a_spec=pl.BlockSpec((None, TMM, FF_SHARD), lambda j, i, k: (j, k, 0)),
                   b_spec=pl.BlockSpec((TMM, D), lambda j, i, k: (k, 0)),
                   o_spec=pl.BlockSpec((None, FF_SHARD, D), lambda j, i, k: (j, 0, 0)),
                   out_shape=_sds((4, FF_SHARD, D), F32), name=tag + "_dwd", scale=0.5)
    gu_spec = pl.BlockSpec((2, None, TM, FF_SHARD), lambda i, j: (0, j, i, 0))
    h_spec = pl.BlockSpec((None, TM, FF_SHARD), lambda i, j: (j, i, 0))
    def dact(b, dhb):
        return jnp.stack(_vjp_fn(_swiglu, 2, (0, 1))(b[0], b[1], dhb))

    (dgu,) = _rowcall(dact, [gu, dh], [gu_spec, h_spec],
                      [_sds((2, 4, t, FF_SHARD), BF16)], [gu_spec], (t // TM, 4), tag + "_dact")
    dgu = dgu.reshape(N_DEV, t, FF_SHARD)
    dwgu8 = _matmul(n, dgu, dims="tn", grid=(N_DEV, 1, nt),
                    a_spec=pl.BlockSpec((TMM, D), lambda j, i, k: (k, 0)),
                    b_spec=pl.BlockSpec((None, TMM, FF_SHARD), lambda j, i, k: (j, k, 0)),
                    o_spec=pl.BlockSpec((None, D, FF_SHARD), lambda j, i, k: (j, 0, 0)),
                    out_shape=_sds((N_DEV, D, FF_SHARD), F32), name=tag + "_dwgu")
    dn = _matmul(dgu, wgu8, dims="nt", grid=(nt, 1, N_DEV),
                 a_spec=pl.BlockSpec((None, TMM, FF_SHARD), lambda i, j, k: (k, i, 0)),
                 b_spec=pl.BlockSpec((None, D, FF_SHARD), lambda i, j, k: (k, 0, 0)),
                 o_spec=pl.BlockSpec((TMM, D), lambda i, j, k: (i, 0)),
                 out_shape=_sds((t, D), BF16), name=tag + "_dn")
    dx, dgain = _rmsnorm_bwd(x, gain, dn, d, tag + "_dnorm")
    return dx, dgain, dwgu8, dwd4


CONV_TC = 768
CONV_K = 4


def _conv_fwd(ha, w, name):
    t = ha.shape[0]
    nb = TM // 8

    def body(prev_ref, cur_ref, w_ref, o_ref):
        i = pl.program_id(0)
        cur = cur_ref[...].astype(F32)
        prev = prev_ref[...].astype(F32) * (i > 0).astype(F32)
        ext = jnp.concatenate([prev, cur], axis=0)
        wv = w_ref[...]
        acc = cur * wv[3:4]
        for k in range(1, CONV_K):
            acc = acc + pltpu.roll(ext, k, axis=0)[8:] * wv[3 - k:4 - k]
        o_ref[...] = acc.astype(o_ref.dtype)

    return pl.pallas_call(
        body, out_shape=_sds((3, t, CONV_TC), BF16), grid=(t // TM, 3),
        in_specs=[pl.BlockSpec((8, CONV_TC), lambda i, c: (jnp.maximum(i * nb - 1, 0), c)),
                  pl.BlockSpec((TM, CONV_TC), lambda i, c: (i, c)),
                  pl.BlockSpec((CONV_K, CONV_TC), lambda i, c: (0, c))],
        out_specs=pl.BlockSpec((None, TM, CONV_TC), lambda i, c: (c, i, 0)),
        name=name, compiler_params=_cparams(("parallel", "parallel")))(ha, ha, w)


def _conv_bwd(ha, w, dy3, name):
    t = ha.shape[0]
    nb = TM // 8
    nt = t // TM

    def body(prev_ref, cur_ref, dy_ref, nxt_ref, w_ref, dx_ref, dw_ref):
        i = pl.program_id(1)
        cur = cur_ref[...].astype(F32)
        prev = prev_ref[...].astype(F32) * (i > 0).astype(F32)
        ext = jnp.concatenate([prev, cur], axis=0)
        dy = dy_ref[...].astype(F32)
        nxt = nxt_ref[...].astype(F32) * (i < nt - 1).astype(F32)
        dext = jnp.concatenate([dy, nxt], axis=0)
        wv = w_ref[...]
        dx = dy * wv[3:4]
        dws = [None] * CONV_K
        dws[3] = jnp.sum(dy * cur, axis=0, keepdims=True)
        for k in range(1, CONV_K):
            dx = dx + pltpu.roll(dext, TM + 8 - k, axis=0)[:TM] * wv[3 - k:4 - k]
            dws[3 - k] = jnp.sum(dy * pltpu.roll(ext, k, axis=0)[8:], axis=0, keepdims=True)
        dx_ref[...] = dx.astype(dx_ref.dtype)

        @pl.when(i == 0)
        def _():
            dw_ref[...] = jnp.zeros_like(dw_ref)

        dw_ref[...] += jnp.concatenate(dws, axis=0)

    return pl.pallas_call(
        body, out_shape=(_sds((t, 3 * CONV_TC), BF16), _sds((CONV_K, 3 * CONV_TC), F32)), grid=(3, nt),
        in_specs=[pl.BlockSpec((8, CONV_TC), lambda c, i: (jnp.maximum(i * nb - 1, 0), c)),
                  pl.BlockSpec((TM, CONV_TC), lambda c, i: (i, c)),
                  pl.BlockSpec((None, TM, CONV_TC), lambda c, i: (c, i, 0)),
                  pl.BlockSpec((None, 8, CONV_TC), lambda c, i: (c, jnp.minimum((i + 1) * nb, t // 8 - 1), 0)),
                  pl.BlockSpec((CONV_K, CONV_TC), lambda c, i: (0, c))],
        out_specs=(pl.BlockSpec((TM, CONV_TC), lambda c, i: (i, c)),
                   pl.BlockSpec((CONV_K, CONV_TC), lambda c, i: (0, c))),
        name=name, compiler_params=_cparams(("parallel", "arbitrary")))(ha, ha, dy3, dy3, w)


def _gdn_specs(t, rev):
    ng = t // GROUP

    def gi(g):
        return ng - 1 - g if rev else g

    qkv = pl.BlockSpec((3, GROUP, GDN_HPS * HD), lambda h, g: (0, gi(g), h))
    bg = pl.BlockSpec((GROUP, 128), lambda h, g: (gi(g), 0))
    dbg = pl.BlockSpec((GDN_HPS, GROUP, 128), lambda h, g: (h, gi(g), 0))
    o = pl.BlockSpec((GROUP, GDN_HPS * HD), lambda h, g: (gi(g), h))
    st = pl.BlockSpec((GDN_HPS, None, HD, HD), lambda h, g: (h, gi(g), 0, 0))
    return qkv, bg, dbg, o, st


def _head_qkv(qkv_ref, j):
    sl = slice(j * HD, (j + 1) * HD)
    return qkv_ref[0, :, sl].astype(F32), qkv_ref[1, :, sl].astype(F32), qkv_ref[2, :, sl].astype(F32)


def _head_gates(bg, h):
    lane = lax.broadcasted_iota(jnp.int32, (1, 128), 1)
    beta = jnp.sum(jnp.where(lane == h, bg, 0.0), axis=1, keepdims=True)
    gcol = jnp.sum(jnp.where(lane == h + A_HEADS, bg, 0.0), axis=1, keepdims=True)
    return beta, gcol, _col_to_row(gcol)


def _gdn_fwd(qkv3, bg, name):
    t = qkv3.shape[1]
    ng = t // GROUP
    qkv_s, bg_s, _, o_s, st_s = _gdn_specs(t, False)

    def body(qkv_ref, bg_ref, o_ref, st_ref, s_scr):
        @pl.when(pl.program_id(1) == 0)
        def _():
            s_scr[...] = jnp.zeros_like(s_scr)

        st_ref[...] = s_scr[...]
        bgv = bg_ref[...]
        loc = [_gdn_local(*_head_qkv(qkv_ref, j), *_head_gates(bgv, pl.program_id(0) * GDN_HPS + j))
               for j in range(GDN_HPS)]
        s = [s_scr[j] for j in range(GDN_HPS)]
        for a in range(GROUP // CHUNK):
            sl = slice(a * CHUNK, (a + 1) * CHUNK)
            for j in range(GDN_HPS):
                u, w, qd, kd, qkc, dm = loc[j]
                s[j], out = _gdn_step(s[j], w[sl], u[sl], qd[sl], kd[sl], qkc[sl], dm[sl])
                o_ref[sl, j * HD:(j + 1) * HD] = out.astype(o_ref.dtype)
        for j in range(GDN_HPS):
            s_scr[j] = s[j]

    return pl.pallas_call(
        body, out_shape=(_sds((t, A_WIDTH), BF16), _sds((A_HEADS, ng, HD, HD), F32)),
        grid=(A_HEADS // GDN_HPS, ng), in_specs=[qkv_s, bg_s], out_specs=(o_s, st_s),
        scratch_shapes=[pltpu.VMEM((GDN_HPS, HD, HD), F32)], name=name,
        compiler_params=_cparams(("parallel", "arbitrary")))(qkv3, bg)


def _gdn_bwd(qkv3, bg, states, do, name):
    t = qkv3.shape[1]
    ng = t // GROUP
    qkv_s, bg_s, dbg_s, o_s, st_s = _gdn_specs(t, True)
    nc = GROUP // CHUNK

    def body(qkv_ref, bg_ref, st_ref, do_ref, dqkv_ref, dbg_ref, ds_scr):
        @pl.when(pl.program_id(1) == 0)
        def _():
            ds_scr[...] = jnp.zeros_like(ds_scr)

        heads = range(GDN_HPS)
        bgv = bg_ref[...]
        hid = [pl.program_id(0) * GDN_HPS + j for j in heads]
        fw = [jax.vjp(_gdn_local, *_head_qkv(qkv_ref, j), *_head_gates(bgv, hid[j])) for j in heads]
        starts = [[None] * nc for _ in heads]
        s = [st_ref[j] for j in heads]
        for a in range(nc):
            sl = slice(a * CHUNK, (a + 1) * CHUNK)
            for j in heads:
                u, w, qd, kd, qkc, dm = fw[j][0]
                starts[j][a] = s[j]
                if a < nc - 1:
                    s[j], _ = _gdn_step(s[j], w[sl], u[sl], qd[sl], kd[sl], qkc[sl], dm[sl])
        ds = [ds_scr[j] for j in heads]
        parts = [[None] * nc for _ in heads]
        for a in reversed(range(nc)):
            sl = slice(a * CHUNK, (a + 1) * CHUNK)
            for j in heads:
                u, w, qd, kd, qkc, dm = fw[j][0]
                _, vjp_step = jax.vjp(_gdn_step, starts[j][a], w[sl], u[sl], qd[sl], kd[sl], qkc[sl], dm[sl])
                grads = vjp_step((ds[j], do_ref[sl, j * HD:(j + 1) * HD].astype(F32)))
                ds[j] = grads[0]
                parts[j][a] = grads[1:]
        lane = lax.broadcasted_iota(jnp.int32, (1, 128), 1)
        for j in heads:
            ds_scr[j] = ds[j]
            dw, du, dqd, dkd, dqk, ddm = [jnp.concatenate([parts[j][a][i] for a in range(nc)], axis=0)
                                          for i in range(6)]
            dq, dk, dv, db, dgc, dgr = fw[j][1]((du, dw, dqd, dkd, dqk, ddm))
            hs = slice(j * HD, (j + 1) * HD)
            dqkv_ref[0, :, hs] = dq.astype(dqkv_ref.dtype)
            dqkv_ref[1, :, hs] = dk.astype(dqkv_ref.dtype)
            dqkv_ref[2, :, hs] = dv.astype(dqkv_ref.dtype)
            dbg_ref[j] = (jnp.where(lane == hid[j], db, 0.0)
                          + jnp.where(lane == hid[j] + A_HEADS, dgc + _row_to_col(dgr), 0.0))

    return pl.pallas_call(
        body, out_shape=(_sds((3, t, A_WIDTH), BF16), _sds((A_HEADS, t, 128), F32)),
        grid=(A_HEADS // GDN_HPS, ng), in_specs=[qkv_s, bg_s, st_s, o_s],
        out_specs=(qkv_s, dbg_s), scratch_shapes=[pltpu.VMEM((GDN_HPS, HD, HD), F32)], name=name,
        compiler_params=_cparams(("parallel", "arbitrary")))(qkv3, bg, states, do)


NEG = -1e30


def _diag_mask(shape, q_axis):
    qi = lax.shift_right_logical(lax.broadcasted_iota(jnp.int32, shape, q_axis), 6)
    ki = lax.shift_right_logical(lax.broadcasted_iota(jnp.int32, shape, 1 - q_axis), 6)
    return ki <= qi


def _col_to_row(col):
    n = col.shape[0]
    eye = lax.broadcasted_iota(jnp.int32, (n, n), 0) == lax.broadcasted_iota(jnp.int32, (n, n), 1)
    return jnp.sum(jnp.where(eye, col, 0.0), axis=0, keepdims=True)


def _row_to_col(row):
    n = row.shape[1]
    eye = lax.broadcasted_iota(jnp.int32, (n, n), 0) == lax.broadcasted_iota(jnp.int32, (n, n), 1)
    return jnp.sum(jnp.where(eye, row, 0.0), axis=1, keepdims=True)


def _blk(ref, i):
    return ref[pl.ds(pl.multiple_of(i * ATT_BQ, ATT_BQ), ATT_BQ), :]


def _att_fwd(qc, kc, v, name):
    t = qc.shape[0]
    nq = t // ATT_BQ

    def body(q_ref, k_ref, v_ref, o_ref, lse_ref, lser_ref, m_scr, l_scr, acc_scr):
        qb = pl.program_id(1)
        q = q_ref[...]
        m_scr[...] = jnp.full_like(m_scr, NEG)
        l_scr[...] = jnp.zeros_like(l_scr)
        acc_scr[...] = jnp.zeros_like(acc_scr)

        def step(kb, diag):
            s = _dot(q, _blk(k_ref, kb), "nt")
            if diag:
                s = jnp.where(_diag_mask(s.shape, 0), s, NEG)
            m_old = m_scr[...]
            m_new = jnp.maximum(m_old, jnp.max(s, axis=1, keepdims=True))
            alpha = jnp.exp2(m_old - m_new)
            p = jnp.exp2(s - m_new)
            l_scr[...] = alpha * l_scr[...] + jnp.sum(p, axis=1, keepdims=True)
            acc_scr[...] = alpha * acc_scr[...] + _dot(p, _blk(v_ref, kb))
            m_scr[...] = m_new

        def loop_body(kb, carry):
            step(kb, False)
            return carry

        lax.fori_loop(0, qb, loop_body, 0)
        step(qb, True)
        o_ref[...] = (acc_scr[...] / l_scr[...]).astype(o_ref.dtype)
        lse = m_scr[...] + jnp.log2(l_scr[...])
        lse_ref[...] = lse
        lser_ref[...] = _col_to_row(lse)

    return pl.pallas_call(
        body, out_shape=(_sds((t, B_HEADS * HD), BF16), _sds((B_HEADS, t, 1), F32),
                         _sds((B_HEADS, nq, 1, ATT_BQ), F32)), grid=(B_HEADS, nq),
        in_specs=[pl.BlockSpec((ATT_BQ, QK_CAT), lambda h, i: (i, h)),
                  pl.BlockSpec((t, QK_CAT), lambda h, i: (0, h)), pl.BlockSpec((t, HD), lambda h, i: (0, h))],
        out_specs=(pl.BlockSpec((ATT_BQ, HD), lambda h, i: (i, h)),
                   pl.BlockSpec((None, ATT_BQ, 1), lambda h, i: (h, i, 0)),
                   pl.BlockSpec((None, None, 1, ATT_BQ), lambda h, i: (h, i, 0, 0))),
        scratch_shapes=[pltpu.VMEM((ATT_BQ, 1), F32), pltpu.VMEM((ATT_BQ, 1), F32), pltpu.VMEM((ATT_BQ, HD), F32)],
        name=name, compiler_params=_cparams(("parallel", "arbitrary")))(qc, kc, v)


def _att_bwd(qc, kc, v, o, lse, lse_row, do, name):
    t = qc.shape[0]
    nq = t // ATT_BQ

    def delta_fn(ob, dob):
        dl = jnp.sum(ob.astype(F32) * dob.astype(F32), axis=1, keepdims=True)
        return dl, _col_to_row(dl)

    delta, delta_row = _rowcall(
        delta_fn, [o, do], [pl.BlockSpec((ATT_BQ, HD), lambda i, h: (i, h))] * 2,
        [_sds((B_HEADS, t, 1), F32), _sds((B_HEADS, nq, 1, ATT_BQ), F32)],
        [pl.BlockSpec((None, ATT_BQ, 1), lambda i, h: (h, i, 0)),
         pl.BlockSpec((None, None, 1, ATT_BQ), lambda i, h: (h, i, 0, 0))], (nq, B_HEADS), name + "_delta")

    def dq_body(q_ref, k_ref, v_ref, do_ref, lse_ref, dl_ref, dq_ref, acc):
        qb = pl.program_id(1)
        q, dob, lse_b, dl_b = q_ref[...], do_ref[...], lse_ref[...], dl_ref[...]
        acc[...] = jnp.zeros_like(acc)

        def step(kb, diag):
            k = _blk(k_ref, kb)
            s = _dot(q, k, "nt")
            if diag:
                s = jnp.where(_diag_mask(s.shape, 0), s, NEG)
            p = jnp.exp2(s - lse_b)
            ds = p * (_dot(dob, _blk(v_ref, kb), "nt") - dl_b)
            acc[...] += _dot(ds, k)

        def loop_body(kb, carry):
            step(kb, False)
            return carry

        lax.fori_loop(0, qb, loop_body, 0)
        step(qb, True)
        dq_ref[...] = (acc[...] * ATT_SCALE).astype(dq_ref.dtype)

    qmap = lambda h, i: (i, h)
    colq = pl.BlockSpec((None, ATT_BQ, 1), lambda h, i: (h, i, 0))
    dq = pl.pallas_call(
        dq_body, out_shape=_sds((t, B_HEADS * QK_CAT), BF16), grid=(B_HEADS, nq),
        in_specs=[pl.BlockSpec((ATT_BQ, QK_CAT), qmap), pl.BlockSpec((t, QK_CAT), lambda h, i: (0, h)),
                  pl.BlockSpec((t, HD), lambda h, i: (0, h)), pl.BlockSpec((ATT_BQ, HD), qmap), colq, colq],
        out_specs=pl.BlockSpec((ATT_BQ, QK_CAT), qmap),
        scratch_shapes=[pltpu.VMEM((ATT_BQ, QK_CAT), F32)], name=name + "_dq",
        compiler_params=_cparams(("parallel", "arbitrary")))(qc, kc, v, do, lse, delta)

    def dkv_body(k_ref, v_ref, q_ref, do_ref, lser_ref, dlr_ref, dk_ref, dv_ref, dk_acc, dv_acc):
        kb = pl.program_id(1)
        k, vv = k_ref[...], v_ref[...]
        dk_acc[...] = jnp.zeros_like(dk_acc)
        dv_acc[...] = jnp.zeros_like(dv_acc)

        def step(qb, diag):
            q, dob = _blk(q_ref, qb), _blk(do_ref, qb)
            st = _dot(k, q, "nt")
            if diag:
                st = jnp.where(_diag_mask(st.shape, 1), st, NEG)
            pt = jnp.exp2(st - lser_ref[qb])
            dst = pt * (_dot(vv, dob, "nt") - dlr_ref[qb])
            dv_acc[...] += _dot(pt, dob)
            dk_acc[...] += _dot(dst, q)

        def loop_body(qb, carry):
            step(qb, False)
            return carry

        step(kb, True)
        lax.fori_loop(kb + 1, nq, loop_body, 0)
        dk_ref[...] = (dk_acc[...] * LN2).astype(dk_ref.dtype)
        dv_ref[...] = dv_acc[...].astype(dv_ref.dtype)

    kmap = lambda h, j: (j, h)
    rowq = pl.BlockSpec((None, nq, 1, ATT_BQ), lambda h, j: (h, 0, 0, 0))
    dk, dv = pl.pallas_call(
        dkv_body, out_shape=(_sds((t, B_HEADS * QK_CAT), BF16), _sds((t, B_HEADS * HD), BF16)),
        grid=(B_HEADS, nq),
        in_specs=[pl.BlockSpec((ATT_BQ, QK_CAT), kmap), pl.BlockSpec((ATT_BQ, HD), kmap),
                  pl.BlockSpec((t, QK_CAT), lambda h, j: (0, h)), pl.BlockSpec((t, HD), lambda h, j: (0, h)),
                  rowq, rowq],
        out_specs=(pl.BlockSpec((ATT_BQ, QK_CAT), kmap), pl.BlockSpec((ATT_BQ, HD), kmap)),
        scratch_shapes=[pltpu.VMEM((ATT_BQ, QK_CAT), F32), pltpu.VMEM((ATT_BQ, HD), F32)], name=name + "_dkv",
        compiler_params=_cparams(("parallel", "arbitrary")))(kc, v, qc, do, lse_row, delta_row)
    return dq, dk, dv


def _mem_fwd(hx, col, mkv, name):
    t = hx.shape[0]
    (o,) = _rowcall(_memattn, [hx, mkv, mkv],
                    [_rows(TM, MEM_W, col), pl.BlockSpec((N_MEM, MEM_W), lambda i: (0, 0)),
                     pl.BlockSpec((N_MEM, MEM_W), lambda i: (0, 1))],
                    [_sds((t, MEM_W), BF16)], [_rows(TM, MEM_W)], (t // TM,), name)
    return o


def _mem_bwd(hx, col, mkv, do, do_col, name):
    t = hx.shape[0]
    dq, dk, dv = _rowcall(_vjp_fn(_memattn, 3, (0, 1, 2)), [hx, mkv, mkv, do],
                          [_rows(TM, MEM_W, col), pl.BlockSpec((N_MEM, MEM_W), lambda i: (0, 0)),
                           pl.BlockSpec((N_MEM, MEM_W), lambda i: (0, 1)), _rows(TM, MEM_W, do_col)],
                          [_sds((t, MEM_W), BF16), _sds((N_MEM, MEM_W), F32), _sds((N_MEM, MEM_W), F32)],
                          [_rows(TM, MEM_W), _shared((N_MEM, MEM_W)), _shared((N_MEM, MEM_W))],
                          (t // TM,), name, n_acc=2)
    return dq, jnp.concatenate([dk, dv], axis=1)


def _a_in_ext(w):
    nb = 4 * A_WIDTH
    ba = jnp.pad(w[:, nb:nb + 2 * A_HEADS], ((0, 0), (0, 128 - 2 * A_HEADS)))
    return jnp.concatenate([w[:, :nb], w[:, nb + 2 * A_HEADS:], ba], axis=1)


def _swap_halves(w):
    return jnp.concatenate([w[..., QK_ROPE // 2:], w[..., :QK_ROPE // 2]], axis=-1)


def _uq_ext(w):
    w = w.reshape(Q_LORA, B_HEADS, QK_NOPE + QK_ROPE)
    nope, rope = w[..., :QK_NOPE], w[..., QK_NOPE:]
    z64 = jnp.zeros((Q_LORA, B_HEADS, QK_CAT - QK_NOPE - QK_ROPE), w.dtype)
    z128 = jnp.zeros((Q_LORA, B_HEADS, QK_NOPE), w.dtype)
    a = jnp.concatenate([nope, rope, z64], axis=-1).reshape(Q_LORA, B_HEADS * QK_CAT)
    b = jnp.concatenate([z128, _swap_halves(rope), z64], axis=-1).reshape(Q_LORA, B_HEADS * QK_CAT)
    return jnp.concatenate([a, b], axis=1)


def _dkv_ext(w):
    ckv, kr = w[:, :KV_LORA], w[:, KV_LORA:]
    z128 = jnp.zeros((D, QK_NOPE), w.dtype)
    z64 = jnp.zeros((D, QK_CAT - QK_NOPE - QK_ROPE), w.dtype)
    return jnp.concatenate([ckv, z128, kr, z64, z128, _swap_halves(kr), z64], axis=1)


def _ukv_ext(w):
    w = w.reshape(KV_LORA, B_HEADS, QK_NOPE + HD)
    kn, vv = w[..., :QK_NOPE], w[..., QK_NOPE:]
    z = jnp.zeros((KV_LORA, B_HEADS, QK_CAT - QK_NOPE), w.dtype)
    a = jnp.concatenate([kn, z], axis=-1).reshape(KV_LORA, B_HEADS * QK_CAT)
    return jnp.concatenate([a, vv.reshape(KV_LORA, B_HEADS * HD)], axis=1)


def _ext_and_back(fn, w):
    ext, back = jax.vjp(fn, w.astype(F32))
    return ext.astype(BF16), lambda g: back(g.astype(F32))[0]


def _rope_tables(pos_col):
    t = pos_col.shape[0]
    inv = (ROPE_THETA ** (-np.arange(0, QK_ROPE, 2, dtype=np.float32) / QK_ROPE)).astype(np.float32)
    inv_row = np.zeros((1, QK_CAT), np.float32)
    inv_row[0, QK_NOPE:QK_NOPE + QK_ROPE] = np.concatenate([inv, inv])
    sign = np.zeros((1, QK_CAT), np.float32)
    sign[0, QK_NOPE:QK_NOPE + QK_ROPE // 2] = -1.0
    sign[0, QK_NOPE + QK_ROPE // 2:QK_NOPE + QK_ROPE] = 1.0
    is_rope = np.abs(sign)
    is_nope = np.zeros((1, QK_CAT), np.float32)
    is_nope[0, :QK_NOPE] = 1.0

    def fn(p, inv_b, sign_b, rope_b, nope_b):
        ang = p.astype(F32) * inv_b
        return jnp.cos(ang) * rope_b + nope_b, jnp.sin(ang) * sign_b

    consts = [jnp.asarray(a) for a in (inv_row, sign, is_rope, is_nope)]
    return _rowcall(fn, [pos_col] + consts, [_rows(TM, 1)] + [_shared((1, QK_CAT))] * 4,
                    [_sds((t, QK_CAT), F32)] * 2, [_rows(TM, QK_CAT)] * 2, (t // TM,), "rope_tables")


def _local_step(x, mem, pos, target, w):
    t = x.shape[0]
    g = {}
    head6 = (t // TM, A_HEADS)

    mem_n = _rmsnorm_fwd_small(mem, w["mem_norm"])
    rope_c, rope_s = _rope_tables(pos.reshape(t, 1))
    mkv = [_mm(mem_n, w["w_mem_kv"][l], "nn", BF16, f"mkv{l}") for l in range(DEPTH)]

    saved = []
    for l in range(DEPTH):
        sv = {}
        x, sv["ffn1"] = _ffn_fwd(x, w["ffn1_norm"][l], w["ffn1_w_gu"][l], w["ffn1_w_down"][l], "ffn1")
        sv["x1"] = x
        n2 = _rmsnorm_fwd(x, w["mix_norm"][l], "mix_norm")
        sv["n2"] = n2
        if l < N_A:
            ha = _mm(n2, w["a_w_in"][l], "nn", BF16, "a_in")
            yc3 = _conv_fwd(ha, w["a_conv"][l], "a_conv")
            blk3 = pl.BlockSpec((3, TM, HD), lambda i, h: (0, i, h))
            (qkv3,) = _rowcall(lambda b: jnp.stack(_gdn_prep(b[0].astype(F32), b[1].astype(F32), b[2].astype(F32))),
                               [yc3], [blk3], [_sds((3, t, A_WIDTH), BF16)], [blk3], head6, "a_prep")
            (bg,) = _rowcall(_gates, [ha, _pad128(w["a_A_log"][l], A_HEADS), _pad128(w["a_dt_bias"][l], A_HEADS)],
                             [_rows(TM, 128, A_BA_BLK), _shared((1, 128)), _shared((1, 128))],
                             [_sds((t, 128), F32)], [_rows(TM, 128)], (t // TM,), "a_gates")
            o_gdn, states = _gdn_fwd(qkv3, bg, "a_gdn")
            (o_a,) = _rowcall(_outnorm_gate, [o_gdn, ha, w["a_out_norm"][l].reshape(1, HD)],
                              [pl.BlockSpec((TM, HD), lambda i, h: (i, h)),
                               pl.BlockSpec((TM, HD), lambda i, h: (i, 3 * A_HEADS + h)), _shared((1, HD))],
                              [_sds((t, A_WIDTH), BF16)], [pl.BlockSpec((TM, HD), lambda i, h: (i, h))],
                              head6, "a_outnorm")
            o_m = _mem_fwd(ha, A_MQ_BLK, mkv[l], "mem_attn_a")
            sv.update(ha=ha, yc3=yc3, qkv3=qkv3, bg=bg, states=states, o_gdn=o_gdn)
            cat = jnp.concatenate([o_a, o_m], axis=1)
        else:
            j = l - N_A
            hb = _mm(n2, w["b_w_in"][j], "nn", BF16, "b_in")
            cqn = _rmsnorm_fwd(hb, w["b_q_norm"][j], "b_qnorm", 0, Q_LORA)
            qq = _mm(cqn, w["b_w_uq"][j], "nn", BF16, "b_uq")
            (qc,) = _rowcall(_rope_mix, [qq, qq, rope_c, rope_s],
                             [pl.BlockSpec((TM, QK_CAT), lambda i, h: (i, h)),
                              pl.BlockSpec((TM, QK_CAT), lambda i, h: (i, B_HEADS + h)),
                              pl.BlockSpec((TM, QK_CAT), lambda i, h: (i, 0)),
                              pl.BlockSpec((TM, QK_CAT), lambda i, h: (i, 0))],
                             [_sds((t, B_HEADS * QK_CAT), BF16)], [pl.BlockSpec((TM, QK_CAT), lambda i, h: (i, h))],
                             head6, "b_qrope")
            o_b, lse, lse_row = _att_fwd(qc, kcat, vmla, "b_attn")
            o_m = _mem_fwd(hb, 1, mkv[l], "mem_attn_b")
            sv.update(hb=hb, cqn=cqn, qc=qc, o_b=o_b, lse=(lse, lse_row))
            cat = jnp.concatenate([o_b, o_m], axis=1)
        sv["cat"] = cat
        x = _mm(cat, w["w_out"][l], "nn", F32, "w_out", res=x)
        x, sv["ffn2"] = _ffn_fwd(x, w["ffn2_norm"][l], w["ffn2_w_gu"][l], w["ffn2_w_down"][l], "ffn2")
        saved.append(sv)
        if l == N_A - 1:
            x_kv = x
            nkv = _rmsnorm_fwd(x, w["kv_in_norm"], "kv_in_norm")
            ckr = _mm(nkv, w["w_dkv"], "nn", BF16, "kv_down")
            ckv_n = _rmsnorm_fwd(ckr, w["kv_lat_norm"], "kv_lat_norm", 0, KV_LORA)
            kvu = _mm(ckv_n, w["w_ukv"], "nn", BF16, "kv_up")
            vmla = kvu[:, B_HEADS * QK_CAT:]
            (kcat,) = _rowcall(_kcat, [kvu, ckr, ckr, rope_c, rope_s],
                               [pl.BlockSpec((TM, QK_CAT), lambda i, h: (i, h)),
                                pl.BlockSpec((TM, QK_CAT), lambda i, h: (i, 1)),
                                pl.BlockSpec((TM, QK_CAT), lambda i, h: (i, 2)),
                                pl.BlockSpec((TM, QK_CAT), lambda i, h: (i, 0)),
                                pl.BlockSpec((TM, QK_CAT), lambda i, h: (i, 0))],
                               [_sds((t, B_HEADS * QK_CAT), BF16)],
                               [pl.BlockSpec((TM, QK_CAT), lambda i, h: (i, h))], head6, "kv_cat")

    def loss_fn(xb, gb, tb):
        def f(xx, gg):
            e = _rms(xx, gg) - tb
            return 0.5 * jnp.sum(jnp.mean(e * e, axis=-1, keepdims=True), axis=0, keepdims=True)
        val, vjp = jax.vjp(f, xb, gb)
        dx, dg = vjp(jnp.ones((1, 1), F32))
        return dx, dg, val * jnp.ones((1, 128), F32)

    d, dfin, loss = _rowcall(loss_fn, [x, w["final_norm"].reshape(1, D), target],
                             [_rows(TM, D), _shared((1, D)), _rows(TM, D)],
                             [_sds((t, D), F32), _sds((1, D), F32), _sds((1, 128), F32)],
                             [_rows(TM, D), _shared((1, D)), _shared((1, 128))], (t // TM,), "loss_head", n_acc=2)
    g["final_norm"] = dfin[0]
    loss = loss[0, 0]

    for name in ("ffn1_norm", "ffn1_w_gu", "ffn1_w_down", "mix_norm", "ffn2_norm", "ffn2_w_gu", "ffn2_w_down",
                 "w_out", "w_mem_kv"):
        g[name] = [None] * DEPTH
    for name in ("a_w_in", "a_conv", "a_A_log", "a_dt_bias", "a_out_norm"):
        g[name] = [None] * N_A
    for name in ("b_w_in", "b_q_norm", "b_w_uq"):
        g[name] = [None] * N_B
    dmkv = [None] * DEPTH
    dkcat = []
    dvmla = []

    for l in reversed(range(DEPTH)):
        sv = saved[l]
        if l == N_A - 1:
            kq = pl.BlockSpec((TM, QK_CAT), lambda i, h: (i, h))
            tab = pl.BlockSpec((TM, QK_CAT), lambda i, h: (i, 0))

            def dk_fn(c, s, d0, d1):
                dk = d0.astype(F32) + d1.astype(F32)
                return dk, dk * c, dk * s

            dkn, dkr_h, dkrs_h = _rowcall(dk_fn, [rope_c, rope_s, dkcat[0], dkcat[1]], [tab, tab, kq, kq],
                                          [_sds((t, B_HEADS * QK_CAT), BF16)] + [_sds((B_HEADS, t, QK_CAT), BF16)] * 2,
                                          [kq] + [pl.BlockSpec((None, TM, QK_CAT), lambda i, h: (h, i, 0))] * 2,
                                          head6, "kv_dcat")

            def sum6(a, b):
                return jnp.sum(a.astype(F32), axis=0), jnp.sum(b.astype(F32), axis=0)

            h6 = pl.BlockSpec((B_HEADS, TM, QK_CAT), lambda i: (0, i, 0))
            dkr, dkrs = _rowcall(sum6, [dkr_h, dkrs_h], [h6, h6], [_sds((t, QK_CAT), BF16)] * 2,
                                 [_rows(TM, QK_CAT)] * 2, (t // TM,), "kv_dkr")

            def addv(a, b):
                return a.astype(F32) + b.astype(F32)

            (dv,) = _rowcall(addv, dvmla, [_rows(TM, B_HEADS * HD)] * 2, [_sds((t, B_HEADS * HD), BF16)],
                             [_rows(TM, B_HEADS * HD)], (t // TM,), "kv_dv")
            dkvu = jnp.concatenate([dkn, dv], axis=1)
            g["w_ukv"] = _mm(ckv_n, dkvu, "tn", F32, "kv_up_dw")
            dckv_n = _mm(dkvu, w["w_ukv"], "nt", BF16, "kv_up_dx")

            def lat_bwd(cb, gb, dnb):
                return _vjp_fn(_rms, 2, (0, 1))(cb, gb, dnb)

            dckv, g["kv_lat_norm"] = _rowcall(lat_bwd, [ckr, w["kv_lat_norm"].reshape(1, KV_LORA), dckv_n],
                                              [_rows(TM, KV_LORA), _shared((1, KV_LORA)), _rows(TM, KV_LORA)],
                                              [_sds((t, KV_LORA), BF16), _sds((1, KV_LORA), F32)],
                                              [_rows(TM, KV_LORA), _shared((1, KV_LORA))], (t // TM,),
                                              "kv_lat_dnorm", n_acc=1)
            g["kv_lat_norm"] = g["kv_lat_norm"][0]
            dckr = jnp.concatenate([dckv, dkr, dkrs], axis=1)
            g["w_dkv"] = _mm(nkv, dckr, "tn", F32, "kv_down_dw")
            dnkv = _mm(dckr, w["w_dkv"], "nt", BF16, "kv_down_dx")
            d, g["kv_in_norm"] = _rmsnorm_bwd(x_kv, w["kv_in_norm"], dnkv, d, "kv_in_dnorm")

        d, g["ffn2_norm"][l], g["ffn2_w_gu"][l], g["ffn2_w_down"][l] = _ffn_bwd(
            d, sv["ffn2"], w["ffn2_norm"][l], w["ffn2_w_gu"][l], w["ffn2_w_down"][l], "ffn2b")
        g["w_out"][l] = _mm(sv["cat"], d, "tn", F32, "w_out_dw")
        dcat = _mm(d, w["w_out"][l], "nt", BF16, "w_out_dx")
        if l < N_A:
            ha, yc3, qkv3, states, o_gdn = sv["ha"], sv["yc3"], sv["qkv3"], sv["states"], sv["o_gdn"]
            dmq, dmkv[l] = _mem_bwd(ha, A_MQ_BLK, mkv[l], dcat, 3, "mem_attn_a_bwd")
            hblk = pl.BlockSpec((TM, HD), lambda i, h: (i, h))
            do_gdn, dgate, dgain = _rowcall(
                _vjp_fn(_outnorm_gate, 3, (0, 1, 2)), [o_gdn, ha, w["a_out_norm"][l].reshape(1, HD), dcat],
                [hblk, pl.BlockSpec((TM, HD), lambda i, h: (i, 3 * A_HEADS + h)), _shared((1, HD)), hblk],
                [_sds((t, A_WIDTH), BF16), _sds((t, A_WIDTH), BF16), _sds((1, HD), F32)],
                [hblk, hblk, _shared((1, HD))], head6, "a_outnorm_bwd", n_acc=1)
            g["a_out_norm"][l] = dgain[0]
            dqkv3, dbg6 = _gdn_bwd(qkv3, sv["bg"], states, do_gdn, "a_gdn_bwd")

            def dgates(bab, alb, dtb, d6):
                return _vjp_fn(_gates, 3, (0, 1, 2))(bab, alb, dtb, jnp.sum(d6, axis=0))

            dba, dalog, ddt = _rowcall(
                dgates, [ha, _pad128(w["a_A_log"][l], A_HEADS), _pad128(w["a_dt_bias"][l], A_HEADS), dbg6],
                [_rows(TM, 128, A_BA_BLK), _shared((1, 128)), _shared((1, 128)),
                 pl.BlockSpec((A_HEADS, TM, 128), lambda i: (0, i, 0))],
                [_sds((t, 128), BF16), _sds((1, 128), F32), _sds((1, 128), F32)],
                [_rows(TM, 128), _shared((1, 128)), _shared((1, 128))], (t // TM,), "a_gates_bwd", n_acc=2)
            g["a_A_log"][l] = dalog[0, A_HEADS:2 * A_HEADS]
            g["a_dt_bias"][l] = ddt[0, A_HEADS:2 * A_HEADS]
            blk3 = pl.BlockSpec((3, TM, HD), lambda i, h: (0, i, h))
            def dprep(b, db):
                return jnp.stack(_vjp_fn(_gdn_prep, 3, (0, 1, 2))(b[0], b[1], b[2], db[0], db[1], db[2]))

            (dyc3,) = _rowcall(dprep, [yc3, dqkv3], [blk3, blk3],
                               [_sds((3, t, A_WIDTH), BF16)], [blk3], head6, "a_prep_bwd")
            dqkv_in, g["a_conv"][l] = _conv_bwd(ha, w["a_conv"][l], dyc3, "a_conv_bwd")
            dha = jnp.concatenate([dqkv_in, dgate, dmq, dba], axis=1)
            g["a_w_in"][l] = _mm(sv["n2"], dha, "tn", F32, "a_in_dw")
            dn2 = _mm(dha, w["a_w_in"][l], "nt", BF16, "a_in_dx")
        else:
            j = l - N_A
            hb, cqn, qc, o_b, lse = sv["hb"], sv["cqn"], sv["qc"], sv["o_b"], sv["lse"]
            dmq, dmkv[l] = _mem_bwd(hb, 1, mkv[l], dcat, 3, "mem_attn_b_bwd")
            dqc, dkc, dvv = _att_bwd(qc, kcat, vmla, o_b, lse[0], lse[1], dcat, "b_attn_bwd")
            dkcat.append(dkc)
            dvmla.append(dvv)
            kq = pl.BlockSpec((TM, QK_CAT), lambda i, h: (i, h))
            tab = pl.BlockSpec((TM, QK_CAT), lambda i, h: (i, 0))

            def dq_fn(c, s, dq):
                dq = dq.astype(F32)
                return jnp.stack([dq * c, dq * s])

            (dqq,) = _rowcall(dq_fn, [rope_c, rope_s, dqc], [tab, tab, kq],
                              [_sds((2, t, B_HEADS * QK_CAT), BF16)],
                              [pl.BlockSpec((2, TM, QK_CAT), lambda i, h: (0, i, h))], head6, "b_qrope_bwd")
            dqq = jnp.concatenate([dqq[0], dqq[1]], axis=1)
            g["b_w_uq"][j] = _mm(cqn, dqq, "tn", F32, "b_uq_dw")
            dcqn = _mm(dqq, w["b_w_uq"][j], "nt", BF16, "b_uq_dx")
            dcq, dqg = _rowcall(_vjp_fn(_rms, 2, (0, 1)), [hb, w["b_q_norm"][j].reshape(1, Q_LORA), dcqn],
                                [_rows(TM, Q_LORA), _shared((1, Q_LORA)), _rows(TM, Q_LORA)],
                                [_sds((t, Q_LORA), BF16), _sds((1, Q_LORA), F32)],
                                [_rows(TM, Q_LORA), _shared((1, Q_LORA))], (t // TM,), "b_qnorm_bwd", n_acc=1)
            g["b_q_norm"][j] = dqg[0]
            dhb = jnp.concatenate([dcq, dmq], axis=1)
            g["b_w_in"][j] = _mm(sv["n2"], dhb, "tn", F32, "b_in_dw")
            dn2 = _mm(dhb, w["b_w_in"][j], "nt", BF16, "b_in_dx")
        d, g["mix_norm"][l] = _rmsnorm_bwd(sv["x1"], w["mix_norm"][l], dn2, d, "mix_dnorm")
        d, g["ffn1_norm"][l], g["ffn1_w_gu"][l], g["ffn1_w_down"][l] = _ffn_bwd(
            d, sv["ffn1"], w["ffn1_norm"][l], w["ffn1_w_gu"][l], w["ffn1_w_down"][l], "ffn1b")

    dmem_n = None
    for l in range(DEPTH):
        g["w_mem_kv"][l] = _mm(mem_n, dmkv[l], "tn", F32, f"mkv_dw{l}")
        dmem_n = _mm(dmkv[l], w["w_mem_kv"][l], "nt", F32, f"mkv_dx{l}", res=dmem_n)
    (_, gmn) = _rowcall(_vjp_fn(_rms, 2, (0, 1)), [mem, w["mem_norm"].reshape(1, D), dmem_n],
                        [_shared((N_MEM, D)), _shared((1, D)), _shared((N_MEM, D))],
                        [_sds((N_MEM, D), F32), _sds((1, D), F32)], [_shared((N_MEM, D)), _shared((1, D))],
                        (1,), "mem_dnorm")
    g["mem_norm"] = gmn[0]
    return loss, d, g


def _pad128(v, offset):
    return jnp.pad(v.astype(F32).reshape(1, -1), ((0, 0), (offset, 128 - offset - v.shape[0])))


def _rmsnorm_fwd_small(x, gain):
    r, w = x.shape
    (n,) = _rowcall(_rms, [x, gain.reshape(1, w)], [_shared((r, w)), _shared((1, w))],
                    [_sds((r, w), BF16)], [_shared((r, w))], (1,), "mem_norm")
    return n


def _exchange(srcs, gather, name):
    n = len(srcs)
    blks = [tuple(s.shape) if gather else tuple(s.shape[1:]) for s in srcs]

    def body(*refs):
        src_refs, out_refs = refs[:n], refs[n:2 * n]
        send_sems, recv_sems, local_sems = refs[2 * n:]
        x, y, c = lax.axis_index("x"), lax.axis_index("y"), lax.axis_index("c")
        me = 4 * x + 2 * y + c
        copies = []
        for k in range(1, N_DEV):
            px = (x + (k >> 2 & 1)) % 2
            py = (y + (k >> 1 & 1)) % 2
            pc = (c + (k & 1)) % 2
            peer = 4 * px + 2 * py + pc
            for a in range(n):
                cp = pltpu.make_async_remote_copy(
                    src_ref=src_refs[a] if gather else src_refs[a].at[peer], dst_ref=out_refs[a].at[me],
                    send_sem=send_sems.at[a, k - 1], recv_sem=recv_sems.at[a, k - 1],
                    device_id=(px, py, pc), device_id_type=pl.DeviceIdType.MESH)
                cp.start()
                copies.append(cp)
        for a in range(n):
            cp = pltpu.make_async_copy(src_refs[a] if gather else src_refs[a].at[me], out_refs[a].at[me],
                                       local_sems.at[a])
            cp.start()
            copies.append(cp)
        for cp in copies:
            cp.wait()

    return pl.pallas_call(
        body, out_shape=tuple(_sds((N_DEV,) + b, s.dtype) for b, s in zip(blks, srcs)),
        in_specs=[pl.BlockSpec(memory_space=pl.ANY)] * n, out_specs=tuple([pl.BlockSpec(memory_space=pl.ANY)] * n),
        scratch_shapes=[pltpu.SemaphoreType.DMA((n, N_DEV - 1)), pltpu.SemaphoreType.DMA((n, N_DEV - 1)),
                        pltpu.SemaphoreType.DMA((n,))],
        name=name)(*srcs)


def _reduce_adamw(parts, wp, mp, vp, name):
    r, cols = wp.shape
    tr = _tile_rows(r, cols)
    c1 = 1.0 - ADAM_B1 ** ADAM_STEP
    c2 = 1.0 - ADAM_B2 ** ADAM_STEP

    def fn(pb, wb, mb, vb):
        gsum = pb[0].astype(F32)
        for j in range(1, N_DEV):
            gsum = gsum + pb[j].astype(F32)
        m_new = ADAM_B1 * mb + (1.0 - ADAM_B1) * gsum
        v_new = ADAM_B2 * vb + (1.0 - ADAM_B2) * (gsum * gsum)
        delta = -ADAM_LR * ((m_new / c1) / (jnp.sqrt(v_new / c2) + ADAM_EPS) + ADAM_WD * wb)
        return gsum, delta, m_new, v_new

    row = _rows(tr, cols)
    return _rowcall(fn, [parts, wp, mp, vp],
                    [pl.BlockSpec((N_DEV, tr, cols), lambda i: (0, i, 0)), row, row, row],
                    [_sds((r, cols), F32)] * 4, [row] * 4, (r // tr,), name)


def _tile_rows(r, cols):
    for t in (512, 256, 128, 64, 32, 16):
        if r % t == 0 and t * cols <= 160 * 1024:
            return t
    return r


def _pack(arrs):
    flat = jnp.concatenate([a.reshape(-1).astype(F32) for a in arrs])
    n = flat.shape[0]
    unit = PACK_W * PACK_ROWS
    tot = -(-n // unit) * unit
    return jnp.pad(flat, (0, tot - n)).reshape(tot // PACK_W, PACK_W)


def _unpack(buf, shapes):
    out, off = [], 0
    flat = buf.reshape(-1)
    for s in shapes:
        n = int(np.prod(s))
        out.append(flat[off:off + n].reshape(s))
        off += n
    return out


def _as2d(a):
    return a.reshape(-1, a.shape[-1])


_SHARDED = ["ffn1_w_gu", "ffn1_w_down", "ffn2_w_gu", "ffn2_w_down", "w_out", "w_mem_kv", "a_w_in", "a_conv",
            "b_w_in", "b_w_uq", "w_dkv", "w_ukv"]
_COL_SHARDED = {"ffn1_w_gu", "ffn2_w_gu", "a_conv", "b_w_uq", "w_ukv"}
_LAYERED = {"ffn1_w_gu": DEPTH, "ffn1_w_down": DEPTH, "ffn2_w_gu": DEPTH, "ffn2_w_down": DEPTH, "w_out": DEPTH,
            "w_mem_kv": DEPTH, "a_w_in": N_A, "a_conv": N_A, "b_w_in": N_B, "b_w_uq": N_B}
_REPLICATED = ["ffn1_norm", "mix_norm", "ffn2_norm", "mem_norm", "a_A_log", "a_dt_bias", "a_out_norm", "b_q_norm",
               "kv_in_norm", "kv_lat_norm", "final_norm"]
_WEIGHTS = ["ffn1_norm", "ffn1_w_gu", "ffn1_w_down", "mix_norm", "ffn2_norm", "ffn2_w_gu", "ffn2_w_down", "w_out",
            "mem_norm", "w_mem_kv", "a_w_in", "a_conv", "a_A_log", "a_dt_bias", "a_out_norm", "b_w_in", "b_q_norm",
            "b_w_uq", "kv_in_norm", "w_dkv", "kv_lat_norm", "w_ukv", "final_norm"]


def _full_from_shards(name, sh):
    if name in ("ffn1_w_gu", "ffn2_w_gu"):
        return sh
    if name in ("ffn1_w_down", "ffn2_w_down"):
        return sh.reshape(4, FF_SHARD, D)
    if name in _COL_SHARDED:
        return jnp.moveaxis(sh, 0, -2).reshape(sh.shape[1:-1] + (N_DEV * sh.shape[-1],))
    return sh.reshape((N_DEV * sh.shape[1],) + sh.shape[2:])


def _shards_from_full(name, full):
    if name in ("ffn1_w_gu", "ffn2_w_gu"):
        return full
    if name in ("ffn1_w_down", "ffn2_w_down"):
        return full.reshape(N_DEV, D_FF // N_DEV, D)
    if name in _COL_SHARDED:
        r, cc = full.shape
        return jnp.moveaxis(full.reshape(r, N_DEV, cc // N_DEV), 1, 0)
    return full.reshape((N_DEV, full.shape[0] // N_DEV) + full.shape[1:])


def kernel(x, mem, positions, ffn1_norm, ffn1_w_gu, ffn1_w_down, mix_norm, ffn2_norm, ffn2_w_gu, ffn2_w_down, w_out, mem_norm, w_mem_kv, a_w_in, a_conv, a_A_log, a_dt_bias, a_out_norm, b_w_in, b_q_norm, b_w_uq, kv_in_norm, w_dkv, kv_lat_norm, w_ukv, final_norm, loss_target, m_ffn1_norm, m_ffn1_w_gu, m_ffn1_w_down, m_mix_norm, m_ffn2_norm, m_ffn2_w_gu, m_ffn2_w_down, m_w_out, m_mem_norm, m_w_mem_kv, m_a_w_in, m_a_conv, m_a_A_log, m_a_dt_bias, m_a_out_norm, m_b_w_in, m_b_q_norm, m_b_w_uq, m_kv_in_norm, m_w_dkv, m_kv_lat_norm, m_w_ukv, m_final_norm, v_ffn1_norm, v_ffn1_w_gu, v_ffn1_w_down, v_mix_norm, v_ffn2_norm, v_ffn2_w_gu, v_ffn2_w_down, v_w_out, v_mem_norm, v_w_mem_kv, v_a_w_in, v_a_conv, v_a_A_log, v_a_dt_bias, v_a_out_norm, v_b_w_in, v_b_q_norm, v_b_w_uq, v_kv_in_norm, v_w_dkv, v_kv_lat_norm, v_w_ukv, v_final_norm):
    loc = dict(locals())
    wl = {n: loc[n] for n in _WEIGHTS}
    ml = {n: loc["m_" + n] for n in _WEIGHTS}
    vl = {n: loc["v_" + n] for n in _WEIGHTS}

    pieces = _exchange([wl[n].astype(BF16) for n in _SHARDED], True, "gather_weights")
    w = {}
    back = {}
    for n, p in zip(_SHARDED, pieces):
        if n in _LAYERED:
            w[n] = [_full_from_shards(n, p[:, l]) for l in range(_LAYERED[n])]
        else:
            w[n] = _full_from_shards(n, p)
    for n in _REPLICATED:
        w[n] = wl[n]
    for l in range(N_A):
        w["a_w_in"][l], back[("a_w_in", l)] = _ext_and_back(_a_in_ext, w["a_w_in"][l])
        w["a_conv"][l] = w["a_conv"][l].astype(F32)
    for j in range(N_B):
        w["b_w_uq"][j], back[("b_w_uq", j)] = _ext_and_back(_uq_ext, w["b_w_uq"][j])
    w["w_dkv"], back["w_dkv"] = _ext_and_back(_dkv_ext, w["w_dkv"])
    w["w_ukv"], back["w_ukv"] = _ext_and_back(_ukv_ext, w["w_ukv"])

    loss, dx, g = _local_step(x[0], mem[0], positions[0], loss_target[0], w)
    loss = lax.psum(loss, ("x", "y", "c"))

    for l in range(N_A):
        g["a_w_in"][l] = back[("a_w_in", l)](g["a_w_in"][l])
    for j in range(N_B):
        g["b_w_uq"][j] = back[("b_w_uq", j)](g["b_w_uq"][j])
    g["w_dkv"] = back["w_dkv"](g["w_dkv"])
    g["w_ukv"] = back["w_ukv"](g["w_ukv"])

    gsh = []
    for n in _SHARDED:
        if n in _LAYERED:
            gsh.append(jnp.stack([_shards_from_full(n, g[n][l]).astype(BF16) for l in range(_LAYERED[n])], axis=1))
        else:
            gsh.append(_shards_from_full(n, g[n]).astype(BF16))
    parts = _exchange(gsh, False, "scatter_grads")
    out = {}
    for n, p in zip(_SHARDED, parts):
        shape = wl[n].shape
        res = _reduce_adamw(p.reshape(N_DEV, -1, shape[-1]), _as2d(wl[n]), _as2d(ml[n]), _as2d(vl[n]), "adamw_" + n)
        for kind, buf in zip(("grad", "delta", "new_m", "new_v"), res):
            out[(kind, n)] = buf.reshape(shape)

    rep_shapes = [wl[n].shape for n in _REPLICATED]
    grep = [jnp.stack(g[n]) if isinstance(g[n], list) else g[n] for n in _REPLICATED]
    (rparts,) = _exchange([_pack(grep)], True, "gather_small_grads")
    res = _reduce_adamw(rparts, _pack([wl[n] for n in _REPLICATED]), _pack([ml[n] for n in _REPLICATED]),
                        _pack([vl[n] for n in _REPLICATED]), "adamw_replicated")
    for kind, buf in zip(("grad", "delta", "new_m", "new_v"), res):
        for n, a in zip(_REPLICATED, _unpack(buf, rep_shapes)):
            out[(kind, n)] = a

    return (loss, dx[None], *[out[("grad", n)] for n in _WEIGHTS], *[out[("delta", n)] for n in _WEIGHTS],
            *[out[("new_m", n)] for n in _WEIGHTS], *[out[("new_v", n)] for n in _WEIGHTS])
```

```python
import functools

import numpy as np
import jax
import jax.numpy as jnp
from jax import lax
from jax.experimental import pallas as pl
from jax.experimental.pallas import tpu as pltpu

F32 = jnp.float32
BF16 = jnp.bfloat16

N_DEV = 8
D = 1024
D_FF = 2816
FF_SHARD = 2 * D_FF // N_DEV
DEPTH = 4
N_A = 2
N_B = 2
EPS = 1e-6
CHUNK = 64
GROUP = 256
GDN_HPS = 3
A_HEADS = 6
HD = 128
A_WIDTH = A_HEADS * HD
B_HEADS = 6
QK_NOPE = 128
QK_ROPE = 64
QK_CAT = 256
Q_LORA = 256
KV_LORA = 256
MEM_HEADS = 4
MEM_HD = 64
MEM_W = 256
N_MEM = 256
ROPE_THETA = 10000.0
ATT_SCALE = (QK_NOPE + QK_ROPE) ** -0.5
LN2 = 0.6931471805599453
Q_PRESCALE = ATT_SCALE / LN2
A_IN = 4 * A_WIDTH + 2 * A_HEADS + MEM_W
A_MQ_BLK = 4 * A_WIDTH // MEM_W
A_BA_BLK = (4 * A_WIDTH + MEM_W) // 128

ADAM_LR = 0.001
ADAM_B1 = 0.9
ADAM_B2 = 0.999
ADAM_EPS = 1e-08
ADAM_WD = 0.01
ADAM_STEP = 10

VMEM_LIMIT = 56 * 1024 * 1024
TM = 512
TMM = 1024
ATT_BQ = 1024
PACK_W = 1024
PACK_ROWS = 32


def _cparams(sem):
    return pltpu.CompilerParams(dimension_semantics=sem, vmem_limit_bytes=VMEM_LIMIT)


_DIMS = {"nn": ((1,), (0,)), "nt": ((1,), (1,)), "tn": ((0,), (0,))}


def _dot(a, b, dims="nn"):
    return lax.dot_general(a.astype(BF16), b.astype(BF16), (_DIMS[dims], ((), ())),
                           preferred_element_type=F32)


def _matmul(a, b, *, dims, grid, a_spec, b_spec, o_spec, out_shape, name, scale=1.0,
            res=None, res_spec=None, a_fn=None, epilogue=None, acc_shape=None):
    nk = grid[-1]
    kax = len(grid) - 1
    if acc_shape is None:
        acc_shape = tuple(s for s in o_spec.block_shape if s is not None)

    def body(*refs):
        if res is None:
            a_ref, b_ref, o_ref, acc = refs
            r_ref = None
        else:
            a_ref, b_ref, r_ref, o_ref, acc = refs
        k = pl.program_id(kax)

        @pl.when(k == 0)
        def _():
            acc[...] = jnp.zeros_like(acc)

        a_blk = a_ref[...] if a_fn is None else a_fn(a_ref[...])
        acc[...] += _dot(a_blk, b_ref[...], dims)

        @pl.when(k == nk - 1)
        def _():
            y = acc[...] * scale
            if epilogue is not None:
                y = epilogue(y, r_ref[...])
            elif r_ref is not None:
                y = y + r_ref[...].astype(F32)
            o_ref[...] = y.astype(o_ref.dtype)

    args = [a, b] + ([res] if res is not None else [])
    specs = [a_spec, b_spec] + ([res_spec] if res is not None else [])
    sem = ("parallel",) * kax + ("arbitrary",)
    return pl.pallas_call(
        body, out_shape=out_shape, grid=grid, in_specs=specs, out_specs=o_spec,
        scratch_shapes=[pltpu.VMEM(acc_shape, F32)], name=name, compiler_params=_cparams(sem))(*args)


def _tile(n, cap):
    if n <= cap:
        return n
    t = cap - cap % 128
    while t >= 128:
        if n % t == 0:
            return t
        t -= 128
    raise ValueError(f"no tile for {n}")


def _mm(a, b, dims, out_dtype, name, scale=1.0, res=None):
    if dims == "tn":
        kk, m = a.shape
        n = b.shape[1]
        tk, tn = _tile(kk, TMM), _tile(n, 1152)
        return _matmul(a, b, dims=dims, grid=(1, n // tn, kk // tk),
                       a_spec=pl.BlockSpec((tk, m), lambda i, j, k: (k, 0)),
                       b_spec=pl.BlockSpec((tk, tn), lambda i, j, k: (k, j)),
                       o_spec=pl.BlockSpec((m, tn), lambda i, j, k: (0, j)),
                       out_shape=jax.ShapeDtypeStruct((m, n), out_dtype), name=name, scale=scale)
    m, kk = a.shape
    n = b.shape[1] if dims == "nn" else b.shape[0]
    tm, tn, tk = _tile(m, TMM), _tile(n, 1152), _tile(kk, 1536)
    if dims == "nn":
        b_spec = pl.BlockSpec((tk, tn), lambda i, j, k: (k, j))
    else:
        b_spec = pl.BlockSpec((tn, tk), lambda i, j, k: (j, k))
    o_spec = pl.BlockSpec((tm, tn), lambda i, j, k: (i, j))
    return _matmul(a, b, dims=dims, grid=(m // tm, n // tn, kk // tk),
                   a_spec=pl.BlockSpec((tm, tk), lambda i, j, k: (i, k)), b_spec=b_spec, o_spec=o_spec,
                   out_shape=jax.ShapeDtypeStruct((m, n), out_dtype), name=name, scale=scale,
                   res=res, res_spec=o_spec if res is not None else None)


def _rowcall(fn, args, in_specs, out_shapes, out_specs, grid, name, n_acc=0):
    n_in, n_out = len(args), len(out_shapes)

    def body(*refs):
        outs = fn(*[r[...] for r in refs[:n_in]])
        if not isinstance(outs, (tuple, list)):
            outs = (outs,)
        first = pl.program_id(0) == 0
        for ax in range(1, len(grid)):
            first = jnp.logical_and(first, pl.program_id(ax) == 0)
        for idx, (o_ref, val) in enumerate(zip(refs[n_in:], outs)):
            if idx >= n_out - n_acc:
                @pl.when(first)
                def _(o_ref=o_ref):
                    o_ref[...] = jnp.zeros_like(o_ref)

                o_ref[...] += val.astype(o_ref.dtype)
            else:
                o_ref[...] = val.astype(o_ref.dtype)

    sem = (("arbitrary",) if n_acc else ("parallel",)) * len(grid)
    res = pl.pallas_call(body, out_shape=tuple(out_shapes), grid=grid, in_specs=list(in_specs),
                         out_specs=tuple(out_specs), name=name, compiler_params=_cparams(sem))(*args)
    return res


def _vjp_fn(fn, n_in, wrt):
    def bwd(*blocks):
        ins = [b.astype(F32) for b in blocks[:n_in]]
        cts = [c.astype(F32) for c in blocks[n_in:]]
        outs, vjp = jax.vjp(fn, *ins)
        if isinstance(outs, (tuple, list)):
            grads = vjp(tuple(cts))
        else:
            grads = vjp(cts[0])
        return tuple(grads[i] for i in wrt)
    return bwd


def _sds(shape, dtype):
    return jax.ShapeDtypeStruct(tuple(shape), dtype)


def _rows(tm, w, col=0):
    return pl.BlockSpec((tm, w), lambda i, *_: (i, col))


def _shared(shape):
    nd = len(shape)
    return pl.BlockSpec(tuple(shape), lambda *_: (0,) * nd)


def _rms(x, g):
    return x * lax.rsqrt(jnp.mean(x * x, axis=-1, keepdims=True) + EPS) * g


def _silu(x):
    return x * jax.nn.sigmoid(x)


def _swiglu_pair(gu):
    return _silu(gu[0].astype(F32)) * gu[1].astype(F32)


def _swiglu_bwd(dh, gu):
    g, u = gu[0].astype(F32), gu[1].astype(F32)
    sg = jax.nn.sigmoid(g)
    return jnp.stack([dh * u * sg * (1.0 + g * (1.0 - sg)), dh * g * sg])


def _gdn_prep(q, k, v):
    q, k, v = _silu(q), _silu(k), _silu(v)
    q = q * lax.rsqrt(jnp.sum(q * q, axis=-1, keepdims=True) + EPS) * (HD ** -0.5)
    k = k * lax.rsqrt(jnp.sum(k * k, axis=-1, keepdims=True) + EPS)
    return q, k, v


def _gates(ba, a_log, dt_bias):
    lane = lax.broadcasted_iota(jnp.int32, ba.shape, 1)
    beta = jax.nn.sigmoid(ba)
    z = ba + dt_bias
    softplus = jnp.maximum(z, 0.0) + jnp.log(1.0 + jnp.exp(-jnp.abs(z)))
    g = -jnp.exp(a_log) * softplus
    return jnp.where(lane < A_HEADS, beta, jnp.where(lane < 2 * A_HEADS, g, 0.0))


def _outnorm_gate(o, gate, gain):
    return _rms(o, gain) * _silu(gate)


def _memattn(q, k, v):
    lane = lax.shift_right_logical(lax.broadcasted_iota(jnp.int32, (1, MEM_W), 1), 6)
    out = jnp.zeros(q.shape, F32)
    for h in range(MEM_HEADS):
        mh = (lane == h).astype(F32)
        s = _dot(q * mh, k, "nt") * (MEM_HD ** -0.5)
        s = s - lax.stop_gradient(jnp.max(s, axis=-1, keepdims=True))
        p = jnp.exp(s)
        p = p / jnp.sum(p, axis=-1, keepdims=True)
        out = out + _dot(p, v * mh)
    return out


def _rope_mix(a, a_sw, c, s):
    return (a * c + a_sw * s) * Q_PRESCALE


def _kcat(kn, kr, kr_sw, c, s):
    return kn + kr * c + kr_sw * s


def _gdn_local(q, k, v, beta, gcol, grow):
    n = GROUP
    ri = lax.broadcasted_iota(jnp.int32, (n, n), 0)
    ci = lax.broadcasted_iota(jnp.int32, (n, n), 1)
    same = lax.shift_right_logical(ri, 6) == lax.shift_right_logical(ci, 6)
    lower = jnp.logical_and(same, ci <= ri)
    strict = jnp.logical_and(same, ci < ri)
    gc_col = jnp.sum(lower.astype(F32) * grow, axis=1, keepdims=True)
    gc_row = jnp.sum(jnp.logical_and(same, ri <= ci).astype(F32) * gcol, axis=0, keepdims=True)
    glast = jnp.sum(same.astype(F32) * grow, axis=1, keepdims=True)
    decay = jnp.where(lower, jnp.exp(jnp.where(lower, gc_col - gc_row, 0.0)), 0.0)
    kb = k * beta
    nmat = -jnp.where(strict, _dot(kb, k, "nt") * decay, 0.0)
    pinv = (ri == ci).astype(F32) + nmat
    npow = nmat
    for _ in range(5):
        npow = _dot(npow, npow)
        pinv = pinv + _dot(pinv, npow)
    e_gc = jnp.exp(gc_col)
    u = _dot(pinv, v * beta)
    w = _dot(pinv, kb * e_gc)
    qk = _dot(q, k, "nt") * decay
    fold = (jnp.bitwise_and(lax.broadcasted_iota(jnp.int32, (n, CHUNK), 0), CHUNK - 1)
            == lax.broadcasted_iota(jnp.int32, (n, CHUNK), 1)).astype(F32)
    qk_c = _dot(qk, fold)
    q_dec = q * e_gc
    k_dec = k * jnp.exp(glast - gc_col)
    dmat = jnp.exp(glast) * jnp.ones((1, HD), F32)
    return u, w, q_dec, k_dec, qk_c, dmat


def _gdn_step(s, w_c, u_c, qd_c, kd_c, qk_c, d_c):
    v_new = u_c - _dot(w_c, s)
    out = _dot(qd_c, s) + _dot(qk_c, v_new)
    d_row = jnp.mean(d_c, axis=0, keepdims=True)
    s_new = s * d_row + _dot(kd_c, v_new, "tn")
    return s_new, out


def _rmsnorm_fwd(x, gain, name, col=0, width=None):
    t = x.shape[0]
    w = width or x.shape[1]
    (n,) = _rowcall(_rms, [x, gain.reshape(1, w)], [_rows(TM, w, col), _shared((1, w))],
                    [_sds((t, w), BF16)], [_rows(TM, w)], (t // TM,), name)
    return n


def _rmsnorm_bwd(x, gain, dn, dres, name):
    t, w = x.shape
    fn = _vjp_fn(_rms, 2, (0, 1))

    def bwd(xb, gb, dnb, drb):
        dx, dg = fn(xb, gb, dnb)
        return dx + drb, dg

    dx, dg = _rowcall(bwd, [x, gain.reshape(1, w), dn, dres],
                      [_rows(TM, w), _shared((1, w)), _rows(TM, w), _rows(TM, w)],
                      [_sds((t, w), F32), _sds((1, w), F32)], [_rows(TM, w), _shared((1, w))],
                      (t // TM,), name, n_acc=1)
    return dx, dg[0]


def _ffn_fwd(x, gain, wgu8, wd4, tag):
    t = x.shape[0]
    nt = t // TMM
    n = _rmsnorm_fwd(x, gain, tag + "_norm")
    gu = _matmul(n, wgu8, dims="nn", grid=(N_DEV, nt, 1),
                 a_spec=pl.BlockSpec((TMM, D), lambda j, i, k: (i, 0)),
                 b_spec=pl.BlockSpec((None, D, FF_SHARD), lambda j, i, k: (j, 0, 0)),
                 o_spec=pl.BlockSpec((None, TMM, FF_SHARD), lambda j, i, k: (j, i, 0)),
                 out_shape=_sds((N_DEV, t, FF_SHARD), BF16), name=tag + "_gu")
    gu = gu.reshape(2, 4, t, FF_SHARD)
    y = _matmul(gu, wd4, dims="nn", grid=(nt, 1, 4), a_fn=_swiglu_pair,
                a_spec=pl.BlockSpec((2, None, TMM, FF_SHARD), lambda i, j, k: (0, k, i, 0)),
                b_spec=pl.BlockSpec((None, FF_SHARD, D), lambda i, j, k: (k, 0, 0)),
                o_spec=pl.BlockSpec((TMM, D), lambda i, j, k: (i, 0)),
                out_shape=_sds((t, D), F32), name=tag + "_down", scale=0.5,
                res=x, res_spec=pl.BlockSpec((TMM, D), lambda i, j, k: (i, 0)))
    return y, (x, n, gu)


def _ffn_bwd(d, saved, gain, wgu8, wd4, tag):
    x, n, gu = saved
    t = x.shape[0]
    nt = t // TMM
    dgu = _matmul(d, wd4, dims="nt", grid=(4, nt, 1),
                  a_spec=pl.BlockSpec((TMM, D), lambda j, i, k: (i, 0)),
                  b_spec=pl.BlockSpec((None, FF_SHARD, D), lambda j, i, k: (j, 0, 0)),
                  o_spec=pl.BlockSpec((2, None, TMM, FF_SHARD), lambda j, i, k: (0, j, i, 0)),
                  out_shape=_sds((2, 4, t, FF_SHARD), BF16), name=tag + "_dgu", scale=0.5,
                  res=gu, res_spec=pl.BlockSpec((2, None, TMM, FF_SHARD), lambda j, i, k: (0, j, i, 0)),
                  epilogue=_swiglu_bwd, acc_shape=(TMM, FF_SHARD))
    dwd4 = _matmul(gu, d, dims="tn", grid=(4, 1, nt), a_fn=_swiglu_pair,
                   a_spec=pl.BlockSpec((2, None, TMM, FF_SHARD), lambda j, i, k: (0, j, k, 0)),
                   b_spec=pl.BlockSpec((TMM, D), lambda j, i, k: (k, 0)),
                   o_spec=pl.BlockSpec((None, FF_SHARD, D), lambda j, i, k: (j, 0, 0)),
                   out_shape=_sds((4, FF_SHARD, D), F32), name=tag + "_dwd", scale=0.5)
    dgu = dgu.reshape(N_DEV, t, FF_SHARD)
    dwgu8 = _matmul(n, dgu, dims="tn", grid=(N_DEV, 1, nt),
                    a_spec=pl.BlockSpec((TMM, D), lambda j, i, k: (k, 0)),
                    b_spec=pl.BlockSpec((None, TMM, FF_SHARD), lambda j, i, k: (j, k, 0)),
                    o_spec=pl.BlockSpec((None, D, FF_SHARD), lambda j, i, k: (j, 0, 0)),
                    out_shape=_sds((N_DEV, D, FF_SHARD), F32), name=tag + "_dwgu")
    dn = _matmul(dgu, wgu8, dims="nt", grid=(nt, 1, N_DEV),
                 a_spec=pl.BlockSpec((None, TMM, FF_SHARD), lambda i, j, k: (k, i, 0)),
                 b_spec=pl.BlockSpec((None, D, FF_SHARD), lambda i, j, k: (k, 0, 0)),
                 o_spec=pl.BlockSpec((TMM, D), lambda i, j, k: (i, 0)),
                 out_shape=_sds((t, D), BF16), name=tag + "_dn")
    dx, dgain = _rmsnorm_bwd(x, gain, dn, d, tag + "_dnorm")
    return dx, dgain, dwgu8, dwd4


CONV_TC = 768
CONV_K = 4


def _conv_fwd(ha, w, name):
    t = ha.shape[0]
    nb = TM // 8

    def body(prev_ref, cur_ref, w_ref, o_ref):
        i = pl.program_id(0)
        cur = cur_ref[...].astype(F32)
        prev = prev_ref[...].astype(F32) * (i > 0).astype(F32)
        ext = jnp.concatenate([prev, cur], axis=0)
        wv = w_ref[...]
        acc = cur * wv[3:4]
        for k in range(1, CONV_K):
            acc = acc + pltpu.roll(ext, k, axis=0)[8:] * wv[3 - k:4 - k]
        o_ref[...] = acc.astype(o_ref.dtype)

    return pl.pallas_call(
        body, out_shape=_sds((3, t, CONV_TC), BF16), grid=(t // TM, 3),
        in_specs=[pl.BlockSpec((8, CONV_TC), lambda i, c: (jnp.maximum(i * nb - 1, 0), c)),
                  pl.BlockSpec((TM, CONV_TC), lambda i, c: (i, c)),
                  pl.BlockSpec((CONV_K, CONV_TC), lambda i, c: (0, c))],
        out_specs=pl.BlockSpec((None, TM, CONV_TC), lambda i, c: (c, i, 0)),
        name=name, compiler_params=_cparams(("parallel", "parallel")))(ha, ha, w)


def _conv_bwd(ha, w, dy3, name):
    t = ha.shape[0]
    nb = TM // 8
    nt = t // TM

    def body(prev_ref, cur_ref, dy_ref, nxt_ref, w_ref, dx_ref, dw_ref):
        i = pl.program_id(1)
        cur = cur_ref[...].astype(F32)
        prev = prev_ref[...].astype(F32) * (i > 0).astype(F32)
        ext = jnp.concatenate([prev, cur], axis=0)
        dy = dy_ref[...].astype(F32)
        nxt = nxt_ref[...].astype(F32) * (i < nt - 1).astype(F32)
        dext = jnp.concatenate([dy, nxt], axis=0)
        wv = w_ref[...]
        dx = dy * wv[3:4]
        dws = [None] * CONV_K
        dws[3] = jnp.sum(dy * cur, axis=0, keepdims=True)
        for k in range(1, CONV_K):
            dx = dx + pltpu.roll(dext, TM + 8 - k, axis=0)[:TM] * wv[3 - k:4 - k]
            dws[3 - k] = jnp.sum(dy * pltpu.roll(ext, k, axis=0)[8:], axis=0, keepdims=True)
        dx_ref[...] = dx.astype(dx_ref.dtype)

        @pl.when(i == 0)
        def _():
            dw_ref[...] = jnp.zeros_like(dw_ref)

        dw_ref[...] += jnp.concatenate(dws, axis=0)

    return pl.pallas_call(
        body, out_shape=(_sds((t, 3 * CONV_TC), BF16), _sds((CONV_K, 3 * CONV_TC), F32)), grid=(3, nt),
        in_specs=[pl.BlockSpec((8, CONV_TC), lambda c, i: (jnp.maximum(i * nb - 1, 0), c)),
                  pl.BlockSpec((TM, CONV_TC), lambda c, i: (i, c)),
                  pl.BlockSpec((None, TM, CONV_TC), lambda c, i: (c, i, 0)),
                  pl.BlockSpec((None, 8, CONV_TC), lambda c, i: (c, jnp.minimum((i + 1) * nb, t // 8 - 1), 0)),
                  pl.BlockSpec((CONV_K, CONV_TC), lambda c, i: (0, c))],
        out_specs=(pl.BlockSpec((TM, CONV_TC), lambda c, i: (i, c)),
                   pl.BlockSpec((CONV_K, CONV_TC), lambda c, i: (0, c))),
        name=name, compiler_params=_cparams(("parallel", "arbitrary")))(ha, ha, dy3, dy3, w)


def _gdn_specs(t, rev):
    ng = t // GROUP

    def gi(g):
        return ng - 1 - g if rev else g

    qkv = pl.BlockSpec((3, GROUP, GDN_HPS * HD), lambda h, g: (0, gi(g), h))
    bg = pl.BlockSpec((GROUP, 128), lambda h, g: (gi(g), 0))
    dbg = pl.BlockSpec((GDN_HPS, GROUP, 128), lambda h, g: (h, gi(g), 0))
    o = pl.BlockSpec((GROUP, GDN_HPS * HD), lambda h, g: (gi(g), h))
    st = pl.BlockSpec((GDN_HPS, None, HD, HD), lambda h, g: (h, gi(g), 0, 0))
    return qkv, bg, dbg, o, st


def _head_qkv(qkv_ref, j):
    sl = slice(j * HD, (j + 1) * HD)
    return qkv_ref[0, :, sl].astype(F32), qkv_ref[1, :, sl].astype(F32), qkv_ref[2, :, sl].astype(F32)


def _head_gates(bg, h):
    lane = lax.broadcasted_iota(jnp.int32, (1, 128), 1)
    beta = jnp.sum(jnp.where(lane == h, bg, 0.0), axis=1, keepdims=True)
    gcol = jnp.sum(jnp.where(lane == h + A_HEADS, bg, 0.0), axis=1, keepdims=True)
    return beta, gcol, _col_to_row(gcol)


def _gdn_fwd(qkv3, bg, name):
    t = qkv3.shape[1]
    ng = t // GROUP
    qkv_s, bg_s, _, o_s, st_s = _gdn_specs(t, False)

    def body(qkv_ref, bg_ref, o_ref, st_ref, s_scr):
        @pl.when(pl.program_id(1) == 0)
        def _():
            s_scr[...] = jnp.zeros_like(s_scr)

        st_ref[...] = s_scr[...]
        bgv = bg_ref[...]
        loc = [_gdn_local(*_head_qkv(qkv_ref, j), *_head_gates(bgv, pl.program_id(0) * GDN_HPS + j))
               for j in range(GDN_HPS)]
        s = [s_scr[j] for j in range(GDN_HPS)]
        for a in range(GROUP // CHUNK):
            sl = slice(a * CHUNK, (a + 1) * CHUNK)
            for j in range(GDN_HPS):
                u, w, qd, kd, qkc, dm = loc[j]
                s[j], out = _gdn_step(s[j], w[sl], u[sl], qd[sl], kd[sl], qkc[sl], dm[sl])
                o_ref[sl, j * HD:(j + 1) * HD] = out.astype(o_ref.dtype)
        for j in range(GDN_HPS):
            s_scr[j] = s[j]

    return pl.pallas_call(
        body, out_shape=(_sds((t, A_WIDTH), BF16), _sds((A_HEADS, ng, HD, HD), F32)),
        grid=(A_HEADS // GDN_HPS, ng), in_specs=[qkv_s, bg_s], out_specs=(o_s, st_s),
        scratch_shapes=[pltpu.VMEM((GDN_HPS, HD, HD), F32)], name=name,
        compiler_params=_cparams(("parallel", "arbitrary")))(qkv3, bg)


def _gdn_bwd(qkv3, bg, states, do, name):
    t = qkv3.shape[1]
    ng = t // GROUP
    qkv_s, bg_s, dbg_s, o_s, st_s = _gdn_specs(t, True)
    nc = GROUP // CHUNK

    def body(qkv_ref, bg_ref, st_ref, do_ref, dqkv_ref, dbg_ref, ds_scr):
        @pl.when(pl.program_id(1) == 0)
        def _():
            ds_scr[...] = jnp.zeros_like(ds_scr)

        heads = range(GDN_HPS)
        bgv = bg_ref[...]
        hid = [pl.program_id(0) * GDN_HPS + j for j in heads]
        fw = [jax.vjp(_gdn_local, *_head_qkv(qkv_ref, j), *_head_gates(bgv, hid[j])) for j in heads]
        starts = [[None] * nc for _ in heads]
        s = [st_ref[j] for j in heads]
        for a in range(nc):
            sl = slice(a * CHUNK, (a + 1) * CHUNK)
            for j in heads:
                u, w, qd, kd, qkc, dm = fw[j][0]
                starts[j][a] = s[j]
                if a < nc - 1:
                    s[j], _ = _gdn_step(s[j], w[sl], u[sl], qd[sl], kd[sl], qkc[sl], dm[sl])
        ds = [ds_scr[j] for j in heads]
        parts = [[None] * nc for _ in heads]
        for a in reversed(range(nc)):
            sl = slice(a * CHUNK, (a + 1) * CHUNK)
            for j in heads:
                u, w, qd, kd, qkc, dm = fw[j][0]
                _, vjp_step = jax.vjp(_gdn_step, starts[j][a], w[sl], u[sl], qd[sl], kd[sl], qkc[sl], dm[sl])
                grads = vjp_step((ds[j], do_ref[sl, j * HD:(j + 1) * HD].astype(F32)))
                ds[j] = grads[0]
                parts[j][a] = grads[1:]
        lane = lax.broadcasted_iota(jnp.int32, (1, 128), 1)
        for j in heads:
            ds_scr[j] = ds[j]
            dw, du, dqd, dkd, dqk, ddm = [jnp.concatenate([parts[j][a][i] for a in range(nc)], axis=0)
                                          for i in range(6)]
            dq, dk, dv, db, dgc, dgr = fw[j][1]((du, dw, dqd, dkd, dqk, ddm))
            hs = slice(j * HD, (j + 1) * HD)
            dqkv_ref[0, :, hs] = dq.astype(dqkv_ref.dtype)
            dqkv_ref[1, :, hs] = dk.astype(dqkv_ref.dtype)
            dqkv_ref[2, :, hs] = dv.astype(dqkv_ref.dtype)
            dbg_ref[j] = (jnp.where(lane == hid[j], db, 0.0)
                          + jnp.where(lane == hid[j] + A_HEADS, dgc + _row_to_col(dgr), 0.0))

    return pl.pallas_call(
        body, out_shape=(_sds((3, t, A_WIDTH), BF16), _sds((A_HEADS, t, 128), F32)),
        grid=(A_HEADS // GDN_HPS, ng), in_specs=[qkv_s, bg_s, st_s, o_s],
        out_specs=(qkv_s, dbg_s), scratch_shapes=[pltpu.VMEM((GDN_HPS, HD, HD), F32)], name=name,
        compiler_params=_cparams(("parallel", "arbitrary")))(qkv3, bg, states, do)


NEG = -1e30


def _diag_mask(shape, q_axis):
    qi = lax.shift_right_logical(lax.broadcasted_iota(jnp.int32, shape, q_axis), 6)
    ki = lax.shift_right_logical(lax.broadcasted_iota(jnp.int32, shape, 1 - q_axis), 6)
    return ki <= qi


def _col_to_row(col):
    n = col.shape[0]
    eye = lax.broadcasted_iota(jnp.int32, (n, n), 0) == lax.broadcasted_iota(jnp.int32, (n, n), 1)
    return jnp.sum(jnp.where(eye, col, 0.0), axis=0, keepdims=True)


def _row_to_col(row):
    n = row.shape[1]
    eye = lax.broadcasted_iota(jnp.int32, (n, n), 0) == lax.broadcasted_iota(jnp.int32, (n, n), 1)
    return jnp.sum(jnp.where(eye, row, 0.0), axis=1, keepdims=True)


def _blk(ref, i):
    return ref[pl.ds(pl.multiple_of(i * ATT_BQ, ATT_BQ), ATT_BQ), :]


def _att_fwd(qc, kc, v, name):
    t = qc.shape[0]
    nq = t // ATT_BQ

    def body(q_ref, k_ref, v_ref, o_ref, lse_ref, lser_ref, m_scr, l_scr, acc_scr):
        qb = pl.program_id(1)
        q = q_ref[...]
        m_scr[...] = jnp.full_like(m_scr, NEG)
        l_scr[...] = jnp.zeros_like(l_scr)
        acc_scr[...] = jnp.zeros_like(acc_scr)

        def step(kb, diag):
            s = _dot(q, _blk(k_ref, kb), "nt")
            if diag:
                s = jnp.where(_diag_mask(s.shape, 0), s, NEG)
            m_old = m_scr[...]
            m_new = jnp.maximum(m_old, jnp.max(s, axis=1, keepdims=True))
            alpha = jnp.exp2(m_old - m_new)
            p = jnp.exp2(s - m_new)
            l_scr[...] = alpha * l_scr[...] + jnp.sum(p, axis=1, keepdims=True)
            acc_scr[...] = alpha * acc_scr[...] + _dot(p, _blk(v_ref, kb))
            m_scr[...] = m_new

        def loop_body(kb, carry):
            step(kb, False)
            return carry

        lax.fori_loop(0, qb, loop_body, 0)
        step(qb, True)
        o_ref[...] = (acc_scr[...] / l_scr[...]).astype(o_ref.dtype)
        lse = m_scr[...] + jnp.log2(l_scr[...])
        lse_ref[...] = lse
        lser_ref[...] = _col_to_row(lse)

    return pl.pallas_call(
        body, out_shape=(_sds((t, B_HEADS * HD), BF16), _sds((B_HEADS, t, 1), F32),
                         _sds((B_HEADS, nq, 1, ATT_BQ), F32)), grid=(B_HEADS, nq),
        in_specs=[pl.BlockSpec((ATT_BQ, QK_CAT), lambda h, i: (i, h)),
                  pl.BlockSpec((t, QK_CAT), lambda h, i: (0, h)), pl.BlockSpec((t, HD), lambda h, i: (0, h))],
        out_specs=(pl.BlockSpec((ATT_BQ, HD), lambda h, i: (i, h)),
                   pl.BlockSpec((None, ATT_BQ, 1), lambda h, i: (h, i, 0)),
                   pl.BlockSpec((None, None, 1, ATT_BQ), lambda h, i: (h, i, 0, 0))),
        scratch_shapes=[pltpu.VMEM((ATT_BQ, 1), F32), pltpu.VMEM((ATT_BQ, 1), F32), pltpu.VMEM((ATT_BQ, HD), F32)],
        name=name, compiler_params=_cparams(("parallel", "arbitrary")))(qc, kc, v)


def _att_bwd(qc, kc, v, o, lse, lse_row, do, name):
    t = qc.shape[0]
    nq = t // ATT_BQ

    def delta_fn(ob, dob):
        dl = jnp.sum(ob.astype(F32) * dob.astype(F32), axis=1, keepdims=True)
        return dl, _col_to_row(dl)

    delta, delta_row = _rowcall(
        delta_fn, [o, do], [pl.BlockSpec((ATT_BQ, HD), lambda i, h: (i, h))] * 2,
        [_sds((B_HEADS, t, 1), F32), _sds((B_HEADS, nq, 1, ATT_BQ), F32)],
        [pl.BlockSpec((None, ATT_BQ, 1), lambda i, h: (h, i, 0)),
         pl.BlockSpec((None, None, 1, ATT_BQ), lambda i, h: (h, i, 0, 0))], (nq, B_HEADS), name + "_delta")

    def dq_body(q_ref, k_ref, v_ref, do_ref, lse_ref, dl_ref, dq_ref, acc):
        qb = pl.program_id(1)
        q, dob, lse_b, dl_b = q_ref[...], do_ref[...], lse_ref[...], dl_ref[...]
        acc[...] = jnp.zeros_like(acc)

        def step(kb, diag):
            k = _blk(k_ref, kb)
            s = _dot(q, k, "nt")
            if diag:
                s = jnp.where(_diag_mask(s.shape, 0), s, NEG)
            p = jnp.exp2(s - lse_b)
            ds = p * (_dot(dob, _blk(v_ref, kb), "nt") - dl_b)
            acc[...] += _dot(ds, k)

        def loop_body(kb, carry):
            step(kb, False)
            return carry

        lax.fori_loop(0, qb, loop_body, 0)
        step(qb, True)
        dq_ref[...] = (acc[...] * ATT_SCALE).astype(dq_ref.dtype)

    qmap = lambda h, i: (i, h)
    colq = pl.BlockSpec((None, ATT_BQ, 1), lambda h, i: (h, i, 0))
    dq = pl.pallas_call(
        dq_body, out_shape=_sds((t, B_HEADS * QK_CAT), BF16), grid=(B_HEADS, nq),
        in_specs=[pl.BlockSpec((ATT_BQ, QK_CAT), qmap), pl.BlockSpec((t, QK_CAT), lambda h, i: (0, h)),
                  pl.BlockSpec((t, HD), lambda h, i: (0, h)), pl.BlockSpec((ATT_BQ, HD), qmap), colq, colq],
        out_specs=pl.BlockSpec((ATT_BQ, QK_CAT), qmap),
        scratch_shapes=[pltpu.VMEM((ATT_BQ, QK_CAT), F32)], name=name + "_dq",
        compiler_params=_cparams(("parallel", "arbitrary")))(qc, kc, v, do, lse, delta)

    def dkv_body(k_ref, v_ref, q_ref, do_ref, lser_ref, dlr_ref, dk_ref, dv_ref, dk_acc, dv_acc):
        kb = pl.program_id(1)
        k, vv = k_ref[...], v_ref[...]
        dk_acc[...] = jnp.zeros_like(dk_acc)
        dv_acc[...] = jnp.zeros_like(dv_acc)

        def step(qb, diag):
            q, dob = _blk(q_ref, qb), _blk(do_ref, qb)
            st = _dot(k, q, "nt")
            if diag:
                st = jnp.where(_diag_mask(st.shape, 1), st, NEG)
            pt = jnp.exp2(st - lser_ref[qb])
            dst = pt * (_dot(vv, dob, "nt") - dlr_ref[qb])
            dv_acc[...] += _dot(pt, dob)
            dk_acc[...] += _dot(dst, q)

        def loop_body(qb, carry):
            step(qb, False)
            return carry

        step(kb, True)
        lax.fori_loop(kb + 1, nq, loop_body, 0)
        dk_ref[...] = (dk_acc[...] * LN2).astype(dk_ref.dtype)
        dv_ref[...] = dv_acc[...].astype(dv_ref.dtype)

    kmap = lambda h, j: (j, h)
    rowq = pl.BlockSpec((None, nq, 1, ATT_BQ), lambda h, j: (h, 0, 0, 0))
    dk, dv = pl.pallas_call(
        dkv_body, out_shape=(_sds((t, B_HEADS * QK_CAT), BF16), _sds((t, B_HEADS * HD), BF16)),
        grid=(B_HEADS, nq),
        in_specs=[pl.BlockSpec((ATT_BQ, QK_CAT), kmap), pl.BlockSpec((ATT_BQ, HD), kmap),
                  pl.BlockSpec((t, QK_CAT), lambda h, j: (0, h)), pl.BlockSpec((t, HD), lambda h, j: (0, h)),
                  rowq, rowq],
        out_specs=(pl.BlockSpec((ATT_BQ, QK_CAT), kmap), pl.BlockSpec((ATT_BQ, HD), kmap)),
        scratch_shapes=[pltpu.VMEM((ATT_BQ, QK_CAT), F32), pltpu.VMEM((ATT_BQ, HD), F32)], name=name + "_dkv",
        compiler_params=_cparams(("parallel", "arbitrary")))(kc, v, qc, do, lse_row, delta_row)
    return dq, dk, dv


def _mem_fwd(hx, col, mkv, name):
    t = hx.shape[0]
    (o,) = _rowcall(_memattn, [hx, mkv, mkv],
                    [_rows(TM, MEM_W, col), pl.BlockSpec((N_MEM, MEM_W), lambda i: (0, 0)),
                     pl.BlockSpec((N_MEM, MEM_W), lambda i: (0, 1))],
                    [_sds((t, MEM_W), BF16)], [_rows(TM, MEM_W)], (t // TM,), name)
    return o


def _mem_bwd(hx, col, mkv, do, do_col, name):
    t = hx.shape[0]
    dq, dk, dv = _rowcall(_vjp_fn(_memattn, 3, (0, 1, 2)), [hx, mkv, mkv, do],
                          [_rows(TM, MEM_W, col), pl.BlockSpec((N_MEM, MEM_W), lambda i: (0, 0)),
                           pl.BlockSpec((N_MEM, MEM_W), lambda i: (0, 1)), _rows(TM, MEM_W, do_col)],
                          [_sds((t, MEM_W), BF16), _sds((N_MEM, MEM_W), F32), _sds((N_MEM, MEM_W), F32)],
                          [_rows(TM, MEM_W), _shared((N_MEM, MEM_W)), _shared((N_MEM, MEM_W))],
                          (t // TM,), name, n_acc=2)
    return dq, jnp.concatenate([dk, dv], axis=1)


def _a_in_ext(w):
    nb = 4 * A_WIDTH
    ba = jnp.pad(w[:, nb:nb + 2 * A_HEADS], ((0, 0), (0, 128 - 2 * A_HEADS)))
    return jnp.concatenate([w[:, :nb], w[:, nb + 2 * A_HEADS:], ba], axis=1)


def _swap_halves(w):
    return jnp.concatenate([w[..., QK_ROPE // 2:], w[..., :QK_ROPE // 2]], axis=-1)


def _uq_ext(w):
    w = w.reshape(Q_LORA, B_HEADS, QK_NOPE + QK_ROPE)
    nope, rope = w[..., :QK_NOPE], w[..., QK_NOPE:]
    z64 = jnp.zeros((Q_LORA, B_HEADS, QK_CAT - QK_NOPE - QK_ROPE), w.dtype)
    z128 = jnp.zeros((Q_LORA, B_HEADS, QK_NOPE), w.dtype)
    a = jnp.concatenate([nope, rope, z64], axis=-1).reshape(Q_LORA, B_HEADS * QK_CAT)
    b = jnp.concatenate([z128, _swap_halves(rope), z64], axis=-1).reshape(Q_LORA, B_HEADS * QK_CAT)
    return jnp.concatenate([a, b], axis=1)


def _dkv_ext(w):
    ckv, kr = w[:, :KV_LORA], w[:, KV_LORA:]
    z128 = jnp.zeros((D, QK_NOPE), w.dtype)
    z64 = jnp.zeros((D, QK_CAT - QK_NOPE - QK_ROPE), w.dtype)
    return jnp.concatenate([ckv, z128, kr, z64, z128, _swap_halves(kr), z64], axis=1)


def _ukv_ext(w):
    w = w.reshape(KV_LORA, B_HEADS, QK_NOPE + HD)
    kn, vv = w[..., :QK_NOPE], w[..., QK_NOPE:]
    z = jnp.zeros((KV_LORA, B_HEADS, QK_CAT - QK_NOPE), w.dtype)
    a = jnp.concatenate([kn, z], axis=-1).reshape(KV_LORA, B_HEADS * QK_CAT)
    return jnp.concatenate([a, vv.reshape(KV_LORA, B_HEADS * HD)], axis=1)


def _ext_and_back(fn, w):
    ext, back = jax.vjp(fn, w.astype(F32))
    return ext.astype(BF16), lambda g: back(g.astype(F32))[0]


def _rope_tables(pos_col):
    t = pos_col.shape[0]
    inv = (ROPE_THETA ** (-np.arange(0, QK_ROPE, 2, dtype=np.float32) / QK_ROPE)).astype(np.float32)
    inv_row = np.zeros((1, QK_CAT), np.float32)
    inv_row[0, QK_NOPE:QK_NOPE + QK_ROPE] = np.concatenate([inv, inv])
    sign = np.zeros((1, QK_CAT), np.float32)
    sign[0, QK_NOPE:QK_NOPE + QK_ROPE // 2] = -1.0
    sign[0, QK_NOPE + QK_ROPE // 2:QK_NOPE + QK_ROPE] = 1.0
    is_rope = np.abs(sign)
    is_nope = np.zeros((1, QK_CAT), np.float32)
    is_nope[0, :QK_NOPE] = 1.0

    def fn(p, inv_b, sign_b, rope_b, nope_b):
        ang = p.astype(F32) * inv_b
        return jnp.cos(ang) * rope_b + nope_b, jnp.sin(ang) * sign_b

    consts = [jnp.asarray(a) for a in (inv_row, sign, is_rope, is_nope)]
    return _rowcall(fn, [pos_col] + consts, [_rows(TM, 1)] + [_shared((1, QK_CAT))] * 4,
                    [_sds((t, QK_CAT), F32)] * 2, [_rows(TM, QK_CAT)] * 2, (t // TM,), "rope_tables")


def _local_step(x, mem, pos, target, w):
    t = x.shape[0]
    g = {}
    head6 = (t // TM, A_HEADS)

    mem_n = _rmsnorm_fwd_small(mem, w["mem_norm"])
    rope_c, rope_s = _rope_tables(pos.reshape(t, 1))
    mkv = [_mm(mem_n, w["w_mem_kv"][l], "nn", BF16, f"mkv{l}") for l in range(DEPTH)]

    saved = []
    for l in range(DEPTH):
        sv = {}
        x, sv["ffn1"] = _ffn_fwd(x, w["ffn1_norm"][l], w["ffn1_w_gu"][l], w["ffn1_w_down"][l], "ffn1")
        sv["x1"] = x
        n2 = _rmsnorm_fwd(x, w["mix_norm"][l], "mix_norm")
        sv["n2"] = n2
        if l < N_A:
            ha = _mm(n2, w["a_w_in"][l], "nn", BF16, "a_in")
            yc3 = _conv_fwd(ha, w["a_conv"][l], "a_conv")
            blk3 = pl.BlockSpec((3, TM, HD), lambda i, h: (0, i, h))
            (qkv3,) = _rowcall(lambda b: jnp.stack(_gdn_prep(b[0].astype(F32), b[1].astype(F32), b[2].astype(F32))),
                               [yc3], [blk3], [_sds((3, t, A_WIDTH), BF16)], [blk3], head6, "a_prep")
            (bg,) = _rowcall(_gates, [ha, _pad128(w["a_A_log"][l], A_HEADS), _pad128(w["a_dt_bias"][l], A_HEADS)],
                             [_rows(TM, 128, A_BA_BLK), _shared((1, 128)), _shared((1, 128))],
                             [_sds((t, 128), F32)], [_rows(TM, 128)], (t // TM,), "a_gates")
            o_gdn, states = _gdn_fwd(qkv3, bg, "a_gdn")
            (o_a,) = _rowcall(_outnorm_gate, [o_gdn, ha, w["a_out_norm"][l].reshape(1, HD)],
                              [pl.BlockSpec((TM, HD), lambda i, h: (i, h)),
                               pl.BlockSpec((TM, HD), lambda i, h: (i, 3 * A_HEADS + h)), _shared((1, HD))],
                              [_sds((t, A_WIDTH), BF16)], [pl.BlockSpec((TM, HD), lambda i, h: (i, h))],
                              head6, "a_outnorm")
            o_m = _mem_fwd(ha, A_MQ_BLK, mkv[l], "mem_attn_a")
            sv.update(ha=ha, yc3=yc3, qkv3=qkv3, bg=bg, states=states, o_gdn=o_gdn)
            cat = jnp.concatenate([o_a, o_m], axis=1)
        else:
            j = l - N_A
            hb = _mm(n2, w["b_w_in"][j], "nn", BF16, "b_in")
            cqn = _rmsnorm_fwd(hb, w["b_q_norm"][j], "b_qnorm", 0, Q_LORA)
            qq = _mm(cqn, w["b_w_uq"][j], "nn", BF16, "b_uq")
            (qc,) = _rowcall(_rope_mix, [qq, qq, rope_c, rope_s],
                             [pl.BlockSpec((TM, QK_CAT), lambda i, h: (i, h)),
                              pl.BlockSpec((TM, QK_CAT), lambda i, h: (i, B_HEADS + h)),
                              pl.BlockSpec((TM, QK_CAT), lambda i, h: (i, 0)),
                              pl.BlockSpec((TM, QK_CAT), lambda i, h: (i, 0))],
                             [_sds((t, B_HEADS * QK_CAT), BF16)], [pl.BlockSpec((TM, QK_CAT), lambda i, h: (i, h))],
                             head6, "b_qrope")
            o_b, lse, lse_row = _att_fwd(qc, kcat, vmla, "b_attn")
            o_m = _mem_fwd(hb, 1, mkv[l], "mem_attn_b")
            sv.update(hb=hb, cqn=cqn, qc=qc, o_b=o_b, lse=(lse, lse_row))
            cat = jnp.concatenate([o_b, o_m], axis=1)
        sv["cat"] = cat
        x = _mm(cat, w["w_out"][l], "nn", F32, "w_out", res=x)
        x, sv["ffn2"] = _ffn_fwd(x, w["ffn2_norm"][l], w["ffn2_w_gu"][l], w["ffn2_w_down"][l], "ffn2")
        saved.append(sv)
        if l == N_A - 1:
            x_kv = x
            nkv = _rmsnorm_fwd(x, w["kv_in_norm"], "kv_in_norm")
            ckr = _mm(nkv, w["w_dkv"], "nn", BF16, "kv_down")
            ckv_n = _rmsnorm_fwd(ckr, w["kv_lat_norm"], "kv_lat_norm", 0, KV_LORA)
            kvu = _mm(ckv_n, w["w_ukv"], "nn", BF16, "kv_up")
            vmla = kvu[:, B_HEADS * QK_CAT:]
            (kcat,) = _rowcall(_kcat, [kvu, ckr, ckr, rope_c, rope_s],
                               [pl.BlockSpec((TM, QK_CAT), lambda i, h: (i, h)),
                                pl.BlockSpec((TM, QK_CAT), lambda i, h: (i, 1)),
                                pl.BlockSpec((TM, QK_CAT), lambda i, h: (i, 2)),
                                pl.BlockSpec((TM, QK_CAT), lambda i, h: (i, 0)),
                                pl.BlockSpec((TM, QK_CAT), lambda i, h: (i, 0))],
                               [_sds((t, B_HEADS * QK_CAT), BF16)],
                               [pl.BlockSpec((TM, QK_CAT), lambda i, h: (i, h))], head6, "kv_cat")

    def loss_fn(xb, gb, tb):
        def f(xx, gg):
            e = _rms(xx, gg) - tb
            return 0.5 * jnp.sum(jnp.mean(e * e, axis=-1, keepdims=True), axis=0, keepdims=True)
        val, vjp = jax.vjp(f, xb, gb)
        dx, dg = vjp(jnp.ones((1, 1), F32))
        return dx, dg, val * jnp.ones((1, 128), F32)

    d, dfin, loss = _rowcall(loss_fn, [x, w["final_norm"].reshape(1, D), target],
                             [_rows(TM, D), _shared((1, D)), _rows(TM, D)],
                             [_sds((t, D), F32), _sds((1, D), F32), _sds((1, 128), F32)],
                             [_rows(TM, D), _shared((1, D)), _shared((1, 128))], (t // TM,), "loss_head", n_acc=2)
    g["final_norm"] = dfin[0]
    loss = loss[0, 0]

    for name in ("ffn1_norm", "ffn1_w_gu", "ffn1_w_down", "mix_norm", "ffn2_norm", "ffn2_w_gu", "ffn2_w_down",
                 "w_out", "w_mem_kv"):
        g[name] = [None] * DEPTH
    for name in ("a_w_in", "a_conv", "a_A_log", "a_dt_bias", "a_out_norm"):
        g[name] = [None] * N_A
    for name in ("b_w_in", "b_q_norm", "b_w_uq"):
        g[name] = [None] * N_B
    dmkv = [None] * DEPTH
    dkcat = []
    dvmla = []

    for l in reversed(range(DEPTH)):
        sv = saved[l]
        if l == N_A - 1:
            kq = pl.BlockSpec((TM, QK_CAT), lambda i, h: (i, h))
            tab = pl.BlockSpec((TM, QK_CAT), lambda i, h: (i, 0))

            def dk_fn(c, s, d0, d1):
                dk = d0.astype(F32) + d1.astype(F32)
                return dk, dk * c, dk * s

            dkn, dkr_h, dkrs_h = _rowcall(dk_fn, [rope_c, rope_s, dkcat[0], dkcat[1]], [tab, tab, kq, kq],
                                          [_sds((t, B_HEADS * QK_CAT), BF16)] + [_sds((B_HEADS, t, QK_CAT), BF16)] * 2,
                                          [kq] + [pl.BlockSpec((None, TM, QK_CAT), lambda i, h: (h, i, 0))] * 2,
                                          head6, "kv_dcat")

            def sum6(a, b):
                return jnp.sum(a.astype(F32), axis=0), jnp.sum(b.astype(F32), axis=0)

            h6 = pl.BlockSpec((B_HEADS, TM, QK_CAT), lambda i: (0, i, 0))
            dkr, dkrs = _rowcall(sum6, [dkr_h, dkrs_h], [h6, h6], [_sds((t, QK_CAT), BF16)] * 2,
                                 [_rows(TM, QK_CAT)] * 2, (t // TM,), "kv_dkr")

            def addv(a, b):
                return a.astype(F32) + b.astype(F32)

            (dv,) = _rowcall(addv, dvmla, [_rows(TM, B_HEADS * HD)] * 2, [_sds((t, B_HEADS * HD), BF16)],
                             [_rows(TM, B_HEADS * HD)], (t // TM,), "kv_dv")
            dkvu = jnp.concatenate([dkn, dv], axis=1)
            g["w_ukv"] = _mm(ckv_n, dkvu, "tn", F32, "kv_up_dw")
            dckv_n = _mm(dkvu, w["w_ukv"], "nt", BF16, "kv_up_dx")

            def lat_bwd(cb, gb, dnb):
                return _vjp_fn(_rms, 2, (0, 1))(cb, gb, dnb)

            dckv, g["kv_lat_norm"] = _rowcall(lat_bwd, [ckr, w["kv_lat_norm"].reshape(1, KV_LORA), dckv_n],
                                              [_rows(TM, KV_LORA), _shared((1, KV_LORA)), _rows(TM, KV_LORA)],
                                              [_sds((t, KV_LORA), BF16), _sds((1, KV_LORA), F32)],
                                              [_rows(TM, KV_LORA), _shared((1, KV_LORA))], (t // TM,),
                                              "kv_lat_dnorm", n_acc=1)
            g["kv_lat_norm"] = g["kv_lat_norm"][0]
            dckr = jnp.concatenate([dckv, dkr, dkrs], axis=1)
            g["w_dkv"] = _mm(nkv, dckr, "tn", F32, "kv_down_dw")
            dnkv = _mm(dckr, w["w_dkv"], "nt", BF16, "kv_down_dx")
            d, g["kv_in_norm"] = _rmsnorm_bwd(x_kv, w["kv_in_norm"], dnkv, d, "kv_in_dnorm")

        d, g["ffn2_norm"][l], g["ffn2_w_gu"][l], g["ffn2_w_down"][l] = _ffn_bwd(
            d, sv["ffn2"], w["ffn2_norm"][l], w["ffn2_w_gu"][l], w["ffn2_w_down"][l], "ffn2b")
        g["w_out"][l] = _mm(sv["cat"], d, "tn", F32, "w_out_dw")
        dcat = _mm(d, w["w_out"][l], "nt", BF16, "w_out_dx")
        if l < N_A:
            ha, yc3, qkv3, states, o_gdn = sv["ha"], sv["yc3"], sv["qkv3"], sv["states"], sv["o_gdn"]
            dmq, dmkv[l] = _mem_bwd(ha, A_MQ_BLK, mkv[l], dcat, 3, "mem_attn_a_bwd")
            hblk = pl.BlockSpec((TM, HD), lambda i, h: (i, h))
            do_gdn, dgate, dgain = _rowcall(
                _vjp_fn(_outnorm_gate, 3, (0, 1, 2)), [o_gdn, ha, w["a_out_norm"][l].reshape(1, HD), dcat],
                [hblk, pl.BlockSpec((TM, HD), lambda i, h: (i, 3 * A_HEADS + h)), _shared((1, HD)), hblk],
                [_sds((t, A_WIDTH), BF16), _sds((t, A_WIDTH), BF16), _sds((1, HD), F32)],
                [hblk, hblk, _shared((1, HD))], head6, "a_outnorm_bwd", n_acc=1)
            g["a_out_norm"][l] = dgain[0]
            dqkv3, dbg6 = _gdn_bwd(qkv3, sv["bg"], states, do_gdn, "a_gdn_bwd")

            def dgates(bab, alb, dtb, d6):
                return _vjp_fn(_gates, 3, (0, 1, 2))(bab, alb, dtb, jnp.sum(d6, axis=0))

            dba, dalog, ddt = _rowcall(
                dgates, [ha, _pad128(w["a_A_log"][l], A_HEADS), _pad128(w["a_dt_bias"][l], A_HEADS), dbg6],
                [_rows(TM, 128, A_BA_BLK), _shared((1, 128)), _shared((1, 128)),
                 pl.BlockSpec((A_HEADS, TM, 128), lambda i: (0, i, 0))],
                [_sds((t, 128), BF16), _sds((1, 128), F32), _sds((1, 128), F32)],
                [_rows(TM, 128), _shared((1, 128)), _shared((1, 128))], (t // TM,), "a_gates_bwd", n_acc=2)
            g["a_A_log"][l] = dalog[0, A_HEADS:2 * A_HEADS]
            g["a_dt_bias"][l] = ddt[0, A_HEADS:2 * A_HEADS]
            blk3 = pl.BlockSpec((3, TM, HD), lambda i, h: (0, i, h))
            def dprep(b, db):
                return jnp.stack(_vjp_fn(_gdn_prep, 3, (0, 1, 2))(b[0], b[1], b[2], db[0], db[1], db[2]))

            (dyc3,) = _rowcall(dprep, [yc3, dqkv3], [blk3, blk3],
                               [_sds((3, t, A_WIDTH), BF16)], [blk3], head6, "a_prep_bwd")
            dqkv_in, g["a_conv"][l] = _conv_bwd(ha, w["a_conv"][l], dyc3, "a_conv_bwd")
            dha = jnp.concatenate([dqkv_in, dgate, dmq, dba], axis=1)
            g["a_w_in"][l] = _mm(sv["n2"], dha, "tn", F32, "a_in_dw")
            dn2 = _mm(dha, w["a_w_in"][l], "nt", BF16, "a_in_dx")
        else:
            j = l - N_A
            hb, cqn, qc, o_b, lse = sv["hb"], sv["cqn"], sv["qc"], sv["o_b"], sv["lse"]
            dmq, dmkv[l] = _mem_bwd(hb, 1, mkv[l], dcat, 3, "mem_attn_b_bwd")
            dqc, dkc, dvv = _att_bwd(qc, kcat, vmla, o_b, lse[0], lse[1], dcat, "b_attn_bwd")
            dkcat.append(dkc)
            dvmla.append(dvv)
            kq = pl.BlockSpec((TM, QK_CAT), lambda i, h: (i, h))
            tab = pl.BlockSpec((TM, QK_CAT), lambda i, h: (i, 0))

            def dq_fn(c, s, dq):
                dq = dq.astype(F32)
                return jnp.stack([dq * c, dq * s])

            (dqq,) = _rowcall(dq_fn, [rope_c, rope_s, dqc], [tab, tab, kq],
                              [_sds((2, t, B_HEADS * QK_CAT), BF16)],
                              [pl.BlockSpec((2, TM, QK_CAT), lambda i, h: (0, i, h))], head6, "b_qrope_bwd")
            dqq = jnp.concatenate([dqq[0], dqq[1]], axis=1)
            g["b_w_uq"][j] = _mm(cqn, dqq, "tn", F32, "b_uq_dw")
            dcqn = _mm(dqq, w["b_w_uq"][j], "nt", BF16, "b_uq_dx")
            dcq, dqg = _rowcall(_vjp_fn(_rms, 2, (0, 1)), [hb, w["b_q_norm"][j].reshape(1, Q_LORA), dcqn],
                                [_rows(TM, Q_LORA), _shared((1, Q_LORA)), _rows(TM, Q_LORA)],
                                [_sds((t, Q_LORA), BF16), _sds((1, Q_LORA), F32)],
                                [_rows(TM, Q_LORA), _shared((1, Q_LORA))], (t // TM,), "b_qnorm_bwd", n_acc=1)
            g["b_q_norm"][j] = dqg[0]
            dhb = jnp.concatenate([dcq, dmq], axis=1)
            g["b_w_in"][j] = _mm(sv["n2"], dhb, "tn", F32, "b_in_dw")
            dn2 = _mm(dhb, w["b_w_in"][j], "nt", BF16, "b_in_dx")
        d, g["mix_norm"][l] = _rmsnorm_bwd(sv["x1"], w["mix_norm"][l], dn2, d, "mix_dnorm")
        d, g["ffn1_norm"][l], g["ffn1_w_gu"][l], g["ffn1_w_down"][l] = _ffn_bwd(
            d, sv["ffn1"], w["ffn1_norm"][l], w["ffn1_w_gu"][l], w["ffn1_w_down"][l], "ffn1b")

    dmem_n = None
    for l in range(DEPTH):
        g["w_mem_kv"][l] = _mm(mem_n, dmkv[l], "tn", F32, f"mkv_dw{l}")
        dmem_n = _mm(dmkv[l], w["w_mem_kv"][l], "nt", F32, f"mkv_dx{l}", res=dmem_n)
    (_, gmn) = _rowcall(_vjp_fn(_rms, 2, (0, 1)), [mem, w["mem_norm"].reshape(1, D), dmem_n],
                        [_shared((N_MEM, D)), _shared((1, D)), _shared((N_MEM, D))],
                        [_sds((N_MEM, D), F32), _sds((1, D), F32)], [_shared((N_MEM, D)), _shared((1, D))],
                        (1,), "mem_dnorm")
    g["mem_norm"] = gmn[0]
    return loss, d, g


def _pad128(v, offset):
    return jnp.pad(v.astype(F32).reshape(1, -1), ((0, 0), (offset, 128 - offset - v.shape[0])))


def _rmsnorm_fwd_small(x, gain):
    r, w = x.shape
    (n,) = _rowcall(_rms, [x, gain.reshape(1, w)], [_shared((r, w)), _shared((1, w))],
                    [_sds((r, w), BF16)], [_shared((r, w))], (1,), "mem_norm")
    return n


def _exchange(srcs, gather, name):
    n = len(srcs)
    blks = [tuple(s.shape) if gather else tuple(s.shape[1:]) for s in srcs]

    def body(*refs):
        src_refs, out_refs = refs[:n], refs[n:2 * n]
        send_sems, recv_sems, local_sems = refs[2 * n:]
        x, y, c = lax.axis_index("x"), lax.axis_index("y"), lax.axis_index("c")
        me = 4 * x + 2 * y + c
        copies = []
        for k in range(1, N_DEV):
            px = (x + (k >> 2 & 1)) % 2
            py = (y + (k >> 1 & 1)) % 2
            pc = (c + (k & 1)) % 2
            peer = 4 * px + 2 * py + pc
            for a in range(n):
                cp = pltpu.make_async_remote_copy(
                    src_ref=src_refs[a] if gather else src_refs[a].at[peer], dst_ref=out_refs[a].at[me],
                    send_sem=send_sems.at[a, k - 1], recv_sem=recv_sems.at[a, k - 1],
                    device_id=(px, py, pc), device_id_type=pl.DeviceIdType.MESH)
                cp.start()
                copies.append(cp)
        for a in range(n):
            cp = pltpu.make_async_copy(src_refs[a] if gather else src_refs[a].at[me], out_refs[a].at[me],
                                       local_sems.at[a])
            cp.start()
            copies.append(cp)
        for cp in copies:
            cp.wait()

    return pl.pallas_call(
        body, out_shape=tuple(_sds((N_DEV,) + b, s.dtype) for b, s in zip(blks, srcs)),
        in_specs=[pl.BlockSpec(memory_space=pl.ANY)] * n, out_specs=tuple([pl.BlockSpec(memory_space=pl.ANY)] * n),
        scratch_shapes=[pltpu.SemaphoreType.DMA((n, N_DEV - 1)), pltpu.SemaphoreType.DMA((n, N_DEV - 1)),
                        pltpu.SemaphoreType.DMA((n,))],
        name=name)(*srcs)


def _reduce_adamw(parts, wp, mp, vp, name):
    r, cols = wp.shape
    tr = _tile_rows(r, cols)
    c1 = 1.0 - ADAM_B1 ** ADAM_STEP
    c2 = 1.0 - ADAM_B2 ** ADAM_STEP

    def fn(pb, wb, mb, vb):
        gsum = pb[0].astype(F32)
        for j in range(1, N_DEV):
            gsum = gsum + pb[j].astype(F32)
        m_new = ADAM_B1 * mb + (1.0 - ADAM_B1) * gsum
        v_new = ADAM_B2 * vb + (1.0 - ADAM_B2) * (gsum * gsum)
        delta = -ADAM_LR * ((m_new / c1) / (jnp.sqrt(v_new / c2) + ADAM_EPS) + ADAM_WD * wb)
        return gsum, delta, m_new, v_new

    row = _rows(tr, cols)
    return _rowcall(fn, [parts, wp, mp, vp],
                    [pl.BlockSpec((N_DEV, tr, cols), lambda i: (0, i, 0)), row, row, row],
                    [_sds((r, cols), F32)] * 4, [row] * 4, (r // tr,), name)


def _tile_rows(r, cols):
    for t in (512, 256, 128, 64, 32, 16):
        if r % t == 0 and t * cols <= 160 * 1024:
            return t
    return r


def _pack(arrs):
    flat = jnp.concatenate([a.reshape(-1).astype(F32) for a in arrs])
    n = flat.shape[0]
    unit = PACK_W * PACK_ROWS
    tot = -(-n // unit) * unit
    return jnp.pad(flat, (0, tot - n)).reshape(tot // PACK_W, PACK_W)


def _unpack(buf, shapes):
    out, off = [], 0
    flat = buf.reshape(-1)
    for s in shapes:
        n = int(np.prod(s))
        out.append(flat[off:off + n].reshape(s))
        off += n
    return out


def _as2d(a):
    return a.reshape(-1, a.shape[-1])


_SHARDED = ["ffn1_w_gu", "ffn1_w_down", "ffn2_w_gu", "ffn2_w_down", "w_out", "w_mem_kv", "a_w_in", "a_conv",
            "b_w_in", "b_w_uq", "w_dkv", "w_ukv"]
_COL_SHARDED = {"ffn1_w_gu", "ffn2_w_gu", "a_conv", "b_w_uq", "w_ukv"}
_LAYERED = {"ffn1_w_gu": DEPTH, "ffn1_w_down": DEPTH, "ffn2_w_gu": DEPTH, "ffn2_w_down": DEPTH, "w_out": DEPTH,
            "w_mem_kv": DEPTH, "a_w_in": N_A, "a_conv": N_A, "b_w_in": N_B, "b_w_uq": N_B}
_REPLICATED = ["ffn1_norm", "mix_norm", "ffn2_norm", "mem_norm", "a_A_log", "a_dt_bias", "a_out_norm", "b_q_norm",
               "kv_in_norm", "kv_lat_norm", "final_norm"]
_WEIGHTS = ["ffn1_norm", "ffn1_w_gu", "ffn1_w_down", "mix_norm", "ffn2_norm", "ffn2_w_gu", "ffn2_w_down", "w_out",
            "mem_norm", "w_mem_kv", "a_w_in", "a_conv", "a_A_log", "a_dt_bias", "a_out_norm", "b_w_in", "b_q_norm",
            "b_w_uq", "kv_in_norm", "w_dkv", "kv_lat_norm", "w_ukv", "final_norm"]


def _full_from_shards(name, sh):
    if name in ("ffn1_w_gu", "ffn2_w_gu"):
        return sh
    if name in ("ffn1_w_down", "ffn2_w_down"):
        return sh.reshape(4, FF_SHARD, D)
    if name in _COL_SHARDED:
        return jnp.moveaxis(sh, 0, -2).reshape(sh.shape[1:-1] + (N_DEV * sh.shape[-1],))
    return sh.reshape((N_DEV * sh.shape[1],) + sh.shape[2:])


def _shards_from_full(name, full):
    if name in ("ffn1_w_gu", "ffn2_w_gu"):
        return full
    if name in ("ffn1_w_down", "ffn2_w_down"):
        return full.reshape(N_DEV, D_FF // N_DEV, D)
    if name in _COL_SHARDED:
        r, cc = full.shape
        return jnp.moveaxis(full.reshape(r, N_DEV, cc // N_DEV), 1, 0)
    return full.reshape((N_DEV, full.shape[0] // N_DEV) + full.shape[1:])


def kernel(x, mem, positions, ffn1_norm, ffn1_w_gu, ffn1_w_down, mix_norm, ffn2_norm, ffn2_w_gu, ffn2_w_down, w_out, mem_norm, w_mem_kv, a_w_in, a_conv, a_A_log, a_dt_bias, a_out_norm, b_w_in, b_q_norm, b_w_uq, kv_in_norm, w_dkv, kv_lat_norm, w_ukv, final_norm, loss_target, m_ffn1_norm, m_ffn1_w_gu, m_ffn1_w_down, m_mix_norm, m_ffn2_norm, m_ffn2_w_gu, m_ffn2_w_down, m_w_out, m_mem_norm, m_w_mem_kv, m_a_w_in, m_a_conv, m_a_A_log, m_a_dt_bias, m_a_out_norm, m_b_w_in, m_b_q_norm, m_b_w_uq, m_kv_in_norm, m_w_dkv, m_kv_lat_norm, m_w_ukv, m_final_norm, v_ffn1_norm, v_ffn1_w_gu, v_ffn1_w_down, v_mix_norm, v_ffn2_norm, v_ffn2_w_gu, v_ffn2_w_down, v_w_out, v_mem_norm, v_w_mem_kv, v_a_w_in, v_a_conv, v_a_A_log, v_a_dt_bias, v_a_out_norm, v_b_w_in, v_b_q_norm, v_b_w_uq, v_kv_in_norm, v_w_dkv, v_kv_lat_norm, v_w_ukv, v_final_norm):
    loc = dict(locals())
    wl = {n: loc[n] for n in _WEIGHTS}
    ml = {n: loc["m_" + n] for n in _WEIGHTS}
    vl = {n: loc["v_" + n] for n in _WEIGHTS}

    pieces = _exchange([wl[n].astype(BF16) for n in _SHARDED], True, "gather_weights")
    w = {}
    back = {}
    for n, p in zip(_SHARDED, pieces):
        if n in _LAYERED:
            w[n] = [_full_from_shards(n, p[:, l]) for l in range(_LAYERED[n])]
        else:
            w[n] = _full_from_shards(n, p)
    for n in _REPLICATED:
        w[n] = wl[n]
    for l in range(N_A):
        w["a_w_in"][l], back[("a_w_in", l)] = _ext_and_back(_a_in_ext, w["a_w_in"][l])
        w["a_conv"][l] = w["a_conv"][l].astype(F32)
    for j in range(N_B):
        w["b_w_uq"][j], back[("b_w_uq", j)] = _ext_and_back(_uq_ext, w["b_w_uq"][j])
    w["w_dkv"], back["w_dkv"] = _ext_and_back(_dkv_ext, w["w_dkv"])
    w["w_ukv"], back["w_ukv"] = _ext_and_back(_ukv_ext, w["w_ukv"])

    loss, dx, g = _local_step(x[0], mem[0], positions[0], loss_target[0], w)
    loss = lax.psum(loss, ("x", "y", "c"))

    for l in range(N_A):
        g["a_w_in"][l] = back[("a_w_in", l)](g["a_w_in"][l])
    for j in range(N_B):
        g["b_w_uq"][j] = back[("b_w_uq", j)](g["b_w_uq"][j])
    g["w_dkv"] = back["w_dkv"](g["w_dkv"])
    g["w_ukv"] = back["w_ukv"](g["w_ukv"])

    gsh = []
    for n in _SHARDED:
        if n in _LAYERED:
            gsh.append(jnp.stack([_shards_from_full(n, g[n][l]).astype(BF16) for l in range(_LAYERED[n])], axis=1))
        else:
            gsh.append(_shards_from_full(n, g[n]).astype(BF16))
    parts = _exchange(gsh, False, "scatter_grads")
    out = {}
    for n, p in zip(_SHARDED, parts):
        shape = wl[n].shape
        res = _reduce_adamw(p.reshape(N_DEV, -1, shape[-1]), _as2d(wl[n]), _as2d(ml[n]), _as2d(vl[n]), "adamw_" + n)
        for kind, buf in zip(("grad", "delta", "new_m", "new_v"), res):
            out[(kind, n)] = buf.reshape(shape)

    rep_shapes = [wl[n].shape for n in _REPLICATED]
    grep = [jnp.stack(g[n]) if isinstance(g[n], list) else g[n] for n in _REPLICATED]
    (rparts,) = _exchange([_pack(grep)], True, "gather_small_grads")
    res = _reduce_adamw(rparts, _pack([wl[n] for n in _REPLICATED]), _pack([ml[n] for n in _REPLICATED]),
                        _pack([vl[n] for n in _REPLICATED]), "adamw_replicated")
    for kind, buf in zip(("grad", "delta", "new_m", "new_v"), res):
        for n, a in zip(_REPLICATED, _unpack(buf, rep_shapes)):
            out[(kind, n)] = a

    return (loss, dx[None], *[out[("grad", n)] for n in _WEIGHTS], *[out[("delta", n)] for n in _WEIGHTS],
            *[out[("new_m", n)] for n in _WEIGHTS], *[out[("new_v", n)] for n in _WEIGHTS])
```

```python
import functools

import numpy as np
import jax
import jax.numpy as jnp
from jax import lax
from jax.experimental import pallas as pl
from jax.experimental.pallas import tpu as pltpu

F32 = jnp.float32
BF16 = jnp.bfloat16

N_DEV = 8
D = 1024
D_FF = 2816
FF_SHARD = 2 * D_FF // N_DEV
DEPTH = 4
N_A = 2
N_B = 2
EPS = 1e-6
CHUNK = 64
GROUP = 256
GDN_HPS = 3
A_HEADS = 6
HD = 128
A_WIDTH = A_HEADS * HD
B_HEADS = 6
QK_NOPE = 128
QK_ROPE = 64
QK_CAT = 256
Q_LORA = 256
KV_LORA = 256
MEM_HEADS = 4
MEM_HD = 64
MEM_W = 256
N_MEM = 256
ROPE_THETA = 10000.0
ATT_SCALE = (QK_NOPE + QK_ROPE) ** -0.5
LN2 = 0.6931471805599453
Q_PRESCALE = ATT_SCALE / LN2
A_IN = 4 * A_WIDTH + 2 * A_HEADS + MEM_W
A_MQ_BLK = 4 * A_WIDTH // MEM_W
A_BA_BLK = (4 * A_WIDTH + MEM_W) // 128

ADAM_LR = 0.001
ADAM_B1 = 0.9
ADAM_B2 = 0.999
ADAM_EPS = 1e-08
ADAM_WD = 0.01
ADAM_STEP = 10

VMEM_LIMIT = 56 * 1024 * 1024
TM = 512
TMM = 1024
ATT_BQ = 1024
PACK_W = 1024
PACK_ROWS = 32


def _cparams(sem):
    return pltpu.CompilerParams(dimension_semantics=sem, vmem_limit_bytes=VMEM_LIMIT)


_DIMS = {"nn": ((1,), (0,)), "nt": ((1,), (1,)), "tn": ((0,), (0,))}


def _dot(a, b, dims="nn"):
    return lax.dot_general(a.astype(BF16), b.astype(BF16), (_DIMS[dims], ((), ())),
                           preferred_element_type=F32)


def _matmul(a, b, *, dims, grid, a_spec, b_spec, o_spec, out_shape, name, scale=1.0,
            res=None, res_spec=None, a_fn=None, epilogue=None, acc_shape=None):
    nk = grid[-1]
    kax = len(grid) - 1
    if acc_shape is None:
        acc_shape = tuple(s for s in o_spec.block_shape if s is not None)

    def body(*refs):
        if res is None:
            a_ref, b_ref, o_ref, acc = refs
            r_ref = None
        else:
            a_ref, b_ref, r_ref, o_ref, acc = refs
        k = pl.program_id(kax)

        @pl.when(k == 0)
        def _():
            acc[...] = jnp.zeros_like(acc)

        a_blk = a_ref[...] if a_fn is None else a_fn(a_ref[...])
        acc[...] += _dot(a_blk, b_ref[...], dims)

        @pl.when(k == nk - 1)
        def _():
            y = acc[...] * scale
            if epilogue is not None:
                y = epilogue(y, r_ref[...])
            elif r_ref is not None:
                y = y + r_ref[...].astype(F32)
            o_ref[...] = y.astype(o_ref.dtype)

    args = [a, b] + ([res] if res is not None else [])
    specs = [a_spec, b_spec] + ([res_spec] if res is not None else [])
    sem = ("parallel",) * kax + ("arbitrary",)
    return pl.pallas_call(
        body, out_shape=out_shape, grid=grid, in_specs=specs, out_specs=o_spec,
        scratch_shapes=[pltpu.VMEM(acc_shape, F32)], name=name, compiler_params=_cparams(sem))(*args)


def _tile(n, cap):
    if n <= cap:
        return n
    t = cap - cap % 128
    while t >= 128:
        if n % t == 0:
            return t
        t -= 128
    raise ValueError(f"no tile for {n}")


def _mm(a, b, dims, out_dtype, name, scale=1.0, res=None):
    if dims == "tn":
        kk, m = a.shape
        n = b.shape[1]
        tk, tn = _tile(kk, TMM), _tile(n, 1152)
        return _matmul(a, b, dims=dims, grid=(1, n // tn, kk // tk),
                       a_spec=pl.BlockSpec((tk, m), lambda i, j, k: (k, 0)),
                       b_spec=pl.BlockSpec((tk, tn), lambda i, j, k: (k, j)),
                       o_spec=pl.BlockSpec((m, tn), lambda i, j, k: (0, j)),
                       out_shape=jax.ShapeDtypeStruct((m, n), out_dtype), name=name, scale=scale)
    m, kk = a.shape
    n = b.shape[1] if dims == "nn" else b.shape[0]
    tm, tn, tk = _tile(m, TMM), _tile(n, 1152), _tile(kk, 1536)
    if dims == "nn":
        b_spec = pl.BlockSpec((tk, tn), lambda i, j, k: (k, j))
    else:
        b_spec = pl.BlockSpec((tn, tk), lambda i, j, k: (j, k))
    o_spec = pl.BlockSpec((tm, tn), lambda i, j, k: (i, j))
    return _matmul(a, b, dims=dims, grid=(m // tm, n // tn, kk // tk),
                   a_spec=pl.BlockSpec((tm, tk), lambda i, j, k: (i, k)), b_spec=b_spec, o_spec=o_spec,
                   out_shape=jax.ShapeDtypeStruct((m, n), out_dtype), name=name, scale=scale,
                   res=res, res_spec=o_spec if res is not None else None)


def _rowcall(fn, args, in_specs, out_shapes, out_specs, grid, name, n_acc=0):
    n_in, n_out = len(args), len(out_shapes)

    def body(*refs):
        outs = fn(*[r[...] for r in refs[:n_in]])
        if not isinstance(outs, (tuple, list)):
            outs = (outs,)
        first = pl.program_id(0) == 0
        for ax in range(1, len(grid)):
            first = jnp.logical_and(first, pl.program_id(ax) == 0)
        for idx, (o_ref, val) in enumerate(zip(refs[n_in:], outs)):
            if idx >= n_out - n_acc:
                @pl.when(first)
                def _(o_ref=o_ref):
                    o_ref[...] = jnp.zeros_like(o_ref)

                o_ref[...] += val.astype(o_ref.dtype)
            else:
                o_ref[...] = val.astype(o_ref.dtype)

    sem = (("arbitrary",) if n_acc else ("parallel",)) * len(grid)
    res = pl.pallas_call(body, out_shape=tuple(out_shapes), grid=grid, in_specs=list(in_specs),
                         out_specs=tuple(out_specs), name=name, compiler_params=_cparams(sem))(*args)
    return res


def _vjp_fn(fn, n_in, wrt):
    def bwd(*blocks):
        ins = [b.astype(F32) for b in blocks[:n_in]]
        cts = [c.astype(F32) for c in blocks[n_in:]]
        outs, vjp = jax.vjp(fn, *ins)
        if isinstance(outs, (tuple, list)):
            grads = vjp(tuple(cts))
        else:
            grads = vjp(cts[0])
        return tuple(grads[i] for i in wrt)
    return bwd


def _sds(shape, dtype):
    return jax.ShapeDtypeStruct(tuple(shape), dtype)


def _rows(tm, w, col=0):
    return pl.BlockSpec((tm, w), lambda i, *_: (i, col))


def _shared(shape):
    nd = len(shape)
    return pl.BlockSpec(tuple(shape), lambda *_: (0,) * nd)


def _rms(x, g):
    return x * lax.rsqrt(jnp.mean(x * x, axis=-1, keepdims=True) + EPS) * g


def _silu(x):
    return x * jax.nn.sigmoid(x)


def _swiglu_pair(gu):
    return _silu(gu[0].astype(F32)) * gu[1].astype(F32)


def _swiglu_bwd(dh, gu):
    g, u = gu[0].astype(F32), gu[1].astype(F32)
    sg = jax.nn.sigmoid(g)
    return jnp.stack([dh * u * sg * (1.0 + g * (1.0 - sg)), dh * g * sg])


def _gdn_prep(q, k, v):
    q, k, v = _silu(q), _silu(k), _silu(v)
    q = q * lax.rsqrt(jnp.sum(q * q, axis=-1, keepdims=True) + EPS) * (HD ** -0.5)
    k = k * lax.rsqrt(jnp.sum(k * k, axis=-1, keepdims=True) + EPS)
    return q, k, v


def _gates(ba, a_log, dt_bias):
    lane = lax.broadcasted_iota(jnp.int32, ba.shape, 1)
    beta = jax.nn.sigmoid(ba)
    z = ba + dt_bias
    softplus = jnp.maximum(z, 0.0) + jnp.log(1.0 + jnp.exp(-jnp.abs(z)))
    g = -jnp.exp(a_log) * softplus
    return jnp.where(lane < A_HEADS, beta, jnp.where(lane < 2 * A_HEADS, g, 0.0))


def _outnorm_gate(o, gate, gain):
    return _rms(o, gain) * _silu(gate)


def _memattn(q, k, v):
    lane = lax.shift_right_logical(lax.broadcasted_iota(jnp.int32, (1, MEM_W), 1), 6)
    out = jnp.zeros(q.shape, F32)
    for h in range(MEM_HEADS):
        mh = (lane == h).astype(F32)
        s = _dot(q * mh, k, "nt") * (MEM_HD ** -0.5)
        s = s - lax.stop_gradient(jnp.max(s, axis=-1, keepdims=True))
        p = jnp.exp(s)
        p = p / jnp.sum(p, axis=-1, keepdims=True)
        out = out + _dot(p, v * mh)
    return out


def _rope_mix(a, a_sw, c, s):
    return (a * c + a_sw * s) * Q_PRESCALE


def _kcat(kn, kr, kr_sw, c, s):
    return kn + kr * c + kr_sw * s


def _gdn_local(q, k, v, beta, gcol, grow):
    n = GROUP
    ri = lax.broadcasted_iota(jnp.int32, (n, n), 0)
    ci = lax.broadcasted_iota(jnp.int32, (n, n), 1)
    same = lax.shift_right_logical(ri, 6) == lax.shift_right_logical(ci, 6)
    lower = jnp.logical_and(same, ci <= ri)
    strict = jnp.logical_and(same, ci < ri)
    gc_col = jnp.sum(lower.astype(F32) * grow, axis=1, keepdims=True)
    gc_row = jnp.sum(jnp.logical_and(same, ri <= ci).astype(F32) * gcol, axis=0, keepdims=True)
    glast = jnp.sum(same.astype(F32) * grow, axis=1, keepdims=True)
    decay = jnp.where(lower, jnp.exp(jnp.where(lower, gc_col - gc_row, 0.0)), 0.0)
    kb = k * beta
    nmat = -jnp.where(strict, _dot(kb, k, "nt") * decay, 0.0)
    pinv = (ri == ci).astype(F32) + nmat
    npow = nmat
    for _ in range(5):
        npow = _dot(npow, npow)
        pinv = pinv + _dot(pinv, npow)
    e_gc = jnp.exp(gc_col)
    u = _dot(pinv, v * beta)
    w = _dot(pinv, kb * e_gc)
    qk = _dot(q, k, "nt") * decay
    fold = (jnp.bitwise_and(lax.broadcasted_iota(jnp.int32, (n, CHUNK), 0), CHUNK - 1)
            == lax.broadcasted_iota(jnp.int32, (n, CHUNK), 1)).astype(F32)
    qk_c = _dot(qk, fold)
    q_dec = q * e_gc
    k_dec = k * jnp.exp(glast - gc_col)
    dmat = jnp.exp(glast) * jnp.ones((1, HD), F32)
    return u, w, q_dec, k_dec, qk_c, dmat


def _gdn_step(s, w_c, u_c, qd_c, kd_c, qk_c, d_c):
    v_new = u_c - _dot(w_c, s)
    out = _dot(qd_c, s) + _dot(qk_c, v_new)
    d_row = jnp.mean(d_c, axis=0, keepdims=True)
    s_new = s * d_row + _dot(kd_c, v_new, "tn")
    return s_new, out


def _rmsnorm_fwd(x, gain, name, col=0, width=None):
    t = x.shape[0]
    w = width or x.shape[1]
    (n,) = _rowcall(_rms, [x, gain.reshape(1, w)], [_rows(TM, w, col), _shared((1, w))],
                    [_sds((t, w), BF16)], [_rows(TM, w)], (t // TM,), name)
    return n


def _rmsnorm_bwd(x, gain, dn, dres, name):
    t, w = x.shape
    fn = _vjp_fn(_rms, 2, (0, 1))

    def bwd(xb, gb, dnb, drb):
        dx, dg = fn(xb, gb, dnb)
        return dx + drb, dg

    dx, dg = _rowcall(bwd, [x, gain.reshape(1, w), dn, dres],
                      [_rows(TM, w), _shared((1, w)), _rows(TM, w), _rows(TM, w)],
                      [_sds((t, w), F32), _sds((1, w), F32)], [_rows(TM, w), _shared((1, w))],
                      (t // TM,), name, n_acc=1)
    return dx, dg[0]


def _ffn_fwd(x, gain, wgu8, wd4, tag):
    t = x.shape[0]
    nt = t // TMM
    n = _rmsnorm_fwd(x, gain, tag + "_norm")
    gu = _matmul(n, wgu8, dims="nn", grid=(N_DEV, nt, 1),
                 a_spec=pl.BlockSpec((TMM, D), lambda j, i, k: (i, 0)),
                 b_spec=pl.BlockSpec((None, D, FF_SHARD), lambda j, i, k: (j, 0, 0)),
                 o_spec=pl.BlockSpec((None, TMM, FF_SHARD), lambda j, i, k: (j, i, 0)),
                 out_shape=_sds((N_DEV, t, FF_SHARD), BF16), name=tag + "_gu")
    gu = gu.reshape(2, 4, t, FF_SHARD)
    y = _matmul(gu, wd4, dims="nn", grid=(nt, 1, 4), a_fn=_swiglu_pair,
                a_spec=pl.BlockSpec((2, None, TMM, FF_SHARD), lambda i, j, k: (0, k, i, 0)),
                b_spec=pl.BlockSpec((None, FF_SHARD, D), lambda i, j, k: (k, 0, 0)),
                o_spec=pl.BlockSpec((TMM, D), lambda i, j, k: (i, 0)),
                out_shape=_sds((t, D), F32), name=tag + "_down", scale=0.5,
                res=x, res_spec=pl.BlockSpec((TMM, D), lambda i, j, k: (i, 0)))
    return y, (x, n, gu)


def _ffn_bwd(d, saved, gain, wgu8, wd4, tag):
    x, n, gu = saved
    t = x.shape[0]
    nt = t // TMM
    dgu = _matmul(d, wd4, dims="nt", grid=(4, nt, 1),
                  a_spec=pl.BlockSpec((TMM, D), lambda j, i, k: (i, 0)),
                  b_spec=pl.BlockSpec((None, FF_SHARD, D), lambda j, i, k: (j, 0, 0)),
                  o_spec=pl.BlockSpec((2, None, TMM, FF_SHARD), lambda j, i, k: (0, j, i, 0)),
                  out_shape=_sds((2, 4, t, FF_SHARD), BF16), name=tag + "_dgu", scale=0.5,
                  res=gu, res_spec=pl.BlockSpec((2, None, TMM, FF_SHARD), lambda j, i, k: (0, j, i, 0)),
                  epilogue=_swiglu_bwd, acc_shape=(TMM, FF_SHARD))
    dwd4 = _matmul(gu, d, dims="tn", grid=(4, 1, nt), a_fn=_swiglu_pair,
                   a_spec=pl.BlockSpec((2, None, TMM, FF_SHARD), lambda j, i, k: (0, j, k, 0)),
                   b_spec=pl.BlockSpec((TMM, D), lambda j, i, k: (k, 0)),
                   o_spec=pl.BlockSpec((None, FF_SHARD, D), lambda j, i, k: (j, 0, 0)),
                   out_shape=_sds((4, FF_SHARD, D), F32), name=tag + "_dwd", scale=0.5)
    dgu = dgu.reshape(N_DEV, t, FF_SHARD)
    dwgu8 = _matmul(n, dgu, dims="tn", grid=(N_DEV, 1, nt),
                    a_spec=pl.BlockSpec((TMM, D), lambda j, i, k: (k, 0)),
                    b_spec=pl.BlockSpec((None, TMM, FF_SHARD), lambda j, i, k: (j, k, 0)),
                    o_spec=pl.BlockSpec((None, D, FF_SHARD), lambda j, i, k: (j, 0, 0)),
                    out_shape=_sds((N_DEV, D, FF_SHARD), F32), name=tag + "_dwgu")
    dn = _matmul(dgu, wgu8, dims="nt", grid=(nt, 1, N_DEV),
                 a_spec=pl.BlockSpec((None, TMM, FF_SHARD), lambda i, j, k: (k, i, 0)),
                 b_spec=pl.BlockSpec((None, D, FF_SHARD), lambda i, j, k: (k, 0, 0)),
                 o_spec=pl.BlockSpec((TMM, D), lambda i, j, k: (i, 0)),
                 out_shape=_sds((t, D), BF16), name=tag + "_dn")
    dx, dgain = _rmsnorm_bwd(x, gain, dn, d, tag + "_dnorm")
    return dx, dgain, dwgu8, dwd4


CONV_TC = 768
CONV_K = 4


def _conv_fwd(ha, w, name):
    t = ha.shape[0]
    nb = TM // 8

    def body(prev_ref, cur_ref, w_ref, o_ref):
        i = pl.program_id(0)
        cur = cur_ref[...].astype(F32)
        prev = prev_ref[...].astype(F32) * (i > 0).astype(F32)
        ext = jnp.concatenate([prev, cur], axis=0)
        wv = w_ref[...]
        acc = cur * wv[3:4]
        for k in range(1, CONV_K):
            acc = acc + pltpu.roll(ext, k, axis=0)[8:] * wv[3 - k:4 - k]
        o_ref[...] = acc.astype(o_ref.dtype)

    return pl.pallas_call(
        body, out_shape=_sds((3, t, CONV_TC), BF16), grid=(t // TM, 3),
        in_specs=[pl.BlockSpec((8, CONV_TC), lambda i, c: (jnp.maximum(i * nb - 1, 0), c)),
                  pl.BlockSpec((TM, CONV_TC), lambda i, c: (i, c)),
                  pl.BlockSpec((CONV_K, CONV_TC), lambda i, c: (0, c))],
        out_specs=pl.BlockSpec((None, TM, CONV_TC), lambda i, c: (c, i, 0)),
        name=name, compiler_params=_cparams(("parallel", "parallel")))(ha, ha, w)


def _conv_bwd(ha, w, dy3, name):
    t = ha.shape[0]
    nb = TM // 8
    nt = t // TM

    def body(prev_ref, cur_ref, dy_ref, nxt_ref, w_ref, dx_ref, dw_ref):
        i = pl.program_id(1)
        cur = cur_ref[...].astype(F32)
        prev = prev_ref[...].astype(F32) * (i > 0).astype(F32)
        ext = jnp.concatenate([prev, cur], axis=0)
        dy = dy_ref[...].astype(F32)
        nxt = nxt_ref[...].astype(F32) * (i < nt - 1).astype(F32)
        dext = jnp.concatenate([dy, nxt], axis=0)
        wv = w_ref[...]
        dx = dy * wv[3:4]
        dws = [None] * CONV_K
        dws[3] = jnp.sum(dy * cur, axis=0, keepdims=True)
        for k in range(1, CONV_K):
            dx = dx + pltpu.roll(dext, TM + 8 - k, axis=0)[:TM] * wv[3 - k:4 - k]
            dws[3 - k] = jnp.sum(dy * pltpu.roll(ext, k, axis=0)[8:], axis=0, keepdims=True)
        dx_ref[...] = dx.astype(dx_ref.dtype)

        @pl.when(i == 0)
        def _():
            dw_ref[...] = jnp.zeros_like(dw_ref)

        dw_ref[...] += jnp.concatenate(dws, axis=0)

    return pl.pallas_call(
        body, out_shape=(_sds((t, 3 * CONV_TC), BF16), _sds((CONV_K, 3 * CONV_TC), F32)), grid=(3, nt),
        in_specs=[pl.BlockSpec((8, CONV_TC), lambda c, i: (jnp.maximum(i * nb - 1, 0), c)),
                  pl.BlockSpec((TM, CONV_TC), lambda c, i: (i, c)),
                  pl.BlockSpec((None, TM, CONV_TC), lambda c, i: (c, i, 0)),
                  pl.BlockSpec((None, 8, CONV_TC), lambda c, i: (c, jnp.minimum((i + 1) * nb, t // 8 - 1), 0)),
                  pl.BlockSpec((CONV_K, CONV_TC), lambda c, i: (0, c))],
        out_specs=(pl.BlockSpec((TM, CONV_TC), lambda c, i: (i, c)),
                   pl.BlockSpec((CONV_K, CONV_TC), lambda c, i: (0, c))),
        name=name, compiler_params=_cparams(("parallel", "arbitrary")))(ha, ha, dy3, dy3, w)


def _gdn_specs(t, rev):
    ng = t // GROUP

    def gi(g):
        return ng - 1 - g if rev else g

    qkv = pl.BlockSpec((3, GROUP, GDN_HPS * HD), lambda h, g: (0, gi(g), h))
    bg = pl.BlockSpec((GROUP, 128), lambda h, g: (gi(g), 0))
    dbg = pl.BlockSpec((GDN_HPS, GROUP, 128), lambda h, g: (h, gi(g), 0))
    o = pl.BlockSpec((GROUP, GDN_HPS * HD), lambda h, g: (gi(g), h))
    st = pl.BlockSpec((GDN_HPS, None, HD, HD), lambda h, g: (h, gi(g), 0, 0))
    return qkv, bg, dbg, o, st


def _head_qkv(qkv_ref, j):
    sl = slice(j * HD, (j + 1) * HD)
    return qkv_ref[0, :, sl].astype(F32), qkv_ref[1, :, sl].astype(F32), qkv_ref[2, :, sl].astype(F32)


def _head_gates(bg, h):
    lane = lax.broadcasted_iota(jnp.int32, (1, 128), 1)
    beta = jnp.sum(jnp.where(lane == h, bg, 0.0), axis=1, keepdims=True)
    gcol = jnp.sum(jnp.where(lane == h + A_HEADS, bg, 0.0), axis=1, keepdims=True)
    return beta, gcol, _col_to_row(gcol)


def _gdn_fwd(qkv3, bg, name):
    t = qkv3.shape[1]
    ng = t // GROUP
    qkv_s, bg_s, _, o_s, st_s = _gdn_specs(t, False)

    def body(qkv_ref, bg_ref, o_ref, st_ref, s_scr):
        @pl.when(pl.program_id(1) == 0)
        def _():
            s_scr[...] = jnp.zeros_like(s_scr)

        st_ref[...] = s_scr[...]
        bgv = bg_ref[...]
        loc = [_gdn_local(*_head_qkv(qkv_ref, j), *_head_gates(bgv, pl.program_id(0) * GDN_HPS + j))
               for j in range(GDN_HPS)]
        s = [s_scr[j] for j in range(GDN_HPS)]
        for a in range(GROUP // CHUNK):
            sl = slice(a * CHUNK, (a + 1) * CHUNK)
            for j in range(GDN_HPS):
                u, w, qd, kd, qkc, dm = loc[j]
                s[j], out = _gdn_step(s[j], w[sl], u[sl], qd[sl], kd[sl], qkc[sl], dm[sl])
                o_ref[sl, j * HD:(j + 1) * HD] = out.astype(o_ref.dtype)
        for j in range(GDN_HPS):
            s_scr[j] = s[j]

    return pl.pallas_call(
        body, out_shape=(_sds((t, A_WIDTH), BF16), _sds((A_HEADS, ng, HD, HD), F32)),
        grid=(A_HEADS // GDN_HPS, ng), in_specs=[qkv_s, bg_s], out_specs=(o_s, st_s),
        scratch_shapes=[pltpu.VMEM((GDN_HPS, HD, HD), F32)], name=name,
        compiler_params=_cparams(("parallel", "arbitrary")))(qkv3, bg)


def _gdn_bwd(qkv3, bg, states, do, name):
    t = qkv3.shape[1]
    ng = t // GROUP
    qkv_s, bg_s, dbg_s, o_s, st_s = _gdn_specs(t, True)
    nc = GROUP // CHUNK

    def body(qkv_ref, bg_ref, st_ref, do_ref, dqkv_ref, dbg_ref, ds_scr):
        @pl.when(pl.program_id(1) == 0)
        def _():
            ds_scr[...] = jnp.zeros_like(ds_scr)

        heads = range(GDN_HPS)
        bgv = bg_ref[...]
        hid = [pl.program_id(0) * GDN_HPS + j for j in heads]
        fw = [jax.vjp(_gdn_local, *_head_qkv(qkv_ref, j), *_head_gates(bgv, hid[j])) for j in heads]
        starts = [[None] * nc for _ in heads]
        s = [st_ref[j] for j in heads]
        for a in range(nc):
            sl = slice(a * CHUNK, (a + 1) * CHUNK)
            for j in heads:
                u, w, qd, kd, qkc, dm = fw[j][0]
                starts[j][a] = s[j]
                if a < nc - 1:
                    s[j], _ = _gdn_step(s[j], w[sl], u[sl], qd[sl], kd[sl], qkc[sl], dm[sl])
        ds = [ds_scr[j] for j in heads]
        parts = [[None] * nc for _ in heads]
        for a in reversed(range(nc)):
            sl = slice(a * CHUNK, (a + 1) * CHUNK)
            for j in heads:
                u, w, qd, kd, qkc, dm = fw[j][0]
                _, vjp_step = jax.vjp(_gdn_step, starts[j][a], w[sl], u[sl], qd[sl], kd[sl], qkc[sl], dm[sl])
                grads = vjp_step((ds[j], do_ref[sl, j * HD:(j + 1) * HD].astype(F32)))
                ds[j] = grads[0]
                parts[j][a] = grads[1:]
        lane = lax.broadcasted_iota(jnp.int32, (1, 128), 1)
        for j in heads:
            ds_scr[j] = ds[j]
            dw, du, dqd, dkd, dqk, ddm = [jnp.concatenate([parts[j][a][i] for a in range(nc)], axis=0)
                                          for i in range(6)]
            dq, dk, dv, db, dgc, dgr = fw[j][1]((du, dw, dqd, dkd, dqk, ddm))
            hs = slice(j * HD, (j + 1) * HD)
            dqkv_ref[0, :, hs] = dq.astype(dqkv_ref.dtype)
            dqkv_ref[1, :, hs] = dk.astype(dqkv_ref.dtype)
            dqkv_ref[2, :, hs] = dv.astype(dqkv_ref.dtype)
            dbg_ref[j] = (jnp.where(lane == hid[j], db, 0.0)
                          + jnp.where(lane == hid[j] + A_HEADS, dgc + _row_to_col(dgr), 0.0))

    return pl.pallas_call(
        body, out_shape=(_sds((3, t, A_WIDTH), BF16), _sds((A_HEADS, t, 128), F32)),
        grid=(A_HEADS // GDN_HPS, ng), in_specs=[qkv_s, bg_s, st_s, o_s],
        out_specs=(qkv_s, dbg_s), scratch_shapes=[pltpu.VMEM((GDN_HPS, HD, HD), F32)], name=name,
        compiler_params=_cparams(("parallel", "arbitrary")))(qkv3, bg, states, do)


NEG = -1e30


def _diag_mask(shape, q_axis):
    qi = lax.shift_right_logical(lax.broadcasted_iota(jnp.int32, shape, q_axis), 6)
    ki = lax.shift_right_logical(lax.broadcasted_iota(jnp.int32, shape, 1 - q_axis), 6)
    return ki <= qi


def _col_to_row(col):
    n = col.shape[0]
    eye = lax.broadcasted_iota(jnp.int32, (n, n), 0) == lax.broadcasted_iota(jnp.int32, (n, n), 1)
    return jnp.sum(jnp.where(eye, col, 0.0), axis=0, keepdims=True)


def _row_to_col(row):
    n = row.shape[1]
    eye = lax.broadcasted_iota(jnp.int32, (n, n), 0) == lax.broadcasted_iota(jnp.int32, (n, n), 1)
    return jnp.sum(jnp.where(eye, row, 0.0), axis=1, keepdims=True)


def _blk(ref, i):
    return ref[pl.ds(pl.multiple_of(i * ATT_BQ, ATT_BQ), ATT_BQ), :]


def _att_fwd(qc, kc, v, name):
    t = qc.shape[0]
    nq = t // ATT_BQ

    def body(q_ref, k_ref, v_ref, o_ref, lse_ref, lser_ref, m_scr, l_scr, acc_scr):
        qb = pl.program_id(1)
        q = q_ref[...]
        m_scr[...] = jnp.full_like(m_scr, NEG)
        l_scr[...] = jnp.zeros_like(l_scr)
        acc_scr[...] = jnp.zeros_like(acc_scr)

        def step(kb, diag):
            s = _dot(q, _blk(k_ref, kb), "nt")
            if diag:
                s = jnp.where(_diag_mask(s.shape, 0), s, NEG)
            m_old = m_scr[...]
            m_new = jnp.maximum(m_old, jnp.max(s, axis=1, keepdims=True))
            alpha = jnp.exp2(m_old - m_new)
            p = jnp.exp2(s - m_new)
            l_scr[...] = alpha * l_scr[...] + jnp.sum(p, axis=1, keepdims=True)
            acc_scr[...] = alpha * acc_scr[...] + _dot(p, _blk(v_ref, kb))
            m_scr[...] = m_new

        def loop_body(kb, carry):
            step(kb, False)
            return carry

        lax.fori_loop(0, qb, loop_body, 0)
        step(qb, True)
        o_ref[...] = (acc_scr[...] / l_scr[...]).astype(o_ref.dtype)
        lse = m_scr[...] + jnp.log2(l_scr[...])
        lse_ref[...] = lse
        lser_ref[...] = _col_to_row(lse)

    return pl.pallas_call(
        body, out_shape=(_sds((t, B_HEADS * HD), BF16), _sds((B_HEADS, t, 1), F32),
                         _sds((B_HEADS, nq, 1, ATT_BQ), F32)), grid=(B_HEADS, nq),
        in_specs=[pl.BlockSpec((ATT_BQ, QK_CAT), lambda h, i: (i, h)),
                  pl.BlockSpec((t, QK_CAT), lambda h, i: (0, h)), pl.BlockSpec((t, HD), lambda h, i: (0, h))],
        out_specs=(pl.BlockSpec((ATT_BQ, HD), lambda h, i: (i, h)),
                   pl.BlockSpec((None, ATT_BQ, 1), lambda h, i: (h, i, 0)),
                   pl.BlockSpec((None, None, 1, ATT_BQ), lambda h, i: (h, i, 0, 0))),
        scratch_shapes=[pltpu.VMEM((ATT_BQ, 1), F32), pltpu.VMEM((ATT_BQ, 1), F32), pltpu.VMEM((ATT_BQ, HD), F32)],
        name=name, compiler_params=_cparams(("parallel", "arbitrary")))(qc, kc, v)


def _att_bwd(qc, kc, v, o, lse, lse_row, do, name):
    t = qc.shape[0]
    nq = t // ATT_BQ

    def delta_fn(ob, dob):
        dl = jnp.sum(ob.astype(F32) * dob.astype(F32), axis=1, keepdims=True)
        return dl, _col_to_row(dl)

    delta, delta_row = _rowcall(
        delta_fn, [o, do], [pl.BlockSpec((ATT_BQ, HD), lambda i, h: (i, h))] * 2,
        [_sds((B_HEADS, t, 1), F32), _sds((B_HEADS, nq, 1, ATT_BQ), F32)],
        [pl.BlockSpec((None, ATT_BQ, 1), lambda i, h: (h, i, 0)),
         pl.BlockSpec((None, None, 1, ATT_BQ), lambda i, h: (h, i, 0, 0))], (nq, B_HEADS), name + "_delta")

    def dq_body(q_ref, k_ref, v_ref, do_ref, lse_ref, dl_ref, dq_ref, acc):
        qb = pl.program_id(1)
        q, dob, lse_b, dl_b = q_ref[...], do_ref[...], lse_ref[...], dl_ref[...]
        acc[...] = jnp.zeros_like(acc)

        def step(kb, diag):
            k = _blk(k_ref, kb)
            s = _dot(q, k, "nt")
            if diag:
                s = jnp.where(_diag_mask(s.shape, 0), s, NEG)
            p = jnp.exp2(s - lse_b)
            ds = p * (_dot(dob, _blk(v_ref, kb), "nt") - dl_b)
            acc[...] += _dot(ds, k)

        def loop_body(kb, carry):
            step(kb, False)
            return carry

        lax.fori_loop(0, qb, loop_body, 0)
        step(qb, True)
        dq_ref[...] = (acc[...] * ATT_SCALE).astype(dq_ref.dtype)

    qmap = lambda h, i: (i, h)
    colq = pl.BlockSpec((None, ATT_BQ, 1), lambda h, i: (h, i, 0))
    dq = pl.pallas_call(
        dq_body, out_shape=_sds((t, B_HEADS * QK_CAT), BF16), grid=(B_HEADS, nq),
        in_specs=[pl.BlockSpec((ATT_BQ, QK_CAT), qmap), pl.BlockSpec((t, QK_CAT), lambda h, i: (0, h)),
                  pl.BlockSpec((t, HD), lambda h, i: (0, h)), pl.BlockSpec((ATT_BQ, HD), qmap), colq, colq],
        out_specs=pl.BlockSpec((ATT_BQ, QK_CAT), qmap),
        scratch_shapes=[pltpu.VMEM((ATT_BQ, QK_CAT), F32)], name=name + "_dq",
        compiler_params=_cparams(("parallel", "arbitrary")))(qc, kc, v, do, lse, delta)

    def dkv_body(k_ref, v_ref, q_ref, do_ref, lser_ref, dlr_ref, dk_ref, dv_ref, dk_acc, dv_acc):
        kb = pl.program_id(1)
        k, vv = k_ref[...], v_ref[...]
        dk_acc[...] = jnp.zeros_like(dk_acc)
        dv_acc[...] = jnp.zeros_like(dv_acc)

        def step(qb, diag):
            q, dob = _blk(q_ref, qb), _blk(do_ref, qb)
            st = _dot(k, q, "nt")
            if diag:
                st = jnp.where(_diag_mask(st.shape, 1), st, NEG)
            pt = jnp.exp2(st - lser_ref[qb])
            dst = pt * (_dot(vv, dob, "nt") - dlr_ref[qb])
            dv_acc[...] += _dot(pt, dob)
            dk_acc[...] += _dot(dst, q)

        def loop_body(qb, carry):
            step(qb, False)
            return carry

        step(kb, True)
        lax.fori_loop(kb + 1, nq, loop_body, 0)
        dk_ref[...] = (dk_acc[...] * LN2).astype(dk_ref.dtype)
        dv_ref[...] = dv_acc[...].astype(dv_ref.dtype)

    kmap = lambda h, j: (j, h)
    rowq = pl.BlockSpec((None, nq, 1, ATT_BQ), lambda h, j: (h, 0, 0, 0))
    dk, dv = pl.pallas_call(
        dkv_body, out_shape=(_sds((t, B_HEADS * QK_CAT), BF16), _sds((t, B_HEADS * HD), BF16)),
        grid=(B_HEADS, nq),
        in_specs=[pl.BlockSpec((ATT_BQ, QK_CAT), kmap), pl.BlockSpec((ATT_BQ, HD), kmap),
                  pl.BlockSpec((t, QK_CAT), lambda h, j: (0, h)), pl.BlockSpec((t, HD), lambda h, j: (0, h)),
                  rowq, rowq],
        out_specs=(pl.BlockSpec((ATT_BQ, QK_CAT), kmap), pl.BlockSpec((ATT_BQ, HD), kmap)),
        scratch_shapes=[pltpu.VMEM((ATT_BQ, QK_CAT), F32), pltpu.VMEM((ATT_BQ, HD), F32)], name=name + "_dkv",
        compiler_params=_cparams(("parallel", "arbitrary")))(kc, v, qc, do, lse_row, delta_row)
    return dq, dk, dv


def _mem_fwd(hx, col, mkv, name):
    t = hx.shape[0]
    (o,) = _rowcall(_memattn, [hx, mkv, mkv],
                    [_rows(TM, MEM_W, col), pl.BlockSpec((N_MEM, MEM_W), lambda i: (0, 0)),
                     pl.BlockSpec((N_MEM, MEM_W), lambda i: (0, 1))],
                    [_sds((t, MEM_W), BF16)], [_rows(TM, MEM_W)], (t // TM,), name)
    return o


def _mem_bwd(hx, col, mkv, do, do_col, name):
    t = hx.shape[0]
    dq, dk, dv = _rowcall(_vjp_fn(_memattn, 3, (0, 1, 2)), [hx, mkv, mkv, do],
                          [_rows(TM, MEM_W, col), pl.BlockSpec((N_MEM, MEM_W), lambda i: (0, 0)),
                           pl.BlockSpec((N_MEM, MEM_W), lambda i: (0, 1)), _rows(TM, MEM_W, do_col)],
                          [_sds((t, MEM_W), BF16), _sds((N_MEM, MEM_W), F32), _sds((N_MEM, MEM_W), F32)],
                          [_rows(TM, MEM_W), _shared((N_MEM, MEM_W)), _shared((N_MEM, MEM_W))],
                          (t // TM,), name, n_acc=2)
    return dq, jnp.concatenate([dk, dv], axis=1)


def _a_in_ext(w):
    nb = 4 * A_WIDTH
    ba = jnp.pad(w[:, nb:nb + 2 * A_HEADS], ((0, 0), (0, 128 - 2 * A_HEADS)))
    return jnp.concatenate([w[:, :nb], w[:, nb + 2 * A_HEADS:], ba], axis=1)


def _swap_halves(w):
    return jnp.concatenate([w[..., QK_ROPE // 2:], w[..., :QK_ROPE // 2]], axis=-1)


def _uq_ext(w):
    w = w.reshape(Q_LORA, B_HEADS, QK_NOPE + QK_ROPE)
    nope, rope = w[..., :QK_NOPE], w[..., QK_NOPE:]
    z64 = jnp.zeros((Q_LORA, B_HEADS, QK_CAT - QK_NOPE - QK_ROPE), w.dtype)
    z128 = jnp.zeros((Q_LORA, B_HEADS, QK_NOPE), w.dtype)
    a = jnp.concatenate([nope, rope, z64], axis=-1).reshape(Q_LORA, B_HEADS * QK_CAT)
    b = jnp.concatenate([z128, _swap_halves(rope), z64], axis=-1).reshape(Q_LORA, B_HEADS * QK_CAT)
    return jnp.concatenate([a, b], axis=1)


def _dkv_ext(w):
    ckv, kr = w[:, :KV_LORA], w[:, KV_LORA:]
    z128 = jnp.zeros((D, QK_NOPE), w.dtype)
    z64 = jnp.zeros((D, QK_CAT - QK_NOPE - QK_ROPE), w.dtype)
    return jnp.concatenate([ckv, z128, kr, z64, z128, _swap_halves(kr), z64], axis=1)


def _ukv_ext(w):
    w = w.reshape(KV_LORA, B_HEADS, QK_NOPE + HD)
    kn, vv = w[..., :QK_NOPE], w[..., QK_NOPE:]
    z = jnp.zeros((KV_LORA, B_HEADS, QK_CAT - QK_NOPE), w.dtype)
    a = jnp.concatenate([kn, z], axis=-1).reshape(KV_LORA, B_HEADS * QK_CAT)
    return jnp.concatenate([a, vv.reshape(KV_LORA, B_HEADS * HD)], axis=1)


def _ext_and_back(fn, w):
    ext, back = jax.vjp(fn, w.astype(F32))
    return ext.astype(BF16), lambda g: back(g.astype(F32))[0]


def _rope_tables(pos_col):
    t = pos_col.shape[0]
    inv = (ROPE_THETA ** (-np.arange(0, QK_ROPE, 2, dtype=np.float32) / QK_ROPE)).astype(np.float32)
    inv_row = np.zeros((1, QK_CAT), np.float32)
    inv_row[0, QK_NOPE:QK_NOPE + QK_ROPE] = np.concatenate([inv, inv])
    sign = np.zeros((1, QK_CAT), np.float32)
    sign[0, QK_NOPE:QK_NOPE + QK_ROPE // 2] = -1.0
    sign[0, QK_NOPE + QK_ROPE // 2:QK_NOPE + QK_ROPE] = 1.0
    is_rope = np.abs(sign)
    is_nope = np.zeros((1, QK_CAT), np.float32)
    is_nope[0, :QK_NOPE] = 1.0

    def fn(p, inv_b, sign_b, rope_b, nope_b):
        ang = p.astype(F32) * inv_b
        return jnp.cos(ang) * rope_b + nope_b, jnp.sin(ang) * sign_b

    consts = [jnp.asarray(a) for a in (inv_row, sign, is_rope, is_nope)]
    return _rowcall(fn, [pos_col] + consts, [_rows(TM, 1)] + [_shared((1, QK_CAT))] * 4,
                    [_sds((t, QK_CAT), F32)] * 2, [_rows(TM, QK_CAT)] * 2, (t // TM,), "rope_tables")


def _local_step(x, mem, pos, target, w, layer_start):
    t = x.shape[0]
    g = {}
    head6 = (t // TM, A_HEADS)

    mem_n = _rmsnorm_fwd_small(mem, w["mem_norm"])
    rope_c, rope_s = _rope_tables(pos.reshape(t, 1))
    mkv = [_mm(mem_n, w["w_mem_kv"][l], "nn", BF16, f"mkv{l}") for l in range(DEPTH)]

    saved = []
    for l in range(DEPTH):
        sv = {}
        layer_start(l, x)
        x, sv["ffn1"] = _ffn_fwd(x, w["ffn1_norm"][l], w["ffn1_w_gu"][l], w["ffn1_w_down"][l], "ffn1")
        sv["x1"] = x
        n2 = _rmsnorm_fwd(x, w["mix_norm"][l], "mix_norm")
        sv["n2"] = n2
        if l < N_A:
            ha = _mm(n2, w["a_w_in"][l], "nn", BF16, "a_in")
            yc3 = _conv_fwd(ha, w["a_conv"][l], "a_conv")
            blk3 = pl.BlockSpec((3, TM, HD), lambda i, h: (0, i, h))
            (qkv3,) = _rowcall(lambda b: jnp.stack(_gdn_prep(b[0].astype(F32), b[1].astype(F32), b[2].astype(F32))),
                               [yc3], [blk3], [_sds((3, t, A_WIDTH), BF16)], [blk3], head6, "a_prep")
            (bg,) = _rowcall(_gates, [ha, _pad128(w["a_A_log"][l], A_HEADS), _pad128(w["a_dt_bias"][l], A_HEADS)],
                             [_rows(TM, 128, A_BA_BLK), _shared((1, 128)), _shared((1, 128))],
                             [_sds((t, 128), F32)], [_rows(TM, 128)], (t // TM,), "a_gates")
            o_gdn, states = _gdn_fwd(qkv3, bg, "a_gdn")
            (o_a,) = _rowcall(_outnorm_gate, [o_gdn, ha, w["a_out_norm"][l].reshape(1, HD)],
                              [pl.BlockSpec((TM, HD), lambda i, h: (i, h)),
                               pl.BlockSpec((TM, HD), lambda i, h: (i, 3 * A_HEADS + h)), _shared((1, HD))],
                              [_sds((t, A_WIDTH), BF16)], [pl.BlockSpec((TM, HD), lambda i, h: (i, h))],
                              head6, "a_outnorm")
            o_m = _mem_fwd(ha, A_MQ_BLK, mkv[l], "mem_attn_a")
            sv.update(ha=ha, yc3=yc3, qkv3=qkv3, bg=bg, states=states, o_gdn=o_gdn)
            cat = jnp.concatenate([o_a, o_m], axis=1)
        else:
            j = l - N_A
            hb = _mm(n2, w["b_w_in"][j], "nn", BF16, "b_in")
            cqn = _rmsnorm_fwd(hb, w["b_q_norm"][j], "b_qnorm", 0, Q_LORA)
            qq = _mm(cqn, w["b_w_uq"][j], "nn", BF16, "b_uq")
            (qc,) = _rowcall(_rope_mix, [qq, qq, rope_c, rope_s],
                             [pl.BlockSpec((TM, QK_CAT), lambda i, h: (i, h)),
                              pl.BlockSpec((TM, QK_CAT), lambda i, h: (i, B_HEADS + h)),
                              pl.BlockSpec((TM, QK_CAT), lambda i, h: (i, 0)),
                              pl.BlockSpec((TM, QK_CAT), lambda i, h: (i, 0))],
                             [_sds((t, B_HEADS * QK_CAT), BF16)], [pl.BlockSpec((TM, QK_CAT), lambda i, h: (i, h))],
                             head6, "b_qrope")
            o_b, lse, lse_row = _att_fwd(qc, kcat, vmla, "b_attn")
            o_m = _mem_fwd(hb, 1, mkv[l], "mem_attn_b")
            sv.update(hb=hb, cqn=cqn, qc=qc, o_b=o_b, lse=(lse, lse_row))
            cat = jnp.concatenate([o_b, o_m], axis=1)
        sv["cat"] = cat
        x = _mm(cat, w["w_out"][l], "nn", F32, "w_out", res=x)
        x, sv["ffn2"] = _ffn_fwd(x, w["ffn2_norm"][l], w["ffn2_w_gu"][l], w["ffn2_w_down"][l], "ffn2")
        saved.append(sv)
        if l == N_A - 1:
            x_kv = x
            nkv = _rmsnorm_fwd(x, w["kv_in_norm"], "kv_in_norm")
            ckr = _mm(nkv, w["w_dkv"], "nn", BF16, "kv_down")
            ckv_n = _rmsnorm_fwd(ckr, w["kv_lat_norm"], "kv_lat_norm", 0, KV_LORA)
            kvu = _mm(ckv_n, w["w_ukv"], "nn", BF16, "kv_up")
            vmla = kvu[:, B_HEADS * QK_CAT:]
            (kcat,) = _rowcall(_kcat, [kvu, ckr, ckr, rope_c, rope_s],
                               [pl.BlockSpec((TM, QK_CAT), lambda i, h: (i, h)),
                                pl.BlockSpec((TM, QK_CAT), lambda i, h: (i, 1)),
                                pl.BlockSpec((TM, QK_CAT), lambda i, h: (i, 2)),
                                pl.BlockSpec((TM, QK_CAT), lambda i, h: (i, 0)),
                                pl.BlockSpec((TM, QK_CAT), lambda i, h: (i, 0))],
                               [_sds((t, B_HEADS * QK_CAT), BF16)],
                               [pl.BlockSpec((TM, QK_CAT), lambda i, h: (i, h))], head6, "kv_cat")

    def loss_fn(xb, gb, tb):
        def f(xx, gg):
            e = _rms(xx, gg) - tb
            return 0.5 * jnp.sum(jnp.mean(e * e, axis=-1, keepdims=True), axis=0, keepdims=True)
        val, vjp = jax.vjp(f, xb, gb)
        dx, dg = vjp(jnp.ones((1, 1), F32))
        return dx, dg, val * jnp.ones((1, 128), F32)

    d, dfin, loss = _rowcall(loss_fn, [x, w["final_norm"].reshape(1, D), target],
                             [_rows(TM, D), _shared((1, D)), _rows(TM, D)],
                             [_sds((t, D), F32), _sds((1, D), F32), _sds((1, 128), F32)],
                             [_rows(TM, D), _shared((1, D)), _shared((1, 128))], (t // TM,), "loss_head", n_acc=2)
    g["final_norm"] = dfin[0]
    loss = loss[0, 0]

    for name in ("ffn1_norm", "ffn1_w_gu", "ffn1_w_down", "mix_norm", "ffn2_norm", "ffn2_w_gu", "ffn2_w_down",
                 "w_out", "w_mem_kv"):
        g[name] = [None] * DEPTH
    for name in ("a_w_in", "a_conv", "a_A_log", "a_dt_bias", "a_out_norm"):
        g[name] = [None] * N_A
    for name in ("b_w_in", "b_q_norm", "b_w_uq"):
        g[name] = [None] * N_B
    dmkv = [None] * DEPTH
    dkcat = []
    dvmla = []

    for l in reversed(range(DEPTH)):
        sv = saved[l]
        if l == N_A - 1:
            kq = pl.BlockSpec((TM, QK_CAT), lambda i, h: (i, h))
            tab = pl.BlockSpec((TM, QK_CAT), lambda i, h: (i, 0))

            def dk_fn(c, s, d0, d1):
                dk = d0.astype(F32) + d1.astype(F32)
                return dk, dk * c, dk * s

            dkn, dkr_h, dkrs_h = _rowcall(dk_fn, [rope_c, rope_s, dkcat[0], dkcat[1]], [tab, tab, kq, kq],
                                          [_sds((t, B_HEADS * QK_CAT), BF16)] + [_sds((B_HEADS, t, QK_CAT), BF16)] * 2,
                                          [kq] + [pl.BlockSpec((None, TM, QK_CAT), lambda i, h: (h, i, 0))] * 2,
                                          head6, "kv_dcat")

            def sum6(a, b):
                return jnp.sum(a.astype(F32), axis=0), jnp.sum(b.astype(F32), axis=0)

            h6 = pl.BlockSpec((B_HEADS, TM, QK_CAT), lambda i: (0, i, 0))
            dkr, dkrs = _rowcall(sum6, [dkr_h, dkrs_h], [h6, h6], [_sds((t, QK_CAT), BF16)] * 2,
                                 [_rows(TM, QK_CAT)] * 2, (t // TM,), "kv_dkr")

            def addv(a, b):
                return a.astype(F32) + b.astype(F32)

            (dv,) = _rowcall(addv, dvmla, [_rows(TM, B_HEADS * HD)] * 2, [_sds((t, B_HEADS * HD), BF16)],
                             [_rows(TM, B_HEADS * HD)], (t // TM,), "kv_dv")
            dkvu = jnp.concatenate([dkn, dv], axis=1)
            g["w_ukv"] = _mm(ckv_n, dkvu, "tn", F32, "kv_up_dw")
            dckv_n = _mm(dkvu, w["w_ukv"], "nt", BF16, "kv_up_dx")

            def lat_bwd(cb, gb, dnb):
                return _vjp_fn(_rms, 2, (0, 1))(cb, gb, dnb)

            dckv, g["kv_lat_norm"] = _rowcall(lat_bwd, [ckr, w["kv_lat_norm"].reshape(1, KV_LORA), dckv_n],
                                              [_rows(TM, KV_LORA), _shared((1, KV_LORA)), _rows(TM, KV_LORA)],
                                              [_sds((t, KV_LORA), BF16), _sds((1, KV_LORA), F32)],
                                              [_rows(TM, KV_LORA), _shared((1, KV_LORA))], (t // TM,),
                                              "kv_lat_dnorm", n_acc=1)
            g["kv_lat_norm"] = g["kv_lat_norm"][0]
            dckr = jnp.concatenate([dckv, dkr, dkrs], axis=1)
            g["w_dkv"] = _mm(nkv, dckr, "tn", F32, "kv_down_dw")
            dnkv = _mm(dckr, w["w_dkv"], "nt", BF16, "kv_down_dx")
            d, g["kv_in_norm"] = _rmsnorm_bwd(x_kv, w["kv_in_norm"], dnkv, d, "kv_in_dnorm")

        d, g["ffn2_norm"][l], g["ffn2_w_gu"][l], g["ffn2_w_down"][l] = _ffn_bwd(
            d, sv["ffn2"], w["ffn2_norm"][l], w["ffn2_w_gu"][l], w["ffn2_w_down"][l], "ffn2b")
        g["w_out"][l] = _mm(sv["cat"], d, "tn", F32, "w_out_dw")
        dcat = _mm(d, w["w_out"][l], "nt", BF16, "w_out_dx")
        if l < N_A:
            ha, yc3, qkv3, states, o_gdn = sv["ha"], sv["yc3"], sv["qkv3"], sv["states"], sv["o_gdn"]
            dmq, dmkv[l] = _mem_bwd(ha, A_MQ_BLK, mkv[l], dcat, 3, "mem_attn_a_bwd")
            hblk = pl.BlockSpec((TM, HD), lambda i, h: (i, h))
            do_gdn, dgate, dgain = _rowcall(
                _vjp_fn(_outnorm_gate, 3, (0, 1, 2)), [o_gdn, ha, w["a_out_norm"][l].reshape(1, HD), dcat],
                [hblk, pl.BlockSpec((TM, HD), lambda i, h: (i, 3 * A_HEADS + h)), _shared((1, HD)), hblk],
                [_sds((t, A_WIDTH), BF16), _sds((t, A_WIDTH), BF16), _sds((1, HD), F32)],
                [hblk, hblk, _shared((1, HD))], head6, "a_outnorm_bwd", n_acc=1)
            g["a_out_norm"][l] = dgain[0]
            dqkv3, dbg6 = _gdn_bwd(qkv3, sv["bg"], states, do_gdn, "a_gdn_bwd")

            def dgates(bab, alb, dtb, d6):
                return _vjp_fn(_gates, 3, (0, 1, 2))(bab, alb, dtb, jnp.sum(d6, axis=0))

            dba, dalog, ddt = _rowcall(
                dgates, [ha, _pad128(w["a_A_log"][l], A_HEADS), _pad128(w["a_dt_bias"][l], A_HEADS), dbg6],
                [_rows(TM, 128, A_BA_BLK), _shared((1, 128)), _shared((1, 128)),
                 pl.BlockSpec((A_HEADS, TM, 128), lambda i: (0, i, 0))],
                [_sds((t, 128), BF16), _sds((1, 128), F32), _sds((1, 128), F32)],
                [_rows(TM, 128), _shared((1, 128)), _shared((1, 128))], (t // TM,), "a_gates_bwd", n_acc=2)
            g["a_A_log"][l] = dalog[0, A_HEADS:2 * A_HEADS]
            g["a_dt_bias"][l] = ddt[0, A_HEADS:2 * A_HEADS]
            blk3 = pl.BlockSpec((3, TM, HD), lambda i, h: (0, i, h))
            def dprep(b, db):
                return jnp.stack(_vjp_fn(_gdn_prep, 3, (0, 1, 2))(b[0], b[1], b[2], db[0], db[1], db[2]))

            (dyc3,) = _rowcall(dprep, [yc3, dqkv3], [blk3, blk3],
                               [_sds((3, t, A_WIDTH), BF16)], [blk3], head6, "a_prep_bwd")
            dqkv_in, g["a_conv"][l] = _conv_bwd(ha, w["a_conv"][l], dyc3, "a_conv_bwd")
            dha = jnp.concatenate([dqkv_in, dgate, dmq, dba], axis=1)
            g["a_w_in"][l] = _mm(sv["n2"], dha, "tn", F32, "a_in_dw")
            dn2 = _mm(dha, w["a_w_in"][l], "nt", BF16, "a_in_dx")
        else:
            j = l - N_A
            hb, cqn, qc, o_b, lse = sv["hb"], sv["cqn"], sv["qc"], sv["o_b"], sv["lse"]
            dmq, dmkv[l] = _mem_bwd(hb, 1, mkv[l], dcat, 3, "mem_attn_b_bwd")
            dqc, dkc, dvv = _att_bwd(qc, kcat, vmla, o_b, lse[0], lse[1], dcat, "b_attn_bwd")
            dkcat.append(dkc)
            dvmla.append(dvv)
            kq = pl.BlockSpec((TM, QK_CAT), lambda i, h: (i, h))
            tab = pl.BlockSpec((TM, QK_CAT), lambda i, h: (i, 0))

            def dq_fn(c, s, dq):
                dq = dq.astype(F32)
                return jnp.stack([dq * c, dq * s])

            (dqq,) = _rowcall(dq_fn, [rope_c, rope_s, dqc], [tab, tab, kq],
                              [_sds((2, t, B_HEADS * QK_CAT), BF16)],
                              [pl.BlockSpec((2, TM, QK_CAT), lambda i, h: (0, i, h))], head6, "b_qrope_bwd")
            dqq = jnp.concatenate([dqq[0], dqq[1]], axis=1)
            g["b_w_uq"][j] = _mm(cqn, dqq, "tn", F32, "b_uq_dw")
            dcqn = _mm(dqq, w["b_w_uq"][j], "nt", BF16, "b_uq_dx")
            dcq, dqg = _rowcall(_vjp_fn(_rms, 2, (0, 1)), [hb, w["b_q_norm"][j].reshape(1, Q_LORA), dcqn],
                                [_rows(TM, Q_LORA), _shared((1, Q_LORA)), _rows(TM, Q_LORA)],
                                [_sds((t, Q_LORA), BF16), _sds((1, Q_LORA), F32)],
                                [_rows(TM, Q_LORA), _shared((1, Q_LORA))], (t // TM,), "b_qnorm_bwd", n_acc=1)
            g["b_q_norm"][j] = dqg[0]
            dhb = jnp.concatenate([dcq, dmq], axis=1)
            g["b_w_in"][j] = _mm(sv["n2"], dhb, "tn", F32, "b_in_dw")
            dn2 = _mm(dhb, w["b_w_in"][j], "nt", BF16, "b_in_dx")
        d, g["mix_norm"][l] = _rmsnorm_bwd(sv["x1"], w["mix_norm"][l], dn2, d, "mix_dnorm")
        d, g["ffn1_norm"][l], g["ffn1_w_gu"][l], g["ffn1_w_down"][l] = _ffn_bwd(
            d, sv["ffn1"], w["ffn1_norm"][l], w["ffn1_w_gu"][l], w["ffn1_w_down"][l], "ffn1b")

    dmem_n = None
    for l in range(DEPTH):
        g["w_mem_kv"][l] = _mm(mem_n, dmkv[l], "tn", F32, f"mkv_dw{l}")
        dmem_n = _mm(dmkv[l], w["w_mem_kv"][l], "nt", F32, f"mkv_dx{l}", res=dmem_n)
    (_, gmn) = _rowcall(_vjp_fn(_rms, 2, (0, 1)), [mem, w["mem_norm"].reshape(1, D), dmem_n],
                        [_shared((N_MEM, D)), _shared((1, D)), _shared((N_MEM, D))],
                        [_sds((N_MEM, D), F32), _sds((1, D), F32)], [_shared((N_MEM, D)), _shared((1, D))],
                        (1,), "mem_dnorm")
    g["mem_norm"] = gmn[0]
    return loss, d, g


def _pad128(v, offset):
    return jnp.pad(v.astype(F32).reshape(1, -1), ((0, 0), (offset, 128 - offset - v.shape[0])))


def _rmsnorm_fwd_small(x, gain):
    r, w = x.shape
    (n,) = _rowcall(_rms, [x, gain.reshape(1, w)], [_shared((r, w)), _shared((1, w))],
                    [_sds((r, w), BF16)], [_shared((r, w))], (1,), "mem_norm")
    return n


def _exchange(srcs, gather, name):
    n = len(srcs)
    blks = [tuple(s.shape) if gather else tuple(s.shape[1:]) for s in srcs]

    def body(*refs):
        src_refs, out_refs = refs[:n], refs[n:2 * n]
        send_sems, recv_sems, local_sems = refs[2 * n:]
        x, y, c = lax.axis_index("x"), lax.axis_index("y"), lax.axis_index("c")
        me = 4 * x + 2 * y + c
        copies = []
        for k in range(1, N_DEV):
            px = (x + (k >> 2 & 1)) % 2
            py = (y + (k >> 1 & 1)) % 2
            pc = (c + (k & 1)) % 2
            peer = 4 * px + 2 * py + pc
            for a in range(n):
                cp = pltpu.make_async_remote_copy(
                    src_ref=src_refs[a] if gather else src_refs[a].at[peer], dst_ref=out_refs[a].at[me],
                    send_sem=send_sems.at[a, k - 1], recv_sem=recv_sems.at[a, k - 1],
                    device_id=(px, py, pc), device_id_type=pl.DeviceIdType.MESH)
                cp.start()
                copies.append(cp)
        for a in range(n):
            cp = pltpu.make_async_copy(src_refs[a] if gather else src_refs[a].at[me], out_refs[a].at[me],
                                       local_sems.at[a])
            cp.start()
            copies.append(cp)
        for cp in copies:
            cp.wait()

    return pl.pallas_call(
        body, out_shape=tuple(_sds((N_DEV,) + b, s.dtype) for b, s in zip(blks, srcs)),
        in_specs=[pl.BlockSpec(memory_space=pl.ANY)] * n, out_specs=tuple([pl.BlockSpec(memory_space=pl.ANY)] * n),
        scratch_shapes=[pltpu.SemaphoreType.DMA((n, N_DEV - 1)), pltpu.SemaphoreType.DMA((n, N_DEV - 1)),
                        pltpu.SemaphoreType.DMA((n,))],
        name=name)(*srcs)


_HBM = pl.BlockSpec(memory_space=pltpu.HBM)
_SEM = pl.BlockSpec(memory_space=pltpu.SEMAPHORE)
_EFFECT = pltpu.SideEffectType.DATAFLOW_SIDE_EFFECTING


def _gather_copies(src_refs, land_refs, send_sems, recv_sems):
    x, y, c = lax.axis_index("x"), lax.axis_index("y"), lax.axis_index("c")
    me = 4 * x + 2 * y + c
    copies = []
    for k in range(1, N_DEV):
        peer = ((x + (k >> 2 & 1)) % 2, (y + (k >> 1 & 1)) % 2, (c + (k & 1)) % 2)
        for a in range(len(src_refs)):
            i = (k - 1) * len(src_refs) + a
            copies.append(pltpu.make_async_remote_copy(
                src_ref=src_refs[a], dst_ref=land_refs[a].at[me], send_sem=send_sems[i],
                recv_sem=recv_sems[i], device_id=peer, device_id_type=pl.DeviceIdType.MESH))
    return copies


def _gather_start(srcs, name):
    n = len(srcs)
    srcs = [pltpu.with_memory_space_constraint(s, pltpu.HBM) for s in srcs]
    lands = [pltpu.with_memory_space_constraint(lax.empty((N_DEV,) + s.shape, s.dtype), pltpu.HBM) for s in srcs]

    ns = n * (N_DEV - 1)

    def body(*refs):
        sems = refs[2 * n:2 * n + 2 * ns]
        for cp in _gather_copies(refs[:n], refs[n:2 * n], sems[:ns], sems[ns:]):
            cp.start()
        refs[-1][...] = jnp.zeros_like(refs[-1])

    outs = pl.pallas_call(
        body, name=name,
        out_shape=(*[pltpu.SemaphoreType.DMA(())] * (2 * ns), *[pltpu.HBM(a.shape, a.dtype) for a in srcs + lands],
                   _sds((8, 128), F32)),
        in_specs=[_HBM] * (2 * n),
        out_specs=(*[_SEM] * (2 * ns), *[_HBM] * (2 * n), pl.BlockSpec(memory_space=pltpu.VMEM)),
        input_output_aliases={i: 2 * ns + i for i in range(2 * n)},
        compiler_params=pltpu.CompilerParams(has_side_effects=_EFFECT))(*srcs, *lands)
    sems, rest = list(outs[:2 * ns]), outs[2 * ns:]
    return sems[:ns], sems[ns:], list(rest[:n]), list(rest[n:2 * n]), rest[-1]


def _gather_wait(flight, after, name):
    send_sems, recv_sems, srcs, lands, _ = flight
    n = len(srcs)

    ns = len(send_sems)

    def body(*refs):
        sems = refs[2 * n:2 * n + 2 * ns]
        for cp in _gather_copies(refs[:n], refs[n:2 * n], sems[:ns], sems[ns:]):
            cp.wait_send()
            cp.wait_recv()

    outs = pl.pallas_call(
        body, name=name, out_shape=tuple(pltpu.HBM(a.shape, a.dtype) for a in srcs + lands),
        in_specs=[_HBM] * (2 * n) + [_SEM] * (2 * ns) + [pl.BlockSpec(memory_space=pl.ANY)],
        out_specs=tuple([_HBM] * (2 * n)), input_output_aliases={i: i for i in range(2 * n)},
        compiler_params=pltpu.CompilerParams(has_side_effects=_EFFECT))(*srcs, *lands, *send_sems, *recv_sems, after)
    return list(outs[n:])


def _reduce_adamw(parts, wp, mp, vp, name):
    r, cols = wp.shape
    tr = _tile_rows(r, cols)
    c1 = 1.0 - ADAM_B1 ** ADAM_STEP
    c2 = 1.0 - ADAM_B2 ** ADAM_STEP

    def fn(pb, wb, mb, vb):
        gsum = pb[0].astype(F32)
        for j in range(1, N_DEV):
            gsum = gsum + pb[j].astype(F32)
        m_new = ADAM_B1 * mb + (1.0 - ADAM_B1) * gsum
        v_new = ADAM_B2 * vb + (1.0 - ADAM_B2) * (gsum * gsum)
        delta = -ADAM_LR * ((m_new / c1) / (jnp.sqrt(v_new / c2) + ADAM_EPS) + ADAM_WD * wb)
        return gsum, delta, m_new, v_new

    row = _rows(tr, cols)
    return _rowcall(fn, [parts, wp, mp, vp],
                    [pl.BlockSpec((N_DEV, tr, cols), lambda i: (0, i, 0)), row, row, row],
                    [_sds((r, cols), F32)] * 4, [row] * 4, (r // tr,), name)


def _tile_rows(r, cols):
    for t in (512, 256, 128, 64, 32, 16):
        if r % t == 0 and t * cols <= 160 * 1024:
            return t
    return r


def _pack(arrs):
    flat = jnp.concatenate([a.reshape(-1).astype(F32) for a in arrs])
    n = flat.shape[0]
    unit = PACK_W * PACK_ROWS
    tot = -(-n // unit) * unit
    return jnp.pad(flat, (0, tot - n)).reshape(tot // PACK_W, PACK_W)


def _unpack(buf, shapes):
    out, off = [], 0
    flat = buf.reshape(-1)
    for s in shapes:
        n = int(np.prod(s))
        out.append(flat[off:off + n].reshape(s))
        off += n
    return out


def _as2d(a):
    return a.reshape(-1, a.shape[-1])


_SHARDED = ["ffn1_w_gu", "ffn1_w_down", "ffn2_w_gu", "ffn2_w_down", "w_out", "w_mem_kv", "a_w_in", "a_conv",
            "b_w_in", "b_w_uq", "w_dkv", "w_ukv"]
_COL_SHARDED = {"ffn1_w_gu", "ffn2_w_gu", "a_conv", "b_w_uq", "w_ukv"}
_LAYERED = {"ffn1_w_gu": DEPTH, "ffn1_w_down": DEPTH, "ffn2_w_gu": DEPTH, "ffn2_w_down": DEPTH, "w_out": DEPTH,
            "w_mem_kv": DEPTH, "a_w_in": N_A, "a_conv": N_A, "b_w_in": N_B, "b_w_uq": N_B}
_GATHER_FIRST = [("ffn1_w_gu", 0, 1), ("ffn1_w_down", 0, 1), ("ffn2_w_gu", 0, 1), ("ffn2_w_down", 0, 1),
                 ("w_out", 0, 1), ("a_w_in", 0, 1), ("a_conv", 0, 1), ("w_mem_kv", 0, DEPTH)]
_GATHER_REST = [("ffn1_w_gu", 1, DEPTH), ("ffn1_w_down", 1, DEPTH), ("ffn2_w_gu", 1, DEPTH),
                ("ffn2_w_down", 1, DEPTH), ("w_out", 1, DEPTH), ("a_w_in", 1, N_A), ("a_conv", 1, N_A),
                ("b_w_in", 0, N_B), ("b_w_uq", 0, N_B), ("w_dkv", None, None), ("w_ukv", None, None)]
_REPLICATED = ["ffn1_norm", "mix_norm", "ffn2_norm", "mem_norm", "a_A_log", "a_dt_bias", "a_out_norm", "b_q_norm",
               "kv_in_norm", "kv_lat_norm", "final_norm"]
_WEIGHTS = ["ffn1_norm", "ffn1_w_gu", "ffn1_w_down", "mix_norm", "ffn2_norm", "ffn2_w_gu", "ffn2_w_down", "w_out",
            "mem_norm", "w_mem_kv", "a_w_in", "a_conv", "a_A_log", "a_dt_bias", "a_out_norm", "b_w_in", "b_q_norm",
            "b_w_uq", "kv_in_norm", "w_dkv", "kv_lat_norm", "w_ukv", "final_norm"]


def _full_from_shards(name, sh):
    if name in ("ffn1_w_gu", "ffn2_w_gu"):
        return sh
    if name in ("ffn1_w_down", "ffn2_w_down"):
        return sh.reshape(4, FF_SHARD, D)
    if name in _COL_SHARDED:
        return jnp.moveaxis(sh, 0, -2).reshape(sh.shape[1:-1] + (N_DEV * sh.shape[-1],))
    return sh.reshape((N_DEV * sh.shape[1],) + sh.shape[2:])


def _shards_from_full(name, full):
    if name in ("ffn1_w_gu", "ffn2_w_gu"):
        return full
    if name in ("ffn1_w_down", "ffn2_w_down"):
        return full.reshape(N_DEV, D_FF // N_DEV, D)
    if name in _COL_SHARDED:
        r, cc = full.shape
        return jnp.moveaxis(full.reshape(r, N_DEV, cc // N_DEV), 1, 0)
    return full.reshape((N_DEV, full.shape[0] // N_DEV) + full.shape[1:])


def kernel(x, mem, positions, ffn1_norm, ffn1_w_gu, ffn1_w_down, mix_norm, ffn2_norm, ffn2_w_gu, ffn2_w_down, w_out, mem_norm, w_mem_kv, a_w_in, a_conv, a_A_log, a_dt_bias, a_out_norm, b_w_in, b_q_norm, b_w_uq, kv_in_norm, w_dkv, kv_lat_norm, w_ukv, final_norm, loss_target, m_ffn1_norm, m_ffn1_w_gu, m_ffn1_w_down, m_mix_norm, m_ffn2_norm, m_ffn2_w_gu, m_ffn2_w_down, m_w_out, m_mem_norm, m_w_mem_kv, m_a_w_in, m_a_conv, m_a_A_log, m_a_dt_bias, m_a_out_norm, m_b_w_in, m_b_q_norm, m_b_w_uq, m_kv_in_norm, m_w_dkv, m_kv_lat_norm, m_w_ukv, m_final_norm, v_ffn1_norm, v_ffn1_w_gu, v_ffn1_w_down, v_mix_norm, v_ffn2_norm, v_ffn2_w_gu, v_ffn2_w_down, v_w_out, v_mem_norm, v_w_mem_kv, v_a_w_in, v_a_conv, v_a_A_log, v_a_dt_bias, v_a_out_norm, v_b_w_in, v_b_q_norm, v_b_w_uq, v_kv_in_norm, v_w_dkv, v_kv_lat_norm, v_w_ukv, v_final_norm):
    loc = dict(locals())
    wl = {n: loc[n] for n in _WEIGHTS}
    ml = {n: loc["m_" + n] for n in _WEIGHTS}
    vl = {n: loc["v_" + n] for n in _WEIGHTS}

    me = 4 * lax.axis_index("x") + 2 * lax.axis_index("y") + lax.axis_index("c")
    w = {n: wl[n] for n in _REPLICATED}
    for n in _SHARDED:
        w[n] = [None] * _LAYERED[n] if n in _LAYERED else None
    back = {}

    def src_of(item):
        n, lo, hi = item
        return (wl[n] if lo is None else wl[n][lo:hi]).astype(BF16)

    def install(items, pieces):
        for (n, lo, hi), p in zip(items, pieces):
            if lo is None:
                w[n] = _full_from_shards(n, p)
            else:
                for l in range(lo, hi):
                    w[n][l] = _full_from_shards(n, p[:, l - lo])
        for n, lo, hi in items:
            if n == "a_w_in":
                for l in range(lo, hi):
                    w[n][l], back[(n, l)] = _ext_and_back(_a_in_ext, w[n][l])
            elif n == "a_conv":
                for l in range(lo, hi):
                    w[n][l] = w[n][l].astype(F32)
            elif n == "b_w_uq":
                for l in range(lo, hi):
                    w[n][l], back[(n, l)] = _ext_and_back(_uq_ext, w[n][l])
            elif n == "w_dkv":
                w[n], back[n] = _ext_and_back(_dkv_ext, w[n])
            elif n == "w_ukv":
                w[n], back[n] = _ext_and_back(_ukv_ext, w[n])

    install(_GATHER_FIRST, _exchange([src_of(it) for it in _GATHER_FIRST], True, "gather_first"))
    rest_srcs = [src_of(it) for it in _GATHER_REST]
    flight = _gather_start(rest_srcs, "gather_rest_start")
    w["ffn1_norm"] = w["ffn1_norm"] + flight[-1][0, 0]

    def layer_start(l, stream):
        if l == 1:
            lands = _gather_wait(flight, stream, "gather_rest_wait")
            own = [lax.dynamic_update_slice(ld, s[None], (me,) + (0,) * s.ndim) for ld, s in zip(lands, rest_srcs)]
            install(_GATHER_REST, own)

    loss, dx, g = _local_step(x[0], mem[0], positions[0], loss_target[0], w, layer_start)
    loss = lax.psum(loss, ("x", "y", "c"))

    for l in range(N_A):
        g["a_w_in"][l] = back[("a_w_in", l)](g["a_w_in"][l])
    for j in range(N_B):
        g["b_w_uq"][j] = back[("b_w_uq", j)](g["b_w_uq"][j])
    g["w_dkv"] = back["w_dkv"](g["w_dkv"])
    g["w_ukv"] = back["w_ukv"](g["w_ukv"])

    gsh = []
    for n in _SHARDED:
        if n in _LAYERED:
            gsh.append(jnp.stack([_shards_from_full(n, g[n][l]).astype(BF16) for l in range(_LAYERED[n])], axis=1))
        else:
            gsh.append(_shards_from_full(n, g[n]).astype(BF16))
    parts = _exchange(gsh, False, "scatter_grads")
    out = {}
    for n, p in zip(_SHARDED, parts):
        shape = wl[n].shape
        res = _reduce_adamw(p.reshape(N_DEV, -1, shape[-1]), _as2d(wl[n]), _as2d(ml[n]), _as2d(vl[n]), "adamw_" + n)
        for kind, buf in zip(("grad", "delta", "new_m", "new_v"), res):
            out[(kind, n)] = buf.reshape(shape)

    rep_shapes = [wl[n].shape for n in _REPLICATED]
    grep = [jnp.stack(g[n]) if isinstance(g[n], list) else g[n] for n in _REPLICATED]
    (rparts,) = _exchange([_pack(grep)], True, "gather_small_grads")
    res = _reduce_adamw(rparts, _pack([wl[n] for n in _REPLICATED]), _pack([ml[n] for n in _REPLICATED]),
                        _pack([vl[n] for n in _REPLICATED]), "adamw_replicated")
    for kind, buf in zip(("grad", "delta", "new_m", "new_v"), res):
        for n, a in zip(_REPLICATED, _unpack(buf, rep_shapes)):
            out[(kind, n)] = a

    return (loss, dx[None], *[out[("grad", n)] for n in _WEIGHTS], *[out[("delta", n)] for n in _WEIGHTS],
            *[out[("new_m", n)] for n in _WEIGHTS], *[out[("new_v", n)] for n in _WEIGHTS])
```

```python
import functools

import numpy as np
import jax
import jax.numpy as jnp
from jax import lax
from jax.experimental import pallas as pl
from jax.experimental.pallas import tpu as pltpu

F32 = jnp.float32
BF16 = jnp.bfloat16

N_DEV = 8
D = 1024
D_FF = 2816
FF_SHARD = 2 * D_FF // N_DEV
DEPTH = 4
N_A = 2
N_B = 2
EPS = 1e-6
CHUNK = 64
GROUP = 256
GDN_HPS = 3
A_HEADS = 6
HD = 128
A_WIDTH = A_HEADS * HD
B_HEADS = 6
QK_NOPE = 128
QK_ROPE = 64
QK_CAT = 256
Q_LORA = 256
KV_LORA = 256
MEM_HEADS = 4
MEM_HD = 64
MEM_W = 256
N_MEM = 256
ROPE_THETA = 10000.0
ATT_SCALE = (QK_NOPE + QK_ROPE) ** -0.5
LN2 = 0.6931471805599453
Q_PRESCALE = ATT_SCALE / LN2
A_IN = 4 * A_WIDTH + 2 * A_HEADS + MEM_W
A_MQ_BLK = 4 * A_WIDTH // MEM_W
A_BA_BLK = (4 * A_WIDTH + MEM_W) // 128

ADAM_LR = 0.001
ADAM_B1 = 0.9
ADAM_B2 = 0.999
ADAM_EPS = 1e-08
ADAM_WD = 0.01
ADAM_STEP = 10

VMEM_LIMIT = 56 * 1024 * 1024
TM = 512
TMM = 1024
ATT_BQ = 1024
PACK_W = 1024
PACK_ROWS = 32


def _cparams(sem):
    return pltpu.CompilerParams(dimension_semantics=sem, vmem_limit_bytes=VMEM_LIMIT)


_DIMS = {"nn": ((1,), (0,)), "nt": ((1,), (1,)), "tn": ((0,), (0,))}


def _dot(a, b, dims="nn"):
    return lax.dot_general(a.astype(BF16), b.astype(BF16), (_DIMS[dims], ((), ())),
                           preferred_element_type=F32)


def _matmul(a, b, *, dims, grid, a_spec, b_spec, o_spec, out_shape, name, scale=1.0,
            res=None, res_spec=None, a_fn=None, epilogue=None, acc_shape=None):
    nk = grid[-1]
    kax = len(grid) - 1
    if acc_shape is None:
        acc_shape = tuple(s for s in o_spec.block_shape if s is not None)

    def body(*refs):
        if res is None:
            a_ref, b_ref, o_ref, acc = refs
            r_ref = None
        else:
            a_ref, b_ref, r_ref, o_ref, acc = refs
        k = pl.program_id(kax)

        @pl.when(k == 0)
        def _():
            acc[...] = jnp.zeros_like(acc)

        a_blk = a_ref[...] if a_fn is None else a_fn(a_ref[...])
        acc[...] += _dot(a_blk, b_ref[...], dims)

        @pl.when(k == nk - 1)
        def _():
            y = acc[...] * scale
            if epilogue is not None:
                y = epilogue(y, r_ref[...])
            elif r_ref is not None:
                y = y + r_ref[...].astype(F32)
            o_ref[...] = y.astype(o_ref.dtype)

    args = [a, b] + ([res] if res is not None else [])
    specs = [a_spec, b_spec] + ([res_spec] if res is not None else [])
    sem = ("parallel",) * kax + ("arbitrary",)
    return pl.pallas_call(
        body, out_shape=out_shape, grid=grid, in_specs=specs, out_specs=o_spec,
        scratch_shapes=[pltpu.VMEM(acc_shape, F32)], name=name, compiler_params=_cparams(sem))(*args)


def _tile(n, cap):
    if n <= cap:
        return n
    t = cap - cap % 128
    while t >= 128:
        if n % t == 0:
            return t
        t -= 128
    raise ValueError(f"no tile for {n}")


def _mm(a, b, dims, out_dtype, name, scale=1.0, res=None):
    if dims == "tn":
        kk, m = a.shape
        n = b.shape[1]
        tk, tn = _tile(kk, TMM), _tile(n, 1152)
        return _matmul(a, b, dims=dims, grid=(1, n // tn, kk // tk),
                       a_spec=pl.BlockSpec((tk, m), lambda i, j, k: (k, 0)),
                       b_spec=pl.BlockSpec((tk, tn), lambda i, j, k: (k, j)),
                       o_spec=pl.BlockSpec((m, tn), lambda i, j, k: (0, j)),
                       out_shape=jax.ShapeDtypeStruct((m, n), out_dtype), name=name, scale=scale)
    m, kk = a.shape
    n = b.shape[1] if dims == "nn" else b.shape[0]
    tm, tn, tk = _tile(m, TMM), _tile(n, 1152), _tile(kk, 1536)
    if dims == "nn":
        b_spec = pl.BlockSpec((tk, tn), lambda i, j, k: (k, j))
    else:
        b_spec = pl.BlockSpec((tn, tk), lambda i, j, k: (j, k))
    o_spec = pl.BlockSpec((tm, tn), lambda i, j, k: (i, j))
    return _matmul(a, b, dims=dims, grid=(m // tm, n // tn, kk // tk),
                   a_spec=pl.BlockSpec((tm, tk), lambda i, j, k: (i, k)), b_spec=b_spec, o_spec=o_spec,
                   out_shape=jax.ShapeDtypeStruct((m, n), out_dtype), name=name, scale=scale,
                   res=res, res_spec=o_spec if res is not None else None)


def _rowcall(fn, args, in_specs, out_shapes, out_specs, grid, name, n_acc=0):
    n_in, n_out = len(args), len(out_shapes)

    def body(*refs):
        outs = fn(*[r[...] for r in refs[:n_in]])
        if not isinstance(outs, (tuple, list)):
            outs = (outs,)
        first = pl.program_id(0) == 0
        for ax in range(1, len(grid)):
            first = jnp.logical_and(first, pl.program_id(ax) == 0)
        for idx, (o_ref, val) in enumerate(zip(refs[n_in:], outs)):
            if idx >= n_out - n_acc:
                @pl.when(first)
                def _(o_ref=o_ref):
                    o_ref[...] = jnp.zeros_like(o_ref)

                o_ref[...] += val.astype(o_ref.dtype)
            else:
                o_ref[...] = val.astype(o_ref.dtype)

    sem = (("arbitrary",) if n_acc else ("parallel",)) * len(grid)
    res = pl.pallas_call(body, out_shape=tuple(out_shapes), grid=grid, in_specs=list(in_specs),
                         out_specs=tuple(out_specs), name=name, compiler_params=_cparams(sem))(*args)
    return res


def _vjp_fn(fn, n_in, wrt):
    def bwd(*blocks):
        ins = [b.astype(F32) for b in blocks[:n_in]]
        cts = [c.astype(F32) for c in blocks[n_in:]]
        outs, vjp = jax.vjp(fn, *ins)
        if isinstance(outs, (tuple, list)):
            grads = vjp(tuple(cts))
        else:
            grads = vjp(cts[0])
        return tuple(grads[i] for i in wrt)
    return bwd


def _sds(shape, dtype):
    return jax.ShapeDtypeStruct(tuple(shape), dtype)


def _rows(tm, w, col=0):
    return pl.BlockSpec((tm, w), lambda i, *_: (i, col))


def _shared(shape):
    nd = len(shape)
    return pl.BlockSpec(tuple(shape), lambda *_: (0,) * nd)


def _rms(x, g):
    return x * lax.rsqrt(jnp.mean(x * x, axis=-1, keepdims=True) + EPS) * g


def _silu(x):
    return x * jax.nn.sigmoid(x)


def _swiglu_pair(gu):
    return _silu(gu[0].astype(F32)) * gu[1].astype(F32)


def _swiglu_bwd(dh, gu):
    g, u = gu[0].astype(F32), gu[1].astype(F32)
    sg = jax.nn.sigmoid(g)
    return jnp.stack([dh * u * sg * (1.0 + g * (1.0 - sg)), dh * g * sg])


def _gdn_prep(q, k, v):
    q, k, v = _silu(q), _silu(k), _silu(v)
    q = q * lax.rsqrt(jnp.sum(q * q, axis=-1, keepdims=True) + EPS) * (HD ** -0.5)
    k = k * lax.rsqrt(jnp.sum(k * k, axis=-1, keepdims=True) + EPS)
    return q, k, v


def _gates(ba, a_log, dt_bias):
    lane = lax.broadcasted_iota(jnp.int32, ba.shape, 1)
    beta = jax.nn.sigmoid(ba)
    z = ba + dt_bias
    softplus = jnp.maximum(z, 0.0) + jnp.log(1.0 + jnp.exp(-jnp.abs(z)))
    g = -jnp.exp(a_log) * softplus
    return jnp.where(lane < A_HEADS, beta, jnp.where(lane < 2 * A_HEADS, g, 0.0))


def _outnorm_gate(o, gate, gain):
    return _rms(o, gain) * _silu(gate)


def _memattn(q, k, v):
    lane = lax.shift_right_logical(lax.broadcasted_iota(jnp.int32, (1, MEM_W), 1), 6)
    out = jnp.zeros(q.shape, F32)
    for h in range(MEM_HEADS):
        mh = (lane == h).astype(F32)
        s = _dot(q * mh, k, "nt") * (MEM_HD ** -0.5)
        s = s - lax.stop_gradient(jnp.max(s, axis=-1, keepdims=True))
        p = jnp.exp(s)
        p = p / jnp.sum(p, axis=-1, keepdims=True)
        out = out + _dot(p, v * mh)
    return out


def _rope_mix(a, a_sw, c, s):
    return (a * c + a_sw * s) * Q_PRESCALE


def _kcat(kn, kr, kr_sw, c, s):
    return kn + kr * c + kr_sw * s


def _gdn_local(q, k, v, beta, gcol, grow):
    n = GROUP
    ri = lax.broadcasted_iota(jnp.int32, (n, n), 0)
    ci = lax.broadcasted_iota(jnp.int32, (n, n), 1)
    same = lax.shift_right_logical(ri, 6) == lax.shift_right_logical(ci, 6)
    lower = jnp.logical_and(same, ci <= ri)
    strict = jnp.logical_and(same, ci < ri)
    gc_col = jnp.sum(lower.astype(F32) * grow, axis=1, keepdims=True)
    gc_row = jnp.sum(jnp.logical_and(same, ri <= ci).astype(F32) * gcol, axis=0, keepdims=True)
    glast = jnp.sum(same.astype(F32) * grow, axis=1, keepdims=True)
    decay = jnp.where(lower, jnp.exp(jnp.where(lower, gc_col - gc_row, 0.0)), 0.0)
    kb = k * beta
    nmat = -jnp.where(strict, _dot(kb, k, "nt") * decay, 0.0)
    pinv = (ri == ci).astype(F32) + nmat
    npow = nmat
    for _ in range(5):
        npow = _dot(npow, npow)
        pinv = pinv + _dot(pinv, npow)
    e_gc = jnp.exp(gc_col)
    u = _dot(pinv, v * beta)
    w = _dot(pinv, kb * e_gc)
    qk = _dot(q, k, "nt") * decay
    fold = (jnp.bitwise_and(lax.broadcasted_iota(jnp.int32, (n, CHUNK), 0), CHUNK - 1)
            == lax.broadcasted_iota(jnp.int32, (n, CHUNK), 1)).astype(F32)
    qk_c = _dot(qk, fold)
    q_dec = q * e_gc
    k_dec = k * jnp.exp(glast - gc_col)
    dmat = jnp.exp(glast) * jnp.ones((1, HD), F32)
    return u, w, q_dec, k_dec, qk_c, dmat


def _gdn_step(s, w_c, u_c, qd_c, kd_c, qk_c, d_c):
    v_new = u_c - _dot(w_c, s)
    out = _dot(qd_c, s) + _dot(qk_c, v_new)
    d_row = jnp.mean(d_c, axis=0, keepdims=True)
    s_new = s * d_row + _dot(kd_c, v_new, "tn")
    return s_new, out


def _rmsnorm_fwd(x, gain, name, col=0, width=None):
    t = x.shape[0]
    w = width or x.shape[1]
    (n,) = _rowcall(_rms, [x, gain.reshape(1, w)], [_rows(TM, w, col), _shared((1, w))],
                    [_sds((t, w), BF16)], [_rows(TM, w)], (t // TM,), name)
    return n


def _rmsnorm_bwd(x, gain, dn, dres, name):
    t, w = x.shape
    fn = _vjp_fn(_rms, 2, (0, 1))

    def bwd(xb, gb, dnb, drb):
        dx, dg = fn(xb, gb, dnb)
        return dx + drb, dg

    dx, dg = _rowcall(bwd, [x, gain.reshape(1, w), dn, dres],
                      [_rows(TM, w), _shared((1, w)), _rows(TM, w), _rows(TM, w)],
                      [_sds((t, w), F32), _sds((1, w), F32)], [_rows(TM, w), _shared((1, w))],
                      (t // TM,), name, n_acc=1)
    return dx, dg[0]


def _ffn_fwd(x, gain, wgu8, wd4, tag):
    t = x.shape[0]
    nt = t // TMM
    n = _rmsnorm_fwd(x, gain, tag + "_norm")
    gu = _matmul(n, wgu8, dims="nn", grid=(N_DEV, nt, 1),
                 a_spec=pl.BlockSpec((TMM, D), lambda j, i, k: (i, 0)),
                 b_spec=pl.BlockSpec((None, D, FF_SHARD), lambda j, i, k: (j, 0, 0)),
                 o_spec=pl.BlockSpec((None, TMM, FF_SHARD), lambda j, i, k: (j, i, 0)),
                 out_shape=_sds((N_DEV, t, FF_SHARD), BF16), name=tag + "_gu")
    gu = gu.reshape(2, 4, t, FF_SHARD)
    y = _matmul(gu, wd4, dims="nn", grid=(nt, 1, 4), a_fn=_swiglu_pair,
                a_spec=pl.BlockSpec((2, None, TMM, FF_SHARD), lambda i, j, k: (0, k, i, 0)),
                b_spec=pl.BlockSpec((None, FF_SHARD, D), lambda i, j, k: (k, 0, 0)),
                o_spec=pl.BlockSpec((TMM, D), lambda i, j, k: (i, 0)),
                out_shape=_sds((t, D), F32), name=tag + "_down", scale=0.5,
                res=x, res_spec=pl.BlockSpec((TMM, D), lambda i, j, k: (i, 0)))
    return y, (x, n, gu)


def _ffn_bwd(d, saved, gain, wgu8, wd4, tag):
    x, n, gu = saved
    t = x.shape[0]
    nt = t // TMM
    dgu = _matmul(d, wd4, dims="nt", grid=(4, nt, 1),
                  a_spec=pl.BlockSpec((TMM, D), lambda j, i, k: (i, 0)),
                  b_spec=pl.BlockSpec((None, FF_SHARD, D), lambda j, i, k: (j, 0, 0)),
                  o_spec=pl.BlockSpec((2, None, TMM, FF_SHARD), lambda j, i, k: (0, j, i, 0)),
                  out_shape=_sds((2, 4, t, FF_SHARD), BF16), name=tag + "_dgu", scale=0.5,
                  res=gu, res_spec=pl.BlockSpec((2, None, TMM, FF_SHARD), lambda j, i, k: (0, j, i, 0)),
                  epilogue=_swiglu_bwd, acc_shape=(TMM, FF_SHARD))
    dwd4 = _matmul(gu, d, dims="tn", grid=(4, 1, nt), a_fn=_swiglu_pair,
                   a_spec=pl.BlockSpec((2, None, TMM, FF_SHARD), lambda j, i, k: (0, j, k, 0)),
                   b_spec=pl.BlockSpec((TMM, D), lambda j, i, k: (k, 0)),
                   o_spec=pl.BlockSpec((None, FF_SHARD, D), lambda j, i, k: (j, 0, 0)),
                   out_shape=_sds((4, FF_SHARD, D), F32), name=tag + "_dwd", scale=0.5)
    dgu = dgu.reshape(N_DEV, t, FF_SHARD)
    dwgu8 = _matmul(n, dgu, dims="tn", grid=(N_DEV, 1, nt),
                    a_spec=pl.BlockSpec((TMM, D), lambda j, i, k: (k, 0)),
                    b_spec=pl.BlockSpec((None, TMM, FF_SHARD), lambda j, i, k: (j, k, 0)),
                    o_spec=pl.BlockSpec((None, D, FF_SHARD), lambda j, i, k: (j, 0, 0)),
                    out_shape=_sds((N_DEV, D, FF_SHARD), F32), name=tag + "_dwgu")
    dn = _matmul(dgu, wgu8, dims="nt", grid=(nt, 1, N_DEV),
                 a_spec=pl.BlockSpec((None, TMM, FF_SHARD), lambda i, j, k: (k, i, 0)),
                 b_spec=pl.BlockSpec((None, D, FF_SHARD), lambda i, j, k: (k, 0, 0)),
                 o_spec=pl.BlockSpec((TMM, D), lambda i, j, k: (i, 0)),
                 out_shape=_sds((t, D), BF16), name=tag + "_dn")
    dx, dgain = _rmsnorm_bwd(x, gain, dn, d, tag + "_dnorm")
    return dx, dgain, dwgu8, dwd4


CONV_TC = 768
CONV_K = 4


def _conv_fwd(ha, w, name):
    t = ha.shape[0]
    nb = TM // 8

    def body(prev_ref, cur_ref, w_ref, o_ref):
        i = pl.program_id(0)
        cur = cur_ref[...].astype(F32)
        prev = prev_ref[...].astype(F32) * (i > 0).astype(F32)
        ext = jnp.concatenate([prev, cur], axis=0)
        wv = w_ref[...]
        acc = cur * wv[3:4]
        for k in range(1, CONV_K):
            acc = acc + pltpu.roll(ext, k, axis=0)[8:] * wv[3 - k:4 - k]
        o_ref[...] = acc.astype(o_ref.dtype)

    return pl.pallas_call(
        body, out_shape=_sds((3, t, CONV_TC), BF16), grid=(t // TM, 3),
        in_specs=[pl.BlockSpec((8, CONV_TC), lambda i, c: (jnp.maximum(i * nb - 1, 0), c)),
                  pl.BlockSpec((TM, CONV_TC), lambda i, c: (i, c)),
                  pl.BlockSpec((CONV_K, CONV_TC), lambda i, c: (0, c))],
        out_specs=pl.BlockSpec((None, TM, CONV_TC), lambda i, c: (c, i, 0)),
        name=name, compiler_params=_cparams(("parallel", "parallel")))(ha, ha, w)


def _conv_bwd(ha, w, dy3, name):
    t = ha.shape[0]
    nb = TM // 8
    nt = t // TM

    def body(prev_ref, cur_ref, dy_ref, nxt_ref, w_ref, dx_ref, dw_ref):
        i = pl.program_id(1)
        cur = cur_ref[...].astype(F32)
        prev = prev_ref[...].astype(F32) * (i > 0).astype(F32)
        ext = jnp.concatenate([prev, cur], axis=0)
        dy = dy_ref[...].astype(F32)
        nxt = nxt_ref[...].astype(F32) * (i < nt - 1).astype(F32)
        dext = jnp.concatenate([dy, nxt], axis=0)
        wv = w_ref[...]
        dx = dy * wv[3:4]
        dws = [None] * CONV_K
        dws[3] = jnp.sum(dy * cur, axis=0, keepdims=True)
        for k in range(1, CONV_K):
            dx = dx + pltpu.roll(dext, TM + 8 - k, axis=0)[:TM] * wv[3 - k:4 - k]
            dws[3 - k] = jnp.sum(dy * pltpu.roll(ext, k, axis=0)[8:], axis=0, keepdims=True)
        dx_ref[...] = dx.astype(dx_ref.dtype)

        @pl.when(i == 0)
        def _():
            dw_ref[...] = jnp.zeros_like(dw_ref)

        dw_ref[...] += jnp.concatenate(dws, axis=0)

    return pl.pallas_call(
        body, out_shape=(_sds((t, 3 * CONV_TC), BF16), _sds((CONV_K, 3 * CONV_TC), F32)), grid=(3, nt),
        in_specs=[pl.BlockSpec((8, CONV_TC), lambda c, i: (jnp.maximum(i * nb - 1, 0), c)),
                  pl.BlockSpec((TM, CONV_TC), lambda c, i: (i, c)),
                  pl.BlockSpec((None, TM, CONV_TC), lambda c, i: (c, i, 0)),
                  pl.BlockSpec((None, 8, CONV_TC), lambda c, i: (c, jnp.minimum((i + 1) * nb, t // 8 - 1), 0)),
                  pl.BlockSpec((CONV_K, CONV_TC), lambda c, i: (0, c))],
        out_specs=(pl.BlockSpec((TM, CONV_TC), lambda c, i: (i, c)),
                   pl.BlockSpec((CONV_K, CONV_TC), lambda c, i: (0, c))),
        name=name, compiler_params=_cparams(("parallel", "arbitrary")))(ha, ha, dy3, dy3, w)


def _gdn_specs(t, rev):
    ng = t // GROUP

    def gi(g):
        return ng - 1 - g if rev else g

    qkv = pl.BlockSpec((3, GROUP, GDN_HPS * HD), lambda h, g: (0, gi(g), h))
    bg = pl.BlockSpec((GROUP, 128), lambda h, g: (gi(g), 0))
    dbg = pl.BlockSpec((GDN_HPS, GROUP, 128), lambda h, g: (h, gi(g), 0))
    o = pl.BlockSpec((GROUP, GDN_HPS * HD), lambda h, g: (gi(g), h))
    st = pl.BlockSpec((GDN_HPS, None, HD, HD), lambda h, g: (h, gi(g), 0, 0))
    return qkv, bg, dbg, o, st


def _head_qkv(qkv_ref, j):
    sl = slice(j * HD, (j + 1) * HD)
    return qkv_ref[0, :, sl].astype(F32), qkv_ref[1, :, sl].astype(F32), qkv_ref[2, :, sl].astype(F32)


def _head_gates(bg, h):
    lane = lax.broadcasted_iota(jnp.int32, (1, 128), 1)
    beta = jnp.sum(jnp.where(lane == h, bg, 0.0), axis=1, keepdims=True)
    gcol = jnp.sum(jnp.where(lane == h + A_HEADS, bg, 0.0), axis=1, keepdims=True)
    return beta, gcol, _col_to_row(gcol)


def _gdn_fwd(qkv3, bg, name):
    t = qkv3.shape[1]
    ng = t // GROUP
    qkv_s, bg_s, _, o_s, st_s = _gdn_specs(t, False)

    def body(qkv_ref, bg_ref, o_ref, st_ref, s_scr):
        @pl.when(pl.program_id(1) == 0)
        def _():
            s_scr[...] = jnp.zeros_like(s_scr)

        st_ref[...] = s_scr[...]
        bgv = bg_ref[...]
        loc = [_gdn_local(*_head_qkv(qkv_ref, j), *_head_gates(bgv, pl.program_id(0) * GDN_HPS + j))
               for j in range(GDN_HPS)]
        s = [s_scr[j] for j in range(GDN_HPS)]
        for a in range(GROUP // CHUNK):
            sl = slice(a * CHUNK, (a + 1) * CHUNK)
            for j in range(GDN_HPS):
                u, w, qd, kd, qkc, dm = loc[j]
                s[j], out = _gdn_step(s[j], w[sl], u[sl], qd[sl], kd[sl], qkc[sl], dm[sl])
                o_ref[sl, j * HD:(j + 1) * HD] = out.astype(o_ref.dtype)
        for j in range(GDN_HPS):
            s_scr[j] = s[j]

    return pl.pallas_call(
        body, out_shape=(_sds((t, A_WIDTH), BF16), _sds((A_HEADS, ng, HD, HD), F32)),
        grid=(A_HEADS // GDN_HPS, ng), in_specs=[qkv_s, bg_s], out_specs=(o_s, st_s),
        scratch_shapes=[pltpu.VMEM((GDN_HPS, HD, HD), F32)], name=name,
        compiler_params=_cparams(("parallel", "arbitrary")))(qkv3, bg)


def _gdn_bwd(qkv3, bg, states, do, name):
    t = qkv3.shape[1]
    ng = t // GROUP
    qkv_s, bg_s, dbg_s, o_s, st_s = _gdn_specs(t, True)
    nc = GROUP // CHUNK

    def body(qkv_ref, bg_ref, st_ref, do_ref, dqkv_ref, dbg_ref, ds_scr):
        @pl.when(pl.program_id(1) == 0)
        def _():
            ds_scr[...] = jnp.zeros_like(ds_scr)

        heads = range(GDN_HPS)
        bgv = bg_ref[...]
        hid = [pl.program_id(0) * GDN_HPS + j for j in heads]
        fw = [jax.vjp(_gdn_local, *_head_qkv(qkv_ref, j), *_head_gates(bgv, hid[j])) for j in heads]
        starts = [[None] * nc for _ in heads]
        s = [st_ref[j] for j in heads]
        for a in range(nc):
            sl = slice(a * CHUNK, (a + 1) * CHUNK)
            for j in heads:
                u, w, qd, kd, qkc, dm = fw[j][0]
                starts[j][a] = s[j]
                if a < nc - 1:
                    s[j], _ = _gdn_step(s[j], w[sl], u[sl], qd[sl], kd[sl], qkc[sl], dm[sl])
        ds = [ds_scr[j] for j in heads]
        parts = [[None] * nc for _ in heads]
        for a in reversed(range(nc)):
            sl = slice(a * CHUNK, (a + 1) * CHUNK)
            for j in heads:
                u, w, qd, kd, qkc, dm = fw[j][0]
                _, vjp_step = jax.vjp(_gdn_step, starts[j][a], w[sl], u[sl], qd[sl], kd[sl], qkc[sl], dm[sl])
                grads = vjp_step((ds[j], do_ref[sl, j * HD:(j + 1) * HD].astype(F32)))
                ds[j] = grads[0]
                parts[j][a] = grads[1:]
        lane = lax.broadcasted_iota(jnp.int32, (1, 128), 1)
        for j in heads:
            ds_scr[j] = ds[j]
            dw, du, dqd, dkd, dqk, ddm = [jnp.concatenate([parts[j][a][i] for a in range(nc)], axis=0)
                                          for i in range(6)]
            dq, dk, dv, db, dgc, dgr = fw[j][1]((du, dw, dqd, dkd, dqk, ddm))
            hs = slice(j * HD, (j + 1) * HD)
            dqkv_ref[0, :, hs] = dq.astype(dqkv_ref.dtype)
            dqkv_ref[1, :, hs] = dk.astype(dqkv_ref.dtype)
            dqkv_ref[2, :, hs] = dv.astype(dqkv_ref.dtype)
            dbg_ref[j] = (jnp.where(lane == hid[j], db, 0.0)
                          + jnp.where(lane == hid[j] + A_HEADS, dgc + _row_to_col(dgr), 0.0))

    return pl.pallas_call(
        body, out_shape=(_sds((3, t, A_WIDTH), BF16), _sds((A_HEADS, t, 128), F32)),
        grid=(A_HEADS // GDN_HPS, ng), in_specs=[qkv_s, bg_s, st_s, o_s],
        out_specs=(qkv_s, dbg_s), scratch_shapes=[pltpu.VMEM((GDN_HPS, HD, HD), F32)], name=name,
        compiler_params=_cparams(("parallel", "arbitrary")))(qkv3, bg, states, do)


NEG = -1e30


def _diag_mask(shape, q_axis):
    qi = lax.shift_right_logical(lax.broadcasted_iota(jnp.int32, shape, q_axis), 6)
    ki = lax.shift_right_logical(lax.broadcasted_iota(jnp.int32, shape, 1 - q_axis), 6)
    return ki <= qi


def _col_to_row(col):
    n = col.shape[0]
    eye = lax.broadcasted_iota(jnp.int32, (n, n), 0) == lax.broadcasted_iota(jnp.int32, (n, n), 1)
    return jnp.sum(jnp.where(eye, col, 0.0), axis=0, keepdims=True)


def _row_to_col(row):
    n = row.shape[1]
    eye = lax.broadcasted_iota(jnp.int32, (n, n), 0) == lax.broadcasted_iota(jnp.int32, (n, n), 1)
    return jnp.sum(jnp.where(eye, row, 0.0), axis=1, keepdims=True)


def _blk(ref, i):
    return ref[pl.ds(pl.multiple_of(i * ATT_BQ, ATT_BQ), ATT_BQ), :]


def _att_fwd(qc, kc, v, name):
    t = qc.shape[0]
    nq = t // ATT_BQ

    def body(q_ref, k_ref, v_ref, o_ref, lse_ref, lser_ref, m_scr, l_scr, acc_scr):
        qb = pl.program_id(1)
        q = q_ref[...]
        m_scr[...] = jnp.full_like(m_scr, NEG)
        l_scr[...] = jnp.zeros_like(l_scr)
        acc_scr[...] = jnp.zeros_like(acc_scr)

        def step(kb, diag):
            s = _dot(q, _blk(k_ref, kb), "nt")
            if diag:
                s = jnp.where(_diag_mask(s.shape, 0), s, NEG)
            m_old = m_scr[...]
            m_new = jnp.maximum(m_old, jnp.max(s, axis=1, keepdims=True))
            alpha = jnp.exp2(m_old - m_new)
            p = jnp.exp2(s - m_new)
            l_scr[...] = alpha * l_scr[...] + jnp.sum(p, axis=1, keepdims=True)
            acc_scr[...] = alpha * acc_scr[...] + _dot(p, _blk(v_ref, kb))
            m_scr[...] = m_new

        def loop_body(kb, carry):
            step(kb, False)
            return carry

        lax.fori_loop(0, qb, loop_body, 0)
        step(qb, True)
        o_ref[...] = (acc_scr[...] / l_scr[...]).astype(o_ref.dtype)
        lse = m_scr[...] + jnp.log2(l_scr[...])
        lse_ref[...] = lse
        lser_ref[...] = _col_to_row(lse)

    return pl.pallas_call(
        body, out_shape=(_sds((t, B_HEADS * HD), BF16), _sds((B_HEADS, t, 1), F32),
                         _sds((B_HEADS, nq, 1, ATT_BQ), F32)), grid=(B_HEADS, nq),
        in_specs=[pl.BlockSpec((ATT_BQ, QK_CAT), lambda h, i: (i, h)),
                  pl.BlockSpec((t, QK_CAT), lambda h, i: (0, h)), pl.BlockSpec((t, HD), lambda h, i: (0, h))],
        out_specs=(pl.BlockSpec((ATT_BQ, HD), lambda h, i: (i, h)),
                   pl.BlockSpec((None, ATT_BQ, 1), lambda h, i: (h, i, 0)),
                   pl.BlockSpec((None, None, 1, ATT_BQ), lambda h, i: (h, i, 0, 0))),
        scratch_shapes=[pltpu.VMEM((ATT_BQ, 1), F32), pltpu.VMEM((ATT_BQ, 1), F32), pltpu.VMEM((ATT_BQ, HD), F32)],
        name=name, compiler_params=_cparams(("parallel", "arbitrary")))(qc, kc, v)


def _att_bwd(qc, kc, v, o, lse, lse_row, do, name):
    t = qc.shape[0]
    nq = t // ATT_BQ

    def delta_fn(ob, dob):
        dl = jnp.sum(ob.astype(F32) * dob.astype(F32), axis=1, keepdims=True)
        return dl, _col_to_row(dl)

    delta, delta_row = _rowcall(
        delta_fn, [o, do], [pl.BlockSpec((ATT_BQ, HD), lambda i, h: (i, h))] * 2,
        [_sds((B_HEADS, t, 1), F32), _sds((B_HEADS, nq, 1, ATT_BQ), F32)],
        [pl.BlockSpec((None, ATT_BQ, 1), lambda i, h: (h, i, 0)),
         pl.BlockSpec((None, None, 1, ATT_BQ), lambda i, h: (h, i, 0, 0))], (nq, B_HEADS), name + "_delta")

    def dq_body(q_ref, k_ref, v_ref, do_ref, lse_ref, dl_ref, dq_ref, acc):
        qb = pl.program_id(1)
        q, dob, lse_b, dl_b = q_ref[...], do_ref[...], lse_ref[...], dl_ref[...]
        acc[...] = jnp.zeros_like(acc)

        def step(kb, diag):
            k = _blk(k_ref, kb)
            s = _dot(q, k, "nt")
            if diag:
                s = jnp.where(_diag_mask(s.shape, 0), s, NEG)
            p = jnp.exp2(s - lse_b)
            ds = p * (_dot(dob, _blk(v_ref, kb), "nt") - dl_b)
            acc[...] += _dot(ds, k)

        def loop_body(kb, carry):
            step(kb, False)
            return carry

        lax.fori_loop(0, qb, loop_body, 0)
        step(qb, True)
        dq_ref[...] = (acc[...] * ATT_SCALE).astype(dq_ref.dtype)

    qmap = lambda h, i: (i, h)
    colq = pl.BlockSpec((None, ATT_BQ, 1), lambda h, i: (h, i, 0))
    dq = pl.pallas_call(
        dq_body, out_shape=_sds((t, B_HEADS * QK_CAT), BF16), grid=(B_HEADS, nq),
        in_specs=[pl.BlockSpec((ATT_BQ, QK_CAT), qmap), pl.BlockSpec((t, QK_CAT), lambda h, i: (0, h)),
                  pl.BlockSpec((t, HD), lambda h, i: (0, h)), pl.BlockSpec((ATT_BQ, HD), qmap), colq, colq],
        out_specs=pl.BlockSpec((ATT_BQ, QK_CAT), qmap),
        scratch_shapes=[pltpu.VMEM((ATT_BQ, QK_CAT), F32)], name=name + "_dq",
        compiler_params=_cparams(("parallel", "arbitrary")))(qc, kc, v, do, lse, delta)

    def dkv_body(k_ref, v_ref, q_ref, do_ref, lser_ref, dlr_ref, dk_ref, dv_ref, dk_acc, dv_acc):
        kb = pl.program_id(1)
        k, vv = k_ref[...], v_ref[...]
        dk_acc[...] = jnp.zeros_like(dk_acc)
        dv_acc[...] = jnp.zeros_like(dv_acc)

        def step(qb, diag):
            q, dob = _blk(q_ref, qb), _blk(do_ref, qb)
            st = _dot(k, q, "nt")
            if diag:
                st = jnp.where(_diag_mask(st.shape, 1), st, NEG)
            pt = jnp.exp2(st - lser_ref[qb])
            dst = pt * (_dot(vv, dob, "nt") - dlr_ref[qb])
            dv_acc[...] += _dot(pt, dob)
            dk_acc[...] += _dot(dst, q)

        def loop_body(qb, carry):
            step(qb, False)
            return carry

        step(kb, True)
        lax.fori_loop(kb + 1, nq, loop_body, 0)
        dk_ref[...] = (dk_acc[...] * LN2).astype(dk_ref.dtype)
        dv_ref[...] = dv_acc[...].astype(dv_ref.dtype)

    kmap = lambda h, j: (j, h)
    rowq = pl.BlockSpec((None, nq, 1, ATT_BQ), lambda h, j: (h, 0, 0, 0))
    dk, dv = pl.pallas_call(
        dkv_body, out_shape=(_sds((t, B_HEADS * QK_CAT), BF16), _sds((t, B_HEADS * HD), BF16)),
        grid=(B_HEADS, nq),
        in_specs=[pl.BlockSpec((ATT_BQ, QK_CAT), kmap), pl.BlockSpec((ATT_BQ, HD), kmap),
                  pl.BlockSpec((t, QK_CAT), lambda h, j: (0, h)), pl.BlockSpec((t, HD), lambda h, j: (0, h)),
                  rowq, rowq],
        out_specs=(pl.BlockSpec((ATT_BQ, QK_CAT), kmap), pl.BlockSpec((ATT_BQ, HD), kmap)),
        scratch_shapes=[pltpu.VMEM((ATT_BQ, QK_CAT), F32), pltpu.VMEM((ATT_BQ, HD), F32)], name=name + "_dkv",
        compiler_params=_cparams(("parallel", "arbitrary")))(kc, v, qc, do, lse_row, delta_row)
    return dq, dk, dv


def _mem_fwd(hx, col, mkv, name):
    t = hx.shape[0]
    (o,) = _rowcall(_memattn, [hx, mkv, mkv],
                    [_rows(TM, MEM_W, col), pl.BlockSpec((N_MEM, MEM_W), lambda i: (0, 0)),
                     pl.BlockSpec((N_MEM, MEM_W), lambda i: (0, 1))],
                    [_sds((t, MEM_W), BF16)], [_rows(TM, MEM_W)], (t // TM,), name)
    return o


def _mem_bwd(hx, col, mkv, do, do_col, name):
    t = hx.shape[0]
    dq, dk, dv = _rowcall(_vjp_fn(_memattn, 3, (0, 1, 2)), [hx, mkv, mkv, do],
                          [_rows(TM, MEM_W, col), pl.BlockSpec((N_MEM, MEM_W), lambda i: (0, 0)),
                           pl.BlockSpec((N_MEM, MEM_W), lambda i: (0, 1)), _rows(TM, MEM_W, do_col)],
                          [_sds((t, MEM_W), BF16), _sds((N_MEM, MEM_W), F32), _sds((N_MEM, MEM_W), F32)],
                          [_rows(TM, MEM_W), _shared((N_MEM, MEM_W)), _shared((N_MEM, MEM_W))],
                          (t // TM,), name, n_acc=2)
    return dq, jnp.concatenate([dk, dv], axis=1)


def _a_in_ext(w):
    nb = 4 * A_WIDTH
    ba = jnp.pad(w[:, nb:nb + 2 * A_HEADS], ((0, 0), (0, 128 - 2 * A_HEADS)))
    return jnp.concatenate([w[:, :nb], w[:, nb + 2 * A_HEADS:], ba], axis=1)


def _swap_halves(w):
    return jnp.concatenate([w[..., QK_ROPE // 2:], w[..., :QK_ROPE // 2]], axis=-1)


def _uq_ext(w):
    w = w.reshape(Q_LORA, B_HEADS, QK_NOPE + QK_ROPE)
    nope, rope = w[..., :QK_NOPE], w[..., QK_NOPE:]
    z64 = jnp.zeros((Q_LORA, B_HEADS, QK_CAT - QK_NOPE - QK_ROPE), w.dtype)
    z128 = jnp.zeros((Q_LORA, B_HEADS, QK_NOPE), w.dtype)
    a = jnp.concatenate([nope, rope, z64], axis=-1).reshape(Q_LORA, B_HEADS * QK_CAT)
    b = jnp.concatenate([z128, _swap_halves(rope), z64], axis=-1).reshape(Q_LORA, B_HEADS * QK_CAT)
    return jnp.concatenate([a, b], axis=1)


def _dkv_ext(w):
    ckv, kr = w[:, :KV_LORA], w[:, KV_LORA:]
    z128 = jnp.zeros((D, QK_NOPE), w.dtype)
    z64 = jnp.zeros((D, QK_CAT - QK_NOPE - QK_ROPE), w.dtype)
    return jnp.concatenate([ckv, z128, kr, z64, z128, _swap_halves(kr), z64], axis=1)


def _ukv_ext(w):
    w = w.reshape(KV_LORA, B_HEADS, QK_NOPE + HD)
    kn, vv = w[..., :QK_NOPE], w[..., QK_NOPE:]
    z = jnp.zeros((KV_LORA, B_HEADS, QK_CAT - QK_NOPE), w.dtype)
    a = jnp.concatenate([kn, z], axis=-1).reshape(KV_LORA, B_HEADS * QK_CAT)
    return jnp.concatenate([a, vv.reshape(KV_LORA, B_HEADS * HD)], axis=1)


def _ext_and_back(fn, w):
    ext, back = jax.vjp(fn, w.astype(F32))
    return ext.astype(BF16), lambda g: back(g.astype(F32))[0]


def _rope_tables(pos_col):
    t = pos_col.shape[0]
    inv = (ROPE_THETA ** (-np.arange(0, QK_ROPE, 2, dtype=np.float32) / QK_ROPE)).astype(np.float32)
    inv_row = np.zeros((1, QK_CAT), np.float32)
    inv_row[0, QK_NOPE:QK_NOPE + QK_ROPE] = np.concatenate([inv, inv])
    sign = np.zeros((1, QK_CAT), np.float32)
    sign[0, QK_NOPE:QK_NOPE + QK_ROPE // 2] = -1.0
    sign[0, QK_NOPE + QK_ROPE // 2:QK_NOPE + QK_ROPE] = 1.0
    is_rope = np.abs(sign)
    is_nope = np.zeros((1, QK_CAT), np.float32)
    is_nope[0, :QK_NOPE] = 1.0

    def fn(p, inv_b, sign_b, rope_b, nope_b):
        ang = p.astype(F32) * inv_b
        return jnp.cos(ang) * rope_b + nope_b, jnp.sin(ang) * sign_b

    consts = [jnp.asarray(a) for a in (inv_row, sign, is_rope, is_nope)]
    return _rowcall(fn, [pos_col] + consts, [_rows(TM, 1)] + [_shared((1, QK_CAT))] * 4,
                    [_sds((t, QK_CAT), F32)] * 2, [_rows(TM, QK_CAT)] * 2, (t // TM,), "rope_tables")


def _local_step(x, mem, pos, target, w, layer_start, layer_done):
    t = x.shape[0]
    g = {}
    head6 = (t // TM, A_HEADS)

    mem_n = _rmsnorm_fwd_small(mem, w["mem_norm"])
    rope_c, rope_s = _rope_tables(pos.reshape(t, 1))
    mkv = [_mm(mem_n, w["w_mem_kv"][l], "nn", BF16, f"mkv{l}") for l in range(DEPTH)]

    saved = []
    for l in range(DEPTH):
        sv = {}
        layer_start(l, x)
        x, sv["ffn1"] = _ffn_fwd(x, w["ffn1_norm"][l], w["ffn1_w_gu"][l], w["ffn1_w_down"][l], "ffn1")
        sv["x1"] = x
        n2 = _rmsnorm_fwd(x, w["mix_norm"][l], "mix_norm")
        sv["n2"] = n2
        if l < N_A:
            ha = _mm(n2, w["a_w_in"][l], "nn", BF16, "a_in")
            yc3 = _conv_fwd(ha, w["a_conv"][l], "a_conv")
            blk3 = pl.BlockSpec((3, TM, HD), lambda i, h: (0, i, h))
            (qkv3,) = _rowcall(lambda b: jnp.stack(_gdn_prep(b[0].astype(F32), b[1].astype(F32), b[2].astype(F32))),
                               [yc3], [blk3], [_sds((3, t, A_WIDTH), BF16)], [blk3], head6, "a_prep")
            (bg,) = _rowcall(_gates, [ha, _pad128(w["a_A_log"][l], A_HEADS), _pad128(w["a_dt_bias"][l], A_HEADS)],
                             [_rows(TM, 128, A_BA_BLK), _shared((1, 128)), _shared((1, 128))],
                             [_sds((t, 128), F32)], [_rows(TM, 128)], (t // TM,), "a_gates")
            o_gdn, states = _gdn_fwd(qkv3, bg, "a_gdn")
            (o_a,) = _rowcall(_outnorm_gate, [o_gdn, ha, w["a_out_norm"][l].reshape(1, HD)],
                              [pl.BlockSpec((TM, HD), lambda i, h: (i, h)),
                               pl.BlockSpec((TM, HD), lambda i, h: (i, 3 * A_HEADS + h)), _shared((1, HD))],
                              [_sds((t, A_WIDTH), BF16)], [pl.BlockSpec((TM, HD), lambda i, h: (i, h))],
                              head6, "a_outnorm")
            o_m = _mem_fwd(ha, A_MQ_BLK, mkv[l], "mem_attn_a")
            sv.update(ha=ha, yc3=yc3, qkv3=qkv3, bg=bg, states=states, o_gdn=o_gdn)
            cat = jnp.concatenate([o_a, o_m], axis=1)
        else:
            j = l - N_A
            hb = _mm(n2, w["b_w_in"][j], "nn", BF16, "b_in")
            cqn = _rmsnorm_fwd(hb, w["b_q_norm"][j], "b_qnorm", 0, Q_LORA)
            qq = _mm(cqn, w["b_w_uq"][j], "nn", BF16, "b_uq")
            (qc,) = _rowcall(_rope_mix, [qq, qq, rope_c, rope_s],
                             [pl.BlockSpec((TM, QK_CAT), lambda i, h: (i, h)),
                              pl.BlockSpec((TM, QK_CAT), lambda i, h: (i, B_HEADS + h)),
                              pl.BlockSpec((TM, QK_CAT), lambda i, h: (i, 0)),
                              pl.BlockSpec((TM, QK_CAT), lambda i, h: (i, 0))],
                             [_sds((t, B_HEADS * QK_CAT), BF16)], [pl.BlockSpec((TM, QK_CAT), lambda i, h: (i, h))],
                             head6, "b_qrope")
            o_b, lse, lse_row = _att_fwd(qc, kcat, vmla, "b_attn")
            o_m = _mem_fwd(hb, 1, mkv[l], "mem_attn_b")
            sv.update(hb=hb, cqn=cqn, qc=qc, o_b=o_b, lse=(lse, lse_row))
            cat = jnp.concatenate([o_b, o_m], axis=1)
        sv["cat"] = cat
        x = _mm(cat, w["w_out"][l], "nn", F32, "w_out", res=x)
        x, sv["ffn2"] = _ffn_fwd(x, w["ffn2_norm"][l], w["ffn2_w_gu"][l], w["ffn2_w_down"][l], "ffn2")
        saved.append(sv)
        if l == N_A - 1:
            x_kv = x
            nkv = _rmsnorm_fwd(x, w["kv_in_norm"], "kv_in_norm")
            ckr = _mm(nkv, w["w_dkv"], "nn", BF16, "kv_down")
            ckv_n = _rmsnorm_fwd(ckr, w["kv_lat_norm"], "kv_lat_norm", 0, KV_LORA)
            kvu = _mm(ckv_n, w["w_ukv"], "nn", BF16, "kv_up")
            vmla = kvu[:, B_HEADS * QK_CAT:]
            (kcat,) = _rowcall(_kcat, [kvu, ckr, ckr, rope_c, rope_s],
                               [pl.BlockSpec((TM, QK_CAT), lambda i, h: (i, h)),
                                pl.BlockSpec((TM, QK_CAT), lambda i, h: (i, 1)),
                                pl.BlockSpec((TM, QK_CAT), lambda i, h: (i, 2)),
                                pl.BlockSpec((TM, QK_CAT), lambda i, h: (i, 0)),
                                pl.BlockSpec((TM, QK_CAT), lambda i, h: (i, 0))],
                               [_sds((t, B_HEADS * QK_CAT), BF16)],
                               [pl.BlockSpec((TM, QK_CAT), lambda i, h: (i, h))], head6, "kv_cat")

    def loss_fn(xb, gb, tb):
        def f(xx, gg):
            e = _rms(xx, gg) - tb
            return 0.5 * jnp.sum(jnp.mean(e * e, axis=-1, keepdims=True), axis=0, keepdims=True)
        val, vjp = jax.vjp(f, xb, gb)
        dx, dg = vjp(jnp.ones((1, 1), F32))
        return dx, dg, val * jnp.ones((1, 128), F32)

    d, dfin, loss = _rowcall(loss_fn, [x, w["final_norm"].reshape(1, D), target],
                             [_rows(TM, D), _shared((1, D)), _rows(TM, D)],
                             [_sds((t, D), F32), _sds((1, D), F32), _sds((1, 128), F32)],
                             [_rows(TM, D), _shared((1, D)), _shared((1, 128))], (t // TM,), "loss_head", n_acc=2)
    g["final_norm"] = dfin[0]
    loss = loss[0, 0]

    for name in ("ffn1_norm", "ffn1_w_gu", "ffn1_w_down", "mix_norm", "ffn2_norm", "ffn2_w_gu", "ffn2_w_down",
                 "w_out", "w_mem_kv"):
        g[name] = [None] * DEPTH
    for name in ("a_w_in", "a_conv", "a_A_log", "a_dt_bias", "a_out_norm"):
        g[name] = [None] * N_A
    for name in ("b_w_in", "b_q_norm", "b_w_uq"):
        g[name] = [None] * N_B
    dmkv = [None] * DEPTH
    dkcat = []
    dvmla = []

    for l in reversed(range(DEPTH)):
        sv = saved[l]
        if l == N_A - 1:
            kq = pl.BlockSpec((TM, QK_CAT), lambda i, h: (i, h))
            tab = pl.BlockSpec((TM, QK_CAT), lambda i, h: (i, 0))

            def dk_fn(c, s, d0, d1):
                dk = d0.astype(F32) + d1.astype(F32)
                return dk, dk * c, dk * s

            dkn, dkr_h, dkrs_h = _rowcall(dk_fn, [rope_c, rope_s, dkcat[0], dkcat[1]], [tab, tab, kq, kq],
                                          [_sds((t, B_HEADS * QK_CAT), BF16)] + [_sds((B_HEADS, t, QK_CAT), BF16)] * 2,
                                          [kq] + [pl.BlockSpec((None, TM, QK_CAT), lambda i, h: (h, i, 0))] * 2,
                                          head6, "kv_dcat")

            def sum6(a, b):
                return jnp.sum(a.astype(F32), axis=0), jnp.sum(b.astype(F32), axis=0)

            h6 = pl.BlockSpec((B_HEADS, TM, QK_CAT), lambda i: (0, i, 0))
            dkr, dkrs = _rowcall(sum6, [dkr_h, dkrs_h], [h6, h6], [_sds((t, QK_CAT), BF16)] * 2,
                                 [_rows(TM, QK_CAT)] * 2, (t // TM,), "kv_dkr")

            def addv(a, b):
                return a.astype(F32) + b.astype(F32)

            (dv,) = _rowcall(addv, dvmla, [_rows(TM, B_HEADS * HD)] * 2, [_sds((t, B_HEADS * HD), BF16)],
                             [_rows(TM, B_HEADS * HD)], (t // TM,), "kv_dv")
            dkvu = jnp.concatenate([dkn, dv], axis=1)
            g["w_ukv"] = _mm(ckv_n, dkvu, "tn", F32, "kv_up_dw")
            dckv_n = _mm(dkvu, w["w_ukv"], "nt", BF16, "kv_up_dx")

            def lat_bwd(cb, gb, dnb):
                return _vjp_fn(_rms, 2, (0, 1))(cb, gb, dnb)

            dckv, g["kv_lat_norm"] = _rowcall(lat_bwd, [ckr, w["kv_lat_norm"].reshape(1, KV_LORA), dckv_n],
                                              [_rows(TM, KV_LORA), _shared((1, KV_LORA)), _rows(TM, KV_LORA)],
                                              [_sds((t, KV_LORA), BF16), _sds((1, KV_LORA), F32)],
                                              [_rows(TM, KV_LORA), _shared((1, KV_LORA))], (t // TM,),
                                              "kv_lat_dnorm", n_acc=1)
            g["kv_lat_norm"] = g["kv_lat_norm"][0]
            dckr = jnp.concatenate([dckv, dkr, dkrs], axis=1)
            g["w_dkv"] = _mm(nkv, dckr, "tn", F32, "kv_down_dw")
            dnkv = _mm(dckr, w["w_dkv"], "nt", BF16, "kv_down_dx")
            d, g["kv_in_norm"] = _rmsnorm_bwd(x_kv, w["kv_in_norm"], dnkv, d, "kv_in_dnorm")

        d, g["ffn2_norm"][l], g["ffn2_w_gu"][l], g["ffn2_w_down"][l] = _ffn_bwd(
            d, sv["ffn2"], w["ffn2_norm"][l], w["ffn2_w_gu"][l], w["ffn2_w_down"][l], "ffn2b")
        g["w_out"][l] = _mm(sv["cat"], d, "tn", F32, "w_out_dw")
        dcat = _mm(d, w["w_out"][l], "nt", BF16, "w_out_dx")
        if l < N_A:
            ha, yc3, qkv3, states, o_gdn = sv["ha"], sv["yc3"], sv["qkv3"], sv["states"], sv["o_gdn"]
            dmq, dmkv[l] = _mem_bwd(ha, A_MQ_BLK, mkv[l], dcat, 3, "mem_attn_a_bwd")
            hblk = pl.BlockSpec((TM, HD), lambda i, h: (i, h))
            do_gdn, dgate, dgain = _rowcall(
                _vjp_fn(_outnorm_gate, 3, (0, 1, 2)), [o_gdn, ha, w["a_out_norm"][l].reshape(1, HD), dcat],
                [hblk, pl.BlockSpec((TM, HD), lambda i, h: (i, 3 * A_HEADS + h)), _shared((1, HD)), hblk],
                [_sds((t, A_WIDTH), BF16), _sds((t, A_WIDTH), BF16), _sds((1, HD), F32)],
                [hblk, hblk, _shared((1, HD))], head6, "a_outnorm_bwd", n_acc=1)
            g["a_out_norm"][l] = dgain[0]
            dqkv3, dbg6 = _gdn_bwd(qkv3, sv["bg"], states, do_gdn, "a_gdn_bwd")

            def dgates(bab, alb, dtb, d6):
                return _vjp_fn(_gates, 3, (0, 1, 2))(bab, alb, dtb, jnp.sum(d6, axis=0))

            dba, dalog, ddt = _rowcall(
                dgates, [ha, _pad128(w["a_A_log"][l], A_HEADS), _pad128(w["a_dt_bias"][l], A_HEADS), dbg6],
                [_rows(TM, 128, A_BA_BLK), _shared((1, 128)), _shared((1, 128)),
                 pl.BlockSpec((A_HEADS, TM, 128), lambda i: (0, i, 0))],
                [_sds((t, 128), BF16), _sds((1, 128), F32), _sds((1, 128), F32)],
                [_rows(TM, 128), _shared((1, 128)), _shared((1, 128))], (t // TM,), "a_gates_bwd", n_acc=2)
            g["a_A_log"][l] = dalog[0, A_HEADS:2 * A_HEADS]
            g["a_dt_bias"][l] = ddt[0, A_HEADS:2 * A_HEADS]
            blk3 = pl.BlockSpec((3, TM, HD), lambda i, h: (0, i, h))
            def dprep(b, db):
                return jnp.stack(_vjp_fn(_gdn_prep, 3, (0, 1, 2))(b[0], b[1], b[2], db[0], db[1], db[2]))

            (dyc3,) = _rowcall(dprep, [yc3, dqkv3], [blk3, blk3],
                               [_sds((3, t, A_WIDTH), BF16)], [blk3], head6, "a_prep_bwd")
            dqkv_in, g["a_conv"][l] = _conv_bwd(ha, w["a_conv"][l], dyc3, "a_conv_bwd")
            dha = jnp.concatenate([dqkv_in, dgate, dmq, dba], axis=1)
            g["a_w_in"][l] = _mm(sv["n2"], dha, "tn", F32, "a_in_dw")
            dn2 = _mm(dha, w["a_w_in"][l], "nt", BF16, "a_in_dx")
        else:
            j = l - N_A
            hb, cqn, qc, o_b, lse = sv["hb"], sv["cqn"], sv["qc"], sv["o_b"], sv["lse"]
            dmq, dmkv[l] = _mem_bwd(hb, 1, mkv[l], dcat, 3, "mem_attn_b_bwd")
            dqc, dkc, dvv = _att_bwd(qc, kcat, vmla, o_b, lse[0], lse[1], dcat, "b_attn_bwd")
            dkcat.append(dkc)
            dvmla.append(dvv)
            kq = pl.BlockSpec((TM, QK_CAT), lambda i, h: (i, h))
            tab = pl.BlockSpec((TM, QK_CAT), lambda i, h: (i, 0))

            def dq_fn(c, s, dq):
                dq = dq.astype(F32)
                return jnp.stack([dq * c, dq * s])

            (dqq,) = _rowcall(dq_fn, [rope_c, rope_s, dqc], [tab, tab, kq],
                              [_sds((2, t, B_HEADS * QK_CAT), BF16)],
                              [pl.BlockSpec((2, TM, QK_CAT), lambda i, h: (0, i, h))], head6, "b_qrope_bwd")
            dqq = jnp.concatenate([dqq[0], dqq[1]], axis=1)
            g["b_w_uq"][j] = _mm(cqn, dqq, "tn", F32, "b_uq_dw")
            dcqn = _mm(dqq, w["b_w_uq"][j], "nt", BF16, "b_uq_dx")
            dcq, dqg = _rowcall(_vjp_fn(_rms, 2, (0, 1)), [hb, w["b_q_norm"][j].reshape(1, Q_LORA), dcqn],
                                [_rows(TM, Q_LORA), _shared((1, Q_LORA)), _rows(TM, Q_LORA)],
                                [_sds((t, Q_LORA), BF16), _sds((1, Q_LORA), F32)],
                                [_rows(TM, Q_LORA), _shared((1, Q_LORA))], (t // TM,), "b_qnorm_bwd", n_acc=1)
            g["b_q_norm"][j] = dqg[0]
            dhb = jnp.concatenate([dcq, dmq], axis=1)
            g["b_w_in"][j] = _mm(sv["n2"], dhb, "tn", F32, "b_in_dw")
            dn2 = _mm(dhb, w["b_w_in"][j], "nt", BF16, "b_in_dx")
        d, g["mix_norm"][l] = _rmsnorm_bwd(sv["x1"], w["mix_norm"][l], dn2, d, "mix_dnorm")
        d, g["ffn1_norm"][l], g["ffn1_w_gu"][l], g["ffn1_w_down"][l] = _ffn_bwd(
            d, sv["ffn1"], w["ffn1_norm"][l], w["ffn1_w_gu"][l], w["ffn1_w_down"][l], "ffn1b")
        d = layer_done(l, g, d)

    dmem_n = None
    for l in range(DEPTH):
        g["w_mem_kv"][l] = _mm(mem_n, dmkv[l], "tn", F32, f"mkv_dw{l}")
        dmem_n = _mm(dmkv[l], w["w_mem_kv"][l], "nt", F32, f"mkv_dx{l}", res=dmem_n)
    (_, gmn) = _rowcall(_vjp_fn(_rms, 2, (0, 1)), [mem, w["mem_norm"].reshape(1, D), dmem_n],
                        [_shared((N_MEM, D)), _shared((1, D)), _shared((N_MEM, D))],
                        [_sds((N_MEM, D), F32), _sds((1, D), F32)], [_shared((N_MEM, D)), _shared((1, D))],
                        (1,), "mem_dnorm")
    g["mem_norm"] = gmn[0]
    return loss, d, g


def _pad128(v, offset):
    return jnp.pad(v.astype(F32).reshape(1, -1), ((0, 0), (offset, 128 - offset - v.shape[0])))


def _rmsnorm_fwd_small(x, gain):
    r, w = x.shape
    (n,) = _rowcall(_rms, [x, gain.reshape(1, w)], [_shared((r, w)), _shared((1, w))],
                    [_sds((r, w), BF16)], [_shared((r, w))], (1,), "mem_norm")
    return n


def _exchange(srcs, gather, name):
    n = len(srcs)
    blks = [tuple(s.shape) if gather else tuple(s.shape[1:]) for s in srcs]

    def body(*refs):
        src_refs, out_refs = refs[:n], refs[n:2 * n]
        send_sems, recv_sems, local_sems = refs[2 * n:]
        x, y, c = lax.axis_index("x"), lax.axis_index("y"), lax.axis_index("c")
        me = 4 * x + 2 * y + c
        copies = []
        for k in range(1, N_DEV):
            px = (x + (k >> 2 & 1)) % 2
            py = (y + (k >> 1 & 1)) % 2
            pc = (c + (k & 1)) % 2
            peer = 4 * px + 2 * py + pc
            for a in range(n):
                cp = pltpu.make_async_remote_copy(
                    src_ref=src_refs[a] if gather else src_refs[a].at[peer], dst_ref=out_refs[a].at[me],
                    send_sem=send_sems.at[a, k - 1], recv_sem=recv_sems.at[a, k - 1],
                    device_id=(px, py, pc), device_id_type=pl.DeviceIdType.MESH)
                cp.start()
                copies.append(cp)
        for a in range(n):
            cp = pltpu.make_async_copy(src_refs[a] if gather else src_refs[a].at[me], out_refs[a].at[me],
                                       local_sems.at[a])
            cp.start()
            copies.append(cp)
        for cp in copies:
            cp.wait()

    return pl.pallas_call(
        body, out_shape=tuple(_sds((N_DEV,) + b, s.dtype) for b, s in zip(blks, srcs)),
        in_specs=[pl.BlockSpec(memory_space=pl.ANY)] * n, out_specs=tuple([pl.BlockSpec(memory_space=pl.ANY)] * n),
        scratch_shapes=[pltpu.SemaphoreType.DMA((n, N_DEV - 1)), pltpu.SemaphoreType.DMA((n, N_DEV - 1)),
                        pltpu.SemaphoreType.DMA((n,))],
        name=name)(*srcs)


_HBM = pl.BlockSpec(memory_space=pltpu.HBM)
_SEM = pl.BlockSpec(memory_space=pltpu.SEMAPHORE)
_EFFECT = pltpu.SideEffectType.DATAFLOW_SIDE_EFFECTING


def _split_copies(src_refs, land_refs, send_sems, recv_sems, gather):
    x, y, c = lax.axis_index("x"), lax.axis_index("y"), lax.axis_index("c")
    me = 4 * x + 2 * y + c
    copies = []
    for k in range(1, N_DEV):
        peer = ((x + (k >> 2 & 1)) % 2, (y + (k >> 1 & 1)) % 2, (c + (k & 1)) % 2)
        for a in range(len(src_refs)):
            i = (k - 1) * len(src_refs) + a
            src = src_refs[a] if gather else src_refs[a].at[4 * peer[0] + 2 * peer[1] + peer[2]]
            copies.append(pltpu.make_async_remote_copy(
                src_ref=src, dst_ref=land_refs[a].at[me], send_sem=send_sems[i],
                recv_sem=recv_sems[i], device_id=peer, device_id_type=pl.DeviceIdType.MESH))
    return copies


def _split_start(srcs, gather, name):
    n = len(srcs)
    srcs = [pltpu.with_memory_space_constraint(s, pltpu.HBM) for s in srcs]
    lands = [pltpu.with_memory_space_constraint(
        lax.empty(((N_DEV,) + s.shape) if gather else s.shape, s.dtype), pltpu.HBM) for s in srcs]

    ns = n * (N_DEV - 1)

    def body(*refs):
        sems = refs[2 * n:2 * n + 2 * ns]
        for cp in _split_copies(refs[:n], refs[n:2 * n], sems[:ns], sems[ns:], gather):
            cp.start()
        refs[-1][...] = jnp.zeros_like(refs[-1])

    outs = pl.pallas_call(
        body, name=name,
        out_shape=(*[pltpu.SemaphoreType.DMA(())] * (2 * ns), *[pltpu.HBM(a.shape, a.dtype) for a in srcs + lands],
                   _sds((8, 128), F32)),
        in_specs=[_HBM] * (2 * n),
        out_specs=(*[_SEM] * (2 * ns), *[_HBM] * (2 * n), pl.BlockSpec(memory_space=pltpu.VMEM)),
        input_output_aliases={i: 2 * ns + i for i in range(2 * n)},
        compiler_params=pltpu.CompilerParams(has_side_effects=_EFFECT))(*srcs, *lands)
    sems, rest = list(outs[:2 * ns]), outs[2 * ns:]
    return sems[:ns], sems[ns:], list(rest[:n]), list(rest[n:2 * n]), rest[-1]


def _split_wait(flight, after, gather, name):
    send_sems, recv_sems, srcs, lands, _ = flight
    n = len(srcs)

    ns = len(send_sems)

    def body(*refs):
        sems = refs[2 * n:2 * n + 2 * ns]
        for cp in _split_copies(refs[:n], refs[n:2 * n], sems[:ns], sems[ns:], gather):
            cp.wait_send()
            cp.wait_recv()

    outs = pl.pallas_call(
        body, name=name, out_shape=tuple(pltpu.HBM(a.shape, a.dtype) for a in srcs + lands),
        in_specs=[_HBM] * (2 * n) + [_SEM] * (2 * ns) + [pl.BlockSpec(memory_space=pl.ANY)],
        out_specs=tuple([_HBM] * (2 * n)), input_output_aliases={i: i for i in range(2 * n)},
        compiler_params=pltpu.CompilerParams(has_side_effects=_EFFECT))(*srcs, *lands, *send_sems, *recv_sems, after)
    return list(outs[n:])


def _reduce_adamw(parts, wp, mp, vp, name):
    r, cols = wp.shape
    tr = _tile_rows(r, cols)
    c1 = 1.0 - ADAM_B1 ** ADAM_STEP
    c2 = 1.0 - ADAM_B2 ** ADAM_STEP

    def fn(pb, wb, mb, vb):
        gsum = pb[0].astype(F32)
        for j in range(1, N_DEV):
            gsum = gsum + pb[j].astype(F32)
        m_new = ADAM_B1 * mb + (1.0 - ADAM_B1) * gsum
        v_new = ADAM_B2 * vb + (1.0 - ADAM_B2) * (gsum * gsum)
        delta = -ADAM_LR * ((m_new / c1) / (jnp.sqrt(v_new / c2) + ADAM_EPS) + ADAM_WD * wb)
        return gsum, delta, m_new, v_new

    row = _rows(tr, cols)
    return _rowcall(fn, [parts, wp, mp, vp],
                    [pl.BlockSpec((N_DEV, tr, cols), lambda i: (0, i, 0)), row, row, row],
                    [_sds((r, cols), F32)] * 4, [row] * 4, (r // tr,), name)


def _tile_rows(r, cols):
    for t in (512, 256, 128, 64, 32, 16):
        if r % t == 0 and t * cols <= 160 * 1024:
            return t
    return r


def _pack(arrs):
    flat = jnp.concatenate([a.reshape(-1).astype(F32) for a in arrs])
    n = flat.shape[0]
    unit = PACK_W * PACK_ROWS
    tot = -(-n // unit) * unit
    return jnp.pad(flat, (0, tot - n)).reshape(tot // PACK_W, PACK_W)


def _unpack(buf, shapes):
    out, off = [], 0
    flat = buf.reshape(-1)
    for s in shapes:
        n = int(np.prod(s))
        out.append(flat[off:off + n].reshape(s))
        off += n
    return out


def _as2d(a):
    return a.reshape(-1, a.shape[-1])


_SHARDED = ["ffn1_w_gu", "ffn1_w_down", "ffn2_w_gu", "ffn2_w_down", "w_out", "w_mem_kv", "a_w_in", "a_conv",
            "b_w_in", "b_w_uq", "w_dkv", "w_ukv"]
_COL_SHARDED = {"ffn1_w_gu", "ffn2_w_gu", "a_conv", "b_w_uq", "w_ukv"}
_LAYERED = {"ffn1_w_gu": DEPTH, "ffn1_w_down": DEPTH, "ffn2_w_gu": DEPTH, "ffn2_w_down": DEPTH, "w_out": DEPTH,
            "w_mem_kv": DEPTH, "a_w_in": N_A, "a_conv": N_A, "b_w_in": N_B, "b_w_uq": N_B}
_GATHER_FIRST = [("ffn1_w_gu", 0, 1), ("ffn1_w_down", 0, 1), ("ffn2_w_gu", 0, 1), ("ffn2_w_down", 0, 1),
                 ("w_out", 0, 1), ("a_w_in", 0, 1), ("a_conv", 0, 1), ("w_mem_kv", 0, DEPTH)]
_GATHER_REST = [("ffn1_w_gu", 1, DEPTH), ("ffn1_w_down", 1, DEPTH), ("ffn2_w_gu", 1, DEPTH),
                ("ffn2_w_down", 1, DEPTH), ("w_out", 1, DEPTH), ("a_w_in", 1, N_A), ("a_conv", 1, N_A),
                ("b_w_in", 0, N_B), ("b_w_uq", 0, N_B), ("w_dkv", None, None), ("w_ukv", None, None)]
_SCATTER_HI = [("ffn1_w_gu", 2, DEPTH), ("ffn1_w_down", 2, DEPTH), ("ffn2_w_gu", 2, DEPTH), ("ffn2_w_down", 2, DEPTH),
               ("w_out", 2, DEPTH), ("b_w_in", 0, N_B), ("b_w_uq", 0, N_B)]
_SCATTER_MID = [("ffn1_w_gu", 1, 2), ("ffn1_w_down", 1, 2), ("ffn2_w_gu", 1, 2), ("ffn2_w_down", 1, 2),
                ("w_out", 1, 2), ("a_w_in", 1, N_A), ("a_conv", 1, N_A), ("w_dkv", None, None), ("w_ukv", None, None)]
_SCATTER_LO = [("ffn1_w_gu", 0, 1), ("ffn1_w_down", 0, 1), ("ffn2_w_gu", 0, 1), ("ffn2_w_down", 0, 1),
               ("w_out", 0, 1), ("a_w_in", 0, 1), ("a_conv", 0, 1), ("w_mem_kv", 0, DEPTH)]
_REPLICATED = ["ffn1_norm", "mix_norm", "ffn2_norm", "mem_norm", "a_A_log", "a_dt_bias", "a_out_norm", "b_q_norm",
               "kv_in_norm", "kv_lat_norm", "final_norm"]
_WEIGHTS = ["ffn1_norm", "ffn1_w_gu", "ffn1_w_down", "mix_norm", "ffn2_norm", "ffn2_w_gu", "ffn2_w_down", "w_out",
            "mem_norm", "w_mem_kv", "a_w_in", "a_conv", "a_A_log", "a_dt_bias", "a_out_norm", "b_w_in", "b_q_norm",
            "b_w_uq", "kv_in_norm", "w_dkv", "kv_lat_norm", "w_ukv", "final_norm"]


def _full_from_shards(name, sh):
    if name in ("ffn1_w_gu", "ffn2_w_gu"):
        return sh
    if name in ("ffn1_w_down", "ffn2_w_down"):
        return sh.reshape(4, FF_SHARD, D)
    if name in _COL_SHARDED:
        return jnp.moveaxis(sh, 0, -2).reshape(sh.shape[1:-1] + (N_DEV * sh.shape[-1],))
    return sh.reshape((N_DEV * sh.shape[1],) + sh.shape[2:])


def _shards_from_full(name, full):
    if name in ("ffn1_w_gu", "ffn2_w_gu"):
        return full
    if name in ("ffn1_w_down", "ffn2_w_down"):
        return full.reshape(N_DEV, D_FF // N_DEV, D)
    if name in _COL_SHARDED:
        r, cc = full.shape
        return jnp.moveaxis(full.reshape(r, N_DEV, cc // N_DEV), 1, 0)
    return full.reshape((N_DEV, full.shape[0] // N_DEV) + full.shape[1:])


def kernel(x, mem, positions, ffn1_norm, ffn1_w_gu, ffn1_w_down, mix_norm, ffn2_norm, ffn2_w_gu, ffn2_w_down, w_out, mem_norm, w_mem_kv, a_w_in, a_conv, a_A_log, a_dt_bias, a_out_norm, b_w_in, b_q_norm, b_w_uq, kv_in_norm, w_dkv, kv_lat_norm, w_ukv, final_norm, loss_target, m_ffn1_norm, m_ffn1_w_gu, m_ffn1_w_down, m_mix_norm, m_ffn2_norm, m_ffn2_w_gu, m_ffn2_w_down, m_w_out, m_mem_norm, m_w_mem_kv, m_a_w_in, m_a_conv, m_a_A_log, m_a_dt_bias, m_a_out_norm, m_b_w_in, m_b_q_norm, m_b_w_uq, m_kv_in_norm, m_w_dkv, m_kv_lat_norm, m_w_ukv, m_final_norm, v_ffn1_norm, v_ffn1_w_gu, v_ffn1_w_down, v_mix_norm, v_ffn2_norm, v_ffn2_w_gu, v_ffn2_w_down, v_w_out, v_mem_norm, v_w_mem_kv, v_a_w_in, v_a_conv, v_a_A_log, v_a_dt_bias, v_a_out_norm, v_b_w_in, v_b_q_norm, v_b_w_uq, v_kv_in_norm, v_w_dkv, v_kv_lat_norm, v_w_ukv, v_final_norm):
    loc = dict(locals())
    wl = {n: loc[n] for n in _WEIGHTS}
    ml = {n: loc["m_" + n] for n in _WEIGHTS}
    vl = {n: loc["v_" + n] for n in _WEIGHTS}

    me = 4 * lax.axis_index("x") + 2 * lax.axis_index("y") + lax.axis_index("c")
    w = {n: wl[n] for n in _REPLICATED}
    for n in _SHARDED:
        w[n] = [None] * _LAYERED[n] if n in _LAYERED else None
    back = {}

    def src_of(item):
        n, lo, hi = item
        return (wl[n] if lo is None else wl[n][lo:hi]).astype(BF16)

    def install(items, pieces):
        for (n, lo, hi), p in zip(items, pieces):
            if lo is None:
                w[n] = _full_from_shards(n, p)
            else:
                for l in range(lo, hi):
                    w[n][l] = _full_from_shards(n, p[:, l - lo])
        for n, lo, hi in items:
            if n == "a_w_in":
                for l in range(lo, hi):
                    w[n][l], back[(n, l)] = _ext_and_back(_a_in_ext, w[n][l])
            elif n == "a_conv":
                for l in range(lo, hi):
                    w[n][l] = w[n][l].astype(F32)
            elif n == "b_w_uq":
                for l in range(lo, hi):
                    w[n][l], back[(n, l)] = _ext_and_back(_uq_ext, w[n][l])
            elif n == "w_dkv":
                w[n], back[n] = _ext_and_back(_dkv_ext, w[n])
            elif n == "w_ukv":
                w[n], back[n] = _ext_and_back(_ukv_ext, w[n])

    install(_GATHER_FIRST, _exchange([src_of(it) for it in _GATHER_FIRST], True, "gather_first"))
    rest_srcs = [src_of(it) for it in _GATHER_REST]
    flight = _split_start(rest_srcs, True, "gather_rest_start")
    w["ffn1_norm"] = w["ffn1_norm"] + flight[-1][0, 0]

    def layer_start(l, stream):
        if l == 1:
            lands = _split_wait(flight, stream, True, "gather_rest_wait")
            own = [lax.dynamic_update_slice(ld, s[None], (me,) + (0,) * s.ndim) for ld, s in zip(lands, rest_srcs)]
            install(_GATHER_REST, own)

    def grad_src(item, g):
        n, lo, hi = item

        def one(l):
            gl = g[n] if l is None else g[n][l]
            key = n if l is None else (n, l)
            if key in back:
                gl = back[key](gl)
            return _shards_from_full(n, gl).astype(BF16)

        return one(None) if lo is None else jnp.stack([one(l) for l in range(lo, hi)], axis=1)

    sent = {}

    def layer_done(l, g, d):
        for lname, items in ((2, _SCATTER_HI), (1, _SCATTER_MID)):
            if l == lname:
                srcs = [grad_src(it, g) for it in items]
                sent[l] = (srcs, _split_start(srcs, False, f"scatter_{l}_start"))
                return d + sent[l][1][-1][0, 0]
        return d

    loss, dx, g = _local_step(x[0], mem[0], positions[0], loss_target[0], w, layer_start, layer_done)
    loss = lax.psum(loss, ("x", "y", "c"))

    lo_srcs = [grad_src(it, g) for it in _SCATTER_LO]
    pieces = {n: [] for n in _SHARDED}
    for (n, lo, hi), p in zip(_SCATTER_LO, _exchange(lo_srcs, False, "scatter_grads")):
        pieces[n].append((lo, p))
    for l, items in ((1, _SCATTER_MID), (2, _SCATTER_HI)):
        srcs, fl = sent[l]
        lands = _split_wait(fl, dx, False, f"scatter_{l}_wait")
        for (n, lo, hi), s, ld in zip(items, srcs, lands):
            mine = lax.dynamic_slice(s, (me,) + (0,) * (s.ndim - 1), (1,) + s.shape[1:])
            pieces[n].append((lo, lax.dynamic_update_slice(ld, mine, (me,) + (0,) * (s.ndim - 1))))
    out = {}
    for n in _SHARDED:
        ps = [p for _, p in sorted(pieces[n], key=lambda e: -1 if e[0] is None else e[0])]
        p = ps[0] if len(ps) == 1 else jnp.concatenate(ps, axis=1)
        shape = wl[n].shape
        res = _reduce_adamw(p.reshape(N_DEV, -1, shape[-1]), _as2d(wl[n]), _as2d(ml[n]), _as2d(vl[n]), "adamw_" + n)
        for kind, buf in zip(("grad", "delta", "new_m", "new_v"), res):
            out[(kind, n)] = buf.reshape(shape)

    rep_shapes = [wl[n].shape for n in _REPLICATED]
    grep = [jnp.stack(g[n]) if isinstance(g[n], list) else g[n] for n in _REPLICATED]
    (rparts,) = _exchange([_pack(grep)], True, "gather_small_grads")
    res = _reduce_adamw(rparts, _pack([wl[n] for n in _REPLICATED]), _pack([ml[n] for n in _REPLICATED]),
                        _pack([vl[n] for n in _REPLICATED]), "adamw_replicated")
    for kind, buf in zip(("grad", "delta", "new_m", "new_v"), res):
        for n, a in zip(_REPLICATED, _unpack(buf, rep_shapes)):
            out[(kind, n)] = a

    return (loss, dx[None], *[out[("grad", n)] for n in _WEIGHTS], *[out[("delta", n)] for n in _WEIGHTS],
            *[out[("new_m", n)] for n in _WEIGHTS], *[out[("new_v", n)] for n in _WEIGHTS])
```

```python
import functools

import numpy as np
import jax
import jax.numpy as jnp
from jax import lax
from jax.experimental import pallas as pl
from jax.experimental.pallas import tpu as pltpu

F32 = jnp.float32
BF16 = jnp.bfloat16

N_DEV = 8
D = 1024
D_FF = 2816
FF_SHARD = 2 * D_FF // N_DEV
DEPTH = 4
N_A = 2
N_B = 2
EPS = 1e-6
CHUNK = 64
GROUP = 256
GDN_HPS = 3
A_HEADS = 6
HD = 128
A_WIDTH = A_HEADS * HD
B_HEADS = 6
QK_NOPE = 128
QK_ROPE = 64
QK_CAT = 256
Q_LORA = 256
KV_LORA = 256
MEM_HEADS = 4
MEM_HD = 64
MEM_W = 256
N_MEM = 256
ROPE_THETA = 10000.0
ATT_SCALE = (QK_NOPE + QK_ROPE) ** -0.5
LN2 = 0.6931471805599453
Q_PRESCALE = ATT_SCALE / LN2
A_IN = 4 * A_WIDTH + 2 * A_HEADS + MEM_W
A_MQ_BLK = 4 * A_WIDTH // MEM_W
A_BA_BLK = (4 * A_WIDTH + MEM_W) // 128

ADAM_LR = 0.001
ADAM_B1 = 0.9
ADAM_B2 = 0.999
ADAM_EPS = 1e-08
ADAM_WD = 0.01
ADAM_STEP = 10

VMEM_LIMIT = 56 * 1024 * 1024
TM = 512
TMM = 1024
TMF = 2048
ATT_BQ = 1024
PACK_W = 1024
PACK_ROWS = 32


def _cparams(sem):
    return pltpu.CompilerParams(dimension_semantics=sem, vmem_limit_bytes=VMEM_LIMIT)


_DIMS = {"nn": ((1,), (0,)), "nt": ((1,), (1,)), "tn": ((0,), (0,))}


def _dot(a, b, dims="nn"):
    return lax.dot_general(a.astype(BF16), b.astype(BF16), (_DIMS[dims], ((), ())),
                           preferred_element_type=F32)


def _matmul(a, b, *, dims, grid, a_spec, b_spec, o_spec, out_shape, name, scale=1.0,
            res=None, res_spec=None, a_fn=None, epilogue=None, acc_shape=None):
    nk = grid[-1]
    kax = len(grid) - 1
    if acc_shape is None:
        acc_shape = tuple(s for s in o_spec.block_shape if s is not None)

    def body(*refs):
        if res is None:
            a_ref, b_ref, o_ref, acc = refs
            r_ref = None
        else:
            a_ref, b_ref, r_ref, o_ref, acc = refs
        k = pl.program_id(kax)

        @pl.when(k == 0)
        def _():
            acc[...] = jnp.zeros_like(acc)

        a_blk = a_ref[...] if a_fn is None else a_fn(a_ref[...])
        acc[...] += _dot(a_blk, b_ref[...], dims)

        @pl.when(k == nk - 1)
        def _():
            y = acc[...] * scale
            if epilogue is not None:
                y = epilogue(y, r_ref[...])
            elif r_ref is not None:
                y = y + r_ref[...].astype(F32)
            o_ref[...] = y.astype(o_ref.dtype)

    args = [a, b] + ([res] if res is not None else [])
    specs = [a_spec, b_spec] + ([res_spec] if res is not None else [])
    sem = ("parallel",) * kax + ("arbitrary",)
    return pl.pallas_call(
        body, out_shape=out_shape, grid=grid, in_specs=specs, out_specs=o_spec,
        scratch_shapes=[pltpu.VMEM(acc_shape, F32)], name=name, compiler_params=_cparams(sem))(*args)


def _tile(n, cap):
    if n <= cap:
        return n
    t = cap - cap % 128
    while t >= 128:
        if n % t == 0:
            return t
        t -= 128
    raise ValueError(f"no tile for {n}")


def _mm(a, b, dims, out_dtype, name, scale=1.0, res=None):
    if dims == "tn":
        kk, m = a.shape
        n = b.shape[1]
        tk, tn = _tile(kk, TMM), _tile(n, 1152)
        return _matmul(a, b, dims=dims, grid=(1, n // tn, kk // tk),
                       a_spec=pl.BlockSpec((tk, m), lambda i, j, k: (k, 0)),
                       b_spec=pl.BlockSpec((tk, tn), lambda i, j, k: (k, j)),
                       o_spec=pl.BlockSpec((m, tn), lambda i, j, k: (0, j)),
                       out_shape=jax.ShapeDtypeStruct((m, n), out_dtype), name=name, scale=scale)
    m, kk = a.shape
    n = b.shape[1] if dims == "nn" else b.shape[0]
    tm, tn, tk = _tile(m, TMM), _tile(n, 1152), _tile(kk, 1536)
    if dims == "nn":
        b_spec = pl.BlockSpec((tk, tn), lambda i, j, k: (k, j))
    else:
        b_spec = pl.BlockSpec((tn, tk), lambda i, j, k: (j, k))
    o_spec = pl.BlockSpec((tm, tn), lambda i, j, k: (i, j))
    return _matmul(a, b, dims=dims, grid=(m // tm, n // tn, kk // tk),
                   a_spec=pl.BlockSpec((tm, tk), lambda i, j, k: (i, k)), b_spec=b_spec, o_spec=o_spec,
                   out_shape=jax.ShapeDtypeStruct((m, n), out_dtype), name=name, scale=scale,
                   res=res, res_spec=o_spec if res is not None else None)


def _rowcall(fn, args, in_specs, out_shapes, out_specs, grid, name, n_acc=0):
    n_in, n_out = len(args), len(out_shapes)

    def body(*refs):
        outs = fn(*[r[...] for r in refs[:n_in]])
        if not isinstance(outs, (tuple, list)):
            outs = (outs,)
        first = pl.program_id(0) == 0
        for ax in range(1, len(grid)):
            first = jnp.logical_and(first, pl.program_id(ax) == 0)
        for idx, (o_ref, val) in enumerate(zip(refs[n_in:], outs)):
            if idx >= n_out - n_acc:
                @pl.when(first)
                def _(o_ref=o_ref):
                    o_ref[...] = jnp.zeros_like(o_ref)

                o_ref[...] += val.astype(o_ref.dtype)
            else:
                o_ref[...] = val.astype(o_ref.dtype)

    sem = (("arbitrary",) if n_acc else ("parallel",)) * len(grid)
    res = pl.pallas_call(body, out_shape=tuple(out_shapes), grid=grid, in_specs=list(in_specs),
                         out_specs=tuple(out_specs), name=name, compiler_params=_cparams(sem))(*args)
    return res


def _vjp_fn(fn, n_in, wrt):
    def bwd(*blocks):
        ins = [b.astype(F32) for b in blocks[:n_in]]
        cts = [c.astype(F32) for c in blocks[n_in:]]
        outs, vjp = jax.vjp(fn, *ins)
        if isinstance(outs, (tuple, list)):
            grads = vjp(tuple(cts))
        else:
            grads = vjp(cts[0])
        return tuple(grads[i] for i in wrt)
    return bwd


def _sds(shape, dtype):
    return jax.ShapeDtypeStruct(tuple(shape), dtype)


def _rows(tm, w, col=0):
    return pl.BlockSpec((tm, w), lambda i, *_: (i, col))


def _shared(shape):
    nd = len(shape)
    return pl.BlockSpec(tuple(shape), lambda *_: (0,) * nd)


def _rms(x, g):
    return x * lax.rsqrt(jnp.mean(x * x, axis=-1, keepdims=True) + EPS) * g


def _silu(x):
    return x * jax.nn.sigmoid(x)


def _swiglu_pair(gu):
    return _silu(gu[0].astype(F32)) * gu[1].astype(F32)


def _swiglu_bwd(dh, gu):
    g, u = gu[0].astype(F32), gu[1].astype(F32)
    sg = jax.nn.sigmoid(g)
    return jnp.stack([dh * u * sg * (1.0 + g * (1.0 - sg)), dh * g * sg])


def _gdn_prep(q, k, v):
    q, k, v = _silu(q), _silu(k), _silu(v)
    q = q * lax.rsqrt(jnp.sum(q * q, axis=-1, keepdims=True) + EPS) * (HD ** -0.5)
    k = k * lax.rsqrt(jnp.sum(k * k, axis=-1, keepdims=True) + EPS)
    return q, k, v


def _gates(ba, a_log, dt_bias):
    lane = lax.broadcasted_iota(jnp.int32, ba.shape, 1)
    beta = jax.nn.sigmoid(ba)
    z = ba + dt_bias
    softplus = jnp.maximum(z, 0.0) + jnp.log(1.0 + jnp.exp(-jnp.abs(z)))
    g = -jnp.exp(a_log) * softplus
    return jnp.where(lane < A_HEADS, beta, jnp.where(lane < 2 * A_HEADS, g, 0.0))


def _outnorm_gate(o, gate, gain):
    return _rms(o, gain) * _silu(gate)


def _memattn(q, k, v):
    lane = lax.shift_right_logical(lax.broadcasted_iota(jnp.int32, (1, MEM_W), 1), 6)
    out = jnp.zeros(q.shape, F32)
    for h in range(MEM_HEADS):
        mh = (lane == h).astype(F32)
        s = _dot(q * mh, k, "nt") * (MEM_HD ** -0.5)
        s = s - lax.stop_gradient(jnp.max(s, axis=-1, keepdims=True))
        p = jnp.exp(s)
        p = p / jnp.sum(p, axis=-1, keepdims=True)
        out = out + _dot(p, v * mh)
    return out


def _rope_mix(a, a_sw, c, s):
    return (a * c + a_sw * s) * Q_PRESCALE


def _kcat(kn, kr, kr_sw, c, s):
    return kn + kr * c + kr_sw * s


@jax.custom_vjp
def _neumann_inverse(nmat):
    n = nmat.shape[0]
    eye = (lax.broadcasted_iota(jnp.int32, (n, n), 0) == lax.broadcasted_iota(jnp.int32, (n, n), 1)).astype(F32)
    pinv = eye + nmat
    npow = nmat
    for _ in range(5):
        npow = _dot(npow, npow)
        pinv = pinv + _dot(pinv, npow)
    return pinv


def _neumann_inverse_fwd(nmat):
    pinv = _neumann_inverse(nmat)
    return pinv, pinv


def _neumann_inverse_bwd(pinv, ct):
    return (_dot(_dot(pinv, ct, "tn"), pinv, "nt"),)


_neumann_inverse.defvjp(_neumann_inverse_fwd, _neumann_inverse_bwd)


@jax.custom_vjp
def _known_inverse(nmat, pinv):
    return pinv


def _known_inverse_fwd(nmat, pinv):
    return pinv, pinv


def _known_inverse_bwd(pinv, ct):
    return _dot(_dot(pinv, ct, "tn"), pinv, "nt"), jnp.zeros_like(pinv)


_known_inverse.defvjp(_known_inverse_fwd, _known_inverse_bwd)


def _gdn_local(q, k, v, beta, gcol, grow, pinv_kept=None):
    n = GROUP
    ri = lax.broadcasted_iota(jnp.int32, (n, n), 0)
    ci = lax.broadcasted_iota(jnp.int32, (n, n), 1)
    same = lax.shift_right_logical(ri, 6) == lax.shift_right_logical(ci, 6)
    lower = jnp.logical_and(same, ci <= ri)
    strict = jnp.logical_and(same, ci < ri)
    gc_col = jnp.sum(lower.astype(F32) * grow, axis=1, keepdims=True)
    gc_row = jnp.sum(jnp.logical_and(same, ri <= ci).astype(F32) * gcol, axis=0, keepdims=True)
    glast = jnp.sum(same.astype(F32) * grow, axis=1, keepdims=True)
    decay = jnp.where(lower, jnp.exp(jnp.where(lower, gc_col - gc_row, 0.0)), 0.0)
    kb = k * beta
    nmat = -jnp.where(strict, _dot(kb, k, "nt") * decay, 0.0)
    pinv = _neumann_inverse(nmat) if pinv_kept is None else _known_inverse(nmat, pinv_kept)
    e_gc = jnp.exp(gc_col)
    u = _dot(pinv, v * beta)
    w = _dot(pinv, kb * e_gc)
    qk = _dot(q, k, "nt") * decay
    fold = (jnp.bitwise_and(lax.broadcasted_iota(jnp.int32, (n, CHUNK), 0), CHUNK - 1)
            == lax.broadcasted_iota(jnp.int32, (n, CHUNK), 1)).astype(F32)
    qk_c = _dot(qk, fold)
    q_dec = q * e_gc
    k_dec = k * jnp.exp(glast - gc_col)
    dmat = jnp.exp(glast) * jnp.ones((1, HD), F32)
    if pinv_kept is None:
        return u, w, q_dec, k_dec, qk_c, dmat, pinv
    return u, w, q_dec, k_dec, qk_c, dmat


def _gdn_step(s, w_c, u_c, qd_c, kd_c, qk_c, d_c):
    v_new = u_c - _dot(w_c, s)
    out = _dot(qd_c, s) + _dot(qk_c, v_new)
    d_row = jnp.mean(d_c, axis=0, keepdims=True)
    s_new = s * d_row + _dot(kd_c, v_new, "tn")
    return s_new, out


def _rmsnorm_fwd(x, gain, name, col=0, width=None):
    t = x.shape[0]
    w = width or x.shape[1]
    (n,) = _rowcall(_rms, [x, gain.reshape(1, w)], [_rows(TM, w, col), _shared((1, w))],
                    [_sds((t, w), BF16)], [_rows(TM, w)], (t // TM,), name)
    return n


def _rmsnorm_bwd(x, gain, dn, dres, name):
    t, w = x.shape
    fn = _vjp_fn(_rms, 2, (0, 1))

    def bwd(xb, gb, dnb, drb):
        dx, dg = fn(xb, gb, dnb)
        return dx + drb, dg

    dx, dg = _rowcall(bwd, [x, gain.reshape(1, w), dn, dres],
                      [_rows(TM, w), _shared((1, w)), _rows(TM, w), _rows(TM, w)],
                      [_sds((t, w), F32), _sds((1, w), F32)], [_rows(TM, w), _shared((1, w))],
                      (t // TM,), name, n_acc=1)
    return dx, dg[0]


def _ffn_fwd(x, gain, wgu8, wd4, tag):
    t = x.shape[0]
    nt = t // TMM
    tf = min(TMF, t)
    n = _rmsnorm_fwd(x, gain, tag + "_norm")
    gu = _matmul(n, wgu8, dims="nn", grid=(N_DEV, t // tf, 1),
                 a_spec=pl.BlockSpec((tf, D), lambda j, i, k: (i, 0)),
                 b_spec=pl.BlockSpec((None, D, FF_SHARD), lambda j, i, k: (j, 0, 0)),
                 o_spec=pl.BlockSpec((None, tf, FF_SHARD), lambda j, i, k: (j, i, 0)),
                 out_shape=_sds((N_DEV, t, FF_SHARD), BF16), name=tag + "_gu")
    gu = gu.reshape(2, 4, t, FF_SHARD)
    y = _matmul(gu, wd4, dims="nn", grid=(nt, 1, 4), a_fn=_swiglu_pair,
                a_spec=pl.BlockSpec((2, None, TMM, FF_SHARD), lambda i, j, k: (0, k, i, 0)),
                b_spec=pl.BlockSpec((None, FF_SHARD, D), lambda i, j, k: (k, 0, 0)),
                o_spec=pl.BlockSpec((TMM, D), lambda i, j, k: (i, 0)),
                out_shape=_sds((t, D), F32), name=tag + "_down", scale=0.5,
                res=x, res_spec=pl.BlockSpec((TMM, D), lambda i, j, k: (i, 0)))
    return y, (x, n, gu)


def _ffn_bwd(d, saved, gain, wgu8, wd4, tag):
    x, n, gu = saved
    t = x.shape[0]
    nt = t // TMM
    dgu = _matmul(d, wd4, dims="nt", grid=(4, nt, 1),
                  a_spec=pl.BlockSpec((TMM, D), lambda j, i, k: (i, 0)),
                  b_spec=pl.BlockSpec((None, FF_SHARD, D), lambda j, i, k: (j, 0, 0)),
                  o_spec=pl.BlockSpec((2, None, TMM, FF_SHARD), lambda j, i, k: (0, j, i, 0)),
                  out_shape=_sds((2, 4, t, FF_SHARD), BF16), name=tag + "_dgu", scale=0.5,
                  res=gu, res_spec=pl.BlockSpec((2, None, TMM, FF_SHARD), lambda j, i, k: (0, j, i, 0)),
                  epilogue=_swiglu_bwd, acc_shape=(TMM, FF_SHARD))
    tf = min(TMF, t)
    nf = t // tf
    dwd4 = _matmul(gu, d, dims="tn", grid=(4, 1, nf), a_fn=_swiglu_pair,
                   a_spec=pl.BlockSpec((2, None, tf, FF_SHARD), lambda j, i, k: (0, j, k, 0)),
                   b_spec=pl.BlockSpec((tf, D), lambda j, i, k: (k, 0)),
                   o_spec=pl.BlockSpec((None, FF_SHARD, D), lambda j, i, k: (j, 0, 0)),
                   out_shape=_sds((4, FF_SHARD, D), F32), name=tag + "_dwd", scale=0.5)
    dgu = dgu.reshape(N_DEV, t, FF_SHARD)
    dwgu8 = _matmul(n, dgu, dims="tn", grid=(N_DEV, 1, nf),
                    a_spec=pl.BlockSpec((tf, D), lambda j, i, k: (k, 0)),
                    b_spec=pl.BlockSpec((None, tf, FF_SHARD), lambda j, i, k: (j, k, 0)),
                    o_spec=pl.BlockSpec((None, D, FF_SHARD), lambda j, i, k: (j, 0, 0)),
                    out_shape=_sds((N_DEV, D, FF_SHARD), F32), name=tag + "_dwgu")
    dn = _matmul(dgu, wgu8, dims="nt", grid=(nf, 1, N_DEV),
                 a_spec=pl.BlockSpec((None, tf, FF_SHARD), lambda i, j, k: (k, i, 0)),
                 b_spec=pl.BlockSpec((None, D, FF_SHARD), lambda i, j, k: (k, 0, 0)),
                 o_spec=pl.BlockSpec((tf, D), lambda i, j, k: (i, 0)),
                 out_shape=_sds((t, D), BF16), name=tag + "_dn")
    dx, dgain = _rmsnorm_bwd(x, gain, dn, d, tag + "_dnorm")
    return dx, dgain, dwgu8, dwd4


CONV_TC = 768
CONV_K = 4


def _conv_fwd(ha, w, name):
    t = ha.shape[0]
    nb = TM // 8

    def body(prev_ref, cur_ref, w_ref, o_ref):
        i = pl.program_id(0)
        cur = cur_ref[...].astype(F32)
        prev = prev_ref[...].astype(F32) * (i > 0).astype(F32)
        ext = jnp.concatenate([prev, cur], axis=0)
        wv = w_ref[...]
        acc = cur * wv[3:4]
        for k in range(1, CONV_K):
            acc = acc + pltpu.roll(ext, k, axis=0)[8:] * wv[3 - k:4 - k]
        o_ref[...] = acc.astype(o_ref.dtype)

    return pl.pallas_call(
        body, out_shape=_sds((3, t, CONV_TC), BF16), grid=(t // TM, 3),
        in_specs=[pl.BlockSpec((8, CONV_TC), lambda i, c: (jnp.maximum(i * nb - 1, 0), c)),
                  pl.BlockSpec((TM, CONV_TC), lambda i, c: (i, c)),
                  pl.BlockSpec((CONV_K, CONV_TC), lambda i, c: (0, c))],
        out_specs=pl.BlockSpec((None, TM, CONV_TC), lambda i, c: (c, i, 0)),
        name=name, compiler_params=_cparams(("parallel", "parallel")))(ha, ha, w)


def _conv_bwd(ha, w, dy3, name):
    t = ha.shape[0]
    nb = TM // 8
    nt = t // TM

    def body(prev_ref, cur_ref, dy_ref, nxt_ref, w_ref, dx_ref, dw_ref):
        i = pl.program_id(1)
        cur = cur_ref[...].astype(F32)
        prev = prev_ref[...].astype(F32) * (i > 0).astype(F32)
        ext = jnp.concatenate([prev, cur], axis=0)
        dy = dy_ref[...].astype(F32)
        nxt = nxt_ref[...].astype(F32) * (i < nt - 1).astype(F32)
        dext = jnp.concatenate([dy, nxt], axis=0)
        wv = w_ref[...]
        dx = dy * wv[3:4]
        dws = [None] * CONV_K
        dws[3] = jnp.sum(dy * cur, axis=0, keepdims=True)
        for k in range(1, CONV_K):
            dx = dx + pltpu.roll(dext, TM + 8 - k, axis=0)[:TM] * wv[3 - k:4 - k]
            dws[3 - k] = jnp.sum(dy * pltpu.roll(ext, k, axis=0)[8:], axis=0, keepdims=True)
        dx_ref[...] = dx.astype(dx_ref.dtype)

        @pl.when(i == 0)
        def _():
            dw_ref[...] = jnp.zeros_like(dw_ref)

        dw_ref[...] += jnp.concatenate(dws, axis=0)

    return pl.pallas_call(
        body, out_shape=(_sds((t, 3 * CONV_TC), BF16), _sds((CONV_K, 3 * CONV_TC), F32)), grid=(3, nt),
        in_specs=[pl.BlockSpec((8, CONV_TC), lambda c, i: (jnp.maximum(i * nb - 1, 0), c)),
                  pl.BlockSpec((TM, CONV_TC), lambda c, i: (i, c)),
                  pl.BlockSpec((None, TM, CONV_TC), lambda c, i: (c, i, 0)),
                  pl.BlockSpec((None, 8, CONV_TC), lambda c, i: (c, jnp.minimum((i + 1) * nb, t // 8 - 1), 0)),
                  pl.BlockSpec((CONV_K, CONV_TC), lambda c, i: (0, c))],
        out_specs=(pl.BlockSpec((TM, CONV_TC), lambda c, i: (i, c)),
                   pl.BlockSpec((CONV_K, CONV_TC), lambda c, i: (0, c))),
        name=name, compiler_params=_cparams(("parallel", "arbitrary")))(ha, ha, dy3, dy3, w)


def _gdn_specs(t, rev):
    ng = t // GROUP

    def gi(g):
        return ng - 1 - g if rev else g

    qkv = pl.BlockSpec((3, GROUP, GDN_HPS * HD), lambda h, g: (0, gi(g), h))
    bg = pl.BlockSpec((GROUP, 128), lambda h, g: (gi(g), 0))
    dbg = pl.BlockSpec((GDN_HPS, GROUP, 128), lambda h, g: (h, gi(g), 0))
    o = pl.BlockSpec((GROUP, GDN_HPS * HD), lambda h, g: (gi(g), h))
    st = pl.BlockSpec((GDN_HPS, None, HD, HD), lambda h, g: (h, gi(g), 0, 0))
    inv = pl.BlockSpec((GDN_HPS, None, GROUP, GROUP), lambda h, g: (h, gi(g), 0, 0))
    return qkv, bg, dbg, o, st, inv


def _head_qkv(qkv_ref, j):
    sl = slice(j * HD, (j + 1) * HD)
    return qkv_ref[0, :, sl].astype(F32), qkv_ref[1, :, sl].astype(F32), qkv_ref[2, :, sl].astype(F32)


def _head_gates(bg, h):
    lane = lax.broadcasted_iota(jnp.int32, (1, 128), 1)
    beta = jnp.sum(jnp.where(lane == h, bg, 0.0), axis=1, keepdims=True)
    gcol = jnp.sum(jnp.where(lane == h + A_HEADS, bg, 0.0), axis=1, keepdims=True)
    return beta, gcol, _col_to_row(gcol)


def _gdn_fwd(qkv3, bg, name):
    t = qkv3.shape[1]
    ng = t // GROUP
    qkv_s, bg_s, _, o_s, st_s, inv_s = _gdn_specs(t, False)

    def body(qkv_ref, bg_ref, o_ref, st_ref, inv_ref, s_scr):
        @pl.when(pl.program_id(1) == 0)
        def _():
            s_scr[...] = jnp.zeros_like(s_scr)

        st_ref[...] = s_scr[...]
        bgv = bg_ref[...]
        loc = [_gdn_local(*_head_qkv(qkv_ref, j), *_head_gates(bgv, pl.program_id(0) * GDN_HPS + j))
               for j in range(GDN_HPS)]
        s = [s_scr[j] for j in range(GDN_HPS)]
        for a in range(GROUP // CHUNK):
            sl = slice(a * CHUNK, (a + 1) * CHUNK)
            for j in range(GDN_HPS):
                u, w, qd, kd, qkc, dm, _ = loc[j]
                s[j], out = _gdn_step(s[j], w[sl], u[sl], qd[sl], kd[sl], qkc[sl], dm[sl])
                o_ref[sl, j * HD:(j + 1) * HD] = out.astype(o_ref.dtype)
        for j in range(GDN_HPS):
            s_scr[j] = s[j]
            inv_ref[j] = loc[j][6].astype(inv_ref.dtype)

    return pl.pallas_call(
        body, out_shape=(_sds((t, A_WIDTH), BF16), _sds((A_HEADS, ng, HD, HD), F32),
                         _sds((A_HEADS, ng, GROUP, GROUP), BF16)),
        grid=(A_HEADS // GDN_HPS, ng), in_specs=[qkv_s, bg_s], out_specs=(o_s, st_s, inv_s),
        scratch_shapes=[pltpu.VMEM((GDN_HPS, HD, HD), F32)], name=name,
        compiler_params=_cparams(("parallel", "arbitrary")))(qkv3, bg)


def _gdn_bwd(qkv3, bg, states, pinvs, do, name):
    t = qkv3.shape[1]
    ng = t // GROUP
    qkv_s, bg_s, dbg_s, o_s, st_s, inv_s = _gdn_specs(t, True)
    nc = GROUP // CHUNK

    def body(qkv_ref, bg_ref, st_ref, inv_ref, do_ref, dqkv_ref, dbg_ref, ds_scr):
        @pl.when(pl.program_id(1) == 0)
        def _():
            ds_scr[...] = jnp.zeros_like(ds_scr)

        heads = range(GDN_HPS)
        bgv = bg_ref[...]
        hid = [pl.program_id(0) * GDN_HPS + j for j in heads]
        fw = [jax.vjp(functools.partial(_gdn_local, pinv_kept=inv_ref[j].astype(F32)),
                      *_head_qkv(qkv_ref, j), *_head_gates(bgv, hid[j])) for j in heads]
        starts = [[None] * nc for _ in heads]
        s = [st_ref[j] for j in heads]
        for a in range(nc):
            sl = slice(a * CHUNK, (a + 1) * CHUNK)
            for j in heads:
                u, w, qd, kd, qkc, dm = fw[j][0]
                starts[j][a] = s[j]
                if a < nc - 1:
                    s[j], _ = _gdn_step(s[j], w[sl], u[sl], qd[sl], kd[sl], qkc[sl], dm[sl])
        ds = [ds_scr[j] for j in heads]
        parts = [[None] * nc for _ in heads]
        for a in reversed(range(nc)):
            sl = slice(a * CHUNK, (a + 1) * CHUNK)
            for j in heads:
                u, w, qd, kd, qkc, dm = fw[j][0]
                _, vjp_step = jax.vjp(_gdn_step, starts[j][a], w[sl], u[sl], qd[sl], kd[sl], qkc[sl], dm[sl])
                grads = vjp_step((ds[j], do_ref[sl, j * HD:(j + 1) * HD].astype(F32)))
                ds[j] = grads[0]
                parts[j][a] = grads[1:]
        lane = lax.broadcasted_iota(jnp.int32, (1, 128), 1)
        for j in heads:
            ds_scr[j] = ds[j]
            dw, du, dqd, dkd, dqk, ddm = [jnp.concatenate([parts[j][a][i] for a in range(nc)], axis=0)
                                          for i in range(6)]
            dq, dk, dv, db, dgc, dgr = fw[j][1]((du, dw, dqd, dkd, dqk, ddm))
            hs = slice(j * HD, (j + 1) * HD)
            dqkv_ref[0, :, hs] = dq.astype(dqkv_ref.dtype)
            dqkv_ref[1, :, hs] = dk.astype(dqkv_ref.dtype)
            dqkv_ref[2, :, hs] = dv.astype(dqkv_ref.dtype)
            dbg_ref[j] = (jnp.where(lane == hid[j], db, 0.0)
                          + jnp.where(lane == hid[j] + A_HEADS, dgc + _row_to_col(dgr), 0.0))

    return pl.pallas_call(
        body, out_shape=(_sds((3, t, A_WIDTH), BF16), _sds((A_HEADS, t, 128), F32)),
        grid=(A_HEADS // GDN_HPS, ng), in_specs=[qkv_s, bg_s, st_s, inv_s, o_s],
        out_specs=(qkv_s, dbg_s), scratch_shapes=[pltpu.VMEM((GDN_HPS, HD, HD), F32)], name=name,
        compiler_params=_cparams(("parallel", "arbitrary")))(qkv3, bg, states, pinvs, do)


NEG = -1e30


def _diag_mask(shape, q_axis):
    qi = lax.shift_right_logical(lax.broadcasted_iota(jnp.int32, shape, q_axis), 6)
    ki = lax.shift_right_logical(lax.broadcasted_iota(jnp.int32, shape, 1 - q_axis), 6)
    return ki <= qi


def _col_to_row(col):
    n = col.shape[0]
    eye = lax.broadcasted_iota(jnp.int32, (n, n), 0) == lax.broadcasted_iota(jnp.int32, (n, n), 1)
    return jnp.sum(jnp.where(eye, col, 0.0), axis=0, keepdims=True)


def _row_to_col(row):
    n = row.shape[1]
    eye = lax.broadcasted_iota(jnp.int32, (n, n), 0) == lax.broadcasted_iota(jnp.int32, (n, n), 1)
    return jnp.sum(jnp.where(eye, row, 0.0), axis=1, keepdims=True)


def _blk(ref, i):
    return ref[pl.ds(pl.multiple_of(i * ATT_BQ, ATT_BQ), ATT_BQ), :]


def _att_fwd(qc, kc, v, name):
    t = qc.shape[0]
    nq = t // ATT_BQ

    def body(q_ref, k_ref, v_ref, o_ref, lse_ref, lser_ref, m_scr, l_scr, acc_scr):
        qb = pl.program_id(1)
        q = q_ref[...]
        m_scr[...] = jnp.full_like(m_scr, NEG)
        l_scr[...] = jnp.zeros_like(l_scr)
        acc_scr[...] = jnp.zeros_like(acc_scr)

        def step(kb, diag):
            s = _dot(q, _blk(k_ref, kb), "nt")
            if diag:
                s = jnp.where(_diag_mask(s.shape, 0), s, NEG)
            m_old = m_scr[...]
            m_new = jnp.maximum(m_old, jnp.max(s, axis=1, keepdims=True))
            alpha = jnp.exp2(m_old - m_new)
            p = jnp.exp2(s - m_new)
            l_scr[...] = alpha * l_scr[...] + jnp.sum(p, axis=1, keepdims=True)
            acc_scr[...] = alpha * acc_scr[...] + _dot(p, _blk(v_ref, kb))
            m_scr[...] = m_new

        def loop_body(kb, carry):
            step(kb, False)
            return carry

        lax.fori_loop(0, qb, loop_body, 0)
        step(qb, True)
        o_ref[...] = (acc_scr[...] / l_scr[...]).astype(o_ref.dtype)
        lse = m_scr[...] + jnp.log2(l_scr[...])
        lse_ref[...] = lse
        lser_ref[...] = _col_to_row(lse)

    return pl.pallas_call(
        body, out_shape=(_sds((t, B_HEADS * HD), BF16), _sds((B_HEADS, t, 1), F32),
                         _sds((B_HEADS, nq, 1, ATT_BQ), F32)), grid=(B_HEADS, nq),
        in_specs=[pl.BlockSpec((ATT_BQ, QK_CAT), lambda h, i: (i, h)),
                  pl.BlockSpec((t, QK_CAT), lambda h, i: (0, h)), pl.BlockSpec((t, HD), lambda h, i: (0, h))],
        out_specs=(pl.BlockSpec((ATT_BQ, HD), lambda h, i: (i, h)),
                   pl.BlockSpec((None, ATT_BQ, 1), lambda h, i: (h, i, 0)),
                   pl.BlockSpec((None, None, 1, ATT_BQ), lambda h, i: (h, i, 0, 0))),
        scratch_shapes=[pltpu.VMEM((ATT_BQ, 1), F32), pltpu.VMEM((ATT_BQ, 1), F32), pltpu.VMEM((ATT_BQ, HD), F32)],
        name=name, compiler_params=_cparams(("parallel", "arbitrary")))(qc, kc, v)


def _att_bwd(qc, kc, v, o, lse, lse_row, do, name):
    t = qc.shape[0]
    nq = t // ATT_BQ

    def delta_fn(ob, dob):
        dl = jnp.sum(ob.astype(F32) * dob.astype(F32), axis=1, keepdims=True)
        return dl, _col_to_row(dl)

    delta, delta_row = _rowcall(
        delta_fn, [o, do], [pl.BlockSpec((ATT_BQ, HD), lambda i, h: (i, h))] * 2,
        [_sds((B_HEADS, t, 1), F32), _sds((B_HEADS, nq, 1, ATT_BQ), F32)],
        [pl.BlockSpec((None, ATT_BQ, 1), lambda i, h: (h, i, 0)),
         pl.BlockSpec((None, None, 1, ATT_BQ), lambda i, h: (h, i, 0, 0))], (nq, B_HEADS), name + "_delta")

    def dq_body(q_ref, k_ref, v_ref, do_ref, lse_ref, dl_ref, dq_ref, acc):
        qb = pl.program_id(1)
        q, dob, lse_b, dl_b = q_ref[...], do_ref[...], lse_ref[...], dl_ref[...]
        acc[...] = jnp.zeros_like(acc)

        def step(kb, diag):
            k = _blk(k_ref, kb)
            s = _dot(q, k, "nt")
            if diag:
                s = jnp.where(_diag_mask(s.shape, 0), s, NEG)
            p = jnp.exp2(s - lse_b)
            ds = p * (_dot(dob, _blk(v_ref, kb), "nt") - dl_b)
            acc[...] += _dot(ds, k)

        def loop_body(kb, carry):
            step(kb, False)
            return carry

        lax.fori_loop(0, qb, loop_body, 0)
        step(qb, True)
        dq_ref[...] = (acc[...] * ATT_SCALE).astype(dq_ref.dtype)

    qmap = lambda h, i: (i, h)
    colq = pl.BlockSpec((None, ATT_BQ, 1), lambda h, i: (h, i, 0))
    dq = pl.pallas_call(
        dq_body, out_shape=_sds((t, B_HEADS * QK_CAT), BF16), grid=(B_HEADS, nq),
        in_specs=[pl.BlockSpec((ATT_BQ, QK_CAT), qmap), pl.BlockSpec((t, QK_CAT), lambda h, i: (0, h)),
                  pl.BlockSpec((t, HD), lambda h, i: (0, h)), pl.BlockSpec((ATT_BQ, HD), qmap), colq, colq],
        out_specs=pl.BlockSpec((ATT_BQ, QK_CAT), qmap),
        scratch_shapes=[pltpu.VMEM((ATT_BQ, QK_CAT), F32)], name=name + "_dq",
        compiler_params=_cparams(("parallel", "arbitrary")))(qc, kc, v, do, lse, delta)

    def dkv_body(k_ref, v_ref, q_ref, do_ref, lser_ref, dlr_ref, dk_ref, dv_ref, dk_acc, dv_acc):
        kb = pl.program_id(1)
        k, vv = k_ref[...], v_ref[...]
        dk_acc[...] = jnp.zeros_like(dk_acc)
        dv_acc[...] = jnp.zeros_like(dv_acc)

        def step(qb, diag):
            q, dob = _blk(q_ref, qb), _blk(do_ref, qb)
            st = _dot(k, q, "nt")
            if diag:
                st = jnp.where(_diag_mask(st.shape, 1), st, NEG)
            pt = jnp.exp2(st - lser_ref[qb])
            dst = pt * (_dot(vv, dob, "nt") - dlr_ref[qb])
            dv_acc[...] += _dot(pt, dob)
            dk_acc[...] += _dot(dst, q)

        def loop_body(qb, carry):
            step(qb, False)
            return carry

        step(kb, True)
        lax.fori_loop(kb + 1, nq, loop_body, 0)
        dk_ref[...] = (dk_acc[...] * LN2).astype(dk_ref.dtype)
        dv_ref[...] = dv_acc[...].astype(dv_ref.dtype)

    kmap = lambda h, j: (j, h)
    rowq = pl.BlockSpec((None, nq, 1, ATT_BQ), lambda h, j: (h, 0, 0, 0))
    dk, dv = pl.pallas_call(
        dkv_body, out_shape=(_sds((t, B_HEADS * QK_CAT), BF16), _sds((t, B_HEADS * HD), BF16)),
        grid=(B_HEADS, nq),
        in_specs=[pl.BlockSpec((ATT_BQ, QK_CAT), kmap), pl.BlockSpec((ATT_BQ, HD), kmap),
                  pl.BlockSpec((t, QK_CAT), lambda h, j: (0, h)), pl.BlockSpec((t, HD), lambda h, j: (0, h)),
                  rowq, rowq],
        out_specs=(pl.BlockSpec((ATT_BQ, QK_CAT), kmap), pl.BlockSpec((ATT_BQ, HD), kmap)),
        scratch_shapes=[pltpu.VMEM((ATT_BQ, QK_CAT), F32), pltpu.VMEM((ATT_BQ, HD), F32)], name=name + "_dkv",
        compiler_params=_cparams(("parallel", "arbitrary")))(kc, v, qc, do, lse_row, delta_row)
    return dq, dk, dv


def _mem_fwd(hx, col, mkv, name):
    t = hx.shape[0]
    (o,) = _rowcall(_memattn, [hx, mkv, mkv],
                    [_rows(TM, MEM_W, col), pl.BlockSpec((N_MEM, MEM_W), lambda i: (0, 0)),
                     pl.BlockSpec((N_MEM, MEM_W), lambda i: (0, 1))],
                    [_sds((t, MEM_W), BF16)], [_rows(TM, MEM_W)], (t // TM,), name)
    return o


def _mem_bwd(hx, col, mkv, do, do_col, name):
    t = hx.shape[0]
    dq, dk, dv = _rowcall(_vjp_fn(_memattn, 3, (0, 1, 2)), [hx, mkv, mkv, do],
                          [_rows(TM, MEM_W, col), pl.BlockSpec((N_MEM, MEM_W), lambda i: (0, 0)),
                           pl.BlockSpec((N_MEM, MEM_W), lambda i: (0, 1)), _rows(TM, MEM_W, do_col)],
                          [_sds((t, MEM_W), BF16), _sds((N_MEM, MEM_W), F32), _sds((N_MEM, MEM_W), F32)],
                          [_rows(TM, MEM_W), _shared((N_MEM, MEM_W)), _shared((N_MEM, MEM_W))],
                          (t // TM,), name, n_acc=2)
    return dq, jnp.concatenate([dk, dv], axis=1)


def _a_in_ext(w):
    nb = 4 * A_WIDTH
    ba = jnp.pad(w[:, nb:nb + 2 * A_HEADS], ((0, 0), (0, 128 - 2 * A_HEADS)))
    return jnp.concatenate([w[:, :nb], w[:, nb + 2 * A_HEADS:], ba], axis=1)


def _swap_halves(w):
    return jnp.concatenate([w[..., QK_ROPE // 2:], w[..., :QK_ROPE // 2]], axis=-1)


def _uq_ext(w):
    w = w.reshape(Q_LORA, B_HEADS, QK_NOPE + QK_ROPE)
    nope, rope = w[..., :QK_NOPE], w[..., QK_NOPE:]
    z64 = jnp.zeros((Q_LORA, B_HEADS, QK_CAT - QK_NOPE - QK_ROPE), w.dtype)
    z128 = jnp.zeros((Q_LORA, B_HEADS, QK_NOPE), w.dtype)
    a = jnp.concatenate([nope, rope, z64], axis=-1).reshape(Q_LORA, B_HEADS * QK_CAT)
    b = jnp.concatenate([z128, _swap_halves(rope), z64], axis=-1).reshape(Q_LORA, B_HEADS * QK_CAT)
    return jnp.concatenate([a, b], axis=1)


def _dkv_ext(w):
    ckv, kr = w[:, :KV_LORA], w[:, KV_LORA:]
    z128 = jnp.zeros((D, QK_NOPE), w.dtype)
    z64 = jnp.zeros((D, QK_CAT - QK_NOPE - QK_ROPE), w.dtype)
    return jnp.concatenate([ckv, z128, kr, z64, z128, _swap_halves(kr), z64], axis=1)


def _ukv_ext(w):
    w = w.reshape(KV_LORA, B_HEADS, QK_NOPE + HD)
    kn, vv = w[..., :QK_NOPE], w[..., QK_NOPE:]
    z = jnp.zeros((KV_LORA, B_HEADS, QK_CAT - QK_NOPE), w.dtype)
    a = jnp.concatenate([kn, z], axis=-1).reshape(KV_LORA, B_HEADS * QK_CAT)
    return jnp.concatenate([a, vv.reshape(KV_LORA, B_HEADS * HD)], axis=1)


def _ext_and_back(fn, w):
    ext, back = jax.vjp(fn, w.astype(F32))
    return ext.astype(BF16), lambda g: back(g.astype(F32))[0]


def _rope_tables(pos_col):
    t = pos_col.shape[0]
    inv = (ROPE_THETA ** (-np.arange(0, QK_ROPE, 2, dtype=np.float32) / QK_ROPE)).astype(np.float32)
    inv_row = np.zeros((1, QK_CAT), np.float32)
    inv_row[0, QK_NOPE:QK_NOPE + QK_ROPE] = np.concatenate([inv, inv])
    sign = np.zeros((1, QK_CAT), np.float32)
    sign[0, QK_NOPE:QK_NOPE + QK_ROPE // 2] = -1.0
    sign[0, QK_NOPE + QK_ROPE // 2:QK_NOPE + QK_ROPE] = 1.0
    is_rope = np.abs(sign)
    is_nope = np.zeros((1, QK_CAT), np.float32)
    is_nope[0, :QK_NOPE] = 1.0

    def fn(p, inv_b, sign_b, rope_b, nope_b):
        ang = p.astype(F32) * inv_b
        return jnp.cos(ang) * rope_b + nope_b, jnp.sin(ang) * sign_b

    consts = [jnp.asarray(a) for a in (inv_row, sign, is_rope, is_nope)]
    return _rowcall(fn, [pos_col] + consts, [_rows(TM, 1)] + [_shared((1, QK_CAT))] * 4,
                    [_sds((t, QK_CAT), F32)] * 2, [_rows(TM, QK_CAT)] * 2, (t // TM,), "rope_tables")


def _local_step(x, mem, pos, target, w, layer_start, layer_done):
    t = x.shape[0]
    g = {}
    head6 = (t // TM, A_HEADS)

    mem_n = _rmsnorm_fwd_small(mem, w["mem_norm"])
    rope_c, rope_s = _rope_tables(pos.reshape(t, 1))
    mkv = [_mm(mem_n, w["w_mem_kv"][l], "nn", BF16, f"mkv{l}") for l in range(DEPTH)]

    saved = []
    for l in range(DEPTH):
        sv = {}
        layer_start(l, x)
        x, sv["ffn1"] = _ffn_fwd(x, w["ffn1_norm"][l], w["ffn1_w_gu"][l], w["ffn1_w_down"][l], "ffn1")
        sv["x1"] = x
        n2 = _rmsnorm_fwd(x, w["mix_norm"][l], "mix_norm")
        sv["n2"] = n2
        if l < N_A:
            ha = _mm(n2, w["a_w_in"][l], "nn", BF16, "a_in")
            yc3 = _conv_fwd(ha, w["a_conv"][l], "a_conv")
            blk3 = pl.BlockSpec((3, TM, HD), lambda i, h: (0, i, h))
            (qkv3,) = _rowcall(lambda b: jnp.stack(_gdn_prep(b[0].astype(F32), b[1].astype(F32), b[2].astype(F32))),
                               [yc3], [blk3], [_sds((3, t, A_WIDTH), BF16)], [blk3], head6, "a_prep")
            (bg,) = _rowcall(_gates, [ha, _pad128(w["a_A_log"][l], A_HEADS), _pad128(w["a_dt_bias"][l], A_HEADS)],
                             [_rows(TM, 128, A_BA_BLK), _shared((1, 128)), _shared((1, 128))],
                             [_sds((t, 128), F32)], [_rows(TM, 128)], (t // TM,), "a_gates")
            o_gdn, states, pinvs = _gdn_fwd(qkv3, bg, "a_gdn")
            (o_a,) = _rowcall(_outnorm_gate, [o_gdn, ha, w["a_out_norm"][l].reshape(1, HD)],
                              [pl.BlockSpec((TM, HD), lambda i, h: (i, h)),
                               pl.BlockSpec((TM, HD), lambda i, h: (i, 3 * A_HEADS + h)), _shared((1, HD))],
                              [_sds((t, A_WIDTH), BF16)], [pl.BlockSpec((TM, HD), lambda i, h: (i, h))],
                              head6, "a_outnorm")
            o_m = _mem_fwd(ha, A_MQ_BLK, mkv[l], "mem_attn_a")
            sv.update(ha=ha, yc3=yc3, qkv3=qkv3, bg=bg, states=states, pinvs=pinvs, o_gdn=o_gdn)
            cat = jnp.concatenate([o_a, o_m], axis=1)
        else:
            j = l - N_A
            hb = _mm(n2, w["b_w_in"][j], "nn", BF16, "b_in")
            cqn = _rmsnorm_fwd(hb, w["b_q_norm"][j], "b_qnorm", 0, Q_LORA)
            qq = _mm(cqn, w["b_w_uq"][j], "nn", BF16, "b_uq")
            (qc,) = _rowcall(_rope_mix, [qq, qq, rope_c, rope_s],
                             [pl.BlockSpec((TM, QK_CAT), lambda i, h: (i, h)),
                              pl.BlockSpec((TM, QK_CAT), lambda i, h: (i, B_HEADS + h)),
                              pl.BlockSpec((TM, QK_CAT), lambda i, h: (i, 0)),
                              pl.BlockSpec((TM, QK_CAT), lambda i, h: (i, 0))],
                             [_sds((t, B_HEADS * QK_CAT), BF16)], [pl.BlockSpec((TM, QK_CAT), lambda i, h: (i, h))],
                             head6, "b_qrope")
            o_b, lse, lse_row = _att_fwd(qc, kcat, vmla, "b_attn")
            o_m = _mem_fwd(hb, 1, mkv[l], "mem_attn_b")
            sv.update(hb=hb, cqn=cqn, qc=qc, o_b=o_b, lse=(lse, lse_row))
            cat = jnp.concatenate([o_b, o_m], axis=1)
        sv["cat"] = cat
        x = _mm(cat, w["w_out"][l], "nn", F32, "w_out", res=x)
        x, sv["ffn2"] = _ffn_fwd(x, w["ffn2_norm"][l], w["ffn2_w_gu"][l], w["ffn2_w_down"][l], "ffn2")
        saved.append(sv)
        if l == N_A - 1:
            x_kv = x
            nkv = _rmsnorm_fwd(x, w["kv_in_norm"], "kv_in_norm")
            ckr = _mm(nkv, w["w_dkv"], "nn", BF16, "kv_down")
            ckv_n = _rmsnorm_fwd(ckr, w["kv_lat_norm"], "kv_lat_norm", 0, KV_LORA)
            kvu = _mm(ckv_n, w["w_ukv"], "nn", BF16, "kv_up")
            vmla = kvu[:, B_HEADS * QK_CAT:]
            (kcat,) = _rowcall(_kcat, [kvu, ckr, ckr, rope_c, rope_s],
                               [pl.BlockSpec((TM, QK_CAT), lambda i, h: (i, h)),
                                pl.BlockSpec((TM, QK_CAT), lambda i, h: (i, 1)),
                                pl.BlockSpec((TM, QK_CAT), lambda i, h: (i, 2)),
                                pl.BlockSpec((TM, QK_CAT), lambda i, h: (i, 0)),
                                pl.BlockSpec((TM, QK_CAT), lambda i, h: (i, 0))],
                               [_sds((t, B_HEADS * QK_CAT), BF16)],
                               [pl.BlockSpec((TM, QK_CAT), lambda i, h: (i, h))], head6, "kv_cat")

    def loss_fn(xb, gb, tb):
        def f(xx, gg):
            e = _rms(xx, gg) - tb
            return 0.5 * jnp.sum(jnp.mean(e * e, axis=-1, keepdims=True), axis=0, keepdims=True)
        val, vjp = jax.vjp(f, xb, gb)
        dx, dg = vjp(jnp.ones((1, 1), F32))
        return dx, dg, val * jnp.ones((1, 128), F32)

    d, dfin, loss = _rowcall(loss_fn, [x, w["final_norm"].reshape(1, D), target],
                             [_rows(TM, D), _shared((1, D)), _rows(TM, D)],
                             [_sds((t, D), F32), _sds((1, D), F32), _sds((1, 128), F32)],
                             [_rows(TM, D), _shared((1, D)), _shared((1, 128))], (t // TM,), "loss_head", n_acc=2)
    g["final_norm"] = dfin[0]
    loss = loss[0, 0]

    for name in ("ffn1_norm", "ffn1_w_gu", "ffn1_w_down", "mix_norm", "ffn2_norm", "ffn2_w_gu", "ffn2_w_down",
                 "w_out", "w_mem_kv"):
        g[name] = [None] * DEPTH
    for name in ("a_w_in", "a_conv", "a_A_log", "a_dt_bias", "a_out_norm"):
        g[name] = [None] * N_A
    for name in ("b_w_in", "b_q_norm", "b_w_uq"):
        g[name] = [None] * N_B
    dmkv = [None] * DEPTH
    dkcat = []
    dvmla = []

    for l in reversed(range(DEPTH)):
        sv = saved[l]
        if l == N_A - 1:
            kq = pl.BlockSpec((TM, QK_CAT), lambda i, h: (i, h))
            tab = pl.BlockSpec((TM, QK_CAT), lambda i, h: (i, 0))

            def dk_fn(c, s, d0, d1):
                dk = d0.astype(F32) + d1.astype(F32)
                return dk, dk * c, dk * s

            dkn, dkr_h, dkrs_h = _rowcall(dk_fn, [rope_c, rope_s, dkcat[0], dkcat[1]], [tab, tab, kq, kq],
                                          [_sds((t, B_HEADS * QK_CAT), BF16)] + [_sds((B_HEADS, t, QK_CAT), BF16)] * 2,
                                          [kq] + [pl.BlockSpec((None, TM, QK_CAT), lambda i, h: (h, i, 0))] * 2,
                                          head6, "kv_dcat")

            def sum6(a, b):
                return jnp.sum(a.astype(F32), axis=0), jnp.sum(b.astype(F32), axis=0)

            h6 = pl.BlockSpec((B_HEADS, TM, QK_CAT), lambda i: (0, i, 0))
            dkr, dkrs = _rowcall(sum6, [dkr_h, dkrs_h], [h6, h6], [_sds((t, QK_CAT), BF16)] * 2,
                                 [_rows(TM, QK_CAT)] * 2, (t // TM,), "kv_dkr")

            def addv(a, b):
                return a.astype(F32) + b.astype(F32)

            (dv,) = _rowcall(addv, dvmla, [_rows(TM, B_HEADS * HD)] * 2, [_sds((t, B_HEADS * HD), BF16)],
                             [_rows(TM, B_HEADS * HD)], (t // TM,), "kv_dv")
            dkvu = jnp.concatenate([dkn, dv], axis=1)
            g["w_ukv"] = _mm(ckv_n, dkvu, "tn", F32, "kv_up_dw")
            dckv_n = _mm(dkvu, w["w_ukv"], "nt", BF16, "kv_up_dx")

            def lat_bwd(cb, gb, dnb):
                return _vjp_fn(_rms, 2, (0, 1))(cb, gb, dnb)

            dckv, g["kv_lat_norm"] = _rowcall(lat_bwd, [ckr, w["kv_lat_norm"].reshape(1, KV_LORA), dckv_n],
                                              [_rows(TM, KV_LORA), _shared((1, KV_LORA)), _rows(TM, KV_LORA)],
                                              [_sds((t, KV_LORA), BF16), _sds((1, KV_LORA), F32)],
                                              [_rows(TM, KV_LORA), _shared((1, KV_LORA))], (t // TM,),
                                              "kv_lat_dnorm", n_acc=1)
            g["kv_lat_norm"] = g["kv_lat_norm"][0]
            dckr = jnp.concatenate([dckv, dkr, dkrs], axis=1)
            g["w_dkv"] = _mm(nkv, dckr, "tn", F32, "kv_down_dw")
            dnkv = _mm(dckr, w["w_dkv"], "nt", BF16, "kv_down_dx")
            d, g["kv_in_norm"] = _rmsnorm_bwd(x_kv, w["kv_in_norm"], dnkv, d, "kv_in_dnorm")

        d, g["ffn2_norm"][l], g["ffn2_w_gu"][l], g["ffn2_w_down"][l] = _ffn_bwd(
            d, sv["ffn2"], w["ffn2_norm"][l], w["ffn2_w_gu"][l], w["ffn2_w_down"][l], "ffn2b")
        g["w_out"][l] = _mm(sv["cat"], d, "tn", F32, "w_out_dw")
        dcat = _mm(d, w["w_out"][l], "nt", BF16, "w_out_dx")
        if l < N_A:
            ha, yc3, qkv3, states, o_gdn = sv["ha"], sv["yc3"], sv["qkv3"], sv["states"], sv["o_gdn"]
            dmq, dmkv[l] = _mem_bwd(ha, A_MQ_BLK, mkv[l], dcat, 3, "mem_attn_a_bwd")
            hblk = pl.BlockSpec((TM, HD), lambda i, h: (i, h))
            do_gdn, dgate, dgain = _rowcall(
                _vjp_fn(_outnorm_gate, 3, (0, 1, 2)), [o_gdn, ha, w["a_out_norm"][l].reshape(1, HD), dcat],
                [hblk, pl.BlockSpec((TM, HD), lambda i, h: (i, 3 * A_HEADS + h)), _shared((1, HD)), hblk],
                [_sds((t, A_WIDTH), BF16), _sds((t, A_WIDTH), BF16), _sds((1, HD), F32)],
                [hblk, hblk, _shared((1, HD))], head6, "a_outnorm_bwd", n_acc=1)
            g["a_out_norm"][l] = dgain[0]
            dqkv3, dbg6 = _gdn_bwd(qkv3, sv["bg"], states, sv["pinvs"], do_gdn, "a_gdn_bwd")

            def dgates(bab, alb, dtb, d6):
                return _vjp_fn(_gates, 3, (0, 1, 2))(bab, alb, dtb, jnp.sum(d6, axis=0))

            dba, dalog, ddt = _rowcall(
                dgates, [ha, _pad128(w["a_A_log"][l], A_HEADS), _pad128(w["a_dt_bias"][l], A_HEADS), dbg6],
                [_rows(TM, 128, A_BA_BLK), _shared((1, 128)), _shared((1, 128)),
                 pl.BlockSpec((A_HEADS, TM, 128), lambda i: (0, i, 0))],
                [_sds((t, 128), BF16), _sds((1, 128), F32), _sds((1, 128), F32)],
                [_rows(TM, 128), _shared((1, 128)), _shared((1, 128))], (t // TM,), "a_gates_bwd", n_acc=2)
            g["a_A_log"][l] = dalog[0, A_HEADS:2 * A_HEADS]
            g["a_dt_bias"][l] = ddt[0, A_HEADS:2 * A_HEADS]
            blk3 = pl.BlockSpec((3, TM, HD), lambda i, h: (0, i, h))
            def dprep(b, db):
                return jnp.stack(_vjp_fn(_gdn_prep, 3, (0, 1, 2))(b[0], b[1], b[2], db[0], db[1], db[2]))

            (dyc3,) = _rowcall(dprep, [yc3, dqkv3], [blk3, blk3],
                               [_sds((3, t, A_WIDTH), BF16)], [blk3], head6, "a_prep_bwd")
            dqkv_in, g["a_conv"][l] = _conv_bwd(ha, w["a_conv"][l], dyc3, "a_conv_bwd")
            dha = jnp.concatenate([dqkv_in, dgate, dmq, dba], axis=1)
            g["a_w_in"][l] = _mm(sv["n2"], dha, "tn", F32, "a_in_dw")
            dn2 = _mm(dha, w["a_w_in"][l], "nt", BF16, "a_in_dx")
        else:
            j = l - N_A
            hb, cqn, qc, o_b, lse = sv["hb"], sv["cqn"], sv["qc"], sv["o_b"], sv["lse"]
            dmq, dmkv[l] = _mem_bwd(hb, 1, mkv[l], dcat, 3, "mem_attn_b_bwd")
            dqc, dkc, dvv = _att_bwd(qc, kcat, vmla, o_b, lse[0], lse[1], dcat, "b_attn_bwd")
            dkcat.append(dkc)
            dvmla.append(dvv)
            kq = pl.BlockSpec((TM, QK_CAT), lambda i, h: (i, h))
            tab = pl.BlockSpec((TM, QK_CAT), lambda i, h: (i, 0))

            def dq_fn(c, s, dq):
                dq = dq.astype(F32)
                return jnp.stack([dq * c, dq * s])

            (dqq,) = _rowcall(dq_fn, [rope_c, rope_s, dqc], [tab, tab, kq],
                              [_sds((2, t, B_HEADS * QK_CAT), BF16)],
                              [pl.BlockSpec((2, TM, QK_CAT), lambda i, h: (0, i, h))], head6, "b_qrope_bwd")
            dqq = jnp.concatenate([dqq[0], dqq[1]], axis=1)
            g["b_w_uq"][j] = _mm(cqn, dqq, "tn", F32, "b_uq_dw")
            dcqn = _mm(dqq, w["b_w_uq"][j], "nt", BF16, "b_uq_dx")
            dcq, dqg = _rowcall(_vjp_fn(_rms, 2, (0, 1)), [hb, w["b_q_norm"][j].reshape(1, Q_LORA), dcqn],
                                [_rows(TM, Q_LORA), _shared((1, Q_LORA)), _rows(TM, Q_LORA)],
                                [_sds((t, Q_LORA), BF16), _sds((1, Q_LORA), F32)],
                                [_rows(TM, Q_LORA), _shared((1, Q_LORA))], (t // TM,), "b_qnorm_bwd", n_acc=1)
            g["b_q_norm"][j] = dqg[0]
            dhb = jnp.concatenate([dcq, dmq], axis=1)
            g["b_w_in"][j] = _mm(sv["n2"], dhb, "tn", F32, "b_in_dw")
            dn2 = _mm(dhb, w["b_w_in"][j], "nt", BF16, "b_in_dx")
        d, g["mix_norm"][l] = _rmsnorm_bwd(sv["x1"], w["mix_norm"][l], dn2, d, "mix_dnorm")
        d, g["ffn1_norm"][l], g["ffn1_w_gu"][l], g["ffn1_w_down"][l] = _ffn_bwd(
            d, sv["ffn1"], w["ffn1_norm"][l], w["ffn1_w_gu"][l], w["ffn1_w_down"][l], "ffn1b")
        d = layer_done(l, g, d)

    dmem_n = None
    for l in range(DEPTH):
        g["w_mem_kv"][l] = _mm(mem_n, dmkv[l], "tn", F32, f"mkv_dw{l}")
        dmem_n = _mm(dmkv[l], w["w_mem_kv"][l], "nt", F32, f"mkv_dx{l}", res=dmem_n)
    (_, gmn) = _rowcall(_vjp_fn(_rms, 2, (0, 1)), [mem, w["mem_norm"].reshape(1, D), dmem_n],
                        [_shared((N_MEM, D)), _shared((1, D)), _shared((N_MEM, D))],
                        [_sds((N_MEM, D), F32), _sds((1, D), F32)], [_shared((N_MEM, D)), _shared((1, D))],
                        (1,), "mem_dnorm")
    g["mem_norm"] = gmn[0]
    return loss, d, g


def _pad128(v, offset):
    return jnp.pad(v.astype(F32).reshape(1, -1), ((0, 0), (offset, 128 - offset - v.shape[0])))


def _rmsnorm_fwd_small(x, gain):
    r, w = x.shape
    (n,) = _rowcall(_rms, [x, gain.reshape(1, w)], [_shared((r, w)), _shared((1, w))],
                    [_sds((r, w), BF16)], [_shared((r, w))], (1,), "mem_norm")
    return n


def _exchange(srcs, gather, name):
    n = len(srcs)
    blks = [tuple(s.shape) if gather else tuple(s.shape[1:]) for s in srcs]

    def body(*refs):
        src_refs, out_refs = refs[:n], refs[n:2 * n]
        send_sems, recv_sems, local_sems = refs[2 * n:]
        x, y, c = lax.axis_index("x"), lax.axis_index("y"), lax.axis_index("c")
        me = 4 * x + 2 * y + c
        copies = []
        for k in range(1, N_DEV):
            px = (x + (k >> 2 & 1)) % 2
            py = (y + (k >> 1 & 1)) % 2
            pc = (c + (k & 1)) % 2
            peer = 4 * px + 2 * py + pc
            for a in range(n):
                cp = pltpu.make_async_remote_copy(
                    src_ref=src_refs[a] if gather else src_refs[a].at[peer], dst_ref=out_refs[a].at[me],
                    send_sem=send_sems.at[a, k - 1], recv_sem=recv_sems.at[a, k - 1],
                    device_id=(px, py, pc), device_id_type=pl.DeviceIdType.MESH)
                cp.start()
                copies.append(cp)
        for a in range(n):
            cp = pltpu.make_async_copy(src_refs[a] if gather else src_refs[a].at[me], out_refs[a].at[me],
                                       local_sems.at[a])
            cp.start()
            copies.append(cp)
        for cp in copies:
            cp.wait()

    return pl.pallas_call(
        body, out_shape=tuple(_sds((N_DEV,) + b, s.dtype) for b, s in zip(blks, srcs)),
        in_specs=[pl.BlockSpec(memory_space=pl.ANY)] * n, out_specs=tuple([pl.BlockSpec(memory_space=pl.ANY)] * n),
        scratch_shapes=[pltpu.SemaphoreType.DMA((n, N_DEV - 1)), pltpu.SemaphoreType.DMA((n, N_DEV - 1)),
                        pltpu.SemaphoreType.DMA((n,))],
        name=name)(*srcs)


_HBM = pl.BlockSpec(memory_space=pltpu.HBM)
_SEM = pl.BlockSpec(memory_space=pltpu.SEMAPHORE)
_EFFECT = pltpu.SideEffectType.DATAFLOW_SIDE_EFFECTING


def _split_copies(src_refs, land_refs, send_sems, recv_sems, gather):
    x, y, c = lax.axis_index("x"), lax.axis_index("y"), lax.axis_index("c")
    me = 4 * x + 2 * y + c
    copies = []
    for k in range(1, N_DEV):
        peer = ((x + (k >> 2 & 1)) % 2, (y + (k >> 1 & 1)) % 2, (c + (k & 1)) % 2)
        for a in range(len(src_refs)):
            i = (k - 1) * len(src_refs) + a
            src = src_refs[a] if gather else src_refs[a].at[4 * peer[0] + 2 * peer[1] + peer[2]]
            copies.append(pltpu.make_async_remote_copy(
                src_ref=src, dst_ref=land_refs[a].at[me], send_sem=send_sems[i],
                recv_sem=recv_sems[i], device_id=peer, device_id_type=pl.DeviceIdType.MESH))
    return copies


def _split_start(srcs, gather, name):
    n = len(srcs)
    srcs = [pltpu.with_memory_space_constraint(s, pltpu.HBM) for s in srcs]
    lands = [pltpu.with_memory_space_constraint(
        lax.empty(((N_DEV,) + s.shape) if gather else s.shape, s.dtype), pltpu.HBM) for s in srcs]

    ns = n * (N_DEV - 1)

    def body(*refs):
        sems = refs[2 * n:2 * n + 2 * ns]
        for cp in _split_copies(refs[:n], refs[n:2 * n], sems[:ns], sems[ns:], gather):
            cp.start()
        refs[-1][...] = jnp.zeros_like(refs[-1])

    outs = pl.pallas_call(
        body, name=name,
        out_shape=(*[pltpu.SemaphoreType.DMA(())] * (2 * ns), *[pltpu.HBM(a.shape, a.dtype) for a in srcs + lands],
                   _sds((8, 128), F32)),
        in_specs=[_HBM] * (2 * n),
        out_specs=(*[_SEM] * (2 * ns), *[_HBM] * (2 * n), pl.BlockSpec(memory_space=pltpu.VMEM)),
        input_output_aliases={i: 2 * ns + i for i in range(2 * n)},
        compiler_params=pltpu.CompilerParams(has_side_effects=_EFFECT))(*srcs, *lands)
    sems, rest = list(outs[:2 * ns]), outs[2 * ns:]
    return sems[:ns], sems[ns:], list(rest[:n]), list(rest[n:2 * n]), rest[-1]


def _split_wait(flight, after, gather, name):
    send_sems, recv_sems, srcs, lands, _ = flight
    n = len(srcs)

    ns = len(send_sems)

    def body(*refs):
        sems = refs[2 * n:2 * n + 2 * ns]
        for cp in _split_copies(refs[:n], refs[n:2 * n], sems[:ns], sems[ns:], gather):
            cp.wait_send()
            cp.wait_recv()

    outs = pl.pallas_call(
        body, name=name, out_shape=tuple(pltpu.HBM(a.shape, a.dtype) for a in srcs + lands),
        in_specs=[_HBM] * (2 * n) + [_SEM] * (2 * ns) + [pl.BlockSpec(memory_space=pl.ANY)],
        out_specs=tuple([_HBM] * (2 * n)), input_output_aliases={i: i for i in range(2 * n)},
        compiler_params=pltpu.CompilerParams(has_side_effects=_EFFECT))(*srcs, *lands, *send_sems, *recv_sems, after)
    return list(outs[n:])


def _reduce_adamw(parts, wp, mp, vp, name):
    r, cols = wp.shape
    tr = _tile_rows(r, cols)
    c1 = 1.0 - ADAM_B1 ** ADAM_STEP
    c2 = 1.0 - ADAM_B2 ** ADAM_STEP

    def fn(pb, wb, mb, vb):
        gsum = pb[0].astype(F32)
        for j in range(1, N_DEV):
            gsum = gsum + pb[j].astype(F32)
        m_new = ADAM_B1 * mb + (1.0 - ADAM_B1) * gsum
        v_new = ADAM_B2 * vb + (1.0 - ADAM_B2) * (gsum * gsum)
        delta = -ADAM_LR * ((m_new / c1) / (jnp.sqrt(v_new / c2) + ADAM_EPS) + ADAM_WD * wb)
        return gsum, delta, m_new, v_new

    row = _rows(tr, cols)
    return _rowcall(fn, [parts, wp, mp, vp],
                    [pl.BlockSpec((N_DEV, tr, cols), lambda i: (0, i, 0)), row, row, row],
                    [_sds((r, cols), F32)] * 4, [row] * 4, (r // tr,), name)


def _tile_rows(r, cols):
    for t in (512, 256, 128, 64, 32, 16):
        if r % t == 0 and t * cols <= 160 * 1024:
            return t
    return r


def _pack(arrs):
    flat = jnp.concatenate([a.reshape(-1).astype(F32) for a in arrs])
    n = flat.shape[0]
    unit = PACK_W * PACK_ROWS
    tot = -(-n // unit) * unit
    return jnp.pad(flat, (0, tot - n)).reshape(tot // PACK_W, PACK_W)


def _unpack(buf, shapes):
    out, off = [], 0
    flat = buf.reshape(-1)
    for s in shapes:
        n = int(np.prod(s))
        out.append(flat[off:off + n].reshape(s))
        off += n
    return out


def _as2d(a):
    return a.reshape(-1, a.shape[-1])


_SHARDED = ["ffn1_w_gu", "ffn1_w_down", "ffn2_w_gu", "ffn2_w_down", "w_out", "w_mem_kv", "a_w_in", "a_conv",
            "b_w_in", "b_w_uq", "w_dkv", "w_ukv"]
_COL_SHARDED = {"ffn1_w_gu", "ffn2_w_gu", "a_conv", "b_w_uq", "w_ukv"}
_LAYERED = {"ffn1_w_gu": DEPTH, "ffn1_w_down": DEPTH, "ffn2_w_gu": DEPTH, "ffn2_w_down": DEPTH, "w_out": DEPTH,
            "w_mem_kv": DEPTH, "a_w_in": N_A, "a_conv": N_A, "b_w_in": N_B, "b_w_uq": N_B}
_GATHER_FIRST = [("ffn1_w_gu", 0, 1), ("ffn1_w_down", 0, 1), ("ffn2_w_gu", 0, 1), ("ffn2_w_down", 0, 1),
                 ("w_out", 0, 1), ("a_w_in", 0, 1), ("a_conv", 0, 1), ("w_mem_kv", 0, DEPTH)]
_GATHER_REST = [("ffn1_w_gu", 1, DEPTH), ("ffn1_w_down", 1, DEPTH), ("ffn2_w_gu", 1, DEPTH),
                ("ffn2_w_down", 1, DEPTH), ("w_out", 1, DEPTH), ("a_w_in", 1, N_A), ("a_conv", 1, N_A),
                ("b_w_in", 0, N_B), ("b_w_uq", 0, N_B), ("w_dkv", None, None), ("w_ukv", None, None)]
_SCATTER_HI = [("ffn1_w_gu", 2, DEPTH), ("ffn1_w_down", 2, DEPTH), ("ffn2_w_gu", 2, DEPTH), ("ffn2_w_down", 2, DEPTH),
               ("w_out", 2, DEPTH), ("b_w_in", 0, N_B), ("b_w_uq", 0, N_B)]
_SCATTER_MID = [("ffn1_w_gu", 1, 2), ("ffn1_w_down", 1, 2), ("ffn2_w_gu", 1, 2), ("ffn2_w_down", 1, 2),
                ("w_out", 1, 2), ("a_w_in", 1, N_A), ("a_conv", 1, N_A), ("w_dkv", None, None), ("w_ukv", None, None)]
_SCATTER_LO = [("ffn1_w_gu", 0, 1), ("ffn1_w_down", 0, 1), ("ffn2_w_gu", 0, 1), ("ffn2_w_down", 0, 1),
               ("w_out", 0, 1), ("a_w_in", 0, 1), ("a_conv", 0, 1), ("w_mem_kv", 0, DEPTH)]
_REPLICATED = ["ffn1_norm", "mix_norm", "ffn2_norm", "mem_norm", "a_A_log", "a_dt_bias", "a_out_norm", "b_q_norm",
               "kv_in_norm", "kv_lat_norm", "final_norm"]
_WEIGHTS = ["ffn1_norm", "ffn1_w_gu", "ffn1_w_down", "mix_norm", "ffn2_norm", "ffn2_w_gu", "ffn2_w_down", "w_out",
            "mem_norm", "w_mem_kv", "a_w_in", "a_conv", "a_A_log", "a_dt_bias", "a_out_norm", "b_w_in", "b_q_norm",
            "b_w_uq", "kv_in_norm", "w_dkv", "kv_lat_norm", "w_ukv", "final_norm"]


def _full_from_shards(name, sh):
    if name in ("ffn1_w_gu", "ffn2_w_gu"):
        return sh
    if name in ("ffn1_w_down", "ffn2_w_down"):
        return sh.reshape(4, FF_SHARD, D)
    if name in _COL_SHARDED:
        return jnp.moveaxis(sh, 0, -2).reshape(sh.shape[1:-1] + (N_DEV * sh.shape[-1],))
    return sh.reshape((N_DEV * sh.shape[1],) + sh.shape[2:])


def _shards_from_full(name, full):
    if name in ("ffn1_w_gu", "ffn2_w_gu"):
        return full
    if name in ("ffn1_w_down", "ffn2_w_down"):
        return full.reshape(N_DEV, D_FF // N_DEV, D)
    if name in _COL_SHARDED:
        r, cc = full.shape
        return jnp.moveaxis(full.reshape(r, N_DEV, cc // N_DEV), 1, 0)
    return full.reshape((N_DEV, full.shape[0] // N_DEV) + full.shape[1:])


def kernel(x, mem, positions, ffn1_norm, ffn1_w_gu, ffn1_w_down, mix_norm, ffn2_norm, ffn2_w_gu, ffn2_w_down, w_out, mem_norm, w_mem_kv, a_w_in, a_conv, a_A_log, a_dt_bias, a_out_norm, b_w_in, b_q_norm, b_w_uq, kv_in_norm, w_dkv, kv_lat_norm, w_ukv, final_norm, loss_target, m_ffn1_norm, m_ffn1_w_gu, m_ffn1_w_down, m_mix_norm, m_ffn2_norm, m_ffn2_w_gu, m_ffn2_w_down, m_w_out, m_mem_norm, m_w_mem_kv, m_a_w_in, m_a_conv, m_a_A_log, m_a_dt_bias, m_a_out_norm, m_b_w_in, m_b_q_norm, m_b_w_uq, m_kv_in_norm, m_w_dkv, m_kv_lat_norm, m_w_ukv, m_final_norm, v_ffn1_norm, v_ffn1_w_gu, v_ffn1_w_down, v_mix_norm, v_ffn2_norm, v_ffn2_w_gu, v_ffn2_w_down, v_w_out, v_mem_norm, v_w_mem_kv, v_a_w_in, v_a_conv, v_a_A_log, v_a_dt_bias, v_a_out_norm, v_b_w_in, v_b_q_norm, v_b_w_uq, v_kv_in_norm, v_w_dkv, v_kv_lat_norm, v_w_ukv, v_final_norm):
    loc = dict(locals())
    wl = {n: loc[n] for n in _WEIGHTS}
    ml = {n: loc["m_" + n] for n in _WEIGHTS}
    vl = {n: loc["v_" + n] for n in _WEIGHTS}

    me = 4 * lax.axis_index("x") + 2 * lax.axis_index("y") + lax.axis_index("c")
    w = {n: wl[n] for n in _REPLICATED}
    for n in _SHARDED:
        w[n] = [None] * _LAYERED[n] if n in _LAYERED else None
    back = {}

    def src_of(item):
        n, lo, hi = item
        return (wl[n] if lo is None else wl[n][lo:hi]).astype(BF16)

    def install(items, pieces):
        for (n, lo, hi), p in zip(items, pieces):
            if lo is None:
                w[n] = _full_from_shards(n, p)
            else:
                for l in range(lo, hi):
                    w[n][l] = _full_from_shards(n, p[:, l - lo])
        for n, lo, hi in items:
            if n == "a_w_in":
                for l in range(lo, hi):
                    w[n][l], back[(n, l)] = _ext_and_back(_a_in_ext, w[n][l])
            elif n == "a_conv":
                for l in range(lo, hi):
                    w[n][l] = w[n][l].astype(F32)
            elif n == "b_w_uq":
                for l in range(lo, hi):
                    w[n][l], back[(n, l)] = _ext_and_back(_uq_ext, w[n][l])
            elif n == "w_dkv":
                w[n], back[n] = _ext_and_back(_dkv_ext, w[n])
            elif n == "w_ukv":
                w[n], back[n] = _ext_and_back(_ukv_ext, w[n])

    install(_GATHER_FIRST, _exchange([src_of(it) for it in _GATHER_FIRST], True, "gather_first"))
    rest_srcs = [src_of(it) for it in _GATHER_REST]
    flight = _split_start(rest_srcs, True, "gather_rest_start")
    w["ffn1_norm"] = w["ffn1_norm"] + flight[-1][0, 0]

    def layer_start(l, stream):
        if l == 1:
            lands = _split_wait(flight, stream, True, "gather_rest_wait")
            own = [lax.dynamic_update_slice(ld, s[None], (me,) + (0,) * s.ndim) for ld, s in zip(lands, rest_srcs)]
            install(_GATHER_REST, own)

    def grad_src(item, g):
        n, lo, hi = item

        def one(l):
            gl = g[n] if l is None else g[n][l]
            key = n if l is None else (n, l)
            if key in back:
                gl = back[key](gl)
            return _shards_from_full(n, gl).astype(BF16)

        return one(None) if lo is None else jnp.stack([one(l) for l in range(lo, hi)], axis=1)

    sent = {}

    def layer_done(l, g, d):
        for lname, items in ((2, _SCATTER_HI), (1, _SCATTER_MID)):
            if l == lname:
                srcs = [grad_src(it, g) for it in items]
                sent[l] = (srcs, _split_start(srcs, False, f"scatter_{l}_start"))
                return d + sent[l][1][-1][0, 0]
        return d

    loss, dx, g = _local_step(x[0], mem[0], positions[0], loss_target[0], w, layer_start, layer_done)
    loss = lax.psum(loss, ("x", "y", "c"))

    lo_srcs = [grad_src(it, g) for it in _SCATTER_LO]
    pieces = {n: [] for n in _SHARDED}
    for (n, lo, hi), p in zip(_SCATTER_LO, _exchange(lo_srcs, False, "scatter_grads")):
        pieces[n].append((lo, p))
    for l, items in ((1, _SCATTER_MID), (2, _SCATTER_HI)):
        srcs, fl = sent[l]
        lands = _split_wait(fl, dx, False, f"scatter_{l}_wait")
        for (n, lo, hi), s, ld in zip(items, srcs, lands):
            mine = lax.dynamic_slice(s, (me,) + (0,) * (s.ndim - 1), (1,) + s.shape[1:])
            pieces[n].append((lo, lax.dynamic_update_slice(ld, mine, (me,) + (0,) * (s.ndim - 1))))
    out = {}
    for n in _SHARDED:
        ps = [p for _, p in sorted(pieces[n], key=lambda e: -1 if e[0] is None else e[0])]
        p = ps[0] if len(ps) == 1 else jnp.concatenate(ps, axis=1)
        shape = wl[n].shape
        res = _reduce_adamw(p.reshape(N_DEV, -1, shape[-1]), _as2d(wl[n]), _as2d(ml[n]), _as2d(vl[n]), "adamw_" + n)
        for kind, buf in zip(("grad", "delta", "new_m", "new_v"), res):
            out[(kind, n)] = buf.reshape(shape)

    rep_shapes = [wl[n].shape for n in _REPLICATED]
    grep = [jnp.stack(g[n]) if isinstance(g[n], list) else g[n] for n in _REPLICATED]
    (rparts,) = _exchange([_pack(grep)], True, "gather_small_grads")
    res = _reduce_adamw(rparts, _pack([wl[n] for n in _REPLICATED]), _pack([ml[n] for n in _REPLICATED]),
                        _pack([vl[n] for n in _REPLICATED]), "adamw_replicated")
    for kind, buf in zip(("grad", "delta", "new_m", "new_v"), res):
        for n, a in zip(_REPLICATED, _unpack(buf, rep_shapes)):
            out[(kind, n)] = a

    return (loss, dx[None], *[out[("grad", n)] for n in _WEIGHTS], *[out[("delta", n)] for n in _WEIGHTS],
            *[out[("new_m", n)] for n in _WEIGHTS], *[out[("new_v", n)] for n in _WEIGHTS])
```

```python
import functools

import numpy as np
import jax
import jax.numpy as jnp
from jax import lax
from jax.experimental import pallas as pl
from jax.experimental.pallas import tpu as pltpu

F32 = jnp.float32
BF16 = jnp.bfloat16

N_DEV = 8
D = 1024
D_FF = 2816
FF_SHARD = 2 * D_FF // N_DEV
DEPTH = 4
N_A = 2
N_B = 2
EPS = 1e-6
CHUNK = 64
GROUP = 256
GDN_HPS = 3
A_HEADS = 6
HD = 128
A_WIDTH = A_HEADS * HD
B_HEADS = 6
QK_NOPE = 128
QK_ROPE = 64
QK_CAT = 256
Q_LORA = 256
KV_LORA = 256
MEM_HEADS = 4
MEM_HD = 64
MEM_W = 256
N_MEM = 256
ROPE_THETA = 10000.0
ATT_SCALE = (QK_NOPE + QK_ROPE) ** -0.5
LN2 = 0.6931471805599453
Q_PRESCALE = ATT_SCALE / LN2
A_IN = 4 * A_WIDTH + 2 * A_HEADS + MEM_W
A_MQ_BLK = 4 * A_WIDTH // MEM_W
A_BA_BLK = (4 * A_WIDTH + MEM_W) // 128

ADAM_LR = 0.001
ADAM_B1 = 0.9
ADAM_B2 = 0.999
ADAM_EPS = 1e-08
ADAM_WD = 0.01
ADAM_STEP = 10

VMEM_LIMIT = 56 * 1024 * 1024
TM = 512
TMM = 1024
TMF = 2048
TMH = 2048
ATT_BQ = 1024
PACK_W = 1024
PACK_ROWS = 32


def _cparams(sem):
    return pltpu.CompilerParams(dimension_semantics=sem, vmem_limit_bytes=VMEM_LIMIT)


_DIMS = {"nn": ((1,), (0,)), "nt": ((1,), (1,)), "tn": ((0,), (0,))}


def _dot(a, b, dims="nn"):
    return lax.dot_general(a.astype(BF16), b.astype(BF16), (_DIMS[dims], ((), ())),
                           preferred_element_type=F32)


def _matmul(a, b, *, dims, grid, a_spec, b_spec, o_spec, out_shape, name, scale=1.0,
            res=None, res_spec=None, a_fn=None, epilogue=None, acc_shape=None):
    nk = grid[-1]
    kax = len(grid) - 1
    if acc_shape is None:
        acc_shape = tuple(s for s in o_spec.block_shape if s is not None)

    def body(*refs):
        if res is None:
            a_ref, b_ref, o_ref, acc = refs
            r_ref = None
        else:
            a_ref, b_ref, r_ref, o_ref, acc = refs
        k = pl.program_id(kax)

        @pl.when(k == 0)
        def _():
            acc[...] = jnp.zeros_like(acc)

        a_blk = a_ref[...] if a_fn is None else a_fn(a_ref[...])
        acc[...] += _dot(a_blk, b_ref[...], dims)

        @pl.when(k == nk - 1)
        def _():
            y = acc[...] * scale
            if epilogue is not None:
                y = epilogue(y, r_ref[...])
            elif r_ref is not None:
                y = y + r_ref[...].astype(F32)
            o_ref[...] = y.astype(o_ref.dtype)

    args = [a, b] + ([res] if res is not None else [])
    specs = [a_spec, b_spec] + ([res_spec] if res is not None else [])
    sem = ("parallel",) * kax + ("arbitrary",)
    return pl.pallas_call(
        body, out_shape=out_shape, grid=grid, in_specs=specs, out_specs=o_spec,
        scratch_shapes=[pltpu.VMEM(acc_shape, F32)], name=name, compiler_params=_cparams(sem))(*args)


def _tile(n, cap):
    if n <= cap:
        return n
    t = cap - cap % 128
    while t >= 128:
        if n % t == 0:
            return t
        t -= 128
    raise ValueError(f"no tile for {n}")


def _mm(a, b, dims, out_dtype, name, scale=1.0, res=None):
    if dims == "tn":
        kk, m = a.shape
        n = b.shape[1]
        tk, tn = _tile(kk, TMM), _tile(n, 1152)
        return _matmul(a, b, dims=dims, grid=(1, n // tn, kk // tk),
                       a_spec=pl.BlockSpec((tk, m), lambda i, j, k: (k, 0)),
                       b_spec=pl.BlockSpec((tk, tn), lambda i, j, k: (k, j)),
                       o_spec=pl.BlockSpec((m, tn), lambda i, j, k: (0, j)),
                       out_shape=jax.ShapeDtypeStruct((m, n), out_dtype), name=name, scale=scale)
    m, kk = a.shape
    n = b.shape[1] if dims == "nn" else b.shape[0]
    tm, tn, tk = _tile(m, TMM), _tile(n, 1152), _tile(kk, 1536)
    if dims == "nn":
        b_spec = pl.BlockSpec((tk, tn), lambda i, j, k: (k, j))
    else:
        b_spec = pl.BlockSpec((tn, tk), lambda i, j, k: (j, k))
    o_spec = pl.BlockSpec((tm, tn), lambda i, j, k: (i, j))
    return _matmul(a, b, dims=dims, grid=(m // tm, n // tn, kk // tk),
                   a_spec=pl.BlockSpec((tm, tk), lambda i, j, k: (i, k)), b_spec=b_spec, o_spec=o_spec,
                   out_shape=jax.ShapeDtypeStruct((m, n), out_dtype), name=name, scale=scale,
                   res=res, res_spec=o_spec if res is not None else None)


def _rowcall(fn, args, in_specs, out_shapes, out_specs, grid, name, n_acc=0):
    n_in, n_out = len(args), len(out_shapes)

    def body(*refs):
        outs = fn(*[r[...] for r in refs[:n_in]])
        if not isinstance(outs, (tuple, list)):
            outs = (outs,)
        first = pl.program_id(0) == 0
        for ax in range(1, len(grid)):
            first = jnp.logical_and(first, pl.program_id(ax) == 0)
        for idx, (o_ref, val) in enumerate(zip(refs[n_in:], outs)):
            if idx >= n_out - n_acc:
                @pl.when(first)
                def _(o_ref=o_ref):
                    o_ref[...] = jnp.zeros_like(o_ref)

                o_ref[...] += val.astype(o_ref.dtype)
            else:
                o_ref[...] = val.astype(o_ref.dtype)

    sem = (("arbitrary",) if n_acc else ("parallel",)) * len(grid)
    res = pl.pallas_call(body, out_shape=tuple(out_shapes), grid=grid, in_specs=list(in_specs),
                         out_specs=tuple(out_specs), name=name, compiler_params=_cparams(sem))(*args)
    return res


def _vjp_fn(fn, n_in, wrt):
    def bwd(*blocks):
        ins = [b.astype(F32) for b in blocks[:n_in]]
        cts = [c.astype(F32) for c in blocks[n_in:]]
        outs, vjp = jax.vjp(fn, *ins)
        if isinstance(outs, (tuple, list)):
            grads = vjp(tuple(cts))
        else:
            grads = vjp(cts[0])
        return tuple(grads[i] for i in wrt)
    return bwd


def _sds(shape, dtype):
    return jax.ShapeDtypeStruct(tuple(shape), dtype)


def _rows(tm, w, col=0):
    return pl.BlockSpec((tm, w), lambda i, *_: (i, col))


def _shared(shape):
    nd = len(shape)
    return pl.BlockSpec(tuple(shape), lambda *_: (0,) * nd)


def _rms(x, g):
    return x * lax.rsqrt(jnp.mean(x * x, axis=-1, keepdims=True) + EPS) * g


def _silu(x):
    return x * jax.nn.sigmoid(x)


def _swiglu_pair(gu):
    return _silu(gu[0].astype(F32)) * gu[1].astype(F32)


def _swiglu_bwd(dh, gu):
    g, u = gu[0].astype(F32), gu[1].astype(F32)
    sg = jax.nn.sigmoid(g)
    return jnp.stack([dh * u * sg * (1.0 + g * (1.0 - sg)), dh * g * sg])


def _gdn_prep(q, k, v):
    q, k, v = _silu(q), _silu(k), _silu(v)
    q = q * lax.rsqrt(jnp.sum(q * q, axis=-1, keepdims=True) + EPS) * (HD ** -0.5)
    k = k * lax.rsqrt(jnp.sum(k * k, axis=-1, keepdims=True) + EPS)
    return q, k, v


def _gates(ba, a_log, dt_bias):
    lane = lax.broadcasted_iota(jnp.int32, ba.shape, 1)
    beta = jax.nn.sigmoid(ba)
    z = ba + dt_bias
    softplus = jnp.maximum(z, 0.0) + jnp.log(1.0 + jnp.exp(-jnp.abs(z)))
    g = -jnp.exp(a_log) * softplus
    return jnp.where(lane < A_HEADS, beta, jnp.where(lane < 2 * A_HEADS, g, 0.0))


def _outnorm_gate(o, gate, gain):
    return _rms(o, gain) * _silu(gate)


def _memattn(q, k, v):
    lane = lax.shift_right_logical(lax.broadcasted_iota(jnp.int32, (1, MEM_W), 1), 6)
    out = jnp.zeros(q.shape, F32)
    for h in range(MEM_HEADS):
        mh = (lane == h).astype(F32)
        s = _dot(q * mh, k, "nt") * (MEM_HD ** -0.5)
        s = s - lax.stop_gradient(jnp.max(s, axis=-1, keepdims=True))
        p = jnp.exp(s)
        p = p / jnp.sum(p, axis=-1, keepdims=True)
        out = out + _dot(p, v * mh)
    return out


def _rope_mix(a, a_sw, c, s):
    return (a * c + a_sw * s) * Q_PRESCALE


def _kcat(kn, kr, kr_sw, c, s):
    return kn + kr * c + kr_sw * s


@jax.custom_vjp
def _neumann_inverse(nmat):
    n = nmat.shape[0]
    eye = (lax.broadcasted_iota(jnp.int32, (n, n), 0) == lax.broadcasted_iota(jnp.int32, (n, n), 1)).astype(F32)
    pinv = eye + nmat
    npow = nmat
    for _ in range(5):
        npow = _dot(npow, npow)
        pinv = pinv + _dot(pinv, npow)
    return pinv


def _neumann_inverse_fwd(nmat):
    pinv = _neumann_inverse(nmat)
    return pinv, pinv


def _neumann_inverse_bwd(pinv, ct):
    return (_dot(_dot(pinv, ct, "tn"), pinv, "nt"),)


_neumann_inverse.defvjp(_neumann_inverse_fwd, _neumann_inverse_bwd)


@jax.custom_vjp
def _known_inverse(nmat, pinv):
    return pinv


def _known_inverse_fwd(nmat, pinv):
    return pinv, pinv


def _known_inverse_bwd(pinv, ct):
    return _dot(_dot(pinv, ct, "tn"), pinv, "nt"), jnp.zeros_like(pinv)


_known_inverse.defvjp(_known_inverse_fwd, _known_inverse_bwd)


def _gdn_local(q, k, v, beta, gcol, grow, pinv_kept=None):
    n = GROUP
    ri = lax.broadcasted_iota(jnp.int32, (n, n), 0)
    ci = lax.broadcasted_iota(jnp.int32, (n, n), 1)
    same = lax.shift_right_logical(ri, 6) == lax.shift_right_logical(ci, 6)
    lower = jnp.logical_and(same, ci <= ri)
    strict = jnp.logical_and(same, ci < ri)
    gc_col = jnp.sum(lower.astype(F32) * grow, axis=1, keepdims=True)
    gc_row = jnp.sum(jnp.logical_and(same, ri <= ci).astype(F32) * gcol, axis=0, keepdims=True)
    glast = jnp.sum(same.astype(F32) * grow, axis=1, keepdims=True)
    decay = jnp.where(lower, jnp.exp(jnp.where(lower, gc_col - gc_row, 0.0)), 0.0)
    kb = k * beta
    nmat = -jnp.where(strict, _dot(kb, k, "nt") * decay, 0.0)
    pinv = _neumann_inverse(nmat) if pinv_kept is None else _known_inverse(nmat, pinv_kept)
    e_gc = jnp.exp(gc_col)
    u = _dot(pinv, v * beta)
    w = _dot(pinv, kb * e_gc)
    qk = _dot(q, k, "nt") * decay
    fold = (jnp.bitwise_and(lax.broadcasted_iota(jnp.int32, (n, CHUNK), 0), CHUNK - 1)
            == lax.broadcasted_iota(jnp.int32, (n, CHUNK), 1)).astype(F32)
    qk_c = _dot(qk, fold)
    q_dec = q * e_gc
    k_dec = k * jnp.exp(glast - gc_col)
    dmat = jnp.exp(glast) * jnp.ones((1, HD), F32)
    if pinv_kept is None:
        return u, w, q_dec, k_dec, qk_c, dmat, pinv
    return u, w, q_dec, k_dec, qk_c, dmat


def _gdn_step(s, w_c, u_c, qd_c, kd_c, qk_c, d_c):
    v_new = u_c - _dot(w_c, s)
    out = _dot(qd_c, s) + _dot(qk_c, v_new)
    d_row = jnp.mean(d_c, axis=0, keepdims=True)
    s_new = s * d_row + _dot(kd_c, v_new, "tn")
    return s_new, out


def _rmsnorm_fwd(x, gain, name, col=0, width=None):
    t = x.shape[0]
    w = width or x.shape[1]
    (n,) = _rowcall(_rms, [x, gain.reshape(1, w)], [_rows(TM, w, col), _shared((1, w))],
                    [_sds((t, w), BF16)], [_rows(TM, w)], (t // TM,), name)
    return n


def _rmsnorm_bwd(x, gain, dn, dres, name):
    t, w = x.shape
    fn = _vjp_fn(_rms, 2, (0, 1))

    def bwd(xb, gb, dnb, drb):
        dx, dg = fn(xb, gb, dnb)
        return dx + drb, dg

    dx, dg = _rowcall(bwd, [x, gain.reshape(1, w), dn, dres],
                      [_rows(TM, w), _shared((1, w)), _rows(TM, w), _rows(TM, w)],
                      [_sds((t, w), F32), _sds((1, w), F32)], [_rows(TM, w), _shared((1, w))],
                      (t // TM,), name, n_acc=1)
    return dx, dg[0]


def _ffn_fwd(x, gain, wgu8, wd4, tag):
    t = x.shape[0]
    nt = t // TMM
    tf = min(TMF, t)
    n = _rmsnorm_fwd(x, gain, tag + "_norm")
    gu = _matmul(n, wgu8, dims="nn", grid=(N_DEV, t // tf, 1),
                 a_spec=pl.BlockSpec((tf, D), lambda j, i, k: (i, 0)),
                 b_spec=pl.BlockSpec((None, D, FF_SHARD), lambda j, i, k: (j, 0, 0)),
                 o_spec=pl.BlockSpec((None, tf, FF_SHARD), lambda j, i, k: (j, i, 0)),
                 out_shape=_sds((N_DEV, t, FF_SHARD), BF16), name=tag + "_gu")
    gu = gu.reshape(2, 4, t, FF_SHARD)
    y = _matmul(gu, wd4, dims="nn", grid=(nt, 1, 4), a_fn=_swiglu_pair,
                a_spec=pl.BlockSpec((2, None, TMM, FF_SHARD), lambda i, j, k: (0, k, i, 0)),
                b_spec=pl.BlockSpec((None, FF_SHARD, D), lambda i, j, k: (k, 0, 0)),
                o_spec=pl.BlockSpec((TMM, D), lambda i, j, k: (i, 0)),
                out_shape=_sds((t, D), F32), name=tag + "_down", scale=0.5,
                res=x, res_spec=pl.BlockSpec((TMM, D), lambda i, j, k: (i, 0)))
    return y, (x, n, gu)


def _ffn_bwd(d, saved, gain, wgu8, wd4, tag):
    x, n, gu = saved
    t = x.shape[0]
    nt = t // TMM
    dgu = _matmul(d, wd4, dims="nt", grid=(4, nt, 1),
                  a_spec=pl.BlockSpec((TMM, D), lambda j, i, k: (i, 0)),
                  b_spec=pl.BlockSpec((None, FF_SHARD, D), lambda j, i, k: (j, 0, 0)),
                  o_spec=pl.BlockSpec((2, None, TMM, FF_SHARD), lambda j, i, k: (0, j, i, 0)),
                  out_shape=_sds((2, 4, t, FF_SHARD), BF16), name=tag + "_dgu", scale=0.5,
                  res=gu, res_spec=pl.BlockSpec((2, None, TMM, FF_SHARD), lambda j, i, k: (0, j, i, 0)),
                  epilogue=_swiglu_bwd, acc_shape=(TMM, FF_SHARD))
    tf = min(TMF, t)
    nf = t // tf
    dwd4 = _matmul(gu, d, dims="tn", grid=(4, 1, nf), a_fn=_swiglu_pair,
                   a_spec=pl.BlockSpec((2, None, tf, FF_SHARD), lambda j, i, k: (0, j, k, 0)),
                   b_spec=pl.BlockSpec((tf, D), lambda j, i, k: (k, 0)),
                   o_spec=pl.BlockSpec((None, FF_SHARD, D), lambda j, i, k: (j, 0, 0)),
                   out_shape=_sds((4, FF_SHARD, D), F32), name=tag + "_dwd", scale=0.5)
    dgu = dgu.reshape(N_DEV, t, FF_SHARD)
    dwgu8 = _matmul(n, dgu, dims="tn", grid=(N_DEV, 1, nf),
                    a_spec=pl.BlockSpec((tf, D), lambda j, i, k: (k, 0)),
                    b_spec=pl.BlockSpec((None, tf, FF_SHARD), lambda j, i, k: (j, k, 0)),
                    o_spec=pl.BlockSpec((None, D, FF_SHARD), lambda j, i, k: (j, 0, 0)),
                    out_shape=_sds((N_DEV, D, FF_SHARD), F32), name=tag + "_dwgu")
    dn = _matmul(dgu, wgu8, dims="nt", grid=(nf, 1, N_DEV),
                 a_spec=pl.BlockSpec((None, tf, FF_SHARD), lambda i, j, k: (k, i, 0)),
                 b_spec=pl.BlockSpec((None, D, FF_SHARD), lambda i, j, k: (k, 0, 0)),
                 o_spec=pl.BlockSpec((tf, D), lambda i, j, k: (i, 0)),
                 out_shape=_sds((t, D), BF16), name=tag + "_dn")
    dx, dgain = _rmsnorm_bwd(x, gain, dn, d, tag + "_dnorm")
    return dx, dgain, dwgu8, dwd4


CONV_TC = 768
CONV_K = 4
CONV_TM = 1024


def _conv_fwd(ha, w, name):
    t = ha.shape[0]
    nb = CONV_TM // 8

    def body(prev_ref, cur_ref, w_ref, o_ref):
        i = pl.program_id(0)
        cur = cur_ref[...].astype(F32)
        prev = prev_ref[...].astype(F32) * (i > 0).astype(F32)
        ext = jnp.concatenate([prev, cur], axis=0)
        wv = w_ref[...]
        acc = cur * wv[3:4]
        for k in range(1, CONV_K):
            acc = acc + pltpu.roll(ext, k, axis=0)[8:] * wv[3 - k:4 - k]
        o_ref[...] = acc.astype(o_ref.dtype)

    return pl.pallas_call(
        body, out_shape=_sds((3, t, CONV_TC), BF16), grid=(t // CONV_TM, 3),
        in_specs=[pl.BlockSpec((8, CONV_TC), lambda i, c: (jnp.maximum(i * nb - 1, 0), c)),
                  pl.BlockSpec((CONV_TM, CONV_TC), lambda i, c: (i, c)),
                  pl.BlockSpec((CONV_K, CONV_TC), lambda i, c: (0, c))],
        out_specs=pl.BlockSpec((None, CONV_TM, CONV_TC), lambda i, c: (c, i, 0)),
        name=name, compiler_params=_cparams(("parallel", "parallel")))(ha, ha, w)


def _conv_bwd(ha, w, dy3, name):
    t = ha.shape[0]
    nb = CONV_TM // 8
    nt = t // CONV_TM

    def body(prev_ref, cur_ref, dy_ref, nxt_ref, w_ref, dx_ref, dw_ref):
        i = pl.program_id(1)
        cur = cur_ref[...].astype(F32)
        prev = prev_ref[...].astype(F32) * (i > 0).astype(F32)
        ext = jnp.concatenate([prev, cur], axis=0)
        dy = dy_ref[...].astype(F32)
        nxt = nxt_ref[...].astype(F32) * (i < nt - 1).astype(F32)
        dext = jnp.concatenate([dy, nxt], axis=0)
        wv = w_ref[...]
        dx = dy * wv[3:4]
        dws = [None] * CONV_K
        dws[3] = jnp.sum(dy * cur, axis=0, keepdims=True)
        for k in range(1, CONV_K):
            dx = dx + pltpu.roll(dext, CONV_TM + 8 - k, axis=0)[:CONV_TM] * wv[3 - k:4 - k]
            dws[3 - k] = jnp.sum(dy * pltpu.roll(ext, k, axis=0)[8:], axis=0, keepdims=True)
        dx_ref[...] = dx.astype(dx_ref.dtype)

        @pl.when(i == 0)
        def _():
            dw_ref[...] = jnp.zeros_like(dw_ref)

        dw_ref[...] += jnp.concatenate(dws, axis=0)

    return pl.pallas_call(
        body, out_shape=(_sds((t, 3 * CONV_TC), BF16), _sds((CONV_K, 3 * CONV_TC), F32)), grid=(3, nt),
        in_specs=[pl.BlockSpec((8, CONV_TC), lambda c, i: (jnp.maximum(i * nb - 1, 0), c)),
                  pl.BlockSpec((CONV_TM, CONV_TC), lambda c, i: (i, c)),
                  pl.BlockSpec((None, CONV_TM, CONV_TC), lambda c, i: (c, i, 0)),
                  pl.BlockSpec((None, 8, CONV_TC), lambda c, i: (c, jnp.minimum((i + 1) * nb, t // 8 - 1), 0)),
                  pl.BlockSpec((CONV_K, CONV_TC), lambda c, i: (0, c))],
        out_specs=(pl.BlockSpec((CONV_TM, CONV_TC), lambda c, i: (i, c)),
                   pl.BlockSpec((CONV_K, CONV_TC), lambda c, i: (0, c))),
        name=name, compiler_params=_cparams(("parallel", "arbitrary")))(ha, ha, dy3, dy3, w)


def _gdn_specs(t, rev):
    ng = t // GROUP

    def gi(g):
        return ng - 1 - g if rev else g

    qkv = pl.BlockSpec((3, GROUP, GDN_HPS * HD), lambda h, g: (0, gi(g), h))
    bg = pl.BlockSpec((GROUP, 128), lambda h, g: (gi(g), 0))
    dbg = pl.BlockSpec((GDN_HPS, GROUP, 128), lambda h, g: (h, gi(g), 0))
    o = pl.BlockSpec((GROUP, GDN_HPS * HD), lambda h, g: (gi(g), h))
    st = pl.BlockSpec((GDN_HPS, None, HD, HD), lambda h, g: (h, gi(g), 0, 0))
    inv = pl.BlockSpec((GDN_HPS, None, GROUP, GROUP), lambda h, g: (h, gi(g), 0, 0))
    return qkv, bg, dbg, o, st, inv


def _head_qkv(qkv_ref, j):
    sl = slice(j * HD, (j + 1) * HD)
    return qkv_ref[0, :, sl].astype(F32), qkv_ref[1, :, sl].astype(F32), qkv_ref[2, :, sl].astype(F32)


def _head_gates(bg, h):
    lane = lax.broadcasted_iota(jnp.int32, (1, 128), 1)
    beta = jnp.sum(jnp.where(lane == h, bg, 0.0), axis=1, keepdims=True)
    gcol = jnp.sum(jnp.where(lane == h + A_HEADS, bg, 0.0), axis=1, keepdims=True)
    return beta, gcol, _col_to_row(gcol)


def _gdn_fwd(qkv3, bg, name):
    t = qkv3.shape[1]
    ng = t // GROUP
    qkv_s, bg_s, _, o_s, st_s, inv_s = _gdn_specs(t, False)

    def body(qkv_ref, bg_ref, o_ref, st_ref, inv_ref, s_scr):
        @pl.when(pl.program_id(1) == 0)
        def _():
            s_scr[...] = jnp.zeros_like(s_scr)

        st_ref[...] = s_scr[...]
        bgv = bg_ref[...]
        loc = [_gdn_local(*_head_qkv(qkv_ref, j), *_head_gates(bgv, pl.program_id(0) * GDN_HPS + j))
               for j in range(GDN_HPS)]
        s = [s_scr[j] for j in range(GDN_HPS)]
        for a in range(GROUP // CHUNK):
            sl = slice(a * CHUNK, (a + 1) * CHUNK)
            for j in range(GDN_HPS):
                u, w, qd, kd, qkc, dm, _ = loc[j]
                s[j], out = _gdn_step(s[j], w[sl], u[sl], qd[sl], kd[sl], qkc[sl], dm[sl])
                o_ref[sl, j * HD:(j + 1) * HD] = out.astype(o_ref.dtype)
        for j in range(GDN_HPS):
            s_scr[j] = s[j]
            inv_ref[j] = loc[j][6].astype(inv_ref.dtype)

    return pl.pallas_call(
        body, out_shape=(_sds((t, A_WIDTH), BF16), _sds((A_HEADS, ng, HD, HD), F32),
                         _sds((A_HEADS, ng, GROUP, GROUP), BF16)),
        grid=(A_HEADS // GDN_HPS, ng), in_specs=[qkv_s, bg_s], out_specs=(o_s, st_s, inv_s),
        scratch_shapes=[pltpu.VMEM((GDN_HPS, HD, HD), F32)], name=name,
        compiler_params=_cparams(("parallel", "arbitrary")))(qkv3, bg)


def _gdn_bwd(qkv3, bg, states, pinvs, do, name):
    t = qkv3.shape[1]
    ng = t // GROUP
    qkv_s, bg_s, dbg_s, o_s, st_s, inv_s = _gdn_specs(t, True)
    nc = GROUP // CHUNK

    def body(qkv_ref, bg_ref, st_ref, inv_ref, do_ref, dqkv_ref, dbg_ref, ds_scr):
        @pl.when(pl.program_id(1) == 0)
        def _():
            ds_scr[...] = jnp.zeros_like(ds_scr)

        heads = range(GDN_HPS)
        bgv = bg_ref[...]
        hid = [pl.program_id(0) * GDN_HPS + j for j in heads]
        fw = [jax.vjp(functools.partial(_gdn_local, pinv_kept=inv_ref[j].astype(F32)),
                      *_head_qkv(qkv_ref, j), *_head_gates(bgv, hid[j])) for j in heads]
        starts = [[None] * nc for _ in heads]
        s = [st_ref[j] for j in heads]
        for a in range(nc):
            sl = slice(a * CHUNK, (a + 1) * CHUNK)
            for j in heads:
                u, w, qd, kd, qkc, dm = fw[j][0]
                starts[j][a] = s[j]
                if a < nc - 1:
                    s[j], _ = _gdn_step(s[j], w[sl], u[sl], qd[sl], kd[sl], qkc[sl], dm[sl])
        ds = [ds_scr[j] for j in heads]
        parts = [[None] * nc for _ in heads]
        for a in reversed(range(nc)):
            sl = slice(a * CHUNK, (a + 1) * CHUNK)
            for j in heads:
                u, w, qd, kd, qkc, dm = fw[j][0]
                _, vjp_step = jax.vjp(_gdn_step, starts[j][a], w[sl], u[sl], qd[sl], kd[sl], qkc[sl], dm[sl])
                grads = vjp_step((ds[j], do_ref[sl, j * HD:(j + 1) * HD].astype(F32)))
                ds[j] = grads[0]
                parts[j][a] = grads[1:]
        lane = lax.broadcasted_iota(jnp.int32, (1, 128), 1)
        for j in heads:
            ds_scr[j] = ds[j]
            dw, du, dqd, dkd, dqk, ddm = [jnp.concatenate([parts[j][a][i] for a in range(nc)], axis=0)
                                          for i in range(6)]
            dq, dk, dv, db, dgc, dgr = fw[j][1]((du, dw, dqd, dkd, dqk, ddm))
            hs = slice(j * HD, (j + 1) * HD)
            dqkv_ref[0, :, hs] = dq.astype(dqkv_ref.dtype)
            dqkv_ref[1, :, hs] = dk.astype(dqkv_ref.dtype)
            dqkv_ref[2, :, hs] = dv.astype(dqkv_ref.dtype)
            dbg_ref[j] = (jnp.where(lane == hid[j], db, 0.0)
                          + jnp.where(lane == hid[j] + A_HEADS, dgc + _row_to_col(dgr), 0.0))

    return pl.pallas_call(
        body, out_shape=(_sds((3, t, A_WIDTH), BF16), _sds((A_HEADS, t, 128), F32)),
        grid=(A_HEADS // GDN_HPS, ng), in_specs=[qkv_s, bg_s, st_s, inv_s, o_s],
        out_specs=(qkv_s, dbg_s), scratch_shapes=[pltpu.VMEM((GDN_HPS, HD, HD), F32)], name=name,
        compiler_params=_cparams(("parallel", "arbitrary")))(qkv3, bg, states, pinvs, do)


NEG = -1e30


def _diag_mask(shape, q_axis):
    qi = lax.shift_right_logical(lax.broadcasted_iota(jnp.int32, shape, q_axis), 6)
    ki = lax.shift_right_logical(lax.broadcasted_iota(jnp.int32, shape, 1 - q_axis), 6)
    return ki <= qi


def _col_to_row(col):
    n = col.shape[0]
    eye = lax.broadcasted_iota(jnp.int32, (n, n), 0) == lax.broadcasted_iota(jnp.int32, (n, n), 1)
    return jnp.sum(jnp.where(eye, col, 0.0), axis=0, keepdims=True)


def _row_to_col(row):
    n = row.shape[1]
    eye = lax.broadcasted_iota(jnp.int32, (n, n), 0) == lax.broadcasted_iota(jnp.int32, (n, n), 1)
    return jnp.sum(jnp.where(eye, row, 0.0), axis=1, keepdims=True)


def _blk(ref, i):
    return ref[pl.ds(pl.multiple_of(i * ATT_BQ, ATT_BQ), ATT_BQ), :]


def _att_fwd(qc, kc, v, name):
    t = qc.shape[0]
    nq = t // ATT_BQ

    def body(q_ref, k_ref, v_ref, o_ref, lse_ref, lser_ref, m_scr, l_scr, acc_scr):
        qb = pl.program_id(1)
        q = q_ref[...]
        m_scr[...] = jnp.full_like(m_scr, NEG)
        l_scr[...] = jnp.zeros_like(l_scr)
        acc_scr[...] = jnp.zeros_like(acc_scr)

        def step(kb, diag):
            s = _dot(q, _blk(k_ref, kb), "nt")
            if diag:
                s = jnp.where(_diag_mask(s.shape, 0), s, NEG)
            m_old = m_scr[...]
            m_new = jnp.maximum(m_old, jnp.max(s, axis=1, keepdims=True))
            alpha = jnp.exp2(m_old - m_new)
            p = jnp.exp2(s - m_new)
            l_scr[...] = alpha * l_scr[...] + jnp.sum(p, axis=1, keepdims=True)
            acc_scr[...] = alpha * acc_scr[...] + _dot(p, _blk(v_ref, kb))
            m_scr[...] = m_new

        def loop_body(kb, carry):
            step(kb, False)
            return carry

        lax.fori_loop(0, qb, loop_body, 0)
        step(qb, True)
        o_ref[...] = (acc_scr[...] / l_scr[...]).astype(o_ref.dtype)
        lse = m_scr[...] + jnp.log2(l_scr[...])
        lse_ref[...] = lse
        lser_ref[...] = _col_to_row(lse)

    return pl.pallas_call(
        body, out_shape=(_sds((t, B_HEADS * HD), BF16), _sds((B_HEADS, t, 1), F32),
                         _sds((B_HEADS, nq, 1, ATT_BQ), F32)), grid=(B_HEADS, nq),
        in_specs=[pl.BlockSpec((ATT_BQ, QK_CAT), lambda h, i: (i, h)),
                  pl.BlockSpec((t, QK_CAT), lambda h, i: (0, h)), pl.BlockSpec((t, HD), lambda h, i: (0, h))],
        out_specs=(pl.BlockSpec((ATT_BQ, HD), lambda h, i: (i, h)),
                   pl.BlockSpec((None, ATT_BQ, 1), lambda h, i: (h, i, 0)),
                   pl.BlockSpec((None, None, 1, ATT_BQ), lambda h, i: (h, i, 0, 0))),
        scratch_shapes=[pltpu.VMEM((ATT_BQ, 1), F32), pltpu.VMEM((ATT_BQ, 1), F32), pltpu.VMEM((ATT_BQ, HD), F32)],
        name=name, compiler_params=_cparams(("parallel", "arbitrary")))(qc, kc, v)


def _att_bwd(qc, kc, v, o, lse, lse_row, do, name):
    t = qc.shape[0]
    nq = t // ATT_BQ

    def delta_fn(ob, dob):
        dl = jnp.sum(ob.astype(F32) * dob.astype(F32), axis=1, keepdims=True)
        return dl, _col_to_row(dl)

    delta, delta_row = _rowcall(
        delta_fn, [o, do], [pl.BlockSpec((ATT_BQ, HD), lambda i, h: (i, h))] * 2,
        [_sds((B_HEADS, t, 1), F32), _sds((B_HEADS, nq, 1, ATT_BQ), F32)],
        [pl.BlockSpec((None, ATT_BQ, 1), lambda i, h: (h, i, 0)),
         pl.BlockSpec((None, None, 1, ATT_BQ), lambda i, h: (h, i, 0, 0))], (nq, B_HEADS), name + "_delta")

    def dq_body(q_ref, k_ref, v_ref, do_ref, lse_ref, dl_ref, dq_ref, acc):
        qb = pl.program_id(1)
        q, dob, lse_b, dl_b = q_ref[...], do_ref[...], lse_ref[...], dl_ref[...]
        acc[...] = jnp.zeros_like(acc)

        def step(kb, diag):
            k = _blk(k_ref, kb)
            s = _dot(q, k, "nt")
            if diag:
                s = jnp.where(_diag_mask(s.shape, 0), s, NEG)
            p = jnp.exp2(s - lse_b)
            ds = p * (_dot(dob, _blk(v_ref, kb), "nt") - dl_b)
            acc[...] += _dot(ds, k)

        def loop_body(kb, carry):
            step(kb, False)
            return carry

        lax.fori_loop(0, qb, loop_body, 0)
        step(qb, True)
        dq_ref[...] = (acc[...] * ATT_SCALE).astype(dq_ref.dtype)

    qmap = lambda h, i: (i, h)
    colq = pl.BlockSpec((None, ATT_BQ, 1), lambda h, i: (h, i, 0))
    dq = pl.pallas_call(
        dq_body, out_shape=_sds((t, B_HEADS * QK_CAT), BF16), grid=(B_HEADS, nq),
        in_specs=[pl.BlockSpec((ATT_BQ, QK_CAT), qmap), pl.BlockSpec((t, QK_CAT), lambda h, i: (0, h)),
                  pl.BlockSpec((t, HD), lambda h, i: (0, h)), pl.BlockSpec((ATT_BQ, HD), qmap), colq, colq],
        out_specs=pl.BlockSpec((ATT_BQ, QK_CAT), qmap),
        scratch_shapes=[pltpu.VMEM((ATT_BQ, QK_CAT), F32)], name=name + "_dq",
        compiler_params=_cparams(("parallel", "arbitrary")))(qc, kc, v, do, lse, delta)

    def dkv_body(k_ref, v_ref, q_ref, do_ref, lser_ref, dlr_ref, dk_ref, dv_ref, dk_acc, dv_acc):
        kb = pl.program_id(1)
        k, vv = k_ref[...], v_ref[...]
        dk_acc[...] = jnp.zeros_like(dk_acc)
        dv_acc[...] = jnp.zeros_like(dv_acc)

        def step(qb, diag):
            q, dob = _blk(q_ref, qb), _blk(do_ref, qb)
            st = _dot(k, q, "nt")
            if diag:
                st = jnp.where(_diag_mask(st.shape, 1), st, NEG)
            pt = jnp.exp2(st - lser_ref[qb])
            dst = pt * (_dot(vv, dob, "nt") - dlr_ref[qb])
            dv_acc[...] += _dot(pt, dob)
            dk_acc[...] += _dot(dst, q)

        def loop_body(qb, carry):
            step(qb, False)
            return carry

        step(kb, True)
        lax.fori_loop(kb + 1, nq, loop_body, 0)
        dk_ref[...] = (dk_acc[...] * LN2).astype(dk_ref.dtype)
        dv_ref[...] = dv_acc[...].astype(dv_ref.dtype)

    kmap = lambda h, j: (j, h)
    rowq = pl.BlockSpec((None, nq, 1, ATT_BQ), lambda h, j: (h, 0, 0, 0))
    dk, dv = pl.pallas_call(
        dkv_body, out_shape=(_sds((t, B_HEADS * QK_CAT), BF16), _sds((t, B_HEADS * HD), BF16)),
        grid=(B_HEADS, nq),
        in_specs=[pl.BlockSpec((ATT_BQ, QK_CAT), kmap), pl.BlockSpec((ATT_BQ, HD), kmap),
                  pl.BlockSpec((t, QK_CAT), lambda h, j: (0, h)), pl.BlockSpec((t, HD), lambda h, j: (0, h)),
                  rowq, rowq],
        out_specs=(pl.BlockSpec((ATT_BQ, QK_CAT), kmap), pl.BlockSpec((ATT_BQ, HD), kmap)),
        scratch_shapes=[pltpu.VMEM((ATT_BQ, QK_CAT), F32), pltpu.VMEM((ATT_BQ, HD), F32)], name=name + "_dkv",
        compiler_params=_cparams(("parallel", "arbitrary")))(kc, v, qc, do, lse_row, delta_row)
    return dq, dk, dv


def _mem_fwd(hx, col, mkv, name):
    t = hx.shape[0]
    (o,) = _rowcall(_memattn, [hx, mkv, mkv],
                    [_rows(TM, MEM_W, col), pl.BlockSpec((N_MEM, MEM_W), lambda i: (0, 0)),
                     pl.BlockSpec((N_MEM, MEM_W), lambda i: (0, 1))],
                    [_sds((t, MEM_W), BF16)], [_rows(TM, MEM_W)], (t // TM,), name)
    return o


def _mem_bwd(hx, col, mkv, do, do_col, name):
    t = hx.shape[0]
    dq, dk, dv = _rowcall(_vjp_fn(_memattn, 3, (0, 1, 2)), [hx, mkv, mkv, do],
                          [_rows(TM, MEM_W, col), pl.BlockSpec((N_MEM, MEM_W), lambda i: (0, 0)),
                           pl.BlockSpec((N_MEM, MEM_W), lambda i: (0, 1)), _rows(TM, MEM_W, do_col)],
                          [_sds((t, MEM_W), BF16), _sds((N_MEM, MEM_W), F32), _sds((N_MEM, MEM_W), F32)],
                          [_rows(TM, MEM_W), _shared((N_MEM, MEM_W)), _shared((N_MEM, MEM_W))],
                          (t // TM,), name, n_acc=2)
    return dq, jnp.concatenate([dk, dv], axis=1)


def _a_in_ext(w):
    nb = 4 * A_WIDTH
    ba = jnp.pad(w[:, nb:nb + 2 * A_HEADS], ((0, 0), (0, 128 - 2 * A_HEADS)))
    return jnp.concatenate([w[:, :nb], w[:, nb + 2 * A_HEADS:], ba], axis=1)


def _swap_halves(w):
    return jnp.concatenate([w[..., QK_ROPE // 2:], w[..., :QK_ROPE // 2]], axis=-1)


def _uq_ext(w):
    w = w.reshape(Q_LORA, B_HEADS, QK_NOPE + QK_ROPE)
    nope, rope = w[..., :QK_NOPE], w[..., QK_NOPE:]
    z64 = jnp.zeros((Q_LORA, B_HEADS, QK_CAT - QK_NOPE - QK_ROPE), w.dtype)
    z128 = jnp.zeros((Q_LORA, B_HEADS, QK_NOPE), w.dtype)
    a = jnp.concatenate([nope, rope, z64], axis=-1).reshape(Q_LORA, B_HEADS * QK_CAT)
    b = jnp.concatenate([z128, _swap_halves(rope), z64], axis=-1).reshape(Q_LORA, B_HEADS * QK_CAT)
    return jnp.concatenate([a, b], axis=1)


def _dkv_ext(w):
    ckv, kr = w[:, :KV_LORA], w[:, KV_LORA:]
    z128 = jnp.zeros((D, QK_NOPE), w.dtype)
    z64 = jnp.zeros((D, QK_CAT - QK_NOPE - QK_ROPE), w.dtype)
    return jnp.concatenate([ckv, z128, kr, z64, z128, _swap_halves(kr), z64], axis=1)


def _ukv_ext(w):
    w = w.reshape(KV_LORA, B_HEADS, QK_NOPE + HD)
    kn, vv = w[..., :QK_NOPE], w[..., QK_NOPE:]
    z = jnp.zeros((KV_LORA, B_HEADS, QK_CAT - QK_NOPE), w.dtype)
    a = jnp.concatenate([kn, z], axis=-1).reshape(KV_LORA, B_HEADS * QK_CAT)
    return jnp.concatenate([a, vv.reshape(KV_LORA, B_HEADS * HD)], axis=1)


def _ext_and_back(fn, w):
    ext, back = jax.vjp(fn, w.astype(F32))
    return ext.astype(BF16), lambda g: back(g.astype(F32))[0]


def _rope_tables(pos_col):
    t = pos_col.shape[0]
    inv = (ROPE_THETA ** (-np.arange(0, QK_ROPE, 2, dtype=np.float32) / QK_ROPE)).astype(np.float32)
    inv_row = np.zeros((1, QK_CAT), np.float32)
    inv_row[0, QK_NOPE:QK_NOPE + QK_ROPE] = np.concatenate([inv, inv])
    sign = np.zeros((1, QK_CAT), np.float32)
    sign[0, QK_NOPE:QK_NOPE + QK_ROPE // 2] = -1.0
    sign[0, QK_NOPE + QK_ROPE // 2:QK_NOPE + QK_ROPE] = 1.0
    is_rope = np.abs(sign)
    is_nope = np.zeros((1, QK_CAT), np.float32)
    is_nope[0, :QK_NOPE] = 1.0

    def fn(p, inv_b, sign_b, rope_b, nope_b):
        ang = p.astype(F32) * inv_b
        return jnp.cos(ang) * rope_b + nope_b, jnp.sin(ang) * sign_b

    consts = [jnp.asarray(a) for a in (inv_row, sign, is_rope, is_nope)]
    return _rowcall(fn, [pos_col] + consts, [_rows(TM, 1)] + [_shared((1, QK_CAT))] * 4,
                    [_sds((t, QK_CAT), F32)] * 2, [_rows(TM, QK_CAT)] * 2, (t // TM,), "rope_tables")


def _local_step(x, mem, pos, target, w, layer_start, layer_done):
    t = x.shape[0]
    g = {}
    th = min(TMH, t)
    head6 = (t // th, A_HEADS)

    mem_n = _rmsnorm_fwd_small(mem, w["mem_norm"])
    rope_c, rope_s = _rope_tables(pos.reshape(t, 1))
    mkv = [_mm(mem_n, w["w_mem_kv"][l], "nn", BF16, f"mkv{l}") for l in range(DEPTH)]

    saved = []
    for l in range(DEPTH):
        sv = {}
        layer_start(l, x)
        x, sv["ffn1"] = _ffn_fwd(x, w["ffn1_norm"][l], w["ffn1_w_gu"][l], w["ffn1_w_down"][l], "ffn1")
        sv["x1"] = x
        n2 = _rmsnorm_fwd(x, w["mix_norm"][l], "mix_norm")
        sv["n2"] = n2
        if l < N_A:
            ha = _mm(n2, w["a_w_in"][l], "nn", BF16, "a_in")
            yc3 = _conv_fwd(ha, w["a_conv"][l], "a_conv")
            blk3 = pl.BlockSpec((3, th, HD), lambda i, h: (0, i, h))
            (qkv3,) = _rowcall(lambda b: jnp.stack(_gdn_prep(b[0].astype(F32), b[1].astype(F32), b[2].astype(F32))),
                               [yc3], [blk3], [_sds((3, t, A_WIDTH), BF16)], [blk3], head6, "a_prep")
            (bg,) = _rowcall(_gates, [ha, _pad128(w["a_A_log"][l], A_HEADS), _pad128(w["a_dt_bias"][l], A_HEADS)],
                             [_rows(TM, 128, A_BA_BLK), _shared((1, 128)), _shared((1, 128))],
                             [_sds((t, 128), F32)], [_rows(TM, 128)], (t // TM,), "a_gates")
            o_gdn, states, pinvs = _gdn_fwd(qkv3, bg, "a_gdn")
            (o_a,) = _rowcall(_outnorm_gate, [o_gdn, ha, w["a_out_norm"][l].reshape(1, HD)],
                              [pl.BlockSpec((th, HD), lambda i, h: (i, h)),
                               pl.BlockSpec((th, HD), lambda i, h: (i, 3 * A_HEADS + h)), _shared((1, HD))],
                              [_sds((t, A_WIDTH), BF16)], [pl.BlockSpec((th, HD), lambda i, h: (i, h))],
                              head6, "a_outnorm")
            o_m = _mem_fwd(ha, A_MQ_BLK, mkv[l], "mem_attn_a")
            sv.update(ha=ha, yc3=yc3, qkv3=qkv3, bg=bg, states=states, pinvs=pinvs, o_gdn=o_gdn)
            cat = jnp.concatenate([o_a, o_m], axis=1)
        else:
            j = l - N_A
            hb = _mm(n2, w["b_w_in"][j], "nn", BF16, "b_in")
            cqn = _rmsnorm_fwd(hb, w["b_q_norm"][j], "b_qnorm", 0, Q_LORA)
            qq = _mm(cqn, w["b_w_uq"][j], "nn", BF16, "b_uq")
            (qc,) = _rowcall(_rope_mix, [qq, qq, rope_c, rope_s],
                             [pl.BlockSpec((th, QK_CAT), lambda i, h: (i, h)),
                              pl.BlockSpec((th, QK_CAT), lambda i, h: (i, B_HEADS + h)),
                              pl.BlockSpec((th, QK_CAT), lambda i, h: (i, 0)),
                              pl.BlockSpec((th, QK_CAT), lambda i, h: (i, 0))],
                             [_sds((t, B_HEADS * QK_CAT), BF16)], [pl.BlockSpec((th, QK_CAT), lambda i, h: (i, h))],
                             head6, "b_qrope")
            o_b, lse, lse_row = _att_fwd(qc, kcat, vmla, "b_attn")
            o_m = _mem_fwd(hb, 1, mkv[l], "mem_attn_b")
            sv.update(hb=hb, cqn=cqn, qc=qc, o_b=o_b, lse=(lse, lse_row))
            cat = jnp.concatenate([o_b, o_m], axis=1)
        sv["cat"] = cat
        x = _mm(cat, w["w_out"][l], "nn", F32, "w_out", res=x)
        x, sv["ffn2"] = _ffn_fwd(x, w["ffn2_norm"][l], w["ffn2_w_gu"][l], w["ffn2_w_down"][l], "ffn2")
        saved.append(sv)
        if l == N_A - 1:
            x_kv = x
            nkv = _rmsnorm_fwd(x, w["kv_in_norm"], "kv_in_norm")
            ckr = _mm(nkv, w["w_dkv"], "nn", BF16, "kv_down")
            ckv_n = _rmsnorm_fwd(ckr, w["kv_lat_norm"], "kv_lat_norm", 0, KV_LORA)
            kvu = _mm(ckv_n, w["w_ukv"], "nn", BF16, "kv_up")
            vmla = kvu[:, B_HEADS * QK_CAT:]
            (kcat,) = _rowcall(_kcat, [kvu, ckr, ckr, rope_c, rope_s],
                               [pl.BlockSpec((th, QK_CAT), lambda i, h: (i, h)),
                                pl.BlockSpec((th, QK_CAT), lambda i, h: (i, 1)),
                                pl.BlockSpec((th, QK_CAT), lambda i, h: (i, 2)),
                                pl.BlockSpec((th, QK_CAT), lambda i, h: (i, 0)),
                                pl.BlockSpec((th, QK_CAT), lambda i, h: (i, 0))],
                               [_sds((t, B_HEADS * QK_CAT), BF16)],
                               [pl.BlockSpec((th, QK_CAT), lambda i, h: (i, h))], head6, "kv_cat")

    def loss_fn(xb, gb, tb):
        def f(xx, gg):
            e = _rms(xx, gg) - tb
            return 0.5 * jnp.sum(jnp.mean(e * e, axis=-1, keepdims=True), axis=0, keepdims=True)
        val, vjp = jax.vjp(f, xb, gb)
        dx, dg = vjp(jnp.ones((1, 1), F32))
        return dx, dg, val * jnp.ones((1, 128), F32)

    d, dfin, loss = _rowcall(loss_fn, [x, w["final_norm"].reshape(1, D), target],
                             [_rows(TM, D), _shared((1, D)), _rows(TM, D)],
                             [_sds((t, D), F32), _sds((1, D), F32), _sds((1, 128), F32)],
                             [_rows(TM, D), _shared((1, D)), _shared((1, 128))], (t // TM,), "loss_head", n_acc=2)
    g["final_norm"] = dfin[0]
    loss = loss[0, 0]

    for name in ("ffn1_norm", "ffn1_w_gu", "ffn1_w_down", "mix_norm", "ffn2_norm", "ffn2_w_gu", "ffn2_w_down",
                 "w_out", "w_mem_kv"):
        g[name] = [None] * DEPTH
    for name in ("a_w_in", "a_conv", "a_A_log", "a_dt_bias", "a_out_norm"):
        g[name] = [None] * N_A
    for name in ("b_w_in", "b_q_norm", "b_w_uq"):
        g[name] = [None] * N_B
    dmkv = [None] * DEPTH
    dkcat = []
    dvmla = []

    for l in reversed(range(DEPTH)):
        sv = saved[l]
        if l == N_A - 1:
            kq = pl.BlockSpec((th, QK_CAT), lambda i, h: (i, h))
            tab = pl.BlockSpec((th, QK_CAT), lambda i, h: (i, 0))

            def dk_fn(c, s, d0, d1):
                dk = d0.astype(F32) + d1.astype(F32)
                return dk, dk * c, dk * s

            dkn, dkr_h, dkrs_h = _rowcall(dk_fn, [rope_c, rope_s, dkcat[0], dkcat[1]], [tab, tab, kq, kq],
                                          [_sds((t, B_HEADS * QK_CAT), BF16)] + [_sds((B_HEADS, t, QK_CAT), BF16)] * 2,
                                          [kq] + [pl.BlockSpec((None, th, QK_CAT), lambda i, h: (h, i, 0))] * 2,
                                          head6, "kv_dcat")

            def sum6(a, b):
                return jnp.sum(a.astype(F32), axis=0), jnp.sum(b.astype(F32), axis=0)

            h6 = pl.BlockSpec((B_HEADS, TM, QK_CAT), lambda i: (0, i, 0))
            dkr, dkrs = _rowcall(sum6, [dkr_h, dkrs_h], [h6, h6], [_sds((t, QK_CAT), BF16)] * 2,
                                 [_rows(TM, QK_CAT)] * 2, (t // TM,), "kv_dkr")

            def addv(a, b):
                return a.astype(F32) + b.astype(F32)

            (dv,) = _rowcall(addv, dvmla, [_rows(TM, B_HEADS * HD)] * 2, [_sds((t, B_HEADS * HD), BF16)],
                             [_rows(TM, B_HEADS * HD)], (t // TM,), "kv_dv")
            dkvu = jnp.concatenate([dkn, dv], axis=1)
            g["w_ukv"] = _mm(ckv_n, dkvu, "tn", F32, "kv_up_dw")
            dckv_n = _mm(dkvu, w["w_ukv"], "nt", BF16, "kv_up_dx")

            def lat_bwd(cb, gb, dnb):
                return _vjp_fn(_rms, 2, (0, 1))(cb, gb, dnb)

            dckv, g["kv_lat_norm"] = _rowcall(lat_bwd, [ckr, w["kv_lat_norm"].reshape(1, KV_LORA), dckv_n],
                                              [_rows(TM, KV_LORA), _shared((1, KV_LORA)), _rows(TM, KV_LORA)],
                                              [_sds((t, KV_LORA), BF16), _sds((1, KV_LORA), F32)],
                                              [_rows(TM, KV_LORA), _shared((1, KV_LORA))], (t // TM,),
                                              "kv_lat_dnorm", n_acc=1)
            g["kv_lat_norm"] = g["kv_lat_norm"][0]
            dckr = jnp.concatenate([dckv, dkr, dkrs], axis=1)
            g["w_dkv"] = _mm(nkv, dckr, "tn", F32, "kv_down_dw")
            dnkv = _mm(dckr, w["w_dkv"], "nt", BF16, "kv_down_dx")
            d, g["kv_in_norm"] = _rmsnorm_bwd(x_kv, w["kv_in_norm"], dnkv, d, "kv_in_dnorm")

        d, g["ffn2_norm"][l], g["ffn2_w_gu"][l], g["ffn2_w_down"][l] = _ffn_bwd(
            d, sv["ffn2"], w["ffn2_norm"][l], w["ffn2_w_gu"][l], w["ffn2_w_down"][l], "ffn2b")
        g["w_out"][l] = _mm(sv["cat"], d, "tn", F32, "w_out_dw")
        dcat = _mm(d, w["w_out"][l], "nt", BF16, "w_out_dx")
        if l < N_A:
            ha, yc3, qkv3, states, o_gdn = sv["ha"], sv["yc3"], sv["qkv3"], sv["states"], sv["o_gdn"]
            dmq, dmkv[l] = _mem_bwd(ha, A_MQ_BLK, mkv[l], dcat, 3, "mem_attn_a_bwd")
            hblk = pl.BlockSpec((th, HD), lambda i, h: (i, h))
            do_gdn, dgate, dgain = _rowcall(
                _vjp_fn(_outnorm_gate, 3, (0, 1, 2)), [o_gdn, ha, w["a_out_norm"][l].reshape(1, HD), dcat],
                [hblk, pl.BlockSpec((th, HD), lambda i, h: (i, 3 * A_HEADS + h)), _shared((1, HD)), hblk],
                [_sds((t, A_WIDTH), BF16), _sds((t, A_WIDTH), BF16), _sds((1, HD), F32)],
                [hblk, hblk, _shared((1, HD))], head6, "a_outnorm_bwd", n_acc=1)
            g["a_out_norm"][l] = dgain[0]
            dqkv3, dbg6 = _gdn_bwd(qkv3, sv["bg"], states, sv["pinvs"], do_gdn, "a_gdn_bwd")

            def dgates(bab, alb, dtb, d6):
                return _vjp_fn(_gates, 3, (0, 1, 2))(bab, alb, dtb, jnp.sum(d6, axis=0))

            dba, dalog, ddt = _rowcall(
                dgates, [ha, _pad128(w["a_A_log"][l], A_HEADS), _pad128(w["a_dt_bias"][l], A_HEADS), dbg6],
                [_rows(TM, 128, A_BA_BLK), _shared((1, 128)), _shared((1, 128)),
                 pl.BlockSpec((A_HEADS, TM, 128), lambda i: (0, i, 0))],
                [_sds((t, 128), BF16), _sds((1, 128), F32), _sds((1, 128), F32)],
                [_rows(TM, 128), _shared((1, 128)), _shared((1, 128))], (t // TM,), "a_gates_bwd", n_acc=2)
            g["a_A_log"][l] = dalog[0, A_HEADS:2 * A_HEADS]
            g["a_dt_bias"][l] = ddt[0, A_HEADS:2 * A_HEADS]
            blk3 = pl.BlockSpec((3, th, HD), lambda i, h: (0, i, h))
            def dprep(b, db):
                return jnp.stack(_vjp_fn(_gdn_prep, 3, (0, 1, 2))(b[0], b[1], b[2], db[0], db[1], db[2]))

            (dyc3,) = _rowcall(dprep, [yc3, dqkv3], [blk3, blk3],
                               [_sds((3, t, A_WIDTH), BF16)], [blk3], head6, "a_prep_bwd")
            dqkv_in, g["a_conv"][l] = _conv_bwd(ha, w["a_conv"][l], dyc3, "a_conv_bwd")
            dha = jnp.concatenate([dqkv_in, dgate, dmq, dba], axis=1)
            g["a_w_in"][l] = _mm(sv["n2"], dha, "tn", F32, "a_in_dw")
            dn2 = _mm(dha, w["a_w_in"][l], "nt", BF16, "a_in_dx")
        else:
            j = l - N_A
            hb, cqn, qc, o_b, lse = sv["hb"], sv["cqn"], sv["qc"], sv["o_b"], sv["lse"]
            dmq, dmkv[l] = _mem_bwd(hb, 1, mkv[l], dcat, 3, "mem_attn_b_bwd")
            dqc, dkc, dvv = _att_bwd(qc, kcat, vmla, o_b, lse[0], lse[1], dcat, "b_attn_bwd")
            dkcat.append(dkc)
            dvmla.append(dvv)
            kq = pl.BlockSpec((th, QK_CAT), lambda i, h: (i, h))
            tab = pl.BlockSpec((th, QK_CAT), lambda i, h: (i, 0))

            def dq_fn(c, s, dq):
                dq = dq.astype(F32)
                return jnp.stack([dq * c, dq * s])

            (dqq,) = _rowcall(dq_fn, [rope_c, rope_s, dqc], [tab, tab, kq],
                              [_sds((2, t, B_HEADS * QK_CAT), BF16)],
                              [pl.BlockSpec((2, th, QK_CAT), lambda i, h: (0, i, h))], head6, "b_qrope_bwd")
            dqq = jnp.concatenate([dqq[0], dqq[1]], axis=1)
            g["b_w_uq"][j] = _mm(cqn, dqq, "tn", F32, "b_uq_dw")
            dcqn = _mm(dqq, w["b_w_uq"][j], "nt", BF16, "b_uq_dx")
            dcq, dqg = _rowcall(_vjp_fn(_rms, 2, (0, 1)), [hb, w["b_q_norm"][j].reshape(1, Q_LORA), dcqn],
                                [_rows(TM, Q_LORA), _shared((1, Q_LORA)), _rows(TM, Q_LORA)],
                                [_sds((t, Q_LORA), BF16), _sds((1, Q_LORA), F32)],
                                [_rows(TM, Q_LORA), _shared((1, Q_LORA))], (t // TM,), "b_qnorm_bwd", n_acc=1)
            g["b_q_norm"][j] = dqg[0]
            dhb = jnp.concatenate([dcq, dmq], axis=1)
            g["b_w_in"][j] = _mm(sv["n2"], dhb, "tn", F32, "b_in_dw")
            dn2 = _mm(dhb, w["b_w_in"][j], "nt", BF16, "b_in_dx")
        d, g["mix_norm"][l] = _rmsnorm_bwd(sv["x1"], w["mix_norm"][l], dn2, d, "mix_dnorm")
        d, g["ffn1_norm"][l], g["ffn1_w_gu"][l], g["ffn1_w_down"][l] = _ffn_bwd(
            d, sv["ffn1"], w["ffn1_norm"][l], w["ffn1_w_gu"][l], w["ffn1_w_down"][l], "ffn1b")
        d = layer_done(l, g, d)

    dmem_n = None
    for l in range(DEPTH):
        g["w_mem_kv"][l] = _mm(mem_n, dmkv[l], "tn", F32, f"mkv_dw{l}")
        dmem_n = _mm(dmkv[l], w["w_mem_kv"][l], "nt", F32, f"mkv_dx{l}", res=dmem_n)
    (_, gmn) = _rowcall(_vjp_fn(_rms, 2, (0, 1)), [mem, w["mem_norm"].reshape(1, D), dmem_n],
                        [_shared((N_MEM, D)), _shared((1, D)), _shared((N_MEM, D))],
                        [_sds((N_MEM, D), F32), _sds((1, D), F32)], [_shared((N_MEM, D)), _shared((1, D))],
                        (1,), "mem_dnorm")
    g["mem_norm"] = gmn[0]
    return loss, d, g


def _pad128(v, offset):
    return jnp.pad(v.astype(F32).reshape(1, -1), ((0, 0), (offset, 128 - offset - v.shape[0])))


def _rmsnorm_fwd_small(x, gain):
    r, w = x.shape
    (n,) = _rowcall(_rms, [x, gain.reshape(1, w)], [_shared((r, w)), _shared((1, w))],
                    [_sds((r, w), BF16)], [_shared((r, w))], (1,), "mem_norm")
    return n


def _exchange(srcs, gather, name):
    n = len(srcs)
    blks = [tuple(s.shape) if gather else tuple(s.shape[1:]) for s in srcs]

    def body(*refs):
        src_refs, out_refs = refs[:n], refs[n:2 * n]
        send_sems, recv_sems, local_sems = refs[2 * n:]
        x, y, c = lax.axis_index("x"), lax.axis_index("y"), lax.axis_index("c")
        me = 4 * x + 2 * y + c
        copies = []
        for k in range(1, N_DEV):
            px = (x + (k >> 2 & 1)) % 2
            py = (y + (k >> 1 & 1)) % 2
            pc = (c + (k & 1)) % 2
            peer = 4 * px + 2 * py + pc
            for a in range(n):
                cp = pltpu.make_async_remote_copy(
                    src_ref=src_refs[a] if gather else src_refs[a].at[peer], dst_ref=out_refs[a].at[me],
                    send_sem=send_sems.at[a, k - 1], recv_sem=recv_sems.at[a, k - 1],
                    device_id=(px, py, pc), device_id_type=pl.DeviceIdType.MESH)
                cp.start()
                copies.append(cp)
        for a in range(n):
            cp = pltpu.make_async_copy(src_refs[a] if gather else src_refs[a].at[me], out_refs[a].at[me],
                                       local_sems.at[a])
            cp.start()
            copies.append(cp)
        for cp in copies:
            cp.wait()

    return pl.pallas_call(
        body, out_shape=tuple(_sds((N_DEV,) + b, s.dtype) for b, s in zip(blks, srcs)),
        in_specs=[pl.BlockSpec(memory_space=pl.ANY)] * n, out_specs=tuple([pl.BlockSpec(memory_space=pl.ANY)] * n),
        scratch_shapes=[pltpu.SemaphoreType.DMA((n, N_DEV - 1)), pltpu.SemaphoreType.DMA((n, N_DEV - 1)),
                        pltpu.SemaphoreType.DMA((n,))],
        name=name)(*srcs)


_HBM = pl.BlockSpec(memory_space=pltpu.HBM)
_SEM = pl.BlockSpec(memory_space=pltpu.SEMAPHORE)
_EFFECT = pltpu.SideEffectType.DATAFLOW_SIDE_EFFECTING


def _split_copies(src_refs, land_refs, send_sems, recv_sems, gather):
    x, y, c = lax.axis_index("x"), lax.axis_index("y"), lax.axis_index("c")
    me = 4 * x + 2 * y + c
    copies = []
    for k in range(1, N_DEV):
        peer = ((x + (k >> 2 & 1)) % 2, (y + (k >> 1 & 1)) % 2, (c + (k & 1)) % 2)
        for a in range(len(src_refs)):
            i = (k - 1) * len(src_refs) + a
            src = src_refs[a] if gather else src_refs[a].at[4 * peer[0] + 2 * peer[1] + peer[2]]
            copies.append(pltpu.make_async_remote_copy(
                src_ref=src, dst_ref=land_refs[a].at[me], send_sem=send_sems[i],
                recv_sem=recv_sems[i], device_id=peer, device_id_type=pl.DeviceIdType.MESH))
    return copies


def _split_start(srcs, gather, name):
    n = len(srcs)
    srcs = [pltpu.with_memory_space_constraint(s, pltpu.HBM) for s in srcs]
    lands = [pltpu.with_memory_space_constraint(
        lax.empty(((N_DEV,) + s.shape) if gather else s.shape, s.dtype), pltpu.HBM) for s in srcs]

    ns = n * (N_DEV - 1)

    def body(*refs):
        sems = refs[2 * n:2 * n + 2 * ns]
        for cp in _split_copies(refs[:n], refs[n:2 * n], sems[:ns], sems[ns:], gather):
            cp.start()
        refs[-1][...] = jnp.zeros_like(refs[-1])

    outs = pl.pallas_call(
        body, name=name,
        out_shape=(*[pltpu.SemaphoreType.DMA(())] * (2 * ns), *[pltpu.HBM(a.shape, a.dtype) for a in srcs + lands],
                   _sds((8, 128), F32)),
        in_specs=[_HBM] * (2 * n),
        out_specs=(*[_SEM] * (2 * ns), *[_HBM] * (2 * n), pl.BlockSpec(memory_space=pltpu.VMEM)),
        input_output_aliases={i: 2 * ns + i for i in range(2 * n)},
        compiler_params=pltpu.CompilerParams(has_side_effects=_EFFECT))(*srcs, *lands)
    sems, rest = list(outs[:2 * ns]), outs[2 * ns:]
    return sems[:ns], sems[ns:], list(rest[:n]), list(rest[n:2 * n]), rest[-1]


def _split_wait(flight, after, gather, name):
    send_sems, recv_sems, srcs, lands, _ = flight
    n = len(srcs)

    ns = len(send_sems)

    def body(*refs):
        sems = refs[2 * n:2 * n + 2 * ns]
        for cp in _split_copies(refs[:n], refs[n:2 * n], sems[:ns], sems[ns:], gather):
            cp.wait_send()
            cp.wait_recv()

    outs = pl.pallas_call(
        body, name=name, out_shape=tuple(pltpu.HBM(a.shape, a.dtype) for a in srcs + lands),
        in_specs=[_HBM] * (2 * n) + [_SEM] * (2 * ns) + [pl.BlockSpec(memory_space=pl.ANY)],
        out_specs=tuple([_HBM] * (2 * n)), input_output_aliases={i: i for i in range(2 * n)},
        compiler_params=pltpu.CompilerParams(has_side_effects=_EFFECT))(*srcs, *lands, *send_sems, *recv_sems, after)
    return list(outs[n:])


def _reduce_adamw(parts, wp, mp, vp, name):
    r, cols = wp.shape
    tr = _tile_rows(r, cols)
    c1 = 1.0 - ADAM_B1 ** ADAM_STEP
    c2 = 1.0 - ADAM_B2 ** ADAM_STEP

    def fn(pb, wb, mb, vb):
        gsum = pb[0].astype(F32)
        for j in range(1, N_DEV):
            gsum = gsum + pb[j].astype(F32)
        m_new = ADAM_B1 * mb + (1.0 - ADAM_B1) * gsum
        v_new = ADAM_B2 * vb + (1.0 - ADAM_B2) * (gsum * gsum)
        delta = -ADAM_LR * ((m_new / c1) / (jnp.sqrt(v_new / c2) + ADAM_EPS) + ADAM_WD * wb)
        return gsum, delta, m_new, v_new

    row = _rows(tr, cols)
    return _rowcall(fn, [parts, wp, mp, vp],
                    [pl.BlockSpec((N_DEV, tr, cols), lambda i: (0, i, 0)), row, row, row],
                    [_sds((r, cols), F32)] * 4, [row] * 4, (r // tr,), name)


def _tile_rows(r, cols):
    for t in (512, 256, 128, 64, 32, 16):
        if r % t == 0 and t * cols <= 160 * 1024:
            return t
    return r


def _pack(arrs):
    flat = jnp.concatenate([a.reshape(-1).astype(F32) for a in arrs])
    n = flat.shape[0]
    unit = PACK_W * PACK_ROWS
    tot = -(-n // unit) * unit
    return jnp.pad(flat, (0, tot - n)).reshape(tot // PACK_W, PACK_W)


def _unpack(buf, shapes):
    out, off = [], 0
    flat = buf.reshape(-1)
    for s in shapes:
        n = int(np.prod(s))
        out.append(flat[off:off + n].reshape(s))
        off += n
    return out


def _as2d(a):
    return a.reshape(-1, a.shape[-1])


_SHARDED = ["ffn1_w_gu", "ffn1_w_down", "ffn2_w_gu", "ffn2_w_down", "w_out", "w_mem_kv", "a_w_in", "a_conv",
            "b_w_in", "b_w_uq", "w_dkv", "w_ukv"]
_COL_SHARDED = {"ffn1_w_gu", "ffn2_w_gu", "a_conv", "b_w_uq", "w_ukv"}
_LAYERED = {"ffn1_w_gu": DEPTH, "ffn1_w_down": DEPTH, "ffn2_w_gu": DEPTH, "ffn2_w_down": DEPTH, "w_out": DEPTH,
            "w_mem_kv": DEPTH, "a_w_in": N_A, "a_conv": N_A, "b_w_in": N_B, "b_w_uq": N_B}
_GATHER_FIRST = [("ffn1_w_gu", 0, 1), ("ffn1_w_down", 0, 1), ("ffn2_w_gu", 0, 1), ("ffn2_w_down", 0, 1),
                 ("w_out", 0, 1), ("a_w_in", 0, 1), ("a_conv", 0, 1), ("w_mem_kv", 0, DEPTH)]
_GATHER_REST = [("ffn1_w_gu", 1, DEPTH), ("ffn1_w_down", 1, DEPTH), ("ffn2_w_gu", 1, DEPTH),
                ("ffn2_w_down", 1, DEPTH), ("w_out", 1, DEPTH), ("a_w_in", 1, N_A), ("a_conv", 1, N_A),
                ("b_w_in", 0, N_B), ("b_w_uq", 0, N_B), ("w_dkv", None, None), ("w_ukv", None, None)]
_SCATTER_HI = [("ffn1_w_gu", 2, DEPTH), ("ffn1_w_down", 2, DEPTH), ("ffn2_w_gu", 2, DEPTH), ("ffn2_w_down", 2, DEPTH),
               ("w_out", 2, DEPTH), ("b_w_in", 0, N_B), ("b_w_uq", 0, N_B)]
_SCATTER_MID = [("ffn1_w_gu", 1, 2), ("ffn1_w_down", 1, 2), ("ffn2_w_gu", 1, 2), ("ffn2_w_down", 1, 2),
                ("w_out", 1, 2), ("a_w_in", 1, N_A), ("a_conv", 1, N_A), ("w_dkv", None, None), ("w_ukv", None, None)]
_SCATTER_LO = [("ffn1_w_gu", 0, 1), ("ffn1_w_down", 0, 1), ("ffn2_w_gu", 0, 1), ("ffn2_w_down", 0, 1),
               ("w_out", 0, 1), ("a_w_in", 0, 1), ("a_conv", 0, 1), ("w_mem_kv", 0, DEPTH)]
_REPLICATED = ["ffn1_norm", "mix_norm", "ffn2_norm", "mem_norm", "a_A_log", "a_dt_bias", "a_out_norm", "b_q_norm",
               "kv_in_norm", "kv_lat_norm", "final_norm"]
_WEIGHTS = ["ffn1_norm", "ffn1_w_gu", "ffn1_w_down", "mix_norm", "ffn2_norm", "ffn2_w_gu", "ffn2_w_down", "w_out",
            "mem_norm", "w_mem_kv", "a_w_in", "a_conv", "a_A_log", "a_dt_bias", "a_out_norm", "b_w_in", "b_q_norm",
            "b_w_uq", "kv_in_norm", "w_dkv", "kv_lat_norm", "w_ukv", "final_norm"]


def _full_from_shards(name, sh):
    if name in ("ffn1_w_gu", "ffn2_w_gu"):
        return sh
    if name in ("ffn1_w_down", "ffn2_w_down"):
        return sh.reshape(4, FF_SHARD, D)
    if name in _COL_SHARDED:
        return jnp.moveaxis(sh, 0, -2).reshape(sh.shape[1:-1] + (N_DEV * sh.shape[-1],))
    return sh.reshape((N_DEV * sh.shape[1],) + sh.shape[2:])


def _shards_from_full(name, full):
    if name in ("ffn1_w_gu", "ffn2_w_gu"):
        return full
    if name in ("ffn1_w_down", "ffn2_w_down"):
        return full.reshape(N_DEV, D_FF // N_DEV, D)
    if name in _COL_SHARDED:
        r, cc = full.shape
        return jnp.moveaxis(full.reshape(r, N_DEV, cc // N_DEV), 1, 0)
    return full.reshape((N_DEV, full.shape[0] // N_DEV) + full.shape[1:])


def kernel(x, mem, positions, ffn1_norm, ffn1_w_gu, ffn1_w_down, mix_norm, ffn2_norm, ffn2_w_gu, ffn2_w_down, w_out, mem_norm, w_mem_kv, a_w_in, a_conv, a_A_log, a_dt_bias, a_out_norm, b_w_in, b_q_norm, b_w_uq, kv_in_norm, w_dkv, kv_lat_norm, w_ukv, final_norm, loss_target, m_ffn1_norm, m_ffn1_w_gu, m_ffn1_w_down, m_mix_norm, m_ffn2_norm, m_ffn2_w_gu, m_ffn2_w_down, m_w_out, m_mem_norm, m_w_mem_kv, m_a_w_in, m_a_conv, m_a_A_log, m_a_dt_bias, m_a_out_norm, m_b_w_in, m_b_q_norm, m_b_w_uq, m_kv_in_norm, m_w_dkv, m_kv_lat_norm, m_w_ukv, m_final_norm, v_ffn1_norm, v_ffn1_w_gu, v_ffn1_w_down, v_mix_norm, v_ffn2_norm, v_ffn2_w_gu, v_ffn2_w_down, v_w_out, v_mem_norm, v_w_mem_kv, v_a_w_in, v_a_conv, v_a_A_log, v_a_dt_bias, v_a_out_norm, v_b_w_in, v_b_q_norm, v_b_w_uq, v_kv_in_norm, v_w_dkv, v_kv_lat_norm, v_w_ukv, v_final_norm):
    loc = dict(locals())
    wl = {n: loc[n] for n in _WEIGHTS}
    ml = {n: loc["m_" + n] for n in _WEIGHTS}
    vl = {n: loc["v_" + n] for n in _WEIGHTS}

    me = 4 * lax.axis_index("x") + 2 * lax.axis_index("y") + lax.axis_index("c")
    w = {n: wl[n] for n in _REPLICATED}
    for n in _SHARDED:
        w[n] = [None] * _LAYERED[n] if n in _LAYERED else None
    back = {}

    def src_of(item):
        n, lo, hi = item
        return (wl[n] if lo is None else wl[n][lo:hi]).astype(BF16)

    def install(items, pieces):
        for (n, lo, hi), p in zip(items, pieces):
            if lo is None:
                w[n] = _full_from_shards(n, p)
            else:
                for l in range(lo, hi):
                    w[n][l] = _full_from_shards(n, p[:, l - lo])
        for n, lo, hi in items:
            if n == "a_w_in":
                for l in range(lo, hi):
                    w[n][l], back[(n, l)] = _ext_and_back(_a_in_ext, w[n][l])
            elif n == "a_conv":
                for l in range(lo, hi):
                    w[n][l] = w[n][l].astype(F32)
            elif n == "b_w_uq":
                for l in range(lo, hi):
                    w[n][l], back[(n, l)] = _ext_and_back(_uq_ext, w[n][l])
            elif n == "w_dkv":
                w[n], back[n] = _ext_and_back(_dkv_ext, w[n])
            elif n == "w_ukv":
                w[n], back[n] = _ext_and_back(_ukv_ext, w[n])

    install(_GATHER_FIRST, _exchange([src_of(it) for it in _GATHER_FIRST], True, "gather_first"))
    rest_srcs = [src_of(it) for it in _GATHER_REST]
    flight = _split_start(rest_srcs, True, "gather_rest_start")
    w["ffn1_norm"] = w["ffn1_norm"] + flight[-1][0, 0]

    def layer_start(l, stream):
        if l == 1:
            lands = _split_wait(flight, stream, True, "gather_rest_wait")
            own = [lax.dynamic_update_slice(ld, s[None], (me,) + (0,) * s.ndim) for ld, s in zip(lands, rest_srcs)]
            install(_GATHER_REST, own)

    def grad_src(item, g):
        n, lo, hi = item

        def one(l):
            gl = g[n] if l is None else g[n][l]
            key = n if l is None else (n, l)
            if key in back:
                gl = back[key](gl)
            return _shards_from_full(n, gl).astype(BF16)

        return one(None) if lo is None else jnp.stack([one(l) for l in range(lo, hi)], axis=1)

    sent = {}

    def layer_done(l, g, d):
        for lname, items in ((2, _SCATTER_HI), (1, _SCATTER_MID)):
            if l == lname:
                srcs = [grad_src(it, g) for it in items]
                sent[l] = (srcs, _split_start(srcs, False, f"scatter_{l}_start"))
                return d + sent[l][1][-1][0, 0]
        return d

    loss, dx, g = _local_step(x[0], mem[0], positions[0], loss_target[0], w, layer_start, layer_done)
    loss = lax.psum(loss, ("x", "y", "c"))

    lo_srcs = [grad_src(it, g) for it in _SCATTER_LO]
    pieces = {n: [] for n in _SHARDED}
    for (n, lo, hi), p in zip(_SCATTER_LO, _exchange(lo_srcs, False, "scatter_grads")):
        pieces[n].append((lo, p))
    for l, items in ((1, _SCATTER_MID), (2, _SCATTER_HI)):
        srcs, fl = sent[l]
        lands = _split_wait(fl, dx, False, f"scatter_{l}_wait")
        for (n, lo, hi), s, ld in zip(items, srcs, lands):
            mine = lax.dynamic_slice(s, (me,) + (0,) * (s.ndim - 1), (1,) + s.shape[1:])
            pieces[n].append((lo, lax.dynamic_update_slice(ld, mine, (me,) + (0,) * (s.ndim - 1))))
    out = {}
    for n in _SHARDED:
        ps = [p for _, p in sorted(pieces[n], key=lambda e: -1 if e[0] is None else e[0])]
        p = ps[0] if len(ps) == 1 else jnp.concatenate(ps, axis=1)
        shape = wl[n].shape
        res = _reduce_adamw(p.reshape(N_DEV, -1, shape[-1]), _as2d(wl[n]), _as2d(ml[n]), _as2d(vl[n]), "adamw_" + n)
        for kind, buf in zip(("grad", "delta", "new_m", "new_v"), res):
            out[(kind, n)] = buf.reshape(shape)

    rep_shapes = [wl[n].shape for n in _REPLICATED]
    grep = [jnp.stack(g[n]) if isinstance(g[n], list) else g[n] for n in _REPLICATED]
    (rparts,) = _exchange([_pack(grep)], True, "gather_small_grads")
    res = _reduce_adamw(rparts, _pack([wl[n] for n in _REPLICATED]), _pack([ml[n] for n in _REPLICATED]),
                        _pack([vl[n] for n in _REPLICATED]), "adamw_replicated")
    for kind, buf in zip(("grad", "delta", "new_m", "new_v"), res):
        for n, a in zip(_REPLICATED, _unpack(buf, rep_shapes)):
            out[(kind, n)] = a

    return (loss, dx[None], *[out[("grad", n)] for n in _WEIGHTS], *[out[("delta", n)] for n in _WEIGHTS],
            *[out[("new_m", n)] for n in _WEIGHTS], *[out[("new_v", n)] for n in _WEIGHTS])
```

```python
import functools

import numpy as np
import jax
import jax.numpy as jnp
from jax import lax
from jax.experimental import pallas as pl
from jax.experimental.pallas import tpu as pltpu

F32 = jnp.float32
BF16 = jnp.bfloat16

N_DEV = 8
D = 1024
D_FF = 2816
FF_SHARD = 2 * D_FF // N_DEV
DEPTH = 4
N_A = 2
N_B = 2
EPS = 1e-6
CHUNK = 64
GROUP = 256
GDN_HPS = 3
A_HEADS = 6
HD = 128
A_WIDTH = A_HEADS * HD
B_HEADS = 6
QK_NOPE = 128
QK_ROPE = 64
QK_CAT = 256
Q_LORA = 256
KV_LORA = 256
MEM_HEADS = 4
MEM_HD = 64
MEM_W = 256
N_MEM = 256
ROPE_THETA = 10000.0
ATT_SCALE = (QK_NOPE + QK_ROPE) ** -0.5
LN2 = 0.6931471805599453
Q_PRESCALE = ATT_SCALE / LN2
A_IN = 4 * A_WIDTH + 2 * A_HEADS + MEM_W
A_MQ_BLK = 4 * A_WIDTH // MEM_W
A_BA_BLK = (4 * A_WIDTH + MEM_W) // 128

ADAM_LR = 0.001
ADAM_B1 = 0.9
ADAM_B2 = 0.999
ADAM_EPS = 1e-08
ADAM_WD = 0.01
ADAM_STEP = 10

VMEM_LIMIT = 56 * 1024 * 1024
TM = 512
TMM = 1024
TMF = 2048
TMH = 2048
ATT_BQ = 1024
PACK_W = 1024
PACK_ROWS = 32


def _cparams(sem):
    return pltpu.CompilerParams(dimension_semantics=sem, vmem_limit_bytes=VMEM_LIMIT)


_DIMS = {"nn": ((1,), (0,)), "nt": ((1,), (1,)), "tn": ((0,), (0,))}


def _dot(a, b, dims="nn"):
    return lax.dot_general(a.astype(BF16), b.astype(BF16), (_DIMS[dims], ((), ())),
                           preferred_element_type=F32)


def _matmul(a, b, *, dims, grid, a_spec, b_spec, o_spec, out_shape, name, scale=1.0,
            res=None, res_spec=None, a_fn=None, epilogue=None, acc_shape=None, chunks=1):
    nk = grid[-1]
    kax = len(grid) - 1
    if acc_shape is None:
        acc_shape = tuple(s for s in o_spec.block_shape if s is not None)
    a_rows = a_spec.block_shape[-2] // chunks

    def rows_of(ref, c):
        rs = slice(c * a_rows, (c + 1) * a_rows)
        return ref[:, rs, :] if len(ref.shape) == 3 else ref[rs, :]

    def body(*refs):
        if res is None:
            a_ref, b_ref, o_ref, acc = refs
            r_ref = None
        else:
            a_ref, b_ref, r_ref, o_ref, acc = refs
        k = pl.program_id(kax)

        if chunks > 1 and nk == 1 and dims != "tn":
            for c in range(chunks):
                a_c = rows_of(a_ref, c)
                y = _dot(a_c if a_fn is None else a_fn(a_c), b_ref[...], dims) * scale
                rs = slice(c * a_rows, (c + 1) * a_rows)
                y = epilogue(y, rows_of(r_ref, c))
                if len(o_ref.shape) == 3:
                    o_ref[:, rs, :] = y.astype(o_ref.dtype)
                else:
                    o_ref[rs, :] = y.astype(o_ref.dtype)
            return

        @pl.when(k == 0)
        def _():
            acc[...] = jnp.zeros_like(acc)

        for c in range(chunks):
            a_c = rows_of(a_ref, c) if chunks > 1 else a_ref[...]
            a_c = a_c if a_fn is None else a_fn(a_c)
            rs = slice(c * a_rows, (c + 1) * a_rows)
            if chunks == 1:
                acc[...] += _dot(a_c, b_ref[...], dims)
            elif dims == "tn":
                acc[...] += _dot(a_c, b_ref[rs, :], dims)
            else:
                acc[rs, :] += _dot(a_c, b_ref[...], dims)

        @pl.when(k == nk - 1)
        def _():
            y = acc[...] * scale
            if epilogue is not None:
                y = epilogue(y, r_ref[...])
            elif r_ref is not None:
                y = y + r_ref[...].astype(F32)
            o_ref[...] = y.astype(o_ref.dtype)

    args = [a, b] + ([res] if res is not None else [])
    specs = [a_spec, b_spec] + ([res_spec] if res is not None else [])
    sem = ("parallel",) * kax + ("arbitrary",)
    return pl.pallas_call(
        body, out_shape=out_shape, grid=grid, in_specs=specs, out_specs=o_spec,
        scratch_shapes=[pltpu.VMEM(acc_shape, F32)], name=name, compiler_params=_cparams(sem))(*args)


def _tile(n, cap):
    if n <= cap:
        return n
    t = cap - cap % 128
    while t >= 128:
        if n % t == 0:
            return t
        t -= 128
    raise ValueError(f"no tile for {n}")


def _mm(a, b, dims, out_dtype, name, scale=1.0, res=None):
    if dims == "tn":
        kk, m = a.shape
        n = b.shape[1]
        tk, tn = _tile(kk, TMM), _tile(n, 1152)
        return _matmul(a, b, dims=dims, grid=(1, n // tn, kk // tk),
                       a_spec=pl.BlockSpec((tk, m), lambda i, j, k: (k, 0)),
                       b_spec=pl.BlockSpec((tk, tn), lambda i, j, k: (k, j)),
                       o_spec=pl.BlockSpec((m, tn), lambda i, j, k: (0, j)),
                       out_shape=jax.ShapeDtypeStruct((m, n), out_dtype), name=name, scale=scale)
    m, kk = a.shape
    n = b.shape[1] if dims == "nn" else b.shape[0]
    tm, tn, tk = _tile(m, TMM), _tile(n, 1152), _tile(kk, 1536)
    if dims == "nn":
        b_spec = pl.BlockSpec((tk, tn), lambda i, j, k: (k, j))
    else:
        b_spec = pl.BlockSpec((tn, tk), lambda i, j, k: (j, k))
    o_spec = pl.BlockSpec((tm, tn), lambda i, j, k: (i, j))
    return _matmul(a, b, dims=dims, grid=(m // tm, n // tn, kk // tk),
                   a_spec=pl.BlockSpec((tm, tk), lambda i, j, k: (i, k)), b_spec=b_spec, o_spec=o_spec,
                   out_shape=jax.ShapeDtypeStruct((m, n), out_dtype), name=name, scale=scale,
                   res=res, res_spec=o_spec if res is not None else None)


def _rowcall(fn, args, in_specs, out_shapes, out_specs, grid, name, n_acc=0):
    n_in, n_out = len(args), len(out_shapes)

    def body(*refs):
        outs = fn(*[r[...] for r in refs[:n_in]])
        if not isinstance(outs, (tuple, list)):
            outs = (outs,)
        first = pl.program_id(0) == 0
        for ax in range(1, len(grid)):
            first = jnp.logical_and(first, pl.program_id(ax) == 0)
        for idx, (o_ref, val) in enumerate(zip(refs[n_in:], outs)):
            if idx >= n_out - n_acc:
                @pl.when(first)
                def _(o_ref=o_ref):
                    o_ref[...] = jnp.zeros_like(o_ref)

                o_ref[...] += val.astype(o_ref.dtype)
            else:
                o_ref[...] = val.astype(o_ref.dtype)

    sem = (("arbitrary",) if n_acc else ("parallel",)) * len(grid)
    res = pl.pallas_call(body, out_shape=tuple(out_shapes), grid=grid, in_specs=list(in_specs),
                         out_specs=tuple(out_specs), name=name, compiler_params=_cparams(sem))(*args)
    return res


def _vjp_fn(fn, n_in, wrt):
    def bwd(*blocks):
        ins = [b.astype(F32) for b in blocks[:n_in]]
        cts = [c.astype(F32) for c in blocks[n_in:]]
        outs, vjp = jax.vjp(fn, *ins)
        if isinstance(outs, (tuple, list)):
            grads = vjp(tuple(cts))
        else:
            grads = vjp(cts[0])
        return tuple(grads[i] for i in wrt)
    return bwd


def _sds(shape, dtype):
    return jax.ShapeDtypeStruct(tuple(shape), dtype)


def _rows(tm, w, col=0):
    return pl.BlockSpec((tm, w), lambda i, *_: (i, col))


def _shared(shape):
    nd = len(shape)
    return pl.BlockSpec(tuple(shape), lambda *_: (0,) * nd)


def _rms(x, g):
    return x * lax.rsqrt(jnp.mean(x * x, axis=-1, keepdims=True) + EPS) * g


def _silu(x):
    return x * jax.nn.sigmoid(x)


def _swiglu_pair(gu):
    return _silu(gu[0].astype(F32)) * gu[1].astype(F32)


def _swiglu_bwd(dh, gu):
    g, u = gu[0].astype(F32), gu[1].astype(F32)
    sg = jax.nn.sigmoid(g)
    return jnp.stack([dh * u * sg * (1.0 + g * (1.0 - sg)), dh * g * sg])


def _gdn_prep(q, k, v):
    q, k, v = _silu(q), _silu(k), _silu(v)
    q = q * lax.rsqrt(jnp.sum(q * q, axis=-1, keepdims=True) + EPS) * (HD ** -0.5)
    k = k * lax.rsqrt(jnp.sum(k * k, axis=-1, keepdims=True) + EPS)
    return q, k, v


def _gates(ba, a_log, dt_bias):
    lane = lax.broadcasted_iota(jnp.int32, ba.shape, 1)
    beta = jax.nn.sigmoid(ba)
    z = ba + dt_bias
    softplus = jnp.maximum(z, 0.0) + jnp.log(1.0 + jnp.exp(-jnp.abs(z)))
    g = -jnp.exp(a_log) * softplus
    return jnp.where(lane < A_HEADS, beta, jnp.where(lane < 2 * A_HEADS, g, 0.0))


def _outnorm_gate(o, gate, gain):
    return _rms(o, gain) * _silu(gate)


def _memattn(q, k, v):
    lane = lax.shift_right_logical(lax.broadcasted_iota(jnp.int32, (1, MEM_W), 1), 6)
    out = jnp.zeros(q.shape, F32)
    for h in range(MEM_HEADS):
        mh = (lane == h).astype(F32)
        s = _dot(q * mh, k, "nt") * (MEM_HD ** -0.5)
        s = s - lax.stop_gradient(jnp.max(s, axis=-1, keepdims=True))
        p = jnp.exp(s)
        p = p / jnp.sum(p, axis=-1, keepdims=True)
        out = out + _dot(p, v * mh)
    return out


def _rope_mix(a, a_sw, c, s):
    return (a * c + a_sw * s) * Q_PRESCALE


def _kcat(kn, kr, kr_sw, c, s):
    return kn + kr * c + kr_sw * s


@jax.custom_vjp
def _neumann_inverse(nmat):
    n = nmat.shape[0]
    eye = (lax.broadcasted_iota(jnp.int32, (n, n), 0) == lax.broadcasted_iota(jnp.int32, (n, n), 1)).astype(F32)
    pinv = eye + nmat
    npow = nmat
    for _ in range(5):
        npow = _dot(npow, npow)
        pinv = pinv + _dot(pinv, npow)
    return pinv


def _neumann_inverse_fwd(nmat):
    pinv = _neumann_inverse(nmat)
    return pinv, pinv


def _neumann_inverse_bwd(pinv, ct):
    return (_dot(_dot(pinv, ct, "tn"), pinv, "nt"),)


_neumann_inverse.defvjp(_neumann_inverse_fwd, _neumann_inverse_bwd)


@jax.custom_vjp
def _known_inverse(nmat, pinv):
    return pinv


def _known_inverse_fwd(nmat, pinv):
    return pinv, pinv


def _known_inverse_bwd(pinv, ct):
    return _dot(_dot(pinv, ct, "tn"), pinv, "nt"), jnp.zeros_like(pinv)


_known_inverse.defvjp(_known_inverse_fwd, _known_inverse_bwd)


def _gdn_local(q, k, v, beta, gcol, grow, pinv_kept=None):
    n = GROUP
    ri = lax.broadcasted_iota(jnp.int32, (n, n), 0)
    ci = lax.broadcasted_iota(jnp.int32, (n, n), 1)
    same = lax.shift_right_logical(ri, 6) == lax.shift_right_logical(ci, 6)
    lower = jnp.logical_and(same, ci <= ri)
    strict = jnp.logical_and(same, ci < ri)
    gc_col = jnp.sum(lower.astype(F32) * grow, axis=1, keepdims=True)
    gc_row = jnp.sum(jnp.logical_and(same, ri <= ci).astype(F32) * gcol, axis=0, keepdims=True)
    glast = jnp.sum(same.astype(F32) * grow, axis=1, keepdims=True)
    decay = jnp.where(lower, jnp.exp(jnp.where(lower, gc_col - gc_row, 0.0)), 0.0)
    kb = k * beta
    nmat = -jnp.where(strict, _dot(kb, k, "nt") * decay, 0.0)
    pinv = _neumann_inverse(nmat) if pinv_kept is None else _known_inverse(nmat, pinv_kept)
    e_gc = jnp.exp(gc_col)
    u = _dot(pinv, v * beta)
    w = _dot(pinv, kb * e_gc)
    qk = _dot(q, k, "nt") * decay
    fold = (jnp.bitwise_and(lax.broadcasted_iota(jnp.int32, (n, CHUNK), 0), CHUNK - 1)
            == lax.broadcasted_iota(jnp.int32, (n, CHUNK), 1)).astype(F32)
    qk_c = _dot(qk, fold)
    q_dec = q * e_gc
    k_dec = k * jnp.exp(glast - gc_col)
    dmat = jnp.exp(glast) * jnp.ones((1, HD), F32)
    if pinv_kept is None:
        return u, w, q_dec, k_dec, qk_c, dmat, pinv
    return u, w, q_dec, k_dec, qk_c, dmat


def _gdn_step(s, w_c, u_c, qd_c, kd_c, qk_c, d_c):
    v_new = u_c - _dot(w_c, s)
    out = _dot(qd_c, s) + _dot(qk_c, v_new)
    d_row = jnp.mean(d_c, axis=0, keepdims=True)
    s_new = s * d_row + _dot(kd_c, v_new, "tn")
    return s_new, out


def _rmsnorm_fwd(x, gain, name, col=0, width=None):
    t = x.shape[0]
    w = width or x.shape[1]
    (n,) = _rowcall(_rms, [x, gain.reshape(1, w)], [_rows(TM, w, col), _shared((1, w))],
                    [_sds((t, w), BF16)], [_rows(TM, w)], (t // TM,), name)
    return n


def _rmsnorm_bwd(x, gain, dn, dres, name):
    t, w = x.shape
    fn = _vjp_fn(_rms, 2, (0, 1))

    def bwd(xb, gb, dnb, drb):
        dx, dg = fn(xb, gb, dnb)
        return dx + drb, dg

    dx, dg = _rowcall(bwd, [x, gain.reshape(1, w), dn, dres],
                      [_rows(TM, w), _shared((1, w)), _rows(TM, w), _rows(TM, w)],
                      [_sds((t, w), F32), _sds((1, w), F32)], [_rows(TM, w), _shared((1, w))],
                      (t // TM,), name, n_acc=1)
    return dx, dg[0]


def _ffn_fwd(x, gain, wgu8, wd4, tag):
    t = x.shape[0]
    nt = t // TMM
    tf = min(TMF, t)
    n = _rmsnorm_fwd(x, gain, tag + "_norm")
    gu = _matmul(n, wgu8, dims="nn", grid=(N_DEV, t // tf, 1),
                 a_spec=pl.BlockSpec((tf, D), lambda j, i, k: (i, 0)),
                 b_spec=pl.BlockSpec((None, D, FF_SHARD), lambda j, i, k: (j, 0, 0)),
                 o_spec=pl.BlockSpec((None, tf, FF_SHARD), lambda j, i, k: (j, i, 0)),
                 out_shape=_sds((N_DEV, t, FF_SHARD), BF16), name=tag + "_gu")
    gu = gu.reshape(2, 4, t, FF_SHARD)
    y = _matmul(gu, wd4, dims="nn", grid=(nt, 1, 4), a_fn=_swiglu_pair, chunks=4,
                a_spec=pl.BlockSpec((2, None, TMM, FF_SHARD), lambda i, j, k: (0, k, i, 0)),
                b_spec=pl.BlockSpec((None, FF_SHARD, D), lambda i, j, k: (k, 0, 0)),
                o_spec=pl.BlockSpec((TMM, D), lambda i, j, k: (i, 0)),
                out_shape=_sds((t, D), F32), name=tag + "_down", scale=0.5,
                res=x, res_spec=pl.BlockSpec((TMM, D), lambda i, j, k: (i, 0)))
    return y, (x, n, gu)


def _ffn_bwd(d, saved, gain, wgu8, wd4, tag):
    x, n, gu = saved
    t = x.shape[0]
    nt = t // TMM
    dgu = _matmul(d, wd4, dims="nt", grid=(4, nt, 1),
                  a_spec=pl.BlockSpec((TMM, D), lambda j, i, k: (i, 0)),
                  b_spec=pl.BlockSpec((None, FF_SHARD, D), lambda j, i, k: (j, 0, 0)),
                  o_spec=pl.BlockSpec((2, None, TMM, FF_SHARD), lambda j, i, k: (0, j, i, 0)),
                  out_shape=_sds((2, 4, t, FF_SHARD), BF16), name=tag + "_dgu", scale=0.5,
                  res=gu, res_spec=pl.BlockSpec((2, None, TMM, FF_SHARD), lambda j, i, k: (0, j, i, 0)),
                  epilogue=_swiglu_bwd, acc_shape=(TMM, FF_SHARD), chunks=4)
    tf = min(TMF, t)
    nf = t // tf
    dwd4 = _matmul(gu, d, dims="tn", grid=(4, 1, nf), a_fn=_swiglu_pair, chunks=4,
                   a_spec=pl.BlockSpec((2, None, tf, FF_SHARD), lambda j, i, k: (0, j, k, 0)),
                   b_spec=pl.BlockSpec((tf, D), lambda j, i, k: (k, 0)),
                   o_spec=pl.BlockSpec((None, FF_SHARD, D), lambda j, i, k: (j, 0, 0)),
                   out_shape=_sds((4, FF_SHARD, D), F32), name=tag + "_dwd", scale=0.5)
    dgu = dgu.reshape(N_DEV, t, FF_SHARD)
    dwgu8 = _matmul(n, dgu, dims="tn", grid=(N_DEV, 1, nf),
                    a_spec=pl.BlockSpec((tf, D), lambda j, i, k: (k, 0)),
                    b_spec=pl.BlockSpec((None, tf, FF_SHARD), lambda j, i, k: (j, k, 0)),
                    o_spec=pl.BlockSpec((None, D, FF_SHARD), lambda j, i, k: (j, 0, 0)),
                    out_shape=_sds((N_DEV, D, FF_SHARD), F32), name=tag + "_dwgu")
    dn = _matmul(dgu, wgu8, dims="nt", grid=(nf, 1, N_DEV),
                 a_spec=pl.BlockSpec((None, tf, FF_SHARD), lambda i, j, k: (k, i, 0)),
                 b_spec=pl.BlockSpec((None, D, FF_SHARD), lambda i, j, k: (k, 0, 0)),
                 o_spec=pl.BlockSpec((tf, D), lambda i, j, k: (i, 0)),
                 out_shape=_sds((t, D), BF16), name=tag + "_dn")
    dx, dgain = _rmsnorm_bwd(x, gain, dn, d, tag + "_dnorm")
    return dx, dgain, dwgu8, dwd4


CONV_TC = 768
CONV_K = 4
CONV_TM = 1024


def _conv_fwd(ha, w, name):
    t = ha.shape[0]
    nb = CONV_TM // 8

    def body(prev_ref, cur_ref, w_ref, o_ref):
        i = pl.program_id(0)
        cur = cur_ref[...].astype(F32)
        prev = prev_ref[...].astype(F32) * (i > 0).astype(F32)
        ext = jnp.concatenate([prev, cur], axis=0)
        wv = w_ref[...]
        acc = cur * wv[3:4]
        for k in range(1, CONV_K):
            acc = acc + pltpu.roll(ext, k, axis=0)[8:] * wv[3 - k:4 - k]
        o_ref[...] = acc.astype(o_ref.dtype)

    return pl.pallas_call(
        body, out_shape=_sds((3, t, CONV_TC), BF16), grid=(t // CONV_TM, 3),
        in_specs=[pl.BlockSpec((8, CONV_TC), lambda i, c: (jnp.maximum(i * nb - 1, 0), c)),
                  pl.BlockSpec((CONV_TM, CONV_TC), lambda i, c: (i, c)),
                  pl.BlockSpec((CONV_K, CONV_TC), lambda i, c: (0, c))],
        out_specs=pl.BlockSpec((None, CONV_TM, CONV_TC), lambda i, c: (c, i, 0)),
        name=name, compiler_params=_cparams(("parallel", "parallel")))(ha, ha, w)


def _conv_bwd(ha, w, dy3, name):
    t = ha.shape[0]
    nb = CONV_TM // 8
    nt = t // CONV_TM

    def body(prev_ref, cur_ref, dy_ref, nxt_ref, w_ref, dx_ref, dw_ref):
        i = pl.program_id(1)
        cur = cur_ref[...].astype(F32)
        prev = prev_ref[...].astype(F32) * (i > 0).astype(F32)
        ext = jnp.concatenate([prev, cur], axis=0)
        dy = dy_ref[...].astype(F32)
        nxt = nxt_ref[...].astype(F32) * (i < nt - 1).astype(F32)
        dext = jnp.concatenate([dy, nxt], axis=0)
        wv = w_ref[...]
        dx = dy * wv[3:4]
        dws = [None] * CONV_K
        dws[3] = jnp.sum(dy * cur, axis=0, keepdims=True)
        for k in range(1, CONV_K):
            dx = dx + pltpu.roll(dext, CONV_TM + 8 - k, axis=0)[:CONV_TM] * wv[3 - k:4 - k]
            dws[3 - k] = jnp.sum(dy * pltpu.roll(ext, k, axis=0)[8:], axis=0, keepdims=True)
        dx_ref[...] = dx.astype(dx_ref.dtype)

        @pl.when(i == 0)
        def _():
            dw_ref[...] = jnp.zeros_like(dw_ref)

        dw_ref[...] += jnp.concatenate(dws, axis=0)

    return pl.pallas_call(
        body, out_shape=(_sds((t, 3 * CONV_TC), BF16), _sds((CONV_K, 3 * CONV_TC), F32)), grid=(3, nt),
        in_specs=[pl.BlockSpec((8, CONV_TC), lambda c, i: (jnp.maximum(i * nb - 1, 0), c)),
                  pl.BlockSpec((CONV_TM, CONV_TC), lambda c, i: (i, c)),
                  pl.BlockSpec((None, CONV_TM, CONV_TC), lambda c, i: (c, i, 0)),
                  pl.BlockSpec((None, 8, CONV_TC), lambda c, i: (c, jnp.minimum((i + 1) * nb, t // 8 - 1), 0)),
                  pl.BlockSpec((CONV_K, CONV_TC), lambda c, i: (0, c))],
        out_specs=(pl.BlockSpec((CONV_TM, CONV_TC), lambda c, i: (i, c)),
                   pl.BlockSpec((CONV_K, CONV_TC), lambda c, i: (0, c))),
        name=name, compiler_params=_cparams(("parallel", "arbitrary")))(ha, ha, dy3, dy3, w)


def _gdn_specs(t, rev):
    ng = t // GROUP

    def gi(g):
        return ng - 1 - g if rev else g

    qkv = pl.BlockSpec((3, GROUP, GDN_HPS * HD), lambda h, g: (0, gi(g), h))
    bg = pl.BlockSpec((GROUP, 128), lambda h, g: (gi(g), 0))
    dbg = pl.BlockSpec((GDN_HPS, GROUP, 128), lambda h, g: (h, gi(g), 0))
    o = pl.BlockSpec((GROUP, GDN_HPS * HD), lambda h, g: (gi(g), h))
    st = pl.BlockSpec((GDN_HPS, None, HD, HD), lambda h, g: (h, gi(g), 0, 0))
    inv = pl.BlockSpec((GDN_HPS, None, GROUP, GROUP), lambda h, g: (h, gi(g), 0, 0))
    return qkv, bg, dbg, o, st, inv


def _head_qkv(qkv_ref, j):
    sl = slice(j * HD, (j + 1) * HD)
    return qkv_ref[0, :, sl].astype(F32), qkv_ref[1, :, sl].astype(F32), qkv_ref[2, :, sl].astype(F32)


def _head_gates(bg, h):
    lane = lax.broadcasted_iota(jnp.int32, (1, 128), 1)
    beta = jnp.sum(jnp.where(lane == h, bg, 0.0), axis=1, keepdims=True)
    gcol = jnp.sum(jnp.where(lane == h + A_HEADS, bg, 0.0), axis=1, keepdims=True)
    return beta, gcol, _col_to_row(gcol)


def _gdn_fwd(qkv3, bg, name):
    t = qkv3.shape[1]
    ng = t // GROUP
    qkv_s, bg_s, _, o_s, st_s, inv_s = _gdn_specs(t, False)

    def body(qkv_ref, bg_ref, o_ref, st_ref, inv_ref, s_scr):
        @pl.when(pl.program_id(1) == 0)
        def _():
            s_scr[...] = jnp.zeros_like(s_scr)

        st_ref[...] = s_scr[...]
        bgv = bg_ref[...]
        loc = [_gdn_local(*_head_qkv(qkv_ref, j), *_head_gates(bgv, pl.program_id(0) * GDN_HPS + j))
               for j in range(GDN_HPS)]
        s = [s_scr[j] for j in range(GDN_HPS)]
        for a in range(GROUP // CHUNK):
            sl = slice(a * CHUNK, (a + 1) * CHUNK)
            for j in range(GDN_HPS):
                u, w, qd, kd, qkc, dm, _ = loc[j]
                s[j], out = _gdn_step(s[j], w[sl], u[sl], qd[sl], kd[sl], qkc[sl], dm[sl])
                o_ref[sl, j * HD:(j + 1) * HD] = out.astype(o_ref.dtype)
        for j in range(GDN_HPS):
            s_scr[j] = s[j]
            inv_ref[j] = loc[j][6].astype(inv_ref.dtype)

    return pl.pallas_call(
        body, out_shape=(_sds((t, A_WIDTH), BF16), _sds((A_HEADS, ng, HD, HD), F32),
                         _sds((A_HEADS, ng, GROUP, GROUP), BF16)),
        grid=(A_HEADS // GDN_HPS, ng), in_specs=[qkv_s, bg_s], out_specs=(o_s, st_s, inv_s),
        scratch_shapes=[pltpu.VMEM((GDN_HPS, HD, HD), F32)], name=name,
        compiler_params=_cparams(("parallel", "arbitrary")))(qkv3, bg)


def _gdn_bwd(qkv3, bg, states, pinvs, do, name):
    t = qkv3.shape[1]
    ng = t // GROUP
    qkv_s, bg_s, dbg_s, o_s, st_s, inv_s = _gdn_specs(t, True)
    nc = GROUP // CHUNK

    def body(qkv_ref, bg_ref, st_ref, inv_ref, do_ref, dqkv_ref, dbg_ref, ds_scr):
        @pl.when(pl.program_id(1) == 0)
        def _():
            ds_scr[...] = jnp.zeros_like(ds_scr)

        heads = range(GDN_HPS)
        bgv = bg_ref[...]
        hid = [pl.program_id(0) * GDN_HPS + j for j in heads]
        fw = [jax.vjp(functools.partial(_gdn_local, pinv_kept=inv_ref[j].astype(F32)),
                      *_head_qkv(qkv_ref, j), *_head_gates(bgv, hid[j])) for j in heads]
        starts = [[None] * nc for _ in heads]
        s = [st_ref[j] for j in heads]
        for a in range(nc):
            sl = slice(a * CHUNK, (a + 1) * CHUNK)
            for j in heads:
                u, w, qd, kd, qkc, dm = fw[j][0]
                starts[j][a] = s[j]
                if a < nc - 1:
                    s[j], _ = _gdn_step(s[j], w[sl], u[sl], qd[sl], kd[sl], qkc[sl], dm[sl])
        ds = [ds_scr[j] for j in heads]
        parts = [[None] * nc for _ in heads]
        for a in reversed(range(nc)):
            sl = slice(a * CHUNK, (a + 1) * CHUNK)
            for j in heads:
                u, w, qd, kd, qkc, dm = fw[j][0]
                _, vjp_step = jax.vjp(_gdn_step, starts[j][a], w[sl], u[sl], qd[sl], kd[sl], qkc[sl], dm[sl])
                grads = vjp_step((ds[j], do_ref[sl, j * HD:(j + 1) * HD].astype(F32)))
                ds[j] = grads[0]
                parts[j][a] = grads[1:]
        lane = lax.broadcasted_iota(jnp.int32, (1, 128), 1)
        for j in heads:
            ds_scr[j] = ds[j]
            dw, du, dqd, dkd, dqk, ddm = [jnp.concatenate([parts[j][a][i] for a in range(nc)], axis=0)
                                          for i in range(6)]
            dq, dk, dv, db, dgc, dgr = fw[j][1]((du, dw, dqd, dkd, dqk, ddm))
            hs = slice(j * HD, (j + 1) * HD)
            dqkv_ref[0, :, hs] = dq.astype(dqkv_ref.dtype)
            dqkv_ref[1, :, hs] = dk.astype(dqkv_ref.dtype)
            dqkv_ref[2, :, hs] = dv.astype(dqkv_ref.dtype)
            dbg_ref[j] = (jnp.where(lane == hid[j], db, 0.0)
                          + jnp.where(lane == hid[j] + A_HEADS, dgc + _row_to_col(dgr), 0.0))

    return pl.pallas_call(
        body, out_shape=(_sds((3, t, A_WIDTH), BF16), _sds((A_HEADS, t, 128), F32)),
        grid=(A_HEADS // GDN_HPS, ng), in_specs=[qkv_s, bg_s, st_s, inv_s, o_s],
        out_specs=(qkv_s, dbg_s), scratch_shapes=[pltpu.VMEM((GDN_HPS, HD, HD), F32)], name=name,
        compiler_params=_cparams(("parallel", "arbitrary")))(qkv3, bg, states, pinvs, do)


NEG = -1e30


def _diag_mask(shape, q_axis):
    qi = lax.shift_right_logical(lax.broadcasted_iota(jnp.int32, shape, q_axis), 6)
    ki = lax.shift_right_logical(lax.broadcasted_iota(jnp.int32, shape, 1 - q_axis), 6)
    return ki <= qi


def _col_to_row(col):
    n = col.shape[0]
    eye = lax.broadcasted_iota(jnp.int32, (n, n), 0) == lax.broadcasted_iota(jnp.int32, (n, n), 1)
    return jnp.sum(jnp.where(eye, col, 0.0), axis=0, keepdims=True)


def _row_to_col(row):
    n = row.shape[1]
    eye = lax.broadcasted_iota(jnp.int32, (n, n), 0) == lax.broadcasted_iota(jnp.int32, (n, n), 1)
    return jnp.sum(jnp.where(eye, row, 0.0), axis=1, keepdims=True)


def _blk(ref, i):
    return ref[pl.ds(pl.multiple_of(i * ATT_BQ, ATT_BQ), ATT_BQ), :]


def _att_fwd(qc, kc, v, name):
    t = qc.shape[0]
    nq = t // ATT_BQ

    def body(q_ref, k_ref, v_ref, o_ref, lse_ref, lser_ref, m_scr, l_scr, acc_scr):
        qb = pl.program_id(1)
        q = q_ref[...]
        m_scr[...] = jnp.full_like(m_scr, NEG)
        l_scr[...] = jnp.zeros_like(l_scr)
        acc_scr[...] = jnp.zeros_like(acc_scr)

        def step(kb, diag):
            s = _dot(q, _blk(k_ref, kb), "nt")
            if diag:
                s = jnp.where(_diag_mask(s.shape, 0), s, NEG)
            m_old = m_scr[...]
            m_new = jnp.maximum(m_old, jnp.max(s, axis=1, keepdims=True))
            alpha = jnp.exp2(m_old - m_new)
            p = jnp.exp2(s - m_new)
            l_scr[...] = alpha * l_scr[...] + jnp.sum(p, axis=1, keepdims=True)
            acc_scr[...] = alpha * acc_scr[...] + _dot(p, _blk(v_ref, kb))
            m_scr[...] = m_new

        def loop_body(kb, carry):
            step(kb, False)
            return carry

        lax.fori_loop(0, qb, loop_body, 0)
        step(qb, True)
        o_ref[...] = (acc_scr[...] / l_scr[...]).astype(o_ref.dtype)
        lse = m_scr[...] + jnp.log2(l_scr[...])
        lse_ref[...] = lse
        lser_ref[...] = _col_to_row(lse)

    return pl.pallas_call(
        body, out_shape=(_sds((t, B_HEADS * HD), BF16), _sds((B_HEADS, t, 1), F32),
                         _sds((B_HEADS, nq, 1, ATT_BQ), F32)), grid=(B_HEADS, nq),
        in_specs=[pl.BlockSpec((ATT_BQ, QK_CAT), lambda h, i: (i, h)),
                  pl.BlockSpec((t, QK_CAT), lambda h, i: (0, h)), pl.BlockSpec((t, HD), lambda h, i: (0, h))],
        out_specs=(pl.BlockSpec((ATT_BQ, HD), lambda h, i: (i, h)),
                   pl.BlockSpec((None, ATT_BQ, 1), lambda h, i: (h, i, 0)),
                   pl.BlockSpec((None, None, 1, ATT_BQ), lambda h, i: (h, i, 0, 0))),
        scratch_shapes=[pltpu.VMEM((ATT_BQ, 1), F32), pltpu.VMEM((ATT_BQ, 1), F32), pltpu.VMEM((ATT_BQ, HD), F32)],
        name=name, compiler_params=_cparams(("parallel", "arbitrary")))(qc, kc, v)


def _att_bwd(qc, kc, v, o, lse, lse_row, do, name):
    t = qc.shape[0]
    nq = t // ATT_BQ

    def delta_fn(ob, dob):
        dl = jnp.sum(ob.astype(F32) * dob.astype(F32), axis=1, keepdims=True)
        return dl, _col_to_row(dl)

    delta, delta_row = _rowcall(
        delta_fn, [o, do], [pl.BlockSpec((ATT_BQ, HD), lambda i, h: (i, h))] * 2,
        [_sds((B_HEADS, t, 1), F32), _sds((B_HEADS, nq, 1, ATT_BQ), F32)],
        [pl.BlockSpec((None, ATT_BQ, 1), lambda i, h: (h, i, 0)),
         pl.BlockSpec((None, None, 1, ATT_BQ), lambda i, h: (h, i, 0, 0))], (nq, B_HEADS), name + "_delta")

    def dq_body(q_ref, k_ref, v_ref, do_ref, lse_ref, dl_ref, dq_ref, acc):
        qb = pl.program_id(1)
        q, dob, lse_b, dl_b = q_ref[...], do_ref[...], lse_ref[...], dl_ref[...]
        acc[...] = jnp.zeros_like(acc)

        def step(kb, diag):
            k = _blk(k_ref, kb)
            s = _dot(q, k, "nt")
            if diag:
                s = jnp.where(_diag_mask(s.shape, 0), s, NEG)
            p = jnp.exp2(s - lse_b)
            ds = p * (_dot(dob, _blk(v_ref, kb), "nt") - dl_b)
            acc[...] += _dot(ds, k)

        def loop_body(kb, carry):
            step(kb, False)
            return carry

        lax.fori_loop(0, qb, loop_body, 0)
        step(qb, True)
        dq_ref[...] = (acc[...] * ATT_SCALE).astype(dq_ref.dtype)

    qmap = lambda h, i: (i, h)
    colq = pl.BlockSpec((None, ATT_BQ, 1), lambda h, i: (h, i, 0))
    dq = pl.pallas_call(
        dq_body, out_shape=_sds((t, B_HEADS * QK_CAT), BF16), grid=(B_HEADS, nq),
        in_specs=[pl.BlockSpec((ATT_BQ, QK_CAT), qmap), pl.BlockSpec((t, QK_CAT), lambda h, i: (0, h)),
                  pl.BlockSpec((t, HD), lambda h, i: (0, h)), pl.BlockSpec((ATT_BQ, HD), qmap), colq, colq],
        out_specs=pl.BlockSpec((ATT_BQ, QK_CAT), qmap),
        scratch_shapes=[pltpu.VMEM((ATT_BQ, QK_CAT), F32)], name=name + "_dq",
        compiler_params=_cparams(("parallel", "arbitrary")))(qc, kc, v, do, lse, delta)

    def dkv_body(k_ref, v_ref, q_ref, do_ref, lser_ref, dlr_ref, dk_ref, dv_ref, dk_acc, dv_acc):
        kb = pl.program_id(1)
        k, vv = k_ref[...], v_ref[...]
        dk_acc[...] = jnp.zeros_like(dk_acc)
        dv_acc[...] = jnp.zeros_like(dv_acc)

        def step(qb, diag):
            q, dob = _blk(q_ref, qb), _blk(do_ref, qb)
            st = _dot(k, q, "nt")
            if diag:
                st = jnp.where(_diag_mask(st.shape, 1), st, NEG)
            pt = jnp.exp2(st - lser_ref[qb])
            dst = pt * (_dot(vv, dob, "nt") - dlr_ref[qb])
            dv_acc[...] += _dot(pt, dob)
            dk_acc[...] += _dot(dst, q)

        def loop_body(qb, carry):
            step(qb, False)
            return carry

        step(kb, True)
        lax.fori_loop(kb + 1, nq, loop_body, 0)
        dk_ref[...] = (dk_acc[...] * LN2).astype(dk_ref.dtype)
        dv_ref[...] = dv_acc[...].astype(dv_ref.dtype)

    kmap = lambda h, j: (j, h)
    rowq = pl.BlockSpec((None, nq, 1, ATT_BQ), lambda h, j: (h, 0, 0, 0))
    dk, dv = pl.pallas_call(
        dkv_body, out_shape=(_sds((t, B_HEADS * QK_CAT), BF16), _sds((t, B_HEADS * HD), BF16)),
        grid=(B_HEADS, nq),
        in_specs=[pl.BlockSpec((ATT_BQ, QK_CAT), kmap), pl.BlockSpec((ATT_BQ, HD), kmap),
                  pl.BlockSpec((t, QK_CAT), lambda h, j: (0, h)), pl.BlockSpec((t, HD), lambda h, j: (0, h)),
                  rowq, rowq],
        out_specs=(pl.BlockSpec((ATT_BQ, QK_CAT), kmap), pl.BlockSpec((ATT_BQ, HD), kmap)),
        scratch_shapes=[pltpu.VMEM((ATT_BQ, QK_CAT), F32), pltpu.VMEM((ATT_BQ, HD), F32)], name=name + "_dkv",
        compiler_params=_cparams(("parallel", "arbitrary")))(kc, v, qc, do, lse_row, delta_row)
    return dq, dk, dv


def _mem_fwd(hx, col, mkv, name):
    t = hx.shape[0]
    (o,) = _rowcall(_memattn, [hx, mkv, mkv],
                    [_rows(TM, MEM_W, col), pl.BlockSpec((N_MEM, MEM_W), lambda i: (0, 0)),
                     pl.BlockSpec((N_MEM, MEM_W), lambda i: (0, 1))],
                    [_sds((t, MEM_W), BF16)], [_rows(TM, MEM_W)], (t // TM,), name)
    return o


def _mem_bwd(hx, col, mkv, do, do_col, name):
    t = hx.shape[0]
    dq, dk, dv = _rowcall(_vjp_fn(_memattn, 3, (0, 1, 2)), [hx, mkv, mkv, do],
                          [_rows(TM, MEM_W, col), pl.BlockSpec((N_MEM, MEM_W), lambda i: (0, 0)),
                           pl.BlockSpec((N_MEM, MEM_W), lambda i: (0, 1)), _rows(TM, MEM_W, do_col)],
                          [_sds((t, MEM_W), BF16), _sds((N_MEM, MEM_W), F32), _sds((N_MEM, MEM_W), F32)],
                          [_rows(TM, MEM_W), _shared((N_MEM, MEM_W)), _shared((N_MEM, MEM_W))],
                          (t // TM,), name, n_acc=2)
    return dq, jnp.concatenate([dk, dv], axis=1)


def _a_in_ext(w):
    nb = 4 * A_WIDTH
    ba = jnp.pad(w[:, nb:nb + 2 * A_HEADS], ((0, 0), (0, 128 - 2 * A_HEADS)))
    return jnp.concatenate([w[:, :nb], w[:, nb + 2 * A_HEADS:], ba], axis=1)


def _swap_halves(w):
    return jnp.concatenate([w[..., QK_ROPE // 2:], w[..., :QK_ROPE // 2]], axis=-1)


def _uq_ext(w):
    w = w.reshape(Q_LORA, B_HEADS, QK_NOPE + QK_ROPE)
    nope, rope = w[..., :QK_NOPE], w[..., QK_NOPE:]
    z64 = jnp.zeros((Q_LORA, B_HEADS, QK_CAT - QK_NOPE - QK_ROPE), w.dtype)
    z128 = jnp.zeros((Q_LORA, B_HEADS, QK_NOPE), w.dtype)
    a = jnp.concatenate([nope, rope, z64], axis=-1).reshape(Q_LORA, B_HEADS * QK_CAT)
    b = jnp.concatenate([z128, _swap_halves(rope), z64], axis=-1).reshape(Q_LORA, B_HEADS * QK_CAT)
    return jnp.concatenate([a, b], axis=1)


def _dkv_ext(w):
    ckv, kr = w[:, :KV_LORA], w[:, KV_LORA:]
    z128 = jnp.zeros((D, QK_NOPE), w.dtype)
    z64 = jnp.zeros((D, QK_CAT - QK_NOPE - QK_ROPE), w.dtype)
    return jnp.concatenate([ckv, z128, kr, z64, z128, _swap_halves(kr), z64], axis=1)


def _ukv_ext(w):
    w = w.reshape(KV_LORA, B_HEADS, QK_NOPE + HD)
    kn, vv = w[..., :QK_NOPE], w[..., QK_NOPE:]
    z = jnp.zeros((KV_LORA, B_HEADS, QK_CAT - QK_NOPE), w.dtype)
    a = jnp.concatenate([kn, z], axis=-1).reshape(KV_LORA, B_HEADS * QK_CAT)
    return jnp.concatenate([a, vv.reshape(KV_LORA, B_HEADS * HD)], axis=1)


def _ext_and_back(fn, w):
    ext, back = jax.vjp(fn, w.astype(F32))
    return ext.astype(BF16), lambda g: back(g.astype(F32))[0]


def _rope_tables(pos_col):
    t = pos_col.shape[0]
    inv = (ROPE_THETA ** (-np.arange(0, QK_ROPE, 2, dtype=np.float32) / QK_ROPE)).astype(np.float32)
    inv_row = np.zeros((1, QK_CAT), np.float32)
    inv_row[0, QK_NOPE:QK_NOPE + QK_ROPE] = np.concatenate([inv, inv])
    sign = np.zeros((1, QK_CAT), np.float32)
    sign[0, QK_NOPE:QK_NOPE + QK_ROPE // 2] = -1.0
    sign[0, QK_NOPE + QK_ROPE // 2:QK_NOPE + QK_ROPE] = 1.0
    is_rope = np.abs(sign)
    is_nope = np.zeros((1, QK_CAT), np.float32)
    is_nope[0, :QK_NOPE] = 1.0

    def fn(p, inv_b, sign_b, rope_b, nope_b):
        ang = p.astype(F32) * inv_b
        return jnp.cos(ang) * rope_b + nope_b, jnp.sin(ang) * sign_b

    consts = [jnp.asarray(a) for a in (inv_row, sign, is_rope, is_nope)]
    return _rowcall(fn, [pos_col] + consts, [_rows(TM, 1)] + [_shared((1, QK_CAT))] * 4,
                    [_sds((t, QK_CAT), F32)] * 2, [_rows(TM, QK_CAT)] * 2, (t // TM,), "rope_tables")


def _local_step(x, mem, pos, target, w, layer_start, mixer_start, mixer_done, layer_done):
    t = x.shape[0]
    g = {}
    th = min(TMH, t)
    head6 = (t // th, A_HEADS)

    mem_n = _rmsnorm_fwd_small(mem, w["mem_norm"])
    rope_c, rope_s = _rope_tables(pos.reshape(t, 1))
    mkv = None

    saved = []
    for l in range(DEPTH):
        sv = {}
        layer_start(l, x)
        x, sv["ffn1"] = _ffn_fwd(x, w["ffn1_norm"][l], w["ffn1_w_gu"][l], w["ffn1_w_down"][l], "ffn1")
        sv["x1"] = x
        mixer_start(l, x)
        if mkv is None:
            mkv = [_mm(mem_n, w["w_mem_kv"][i], "nn", BF16, f"mkv{i}") for i in range(DEPTH)]
        n2 = _rmsnorm_fwd(x, w["mix_norm"][l], "mix_norm")
        sv["n2"] = n2
        if l < N_A:
            ha = _mm(n2, w["a_w_in"][l], "nn", BF16, "a_in")
            yc3 = _conv_fwd(ha, w["a_conv"][l], "a_conv")
            blk3 = pl.BlockSpec((3, th, HD), lambda i, h: (0, i, h))
            (qkv3,) = _rowcall(lambda b: jnp.stack(_gdn_prep(b[0].astype(F32), b[1].astype(F32), b[2].astype(F32))),
                               [yc3], [blk3], [_sds((3, t, A_WIDTH), BF16)], [blk3], head6, "a_prep")
            (bg,) = _rowcall(_gates, [ha, _pad128(w["a_A_log"][l], A_HEADS), _pad128(w["a_dt_bias"][l], A_HEADS)],
                             [_rows(TM, 128, A_BA_BLK), _shared((1, 128)), _shared((1, 128))],
                             [_sds((t, 128), F32)], [_rows(TM, 128)], (t // TM,), "a_gates")
            o_gdn, states, pinvs = _gdn_fwd(qkv3, bg, "a_gdn")
            (o_a,) = _rowcall(_outnorm_gate, [o_gdn, ha, w["a_out_norm"][l].reshape(1, HD)],
                              [pl.BlockSpec((th, HD), lambda i, h: (i, h)),
                               pl.BlockSpec((th, HD), lambda i, h: (i, 3 * A_HEADS + h)), _shared((1, HD))],
                              [_sds((t, A_WIDTH), BF16)], [pl.BlockSpec((th, HD), lambda i, h: (i, h))],
                              head6, "a_outnorm")
            o_m = _mem_fwd(ha, A_MQ_BLK, mkv[l], "mem_attn_a")
            sv.update(ha=ha, yc3=yc3, qkv3=qkv3, bg=bg, states=states, pinvs=pinvs, o_gdn=o_gdn)
            cat = jnp.concatenate([o_a, o_m], axis=1)
        else:
            j = l - N_A
            hb = _mm(n2, w["b_w_in"][j], "nn", BF16, "b_in")
            cqn = _rmsnorm_fwd(hb, w["b_q_norm"][j], "b_qnorm", 0, Q_LORA)
            qq = _mm(cqn, w["b_w_uq"][j], "nn", BF16, "b_uq")
            (qc,) = _rowcall(_rope_mix, [qq, qq, rope_c, rope_s],
                             [pl.BlockSpec((th, QK_CAT), lambda i, h: (i, h)),
                              pl.BlockSpec((th, QK_CAT), lambda i, h: (i, B_HEADS + h)),
                              pl.BlockSpec((th, QK_CAT), lambda i, h: (i, 0)),
                              pl.BlockSpec((th, QK_CAT), lambda i, h: (i, 0))],
                             [_sds((t, B_HEADS * QK_CAT), BF16)], [pl.BlockSpec((th, QK_CAT), lambda i, h: (i, h))],
                             head6, "b_qrope")
            o_b, lse, lse_row = _att_fwd(qc, kcat, vmla, "b_attn")
            o_m = _mem_fwd(hb, 1, mkv[l], "mem_attn_b")
            sv.update(hb=hb, cqn=cqn, qc=qc, o_b=o_b, lse=(lse, lse_row))
            cat = jnp.concatenate([o_b, o_m], axis=1)
        sv["cat"] = cat
        x = _mm(cat, w["w_out"][l], "nn", F32, "w_out", res=x)
        x, sv["ffn2"] = _ffn_fwd(x, w["ffn2_norm"][l], w["ffn2_w_gu"][l], w["ffn2_w_down"][l], "ffn2")
        saved.append(sv)
        if l == N_A - 1:
            x_kv = x
            nkv = _rmsnorm_fwd(x, w["kv_in_norm"], "kv_in_norm")
            ckr = _mm(nkv, w["w_dkv"], "nn", BF16, "kv_down")
            ckv_n = _rmsnorm_fwd(ckr, w["kv_lat_norm"], "kv_lat_norm", 0, KV_LORA)
            kvu = _mm(ckv_n, w["w_ukv"], "nn", BF16, "kv_up")
            vmla = kvu[:, B_HEADS * QK_CAT:]
            (kcat,) = _rowcall(_kcat, [kvu, ckr, ckr, rope_c, rope_s],
                               [pl.BlockSpec((th, QK_CAT), lambda i, h: (i, h)),
                                pl.BlockSpec((th, QK_CAT), lambda i, h: (i, 1)),
                                pl.BlockSpec((th, QK_CAT), lambda i, h: (i, 2)),
                                pl.BlockSpec((th, QK_CAT), lambda i, h: (i, 0)),
                                pl.BlockSpec((th, QK_CAT), lambda i, h: (i, 0))],
                               [_sds((t, B_HEADS * QK_CAT), BF16)],
                               [pl.BlockSpec((th, QK_CAT), lambda i, h: (i, h))], head6, "kv_cat")

    def loss_fn(xb, gb, tb):
        def f(xx, gg):
            e = _rms(xx, gg) - tb
            return 0.5 * jnp.sum(jnp.mean(e * e, axis=-1, keepdims=True), axis=0, keepdims=True)
        val, vjp = jax.vjp(f, xb, gb)
        dx, dg = vjp(jnp.ones((1, 1), F32))
        return dx, dg, val * jnp.ones((1, 128), F32)

    d, dfin, loss = _rowcall(loss_fn, [x, w["final_norm"].reshape(1, D), target],
                             [_rows(TM, D), _shared((1, D)), _rows(TM, D)],
                             [_sds((t, D), F32), _sds((1, D), F32), _sds((1, 128), F32)],
                             [_rows(TM, D), _shared((1, D)), _shared((1, 128))], (t // TM,), "loss_head", n_acc=2)
    g["final_norm"] = dfin[0]
    loss = loss[0, 0]

    for name in ("ffn1_norm", "ffn1_w_gu", "ffn1_w_down", "mix_norm", "ffn2_norm", "ffn2_w_gu", "ffn2_w_down",
                 "w_out", "w_mem_kv"):
        g[name] = [None] * DEPTH
    for name in ("a_w_in", "a_conv", "a_A_log", "a_dt_bias", "a_out_norm"):
        g[name] = [None] * N_A
    for name in ("b_w_in", "b_q_norm", "b_w_uq"):
        g[name] = [None] * N_B
    dmkv = [None] * DEPTH
    dkcat = []
    dvmla = []

    for l in reversed(range(DEPTH)):
        sv = saved[l]
        if l == N_A - 1:
            kq = pl.BlockSpec((th, QK_CAT), lambda i, h: (i, h))
            tab = pl.BlockSpec((th, QK_CAT), lambda i, h: (i, 0))

            def dk_fn(c, s, d0, d1):
                dk = d0.astype(F32) + d1.astype(F32)
                return dk, dk * c, dk * s

            dkn, dkr_h, dkrs_h = _rowcall(dk_fn, [rope_c, rope_s, dkcat[0], dkcat[1]], [tab, tab, kq, kq],
                                          [_sds((t, B_HEADS * QK_CAT), BF16)] + [_sds((B_HEADS, t, QK_CAT), BF16)] * 2,
                                          [kq] + [pl.BlockSpec((None, th, QK_CAT), lambda i, h: (h, i, 0))] * 2,
                                          head6, "kv_dcat")

            def sum6(a, b):
                return jnp.sum(a.astype(F32), axis=0), jnp.sum(b.astype(F32), axis=0)

            h6 = pl.BlockSpec((B_HEADS, TM, QK_CAT), lambda i: (0, i, 0))
            dkr, dkrs = _rowcall(sum6, [dkr_h, dkrs_h], [h6, h6], [_sds((t, QK_CAT), BF16)] * 2,
                                 [_rows(TM, QK_CAT)] * 2, (t // TM,), "kv_dkr")

            def addv(a, b):
                return a.astype(F32) + b.astype(F32)

            (dv,) = _rowcall(addv, dvmla, [_rows(TM, B_HEADS * HD)] * 2, [_sds((t, B_HEADS * HD), BF16)],
                             [_rows(TM, B_HEADS * HD)], (t // TM,), "kv_dv")
            dkvu = jnp.concatenate([dkn, dv], axis=1)
            g["w_ukv"] = _mm(ckv_n, dkvu, "tn", F32, "kv_up_dw")
            dckv_n = _mm(dkvu, w["w_ukv"], "nt", BF16, "kv_up_dx")

            def lat_bwd(cb, gb, dnb):
                return _vjp_fn(_rms, 2, (0, 1))(cb, gb, dnb)

            dckv, g["kv_lat_norm"] = _rowcall(lat_bwd, [ckr, w["kv_lat_norm"].reshape(1, KV_LORA), dckv_n],
                                              [_rows(TM, KV_LORA), _shared((1, KV_LORA)), _rows(TM, KV_LORA)],
                                              [_sds((t, KV_LORA), BF16), _sds((1, KV_LORA), F32)],
                                              [_rows(TM, KV_LORA), _shared((1, KV_LORA))], (t // TM,),
                                              "kv_lat_dnorm", n_acc=1)
            g["kv_lat_norm"] = g["kv_lat_norm"][0]
            dckr = jnp.concatenate([dckv, dkr, dkrs], axis=1)
            g["w_dkv"] = _mm(nkv, dckr, "tn", F32, "kv_down_dw")
            dnkv = _mm(dckr, w["w_dkv"], "nt", BF16, "kv_down_dx")
            d, g["kv_in_norm"] = _rmsnorm_bwd(x_kv, w["kv_in_norm"], dnkv, d, "kv_in_dnorm")

        d, g["ffn2_norm"][l], g["ffn2_w_gu"][l], g["ffn2_w_down"][l] = _ffn_bwd(
            d, sv["ffn2"], w["ffn2_norm"][l], w["ffn2_w_gu"][l], w["ffn2_w_down"][l], "ffn2b")
        g["w_out"][l] = _mm(sv["cat"], d, "tn", F32, "w_out_dw")
        dcat = _mm(d, w["w_out"][l], "nt", BF16, "w_out_dx")
        if l < N_A:
            ha, yc3, qkv3, states, o_gdn = sv["ha"], sv["yc3"], sv["qkv3"], sv["states"], sv["o_gdn"]
            dmq, dmkv[l] = _mem_bwd(ha, A_MQ_BLK, mkv[l], dcat, 3, "mem_attn_a_bwd")
            hblk = pl.BlockSpec((th, HD), lambda i, h: (i, h))
            do_gdn, dgate, dgain = _rowcall(
                _vjp_fn(_outnorm_gate, 3, (0, 1, 2)), [o_gdn, ha, w["a_out_norm"][l].reshape(1, HD), dcat],
                [hblk, pl.BlockSpec((th, HD), lambda i, h: (i, 3 * A_HEADS + h)), _shared((1, HD)), hblk],
                [_sds((t, A_WIDTH), BF16), _sds((t, A_WIDTH), BF16), _sds((1, HD), F32)],
                [hblk, hblk, _shared((1, HD))], head6, "a_outnorm_bwd", n_acc=1)
            g["a_out_norm"][l] = dgain[0]
            dqkv3, dbg6 = _gdn_bwd(qkv3, sv["bg"], states, sv["pinvs"], do_gdn, "a_gdn_bwd")

            def dgates(bab, alb, dtb, d6):
                return _vjp_fn(_gates, 3, (0, 1, 2))(bab, alb, dtb, jnp.sum(d6, axis=0))

            dba, dalog, ddt = _rowcall(
                dgates, [ha, _pad128(w["a_A_log"][l], A_HEADS), _pad128(w["a_dt_bias"][l], A_HEADS), dbg6],
                [_rows(TM, 128, A_BA_BLK), _shared((1, 128)), _shared((1, 128)),
                 pl.BlockSpec((A_HEADS, TM, 128), lambda i: (0, i, 0))],
                [_sds((t, 128), BF16), _sds((1, 128), F32), _sds((1, 128), F32)],
                [_rows(TM, 128), _shared((1, 128)), _shared((1, 128))], (t // TM,), "a_gates_bwd", n_acc=2)
            g["a_A_log"][l] = dalog[0, A_HEADS:2 * A_HEADS]
            g["a_dt_bias"][l] = ddt[0, A_HEADS:2 * A_HEADS]
            blk3 = pl.BlockSpec((3, th, HD), lambda i, h: (0, i, h))
            def dprep(b, db):
                return jnp.stack(_vjp_fn(_gdn_prep, 3, (0, 1, 2))(b[0], b[1], b[2], db[0], db[1], db[2]))

            (dyc3,) = _rowcall(dprep, [yc3, dqkv3], [blk3, blk3],
                               [_sds((3, t, A_WIDTH), BF16)], [blk3], head6, "a_prep_bwd")
            dqkv_in, g["a_conv"][l] = _conv_bwd(ha, w["a_conv"][l], dyc3, "a_conv_bwd")
            dha = jnp.concatenate([dqkv_in, dgate, dmq, dba], axis=1)
            g["a_w_in"][l] = _mm(sv["n2"], dha, "tn", F32, "a_in_dw")
            dn2 = _mm(dha, w["a_w_in"][l], "nt", BF16, "a_in_dx")
        else:
            j = l - N_A
            hb, cqn, qc, o_b, lse = sv["hb"], sv["cqn"], sv["qc"], sv["o_b"], sv["lse"]
            dmq, dmkv[l] = _mem_bwd(hb, 1, mkv[l], dcat, 3, "mem_attn_b_bwd")
            dqc, dkc, dvv = _att_bwd(qc, kcat, vmla, o_b, lse[0], lse[1], dcat, "b_attn_bwd")
            dkcat.append(dkc)
            dvmla.append(dvv)
            kq = pl.BlockSpec((th, QK_CAT), lambda i, h: (i, h))
            tab = pl.BlockSpec((th, QK_CAT), lambda i, h: (i, 0))

            def dq_fn(c, s, dq):
                dq = dq.astype(F32)
                return jnp.stack([dq * c, dq * s])

            (dqq,) = _rowcall(dq_fn, [rope_c, rope_s, dqc], [tab, tab, kq],
                              [_sds((2, t, B_HEADS * QK_CAT), BF16)],
                              [pl.BlockSpec((2, th, QK_CAT), lambda i, h: (0, i, h))], head6, "b_qrope_bwd")
            dqq = jnp.concatenate([dqq[0], dqq[1]], axis=1)
            g["b_w_uq"][j] = _mm(cqn, dqq, "tn", F32, "b_uq_dw")
            dcqn = _mm(dqq, w["b_w_uq"][j], "nt", BF16, "b_uq_dx")
            dcq, dqg = _rowcall(_vjp_fn(_rms, 2, (0, 1)), [hb, w["b_q_norm"][j].reshape(1, Q_LORA), dcqn],
                                [_rows(TM, Q_LORA), _shared((1, Q_LORA)), _rows(TM, Q_LORA)],
                                [_sds((t, Q_LORA), BF16), _sds((1, Q_LORA), F32)],
                                [_rows(TM, Q_LORA), _shared((1, Q_LORA))], (t // TM,), "b_qnorm_bwd", n_acc=1)
            g["b_q_norm"][j] = dqg[0]
            dhb = jnp.concatenate([dcq, dmq], axis=1)
            g["b_w_in"][j] = _mm(sv["n2"], dhb, "tn", F32, "b_in_dw")
            dn2 = _mm(dhb, w["b_w_in"][j], "nt", BF16, "b_in_dx")
        d, g["mix_norm"][l] = _rmsnorm_bwd(sv["x1"], w["mix_norm"][l], dn2, d, "mix_dnorm")
        d = mixer_done(l, g, d)
        d, g["ffn1_norm"][l], g["ffn1_w_gu"][l], g["ffn1_w_down"][l] = _ffn_bwd(
            d, sv["ffn1"], w["ffn1_norm"][l], w["ffn1_w_gu"][l], w["ffn1_w_down"][l], "ffn1b")
        d = layer_done(l, g, d)

    dmem_n = None
    for l in range(DEPTH):
        g["w_mem_kv"][l] = _mm(mem_n, dmkv[l], "tn", F32, f"mkv_dw{l}")
        dmem_n = _mm(dmkv[l], w["w_mem_kv"][l], "nt", F32, f"mkv_dx{l}", res=dmem_n)
    (_, gmn) = _rowcall(_vjp_fn(_rms, 2, (0, 1)), [mem, w["mem_norm"].reshape(1, D), dmem_n],
                        [_shared((N_MEM, D)), _shared((1, D)), _shared((N_MEM, D))],
                        [_sds((N_MEM, D), F32), _sds((1, D), F32)], [_shared((N_MEM, D)), _shared((1, D))],
                        (1,), "mem_dnorm")
    g["mem_norm"] = gmn[0]
    return loss, d, g


def _pad128(v, offset):
    return jnp.pad(v.astype(F32).reshape(1, -1), ((0, 0), (offset, 128 - offset - v.shape[0])))


def _rmsnorm_fwd_small(x, gain):
    r, w = x.shape
    (n,) = _rowcall(_rms, [x, gain.reshape(1, w)], [_shared((r, w)), _shared((1, w))],
                    [_sds((r, w), BF16)], [_shared((r, w))], (1,), "mem_norm")
    return n


def _exchange(srcs, gather, name):
    n = len(srcs)
    blks = [tuple(s.shape) if gather else tuple(s.shape[1:]) for s in srcs]

    def body(*refs):
        src_refs, out_refs = refs[:n], refs[n:2 * n]
        send_sems, recv_sems, local_sems = refs[2 * n:]
        x, y, c = lax.axis_index("x"), lax.axis_index("y"), lax.axis_index("c")
        me = 4 * x + 2 * y + c
        copies = []
        for k in range(1, N_DEV):
            px = (x + (k >> 2 & 1)) % 2
            py = (y + (k >> 1 & 1)) % 2
            pc = (c + (k & 1)) % 2
            peer = 4 * px + 2 * py + pc
            for a in range(n):
                cp = pltpu.make_async_remote_copy(
                    src_ref=src_refs[a] if gather else src_refs[a].at[peer], dst_ref=out_refs[a].at[me],
                    send_sem=send_sems.at[a, k - 1], recv_sem=recv_sems.at[a, k - 1],
                    device_id=(px, py, pc), device_id_type=pl.DeviceIdType.MESH)
                cp.start()
                copies.append(cp)
        for a in range(n):
            cp = pltpu.make_async_copy(src_refs[a] if gather else src_refs[a].at[me], out_refs[a].at[me],
                                       local_sems.at[a])
            cp.start()
            copies.append(cp)
        for cp in copies:
            cp.wait()

    return pl.pallas_call(
        body, out_shape=tuple(_sds((N_DEV,) + b, s.dtype) for b, s in zip(blks, srcs)),
        in_specs=[pl.BlockSpec(memory_space=pl.ANY)] * n, out_specs=tuple([pl.BlockSpec(memory_space=pl.ANY)] * n),
        scratch_shapes=[pltpu.SemaphoreType.DMA((n, N_DEV - 1)), pltpu.SemaphoreType.DMA((n, N_DEV - 1)),
                        pltpu.SemaphoreType.DMA((n,))],
        name=name)(*srcs)


_HBM = pl.BlockSpec(memory_space=pltpu.HBM)
_SEM = pl.BlockSpec(memory_space=pltpu.SEMAPHORE)
_EFFECT = pltpu.SideEffectType.DATAFLOW_SIDE_EFFECTING


def _split_copies(src_refs, land_refs, send_sems, recv_sems, gather):
    x, y, c = lax.axis_index("x"), lax.axis_index("y"), lax.axis_index("c")
    me = 4 * x + 2 * y + c
    copies = []
    for k in range(1, N_DEV):
        peer = ((x + (k >> 2 & 1)) % 2, (y + (k >> 1 & 1)) % 2, (c + (k & 1)) % 2)
        for a in range(len(src_refs)):
            i = (k - 1) * len(src_refs) + a
            src = src_refs[a] if gather else src_refs[a].at[4 * peer[0] + 2 * peer[1] + peer[2]]
            copies.append(pltpu.make_async_remote_copy(
                src_ref=src, dst_ref=land_refs[a].at[me], send_sem=send_sems[i],
                recv_sem=recv_sems[i], device_id=peer, device_id_type=pl.DeviceIdType.MESH))
    return copies


def _split_start(srcs, gather, name):
    n = len(srcs)
    srcs = [pltpu.with_memory_space_constraint(s, pltpu.HBM) for s in srcs]
    lands = [pltpu.with_memory_space_constraint(
        lax.empty(((N_DEV,) + s.shape) if gather else s.shape, s.dtype), pltpu.HBM) for s in srcs]

    ns = n * (N_DEV - 1)

    def body(*refs):
        sems = refs[2 * n:2 * n + 2 * ns]
        for cp in _split_copies(refs[:n], refs[n:2 * n], sems[:ns], sems[ns:], gather):
            cp.start()
        refs[-1][...] = jnp.zeros_like(refs[-1])

    outs = pl.pallas_call(
        body, name=name,
        out_shape=(*[pltpu.SemaphoreType.DMA(())] * (2 * ns), *[pltpu.HBM(a.shape, a.dtype) for a in srcs + lands],
                   _sds((8, 128), F32)),
        in_specs=[_HBM] * (2 * n),
        out_specs=(*[_SEM] * (2 * ns), *[_HBM] * (2 * n), pl.BlockSpec(memory_space=pltpu.VMEM)),
        input_output_aliases={i: 2 * ns + i for i in range(2 * n)},
        compiler_params=pltpu.CompilerParams(has_side_effects=_EFFECT))(*srcs, *lands)
    sems, rest = list(outs[:2 * ns]), outs[2 * ns:]
    return sems[:ns], sems[ns:], list(rest[:n]), list(rest[n:2 * n]), rest[-1]


def _split_wait(flight, after, gather, name):
    send_sems, recv_sems, srcs, lands, _ = flight
    n = len(srcs)

    ns = len(send_sems)

    def body(*refs):
        sems = refs[2 * n:2 * n + 2 * ns]
        for cp in _split_copies(refs[:n], refs[n:2 * n], sems[:ns], sems[ns:], gather):
            cp.wait_send()
            cp.wait_recv()

    outs = pl.pallas_call(
        body, name=name, out_shape=tuple(pltpu.HBM(a.shape, a.dtype) for a in srcs + lands),
        in_specs=[_HBM] * (2 * n) + [_SEM] * (2 * ns) + [pl.BlockSpec(memory_space=pl.ANY)],
        out_specs=tuple([_HBM] * (2 * n)), input_output_aliases={i: i for i in range(2 * n)},
        compiler_params=pltpu.CompilerParams(has_side_effects=_EFFECT))(*srcs, *lands, *send_sems, *recv_sems, after)
    return list(outs[n:])


def _reduce_adamw(parts, wp, mp, vp, name):
    r, cols = wp.shape
    tr = _tile_rows(r, cols)
    c1 = 1.0 - ADAM_B1 ** ADAM_STEP
    c2 = 1.0 - ADAM_B2 ** ADAM_STEP

    def fn(pb, wb, mb, vb):
        gsum = pb[0].astype(F32)
        for j in range(1, N_DEV):
            gsum = gsum + pb[j].astype(F32)
        m_new = ADAM_B1 * mb + (1.0 - ADAM_B1) * gsum
        v_new = ADAM_B2 * vb + (1.0 - ADAM_B2) * (gsum * gsum)
        delta = -ADAM_LR * ((m_new / c1) / (jnp.sqrt(v_new / c2) + ADAM_EPS) + ADAM_WD * wb)
        return gsum, delta, m_new, v_new

    row = _rows(tr, cols)
    return _rowcall(fn, [parts, wp, mp, vp],
                    [pl.BlockSpec((N_DEV, tr, cols), lambda i: (0, i, 0)), row, row, row],
                    [_sds((r, cols), F32)] * 4, [row] * 4, (r // tr,), name)


def _tile_rows(r, cols):
    for t in (512, 256, 128, 64, 32, 16):
        if r % t == 0 and t * cols <= 160 * 1024:
            return t
    return r


def _pack(arrs):
    flat = jnp.concatenate([a.reshape(-1).astype(F32) for a in arrs])
    n = flat.shape[0]
    unit = PACK_W * PACK_ROWS
    tot = -(-n // unit) * unit
    return jnp.pad(flat, (0, tot - n)).reshape(tot // PACK_W, PACK_W)


def _unpack(buf, shapes):
    out, off = [], 0
    flat = buf.reshape(-1)
    for s in shapes:
        n = int(np.prod(s))
        out.append(flat[off:off + n].reshape(s))
        off += n
    return out


def _as2d(a):
    return a.reshape(-1, a.shape[-1])


_SHARDED = ["ffn1_w_gu", "ffn1_w_down", "ffn2_w_gu", "ffn2_w_down", "w_out", "w_mem_kv", "a_w_in", "a_conv",
            "b_w_in", "b_w_uq", "w_dkv", "w_ukv"]
_COL_SHARDED = {"ffn1_w_gu", "ffn2_w_gu", "a_conv", "b_w_uq", "w_ukv"}
_LAYERED = {"ffn1_w_gu": DEPTH, "ffn1_w_down": DEPTH, "ffn2_w_gu": DEPTH, "ffn2_w_down": DEPTH, "w_out": DEPTH,
            "w_mem_kv": DEPTH, "a_w_in": N_A, "a_conv": N_A, "b_w_in": N_B, "b_w_uq": N_B}
_GATHER_FIRST = [("ffn1_w_gu", 0, 1), ("ffn1_w_down", 0, 1)]
_GATHER_MIX0 = [("ffn2_w_gu", 0, 1), ("ffn2_w_down", 0, 1), ("w_out", 0, 1), ("a_w_in", 0, 1), ("a_conv", 0, 1),
                ("w_mem_kv", 0, DEPTH)]
_GATHER_REST = [("ffn1_w_gu", 1, DEPTH), ("ffn1_w_down", 1, DEPTH), ("ffn2_w_gu", 1, DEPTH),
                ("ffn2_w_down", 1, DEPTH), ("w_out", 1, DEPTH), ("a_w_in", 1, N_A), ("a_conv", 1, N_A),
                ("b_w_in", 0, N_B), ("b_w_uq", 0, N_B), ("w_dkv", None, None), ("w_ukv", None, None)]
_SCATTER_HI = [("ffn1_w_gu", 2, DEPTH), ("ffn1_w_down", 2, DEPTH), ("ffn2_w_gu", 2, DEPTH), ("ffn2_w_down", 2, DEPTH),
               ("w_out", 2, DEPTH), ("b_w_in", 0, N_B), ("b_w_uq", 0, N_B)]
_SCATTER_MID = [("ffn1_w_gu", 1, 2), ("ffn1_w_down", 1, 2), ("ffn2_w_gu", 1, 2), ("ffn2_w_down", 1, 2),
                ("w_out", 1, 2), ("a_w_in", 1, N_A), ("a_conv", 1, N_A), ("w_dkv", None, None), ("w_ukv", None, None)]
_SCATTER_MIX0 = [("ffn2_w_gu", 0, 1), ("ffn2_w_down", 0, 1), ("w_out", 0, 1), ("a_w_in", 0, 1), ("a_conv", 0, 1)]
_SCATTER_LO = [("ffn1_w_gu", 0, 1), ("ffn1_w_down", 0, 1), ("w_mem_kv", 0, DEPTH)]
_REPLICATED = ["ffn1_norm", "mix_norm", "ffn2_norm", "mem_norm", "a_A_log", "a_dt_bias", "a_out_norm", "b_q_norm",
               "kv_in_norm", "kv_lat_norm", "final_norm"]
_WEIGHTS = ["ffn1_norm", "ffn1_w_gu", "ffn1_w_down", "mix_norm", "ffn2_norm", "ffn2_w_gu", "ffn2_w_down", "w_out",
            "mem_norm", "w_mem_kv", "a_w_in", "a_conv", "a_A_log", "a_dt_bias", "a_out_norm", "b_w_in", "b_q_norm",
            "b_w_uq", "kv_in_norm", "w_dkv", "kv_lat_norm", "w_ukv", "final_norm"]


def _full_from_shards(name, sh):
    if name in ("ffn1_w_gu", "ffn2_w_gu"):
        return sh
    if name in ("ffn1_w_down", "ffn2_w_down"):
        return sh.reshape(4, FF_SHARD, D)
    if name in _COL_SHARDED:
        return jnp.moveaxis(sh, 0, -2).reshape(sh.shape[1:-1] + (N_DEV * sh.shape[-1],))
    return sh.reshape((N_DEV * sh.shape[1],) + sh.shape[2:])


def _shards_from_full(name, full):
    if name in ("ffn1_w_gu", "ffn2_w_gu"):
        return full
    if name in ("ffn1_w_down", "ffn2_w_down"):
        return full.reshape(N_DEV, D_FF // N_DEV, D)
    if name in _COL_SHARDED:
        r, cc = full.shape
        return jnp.moveaxis(full.reshape(r, N_DEV, cc // N_DEV), 1, 0)
    return full.reshape((N_DEV, full.shape[0] // N_DEV) + full.shape[1:])


def kernel(x, mem, positions, ffn1_norm, ffn1_w_gu, ffn1_w_down, mix_norm, ffn2_norm, ffn2_w_gu, ffn2_w_down, w_out, mem_norm, w_mem_kv, a_w_in, a_conv, a_A_log, a_dt_bias, a_out_norm, b_w_in, b_q_norm, b_w_uq, kv_in_norm, w_dkv, kv_lat_norm, w_ukv, final_norm, loss_target, m_ffn1_norm, m_ffn1_w_gu, m_ffn1_w_down, m_mix_norm, m_ffn2_norm, m_ffn2_w_gu, m_ffn2_w_down, m_w_out, m_mem_norm, m_w_mem_kv, m_a_w_in, m_a_conv, m_a_A_log, m_a_dt_bias, m_a_out_norm, m_b_w_in, m_b_q_norm, m_b_w_uq, m_kv_in_norm, m_w_dkv, m_kv_lat_norm, m_w_ukv, m_final_norm, v_ffn1_norm, v_ffn1_w_gu, v_ffn1_w_down, v_mix_norm, v_ffn2_norm, v_ffn2_w_gu, v_ffn2_w_down, v_w_out, v_mem_norm, v_w_mem_kv, v_a_w_in, v_a_conv, v_a_A_log, v_a_dt_bias, v_a_out_norm, v_b_w_in, v_b_q_norm, v_b_w_uq, v_kv_in_norm, v_w_dkv, v_kv_lat_norm, v_w_ukv, v_final_norm):
    loc = dict(locals())
    wl = {n: loc[n] for n in _WEIGHTS}
    ml = {n: loc["m_" + n] for n in _WEIGHTS}
    vl = {n: loc["v_" + n] for n in _WEIGHTS}

    me = 4 * lax.axis_index("x") + 2 * lax.axis_index("y") + lax.axis_index("c")
    w = {n: wl[n] for n in _REPLICATED}
    for n in _SHARDED:
        w[n] = [None] * _LAYERED[n] if n in _LAYERED else None
    back = {}

    def src_of(item):
        n, lo, hi = item
        return (wl[n] if lo is None else wl[n][lo:hi]).astype(BF16)

    def install(items, pieces):
        for (n, lo, hi), p in zip(items, pieces):
            if lo is None:
                w[n] = _full_from_shards(n, p)
            else:
                for l in range(lo, hi):
                    w[n][l] = _full_from_shards(n, p[:, l - lo])
        for n, lo, hi in items:
            if n == "a_w_in":
                for l in range(lo, hi):
                    w[n][l], back[(n, l)] = _ext_and_back(_a_in_ext, w[n][l])
            elif n == "a_conv":
                for l in range(lo, hi):
                    w[n][l] = w[n][l].astype(F32)
            elif n == "b_w_uq":
                for l in range(lo, hi):
                    w[n][l], back[(n, l)] = _ext_and_back(_uq_ext, w[n][l])
            elif n == "w_dkv":
                w[n], back[n] = _ext_and_back(_dkv_ext, w[n])
            elif n == "w_ukv":
                w[n], back[n] = _ext_and_back(_ukv_ext, w[n])

    install(_GATHER_FIRST, _exchange([src_of(it) for it in _GATHER_FIRST], True, "gather_first"))
    flights = {}
    for key, items in (("mix0", _GATHER_MIX0), ("rest", _GATHER_REST)):
        srcs = [src_of(it) for it in items]
        flights[key] = (items, srcs, _split_start(srcs, True, f"gather_{key}_start"))
        w["ffn1_norm"] = w["ffn1_norm"] + flights[key][2][-1][0, 0]

    def arrive(key, stream):
        items, srcs, fl = flights[key]
        lands = _split_wait(fl, stream, True, f"gather_{key}_wait")
        install(items, [lax.dynamic_update_slice(ld, s[None], (me,) + (0,) * s.ndim) for ld, s in zip(lands, srcs)])

    def layer_start(l, stream):
        if l == 1:
            arrive("rest", stream)

    def mixer_start(l, stream):
        if l == 0:
            arrive("mix0", stream)

    def grad_src(item, g):
        n, lo, hi = item

        def one(l):
            gl = g[n] if l is None else g[n][l]
            key = n if l is None else (n, l)
            if key in back:
                gl = back[key](gl)
            return _shards_from_full(n, gl).astype(BF16)

        return one(None) if lo is None else jnp.stack([one(l) for l in range(lo, hi)], axis=1)

    sent = {}

    def depart(key, items, g, d):
        srcs = [grad_src(it, g) for it in items]
        sent[key] = (items, srcs, _split_start(srcs, False, f"scatter_{key}_start"))
        return d + sent[key][2][-1][0, 0]

    def layer_done(l, g, d):
        if l == 2:
            return depart("hi", _SCATTER_HI, g, d)
        if l == 1:
            return depart("mid", _SCATTER_MID, g, d)
        return d

    def mixer_done(l, g, d):
        return depart("mix0", _SCATTER_MIX0, g, d) if l == 0 else d

    loss, dx, g = _local_step(x[0], mem[0], positions[0], loss_target[0], w, layer_start, mixer_start,
                              mixer_done, layer_done)
    loss = lax.psum(loss, ("x", "y", "c"))

    lo_srcs = [grad_src(it, g) for it in _SCATTER_LO]
    pieces = {n: [] for n in _SHARDED}
    for (n, lo, hi), p in zip(_SCATTER_LO, _exchange(lo_srcs, False, "scatter_grads")):
        pieces[n].append((lo, p))
    for key in ("mix0", "mid", "hi"):
        items, srcs, fl = sent[key]
        lands = _split_wait(fl, dx, False, f"scatter_{key}_wait")
        for (n, lo, hi), s, ld in zip(items, srcs, lands):
            mine = lax.dynamic_slice(s, (me,) + (0,) * (s.ndim - 1), (1,) + s.shape[1:])
            pieces[n].append((lo, lax.dynamic_update_slice(ld, mine, (me,) + (0,) * (s.ndim - 1))))
    out = {}
    for n in _SHARDED:
        ps = [p for _, p in sorted(pieces[n], key=lambda e: -1 if e[0] is None else e[0])]
        p = ps[0] if len(ps) == 1 else jnp.concatenate(ps, axis=1)
        shape = wl[n].shape
        res = _reduce_adamw(p.reshape(N_DEV, -1, shape[-1]), _as2d(wl[n]), _as2d(ml[n]), _as2d(vl[n]), "adamw_" + n)
        for kind, buf in zip(("grad", "delta", "new_m", "new_v"), res):
            out[(kind, n)] = buf.reshape(shape)

    rep_shapes = [wl[n].shape for n in _REPLICATED]
    grep = [jnp.stack(g[n]) if isinstance(g[n], list) else g[n] for n in _REPLICATED]
    (rparts,) = _exchange([_pack(grep)], True, "gather_small_grads")
    res = _reduce_adamw(rparts, _pack([wl[n] for n in _REPLICATED]), _pack([ml[n] for n in _REPLICATED]),
                        _pack([vl[n] for n in _REPLICATED]), "adamw_replicated")
    for kind, buf in zip(("grad", "delta", "new_m", "new_v"), res):
        for n, a in zip(_REPLICATED, _unpack(buf, rep_shapes)):
            out[(kind, n)] = a

    return (loss, dx[None], *[out[("grad", n)] for n in _WEIGHTS], *[out[("delta", n)] for n in _WEIGHTS],
            *[out[("new_m", n)] for n in _WEIGHTS], *[out[("new_v", n)] for n in _WEIGHTS])
```

```python
import functools

import numpy as np
import jax
import jax.numpy as jnp
from jax import lax
from jax.experimental import pallas as pl
from jax.experimental.pallas import tpu as pltpu

F32 = jnp.float32
BF16 = jnp.bfloat16

N_DEV = 8
D = 1024
D_FF = 2816
FF_SHARD = 2 * D_FF // N_DEV
DEPTH = 4
N_A = 2
N_B = 2
EPS = 1e-6
CHUNK = 64
GROUP = 256
GDN_HPS = 3
A_HEADS = 6
HD = 128
A_WIDTH = A_HEADS * HD
B_HEADS = 6
QK_NOPE = 128
QK_ROPE = 64
QK_CAT = 256
Q_LORA = 256
KV_LORA = 256
MEM_HEADS = 4
MEM_HD = 64
MEM_W = 256
N_MEM = 256
ROPE_THETA = 10000.0
ATT_SCALE = (QK_NOPE + QK_ROPE) ** -0.5
LN2 = 0.6931471805599453
Q_PRESCALE = ATT_SCALE / LN2
A_IN = 4 * A_WIDTH + 2 * A_HEADS + MEM_W
A_MQ_BLK = 4 * A_WIDTH // MEM_W
A_BA_BLK = (4 * A_WIDTH + MEM_W) // 128

ADAM_LR = 0.001
ADAM_B1 = 0.9
ADAM_B2 = 0.999
ADAM_EPS = 1e-08
ADAM_WD = 0.01
ADAM_STEP = 10

VMEM_LIMIT = 56 * 1024 * 1024
TM = 512
TMM = 1024
TMF = 2048
TMH = 2048
ATT_BQ = 1024
PACK_W = 1024
PACK_ROWS = 32


def _cparams(sem):
    return pltpu.CompilerParams(dimension_semantics=sem, vmem_limit_bytes=VMEM_LIMIT)


_DIMS = {"nn": ((1,), (0,)), "nt": ((1,), (1,)), "tn": ((0,), (0,))}


def _dot(a, b, dims="nn"):
    return lax.dot_general(a.astype(BF16), b.astype(BF16), (_DIMS[dims], ((), ())),
                           preferred_element_type=F32)


def _matmul(a, b, *, dims, grid, a_spec, b_spec, o_spec, out_shape, name, scale=1.0,
            res=None, res_spec=None, a_fn=None, epilogue=None, acc_shape=None, chunks=1):
    nk = grid[-1]
    kax = len(grid) - 1
    if acc_shape is None:
        acc_shape = tuple(s for s in o_spec.block_shape if s is not None)
    a_rows = a_spec.block_shape[-2] // chunks

    def rows_of(ref, c):
        rs = slice(c * a_rows, (c + 1) * a_rows)
        return ref[:, rs, :] if len(ref.shape) == 3 else ref[rs, :]

    def body(*refs):
        if res is None:
            a_ref, b_ref, o_ref, acc = refs
            r_ref = None
        else:
            a_ref, b_ref, r_ref, o_ref, acc = refs
        k = pl.program_id(kax)

        if chunks > 1 and nk == 1 and dims != "tn":
            for c in range(chunks):
                a_c = rows_of(a_ref, c)
                y = _dot(a_c if a_fn is None else a_fn(a_c), b_ref[...], dims) * scale
                rs = slice(c * a_rows, (c + 1) * a_rows)
                y = epilogue(y, rows_of(r_ref, c))
                if len(o_ref.shape) == 3:
                    o_ref[:, rs, :] = y.astype(o_ref.dtype)
                else:
                    o_ref[rs, :] = y.astype(o_ref.dtype)
            return

        @pl.when(k == 0)
        def _():
            acc[...] = jnp.zeros_like(acc)

        for c in range(chunks):
            a_c = rows_of(a_ref, c) if chunks > 1 else a_ref[...]
            a_c = a_c if a_fn is None else a_fn(a_c)
            rs = slice(c * a_rows, (c + 1) * a_rows)
            if chunks == 1:
                acc[...] += _dot(a_c, b_ref[...], dims)
            elif dims == "tn":
                acc[...] += _dot(a_c, b_ref[rs, :], dims)
            else:
                acc[rs, :] += _dot(a_c, b_ref[...], dims)

        @pl.when(k == nk - 1)
        def _():
            y = acc[...] * scale
            if epilogue is not None:
                y = epilogue(y, r_ref[...])
            elif r_ref is not None:
                y = y + r_ref[...].astype(F32)
            o_ref[...] = y.astype(o_ref.dtype)

    args = [a, b] + ([res] if res is not None else [])
    specs = [a_spec, b_spec] + ([res_spec] if res is not None else [])
    sem = ("parallel",) * kax + ("arbitrary",)
    return pl.pallas_call(
        body, out_shape=out_shape, grid=grid, in_specs=specs, out_specs=o_spec,
        scratch_shapes=[pltpu.VMEM(acc_shape, F32)], name=name, compiler_params=_cparams(sem))(*args)


def _tile(n, cap):
    if n <= cap:
        return n
    t = cap - cap % 128
    while t >= 128:
        if n % t == 0:
            return t
        t -= 128
    raise ValueError(f"no tile for {n}")


def _mm(a, b, dims, out_dtype, name, scale=1.0, res=None):
    if dims == "tn":
        kk, m = a.shape
        n = b.shape[1]
        tk, tn = _tile(kk, TMM), _tile(n, 1152)
        return _matmul(a, b, dims=dims, grid=(1, n // tn, kk // tk),
                       a_spec=pl.BlockSpec((tk, m), lambda i, j, k: (k, 0)),
                       b_spec=pl.BlockSpec((tk, tn), lambda i, j, k: (k, j)),
                       o_spec=pl.BlockSpec((m, tn), lambda i, j, k: (0, j)),
                       out_shape=jax.ShapeDtypeStruct((m, n), out_dtype), name=name, scale=scale)
    m, kk = a.shape
    n = b.shape[1] if dims == "nn" else b.shape[0]
    tm, tn, tk = _tile(m, TMM), _tile(n, 1152), _tile(kk, 1536)
    if dims == "nn":
        b_spec = pl.BlockSpec((tk, tn), lambda i, j, k: (k, j))
    else:
        b_spec = pl.BlockSpec((tn, tk), lambda i, j, k: (j, k))
    o_spec = pl.BlockSpec((tm, tn), lambda i, j, k: (i, j))
    return _matmul(a, b, dims=dims, grid=(m // tm, n // tn, kk // tk),
                   a_spec=pl.BlockSpec((tm, tk), lambda i, j, k: (i, k)), b_spec=b_spec, o_spec=o_spec,
                   out_shape=jax.ShapeDtypeStruct((m, n), out_dtype), name=name, scale=scale,
                   res=res, res_spec=o_spec if res is not None else None)


def _rowcall(fn, args, in_specs, out_shapes, out_specs, grid, name, n_acc=0):
    n_in, n_out = len(args), len(out_shapes)

    def body(*refs):
        outs = fn(*[r[...] for r in refs[:n_in]])
        if not isinstance(outs, (tuple, list)):
            outs = (outs,)
        first = pl.program_id(0) == 0
        for ax in range(1, len(grid)):
            first = jnp.logical_and(first, pl.program_id(ax) == 0)
        for idx, (o_ref, val) in enumerate(zip(refs[n_in:], outs)):
            if idx >= n_out - n_acc:
                @pl.when(first)
                def _(o_ref=o_ref):
                    o_ref[...] = jnp.zeros_like(o_ref)

                o_ref[...] += val.astype(o_ref.dtype)
            else:
                o_ref[...] = val.astype(o_ref.dtype)

    sem = (("arbitrary",) if n_acc else ("parallel",)) * len(grid)
    res = pl.pallas_call(body, out_shape=tuple(out_shapes), grid=grid, in_specs=list(in_specs),
                         out_specs=tuple(out_specs), name=name, compiler_params=_cparams(sem))(*args)
    return res


def _vjp_fn(fn, n_in, wrt):
    def bwd(*blocks):
        ins = [b.astype(F32) for b in blocks[:n_in]]
        cts = [c.astype(F32) for c in blocks[n_in:]]
        outs, vjp = jax.vjp(fn, *ins)
        if isinstance(outs, (tuple, list)):
            grads = vjp(tuple(cts))
        else:
            grads = vjp(cts[0])
        return tuple(grads[i] for i in wrt)
    return bwd


def _sds(shape, dtype):
    return jax.ShapeDtypeStruct(tuple(shape), dtype)


def _rows(tm, w, col=0):
    return pl.BlockSpec((tm, w), lambda i, *_: (i, col))


def _shared(shape):
    nd = len(shape)
    return pl.BlockSpec(tuple(shape), lambda *_: (0,) * nd)


def _rms(x, g):
    return x * lax.rsqrt(jnp.mean(x * x, axis=-1, keepdims=True) + EPS) * g


def _silu(x):
    return x * jax.nn.sigmoid(x)


def _swiglu_pair(gu):
    return _silu(gu[0].astype(F32)) * gu[1].astype(F32)


def _swiglu_bwd(dh, gu):
    g, u = gu[0].astype(F32), gu[1].astype(F32)
    sg = jax.nn.sigmoid(g)
    return jnp.stack([dh * u * sg * (1.0 + g * (1.0 - sg)), dh * g * sg])


def _gdn_prep(q, k, v):
    q, k, v = _silu(q), _silu(k), _silu(v)
    q = q * lax.rsqrt(jnp.sum(q * q, axis=-1, keepdims=True) + EPS) * (HD ** -0.5)
    k = k * lax.rsqrt(jnp.sum(k * k, axis=-1, keepdims=True) + EPS)
    return q, k, v


def _gates(ba, a_log, dt_bias):
    lane = lax.broadcasted_iota(jnp.int32, ba.shape, 1)
    beta = jax.nn.sigmoid(ba)
    z = ba + dt_bias
    softplus = jnp.maximum(z, 0.0) + jnp.log(1.0 + jnp.exp(-jnp.abs(z)))
    g = -jnp.exp(a_log) * softplus
    return jnp.where(lane < A_HEADS, beta, jnp.where(lane < 2 * A_HEADS, g, 0.0))


def _outnorm_gate(o, gate, gain):
    return _rms(o, gain) * _silu(gate)


def _memattn(q, k, v):
    lane = lax.shift_right_logical(lax.broadcasted_iota(jnp.int32, (1, MEM_W), 1), 6)
    out = jnp.zeros(q.shape, F32)
    for h in range(MEM_HEADS):
        mh = (lane == h).astype(F32)
        s = _dot(q * mh, k, "nt") * (MEM_HD ** -0.5)
        s = s - lax.stop_gradient(jnp.max(s, axis=-1, keepdims=True))
        p = jnp.exp(s)
        p = p / jnp.sum(p, axis=-1, keepdims=True)
        out = out + _dot(p, v * mh)
    return out


def _rope_mix(a, a_sw, c, s):
    return (a * c + a_sw * s) * Q_PRESCALE


def _kcat(kn, kr, kr_sw, c, s):
    return kn + kr * c + kr_sw * s


@jax.custom_vjp
def _neumann_inverse(nmat):
    n = nmat.shape[0]
    eye = (lax.broadcasted_iota(jnp.int32, (n, n), 0) == lax.broadcasted_iota(jnp.int32, (n, n), 1)).astype(F32)
    pinv = eye + nmat
    npow = nmat
    for _ in range(5):
        npow = _dot(npow, npow)
        pinv = pinv + _dot(pinv, npow)
    return pinv


def _neumann_inverse_fwd(nmat):
    pinv = _neumann_inverse(nmat)
    return pinv, pinv


def _neumann_inverse_bwd(pinv, ct):
    return (_dot(_dot(pinv, ct, "tn"), pinv, "nt"),)


_neumann_inverse.defvjp(_neumann_inverse_fwd, _neumann_inverse_bwd)


@jax.custom_vjp
def _known_inverse(nmat, pinv):
    return pinv


def _known_inverse_fwd(nmat, pinv):
    return pinv, pinv


def _known_inverse_bwd(pinv, ct):
    return _dot(_dot(pinv, ct, "tn"), pinv, "nt"), jnp.zeros_like(pinv)


_known_inverse.defvjp(_known_inverse_fwd, _known_inverse_bwd)


def _gdn_local(q, k, v, beta, gcol, grow, pinv_kept=None):
    n = GROUP
    ri = lax.broadcasted_iota(jnp.int32, (n, n), 0)
    ci = lax.broadcasted_iota(jnp.int32, (n, n), 1)
    same = lax.shift_right_logical(ri, 6) == lax.shift_right_logical(ci, 6)
    lower = jnp.logical_and(same, ci <= ri)
    strict = jnp.logical_and(same, ci < ri)
    gc_col = jnp.sum(lower.astype(F32) * grow, axis=1, keepdims=True)
    gc_row = jnp.sum(jnp.logical_and(same, ri <= ci).astype(F32) * gcol, axis=0, keepdims=True)
    glast = jnp.sum(same.astype(F32) * grow, axis=1, keepdims=True)
    decay = jnp.where(lower, jnp.exp(jnp.where(lower, gc_col - gc_row, 0.0)), 0.0)
    kb = k * beta
    nmat = -jnp.where(strict, _dot(kb, k, "nt") * decay, 0.0)
    pinv = _neumann_inverse(nmat) if pinv_kept is None else _known_inverse(nmat, pinv_kept)
    e_gc = jnp.exp(gc_col)
    u = _dot(pinv, v * beta)
    w = _dot(pinv, kb * e_gc)
    qk = _dot(q, k, "nt") * decay
    fold = (jnp.bitwise_and(lax.broadcasted_iota(jnp.int32, (n, CHUNK), 0), CHUNK - 1)
            == lax.broadcasted_iota(jnp.int32, (n, CHUNK), 1)).astype(F32)
    qk_c = _dot(qk, fold)
    q_dec = q * e_gc
    k_dec = k * jnp.exp(glast - gc_col)
    dmat = jnp.exp(glast) * jnp.ones((1, HD), F32)
    if pinv_kept is None:
        return u, w, q_dec, k_dec, qk_c, dmat, pinv
    return u, w, q_dec, k_dec, qk_c, dmat


def _gdn_step(s, w_c, u_c, qd_c, kd_c, qk_c, d_c):
    v_new = u_c - _dot(w_c, s)
    out = _dot(qd_c, s) + _dot(qk_c, v_new)
    d_row = jnp.mean(d_c, axis=0, keepdims=True)
    s_new = s * d_row + _dot(kd_c, v_new, "tn")
    return s_new, out


def _rmsnorm_fwd(x, gain, name, col=0, width=None):
    t = x.shape[0]
    w = width or x.shape[1]
    (n,) = _rowcall(_rms, [x, gain.reshape(1, w)], [_rows(TM, w, col), _shared((1, w))],
                    [_sds((t, w), BF16)], [_rows(TM, w)], (t // TM,), name)
    return n


def _rmsnorm_bwd(x, gain, dn, dres, name):
    t, w = x.shape
    fn = _vjp_fn(_rms, 2, (0, 1))

    def bwd(xb, gb, dnb, drb):
        dx, dg = fn(xb, gb, dnb)
        return dx + drb, dg

    dx, dg = _rowcall(bwd, [x, gain.reshape(1, w), dn, dres],
                      [_rows(TM, w), _shared((1, w)), _rows(TM, w), _rows(TM, w)],
                      [_sds((t, w), F32), _sds((1, w), F32)], [_rows(TM, w), _shared((1, w))],
                      (t // TM,), name, n_acc=1)
    return dx, dg[0]


def _ffn_fwd(x, gain, wgu8, wd4, tag):
    t = x.shape[0]
    nt = t // TMM
    tf = min(TMF, t)
    n = _rmsnorm_fwd(x, gain, tag + "_norm")
    gu = _matmul(n, wgu8, dims="nn", grid=(N_DEV, t // tf, 1),
                 a_spec=pl.BlockSpec((tf, D), lambda j, i, k: (i, 0)),
                 b_spec=pl.BlockSpec((None, D, FF_SHARD), lambda j, i, k: (j, 0, 0)),
                 o_spec=pl.BlockSpec((None, tf, FF_SHARD), lambda j, i, k: (j, i, 0)),
                 out_shape=_sds((N_DEV, t, FF_SHARD), BF16), name=tag + "_gu")
    gu = gu.reshape(2, 4, t, FF_SHARD)
    y = _matmul(gu, wd4, dims="nn", grid=(nt, 1, 4), a_fn=_swiglu_pair, chunks=4,
                a_spec=pl.BlockSpec((2, None, TMM, FF_SHARD), lambda i, j, k: (0, k, i, 0)),
                b_spec=pl.BlockSpec((None, FF_SHARD, D), lambda i, j, k: (k, 0, 0)),
                o_spec=pl.BlockSpec((TMM, D), lambda i, j, k: (i, 0)),
                out_shape=_sds((t, D), F32), name=tag + "_down", scale=0.5,
                res=x, res_spec=pl.BlockSpec((TMM, D), lambda i, j, k: (i, 0)))
    return y, (x, n, gu)


def _ffn_bwd(d, saved, gain, wgu8, wd4, tag):
    x, n, gu = saved
    t = x.shape[0]
    nt = t // TMM
    dgu = _matmul(d, wd4, dims="nt", grid=(4, nt, 1),
                  a_spec=pl.BlockSpec((TMM, D), lambda j, i, k: (i, 0)),
                  b_spec=pl.BlockSpec((None, FF_SHARD, D), lambda j, i, k: (j, 0, 0)),
                  o_spec=pl.BlockSpec((2, None, TMM, FF_SHARD), lambda j, i, k: (0, j, i, 0)),
                  out_shape=_sds((2, 4, t, FF_SHARD), BF16), name=tag + "_dgu", scale=0.5,
                  res=gu, res_spec=pl.BlockSpec((2, None, TMM, FF_SHARD), lambda j, i, k: (0, j, i, 0)),
                  epilogue=_swiglu_bwd, acc_shape=(TMM, FF_SHARD))
    tf = min(TMF, t)
    nf = t // tf
    dwd4 = _matmul(gu, d, dims="tn", grid=(4, 1, nf), a_fn=_swiglu_pair, chunks=4,
                   a_spec=pl.BlockSpec((2, None, tf, FF_SHARD), lambda j, i, k: (0, j, k, 0)),
                   b_spec=pl.BlockSpec((tf, D), lambda j, i, k: (k, 0)),
                   o_spec=pl.BlockSpec((None, FF_SHARD, D), lambda j, i, k: (j, 0, 0)),
                   out_shape=_sds((4, FF_SHARD, D), F32), name=tag + "_dwd", scale=0.5)
    dgu = dgu.reshape(N_DEV, t, FF_SHARD)
    dwgu8 = _matmul(n, dgu, dims="tn", grid=(N_DEV, 1, nf),
                    a_spec=pl.BlockSpec((tf, D), lambda j, i, k: (k, 0)),
                    b_spec=pl.BlockSpec((None, tf, FF_SHARD), lambda j, i, k: (j, k, 0)),
                    o_spec=pl.BlockSpec((None, D, FF_SHARD), lambda j, i, k: (j, 0, 0)),
                    out_shape=_sds((N_DEV, D, FF_SHARD), F32), name=tag + "_dwgu")
    dn = _matmul(dgu, wgu8, dims="nt", grid=(nf, 1, N_DEV),
                 a_spec=pl.BlockSpec((None, tf, FF_SHARD), lambda i, j, k: (k, i, 0)),
                 b_spec=pl.BlockSpec((None, D, FF_SHARD), lambda i, j, k: (k, 0, 0)),
                 o_spec=pl.BlockSpec((tf, D), lambda i, j, k: (i, 0)),
                 out_shape=_sds((t, D), BF16), name=tag + "_dn")
    dx, dgain = _rmsnorm_bwd(x, gain, dn, d, tag + "_dnorm")
    return dx, dgain, dwgu8, dwd4


CONV_TC = 768
CONV_K = 4
CONV_TM = 1024


def _conv_fwd(ha, w, name):
    t = ha.shape[0]
    nb = CONV_TM // 8

    def body(prev_ref, cur_ref, w_ref, o_ref):
        i = pl.program_id(0)
        cur = cur_ref[...].astype(F32)
        prev = prev_ref[...].astype(F32) * (i > 0).astype(F32)
        ext = jnp.concatenate([prev, cur], axis=0)
        wv = w_ref[...]
        acc = cur * wv[3:4]
        for k in range(1, CONV_K):
            acc = acc + pltpu.roll(ext, k, axis=0)[8:] * wv[3 - k:4 - k]
        o_ref[...] = acc.astype(o_ref.dtype)

    return pl.pallas_call(
        body, out_shape=_sds((3, t, CONV_TC), BF16), grid=(t // CONV_TM, 3),
        in_specs=[pl.BlockSpec((8, CONV_TC), lambda i, c: (jnp.maximum(i * nb - 1, 0), c)),
                  pl.BlockSpec((CONV_TM, CONV_TC), lambda i, c: (i, c)),
                  pl.BlockSpec((CONV_K, CONV_TC), lambda i, c: (0, c))],
        out_specs=pl.BlockSpec((None, CONV_TM, CONV_TC), lambda i, c: (c, i, 0)),
        name=name, compiler_params=_cparams(("parallel", "parallel")))(ha, ha, w)


def _conv_bwd(ha, w, dy3, name):
    t = ha.shape[0]
    nb = CONV_TM // 8
    nt = t // CONV_TM

    def body(prev_ref, cur_ref, dy_ref, nxt_ref, w_ref, dx_ref, dw_ref):
        i = pl.program_id(1)
        cur = cur_ref[...].astype(F32)
        prev = prev_ref[...].astype(F32) * (i > 0).astype(F32)
        ext = jnp.concatenate([prev, cur], axis=0)
        dy = dy_ref[...].astype(F32)
        nxt = nxt_ref[...].astype(F32) * (i < nt - 1).astype(F32)
        dext = jnp.concatenate([dy, nxt], axis=0)
        wv = w_ref[...]
        dx = dy * wv[3:4]
        dws = [None] * CONV_K
        dws[3] = jnp.sum(dy * cur, axis=0, keepdims=True)
        for k in range(1, CONV_K):
            dx = dx + pltpu.roll(dext, CONV_TM + 8 - k, axis=0)[:CONV_TM] * wv[3 - k:4 - k]
            dws[3 - k] = jnp.sum(dy * pltpu.roll(ext, k, axis=0)[8:], axis=0, keepdims=True)
        dx_ref[...] = dx.astype(dx_ref.dtype)

        @pl.when(i == 0)
        def _():
            dw_ref[...] = jnp.zeros_like(dw_ref)

        dw_ref[...] += jnp.concatenate(dws, axis=0)

    return pl.pallas_call(
        body, out_shape=(_sds((t, 3 * CONV_TC), BF16), _sds((CONV_K, 3 * CONV_TC), F32)), grid=(3, nt),
        in_specs=[pl.BlockSpec((8, CONV_TC), lambda c, i: (jnp.maximum(i * nb - 1, 0), c)),
                  pl.BlockSpec((CONV_TM, CONV_TC), lambda c, i: (i, c)),
                  pl.BlockSpec((None, CONV_TM, CONV_TC), lambda c, i: (c, i, 0)),
                  pl.BlockSpec((None, 8, CONV_TC), lambda c, i: (c, jnp.minimum((i + 1) * nb, t // 8 - 1), 0)),
                  pl.BlockSpec((CONV_K, CONV_TC), lambda c, i: (0, c))],
        out_specs=(pl.BlockSpec((CONV_TM, CONV_TC), lambda c, i: (i, c)),
                   pl.BlockSpec((CONV_K, CONV_TC), lambda c, i: (0, c))),
        name=name, compiler_params=_cparams(("parallel", "arbitrary")))(ha, ha, dy3, dy3, w)


def _gdn_specs(t, rev):
    ng = t // GROUP

    def gi(g):
        return ng - 1 - g if rev else g

    qkv = pl.BlockSpec((3, GROUP, GDN_HPS * HD), lambda h, g: (0, gi(g), h))
    bg = pl.BlockSpec((GROUP, 128), lambda h, g: (gi(g), 0))
    dbg = pl.BlockSpec((GDN_HPS, GROUP, 128), lambda h, g: (h, gi(g), 0))
    o = pl.BlockSpec((GROUP, GDN_HPS * HD), lambda h, g: (gi(g), h))
    st = pl.BlockSpec((GDN_HPS, None, HD, HD), lambda h, g: (h, gi(g), 0, 0))
    inv = pl.BlockSpec((GDN_HPS, None, GROUP, GROUP), lambda h, g: (h, gi(g), 0, 0))
    return qkv, bg, dbg, o, st, inv


def _head_qkv(qkv_ref, j):
    sl = slice(j * HD, (j + 1) * HD)
    return qkv_ref[0, :, sl].astype(F32), qkv_ref[1, :, sl].astype(F32), qkv_ref[2, :, sl].astype(F32)


def _head_gates(bg, h):
    lane = lax.broadcasted_iota(jnp.int32, (1, 128), 1)
    beta = jnp.sum(jnp.where(lane == h, bg, 0.0), axis=1, keepdims=True)
    gcol = jnp.sum(jnp.where(lane == h + A_HEADS, bg, 0.0), axis=1, keepdims=True)
    return beta, gcol, _col_to_row(gcol)


def _gdn_fwd(qkv3, bg, name):
    t = qkv3.shape[1]
    ng = t // GROUP
    qkv_s, bg_s, _, o_s, st_s, inv_s = _gdn_specs(t, False)

    def body(qkv_ref, bg_ref, o_ref, st_ref, inv_ref, s_scr):
        @pl.when(pl.program_id(1) == 0)
        def _():
            s_scr[...] = jnp.zeros_like(s_scr)

        st_ref[...] = s_scr[...]
        bgv = bg_ref[...]
        loc = [_gdn_local(*_head_qkv(qkv_ref, j), *_head_gates(bgv, pl.program_id(0) * GDN_HPS + j))
               for j in range(GDN_HPS)]
        s = [s_scr[j] for j in range(GDN_HPS)]
        for a in range(GROUP // CHUNK):
            sl = slice(a * CHUNK, (a + 1) * CHUNK)
            for j in range(GDN_HPS):
                u, w, qd, kd, qkc, dm, _ = loc[j]
                s[j], out = _gdn_step(s[j], w[sl], u[sl], qd[sl], kd[sl], qkc[sl], dm[sl])
                o_ref[sl, j * HD:(j + 1) * HD] = out.astype(o_ref.dtype)
        for j in range(GDN_HPS):
            s_scr[j] = s[j]
            inv_ref[j] = loc[j][6].astype(inv_ref.dtype)

    return pl.pallas_call(
        body, out_shape=(_sds((t, A_WIDTH), BF16), _sds((A_HEADS, ng, HD, HD), F32),
                         _sds((A_HEADS, ng, GROUP, GROUP), BF16)),
        grid=(A_HEADS // GDN_HPS, ng), in_specs=[qkv_s, bg_s], out_specs=(o_s, st_s, inv_s),
        scratch_shapes=[pltpu.VMEM((GDN_HPS, HD, HD), F32)], name=name,
        compiler_params=_cparams(("parallel", "arbitrary")))(qkv3, bg)


def _gdn_bwd(qkv3, bg, states, pinvs, do, name):
    t = qkv3.shape[1]
    ng = t // GROUP
    qkv_s, bg_s, dbg_s, o_s, st_s, inv_s = _gdn_specs(t, True)
    nc = GROUP // CHUNK

    def body(qkv_ref, bg_ref, st_ref, inv_ref, do_ref, dqkv_ref, dbg_ref, ds_scr):
        @pl.when(pl.program_id(1) == 0)
        def _():
            ds_scr[...] = jnp.zeros_like(ds_scr)

        heads = range(GDN_HPS)
        bgv = bg_ref[...]
        hid = [pl.program_id(0) * GDN_HPS + j for j in heads]
        fw = [jax.vjp(functools.partial(_gdn_local, pinv_kept=inv_ref[j].astype(F32)),
                      *_head_qkv(qkv_ref, j), *_head_gates(bgv, hid[j])) for j in heads]
        starts = [[None] * nc for _ in heads]
        s = [st_ref[j] for j in heads]
        for a in range(nc):
            sl = slice(a * CHUNK, (a + 1) * CHUNK)
            for j in heads:
                u, w, qd, kd, qkc, dm = fw[j][0]
                starts[j][a] = s[j]
                if a < nc - 1:
                    s[j], _ = _gdn_step(s[j], w[sl], u[sl], qd[sl], kd[sl], qkc[sl], dm[sl])
        ds = [ds_scr[j] for j in heads]
        parts = [[None] * nc for _ in heads]
        for a in reversed(range(nc)):
            sl = slice(a * CHUNK, (a + 1) * CHUNK)
            for j in heads:
                u, w, qd, kd, qkc, dm = fw[j][0]
                _, vjp_step = jax.vjp(_gdn_step, starts[j][a], w[sl], u[sl], qd[sl], kd[sl], qkc[sl], dm[sl])
                grads = vjp_step((ds[j], do_ref[sl, j * HD:(j + 1) * HD].astype(F32)))
                ds[j] = grads[0]
                parts[j][a] = grads[1:]
        lane = lax.broadcasted_iota(jnp.int32, (1, 128), 1)
        for j in heads:
            ds_scr[j] = ds[j]
            dw, du, dqd, dkd, dqk, ddm = [jnp.concatenate([parts[j][a][i] for a in range(nc)], axis=0)
                                          for i in range(6)]
            dq, dk, dv, db, dgc, dgr = fw[j][1]((du, dw, dqd, dkd, dqk, ddm))
            hs = slice(j * HD, (j + 1) * HD)
            dqkv_ref[0, :, hs] = dq.astype(dqkv_ref.dtype)
            dqkv_ref[1, :, hs] = dk.astype(dqkv_ref.dtype)
            dqkv_ref[2, :, hs] = dv.astype(dqkv_ref.dtype)
            dbg_ref[j] = (jnp.where(lane == hid[j], db, 0.0)
                          + jnp.where(lane == hid[j] + A_HEADS, dgc + _row_to_col(dgr), 0.0))

    return pl.pallas_call(
        body, out_shape=(_sds((3, t, A_WIDTH), BF16), _sds((A_HEADS, t, 128), F32)),
        grid=(A_HEADS // GDN_HPS, ng), in_specs=[qkv_s, bg_s, st_s, inv_s, o_s],
        out_specs=(qkv_s, dbg_s), scratch_shapes=[pltpu.VMEM((GDN_HPS, HD, HD), F32)], name=name,
        compiler_params=_cparams(("parallel", "arbitrary")))(qkv3, bg, states, pinvs, do)


NEG = -1e30


def _diag_mask(shape, q_axis):
    qi = lax.shift_right_logical(lax.broadcasted_iota(jnp.int32, shape, q_axis), 6)
    ki = lax.shift_right_logical(lax.broadcasted_iota(jnp.int32, shape, 1 - q_axis), 6)
    return ki <= qi


def _col_to_row(col):
    n = col.shape[0]
    eye = lax.broadcasted_iota(jnp.int32, (n, n), 0) == lax.broadcasted_iota(jnp.int32, (n, n), 1)
    return jnp.sum(jnp.where(eye, col, 0.0), axis=0, keepdims=True)


def _row_to_col(row):
    n = row.shape[1]
    eye = lax.broadcasted_iota(jnp.int32, (n, n), 0) == lax.broadcasted_iota(jnp.int32, (n, n), 1)
    return jnp.sum(jnp.where(eye, row, 0.0), axis=1, keepdims=True)


def _blk(ref, i):
    return ref[pl.ds(pl.multiple_of(i * ATT_BQ, ATT_BQ), ATT_BQ), :]


def _att_fwd(qc, kc, v, name):
    t = qc.shape[0]
    nq = t // ATT_BQ

    def body(q_ref, k_ref, v_ref, o_ref, lse_ref, lser_ref, m_scr, l_scr, acc_scr):
        qb = pl.program_id(1)
        q = q_ref[...]
        m_scr[...] = jnp.full_like(m_scr, NEG)
        l_scr[...] = jnp.zeros_like(l_scr)
        acc_scr[...] = jnp.zeros_like(acc_scr)

        def step(kb, diag):
            s = _dot(q, _blk(k_ref, kb), "nt")
            if diag:
                s = jnp.where(_diag_mask(s.shape, 0), s, NEG)
            m_old = m_scr[...]
            m_new = jnp.maximum(m_old, jnp.max(s, axis=1, keepdims=True))
            alpha = jnp.exp2(m_old - m_new)
            p = jnp.exp2(s - m_new)
            l_scr[...] = alpha * l_scr[...] + jnp.sum(p, axis=1, keepdims=True)
            acc_scr[...] = alpha * acc_scr[...] + _dot(p, _blk(v_ref, kb))
            m_scr[...] = m_new

        def loop_body(kb, carry):
            step(kb, False)
            return carry

        lax.fori_loop(0, qb, loop_body, 0)
        step(qb, True)
        o_ref[...] = (acc_scr[...] / l_scr[...]).astype(o_ref.dtype)
        lse = m_scr[...] + jnp.log2(l_scr[...])
        lse_ref[...] = lse
        lser_ref[...] = _col_to_row(lse)

    return pl.pallas_call(
        body, out_shape=(_sds((t, B_HEADS * HD), BF16), _sds((B_HEADS, t, 1), F32),
                         _sds((B_HEADS, nq, 1, ATT_BQ), F32)), grid=(B_HEADS, nq),
        in_specs=[pl.BlockSpec((ATT_BQ, QK_CAT), lambda h, i: (i, h)),
                  pl.BlockSpec((t, QK_CAT), lambda h, i: (0, h)), pl.BlockSpec((t, HD), lambda h, i: (0, h))],
        out_specs=(pl.BlockSpec((ATT_BQ, HD), lambda h, i: (i, h)),
                   pl.BlockSpec((None, ATT_BQ, 1), lambda h, i: (h, i, 0)),
                   pl.BlockSpec((None, None, 1, ATT_BQ), lambda h, i: (h, i, 0, 0))),
        scratch_shapes=[pltpu.VMEM((ATT_BQ, 1), F32), pltpu.VMEM((ATT_BQ, 1), F32), pltpu.VMEM((ATT_BQ, HD), F32)],
        name=name, compiler_params=_cparams(("parallel", "arbitrary")))(qc, kc, v)


def _att_bwd(qc, kc, v, o, lse, lse_row, do, name):
    t = qc.shape[0]
    nq = t // ATT_BQ

    def delta_fn(ob, dob):
        dl = jnp.sum(ob.astype(F32) * dob.astype(F32), axis=1, keepdims=True)
        return dl, _col_to_row(dl)

    delta, delta_row = _rowcall(
        delta_fn, [o, do], [pl.BlockSpec((ATT_BQ, HD), lambda i, h: (i, h))] * 2,
        [_sds((B_HEADS, t, 1), F32), _sds((B_HEADS, nq, 1, ATT_BQ), F32)],
        [pl.BlockSpec((None, ATT_BQ, 1), lambda i, h: (h, i, 0)),
         pl.BlockSpec((None, None, 1, ATT_BQ), lambda i, h: (h, i, 0, 0))], (nq, B_HEADS), name + "_delta")

    def dq_body(q_ref, k_ref, v_ref, do_ref, lse_ref, dl_ref, dq_ref, acc):
        qb = pl.program_id(1)
        q, dob, lse_b, dl_b = q_ref[...], do_ref[...], lse_ref[...], dl_ref[...]
        acc[...] = jnp.zeros_like(acc)

        def step(kb, diag):
            k = _blk(k_ref, kb)
            s = _dot(q, k, "nt")
            if diag:
                s = jnp.where(_diag_mask(s.shape, 0), s, NEG)
            p = jnp.exp2(s - lse_b)
            ds = p * (_dot(dob, _blk(v_ref, kb), "nt") - dl_b)
            acc[...] += _dot(ds, k)

        def loop_body(kb, carry):
            step(kb, False)
            return carry

        lax.fori_loop(0, qb, loop_body, 0)
        step(qb, True)
        dq_ref[...] = (acc[...] * ATT_SCALE).astype(dq_ref.dtype)

    qmap = lambda h, i: (i, h)
    colq = pl.BlockSpec((None, ATT_BQ, 1), lambda h, i: (h, i, 0))
    dq = pl.pallas_call(
        dq_body, out_shape=_sds((t, B_HEADS * QK_CAT), BF16), grid=(B_HEADS, nq),
        in_specs=[pl.BlockSpec((ATT_BQ, QK_CAT), qmap), pl.BlockSpec((t, QK_CAT), lambda h, i: (0, h)),
                  pl.BlockSpec((t, HD), lambda h, i: (0, h)), pl.BlockSpec((ATT_BQ, HD), qmap), colq, colq],
        out_specs=pl.BlockSpec((ATT_BQ, QK_CAT), qmap),
        scratch_shapes=[pltpu.VMEM((ATT_BQ, QK_CAT), F32)], name=name + "_dq",
        compiler_params=_cparams(("parallel", "arbitrary")))(qc, kc, v, do, lse, delta)

    def dkv_body(k_ref, v_ref, q_ref, do_ref, lser_ref, dlr_ref, dk_ref, dv_ref, dk_acc, dv_acc):
        kb = pl.program_id(1)
        k, vv = k_ref[...], v_ref[...]
        dk_acc[...] = jnp.zeros_like(dk_acc)
        dv_acc[...] = jnp.zeros_like(dv_acc)

        def step(qb, diag):
            q, dob = _blk(q_ref, qb), _blk(do_ref, qb)
            st = _dot(k, q, "nt")
            if diag:
                st = jnp.where(_diag_mask(st.shape, 1), st, NEG)
            pt = jnp.exp2(st - lser_ref[qb])
            dst = pt * (_dot(vv, dob, "nt") - dlr_ref[qb])
            dv_acc[...] += _dot(pt, dob)
            dk_acc[...] += _dot(dst, q)

        def loop_body(qb, carry):
            step(qb, False)
            return carry

        step(kb, True)
        lax.fori_loop(kb + 1, nq, loop_body, 0)
        dk_ref[...] = (dk_acc[...] * LN2).astype(dk_ref.dtype)
        dv_ref[...] = dv_acc[...].astype(dv_ref.dtype)

    kmap = lambda h, j: (j, h)
    rowq = pl.BlockSpec((None, nq, 1, ATT_BQ), lambda h, j: (h, 0, 0, 0))
    dk, dv = pl.pallas_call(
        dkv_body, out_shape=(_sds((t, B_HEADS * QK_CAT), BF16), _sds((t, B_HEADS * HD), BF16)),
        grid=(B_HEADS, nq),
        in_specs=[pl.BlockSpec((ATT_BQ, QK_CAT), kmap), pl.BlockSpec((ATT_BQ, HD), kmap),
                  pl.BlockSpec((t, QK_CAT), lambda h, j: (0, h)), pl.BlockSpec((t, HD), lambda h, j: (0, h)),
                  rowq, rowq],
        out_specs=(pl.BlockSpec((ATT_BQ, QK_CAT), kmap), pl.BlockSpec((ATT_BQ, HD), kmap)),
        scratch_shapes=[pltpu.VMEM((ATT_BQ, QK_CAT), F32), pltpu.VMEM((ATT_BQ, HD), F32)], name=name + "_dkv",
        compiler_params=_cparams(("parallel", "arbitrary")))(kc, v, qc, do, lse_row, delta_row)
    return dq, dk, dv


def _mem_fwd(hx, col, mkv, name):
    t = hx.shape[0]
    (o,) = _rowcall(_memattn, [hx, mkv, mkv],
                    [_rows(TM, MEM_W, col), pl.BlockSpec((N_MEM, MEM_W), lambda i: (0, 0)),
                     pl.BlockSpec((N_MEM, MEM_W), lambda i: (0, 1))],
                    [_sds((t, MEM_W), BF16)], [_rows(TM, MEM_W)], (t // TM,), name)
    return o


def _mem_bwd(hx, col, mkv, do, do_col, name):
    t = hx.shape[0]
    dq, dk, dv = _rowcall(_vjp_fn(_memattn, 3, (0, 1, 2)), [hx, mkv, mkv, do],
                          [_rows(TM, MEM_W, col), pl.BlockSpec((N_MEM, MEM_W), lambda i: (0, 0)),
                           pl.BlockSpec((N_MEM, MEM_W), lambda i: (0, 1)), _rows(TM, MEM_W, do_col)],
                          [_sds((t, MEM_W), BF16), _sds((N_MEM, MEM_W), F32), _sds((N_MEM, MEM_W), F32)],
                          [_rows(TM, MEM_W), _shared((N_MEM, MEM_W)), _shared((N_MEM, MEM_W))],
                          (t // TM,), name, n_acc=2)
    return dq, jnp.concatenate([dk, dv], axis=1)


def _a_in_ext(w):
    nb = 4 * A_WIDTH
    ba = jnp.pad(w[:, nb:nb + 2 * A_HEADS], ((0, 0), (0, 128 - 2 * A_HEADS)))
    return jnp.concatenate([w[:, :nb], w[:, nb + 2 * A_HEADS:], ba], axis=1)


def _swap_halves(w):
    return jnp.concatenate([w[..., QK_ROPE // 2:], w[..., :QK_ROPE // 2]], axis=-1)


def _uq_ext(w):
    w = w.reshape(Q_LORA, B_HEADS, QK_NOPE + QK_ROPE)
    nope, rope = w[..., :QK_NOPE], w[..., QK_NOPE:]
    z64 = jnp.zeros((Q_LORA, B_HEADS, QK_CAT - QK_NOPE - QK_ROPE), w.dtype)
    z128 = jnp.zeros((Q_LORA, B_HEADS, QK_NOPE), w.dtype)
    a = jnp.concatenate([nope, rope, z64], axis=-1).reshape(Q_LORA, B_HEADS * QK_CAT)
    b = jnp.concatenate([z128, _swap_halves(rope), z64], axis=-1).reshape(Q_LORA, B_HEADS * QK_CAT)
    return jnp.concatenate([a, b], axis=1)


def _dkv_ext(w):
    ckv, kr = w[:, :KV_LORA], w[:, KV_LORA:]
    z128 = jnp.zeros((D, QK_NOPE), w.dtype)
    z64 = jnp.zeros((D, QK_CAT - QK_NOPE - QK_ROPE), w.dtype)
    return jnp.concatenate([ckv, z128, kr, z64, z128, _swap_halves(kr), z64], axis=1)


def _ukv_ext(w):
    w = w.reshape(KV_LORA, B_HEADS, QK_NOPE + HD)
    kn, vv = w[..., :QK_NOPE], w[..., QK_NOPE:]
    z = jnp.zeros((KV_LORA, B_HEADS, QK_CAT - QK_NOPE), w.dtype)
    a = jnp.concatenate([kn, z], axis=-1).reshape(KV_LORA, B_HEADS * QK_CAT)
    return jnp.concatenate([a, vv.reshape(KV_LORA, B_HEADS * HD)], axis=1)


def _ext_and_back(fn, w):
    ext, back = jax.vjp(fn, w.astype(F32))
    return ext.astype(BF16), lambda g: back(g.astype(F32))[0]


def _rope_tables(pos_col):
    t = pos_col.shape[0]
    inv = (ROPE_THETA ** (-np.arange(0, QK_ROPE, 2, dtype=np.float32) / QK_ROPE)).astype(np.float32)
    inv_row = np.zeros((1, QK_CAT), np.float32)
    inv_row[0, QK_NOPE:QK_NOPE + QK_ROPE] = np.concatenate([inv, inv])
    sign = np.zeros((1, QK_CAT), np.float32)
    sign[0, QK_NOPE:QK_NOPE + QK_ROPE // 2] = -1.0
    sign[0, QK_NOPE + QK_ROPE // 2:QK_NOPE + QK_ROPE] = 1.0
    is_rope = np.abs(sign)
    is_nope = np.zeros((1, QK_CAT), np.float32)
    is_nope[0, :QK_NOPE] = 1.0

    def fn(p, inv_b, sign_b, rope_b, nope_b):
        ang = p.astype(F32) * inv_b
        return jnp.cos(ang) * rope_b + nope_b, jnp.sin(ang) * sign_b

    consts = [jnp.asarray(a) for a in (inv_row, sign, is_rope, is_nope)]
    return _rowcall(fn, [pos_col] + consts, [_rows(TM, 1)] + [_shared((1, QK_CAT))] * 4,
                    [_sds((t, QK_CAT), F32)] * 2, [_rows(TM, QK_CAT)] * 2, (t // TM,), "rope_tables")


def _local_step(x, mem, pos, target, w, layer_start, mixer_start, mixer_done, layer_done):
    t = x.shape[0]
    g = {}
    th = min(TMH, t)
    head6 = (t // th, A_HEADS)

    mem_n = _rmsnorm_fwd_small(mem, w["mem_norm"])
    rope_c, rope_s = _rope_tables(pos.reshape(t, 1))
    mkv = None

    saved = []
    for l in range(DEPTH):
        sv = {}
        layer_start(l, x)
        x, sv["ffn1"] = _ffn_fwd(x, w["ffn1_norm"][l], w["ffn1_w_gu"][l], w["ffn1_w_down"][l], "ffn1")
        sv["x1"] = x
        mixer_start(l, x)
        if mkv is None:
            mkv = [_mm(mem_n, w["w_mem_kv"][i], "nn", BF16, f"mkv{i}") for i in range(DEPTH)]
        n2 = _rmsnorm_fwd(x, w["mix_norm"][l], "mix_norm")
        sv["n2"] = n2
        if l < N_A:
            ha = _mm(n2, w["a_w_in"][l], "nn", BF16, "a_in")
            yc3 = _conv_fwd(ha, w["a_conv"][l], "a_conv")
            blk3 = pl.BlockSpec((3, th, HD), lambda i, h: (0, i, h))
            (qkv3,) = _rowcall(lambda b: jnp.stack(_gdn_prep(b[0].astype(F32), b[1].astype(F32), b[2].astype(F32))),
                               [yc3], [blk3], [_sds((3, t, A_WIDTH), BF16)], [blk3], head6, "a_prep")
            (bg,) = _rowcall(_gates, [ha, _pad128(w["a_A_log"][l], A_HEADS), _pad128(w["a_dt_bias"][l], A_HEADS)],
                             [_rows(TM, 128, A_BA_BLK), _shared((1, 128)), _shared((1, 128))],
                             [_sds((t, 128), F32)], [_rows(TM, 128)], (t // TM,), "a_gates")
            o_gdn, states, pinvs = _gdn_fwd(qkv3, bg, "a_gdn")
            (o_a,) = _rowcall(_outnorm_gate, [o_gdn, ha, w["a_out_norm"][l].reshape(1, HD)],
                              [pl.BlockSpec((th, HD), lambda i, h: (i, h)),
                               pl.BlockSpec((th, HD), lambda i, h: (i, 3 * A_HEADS + h)), _shared((1, HD))],
                              [_sds((t, A_WIDTH), BF16)], [pl.BlockSpec((th, HD), lambda i, h: (i, h))],
                              head6, "a_outnorm")
            o_m = _mem_fwd(ha, A_MQ_BLK, mkv[l], "mem_attn_a")
            sv.update(ha=ha, yc3=yc3, qkv3=qkv3, bg=bg, states=states, pinvs=pinvs, o_gdn=o_gdn)
            cat = jnp.concatenate([o_a, o_m], axis=1)
        else:
            j = l - N_A
            hb = _mm(n2, w["b_w_in"][j], "nn", BF16, "b_in")
            cqn = _rmsnorm_fwd(hb, w["b_q_norm"][j], "b_qnorm", 0, Q_LORA)
            qq = _mm(cqn, w["b_w_uq"][j], "nn", BF16, "b_uq")
            (qc,) = _rowcall(_rope_mix, [qq, qq, rope_c, rope_s],
                             [pl.BlockSpec((th, QK_CAT), lambda i, h: (i, h)),
                              pl.BlockSpec((th, QK_CAT), lambda i, h: (i, B_HEADS + h)),
                              pl.BlockSpec((th, QK_CAT), lambda i, h: (i, 0)),
                              pl.BlockSpec((th, QK_CAT), lambda i, h: (i, 0))],
                             [_sds((t, B_HEADS * QK_CAT), BF16)], [pl.BlockSpec((th, QK_CAT), lambda i, h: (i, h))],
                             head6, "b_qrope")
            o_b, lse, lse_row = _att_fwd(qc, kcat, vmla, "b_attn")
            o_m = _mem_fwd(hb, 1, mkv[l], "mem_attn_b")
            sv.update(hb=hb, cqn=cqn, qc=qc, o_b=o_b, lse=(lse, lse_row))
            cat = jnp.concatenate([o_b, o_m], axis=1)
        sv["cat"] = cat
        x = _mm(cat, w["w_out"][l], "nn", F32, "w_out", res=x)
        x, sv["ffn2"] = _ffn_fwd(x, w["ffn2_norm"][l], w["ffn2_w_gu"][l], w["ffn2_w_down"][l], "ffn2")
        saved.append(sv)
        if l == N_A - 1:
            x_kv = x
            nkv = _rmsnorm_fwd(x, w["kv_in_norm"], "kv_in_norm")
            ckr = _mm(nkv, w["w_dkv"], "nn", BF16, "kv_down")
            ckv_n = _rmsnorm_fwd(ckr, w["kv_lat_norm"], "kv_lat_norm", 0, KV_LORA)
            kvu = _mm(ckv_n, w["w_ukv"], "nn", BF16, "kv_up")
            vmla = kvu[:, B_HEADS * QK_CAT:]
            (kcat,) = _rowcall(_kcat, [kvu, ckr, ckr, rope_c, rope_s],
                               [pl.BlockSpec((th, QK_CAT), lambda i, h: (i, h)),
                                pl.BlockSpec((th, QK_CAT), lambda i, h: (i, 1)),
                                pl.BlockSpec((th, QK_CAT), lambda i, h: (i, 2)),
                                pl.BlockSpec((th, QK_CAT), lambda i, h: (i, 0)),
                                pl.BlockSpec((th, QK_CAT), lambda i, h: (i, 0))],
                               [_sds((t, B_HEADS * QK_CAT), BF16)],
                               [pl.BlockSpec((th, QK_CAT), lambda i, h: (i, h))], head6, "kv_cat")

    def loss_fn(xb, gb, tb):
        def f(xx, gg):
            e = _rms(xx, gg) - tb
            return 0.5 * jnp.sum(jnp.mean(e * e, axis=-1, keepdims=True), axis=0, keepdims=True)
        val, vjp = jax.vjp(f, xb, gb)
        dx, dg = vjp(jnp.ones((1, 1), F32))
        return dx, dg, val * jnp.ones((1, 128), F32)

    d, dfin, loss = _rowcall(loss_fn, [x, w["final_norm"].reshape(1, D), target],
                             [_rows(TM, D), _shared((1, D)), _rows(TM, D)],
                             [_sds((t, D), F32), _sds((1, D), F32), _sds((1, 128), F32)],
                             [_rows(TM, D), _shared((1, D)), _shared((1, 128))], (t // TM,), "loss_head", n_acc=2)
    g["final_norm"] = dfin[0]
    loss = loss[0, 0]

    for name in ("ffn1_norm", "ffn1_w_gu", "ffn1_w_down", "mix_norm", "ffn2_norm", "ffn2_w_gu", "ffn2_w_down",
                 "w_out", "w_mem_kv"):
        g[name] = [None] * DEPTH
    for name in ("a_w_in", "a_conv", "a_A_log", "a_dt_bias", "a_out_norm"):
        g[name] = [None] * N_A
    for name in ("b_w_in", "b_q_norm", "b_w_uq"):
        g[name] = [None] * N_B
    dmkv = [None] * DEPTH
    dkcat = []
    dvmla = []

    for l in reversed(range(DEPTH)):
        sv = saved[l]
        if l == N_A - 1:
            kq = pl.BlockSpec((th, QK_CAT), lambda i, h: (i, h))
            tab = pl.BlockSpec((th, QK_CAT), lambda i, h: (i, 0))

            def dk_fn(c, s, d0, d1):
                dk = d0.astype(F32) + d1.astype(F32)
                return dk, dk * c, dk * s

            dkn, dkr_h, dkrs_h = _rowcall(dk_fn, [rope_c, rope_s, dkcat[0], dkcat[1]], [tab, tab, kq, kq],
                                          [_sds((t, B_HEADS * QK_CAT), BF16)] + [_sds((B_HEADS, t, QK_CAT), BF16)] * 2,
                                          [kq] + [pl.BlockSpec((None, th, QK_CAT), lambda i, h: (h, i, 0))] * 2,
                                          head6, "kv_dcat")

            def sum6(a, b):
                return jnp.sum(a.astype(F32), axis=0), jnp.sum(b.astype(F32), axis=0)

            h6 = pl.BlockSpec((B_HEADS, TM, QK_CAT), lambda i: (0, i, 0))
            dkr, dkrs = _rowcall(sum6, [dkr_h, dkrs_h], [h6, h6], [_sds((t, QK_CAT), BF16)] * 2,
                                 [_rows(TM, QK_CAT)] * 2, (t // TM,), "kv_dkr")

            def addv(a, b):
                return a.astype(F32) + b.astype(F32)

            (dv,) = _rowcall(addv, dvmla, [_rows(TM, B_HEADS * HD)] * 2, [_sds((t, B_HEADS * HD), BF16)],
                             [_rows(TM, B_HEADS * HD)], (t // TM,), "kv_dv")
            dkvu = jnp.concatenate([dkn, dv], axis=1)
            g["w_ukv"] = _mm(ckv_n, dkvu, "tn", F32, "kv_up_dw")
            dckv_n = _mm(dkvu, w["w_ukv"], "nt", BF16, "kv_up_dx")

            def lat_bwd(cb, gb, dnb):
                return _vjp_fn(_rms, 2, (0, 1))(cb, gb, dnb)

            dckv, g["kv_lat_norm"] = _rowcall(lat_bwd, [ckr, w["kv_lat_norm"].reshape(1, KV_LORA), dckv_n],
                                              [_rows(TM, KV_LORA), _shared((1, KV_LORA)), _rows(TM, KV_LORA)],
                                              [_sds((t, KV_LORA), BF16), _sds((1, KV_LORA), F32)],
                                              [_rows(TM, KV_LORA), _shared((1, KV_LORA))], (t // TM,),
                                              "kv_lat_dnorm", n_acc=1)
            g["kv_lat_norm"] = g["kv_lat_norm"][0]
            dckr = jnp.concatenate([dckv, dkr, dkrs], axis=1)
            g["w_dkv"] = _mm(nkv, dckr, "tn", F32, "kv_down_dw")
            dnkv = _mm(dckr, w["w_dkv"], "nt", BF16, "kv_down_dx")
            d, g["kv_in_norm"] = _rmsnorm_bwd(x_kv, w["kv_in_norm"], dnkv, d, "kv_in_dnorm")

        d, g["ffn2_norm"][l], g["ffn2_w_gu"][l], g["ffn2_w_down"][l] = _ffn_bwd(
            d, sv["ffn2"], w["ffn2_norm"][l], w["ffn2_w_gu"][l], w["ffn2_w_down"][l], "ffn2b")
        g["w_out"][l] = _mm(sv["cat"], d, "tn", F32, "w_out_dw")
        dcat = _mm(d, w["w_out"][l], "nt", BF16, "w_out_dx")
        if l < N_A:
            ha, yc3, qkv3, states, o_gdn = sv["ha"], sv["yc3"], sv["qkv3"], sv["states"], sv["o_gdn"]
            dmq, dmkv[l] = _mem_bwd(ha, A_MQ_BLK, mkv[l], dcat, 3, "mem_attn_a_bwd")
            hblk = pl.BlockSpec((th, HD), lambda i, h: (i, h))
            do_gdn, dgate, dgain = _rowcall(
                _vjp_fn(_outnorm_gate, 3, (0, 1, 2)), [o_gdn, ha, w["a_out_norm"][l].reshape(1, HD), dcat],
                [hblk, pl.BlockSpec((th, HD), lambda i, h: (i, 3 * A_HEADS + h)), _shared((1, HD)), hblk],
                [_sds((t, A_WIDTH), BF16), _sds((t, A_WIDTH), BF16), _sds((1, HD), F32)],
                [hblk, hblk, _shared((1, HD))], head6, "a_outnorm_bwd", n_acc=1)
            g["a_out_norm"][l] = dgain[0]
            dqkv3, dbg6 = _gdn_bwd(qkv3, sv["bg"], states, sv["pinvs"], do_gdn, "a_gdn_bwd")

            def dgates(bab, alb, dtb, d6):
                return _vjp_fn(_gates, 3, (0, 1, 2))(bab, alb, dtb, jnp.sum(d6, axis=0))

            dba, dalog, ddt = _rowcall(
                dgates, [ha, _pad128(w["a_A_log"][l], A_HEADS), _pad128(w["a_dt_bias"][l], A_HEADS), dbg6],
                [_rows(TM, 128, A_BA_BLK), _shared((1, 128)), _shared((1, 128)),
                 pl.BlockSpec((A_HEADS, TM, 128), lambda i: (0, i, 0))],
                [_sds((t, 128), BF16), _sds((1, 128), F32), _sds((1, 128), F32)],
                [_rows(TM, 128), _shared((1, 128)), _shared((1, 128))], (t // TM,), "a_gates_bwd", n_acc=2)
            g["a_A_log"][l] = dalog[0, A_HEADS:2 * A_HEADS]
            g["a_dt_bias"][l] = ddt[0, A_HEADS:2 * A_HEADS]
            blk3 = pl.BlockSpec((3, th, HD), lambda i, h: (0, i, h))
            def dprep(b, db):
                return jnp.stack(_vjp_fn(_gdn_prep, 3, (0, 1, 2))(b[0], b[1], b[2], db[0], db[1], db[2]))

            (dyc3,) = _rowcall(dprep, [yc3, dqkv3], [blk3, blk3],
                               [_sds((3, t, A_WIDTH), BF16)], [blk3], head6, "a_prep_bwd")
            dqkv_in, g["a_conv"][l] = _conv_bwd(ha, w["a_conv"][l], dyc3, "a_conv_bwd")
            dha = jnp.concatenate([dqkv_in, dgate, dmq, dba], axis=1)
            g["a_w_in"][l] = _mm(sv["n2"], dha, "tn", F32, "a_in_dw")
            dn2 = _mm(dha, w["a_w_in"][l], "nt", BF16, "a_in_dx")
        else:
            j = l - N_A
            hb, cqn, qc, o_b, lse = sv["hb"], sv["cqn"], sv["qc"], sv["o_b"], sv["lse"]
            dmq, dmkv[l] = _mem_bwd(hb, 1, mkv[l], dcat, 3, "mem_attn_b_bwd")
            dqc, dkc, dvv = _att_bwd(qc, kcat, vmla, o_b, lse[0], lse[1], dcat, "b_attn_bwd")
            dkcat.append(dkc)
            dvmla.append(dvv)
            kq = pl.BlockSpec((th, QK_CAT), lambda i, h: (i, h))
            tab = pl.BlockSpec((th, QK_CAT), lambda i, h: (i, 0))

            def dq_fn(c, s, dq):
                dq = dq.astype(F32)
                return jnp.stack([dq * c, dq * s])

            (dqq,) = _rowcall(dq_fn, [rope_c, rope_s, dqc], [tab, tab, kq],
                              [_sds((2, t, B_HEADS * QK_CAT), BF16)],
                              [pl.BlockSpec((2, th, QK_CAT), lambda i, h: (0, i, h))], head6, "b_qrope_bwd")
            dqq = jnp.concatenate([dqq[0], dqq[1]], axis=1)
            g["b_w_uq"][j] = _mm(cqn, dqq, "tn", F32, "b_uq_dw")
            dcqn = _mm(dqq, w["b_w_uq"][j], "nt", BF16, "b_uq_dx")
            dcq, dqg = _rowcall(_vjp_fn(_rms, 2, (0, 1)), [hb, w["b_q_norm"][j].reshape(1, Q_LORA), dcqn],
                                [_rows(TM, Q_LORA), _shared((1, Q_LORA)), _rows(TM, Q_LORA)],
                                [_sds((t, Q_LORA), BF16), _sds((1, Q_LORA), F32)],
                                [_rows(TM, Q_LORA), _shared((1, Q_LORA))], (t // TM,), "b_qnorm_bwd", n_acc=1)
            g["b_q_norm"][j] = dqg[0]
            dhb = jnp.concatenate([dcq, dmq], axis=1)
            g["b_w_in"][j] = _mm(sv["n2"], dhb, "tn", F32, "b_in_dw")
            dn2 = _mm(dhb, w["b_w_in"][j], "nt", BF16, "b_in_dx")
        d, g["mix_norm"][l] = _rmsnorm_bwd(sv["x1"], w["mix_norm"][l], dn2, d, "mix_dnorm")
        d = mixer_done(l, g, d)
        d, g["ffn1_norm"][l], g["ffn1_w_gu"][l], g["ffn1_w_down"][l] = _ffn_bwd(
            d, sv["ffn1"], w["ffn1_norm"][l], w["ffn1_w_gu"][l], w["ffn1_w_down"][l], "ffn1b")
        d = layer_done(l, g, d)

    dmem_n = None
    for l in range(DEPTH):
        g["w_mem_kv"][l] = _mm(mem_n, dmkv[l], "tn", F32, f"mkv_dw{l}")
        dmem_n = _mm(dmkv[l], w["w_mem_kv"][l], "nt", F32, f"mkv_dx{l}", res=dmem_n)
    (_, gmn) = _rowcall(_vjp_fn(_rms, 2, (0, 1)), [mem, w["mem_norm"].reshape(1, D), dmem_n],
                        [_shared((N_MEM, D)), _shared((1, D)), _shared((N_MEM, D))],
                        [_sds((N_MEM, D), F32), _sds((1, D), F32)], [_shared((N_MEM, D)), _shared((1, D))],
                        (1,), "mem_dnorm")
    g["mem_norm"] = gmn[0]
    return loss, d, g


def _pad128(v, offset):
    return jnp.pad(v.astype(F32).reshape(1, -1), ((0, 0), (offset, 128 - offset - v.shape[0])))


def _rmsnorm_fwd_small(x, gain):
    r, w = x.shape
    (n,) = _rowcall(_rms, [x, gain.reshape(1, w)], [_shared((r, w)), _shared((1, w))],
                    [_sds((r, w), BF16)], [_shared((r, w))], (1,), "mem_norm")
    return n


def _exchange(srcs, gather, name):
    n = len(srcs)
    blks = [tuple(s.shape) if gather else tuple(s.shape[1:]) for s in srcs]

    def body(*refs):
        src_refs, out_refs = refs[:n], refs[n:2 * n]
        send_sems, recv_sems, local_sems = refs[2 * n:]
        x, y, c = lax.axis_index("x"), lax.axis_index("y"), lax.axis_index("c")
        me = 4 * x + 2 * y + c
        copies = []
        for k in range(1, N_DEV):
            px = (x + (k >> 2 & 1)) % 2
            py = (y + (k >> 1 & 1)) % 2
            pc = (c + (k & 1)) % 2
            peer = 4 * px + 2 * py + pc
            for a in range(n):
                cp = pltpu.make_async_remote_copy(
                    src_ref=src_refs[a] if gather else src_refs[a].at[peer], dst_ref=out_refs[a].at[me],
                    send_sem=send_sems.at[a, k - 1], recv_sem=recv_sems.at[a, k - 1],
                    device_id=(px, py, pc), device_id_type=pl.DeviceIdType.MESH)
                cp.start()
                copies.append(cp)
        for a in range(n):
            cp = pltpu.make_async_copy(src_refs[a] if gather else src_refs[a].at[me], out_refs[a].at[me],
                                       local_sems.at[a])
            cp.start()
            copies.append(cp)
        for cp in copies:
            cp.wait()

    return pl.pallas_call(
        body, out_shape=tuple(_sds((N_DEV,) + b, s.dtype) for b, s in zip(blks, srcs)),
        in_specs=[pl.BlockSpec(memory_space=pl.ANY)] * n, out_specs=tuple([pl.BlockSpec(memory_space=pl.ANY)] * n),
        scratch_shapes=[pltpu.SemaphoreType.DMA((n, N_DEV - 1)), pltpu.SemaphoreType.DMA((n, N_DEV - 1)),
                        pltpu.SemaphoreType.DMA((n,))],
        name=name)(*srcs)


_HBM = pl.BlockSpec(memory_space=pltpu.HBM)
_SEM = pl.BlockSpec(memory_space=pltpu.SEMAPHORE)
_EFFECT = pltpu.SideEffectType.DATAFLOW_SIDE_EFFECTING


def _split_copies(src_refs, land_refs, send_sems, recv_sems, gather):
    x, y, c = lax.axis_index("x"), lax.axis_index("y"), lax.axis_index("c")
    me = 4 * x + 2 * y + c
    copies = []
    for k in range(1, N_DEV):
        peer = ((x + (k >> 2 & 1)) % 2, (y + (k >> 1 & 1)) % 2, (c + (k & 1)) % 2)
        for a in range(len(src_refs)):
            i = (k - 1) * len(src_refs) + a
            src = src_refs[a] if gather else src_refs[a].at[4 * peer[0] + 2 * peer[1] + peer[2]]
            copies.append(pltpu.make_async_remote_copy(
                src_ref=src, dst_ref=land_refs[a].at[me], send_sem=send_sems[i],
                recv_sem=recv_sems[i], device_id=peer, device_id_type=pl.DeviceIdType.MESH))
    return copies


def _split_start(srcs, gather, name):
    n = len(srcs)
    srcs = [pltpu.with_memory_space_constraint(s, pltpu.HBM) for s in srcs]
    lands = [pltpu.with_memory_space_constraint(
        lax.empty(((N_DEV,) + s.shape) if gather else s.shape, s.dtype), pltpu.HBM) for s in srcs]

    ns = n * (N_DEV - 1)

    def body(*refs):
        sems = refs[2 * n:2 * n + 2 * ns]
        for cp in _split_copies(refs[:n], refs[n:2 * n], sems[:ns], sems[ns:], gather):
            cp.start()
        refs[-1][...] = jnp.zeros_like(refs[-1])

    outs = pl.pallas_call(
        body, name=name,
        out_shape=(*[pltpu.SemaphoreType.DMA(())] * (2 * ns), *[pltpu.HBM(a.shape, a.dtype) for a in srcs + lands],
                   _sds((8, 128), F32)),
        in_specs=[_HBM] * (2 * n),
        out_specs=(*[_SEM] * (2 * ns), *[_HBM] * (2 * n), pl.BlockSpec(memory_space=pltpu.VMEM)),
        input_output_aliases={i: 2 * ns + i for i in range(2 * n)},
        compiler_params=pltpu.CompilerParams(has_side_effects=_EFFECT))(*srcs, *lands)
    sems, rest = list(outs[:2 * ns]), outs[2 * ns:]
    return sems[:ns], sems[ns:], list(rest[:n]), list(rest[n:2 * n]), rest[-1]


def _split_wait(flight, after, gather, name):
    send_sems, recv_sems, srcs, lands, _ = flight
    n = len(srcs)

    ns = len(send_sems)

    def body(*refs):
        sems = refs[2 * n:2 * n + 2 * ns]
        for cp in _split_copies(refs[:n], refs[n:2 * n], sems[:ns], sems[ns:], gather):
            cp.wait_send()
            cp.wait_recv()

    outs = pl.pallas_call(
        body, name=name, out_shape=tuple(pltpu.HBM(a.shape, a.dtype) for a in srcs + lands),
        in_specs=[_HBM] * (2 * n) + [_SEM] * (2 * ns) + [pl.BlockSpec(memory_space=pl.ANY)],
        out_specs=tuple([_HBM] * (2 * n)), input_output_aliases={i: i for i in range(2 * n)},
        compiler_params=pltpu.CompilerParams(has_side_effects=_EFFECT))(*srcs, *lands, *send_sems, *recv_sems, after)
    return list(outs[n:])


def _reduce_adamw(parts, wp, mp, vp, name):
    r, cols = wp.shape
    tr = _tile_rows(r, cols)
    c1 = 1.0 - ADAM_B1 ** ADAM_STEP
    c2 = 1.0 - ADAM_B2 ** ADAM_STEP

    def fn(pb, wb, mb, vb):
        gsum = pb[0].astype(F32)
        for j in range(1, N_DEV):
            gsum = gsum + pb[j].astype(F32)
        m_new = ADAM_B1 * mb + (1.0 - ADAM_B1) * gsum
        v_new = ADAM_B2 * vb + (1.0 - ADAM_B2) * (gsum * gsum)
        delta = -ADAM_LR * ((m_new / c1) / (jnp.sqrt(v_new / c2) + ADAM_EPS) + ADAM_WD * wb)
        return gsum, delta, m_new, v_new

    row = _rows(tr, cols)
    return _rowcall(fn, [parts, wp, mp, vp],
                    [pl.BlockSpec((N_DEV, tr, cols), lambda i: (0, i, 0)), row, row, row],
                    [_sds((r, cols), F32)] * 4, [row] * 4, (r // tr,), name)


def _tile_rows(r, cols):
    for t in (512, 256, 128, 64, 32, 16):
        if r % t == 0 and t * cols <= 160 * 1024:
            return t
    return r


def _pack(arrs):
    flat = jnp.concatenate([a.reshape(-1).astype(F32) for a in arrs])
    n = flat.shape[0]
    unit = PACK_W * PACK_ROWS
    tot = -(-n // unit) * unit
    return jnp.pad(flat, (0, tot - n)).reshape(tot // PACK_W, PACK_W)


def _unpack(buf, shapes):
    out, off = [], 0
    flat = buf.reshape(-1)
    for s in shapes:
        n = int(np.prod(s))
        out.append(flat[off:off + n].reshape(s))
        off += n
    return out


def _as2d(a):
    return a.reshape(-1, a.shape[-1])


_SHARDED = ["ffn1_w_gu", "ffn1_w_down", "ffn2_w_gu", "ffn2_w_down", "w_out", "w_mem_kv", "a_w_in", "a_conv",
            "b_w_in", "b_w_uq", "w_dkv", "w_ukv"]
_COL_SHARDED = {"ffn1_w_gu", "ffn2_w_gu", "a_conv", "b_w_uq", "w_ukv"}
_LAYERED = {"ffn1_w_gu": DEPTH, "ffn1_w_down": DEPTH, "ffn2_w_gu": DEPTH, "ffn2_w_down": DEPTH, "w_out": DEPTH,
            "w_mem_kv": DEPTH, "a_w_in": N_A, "a_conv": N_A, "b_w_in": N_B, "b_w_uq": N_B}
_GATHER_FIRST = [("ffn1_w_gu", 0, 1), ("ffn1_w_down", 0, 1)]
_GATHER_MIX0 = [("ffn2_w_gu", 0, 1), ("ffn2_w_down", 0, 1), ("w_out", 0, 1), ("a_w_in", 0, 1), ("a_conv", 0, 1),
                ("w_mem_kv", 0, DEPTH)]
_GATHER_REST = [("ffn1_w_gu", 1, DEPTH), ("ffn1_w_down", 1, DEPTH), ("ffn2_w_gu", 1, DEPTH),
                ("ffn2_w_down", 1, DEPTH), ("w_out", 1, DEPTH), ("a_w_in", 1, N_A), ("a_conv", 1, N_A),
                ("b_w_in", 0, N_B), ("b_w_uq", 0, N_B), ("w_dkv", None, None), ("w_ukv", None, None)]
_SCATTER_HI = [("ffn1_w_gu", 2, DEPTH), ("ffn1_w_down", 2, DEPTH), ("ffn2_w_gu", 2, DEPTH), ("ffn2_w_down", 2, DEPTH),
               ("w_out", 2, DEPTH), ("b_w_in", 0, N_B), ("b_w_uq", 0, N_B)]
_SCATTER_MID = [("ffn1_w_gu", 1, 2), ("ffn1_w_down", 1, 2), ("ffn2_w_gu", 1, 2), ("ffn2_w_down", 1, 2),
                ("w_out", 1, 2), ("a_w_in", 1, N_A), ("a_conv", 1, N_A), ("w_dkv", None, None), ("w_ukv", None, None)]
_SCATTER_MIX0 = [("ffn2_w_gu", 0, 1), ("ffn2_w_down", 0, 1), ("w_out", 0, 1), ("a_w_in", 0, 1), ("a_conv", 0, 1)]
_SCATTER_LO = [("ffn1_w_gu", 0, 1), ("ffn1_w_down", 0, 1), ("w_mem_kv", 0, DEPTH)]
_REPLICATED = ["ffn1_norm", "mix_norm", "ffn2_norm", "mem_norm", "a_A_log", "a_dt_bias", "a_out_norm", "b_q_norm",
               "kv_in_norm", "kv_lat_norm", "final_norm"]
_WEIGHTS = ["ffn1_norm", "ffn1_w_gu", "ffn1_w_down", "mix_norm", "ffn2_norm", "ffn2_w_gu", "ffn2_w_down", "w_out",
            "mem_norm", "w_mem_kv", "a_w_in", "a_conv", "a_A_log", "a_dt_bias", "a_out_norm", "b_w_in", "b_q_norm",
            "b_w_uq", "kv_in_norm", "w_dkv", "kv_lat_norm", "w_ukv", "final_norm"]


def _full_from_shards(name, sh):
    if name in ("ffn1_w_gu", "ffn2_w_gu"):
        return sh
    if name in ("ffn1_w_down", "ffn2_w_down"):
        return sh.reshape(4, FF_SHARD, D)
    if name in _COL_SHARDED:
        return jnp.moveaxis(sh, 0, -2).reshape(sh.shape[1:-1] + (N_DEV * sh.shape[-1],))
    return sh.reshape((N_DEV * sh.shape[1],) + sh.shape[2:])


def _shards_from_full(name, full):
    if name in ("ffn1_w_gu", "ffn2_w_gu"):
        return full
    if name in ("ffn1_w_down", "ffn2_w_down"):
        return full.reshape(N_DEV, D_FF // N_DEV, D)
    if name in _COL_SHARDED:
        r, cc = full.shape
        return jnp.moveaxis(full.reshape(r, N_DEV, cc // N_DEV), 1, 0)
    return full.reshape((N_DEV, full.shape[0] // N_DEV) + full.shape[1:])


def kernel(x, mem, positions, ffn1_norm, ffn1_w_gu, ffn1_w_down, mix_norm, ffn2_norm, ffn2_w_gu, ffn2_w_down, w_out, mem_norm, w_mem_kv, a_w_in, a_conv, a_A_log, a_dt_bias, a_out_norm, b_w_in, b_q_norm, b_w_uq, kv_in_norm, w_dkv, kv_lat_norm, w_ukv, final_norm, loss_target, m_ffn1_norm, m_ffn1_w_gu, m_ffn1_w_down, m_mix_norm, m_ffn2_norm, m_ffn2_w_gu, m_ffn2_w_down, m_w_out, m_mem_norm, m_w_mem_kv, m_a_w_in, m_a_conv, m_a_A_log, m_a_dt_bias, m_a_out_norm, m_b_w_in, m_b_q_norm, m_b_w_uq, m_kv_in_norm, m_w_dkv, m_kv_lat_norm, m_w_ukv, m_final_norm, v_ffn1_norm, v_ffn1_w_gu, v_ffn1_w_down, v_mix_norm, v_ffn2_norm, v_ffn2_w_gu, v_ffn2_w_down, v_w_out, v_mem_norm, v_w_mem_kv, v_a_w_in, v_a_conv, v_a_A_log, v_a_dt_bias, v_a_out_norm, v_b_w_in, v_b_q_norm, v_b_w_uq, v_kv_in_norm, v_w_dkv, v_kv_lat_norm, v_w_ukv, v_final_norm):
    loc = dict(locals())
    wl = {n: loc[n] for n in _WEIGHTS}
    ml = {n: loc["m_" + n] for n in _WEIGHTS}
    vl = {n: loc["v_" + n] for n in _WEIGHTS}

    me = 4 * lax.axis_index("x") + 2 * lax.axis_index("y") + lax.axis_index("c")
    w = {n: wl[n] for n in _REPLICATED}
    for n in _SHARDED:
        w[n] = [None] * _LAYERED[n] if n in _LAYERED else None
    back = {}

    def src_of(item):
        n, lo, hi = item
        return (wl[n] if lo is None else wl[n][lo:hi]).astype(BF16)

    def install(items, pieces):
        for (n, lo, hi), p in zip(items, pieces):
            if lo is None:
                w[n] = _full_from_shards(n, p)
            else:
                for l in range(lo, hi):
                    w[n][l] = _full_from_shards(n, p[:, l - lo])
        for n, lo, hi in items:
            if n == "a_w_in":
                for l in range(lo, hi):
                    w[n][l], back[(n, l)] = _ext_and_back(_a_in_ext, w[n][l])
            elif n == "a_conv":
                for l in range(lo, hi):
                    w[n][l] = w[n][l].astype(F32)
            elif n == "b_w_uq":
                for l in range(lo, hi):
                    w[n][l], back[(n, l)] = _ext_and_back(_uq_ext, w[n][l])
            elif n == "w_dkv":
                w[n], back[n] = _ext_and_back(_dkv_ext, w[n])
            elif n == "w_ukv":
                w[n], back[n] = _ext_and_back(_ukv_ext, w[n])

    first = _exchange([src_of(it) for it in _GATHER_FIRST], True, "gather_first")
    later = {key: [src_of(it) for it in items] for key, items in (("mix0", _GATHER_MIX0), ("rest", _GATHER_REST))}
    first, later = lax.optimization_barrier((first, later))
    install(_GATHER_FIRST, first)
    flights = {}
    for key, items in (("mix0", _GATHER_MIX0), ("rest", _GATHER_REST)):
        srcs = later[key]
        flights[key] = (items, srcs, _split_start(srcs, True, f"gather_{key}_start"))
        w["ffn1_norm"] = w["ffn1_norm"] + flights[key][2][-1][0, 0]

    def arrive(key, stream):
        items, srcs, fl = flights[key]
        lands = _split_wait(fl, stream, True, f"gather_{key}_wait")
        install(items, [lax.dynamic_update_slice(ld, s[None], (me,) + (0,) * s.ndim) for ld, s in zip(lands, srcs)])

    def layer_start(l, stream):
        if l == 1:
            arrive("rest", stream)

    def mixer_start(l, stream):
        if l == 0:
            arrive("mix0", stream)

    def grad_src(item, g):
        n, lo, hi = item

        def one(l):
            gl = g[n] if l is None else g[n][l]
            key = n if l is None else (n, l)
            if key in back:
                gl = back[key](gl)
            return _shards_from_full(n, gl).astype(BF16)

        return one(None) if lo is None else jnp.stack([one(l) for l in range(lo, hi)], axis=1)

    sent = {}

    def depart(key, items, g, d):
        srcs = [grad_src(it, g) for it in items]
        sent[key] = (items, srcs, _split_start(srcs, False, f"scatter_{key}_start"))
        return d + sent[key][2][-1][0, 0]

    def layer_done(l, g, d):
        if l == 2:
            return depart("hi", _SCATTER_HI, g, d)
        if l == 1:
            return depart("mid", _SCATTER_MID, g, d)
        return d

    def mixer_done(l, g, d):
        return depart("mix0", _SCATTER_MIX0, g, d) if l == 0 else d

    loss, dx, g = _local_step(x[0], mem[0], positions[0], loss_target[0], w, layer_start, mixer_start,
                              mixer_done, layer_done)
    loss = lax.psum(loss, ("x", "y", "c"))

    lo_srcs = [grad_src(it, g) for it in _SCATTER_LO]
    pieces = {n: [] for n in _SHARDED}
    for (n, lo, hi), p in zip(_SCATTER_LO, _exchange(lo_srcs, False, "scatter_grads")):
        pieces[n].append((lo, p))
    for key in ("mix0", "mid", "hi"):
        items, srcs, fl = sent[key]
        lands = _split_wait(fl, dx, False, f"scatter_{key}_wait")
        for (n, lo, hi), s, ld in zip(items, srcs, lands):
            mine = lax.dynamic_slice(s, (me,) + (0,) * (s.ndim - 1), (1,) + s.shape[1:])
            pieces[n].append((lo, lax.dynamic_update_slice(ld, mine, (me,) + (0,) * (s.ndim - 1))))
    out = {}
    for n in _SHARDED:
        ps = [p for _, p in sorted(pieces[n], key=lambda e: -1 if e[0] is None else e[0])]
        p = ps[0] if len(ps) == 1 else jnp.concatenate(ps, axis=1)
        shape = wl[n].shape
        res = _reduce_adamw(p.reshape(N_DEV, -1, shape[-1]), _as2d(wl[n]), _as2d(ml[n]), _as2d(vl[n]), "adamw_" + n)
        for kind, buf in zip(("grad", "delta", "new_m", "new_v"), res):
            out[(kind, n)] = buf.reshape(shape)

    rep_shapes = [wl[n].shape for n in _REPLICATED]
    grep = [jnp.stack(g[n]) if isinstance(g[n], list) else g[n] for n in _REPLICATED]
    (rparts,) = _exchange([_pack(grep)], True, "gather_small_grads")
    res = _reduce_adamw(rparts, _pack([wl[n] for n in _REPLICATED]), _pack([ml[n] for n in _REPLICATED]),
                        _pack([vl[n] for n in _REPLICATED]), "adamw_replicated")
    for kind, buf in zip(("grad", "delta", "new_m", "new_v"), res):
        for n, a in zip(_REPLICATED, _unpack(buf, rep_shapes)):
            out[(kind, n)] = a

    return (loss, dx[None], *[out[("grad", n)] for n in _WEIGHTS], *[out[("delta", n)] for n in _WEIGHTS],
            *[out[("new_m", n)] for n in _WEIGHTS], *[out[("new_v", n)] for n in _WEIGHTS])
```

```python
import functools

import numpy as np
import jax
import jax.numpy as jnp
from jax import lax
from jax.experimental import pallas as pl
from jax.experimental.pallas import tpu as pltpu

F32 = jnp.float32
BF16 = jnp.bfloat16

N_DEV = 8
D = 1024
D_FF = 2816
FF_SHARD = 2 * D_FF // N_DEV
DEPTH = 4
N_A = 2
N_B = 2
EPS = 1e-6
CHUNK = 64
GROUP = 256
GDN_HPS = 3
A_HEADS = 6
HD = 128
A_WIDTH = A_HEADS * HD
B_HEADS = 6
QK_NOPE = 128
QK_ROPE = 64
QK_CAT = 256
Q_LORA = 256
KV_LORA = 256
MEM_HEADS = 4
MEM_HD = 64
MEM_W = 256
N_MEM = 256
ROPE_THETA = 10000.0
ATT_SCALE = (QK_NOPE + QK_ROPE) ** -0.5
LN2 = 0.6931471805599453
Q_PRESCALE = ATT_SCALE / LN2
A_IN = 4 * A_WIDTH + 2 * A_HEADS + MEM_W
A_MQ_BLK = 4 * A_WIDTH // MEM_W
A_BA_BLK = (4 * A_WIDTH + MEM_W) // 128

ADAM_LR = 0.001
ADAM_B1 = 0.9
ADAM_B2 = 0.999
ADAM_EPS = 1e-08
ADAM_WD = 0.01
ADAM_STEP = 10

VMEM_LIMIT = 56 * 1024 * 1024
TM = 512
TMM = 1024
TMF = 2048
TMH = 2048
ATT_BQ = 1024
PACK_W = 1024
PACK_ROWS = 32


def _cparams(sem):
    return pltpu.CompilerParams(dimension_semantics=sem, vmem_limit_bytes=VMEM_LIMIT)


_DIMS = {"nn": ((1,), (0,)), "nt": ((1,), (1,)), "tn": ((0,), (0,))}


def _dot(a, b, dims="nn"):
    return lax.dot_general(a.astype(BF16), b.astype(BF16), (_DIMS[dims], ((), ())),
                           preferred_element_type=F32)


def _matmul(a, b, *, dims, grid, a_spec, b_spec, o_spec, out_shape, name, scale=1.0,
            res=None, res_spec=None, a_fn=None, epilogue=None, acc_shape=None, chunks=1):
    nk = grid[-1]
    kax = len(grid) - 1
    if acc_shape is None:
        acc_shape = tuple(s for s in o_spec.block_shape if s is not None)
    a_rows = a_spec.block_shape[-2] // chunks

    def rows_of(ref, c):
        rs = slice(c * a_rows, (c + 1) * a_rows)
        return ref[:, rs, :] if len(ref.shape) == 3 else ref[rs, :]

    def body(*refs):
        if res is None:
            a_ref, b_ref, o_ref, acc = refs
            r_ref = None
        else:
            a_ref, b_ref, r_ref, o_ref, acc = refs
        k = pl.program_id(kax)

        if chunks > 1 and nk == 1 and dims != "tn":
            for c in range(chunks):
                a_c = rows_of(a_ref, c)
                y = _dot(a_c if a_fn is None else a_fn(a_c), b_ref[...], dims) * scale
                rs = slice(c * a_rows, (c + 1) * a_rows)
                y = epilogue(y, rows_of(r_ref, c))
                if len(o_ref.shape) == 3:
                    o_ref[:, rs, :] = y.astype(o_ref.dtype)
                else:
                    o_ref[rs, :] = y.astype(o_ref.dtype)
            return

        @pl.when(k == 0)
        def _():
            acc[...] = jnp.zeros_like(acc)

        for c in range(chunks):
            a_c = rows_of(a_ref, c) if chunks > 1 else a_ref[...]
            a_c = a_c if a_fn is None else a_fn(a_c)
            rs = slice(c * a_rows, (c + 1) * a_rows)
            if chunks == 1:
                acc[...] += _dot(a_c, b_ref[...], dims)
            elif dims == "tn":
                acc[...] += _dot(a_c, b_ref[rs, :], dims)
            else:
                acc[rs, :] += _dot(a_c, b_ref[...], dims)

        @pl.when(k == nk - 1)
        def _():
            y = acc[...] * scale
            if epilogue is not None:
                y = epilogue(y, r_ref[...])
            elif r_ref is not None:
                y = y + r_ref[...].astype(F32)
            o_ref[...] = y.astype(o_ref.dtype)

    args = [a, b] + ([res] if res is not None else [])
    specs = [a_spec, b_spec] + ([res_spec] if res is not None else [])
    sem = ("parallel",) * kax + ("arbitrary",)
    return pl.pallas_call(
        body, out_shape=out_shape, grid=grid, in_specs=specs, out_specs=o_spec,
        scratch_shapes=[pltpu.VMEM(acc_shape, F32)], name=name, compiler_params=_cparams(sem))(*args)


def _tile(n, cap):
    if n <= cap:
        return n
    t = cap - cap % 128
    while t >= 128:
        if n % t == 0:
            return t
        t -= 128
    raise ValueError(f"no tile for {n}")


def _mm(a, b, dims, out_dtype, name, scale=1.0, res=None):
    if dims == "tn":
        kk, m = a.shape
        n = b.shape[1]
        tk, tn = _tile(kk, TMM), _tile(n, 1152)
        return _matmul(a, b, dims=dims, grid=(1, n // tn, kk // tk),
                       a_spec=pl.BlockSpec((tk, m), lambda i, j, k: (k, 0)),
                       b_spec=pl.BlockSpec((tk, tn), lambda i, j, k: (k, j)),
                       o_spec=pl.BlockSpec((m, tn), lambda i, j, k: (0, j)),
                       out_shape=jax.ShapeDtypeStruct((m, n), out_dtype), name=name, scale=scale)
    m, kk = a.shape
    n = b.shape[1] if dims == "nn" else b.shape[0]
    tm, tn, tk = _tile(m, TMM), _tile(n, 1152), _tile(kk, 1536)
    if dims == "nn":
        b_spec = pl.BlockSpec((tk, tn), lambda i, j, k: (k, j))
    else:
        b_spec = pl.BlockSpec((tn, tk), lambda i, j, k: (j, k))
    o_spec = pl.BlockSpec((tm, tn), lambda i, j, k: (i, j))
    return _matmul(a, b, dims=dims, grid=(m // tm, n // tn, kk // tk),
                   a_spec=pl.BlockSpec((tm, tk), lambda i, j, k: (i, k)), b_spec=b_spec, o_spec=o_spec,
                   out_shape=jax.ShapeDtypeStruct((m, n), out_dtype), name=name, scale=scale,
                   res=res, res_spec=o_spec if res is not None else None)


def _rowcall(fn, args, in_specs, out_shapes, out_specs, grid, name, n_acc=0):
    n_in, n_out = len(args), len(out_shapes)

    def body(*refs):
        outs = fn(*[r[...] for r in refs[:n_in]])
        if not isinstance(outs, (tuple, list)):
            outs = (outs,)
        first = pl.program_id(0) == 0
        for ax in range(1, len(grid)):
            first = jnp.logical_and(first, pl.program_id(ax) == 0)
        for idx, (o_ref, val) in enumerate(zip(refs[n_in:], outs)):
            if idx >= n_out - n_acc:
                @pl.when(first)
                def _(o_ref=o_ref):
                    o_ref[...] = jnp.zeros_like(o_ref)

                o_ref[...] += val.astype(o_ref.dtype)
            else:
                o_ref[...] = val.astype(o_ref.dtype)

    sem = (("arbitrary",) if n_acc else ("parallel",)) * len(grid)
    res = pl.pallas_call(body, out_shape=tuple(out_shapes), grid=grid, in_specs=list(in_specs),
                         out_specs=tuple(out_specs), name=name, compiler_params=_cparams(sem))(*args)
    return res


def _vjp_fn(fn, n_in, wrt):
    def bwd(*blocks):
        ins = [b.astype(F32) for b in blocks[:n_in]]
        cts = [c.astype(F32) for c in blocks[n_in:]]
        outs, vjp = jax.vjp(fn, *ins)
        if isinstance(outs, (tuple, list)):
            grads = vjp(tuple(cts))
        else:
            grads = vjp(cts[0])
        return tuple(grads[i] for i in wrt)
    return bwd


def _sds(shape, dtype):
    return jax.ShapeDtypeStruct(tuple(shape), dtype)


def _rows(tm, w, col=0):
    return pl.BlockSpec((tm, w), lambda i, *_: (i, col))


def _shared(shape):
    nd = len(shape)
    return pl.BlockSpec(tuple(shape), lambda *_: (0,) * nd)


def _rms(x, g):
    return x * lax.rsqrt(jnp.mean(x * x, axis=-1, keepdims=True) + EPS) * g


def _silu(x):
    return x * jax.nn.sigmoid(x)


def _swiglu_pair(gu):
    return _silu(gu[0].astype(F32)) * gu[1].astype(F32)


def _swiglu_bwd(dh, gu):
    g, u = gu[0].astype(F32), gu[1].astype(F32)
    sg = jax.nn.sigmoid(g)
    return jnp.stack([dh * u * sg * (1.0 + g * (1.0 - sg)), dh * g * sg])


def _gdn_prep(q, k, v):
    q, k, v = _silu(q), _silu(k), _silu(v)
    q = q * lax.rsqrt(jnp.sum(q * q, axis=-1, keepdims=True) + EPS) * (HD ** -0.5)
    k = k * lax.rsqrt(jnp.sum(k * k, axis=-1, keepdims=True) + EPS)
    return q, k, v


def _gates(ba, a_log, dt_bias):
    lane = lax.broadcasted_iota(jnp.int32, ba.shape, 1)
    beta = jax.nn.sigmoid(ba)
    z = ba + dt_bias
    softplus = jnp.maximum(z, 0.0) + jnp.log(1.0 + jnp.exp(-jnp.abs(z)))
    g = -jnp.exp(a_log) * softplus
    return jnp.where(lane < A_HEADS, beta, jnp.where(lane < 2 * A_HEADS, g, 0.0))


def _outnorm_gate(o, gate, gain):
    return _rms(o, gain) * _silu(gate)


def _memattn(q, k, v):
    lane = lax.shift_right_logical(lax.broadcasted_iota(jnp.int32, (1, MEM_W), 1), 6)
    out = jnp.zeros(q.shape, F32)
    for h in range(MEM_HEADS):
        mh = (lane == h).astype(F32)
        s = _dot(q * mh, k, "nt") * (MEM_HD ** -0.5)
        s = s - lax.stop_gradient(jnp.max(s, axis=-1, keepdims=True))
        p = jnp.exp(s)
        p = p / jnp.sum(p, axis=-1, keepdims=True)
        out = out + _dot(p, v * mh)
    return out


def _rope_mix(a, a_sw, c, s):
    return (a * c + a_sw * s) * Q_PRESCALE


def _kcat(kn, kr, kr_sw, c, s):
    return kn + kr * c + kr_sw * s


@jax.custom_vjp
def _neumann_inverse(nmat):
    n = nmat.shape[0]
    eye = (lax.broadcasted_iota(jnp.int32, (n, n), 0) == lax.broadcasted_iota(jnp.int32, (n, n), 1)).astype(F32)
    pinv = eye + nmat
    npow = nmat
    for _ in range(5):
        npow = _dot(npow, npow)
        pinv = pinv + _dot(pinv, npow)
    return pinv


def _neumann_inverse_fwd(nmat):
    pinv = _neumann_inverse(nmat)
    return pinv, pinv


def _neumann_inverse_bwd(pinv, ct):
    return (_dot(_dot(pinv, ct, "tn"), pinv, "nt"),)


_neumann_inverse.defvjp(_neumann_inverse_fwd, _neumann_inverse_bwd)


@jax.custom_vjp
def _known_inverse(nmat, pinv):
    return pinv


def _known_inverse_fwd(nmat, pinv):
    return pinv, pinv


def _known_inverse_bwd(pinv, ct):
    return _dot(_dot(pinv, ct, "tn"), pinv, "nt"), jnp.zeros_like(pinv)


_known_inverse.defvjp(_known_inverse_fwd, _known_inverse_bwd)


def _gdn_local(q, k, v, beta, gcol, grow, pinv_kept=None):
    n = GROUP
    ri = lax.broadcasted_iota(jnp.int32, (n, n), 0)
    ci = lax.broadcasted_iota(jnp.int32, (n, n), 1)
    same = lax.shift_right_logical(ri, 6) == lax.shift_right_logical(ci, 6)
    lower = jnp.logical_and(same, ci <= ri)
    strict = jnp.logical_and(same, ci < ri)
    gc_col = jnp.sum(lower.astype(F32) * grow, axis=1, keepdims=True)
    gc_row = jnp.sum(jnp.logical_and(same, ri <= ci).astype(F32) * gcol, axis=0, keepdims=True)
    glast = jnp.sum(same.astype(F32) * grow, axis=1, keepdims=True)
    decay = jnp.where(lower, jnp.exp(jnp.where(lower, gc_col - gc_row, 0.0)), 0.0)
    kb = k * beta
    nmat = -jnp.where(strict, _dot(kb, k, "nt") * decay, 0.0)
    pinv = _neumann_inverse(nmat) if pinv_kept is None else _known_inverse(nmat, pinv_kept)
    e_gc = jnp.exp(gc_col)
    u = _dot(pinv, v * beta)
    w = _dot(pinv, kb * e_gc)
    qk = _dot(q, k, "nt") * decay
    fold = (jnp.bitwise_and(lax.broadcasted_iota(jnp.int32, (n, CHUNK), 0), CHUNK - 1)
            == lax.broadcasted_iota(jnp.int32, (n, CHUNK), 1)).astype(F32)
    qk_c = _dot(qk, fold)
    q_dec = q * e_gc
    k_dec = k * jnp.exp(glast - gc_col)
    dmat = jnp.exp(glast) * jnp.ones((1, HD), F32)
    if pinv_kept is None:
        return u, w, q_dec, k_dec, qk_c, dmat, pinv
    return u, w, q_dec, k_dec, qk_c, dmat


def _gdn_step(s, w_c, u_c, qd_c, kd_c, qk_c, d_c):
    v_new = u_c - _dot(w_c, s)
    out = _dot(qd_c, s) + _dot(qk_c, v_new)
    d_row = jnp.mean(d_c, axis=0, keepdims=True)
    s_new = s * d_row + _dot(kd_c, v_new, "tn")
    return s_new, out


def _rmsnorm_fwd(x, gain, name, col=0, width=None):
    t = x.shape[0]
    w = width or x.shape[1]
    (n,) = _rowcall(_rms, [x, gain.reshape(1, w)], [_rows(TM, w, col), _shared((1, w))],
                    [_sds((t, w), BF16)], [_rows(TM, w)], (t // TM,), name)
    return n


def _rmsnorm_bwd(x, gain, dn, dres, name):
    t, w = x.shape
    fn = _vjp_fn(_rms, 2, (0, 1))

    def bwd(xb, gb, dnb, drb):
        dx, dg = fn(xb, gb, dnb)
        return dx + drb, dg

    dx, dg = _rowcall(bwd, [x, gain.reshape(1, w), dn, dres],
                      [_rows(TM, w), _shared((1, w)), _rows(TM, w), _rows(TM, w)],
                      [_sds((t, w), F32), _sds((1, w), F32)], [_rows(TM, w), _shared((1, w))],
                      (t // TM,), name, n_acc=1)
    return dx, dg[0]


def _ffn_fwd(x, gain, wgu8, wd4, tag):
    t = x.shape[0]
    nt = t // TMM
    tf = min(TMF, t)
    n = _rmsnorm_fwd(x, gain, tag + "_norm")
    gu = _matmul(n, wgu8, dims="nn", grid=(N_DEV, t // tf, 1),
                 a_spec=pl.BlockSpec((tf, D), lambda j, i, k: (i, 0)),
                 b_spec=pl.BlockSpec((None, D, FF_SHARD), lambda j, i, k: (j, 0, 0)),
                 o_spec=pl.BlockSpec((None, tf, FF_SHARD), lambda j, i, k: (j, i, 0)),
                 out_shape=_sds((N_DEV, t, FF_SHARD), BF16), name=tag + "_gu")
    gu = gu.reshape(2, 4, t, FF_SHARD)
    y = _matmul(gu, wd4, dims="nn", grid=(nt, 1, 4), a_fn=_swiglu_pair, chunks=4,
                a_spec=pl.BlockSpec((2, None, TMM, FF_SHARD), lambda i, j, k: (0, k, i, 0)),
                b_spec=pl.BlockSpec((None, FF_SHARD, D), lambda i, j, k: (k, 0, 0)),
                o_spec=pl.BlockSpec((TMM, D), lambda i, j, k: (i, 0)),
                out_shape=_sds((t, D), F32), name=tag + "_down", scale=0.5,
                res=x, res_spec=pl.BlockSpec((TMM, D), lambda i, j, k: (i, 0)))
    return y, (x, n, gu)


def _ffn_bwd(d, saved, gain, wgu8, wd4, tag):
    x, n, gu = saved
    t = x.shape[0]
    nt = t // TMM
    dgu = _matmul(d, wd4, dims="nt", grid=(4, nt, 1),
                  a_spec=pl.BlockSpec((TMM, D), lambda j, i, k: (i, 0)),
                  b_spec=pl.BlockSpec((None, FF_SHARD, D), lambda j, i, k: (j, 0, 0)),
                  o_spec=pl.BlockSpec((2, None, TMM, FF_SHARD), lambda j, i, k: (0, j, i, 0)),
                  out_shape=_sds((2, 4, t, FF_SHARD), BF16), name=tag + "_dgu", scale=0.5,
                  res=gu, res_spec=pl.BlockSpec((2, None, TMM, FF_SHARD), lambda j, i, k: (0, j, i, 0)),
                  epilogue=_swiglu_bwd, acc_shape=(TMM, FF_SHARD))
    tf = min(TMF, t)
    nf = t // tf
    dwd4 = _matmul(gu, d, dims="tn", grid=(4, 1, nf), a_fn=_swiglu_pair, chunks=4,
                   a_spec=pl.BlockSpec((2, None, tf, FF_SHARD), lambda j, i, k: (0, j, k, 0)),
                   b_spec=pl.BlockSpec((tf, D), lambda j, i, k: (k, 0)),
                   o_spec=pl.BlockSpec((None, FF_SHARD, D), lambda j, i, k: (j, 0, 0)),
                   out_shape=_sds((4, FF_SHARD, D), F32), name=tag + "_dwd", scale=0.5)
    dgu = dgu.reshape(N_DEV, t, FF_SHARD)
    dwgu8 = _matmul(n, dgu, dims="tn", grid=(N_DEV, 1, nf),
                    a_spec=pl.BlockSpec((tf, D), lambda j, i, k: (k, 0)),
                    b_spec=pl.BlockSpec((None, tf, FF_SHARD), lambda j, i, k: (j, k, 0)),
                    o_spec=pl.BlockSpec((None, D, FF_SHARD), lambda j, i, k: (j, 0, 0)),
                    out_shape=_sds((N_DEV, D, FF_SHARD), F32), name=tag + "_dwgu")
    dn = _matmul(dgu, wgu8, dims="nt", grid=(nf, 1, N_DEV),
                 a_spec=pl.BlockSpec((None, tf, FF_SHARD), lambda i, j, k: (k, i, 0)),
                 b_spec=pl.BlockSpec((None, D, FF_SHARD), lambda i, j, k: (k, 0, 0)),
                 o_spec=pl.BlockSpec((tf, D), lambda i, j, k: (i, 0)),
                 out_shape=_sds((t, D), BF16), name=tag + "_dn")
    dx, dgain = _rmsnorm_bwd(x, gain, dn, d, tag + "_dnorm")
    return dx, dgain, dwgu8, dwd4


CONV_TC = 768
CONV_K = 4
CONV_TM = 1024


def _conv_fwd(ha, w, name):
    t = ha.shape[0]
    nb = CONV_TM // 8

    def body(prev_ref, cur_ref, w_ref, o_ref):
        i = pl.program_id(0)
        cur = cur_ref[...].astype(F32)
        prev = prev_ref[...].astype(F32) * (i > 0).astype(F32)
        ext = jnp.concatenate([prev, cur], axis=0)
        wv = w_ref[...]
        acc = cur * wv[3:4]
        for k in range(1, CONV_K):
            acc = acc + pltpu.roll(ext, k, axis=0)[8:] * wv[3 - k:4 - k]
        o_ref[...] = acc.astype(o_ref.dtype)

    return pl.pallas_call(
        body, out_shape=_sds((3, t, CONV_TC), BF16), grid=(t // CONV_TM, 3),
        in_specs=[pl.BlockSpec((8, CONV_TC), lambda i, c: (jnp.maximum(i * nb - 1, 0), c)),
                  pl.BlockSpec((CONV_TM, CONV_TC), lambda i, c: (i, c)),
                  pl.BlockSpec((CONV_K, CONV_TC), lambda i, c: (0, c))],
        out_specs=pl.BlockSpec((None, CONV_TM, CONV_TC), lambda i, c: (c, i, 0)),
        name=name, compiler_params=_cparams(("parallel", "parallel")))(ha, ha, w)


def _conv_bwd(ha, w, dy3, name):
    t = ha.shape[0]
    nb = CONV_TM // 8
    nt = t // CONV_TM

    def body(prev_ref, cur_ref, dy_ref, nxt_ref, w_ref, dx_ref, dw_ref):
        i = pl.program_id(1)
        cur = cur_ref[...].astype(F32)
        prev = prev_ref[...].astype(F32) * (i > 0).astype(F32)
        ext = jnp.concatenate([prev, cur], axis=0)
        dy = dy_ref[...].astype(F32)
        nxt = nxt_ref[...].astype(F32) * (i < nt - 1).astype(F32)
        dext = jnp.concatenate([dy, nxt], axis=0)
        wv = w_ref[...]
        dx = dy * wv[3:4]
        dws = [None] * CONV_K
        dws[3] = jnp.sum(dy * cur, axis=0, keepdims=True)
        for k in range(1, CONV_K):
            dx = dx + pltpu.roll(dext, CONV_TM + 8 - k, axis=0)[:CONV_TM] * wv[3 - k:4 - k]
            dws[3 - k] = jnp.sum(dy * pltpu.roll(ext, k, axis=0)[8:], axis=0, keepdims=True)
        dx_ref[...] = dx.astype(dx_ref.dtype)

        @pl.when(i == 0)
        def _():
            dw_ref[...] = jnp.zeros_like(dw_ref)

        dw_ref[...] += jnp.concatenate(dws, axis=0)

    return pl.pallas_call(
        body, out_shape=(_sds((t, 3 * CONV_TC), BF16), _sds((CONV_K, 3 * CONV_TC), F32)), grid=(3, nt),
        in_specs=[pl.BlockSpec((8, CONV_TC), lambda c, i: (jnp.maximum(i * nb - 1, 0), c)),
                  pl.BlockSpec((CONV_TM, CONV_TC), lambda c, i: (i, c)),
                  pl.BlockSpec((None, CONV_TM, CONV_TC), lambda c, i: (c, i, 0)),
                  pl.BlockSpec((None, 8, CONV_TC), lambda c, i: (c, jnp.minimum((i + 1) * nb, t // 8 - 1), 0)),
                  pl.BlockSpec((CONV_K, CONV_TC), lambda c, i: (0, c))],
        out_specs=(pl.BlockSpec((CONV_TM, CONV_TC), lambda c, i: (i, c)),
                   pl.BlockSpec((CONV_K, CONV_TC), lambda c, i: (0, c))),
        name=name, compiler_params=_cparams(("parallel", "arbitrary")))(ha, ha, dy3, dy3, w)


def _gdn_specs(t, rev):
    ng = t // GROUP

    def gi(g):
        return ng - 1 - g if rev else g

    qkv = pl.BlockSpec((3, GROUP, GDN_HPS * HD), lambda h, g: (0, gi(g), h))
    bg = pl.BlockSpec((GROUP, 128), lambda h, g: (gi(g), 0))
    dbg = pl.BlockSpec((GDN_HPS, GROUP, 128), lambda h, g: (h, gi(g), 0))
    o = pl.BlockSpec((GROUP, GDN_HPS * HD), lambda h, g: (gi(g), h))
    st = pl.BlockSpec((GDN_HPS, None, HD, HD), lambda h, g: (h, gi(g), 0, 0))
    inv = pl.BlockSpec((GDN_HPS, None, GROUP, GROUP), lambda h, g: (h, gi(g), 0, 0))
    return qkv, bg, dbg, o, st, inv


def _head_qkv(qkv_ref, j):
    sl = slice(j * HD, (j + 1) * HD)
    return qkv_ref[0, :, sl].astype(F32), qkv_ref[1, :, sl].astype(F32), qkv_ref[2, :, sl].astype(F32)


def _head_gates(bg, h):
    lane = lax.broadcasted_iota(jnp.int32, (1, 128), 1)
    beta = jnp.sum(jnp.where(lane == h, bg, 0.0), axis=1, keepdims=True)
    gcol = jnp.sum(jnp.where(lane == h + A_HEADS, bg, 0.0), axis=1, keepdims=True)
    return beta, gcol, _col_to_row(gcol)


def _gdn_fwd(qkv3, bg, name):
    t = qkv3.shape[1]
    ng = t // GROUP
    qkv_s, bg_s, _, o_s, st_s, inv_s = _gdn_specs(t, False)

    def body(qkv_ref, bg_ref, o_ref, st_ref, inv_ref, s_scr):
        @pl.when(pl.program_id(1) == 0)
        def _():
            s_scr[...] = jnp.zeros_like(s_scr)

        st_ref[...] = s_scr[...]
        bgv = bg_ref[...]
        loc = [_gdn_local(*_head_qkv(qkv_ref, j), *_head_gates(bgv, pl.program_id(0) * GDN_HPS + j))
               for j in range(GDN_HPS)]
        s = [s_scr[j] for j in range(GDN_HPS)]
        for a in range(GROUP // CHUNK):
            sl = slice(a * CHUNK, (a + 1) * CHUNK)
            for j in range(GDN_HPS):
                u, w, qd, kd, qkc, dm, _ = loc[j]
                s[j], out = _gdn_step(s[j], w[sl], u[sl], qd[sl], kd[sl], qkc[sl], dm[sl])
                o_ref[sl, j * HD:(j + 1) * HD] = out.astype(o_ref.dtype)
        for j in range(GDN_HPS):
            s_scr[j] = s[j]
            inv_ref[j] = loc[j][6].astype(inv_ref.dtype)

    return pl.pallas_call(
        body, out_shape=(_sds((t, A_WIDTH), BF16), _sds((A_HEADS, ng, HD, HD), F32),
                         _sds((A_HEADS, ng, GROUP, GROUP), BF16)),
        grid=(A_HEADS // GDN_HPS, ng), in_specs=[qkv_s, bg_s], out_specs=(o_s, st_s, inv_s),
        scratch_shapes=[pltpu.VMEM((GDN_HPS, HD, HD), F32)], name=name,
        compiler_params=_cparams(("parallel", "arbitrary")))(qkv3, bg)


def _gdn_bwd(qkv3, bg, states, pinvs, do, name):
    t = qkv3.shape[1]
    ng = t // GROUP
    qkv_s, bg_s, dbg_s, o_s, st_s, inv_s = _gdn_specs(t, True)
    nc = GROUP // CHUNK

    def body(qkv_ref, bg_ref, st_ref, inv_ref, do_ref, dqkv_ref, dbg_ref, ds_scr):
        @pl.when(pl.program_id(1) == 0)
        def _():
            ds_scr[...] = jnp.zeros_like(ds_scr)

        heads = range(GDN_HPS)
        bgv = bg_ref[...]
        hid = [pl.program_id(0) * GDN_HPS + j for j in heads]
        fw = [jax.vjp(functools.partial(_gdn_local, pinv_kept=inv_ref[j].astype(F32)),
                      *_head_qkv(qkv_ref, j), *_head_gates(bgv, hid[j])) for j in heads]
        starts = [[None] * nc for _ in heads]
        s = [st_ref[j] for j in heads]
        for a in range(nc):
            sl = slice(a * CHUNK, (a + 1) * CHUNK)
            for j in heads:
                u, w, qd, kd, qkc, dm = fw[j][0]
                starts[j][a] = s[j]
                if a < nc - 1:
                    s[j], _ = _gdn_step(s[j], w[sl], u[sl], qd[sl], kd[sl], qkc[sl], dm[sl])
        ds = [ds_scr[j] for j in heads]
        parts = [[None] * nc for _ in heads]
        for a in reversed(range(nc)):
            sl = slice(a * CHUNK, (a + 1) * CHUNK)
            for j in heads:
                u, w, qd, kd, qkc, dm = fw[j][0]
                _, vjp_step = jax.vjp(_gdn_step, starts[j][a], w[sl], u[sl], qd[sl], kd[sl], qkc[sl], dm[sl])
                grads = vjp_step((ds[j], do_ref[sl, j * HD:(j + 1) * HD].astype(F32)))
                ds[j] = grads[0]
                parts[j][a] = grads[1:]
        lane = lax.broadcasted_iota(jnp.int32, (1, 128), 1)
        for j in heads:
            ds_scr[j] = ds[j]
            dw, du, dqd, dkd, dqk, ddm = [jnp.concatenate([parts[j][a][i] for a in range(nc)], axis=0)
                                          for i in range(6)]
            dq, dk, dv, db, dgc, dgr = fw[j][1]((du, dw, dqd, dkd, dqk, ddm))
            hs = slice(j * HD, (j + 1) * HD)
            dqkv_ref[0, :, hs] = dq.astype(dqkv_ref.dtype)
            dqkv_ref[1, :, hs] = dk.astype(dqkv_ref.dtype)
            dqkv_ref[2, :, hs] = dv.astype(dqkv_ref.dtype)
            dbg_ref[j] = (jnp.where(lane == hid[j], db, 0.0)
                          + jnp.where(lane == hid[j] + A_HEADS, dgc + _row_to_col(dgr), 0.0))

    return pl.pallas_call(
        body, out_shape=(_sds((3, t, A_WIDTH), BF16), _sds((A_HEADS, t, 128), F32)),
        grid=(A_HEADS // GDN_HPS, ng), in_specs=[qkv_s, bg_s, st_s, inv_s, o_s],
        out_specs=(qkv_s, dbg_s), scratch_shapes=[pltpu.VMEM((GDN_HPS, HD, HD), F32)], name=name,
        compiler_params=_cparams(("parallel", "arbitrary")))(qkv3, bg, states, pinvs, do)


NEG = -1e30


def _diag_mask(shape, q_axis):
    qi = lax.shift_right_logical(lax.broadcasted_iota(jnp.int32, shape, q_axis), 6)
    ki = lax.shift_right_logical(lax.broadcasted_iota(jnp.int32, shape, 1 - q_axis), 6)
    return ki <= qi


def _col_to_row(col):
    n = col.shape[0]
    eye = lax.broadcasted_iota(jnp.int32, (n, n), 0) == lax.broadcasted_iota(jnp.int32, (n, n), 1)
    return jnp.sum(jnp.where(eye, col, 0.0), axis=0, keepdims=True)


def _row_to_col(row):
    n = row.shape[1]
    eye = lax.broadcasted_iota(jnp.int32, (n, n), 0) == lax.broadcasted_iota(jnp.int32, (n, n), 1)
    return jnp.sum(jnp.where(eye, row, 0.0), axis=1, keepdims=True)


def _blk(ref, i):
    return ref[pl.ds(pl.multiple_of(i * ATT_BQ, ATT_BQ), ATT_BQ), :]


def _att_fwd(qc, kc, v, name):
    t = qc.shape[0]
    nq = t // ATT_BQ

    def body(q_ref, k_ref, v_ref, o_ref, lse_ref, lser_ref, m_scr, l_scr, acc_scr):
        qb = pl.program_id(1)
        q = q_ref[...]
        m_scr[...] = jnp.full_like(m_scr, NEG)
        l_scr[...] = jnp.zeros_like(l_scr)
        acc_scr[...] = jnp.zeros_like(acc_scr)

        def step(kb, diag):
            s = _dot(q, _blk(k_ref, kb), "nt")
            if diag:
                s = jnp.where(_diag_mask(s.shape, 0), s, NEG)
            m_old = m_scr[...]
            m_new = jnp.maximum(m_old, jnp.max(s, axis=1, keepdims=True))
            alpha = jnp.exp2(m_old - m_new)
            p = jnp.exp2(s - m_new)
            l_scr[...] = alpha * l_scr[...] + jnp.sum(p, axis=1, keepdims=True)
            acc_scr[...] = alpha * acc_scr[...] + _dot(p, _blk(v_ref, kb))
            m_scr[...] = m_new

        def loop_body(kb, carry):
            step(kb, False)
            return carry

        lax.fori_loop(0, qb, loop_body, 0)
        step(qb, True)
        o_ref[...] = (acc_scr[...] / l_scr[...]).astype(o_ref.dtype)
        lse = m_scr[...] + jnp.log2(l_scr[...])
        lse_ref[...] = lse
        lser_ref[...] = _col_to_row(lse)

    return pl.pallas_call(
        body, out_shape=(_sds((t, B_HEADS * HD), BF16), _sds((B_HEADS, t, 1), F32),
                         _sds((B_HEADS, nq, 1, ATT_BQ), F32)), grid=(B_HEADS, nq),
        in_specs=[pl.BlockSpec((ATT_BQ, QK_CAT), lambda h, i: (i, h)),
                  pl.BlockSpec((t, QK_CAT), lambda h, i: (0, h)), pl.BlockSpec((t, HD), lambda h, i: (0, h))],
        out_specs=(pl.BlockSpec((ATT_BQ, HD), lambda h, i: (i, h)),
                   pl.BlockSpec((None, ATT_BQ, 1), lambda h, i: (h, i, 0)),
                   pl.BlockSpec((None, None, 1, ATT_BQ), lambda h, i: (h, i, 0, 0))),
        scratch_shapes=[pltpu.VMEM((ATT_BQ, 1), F32), pltpu.VMEM((ATT_BQ, 1), F32), pltpu.VMEM((ATT_BQ, HD), F32)],
        name=name, compiler_params=_cparams(("parallel", "arbitrary")))(qc, kc, v)


def _att_bwd(qc, kc, v, o, lse, lse_row, do, name):
    t = qc.shape[0]
    nq = t // ATT_BQ

    def delta_fn(ob, dob):
        dl = jnp.sum(ob.astype(F32) * dob.astype(F32), axis=1, keepdims=True)
        return dl, _col_to_row(dl)

    delta, delta_row = _rowcall(
        delta_fn, [o, do], [pl.BlockSpec((ATT_BQ, HD), lambda i, h: (i, h))] * 2,
        [_sds((B_HEADS, t, 1), F32), _sds((B_HEADS, nq, 1, ATT_BQ), F32)],
        [pl.BlockSpec((None, ATT_BQ, 1), lambda i, h: (h, i, 0)),
         pl.BlockSpec((None, None, 1, ATT_BQ), lambda i, h: (h, i, 0, 0))], (nq, B_HEADS), name + "_delta")

    def dq_body(q_ref, k_ref, v_ref, do_ref, lse_ref, dl_ref, dq_ref, acc):
        qb = pl.program_id(1)
        q, dob, lse_b, dl_b = q_ref[...], do_ref[...], lse_ref[...], dl_ref[...]
        acc[...] = jnp.zeros_like(acc)

        def step(kb, diag):
            k = _blk(k_ref, kb)
            s = _dot(q, k, "nt")
            if diag:
                s = jnp.where(_diag_mask(s.shape, 0), s, NEG)
            p = jnp.exp2(s - lse_b)
            ds = p * (_dot(dob, _blk(v_ref, kb), "nt") - dl_b)
            acc[...] += _dot(ds, k)

        def loop_body(kb, carry):
            step(kb, False)
            return carry

        lax.fori_loop(0, qb, loop_body, 0)
        step(qb, True)
        dq_ref[...] = (acc[...] * ATT_SCALE).astype(dq_ref.dtype)

    qmap = lambda h, i: (i, h)
    colq = pl.BlockSpec((None, ATT_BQ, 1), lambda h, i: (h, i, 0))
    dq = pl.pallas_call(
        dq_body, out_shape=_sds((t, B_HEADS * QK_CAT), BF16), grid=(B_HEADS, nq),
        in_specs=[pl.BlockSpec((ATT_BQ, QK_CAT), qmap), pl.BlockSpec((t, QK_CAT), lambda h, i: (0, h)),
                  pl.BlockSpec((t, HD), lambda h, i: (0, h)), pl.BlockSpec((ATT_BQ, HD), qmap), colq, colq],
        out_specs=pl.BlockSpec((ATT_BQ, QK_CAT), qmap),
        scratch_shapes=[pltpu.VMEM((ATT_BQ, QK_CAT), F32)], name=name + "_dq",
        compiler_params=_cparams(("parallel", "arbitrary")))(qc, kc, v, do, lse, delta)

    def dkv_body(k_ref, v_ref, q_ref, do_ref, lser_ref, dlr_ref, dk_ref, dv_ref, dk_acc, dv_acc):
        kb = pl.program_id(1)
        k, vv = k_ref[...], v_ref[...]
        dk_acc[...] = jnp.zeros_like(dk_acc)
        dv_acc[...] = jnp.zeros_like(dv_acc)

        def step(qb, diag):
            q, dob = _blk(q_ref, qb), _blk(do_ref, qb)
            st = _dot(k, q, "nt")
            if diag:
                st = jnp.where(_diag_mask(st.shape, 1), st, NEG)
            pt = jnp.exp2(st - lser_ref[qb])
            dst = pt * (_dot(vv, dob, "nt") - dlr_ref[qb])
            dv_acc[...] += _dot(pt, dob)
            dk_acc[...] += _dot(dst, q)

        def loop_body(qb, carry):
            step(qb, False)
            return carry

        step(kb, True)
        lax.fori_loop(kb + 1, nq, loop_body, 0)
        dk_ref[...] = (dk_acc[...] * LN2).astype(dk_ref.dtype)
        dv_ref[...] = dv_acc[...].astype(dv_ref.dtype)

    kmap = lambda h, j: (j, h)
    rowq = pl.BlockSpec((None, nq, 1, ATT_BQ), lambda h, j: (h, 0, 0, 0))
    dk, dv = pl.pallas_call(
        dkv_body, out_shape=(_sds((t, B_HEADS * QK_CAT), BF16), _sds((t, B_HEADS * HD), BF16)),
        grid=(B_HEADS, nq),
        in_specs=[pl.BlockSpec((ATT_BQ, QK_CAT), kmap), pl.BlockSpec((ATT_BQ, HD), kmap),
                  pl.BlockSpec((t, QK_CAT), lambda h, j: (0, h)), pl.BlockSpec((t, HD), lambda h, j: (0, h)),
                  rowq, rowq],
        out_specs=(pl.BlockSpec((ATT_BQ, QK_CAT), kmap), pl.BlockSpec((ATT_BQ, HD), kmap)),
        scratch_shapes=[pltpu.VMEM((ATT_BQ, QK_CAT), F32), pltpu.VMEM((ATT_BQ, HD), F32)], name=name + "_dkv",
        compiler_params=_cparams(("parallel", "arbitrary")))(kc, v, qc, do, lse_row, delta_row)
    return dq, dk, dv


def _mem_fwd(hx, col, mkv, name):
    t = hx.shape[0]
    (o,) = _rowcall(_memattn, [hx, mkv, mkv],
                    [_rows(TM, MEM_W, col), pl.BlockSpec((N_MEM, MEM_W), lambda i: (0, 0)),
                     pl.BlockSpec((N_MEM, MEM_W), lambda i: (0, 1))],
                    [_sds((t, MEM_W), BF16)], [_rows(TM, MEM_W)], (t // TM,), name)
    return o


def _mem_bwd(hx, col, mkv, do, do_col, name):
    t = hx.shape[0]
    dq, dk, dv = _rowcall(_vjp_fn(_memattn, 3, (0, 1, 2)), [hx, mkv, mkv, do],
                          [_rows(TM, MEM_W, col), pl.BlockSpec((N_MEM, MEM_W), lambda i: (0, 0)),
                           pl.BlockSpec((N_MEM, MEM_W), lambda i: (0, 1)), _rows(TM, MEM_W, do_col)],
                          [_sds((t, MEM_W), BF16), _sds((N_MEM, MEM_W), F32), _sds((N_MEM, MEM_W), F32)],
                          [_rows(TM, MEM_W), _shared((N_MEM, MEM_W)), _shared((N_MEM, MEM_W))],
                          (t // TM,), name, n_acc=2)
    return dq, jnp.concatenate([dk, dv], axis=1)


def _a_in_ext(w):
    nb = 4 * A_WIDTH
    ba = jnp.pad(w[:, nb:nb + 2 * A_HEADS], ((0, 0), (0, 128 - 2 * A_HEADS)))
    return jnp.concatenate([w[:, :nb], w[:, nb + 2 * A_HEADS:], ba], axis=1)


def _swap_halves(w):
    return jnp.concatenate([w[..., QK_ROPE // 2:], w[..., :QK_ROPE // 2]], axis=-1)


def _uq_ext(w):
    w = w.reshape(Q_LORA, B_HEADS, QK_NOPE + QK_ROPE)
    nope, rope = w[..., :QK_NOPE], w[..., QK_NOPE:]
    z64 = jnp.zeros((Q_LORA, B_HEADS, QK_CAT - QK_NOPE - QK_ROPE), w.dtype)
    z128 = jnp.zeros((Q_LORA, B_HEADS, QK_NOPE), w.dtype)
    a = jnp.concatenate([nope, rope, z64], axis=-1).reshape(Q_LORA, B_HEADS * QK_CAT)
    b = jnp.concatenate([z128, _swap_halves(rope), z64], axis=-1).reshape(Q_LORA, B_HEADS * QK_CAT)
    return jnp.concatenate([a, b], axis=1)


def _dkv_ext(w):
    ckv, kr = w[:, :KV_LORA], w[:, KV_LORA:]
    z128 = jnp.zeros((D, QK_NOPE), w.dtype)
    z64 = jnp.zeros((D, QK_CAT - QK_NOPE - QK_ROPE), w.dtype)
    return jnp.concatenate([ckv, z128, kr, z64, z128, _swap_halves(kr), z64], axis=1)


def _ukv_ext(w):
    w = w.reshape(KV_LORA, B_HEADS, QK_NOPE + HD)
    kn, vv = w[..., :QK_NOPE], w[..., QK_NOPE:]
    z = jnp.zeros((KV_LORA, B_HEADS, QK_CAT - QK_NOPE), w.dtype)
    a = jnp.concatenate([kn, z], axis=-1).reshape(KV_LORA, B_HEADS * QK_CAT)
    return jnp.concatenate([a, vv.reshape(KV_LORA, B_HEADS * HD)], axis=1)


def _ext_and_back(fn, w):
    ext, back = jax.vjp(fn, w.astype(F32))
    return ext.astype(BF16), lambda g: back(g.astype(F32))[0]


def _rope_tables(pos_col):
    t = pos_col.shape[0]
    inv = (ROPE_THETA ** (-np.arange(0, QK_ROPE, 2, dtype=np.float32) / QK_ROPE)).astype(np.float32)
    inv_row = np.zeros((1, QK_CAT), np.float32)
    inv_row[0, QK_NOPE:QK_NOPE + QK_ROPE] = np.concatenate([inv, inv])
    sign = np.zeros((1, QK_CAT), np.float32)
    sign[0, QK_NOPE:QK_NOPE + QK_ROPE // 2] = -1.0
    sign[0, QK_NOPE + QK_ROPE // 2:QK_NOPE + QK_ROPE] = 1.0
    is_rope = np.abs(sign)
    is_nope = np.zeros((1, QK_CAT), np.float32)
    is_nope[0, :QK_NOPE] = 1.0

    def fn(p, inv_b, sign_b, rope_b, nope_b):
        ang = p.astype(F32) * inv_b
        return jnp.cos(ang) * rope_b + nope_b, jnp.sin(ang) * sign_b

    consts = [jnp.asarray(a) for a in (inv_row, sign, is_rope, is_nope)]
    return _rowcall(fn, [pos_col] + consts, [_rows(TM, 1)] + [_shared((1, QK_CAT))] * 4,
                    [_sds((t, QK_CAT), F32)] * 2, [_rows(TM, QK_CAT)] * 2, (t // TM,), "rope_tables")


def _local_step(x, mem, pos, target, w, layer_start, mixer_start, mixer_done, layer_done):
    t = x.shape[0]
    g = {}
    th = min(TMH, t)
    head6 = (t // th, A_HEADS)

    mem_n = _rmsnorm_fwd_small(mem, w["mem_norm"])
    rope_c, rope_s = _rope_tables(pos.reshape(t, 1))
    mkv = None

    saved = []
    for l in range(DEPTH):
        sv = {}
        layer_start(l, x)
        x, sv["ffn1"] = _ffn_fwd(x, w["ffn1_norm"][l], w["ffn1_w_gu"][l], w["ffn1_w_down"][l], "ffn1")
        sv["x1"] = x
        mixer_start(l, x)
        if mkv is None:
            mkv = [_mm(mem_n, w["w_mem_kv"][i], "nn", BF16, f"mkv{i}") for i in range(DEPTH)]
        n2 = _rmsnorm_fwd(x, w["mix_norm"][l], "mix_norm")
        sv["n2"] = n2
        if l < N_A:
            ha = _mm(n2, w["a_w_in"][l], "nn", BF16, "a_in")
            yc3 = _conv_fwd(ha, w["a_conv"][l], "a_conv")
            blk3 = pl.BlockSpec((3, th, HD), lambda i, h: (0, i, h))
            (qkv3,) = _rowcall(lambda b: jnp.stack(_gdn_prep(b[0].astype(F32), b[1].astype(F32), b[2].astype(F32))),
                               [yc3], [blk3], [_sds((3, t, A_WIDTH), BF16)], [blk3], head6, "a_prep")
            (bg,) = _rowcall(_gates, [ha, _pad128(w["a_A_log"][l], A_HEADS), _pad128(w["a_dt_bias"][l], A_HEADS)],
                             [_rows(TM, 128, A_BA_BLK), _shared((1, 128)), _shared((1, 128))],
                             [_sds((t, 128), F32)], [_rows(TM, 128)], (t // TM,), "a_gates")
            o_gdn, states, pinvs = _gdn_fwd(qkv3, bg, "a_gdn")
            (o_a,) = _rowcall(_outnorm_gate, [o_gdn, ha, w["a_out_norm"][l].reshape(1, HD)],
                              [pl.BlockSpec((th, HD), lambda i, h: (i, h)),
                               pl.BlockSpec((th, HD), lambda i, h: (i, 3 * A_HEADS + h)), _shared((1, HD))],
                              [_sds((t, A_WIDTH), BF16)], [pl.BlockSpec((th, HD), lambda i, h: (i, h))],
                              head6, "a_outnorm")
            o_m = _mem_fwd(ha, A_MQ_BLK, mkv[l], "mem_attn_a")
            sv.update(ha=ha, yc3=yc3, qkv3=qkv3, bg=bg, states=states, pinvs=pinvs, o_gdn=o_gdn)
            cat = jnp.concatenate([o_a, o_m], axis=1)
        else:
            j = l - N_A
            hb = _mm(n2, w["b_w_in"][j], "nn", BF16, "b_in")
            cqn = _rmsnorm_fwd(hb, w["b_q_norm"][j], "b_qnorm", 0, Q_LORA)
            qq = _mm(cqn, w["b_w_uq"][j], "nn", BF16, "b_uq")
            (qc,) = _rowcall(_rope_mix, [qq, qq, rope_c, rope_s],
                             [pl.BlockSpec((th, QK_CAT), lambda i, h: (i, h)),
                              pl.BlockSpec((th, QK_CAT), lambda i, h: (i, B_HEADS + h)),
                              pl.BlockSpec((th, QK_CAT), lambda i, h: (i, 0)),
                              pl.BlockSpec((th, QK_CAT), lambda i, h: (i, 0))],
                             [_sds((t, B_HEADS * QK_CAT), BF16)], [pl.BlockSpec((th, QK_CAT), lambda i, h: (i, h))],
                             head6, "b_qrope")
            o_b, lse, lse_row = _att_fwd(qc, kcat, vmla, "b_attn")
            o_m = _mem_fwd(hb, 1, mkv[l], "mem_attn_b")
            sv.update(hb=hb, cqn=cqn, qc=qc, o_b=o_b, lse=(lse, lse_row))
            cat = jnp.concatenate([o_b, o_m], axis=1)
        sv["cat"] = cat
        x = _mm(cat, w["w_out"][l], "nn", F32, "w_out", res=x)
        x, sv["ffn2"] = _ffn_fwd(x, w["ffn2_norm"][l], w["ffn2_w_gu"][l], w["ffn2_w_down"][l], "ffn2")
        saved.append(sv)
        if l == N_A - 1:
            x_kv = x
            nkv = _rmsnorm_fwd(x, w["kv_in_norm"], "kv_in_norm")
            ckr = _mm(nkv, w["w_dkv"], "nn", BF16, "kv_down")
            ckv_n = _rmsnorm_fwd(ckr, w["kv_lat_norm"], "kv_lat_norm", 0, KV_LORA)
            kvu = _mm(ckv_n, w["w_ukv"], "nn", BF16, "kv_up")
            vmla = kvu[:, B_HEADS * QK_CAT:]
            (kcat,) = _rowcall(_kcat, [kvu, ckr, ckr, rope_c, rope_s],
                               [pl.BlockSpec((th, QK_CAT), lambda i, h: (i, h)),
                                pl.BlockSpec((th, QK_CAT), lambda i, h: (i, 1)),
                                pl.BlockSpec((th, QK_CAT), lambda i, h: (i, 2)),
                                pl.BlockSpec((th, QK_CAT), lambda i, h: (i, 0)),
                                pl.BlockSpec((th, QK_CAT), lambda i, h: (i, 0))],
                               [_sds((t, B_HEADS * QK_CAT), BF16)],
                               [pl.BlockSpec((th, QK_CAT), lambda i, h: (i, h))], head6, "kv_cat")

    def loss_fn(xb, gb, tb):
        def f(xx, gg):
            e = _rms(xx, gg) - tb
            return 0.5 * jnp.sum(jnp.mean(e * e, axis=-1, keepdims=True), axis=0, keepdims=True)
        val, vjp = jax.vjp(f, xb, gb)
        dx, dg = vjp(jnp.ones((1, 1), F32))
        return dx, dg, val * jnp.ones((1, 128), F32)

    d, dfin, loss = _rowcall(loss_fn, [x, w["final_norm"].reshape(1, D), target],
                             [_rows(TM, D), _shared((1, D)), _rows(TM, D)],
                             [_sds((t, D), F32), _sds((1, D), F32), _sds((1, 128), F32)],
                             [_rows(TM, D), _shared((1, D)), _shared((1, 128))], (t // TM,), "loss_head", n_acc=2)
    g["final_norm"] = dfin[0]
    loss = loss[0, 0]

    for name in ("ffn1_norm", "ffn1_w_gu", "ffn1_w_down", "mix_norm", "ffn2_norm", "ffn2_w_gu", "ffn2_w_down",
                 "w_out", "w_mem_kv"):
        g[name] = [None] * DEPTH
    for name in ("a_w_in", "a_conv", "a_A_log", "a_dt_bias", "a_out_norm"):
        g[name] = [None] * N_A
    for name in ("b_w_in", "b_q_norm", "b_w_uq"):
        g[name] = [None] * N_B
    dmkv = [None] * DEPTH
    dkcat = []
    dvmla = []

    for l in reversed(range(DEPTH)):
        sv = saved[l]
        if l == N_A - 1:
            kq = pl.BlockSpec((th, QK_CAT), lambda i, h: (i, h))
            tab = pl.BlockSpec((th, QK_CAT), lambda i, h: (i, 0))

            def dk_fn(c, s, d0, d1):
                dk = d0.astype(F32) + d1.astype(F32)
                return dk, dk * c, dk * s

            dkn, dkr_h, dkrs_h = _rowcall(dk_fn, [rope_c, rope_s, dkcat[0], dkcat[1]], [tab, tab, kq, kq],
                                          [_sds((t, B_HEADS * QK_CAT), BF16)] + [_sds((B_HEADS, t, QK_CAT), BF16)] * 2,
                                          [kq] + [pl.BlockSpec((None, th, QK_CAT), lambda i, h: (h, i, 0))] * 2,
                                          head6, "kv_dcat")

            def sum6(a, b):
                return jnp.sum(a.astype(F32), axis=0), jnp.sum(b.astype(F32), axis=0)

            h6 = pl.BlockSpec((B_HEADS, TM, QK_CAT), lambda i: (0, i, 0))
            dkr, dkrs = _rowcall(sum6, [dkr_h, dkrs_h], [h6, h6], [_sds((t, QK_CAT), BF16)] * 2,
                                 [_rows(TM, QK_CAT)] * 2, (t // TM,), "kv_dkr")

            def addv(a, b):
                return a.astype(F32) + b.astype(F32)

            (dv,) = _rowcall(addv, dvmla, [_rows(TM, B_HEADS * HD)] * 2, [_sds((t, B_HEADS * HD), BF16)],
                             [_rows(TM, B_HEADS * HD)], (t // TM,), "kv_dv")
            dkvu = jnp.concatenate([dkn, dv], axis=1)
            g["w_ukv"] = _mm(ckv_n, dkvu, "tn", F32, "kv_up_dw")
            dckv_n = _mm(dkvu, w["w_ukv"], "nt", BF16, "kv_up_dx")

            def lat_bwd(cb, gb, dnb):
                return _vjp_fn(_rms, 2, (0, 1))(cb, gb, dnb)

            dckv, g["kv_lat_norm"] = _rowcall(lat_bwd, [ckr, w["kv_lat_norm"].reshape(1, KV_LORA), dckv_n],
                                              [_rows(TM, KV_LORA), _shared((1, KV_LORA)), _rows(TM, KV_LORA)],
                                              [_sds((t, KV_LORA), BF16), _sds((1, KV_LORA), F32)],
                                              [_rows(TM, KV_LORA), _shared((1, KV_LORA))], (t // TM,),
                                              "kv_lat_dnorm", n_acc=1)
            g["kv_lat_norm"] = g["kv_lat_norm"][0]
            dckr = jnp.concatenate([dckv, dkr, dkrs], axis=1)
            g["w_dkv"] = _mm(nkv, dckr, "tn", F32, "kv_down_dw")
            dnkv = _mm(dckr, w["w_dkv"], "nt", BF16, "kv_down_dx")
            d, g["kv_in_norm"] = _rmsnorm_bwd(x_kv, w["kv_in_norm"], dnkv, d, "kv_in_dnorm")

        d, g["ffn2_norm"][l], g["ffn2_w_gu"][l], g["ffn2_w_down"][l] = _ffn_bwd(
            d, sv["ffn2"], w["ffn2_norm"][l], w["ffn2_w_gu"][l], w["ffn2_w_down"][l], "ffn2b")
        g["w_out"][l] = _mm(sv["cat"], d, "tn", F32, "w_out_dw")
        dcat = _mm(d, w["w_out"][l], "nt", BF16, "w_out_dx")
        if l < N_A:
            ha, yc3, qkv3, states, o_gdn = sv["ha"], sv["yc3"], sv["qkv3"], sv["states"], sv["o_gdn"]
            dmq, dmkv[l] = _mem_bwd(ha, A_MQ_BLK, mkv[l], dcat, 3, "mem_attn_a_bwd")
            hblk = pl.BlockSpec((th, HD), lambda i, h: (i, h))
            do_gdn, dgate, dgain = _rowcall(
                _vjp_fn(_outnorm_gate, 3, (0, 1, 2)), [o_gdn, ha, w["a_out_norm"][l].reshape(1, HD), dcat],
                [hblk, pl.BlockSpec((th, HD), lambda i, h: (i, 3 * A_HEADS + h)), _shared((1, HD)), hblk],
                [_sds((t, A_WIDTH), BF16), _sds((t, A_WIDTH), BF16), _sds((1, HD), F32)],
                [hblk, hblk, _shared((1, HD))], head6, "a_outnorm_bwd", n_acc=1)
            g["a_out_norm"][l] = dgain[0]
            dqkv3, dbg6 = _gdn_bwd(qkv3, sv["bg"], states, sv["pinvs"], do_gdn, "a_gdn_bwd")

            def dgates(bab, alb, dtb, d6):
                return _vjp_fn(_gates, 3, (0, 1, 2))(bab, alb, dtb, jnp.sum(d6, axis=0))

            dba, dalog, ddt = _rowcall(
                dgates, [ha, _pad128(w["a_A_log"][l], A_HEADS), _pad128(w["a_dt_bias"][l], A_HEADS), dbg6],
                [_rows(TM, 128, A_BA_BLK), _shared((1, 128)), _shared((1, 128)),
                 pl.BlockSpec((A_HEADS, TM, 128), lambda i: (0, i, 0))],
                [_sds((t, 128), BF16), _sds((1, 128), F32), _sds((1, 128), F32)],
                [_rows(TM, 128), _shared((1, 128)), _shared((1, 128))], (t // TM,), "a_gates_bwd", n_acc=2)
            g["a_A_log"][l] = dalog[0, A_HEADS:2 * A_HEADS]
            g["a_dt_bias"][l] = ddt[0, A_HEADS:2 * A_HEADS]
            blk3 = pl.BlockSpec((3, th, HD), lambda i, h: (0, i, h))
            def dprep(b, db):
                return jnp.stack(_vjp_fn(_gdn_prep, 3, (0, 1, 2))(b[0], b[1], b[2], db[0], db[1], db[2]))

            (dyc3,) = _rowcall(dprep, [yc3, dqkv3], [blk3, blk3],
                               [_sds((3, t, A_WIDTH), BF16)], [blk3], head6, "a_prep_bwd")
            dqkv_in, g["a_conv"][l] = _conv_bwd(ha, w["a_conv"][l], dyc3, "a_conv_bwd")
            dha = jnp.concatenate([dqkv_in, dgate, dmq, dba], axis=1)
            g["a_w_in"][l] = _mm(sv["n2"], dha, "tn", F32, "a_in_dw")
            dn2 = _mm(dha, w["a_w_in"][l], "nt", BF16, "a_in_dx")
        else:
            j = l - N_A
            hb, cqn, qc, o_b, lse = sv["hb"], sv["cqn"], sv["qc"], sv["o_b"], sv["lse"]
            dmq, dmkv[l] = _mem_bwd(hb, 1, mkv[l], dcat, 3, "mem_attn_b_bwd")
            dqc, dkc, dvv = _att_bwd(qc, kcat, vmla, o_b, lse[0], lse[1], dcat, "b_attn_bwd")
            dkcat.append(dkc)
            dvmla.append(dvv)
            kq = pl.BlockSpec((th, QK_CAT), lambda i, h: (i, h))
            tab = pl.BlockSpec((th, QK_CAT), lambda i, h: (i, 0))

            def dq_fn(c, s, dq):
                dq = dq.astype(F32)
                return jnp.stack([dq * c, dq * s])

            (dqq,) = _rowcall(dq_fn, [rope_c, rope_s, dqc], [tab, tab, kq],
                              [_sds((2, t, B_HEADS * QK_CAT), BF16)],
                              [pl.BlockSpec((2, th, QK_CAT), lambda i, h: (0, i, h))], head6, "b_qrope_bwd")
            dqq = jnp.concatenate([dqq[0], dqq[1]], axis=1)
            g["b_w_uq"][j] = _mm(cqn, dqq, "tn", F32, "b_uq_dw")
            dcqn = _mm(dqq, w["b_w_uq"][j], "nt", BF16, "b_uq_dx")
            dcq, dqg = _rowcall(_vjp_fn(_rms, 2, (0, 1)), [hb, w["b_q_norm"][j].reshape(1, Q_LORA), dcqn],
                                [_rows(TM, Q_LORA), _shared((1, Q_LORA)), _rows(TM, Q_LORA)],
                                [_sds((t, Q_LORA), BF16), _sds((1, Q_LORA), F32)],
                                [_rows(TM, Q_LORA), _shared((1, Q_LORA))], (t // TM,), "b_qnorm_bwd", n_acc=1)
            g["b_q_norm"][j] = dqg[0]
            dhb = jnp.concatenate([dcq, dmq], axis=1)
            g["b_w_in"][j] = _mm(sv["n2"], dhb, "tn", F32, "b_in_dw")
            dn2 = _mm(dhb, w["b_w_in"][j], "nt", BF16, "b_in_dx")
        d, g["mix_norm"][l] = _rmsnorm_bwd(sv["x1"], w["mix_norm"][l], dn2, d, "mix_dnorm")
        d = mixer_done(l, g, d)
        d, g["ffn1_norm"][l], g["ffn1_w_gu"][l], g["ffn1_w_down"][l] = _ffn_bwd(
            d, sv["ffn1"], w["ffn1_norm"][l], w["ffn1_w_gu"][l], w["ffn1_w_down"][l], "ffn1b")
        d = layer_done(l, g, d)

    dmem_n = None
    for l in range(DEPTH):
        g["w_mem_kv"][l] = _mm(mem_n, dmkv[l], "tn", F32, f"mkv_dw{l}")
        dmem_n = _mm(dmkv[l], w["w_mem_kv"][l], "nt", F32, f"mkv_dx{l}", res=dmem_n)
    (_, gmn) = _rowcall(_vjp_fn(_rms, 2, (0, 1)), [mem, w["mem_norm"].reshape(1, D), dmem_n],
                        [_shared((N_MEM, D)), _shared((1, D)), _shared((N_MEM, D))],
                        [_sds((N_MEM, D), F32), _sds((1, D), F32)], [_shared((N_MEM, D)), _shared((1, D))],
                        (1,), "mem_dnorm")
    g["mem_norm"] = gmn[0]
    return loss, d, g


def _pad128(v, offset):
    return jnp.pad(v.astype(F32).reshape(1, -1), ((0, 0), (offset, 128 - offset - v.shape[0])))


def _rmsnorm_fwd_small(x, gain):
    r, w = x.shape
    (n,) = _rowcall(_rms, [x, gain.reshape(1, w)], [_shared((r, w)), _shared((1, w))],
                    [_sds((r, w), BF16)], [_shared((r, w))], (1,), "mem_norm")
    return n


def _exchange(srcs, gather, name):
    n = len(srcs)
    blks = [tuple(s.shape) if gather else tuple(s.shape[1:]) for s in srcs]

    def body(*refs):
        src_refs, out_refs = refs[:n], refs[n:2 * n]
        send_sems, recv_sems, local_sems = refs[2 * n:]
        x, y, c = lax.axis_index("x"), lax.axis_index("y"), lax.axis_index("c")
        me = 4 * x + 2 * y + c
        copies = []
        for k in range(1, N_DEV):
            px = (x + (k >> 2 & 1)) % 2
            py = (y + (k >> 1 & 1)) % 2
            pc = (c + (k & 1)) % 2
            peer = 4 * px + 2 * py + pc
            for a in range(n):
                cp = pltpu.make_async_remote_copy(
                    src_ref=src_refs[a] if gather else src_refs[a].at[peer], dst_ref=out_refs[a].at[me],
                    send_sem=send_sems.at[a, k - 1], recv_sem=recv_sems.at[a, k - 1],
                    device_id=(px, py, pc), device_id_type=pl.DeviceIdType.MESH)
                cp.start()
                copies.append(cp)
        for a in range(n):
            cp = pltpu.make_async_copy(src_refs[a] if gather else src_refs[a].at[me], out_refs[a].at[me],
                                       local_sems.at[a])
            cp.start()
            copies.append(cp)
        for cp in copies:
            cp.wait()

    return pl.pallas_call(
        body, out_shape=tuple(_sds((N_DEV,) + b, s.dtype) for b, s in zip(blks, srcs)),
        in_specs=[pl.BlockSpec(memory_space=pl.ANY)] * n, out_specs=tuple([pl.BlockSpec(memory_space=pl.ANY)] * n),
        scratch_shapes=[pltpu.SemaphoreType.DMA((n, N_DEV - 1)), pltpu.SemaphoreType.DMA((n, N_DEV - 1)),
                        pltpu.SemaphoreType.DMA((n,))],
        name=name)(*srcs)


_HBM = pl.BlockSpec(memory_space=pltpu.HBM)
_SEM = pl.BlockSpec(memory_space=pltpu.SEMAPHORE)
_EFFECT = pltpu.SideEffectType.DATAFLOW_SIDE_EFFECTING


def _split_copies(src_refs, land_refs, send_sems, recv_sems, gather):
    x, y, c = lax.axis_index("x"), lax.axis_index("y"), lax.axis_index("c")
    me = 4 * x + 2 * y + c
    copies = []
    for k in range(1, N_DEV):
        peer = ((x + (k >> 2 & 1)) % 2, (y + (k >> 1 & 1)) % 2, (c + (k & 1)) % 2)
        for a in range(len(src_refs)):
            i = (k - 1) * len(src_refs) + a
            src = src_refs[a] if gather else src_refs[a].at[4 * peer[0] + 2 * peer[1] + peer[2]]
            copies.append(pltpu.make_async_remote_copy(
                src_ref=src, dst_ref=land_refs[a].at[me], send_sem=send_sems[i],
                recv_sem=recv_sems[i], device_id=peer, device_id_type=pl.DeviceIdType.MESH))
    return copies


def _split_start(srcs, gather, name):
    n = len(srcs)
    srcs = [pltpu.with_memory_space_constraint(s, pltpu.HBM) for s in srcs]
    lands = [pltpu.with_memory_space_constraint(
        lax.empty(((N_DEV,) + s.shape) if gather else s.shape, s.dtype), pltpu.HBM) for s in srcs]

    ns = n * (N_DEV - 1)

    def body(*refs):
        sems = refs[2 * n:2 * n + 2 * ns]
        for cp in _split_copies(refs[:n], refs[n:2 * n], sems[:ns], sems[ns:], gather):
            cp.start()
        refs[-1][...] = jnp.zeros_like(refs[-1])

    outs = pl.pallas_call(
        body, name=name,
        out_shape=(*[pltpu.SemaphoreType.DMA(())] * (2 * ns), *[pltpu.HBM(a.shape, a.dtype) for a in srcs + lands],
                   _sds((8, 128), F32)),
        in_specs=[_HBM] * (2 * n),
        out_specs=(*[_SEM] * (2 * ns), *[_HBM] * (2 * n), pl.BlockSpec(memory_space=pltpu.VMEM)),
        input_output_aliases={i: 2 * ns + i for i in range(2 * n)},
        compiler_params=pltpu.CompilerParams(has_side_effects=_EFFECT))(*srcs, *lands)
    sems, rest = list(outs[:2 * ns]), outs[2 * ns:]
    return sems[:ns], sems[ns:], list(rest[:n]), list(rest[n:2 * n]), rest[-1]


def _split_wait(flight, after, gather, name):
    send_sems, recv_sems, srcs, lands, _ = flight
    n = len(srcs)

    ns = len(send_sems)

    def body(*refs):
        sems = refs[2 * n:2 * n + 2 * ns]
        for cp in _split_copies(refs[:n], refs[n:2 * n], sems[:ns], sems[ns:], gather):
            cp.wait_send()
            cp.wait_recv()

    outs = pl.pallas_call(
        body, name=name, out_shape=tuple(pltpu.HBM(a.shape, a.dtype) for a in srcs + lands),
        in_specs=[_HBM] * (2 * n) + [_SEM] * (2 * ns) + [pl.BlockSpec(memory_space=pl.ANY)],
        out_specs=tuple([_HBM] * (2 * n)), input_output_aliases={i: i for i in range(2 * n)},
        compiler_params=pltpu.CompilerParams(has_side_effects=_EFFECT))(*srcs, *lands, *send_sems, *recv_sems, after)
    return list(outs[n:])


def _reduce_adamw(parts, wp, mp, vp, name):
    r, cols = wp.shape
    tr = _tile_rows(r, cols)
    c1 = 1.0 - ADAM_B1 ** ADAM_STEP
    c2 = 1.0 - ADAM_B2 ** ADAM_STEP

    def fn(pb, wb, mb, vb):
        gsum = pb[0].astype(F32)
        for j in range(1, N_DEV):
            gsum = gsum + pb[j].astype(F32)
        m_new = ADAM_B1 * mb + (1.0 - ADAM_B1) * gsum
        v_new = ADAM_B2 * vb + (1.0 - ADAM_B2) * (gsum * gsum)
        delta = -ADAM_LR * ((m_new / c1) / (jnp.sqrt(v_new / c2) + ADAM_EPS) + ADAM_WD * wb)
        return gsum, delta, m_new, v_new

    row = _rows(tr, cols)
    return _rowcall(fn, [parts, wp, mp, vp],
                    [pl.BlockSpec((N_DEV, tr, cols), lambda i: (0, i, 0)), row, row, row],
                    [_sds((r, cols), F32)] * 4, [row] * 4, (r // tr,), name)


def _tile_rows(r, cols):
    for t in (512, 256, 128, 64, 32, 16):
        if r % t == 0 and t * cols <= 160 * 1024:
            return t
    return r


def _pack(arrs):
    flat = jnp.concatenate([a.reshape(-1).astype(F32) for a in arrs])
    n = flat.shape[0]
    unit = PACK_W * PACK_ROWS
    tot = -(-n // unit) * unit
    return jnp.pad(flat, (0, tot - n)).reshape(tot // PACK_W, PACK_W)


def _unpack(buf, shapes):
    out, off = [], 0
    flat = buf.reshape(-1)
    for s in shapes:
        n = int(np.prod(s))
        out.append(flat[off:off + n].reshape(s))
        off += n
    return out


def _as2d(a):
    return a.reshape(-1, a.shape[-1])


_SHARDED = ["ffn1_w_gu", "ffn1_w_down", "ffn2_w_gu", "ffn2_w_down", "w_out", "w_mem_kv", "a_w_in", "a_conv",
            "b_w_in", "b_w_uq", "w_dkv", "w_ukv"]
_COL_SHARDED = {"ffn1_w_gu", "ffn2_w_gu", "a_conv", "b_w_uq", "w_ukv"}
_LAYERED = {"ffn1_w_gu": DEPTH, "ffn1_w_down": DEPTH, "ffn2_w_gu": DEPTH, "ffn2_w_down": DEPTH, "w_out": DEPTH,
            "w_mem_kv": DEPTH, "a_w_in": N_A, "a_conv": N_A, "b_w_in": N_B, "b_w_uq": N_B}
_GATHER_FIRST = [("ffn1_w_gu", 0, 1), ("ffn1_w_down", 0, 1)]
_GATHER_MIX0 = [("ffn2_w_gu", 0, 1), ("ffn2_w_down", 0, 1), ("w_out", 0, 1), ("a_w_in", 0, 1), ("a_conv", 0, 1),
                ("w_mem_kv", 0, DEPTH)]
_GATHER_REST = [("ffn1_w_gu", 1, DEPTH), ("ffn1_w_down", 1, DEPTH), ("ffn2_w_gu", 1, DEPTH),
                ("ffn2_w_down", 1, DEPTH), ("w_out", 1, DEPTH), ("a_w_in", 1, N_A), ("a_conv", 1, N_A),
                ("b_w_in", 0, N_B), ("b_w_uq", 0, N_B), ("w_dkv", None, None), ("w_ukv", None, None)]
_SCATTER_HI = [("ffn1_w_gu", 2, DEPTH), ("ffn1_w_down", 2, DEPTH), ("ffn2_w_gu", 2, DEPTH), ("ffn2_w_down", 2, DEPTH),
               ("w_out", 2, DEPTH), ("b_w_in", 0, N_B), ("b_w_uq", 0, N_B)]
_SCATTER_MID = [("ffn1_w_gu", 1, 2), ("ffn1_w_down", 1, 2), ("ffn2_w_gu", 1, 2), ("ffn2_w_down", 1, 2),
                ("w_out", 1, 2), ("a_w_in", 1, N_A), ("a_conv", 1, N_A), ("w_dkv", None, None), ("w_ukv", None, None)]
_SCATTER_MIX0 = [("ffn2_w_gu", 0, 1), ("ffn2_w_down", 0, 1), ("w_out", 0, 1), ("a_w_in", 0, 1), ("a_conv", 0, 1)]
_SCATTER_LO = [("ffn1_w_gu", 0, 1), ("ffn1_w_down", 0, 1), ("w_mem_kv", 0, DEPTH)]
_REPLICATED = ["ffn1_norm", "mix_norm", "ffn2_norm", "mem_norm", "a_A_log", "a_dt_bias", "a_out_norm", "b_q_norm",
               "kv_in_norm", "kv_lat_norm", "final_norm"]
_WEIGHTS = ["ffn1_norm", "ffn1_w_gu", "ffn1_w_down", "mix_norm", "ffn2_norm", "ffn2_w_gu", "ffn2_w_down", "w_out",
            "mem_norm", "w_mem_kv", "a_w_in", "a_conv", "a_A_log", "a_dt_bias", "a_out_norm", "b_w_in", "b_q_norm",
            "b_w_uq", "kv_in_norm", "w_dkv", "kv_lat_norm", "w_ukv", "final_norm"]


def _full_from_shards(name, sh):
    if name in ("ffn1_w_gu", "ffn2_w_gu"):
        return sh
    if name in ("ffn1_w_down", "ffn2_w_down"):
        return sh.reshape(4, FF_SHARD, D)
    if name in _COL_SHARDED:
        return jnp.moveaxis(sh, 0, -2).reshape(sh.shape[1:-1] + (N_DEV * sh.shape[-1],))
    return sh.reshape((N_DEV * sh.shape[1],) + sh.shape[2:])


def _shards_from_full(name, full):
    if name in ("ffn1_w_gu", "ffn2_w_gu"):
        return full
    if name in ("ffn1_w_down", "ffn2_w_down"):
        return full.reshape(N_DEV, D_FF // N_DEV, D)
    if name in _COL_SHARDED:
        r, cc = full.shape
        return jnp.moveaxis(full.reshape(r, N_DEV, cc // N_DEV), 1, 0)
    return full.reshape((N_DEV, full.shape[0] // N_DEV) + full.shape[1:])


def kernel(x, mem, positions, ffn1_norm, ffn1_w_gu, ffn1_w_down, mix_norm, ffn2_norm, ffn2_w_gu, ffn2_w_down, w_out, mem_norm, w_mem_kv, a_w_in, a_conv, a_A_log, a_dt_bias, a_out_norm, b_w_in, b_q_norm, b_w_uq, kv_in_norm, w_dkv, kv_lat_norm, w_ukv, final_norm, loss_target, m_ffn1_norm, m_ffn1_w_gu, m_ffn1_w_down, m_mix_norm, m_ffn2_norm, m_ffn2_w_gu, m_ffn2_w_down, m_w_out, m_mem_norm, m_w_mem_kv, m_a_w_in, m_a_conv, m_a_A_log, m_a_dt_bias, m_a_out_norm, m_b_w_in, m_b_q_norm, m_b_w_uq, m_kv_in_norm, m_w_dkv, m_kv_lat_norm, m_w_ukv, m_final_norm, v_ffn1_norm, v_ffn1_w_gu, v_ffn1_w_down, v_mix_norm, v_ffn2_norm, v_ffn2_w_gu, v_ffn2_w_down, v_w_out, v_mem_norm, v_w_mem_kv, v_a_w_in, v_a_conv, v_a_A_log, v_a_dt_bias, v_a_out_norm, v_b_w_in, v_b_q_norm, v_b_w_uq, v_kv_in_norm, v_w_dkv, v_kv_lat_norm, v_w_ukv, v_final_norm):
    loc = dict(locals())
    wl = {n: loc[n] for n in _WEIGHTS}
    ml = {n: loc["m_" + n] for n in _WEIGHTS}
    vl = {n: loc["v_" + n] for n in _WEIGHTS}

    me = 4 * lax.axis_index("x") + 2 * lax.axis_index("y") + lax.axis_index("c")
    w = {n: wl[n] for n in _REPLICATED}
    for n in _SHARDED:
        w[n] = [None] * _LAYERED[n] if n in _LAYERED else None
    back = {}

    def src_of(item):
        n, lo, hi = item
        return (wl[n] if lo is None else wl[n][lo:hi]).astype(BF16)

    def install(items, pieces):
        for (n, lo, hi), p in zip(items, pieces):
            if lo is None:
                w[n] = _full_from_shards(n, p)
            else:
                for l in range(lo, hi):
                    w[n][l] = _full_from_shards(n, p[:, l - lo])
        for n, lo, hi in items:
            if n == "a_w_in":
                for l in range(lo, hi):
                    w[n][l], back[(n, l)] = _ext_and_back(_a_in_ext, w[n][l])
            elif n == "a_conv":
                for l in range(lo, hi):
                    w[n][l] = w[n][l].astype(F32)
            elif n == "b_w_uq":
                for l in range(lo, hi):
                    w[n][l], back[(n, l)] = _ext_and_back(_uq_ext, w[n][l])
            elif n == "w_dkv":
                w[n], back[n] = _ext_and_back(_dkv_ext, w[n])
            elif n == "w_ukv":
                w[n], back[n] = _ext_and_back(_ukv_ext, w[n])

    first = _exchange([src_of(it) for it in _GATHER_FIRST], True, "gather_first")
    later = {key: [src_of(it) for it in items] for key, items in (("mix0", _GATHER_MIX0), ("rest", _GATHER_REST))}
    first, later = lax.optimization_barrier((first, later))
    install(_GATHER_FIRST, first)
    flights = {}
    for key, items in (("mix0", _GATHER_MIX0), ("rest", _GATHER_REST)):
        srcs = later[key]
        flights[key] = (items, srcs, _split_start(srcs, True, f"gather_{key}_start"))
        w["ffn1_norm"] = w["ffn1_norm"] + flights[key][2][-1][0, 0]

    def arrive(key, stream):
        items, srcs, fl = flights[key]
        lands = _split_wait(fl, stream, True, f"gather_{key}_wait")
        install(items, [lax.dynamic_update_slice(ld, s[None], (me,) + (0,) * s.ndim) for ld, s in zip(lands, srcs)])

    def layer_start(l, stream):
        if l == 1:
            arrive("rest", stream)

    def mixer_start(l, stream):
        if l == 0:
            arrive("mix0", stream)

    def grad_src(item, g):
        n, lo, hi = item

        def one(l):
            gl = g[n] if l is None else g[n][l]
            key = n if l is None else (n, l)
            if key in back:
                gl = back[key](gl)
            return _shards_from_full(n, gl).astype(BF16)

        return one(None) if lo is None else jnp.stack([one(l) for l in range(lo, hi)], axis=1)

    sent = {}

    def depart(key, items, g, d):
        srcs = [grad_src(it, g) for it in items]
        sent[key] = (items, srcs, _split_start(srcs, False, f"scatter_{key}_start"))
        return lax.optimization_barrier((d, sent[key][2][-1]))[0]

    def layer_done(l, g, d):
        if l == 2:
            return depart("hi", _SCATTER_HI, g, d)
        if l == 1:
            return depart("mid", _SCATTER_MID, g, d)
        return d

    def mixer_done(l, g, d):
        return depart("mix0", _SCATTER_MIX0, g, d) if l == 0 else d

    loss, dx, g = _local_step(x[0], mem[0], positions[0], loss_target[0], w, layer_start, mixer_start,
                              mixer_done, layer_done)
    loss = lax.psum(loss, ("x", "y", "c"))

    lo_srcs = [grad_src(it, g) for it in _SCATTER_LO]
    pieces = {n: [] for n in _SHARDED}
    for (n, lo, hi), p in zip(_SCATTER_LO, _exchange(lo_srcs, False, "scatter_grads")):
        pieces[n].append((lo, p))
    for key in ("mix0", "mid", "hi"):
        items, srcs, fl = sent[key]
        lands = _split_wait(fl, dx, False, f"scatter_{key}_wait")
        for (n, lo, hi), s, ld in zip(items, srcs, lands):
            mine = lax.dynamic_slice(s, (me,) + (0,) * (s.ndim - 1), (1,) + s.shape[1:])
            pieces[n].append((lo, lax.dynamic_update_slice(ld, mine, (me,) + (0,) * (s.ndim - 1))))
    out = {}
    for n in _SHARDED:
        ps = [p for _, p in sorted(pieces[n], key=lambda e: -1 if e[0] is None else e[0])]
        p = ps[0] if len(ps) == 1 else jnp.concatenate(ps, axis=1)
        shape = wl[n].shape
        res = _reduce_adamw(p.reshape(N_DEV, -1, shape[-1]), _as2d(wl[n]), _as2d(ml[n]), _as2d(vl[n]), "adamw_" + n)
        for kind, buf in zip(("grad", "delta", "new_m", "new_v"), res):
            out[(kind, n)] = buf.reshape(shape)

    rep_shapes = [wl[n].shape for n in _REPLICATED]
    grep = [jnp.stack(g[n]) if isinstance(g[n], list) else g[n] for n in _REPLICATED]
    (rparts,) = _exchange([_pack(grep)], True, "gather_small_grads")
    res = _reduce_adamw(rparts, _pack([wl[n] for n in _REPLICATED]), _pack([ml[n] for n in _REPLICATED]),
                        _pack([vl[n] for n in _REPLICATED]), "adamw_replicated")
    for kind, buf in zip(("grad", "delta", "new_m", "new_v"), res):
        for n, a in zip(_REPLICATED, _unpack(buf, rep_shapes)):
            out[(kind, n)] = a

    return (loss, dx[None], *[out[("grad", n)] for n in _WEIGHTS], *[out[("delta", n)] for n in _WEIGHTS],
            *[out[("new_m", n)] for n in _WEIGHTS], *[out[("new_v", n)] for n in _WEIGHTS])
```

```python
import functools

import numpy as np
import jax
import jax.numpy as jnp
from jax import lax
from jax.experimental import pallas as pl
from jax.experimental.pallas import tpu as pltpu

F32 = jnp.float32
BF16 = jnp.bfloat16

N_DEV = 8
D = 1024
D_FF = 2816
FF_SHARD = 2 * D_FF // N_DEV
DEPTH = 4
N_A = 2
N_B = 2
EPS = 1e-6
CHUNK = 64
GROUP = 256
GDN_HPS = 3
A_HEADS = 6
HD = 128
A_WIDTH = A_HEADS * HD
B_HEADS = 6
QK_NOPE = 128
QK_ROPE = 64
QK_CAT = 256
Q_LORA = 256
KV_LORA = 256
MEM_HEADS = 4
MEM_HD = 64
MEM_W = 256
N_MEM = 256
ROPE_THETA = 10000.0
ATT_SCALE = (QK_NOPE + QK_ROPE) ** -0.5
LN2 = 0.6931471805599453
Q_PRESCALE = ATT_SCALE / LN2
A_IN = 4 * A_WIDTH + 2 * A_HEADS + MEM_W
A_MQ_BLK = 4 * A_WIDTH // MEM_W
A_BA_BLK = (4 * A_WIDTH + MEM_W) // 128

ADAM_LR = 0.001
ADAM_B1 = 0.9
ADAM_B2 = 0.999
ADAM_EPS = 1e-08
ADAM_WD = 0.01
ADAM_STEP = 10

VMEM_LIMIT = 56 * 1024 * 1024
TM = 512
TMM = 1024
TMF = 2048
TMH = 2048
ATT_BQ = 1024
PACK_W = 1024
PACK_ROWS = 32


def _cparams(sem):
    return pltpu.CompilerParams(dimension_semantics=sem, vmem_limit_bytes=VMEM_LIMIT)


_DIMS = {"nn": ((1,), (0,)), "nt": ((1,), (1,)), "tn": ((0,), (0,))}


def _dot(a, b, dims="nn"):
    return lax.dot_general(a.astype(BF16), b.astype(BF16), (_DIMS[dims], ((), ())),
                           preferred_element_type=F32)


def _matmul(a, b, *, dims, grid, a_spec, b_spec, o_spec, out_shape, name, scale=1.0,
            res=None, res_spec=None, a_fn=None, epilogue=None, acc_shape=None, chunks=1):
    nk = grid[-1]
    kax = len(grid) - 1
    if acc_shape is None:
        acc_shape = tuple(s for s in o_spec.block_shape if s is not None)
    a_rows = a_spec.block_shape[-2] // chunks

    def rows_of(ref, c):
        rs = slice(c * a_rows, (c + 1) * a_rows)
        return ref[:, rs, :] if len(ref.shape) == 3 else ref[rs, :]

    def body(*refs):
        if res is None:
            a_ref, b_ref, o_ref, acc = refs
            r_ref = None
        else:
            a_ref, b_ref, r_ref, o_ref, acc = refs
        k = pl.program_id(kax)

        if chunks > 1 and nk == 1 and dims != "tn":
            for c in range(chunks):
                a_c = rows_of(a_ref, c)
                y = _dot(a_c if a_fn is None else a_fn(a_c), b_ref[...], dims) * scale
                rs = slice(c * a_rows, (c + 1) * a_rows)
                y = epilogue(y, rows_of(r_ref, c))
                if len(o_ref.shape) == 3:
                    o_ref[:, rs, :] = y.astype(o_ref.dtype)
                else:
                    o_ref[rs, :] = y.astype(o_ref.dtype)
            return

        @pl.when(k == 0)
        def _():
            acc[...] = jnp.zeros_like(acc)

        for c in range(chunks):
            a_c = rows_of(a_ref, c) if chunks > 1 else a_ref[...]
            a_c = a_c if a_fn is None else a_fn(a_c)
            rs = slice(c * a_rows, (c + 1) * a_rows)
            if chunks == 1:
                acc[...] += _dot(a_c, b_ref[...], dims)
            elif dims == "tn":
                acc[...] += _dot(a_c, b_ref[rs, :], dims)
            else:
                acc[rs, :] += _dot(a_c, b_ref[...], dims)

        @pl.when(k == nk - 1)
        def _():
            y = acc[...] * scale
            if epilogue is not None:
                y = epilogue(y, r_ref[...])
            elif r_ref is not None:
                y = y + r_ref[...].astype(F32)
            o_ref[...] = y.astype(o_ref.dtype)

    args = [a, b] + ([res] if res is not None else [])
    specs = [a_spec, b_spec] + ([res_spec] if res is not None else [])
    sem = ("parallel",) * kax + ("arbitrary",)
    return pl.pallas_call(
        body, out_shape=out_shape, grid=grid, in_specs=specs, out_specs=o_spec,
        scratch_shapes=[pltpu.VMEM(acc_shape, F32)], name=name, compiler_params=_cparams(sem))(*args)


def _tile(n, cap):
    if n <= cap:
        return n
    t = cap - cap % 128
    while t >= 128:
        if n % t == 0:
            return t
        t -= 128
    raise ValueError(f"no tile for {n}")


def _mm(a, b, dims, out_dtype, name, scale=1.0, res=None):
    if dims == "tn":
        kk, m = a.shape
        n = b.shape[1]
        tk, tn = _tile(kk, TMM), _tile(n, 1152)
        return _matmul(a, b, dims=dims, grid=(1, n // tn, kk // tk),
                       a_spec=pl.BlockSpec((tk, m), lambda i, j, k: (k, 0)),
                       b_spec=pl.BlockSpec((tk, tn), lambda i, j, k: (k, j)),
                       o_spec=pl.BlockSpec((m, tn), lambda i, j, k: (0, j)),
                       out_shape=jax.ShapeDtypeStruct((m, n), out_dtype), name=name, scale=scale)
    m, kk = a.shape
    n = b.shape[1] if dims == "nn" else b.shape[0]
    tm, tn, tk = _tile(m, TMM), _tile(n, 1152), _tile(kk, 1536)
    if dims == "nn":
        b_spec = pl.BlockSpec((tk, tn), lambda i, j, k: (k, j))
    else:
        b_spec = pl.BlockSpec((tn, tk), lambda i, j, k: (j, k))
    o_spec = pl.BlockSpec((tm, tn), lambda i, j, k: (i, j))
    return _matmul(a, b, dims=dims, grid=(m // tm, n // tn, kk // tk),
                   a_spec=pl.BlockSpec((tm, tk), lambda i, j, k: (i, k)), b_spec=b_spec, o_spec=o_spec,
                   out_shape=jax.ShapeDtypeStruct((m, n), out_dtype), name=name, scale=scale,
                   res=res, res_spec=o_spec if res is not None else None)


def _rowcall(fn, args, in_specs, out_shapes, out_specs, grid, name, n_acc=0):
    n_in, n_out = len(args), len(out_shapes)

    def body(*refs):
        outs = fn(*[r[...] for r in refs[:n_in]])
        if not isinstance(outs, (tuple, list)):
            outs = (outs,)
        first = pl.program_id(0) == 0
        for ax in range(1, len(grid)):
            first = jnp.logical_and(first, pl.program_id(ax) == 0)
        for idx, (o_ref, val) in enumerate(zip(refs[n_in:], outs)):
            if idx >= n_out - n_acc:
                @pl.when(first)
                def _(o_ref=o_ref):
                    o_ref[...] = jnp.zeros_like(o_ref)

                o_ref[...] += val.astype(o_ref.dtype)
            else:
                o_ref[...] = val.astype(o_ref.dtype)

    sem = (("arbitrary",) if n_acc else ("parallel",)) * len(grid)
    res = pl.pallas_call(body, out_shape=tuple(out_shapes), grid=grid, in_specs=list(in_specs),
                         out_specs=tuple(out_specs), name=name, compiler_params=_cparams(sem))(*args)
    return res


def _vjp_fn(fn, n_in, wrt):
    def bwd(*blocks):
        ins = [b.astype(F32) for b in blocks[:n_in]]
        cts = [c.astype(F32) for c in blocks[n_in:]]
        outs, vjp = jax.vjp(fn, *ins)
        if isinstance(outs, (tuple, list)):
            grads = vjp(tuple(cts))
        else:
            grads = vjp(cts[0])
        return tuple(grads[i] for i in wrt)
    return bwd


def _sds(shape, dtype):
    return jax.ShapeDtypeStruct(tuple(shape), dtype)


def _rows(tm, w, col=0):
    return pl.BlockSpec((tm, w), lambda i, *_: (i, col))


def _shared(shape):
    nd = len(shape)
    return pl.BlockSpec(tuple(shape), lambda *_: (0,) * nd)


def _rms(x, g):
    return x * lax.rsqrt(jnp.mean(x * x, axis=-1, keepdims=True) + EPS) * g


def _silu(x):
    return x * jax.nn.sigmoid(x)


def _swiglu_pair(gu):
    return _silu(gu[0].astype(F32)) * gu[1].astype(F32)


def _swiglu_bwd(dh, gu):
    g, u = gu[0].astype(F32), gu[1].astype(F32)
    sg = jax.nn.sigmoid(g)
    return jnp.stack([dh * u * sg * (1.0 + g * (1.0 - sg)), dh * g * sg])


def _gdn_prep(q, k, v):
    q, k, v = _silu(q), _silu(k), _silu(v)
    q = q * lax.rsqrt(jnp.sum(q * q, axis=-1, keepdims=True) + EPS) * (HD ** -0.5)
    k = k * lax.rsqrt(jnp.sum(k * k, axis=-1, keepdims=True) + EPS)
    return q, k, v


def _gates(ba, a_log, dt_bias):
    lane = lax.broadcasted_iota(jnp.int32, ba.shape, 1)
    beta = jax.nn.sigmoid(ba)
    z = ba + dt_bias
    softplus = jnp.maximum(z, 0.0) + jnp.log(1.0 + jnp.exp(-jnp.abs(z)))
    g = -jnp.exp(a_log) * softplus
    return jnp.where(lane < A_HEADS, beta, jnp.where(lane < 2 * A_HEADS, g, 0.0))


def _outnorm_gate(o, gate, gain):
    return _rms(o, gain) * _silu(gate)


def _memattn(q, k, v):
    lane = lax.shift_right_logical(lax.broadcasted_iota(jnp.int32, (1, MEM_W), 1), 6)
    out = jnp.zeros(q.shape, F32)
    for h in range(MEM_HEADS):
        mh = (lane == h).astype(F32)
        s = _dot(q * mh, k, "nt") * (MEM_HD ** -0.5)
        s = s - lax.stop_gradient(jnp.max(s, axis=-1, keepdims=True))
        p = jnp.exp(s)
        p = p / jnp.sum(p, axis=-1, keepdims=True)
        out = out + _dot(p, v * mh)
    return out


def _rope_mix(a, a_sw, c, s):
    return (a * c + a_sw * s) * Q_PRESCALE


def _kcat(kn, kr, kr_sw, c, s):
    return kn + kr * c + kr_sw * s


@jax.custom_vjp
def _neumann_inverse(nmat):
    n = nmat.shape[0]
    eye = (lax.broadcasted_iota(jnp.int32, (n, n), 0) == lax.broadcasted_iota(jnp.int32, (n, n), 1)).astype(F32)
    pinv = eye + nmat
    npow = nmat
    for _ in range(5):
        npow = _dot(npow, npow)
        pinv = pinv + _dot(pinv, npow)
    return pinv


def _neumann_inverse_fwd(nmat):
    pinv = _neumann_inverse(nmat)
    return pinv, pinv


def _neumann_inverse_bwd(pinv, ct):
    return (_dot(_dot(pinv, ct, "tn"), pinv, "nt"),)


_neumann_inverse.defvjp(_neumann_inverse_fwd, _neumann_inverse_bwd)


@jax.custom_vjp
def _known_inverse(nmat, pinv):
    return pinv


def _known_inverse_fwd(nmat, pinv):
    return pinv, pinv


def _known_inverse_bwd(pinv, ct):
    return _dot(_dot(pinv, ct, "tn"), pinv, "nt"), jnp.zeros_like(pinv)


_known_inverse.defvjp(_known_inverse_fwd, _known_inverse_bwd)


def _gdn_local(q, k, v, beta, gcol, grow, pinv_kept=None):
    n = GROUP
    ri = lax.broadcasted_iota(jnp.int32, (n, n), 0)
    ci = lax.broadcasted_iota(jnp.int32, (n, n), 1)
    same = lax.shift_right_logical(ri, 6) == lax.shift_right_logical(ci, 6)
    lower = jnp.logical_and(same, ci <= ri)
    strict = jnp.logical_and(same, ci < ri)
    gc_col = jnp.sum(lower.astype(F32) * grow, axis=1, keepdims=True)
    gc_row = jnp.sum(jnp.logical_and(same, ri <= ci).astype(F32) * gcol, axis=0, keepdims=True)
    glast = jnp.sum(same.astype(F32) * grow, axis=1, keepdims=True)
    decay = jnp.where(lower, jnp.exp(jnp.where(lower, gc_col - gc_row, 0.0)), 0.0)
    kb = k * beta
    nmat = -jnp.where(strict, _dot(kb, k, "nt") * decay, 0.0)
    pinv = _neumann_inverse(nmat) if pinv_kept is None else _known_inverse(nmat, pinv_kept)
    e_gc = jnp.exp(gc_col)
    u = _dot(pinv, v * beta)
    w = _dot(pinv, kb * e_gc)
    qk = _dot(q, k, "nt") * decay
    fold = (jnp.bitwise_and(lax.broadcasted_iota(jnp.int32, (n, CHUNK), 0), CHUNK - 1)
            == lax.broadcasted_iota(jnp.int32, (n, CHUNK), 1)).astype(F32)
    qk_c = _dot(qk, fold)
    q_dec = q * e_gc
    k_dec = k * jnp.exp(glast - gc_col)
    dmat = jnp.exp(glast) * jnp.ones((1, HD), F32)
    if pinv_kept is None:
        return u, w, q_dec, k_dec, qk_c, dmat, pinv
    return u, w, q_dec, k_dec, qk_c, dmat


def _gdn_step(s, w_c, u_c, qd_c, kd_c, qk_c, d_c):
    v_new = u_c - _dot(w_c, s)
    out = _dot(qd_c, s) + _dot(qk_c, v_new)
    d_row = jnp.mean(d_c, axis=0, keepdims=True)
    s_new = s * d_row + _dot(kd_c, v_new, "tn")
    return s_new, out


def _rmsnorm_fwd(x, gain, name, col=0, width=None):
    t = x.shape[0]
    w = width or x.shape[1]
    (n,) = _rowcall(_rms, [x, gain.reshape(1, w)], [_rows(TM, w, col), _shared((1, w))],
                    [_sds((t, w), BF16)], [_rows(TM, w)], (t // TM,), name)
    return n


def _rmsnorm_bwd(x, gain, dn, dres, name):
    t, w = x.shape
    fn = _vjp_fn(_rms, 2, (0, 1))

    def bwd(xb, gb, dnb, drb):
        dx, dg = fn(xb, gb, dnb)
        return dx + drb, dg

    dx, dg = _rowcall(bwd, [x, gain.reshape(1, w), dn, dres],
                      [_rows(TM, w), _shared((1, w)), _rows(TM, w), _rows(TM, w)],
                      [_sds((t, w), F32), _sds((1, w), F32)], [_rows(TM, w), _shared((1, w))],
                      (t // TM,), name, n_acc=1)
    return dx, dg[0]


def _ffn_fwd(x, gain, wgu8, wd4, tag):
    t = x.shape[0]
    nt = t // TMM
    tf = min(TMF, t)
    n = _rmsnorm_fwd(x, gain, tag + "_norm")
    gu = _matmul(n, wgu8, dims="nn", grid=(N_DEV, t // tf, 1),
                 a_spec=pl.BlockSpec((tf, D), lambda j, i, k: (i, 0)),
                 b_spec=pl.BlockSpec((None, D, FF_SHARD), lambda j, i, k: (j, 0, 0)),
                 o_spec=pl.BlockSpec((None, tf, FF_SHARD), lambda j, i, k: (j, i, 0)),
                 out_shape=_sds((N_DEV, t, FF_SHARD), BF16), name=tag + "_gu")
    gu = gu.reshape(2, 4, t, FF_SHARD)
    y = _matmul(gu, wd4, dims="nn", grid=(nt, 1, 4), a_fn=_swiglu_pair, chunks=4,
                a_spec=pl.BlockSpec((2, None, TMM, FF_SHARD), lambda i, j, k: (0, k, i, 0)),
                b_spec=pl.BlockSpec((None, FF_SHARD, D), lambda i, j, k: (k, 0, 0)),
                o_spec=pl.BlockSpec((TMM, D), lambda i, j, k: (i, 0)),
                out_shape=_sds((t, D), F32), name=tag + "_down", scale=0.5,
                res=x, res_spec=pl.BlockSpec((TMM, D), lambda i, j, k: (i, 0)))
    return y, (x, n, gu)


def _ffn_bwd(d, saved, gain, wgu8, wd4, tag):
    x, n, gu = saved
    t = x.shape[0]
    nt = t // TMM
    dgu = _matmul(d, wd4, dims="nt", grid=(4, nt, 1),
                  a_spec=pl.BlockSpec((TMM, D), lambda j, i, k: (i, 0)),
                  b_spec=pl.BlockSpec((None, FF_SHARD, D), lambda j, i, k: (j, 0, 0)),
                  o_spec=pl.BlockSpec((2, None, TMM, FF_SHARD), lambda j, i, k: (0, j, i, 0)),
                  out_shape=_sds((2, 4, t, FF_SHARD), BF16), name=tag + "_dgu", scale=0.5,
                  res=gu, res_spec=pl.BlockSpec((2, None, TMM, FF_SHARD), lambda j, i, k: (0, j, i, 0)),
                  epilogue=_swiglu_bwd, acc_shape=(TMM, FF_SHARD))
    tf = min(TMF, t)
    nf = t // tf
    dwd4 = _matmul(gu, d, dims="tn", grid=(4, 1, nf), a_fn=_swiglu_pair, chunks=4,
                   a_spec=pl.BlockSpec((2, None, tf, FF_SHARD), lambda j, i, k: (0, j, k, 0)),
                   b_spec=pl.BlockSpec((tf, D), lambda j, i, k: (k, 0)),
                   o_spec=pl.BlockSpec((None, FF_SHARD, D), lambda j, i, k: (j, 0, 0)),
                   out_shape=_sds((4, FF_SHARD, D), F32), name=tag + "_dwd", scale=0.5)
    dgu = dgu.reshape(N_DEV, t, FF_SHARD)
    dwgu8 = _matmul(n, dgu, dims="tn", grid=(N_DEV, 1, nf),
                    a_spec=pl.BlockSpec((tf, D), lambda j, i, k: (k, 0)),
                    b_spec=pl.BlockSpec((None, tf, FF_SHARD), lambda j, i, k: (j, k, 0)),
                    o_spec=pl.BlockSpec((None, D, FF_SHARD), lambda j, i, k: (j, 0, 0)),
                    out_shape=_sds((N_DEV, D, FF_SHARD), F32), name=tag + "_dwgu")
    dn = _matmul(dgu, wgu8, dims="nt", grid=(nf, 1, N_DEV),
                 a_spec=pl.BlockSpec((None, tf, FF_SHARD), lambda i, j, k: (k, i, 0)),
                 b_spec=pl.BlockSpec((None, D, FF_SHARD), lambda i, j, k: (k, 0, 0)),
                 o_spec=pl.BlockSpec((tf, D), lambda i, j, k: (i, 0)),
                 out_shape=_sds((t, D), BF16), name=tag + "_dn")
    dx, dgain = _rmsnorm_bwd(x, gain, dn, d, tag + "_dnorm")
    return dx, dgain, dwgu8, dwd4


CONV_TC = 768
CONV_K = 4
CONV_TM = 1024


def _conv_fwd(ha, w, name):
    t = ha.shape[0]
    nb = CONV_TM // 8

    def body(prev_ref, cur_ref, w_ref, o_ref):
        i = pl.program_id(0)
        cur = cur_ref[...].astype(F32)
        prev = prev_ref[...].astype(F32) * (i > 0).astype(F32)
        ext = jnp.concatenate([prev, cur], axis=0)
        wv = w_ref[...]
        acc = cur * wv[3:4]
        for k in range(1, CONV_K):
            acc = acc + pltpu.roll(ext, k, axis=0)[8:] * wv[3 - k:4 - k]
        o_ref[...] = acc.astype(o_ref.dtype)

    return pl.pallas_call(
        body, out_shape=_sds((3, t, CONV_TC), BF16), grid=(t // CONV_TM, 3),
        in_specs=[pl.BlockSpec((8, CONV_TC), lambda i, c: (jnp.maximum(i * nb - 1, 0), c)),
                  pl.BlockSpec((CONV_TM, CONV_TC), lambda i, c: (i, c)),
                  pl.BlockSpec((CONV_K, CONV_TC), lambda i, c: (0, c))],
        out_specs=pl.BlockSpec((None, CONV_TM, CONV_TC), lambda i, c: (c, i, 0)),
        name=name, compiler_params=_cparams(("parallel", "parallel")))(ha, ha, w)


def _conv_bwd(ha, w, dy3, name):
    t = ha.shape[0]
    nb = CONV_TM // 8
    nt = t // CONV_TM

    def body(prev_ref, cur_ref, dy_ref, nxt_ref, w_ref, dx_ref, dw_ref):
        i = pl.program_id(1)
        cur = cur_ref[...].astype(F32)
        prev = prev_ref[...].astype(F32) * (i > 0).astype(F32)
        ext = jnp.concatenate([prev, cur], axis=0)
        dy = dy_ref[...].astype(F32)
        nxt = nxt_ref[...].astype(F32) * (i < nt - 1).astype(F32)
        dext = jnp.concatenate([dy, nxt], axis=0)
        wv = w_ref[...]
        dx = dy * wv[3:4]
        dws = [None] * CONV_K
        dws[3] = jnp.sum(dy * cur, axis=0, keepdims=True)
        for k in range(1, CONV_K):
            dx = dx + pltpu.roll(dext, CONV_TM + 8 - k, axis=0)[:CONV_TM] * wv[3 - k:4 - k]
            dws[3 - k] = jnp.sum(dy * pltpu.roll(ext, k, axis=0)[8:], axis=0, keepdims=True)
        dx_ref[...] = dx.astype(dx_ref.dtype)

        @pl.when(i == 0)
        def _():
            dw_ref[...] = jnp.zeros_like(dw_ref)

        dw_ref[...] += jnp.concatenate(dws, axis=0)

    return pl.pallas_call(
        body, out_shape=(_sds((t, 3 * CONV_TC), BF16), _sds((CONV_K, 3 * CONV_TC), F32)), grid=(3, nt),
        in_specs=[pl.BlockSpec((8, CONV_TC), lambda c, i: (jnp.maximum(i * nb - 1, 0), c)),
                  pl.BlockSpec((CONV_TM, CONV_TC), lambda c, i: (i, c)),
                  pl.BlockSpec((None, CONV_TM, CONV_TC), lambda c, i: (c, i, 0)),
                  pl.BlockSpec((None, 8, CONV_TC), lambda c, i: (c, jnp.minimum((i + 1) * nb, t // 8 - 1), 0)),
                  pl.BlockSpec((CONV_K, CONV_TC), lambda c, i: (0, c))],
        out_specs=(pl.BlockSpec((CONV_TM, CONV_TC), lambda c, i: (i, c)),
                   pl.BlockSpec((CONV_K, CONV_TC), lambda c, i: (0, c))),
        name=name, compiler_params=_cparams(("parallel", "arbitrary")))(ha, ha, dy3, dy3, w)


def _gdn_specs(t, rev):
    ng = t // GROUP

    def gi(g):
        return ng - 1 - g if rev else g

    qkv = pl.BlockSpec((3, GROUP, GDN_HPS * HD), lambda h, g: (0, gi(g), h))
    bg = pl.BlockSpec((GROUP, 128), lambda h, g: (gi(g), 0))
    dbg = pl.BlockSpec((GDN_HPS, GROUP, 128), lambda h, g: (h, gi(g), 0))
    o = pl.BlockSpec((GROUP, GDN_HPS * HD), lambda h, g: (gi(g), h))
    st = pl.BlockSpec((GDN_HPS, None, HD, HD), lambda h, g: (h, gi(g), 0, 0))
    inv = pl.BlockSpec((GDN_HPS, None, GROUP, GROUP), lambda h, g: (h, gi(g), 0, 0))
    return qkv, bg, dbg, o, st, inv


def _head_qkv(qkv_ref, j):
    sl = slice(j * HD, (j + 1) * HD)
    return qkv_ref[0, :, sl].astype(F32), qkv_ref[1, :, sl].astype(F32), qkv_ref[2, :, sl].astype(F32)


def _head_gates(bg, h):
    lane = lax.broadcasted_iota(jnp.int32, (1, 128), 1)
    beta = jnp.sum(jnp.where(lane == h, bg, 0.0), axis=1, keepdims=True)
    gcol = jnp.sum(jnp.where(lane == h + A_HEADS, bg, 0.0), axis=1, keepdims=True)
    return beta, gcol, _col_to_row(gcol)


def _gdn_fwd(qkv3, bg, name):
    t = qkv3.shape[1]
    ng = t // GROUP
    qkv_s, bg_s, _, o_s, st_s, inv_s = _gdn_specs(t, False)

    def body(qkv_ref, bg_ref, o_ref, st_ref, inv_ref, s_scr):
        @pl.when(pl.program_id(1) == 0)
        def _():
            s_scr[...] = jnp.zeros_like(s_scr)

        st_ref[...] = s_scr[...]
        bgv = bg_ref[...]
        loc = [_gdn_local(*_head_qkv(qkv_ref, j), *_head_gates(bgv, pl.program_id(0) * GDN_HPS + j))
               for j in range(GDN_HPS)]
        s = [s_scr[j] for j in range(GDN_HPS)]
        for a in range(GROUP // CHUNK):
            sl = slice(a * CHUNK, (a + 1) * CHUNK)
            for j in range(GDN_HPS):
                u, w, qd, kd, qkc, dm, _ = loc[j]
                s[j], out = _gdn_step(s[j], w[sl], u[sl], qd[sl], kd[sl], qkc[sl], dm[sl])
                o_ref[sl, j * HD:(j + 1) * HD] = out.astype(o_ref.dtype)
        for j in range(GDN_HPS):
            s_scr[j] = s[j]
            inv_ref[j] = loc[j][6].astype(inv_ref.dtype)

    return pl.pallas_call(
        body, out_shape=(_sds((t, A_WIDTH), BF16), _sds((A_HEADS, ng, HD, HD), F32),
                         _sds((A_HEADS, ng, GROUP, GROUP), BF16)),
        grid=(A_HEADS // GDN_HPS, ng), in_specs=[qkv_s, bg_s], out_specs=(o_s, st_s, inv_s),
        scratch_shapes=[pltpu.VMEM((GDN_HPS, HD, HD), F32)], name=name,
        compiler_params=_cparams(("parallel", "arbitrary")))(qkv3, bg)


def _gdn_bwd(qkv3, bg, states, pinvs, do, name):
    t = qkv3.shape[1]
    ng = t // GROUP
    qkv_s, bg_s, dbg_s, o_s, st_s, inv_s = _gdn_specs(t, True)
    nc = GROUP // CHUNK

    def body(qkv_ref, bg_ref, st_ref, inv_ref, do_ref, dqkv_ref, dbg_ref, ds_scr):
        @pl.when(pl.program_id(1) == 0)
        def _():
            ds_scr[...] = jnp.zeros_like(ds_scr)

        heads = range(GDN_HPS)
        bgv = bg_ref[...]
        hid = [pl.program_id(0) * GDN_HPS + j for j in heads]
        fw = [jax.vjp(functools.partial(_gdn_local, pinv_kept=inv_ref[j].astype(F32)),
                      *_head_qkv(qkv_ref, j), *_head_gates(bgv, hid[j])) for j in heads]
        starts = [[None] * nc for _ in heads]
        s = [st_ref[j] for j in heads]
        for a in range(nc):
            sl = slice(a * CHUNK, (a + 1) * CHUNK)
            for j in heads:
                u, w, qd, kd, qkc, dm = fw[j][0]
                starts[j][a] = s[j]
                if a < nc - 1:
                    s[j], _ = _gdn_step(s[j], w[sl], u[sl], qd[sl], kd[sl], qkc[sl], dm[sl])
        ds = [ds_scr[j] for j in heads]
        parts = [[None] * nc for _ in heads]
        for a in reversed(range(nc)):
            sl = slice(a * CHUNK, (a + 1) * CHUNK)
            for j in heads:
                u, w, qd, kd, qkc, dm = fw[j][0]
                _, vjp_step = jax.vjp(_gdn_step, starts[j][a], w[sl], u[sl], qd[sl], kd[sl], qkc[sl], dm[sl])
                grads = vjp_step((ds[j], do_ref[sl, j * HD:(j + 1) * HD].astype(F32)))
                ds[j] = grads[0]
                parts[j][a] = grads[1:]
        lane = lax.broadcasted_iota(jnp.int32, (1, 128), 1)
        for j in heads:
            ds_scr[j] = ds[j]
            dw, du, dqd, dkd, dqk, ddm = [jnp.concatenate([parts[j][a][i] for a in range(nc)], axis=0)
                                          for i in range(6)]
            dq, dk, dv, db, dgc, dgr = fw[j][1]((du, dw, dqd, dkd, dqk, ddm))
            hs = slice(j * HD, (j + 1) * HD)
            dqkv_ref[0, :, hs] = dq.astype(dqkv_ref.dtype)
            dqkv_ref[1, :, hs] = dk.astype(dqkv_ref.dtype)
            dqkv_ref[2, :, hs] = dv.astype(dqkv_ref.dtype)
            dbg_ref[j] = (jnp.where(lane == hid[j], db, 0.0)
                          + jnp.where(lane == hid[j] + A_HEADS, dgc + _row_to_col(dgr), 0.0))

    return pl.pallas_call(
        body, out_shape=(_sds((3, t, A_WIDTH), BF16), _sds((A_HEADS, t, 128), F32)),
        grid=(A_HEADS // GDN_HPS, ng), in_specs=[qkv_s, bg_s, st_s, inv_s, o_s],
        out_specs=(qkv_s, dbg_s), scratch_shapes=[pltpu.VMEM((GDN_HPS, HD, HD), F32)], name=name,
        compiler_params=_cparams(("parallel", "arbitrary")))(qkv3, bg, states, pinvs, do)


NEG = -1e30


def _diag_mask(shape, q_axis):
    qi = lax.shift_right_logical(lax.broadcasted_iota(jnp.int32, shape, q_axis), 6)
    ki = lax.shift_right_logical(lax.broadcasted_iota(jnp.int32, shape, 1 - q_axis), 6)
    return ki <= qi


def _col_to_row(col):
    n = col.shape[0]
    eye = lax.broadcasted_iota(jnp.int32, (n, n), 0) == lax.broadcasted_iota(jnp.int32, (n, n), 1)
    return jnp.sum(jnp.where(eye, col, 0.0), axis=0, keepdims=True)


def _row_to_col(row):
    n = row.shape[1]
    eye = lax.broadcasted_iota(jnp.int32, (n, n), 0) == lax.broadcasted_iota(jnp.int32, (n, n), 1)
    return jnp.sum(jnp.where(eye, row, 0.0), axis=1, keepdims=True)


def _blk(ref, i):
    return ref[pl.ds(pl.multiple_of(i * ATT_BQ, ATT_BQ), ATT_BQ), :]


def _att_fwd(qc, kc, v, name):
    t = qc.shape[0]
    nq = t // ATT_BQ

    def body(q_ref, k_ref, v_ref, o_ref, lse_ref, lser_ref, m_scr, l_scr, acc_scr):
        qb = pl.program_id(1)
        q = q_ref[...]
        m_scr[...] = jnp.full_like(m_scr, NEG)
        l_scr[...] = jnp.zeros_like(l_scr)
        acc_scr[...] = jnp.zeros_like(acc_scr)

        def step(kb, diag):
            s = _dot(q, _blk(k_ref, kb), "nt")
            if diag:
                s = jnp.where(_diag_mask(s.shape, 0), s, NEG)
            m_old = m_scr[...]
            m_new = jnp.maximum(m_old, jnp.max(s, axis=1, keepdims=True))
            alpha = jnp.exp2(m_old - m_new)
            p = jnp.exp2(s - m_new)
            l_scr[...] = alpha * l_scr[...] + jnp.sum(p, axis=1, keepdims=True)
            acc_scr[...] = alpha * acc_scr[...] + _dot(p, _blk(v_ref, kb))
            m_scr[...] = m_new

        def loop_body(kb, carry):
            step(kb, False)
            return carry

        lax.fori_loop(0, qb, loop_body, 0)
        step(qb, True)
        o_ref[...] = (acc_scr[...] / l_scr[...]).astype(o_ref.dtype)
        lse = m_scr[...] + jnp.log2(l_scr[...])
        lse_ref[...] = lse
        lser_ref[...] = _col_to_row(lse)

    return pl.pallas_call(
        body, out_shape=(_sds((t, B_HEADS * HD), BF16), _sds((B_HEADS, t, 1), F32),
                         _sds((B_HEADS, nq, 1, ATT_BQ), F32)), grid=(B_HEADS, nq),
        in_specs=[pl.BlockSpec((ATT_BQ, QK_CAT), lambda h, i: (i, h)),
                  pl.BlockSpec((t, QK_CAT), lambda h, i: (0, h)), pl.BlockSpec((t, HD), lambda h, i: (0, h))],
        out_specs=(pl.BlockSpec((ATT_BQ, HD), lambda h, i: (i, h)),
                   pl.BlockSpec((None, ATT_BQ, 1), lambda h, i: (h, i, 0)),
                   pl.BlockSpec((None, None, 1, ATT_BQ), lambda h, i: (h, i, 0, 0))),
        scratch_shapes=[pltpu.VMEM((ATT_BQ, 1), F32), pltpu.VMEM((ATT_BQ, 1), F32), pltpu.VMEM((ATT_BQ, HD), F32)],
        name=name, compiler_params=_cparams(("parallel", "arbitrary")))(qc, kc, v)


def _att_bwd(qc, kc, v, o, lse, lse_row, do, name):
    t = qc.shape[0]
    nq = t // ATT_BQ

    def dq_body(q_ref, k_ref, v_ref, do_ref, lse_ref, o_ref, dq_ref, dlr_ref, acc):
        qb = pl.program_id(1)
        q, dob, lse_b = q_ref[...], do_ref[...], lse_ref[...]
        dl_b = jnp.sum(o_ref[...].astype(F32) * dob.astype(F32), axis=1, keepdims=True)
        dlr_ref[...] = _col_to_row(dl_b)
        acc[...] = jnp.zeros_like(acc)

        def step(kb, diag):
            k = _blk(k_ref, kb)
            s = _dot(q, k, "nt")
            if diag:
                s = jnp.where(_diag_mask(s.shape, 0), s, NEG)
            p = jnp.exp2(s - lse_b)
            ds = p * (_dot(dob, _blk(v_ref, kb), "nt") - dl_b)
            acc[...] += _dot(ds, k)

        def loop_body(kb, carry):
            step(kb, False)
            return carry

        lax.fori_loop(0, qb, loop_body, 0)
        step(qb, True)
        dq_ref[...] = (acc[...] * ATT_SCALE).astype(dq_ref.dtype)

    qmap = lambda h, i: (i, h)
    colq = pl.BlockSpec((None, ATT_BQ, 1), lambda h, i: (h, i, 0))
    dq, delta_row = pl.pallas_call(
        dq_body, out_shape=(_sds((t, B_HEADS * QK_CAT), BF16), _sds((B_HEADS, nq, 1, ATT_BQ), F32)),
        grid=(B_HEADS, nq),
        in_specs=[pl.BlockSpec((ATT_BQ, QK_CAT), qmap), pl.BlockSpec((t, QK_CAT), lambda h, i: (0, h)),
                  pl.BlockSpec((t, HD), lambda h, i: (0, h)), pl.BlockSpec((ATT_BQ, HD), qmap), colq,
                  pl.BlockSpec((ATT_BQ, HD), qmap)],
        out_specs=(pl.BlockSpec((ATT_BQ, QK_CAT), qmap),
                   pl.BlockSpec((None, None, 1, ATT_BQ), lambda h, i: (h, i, 0, 0))),
        scratch_shapes=[pltpu.VMEM((ATT_BQ, QK_CAT), F32)], name=name + "_dq",
        compiler_params=_cparams(("parallel", "arbitrary")))(qc, kc, v, do, lse, o)

    def dkv_body(k_ref, v_ref, q_ref, do_ref, lser_ref, dlr_ref, dk_ref, dv_ref, dk_acc, dv_acc):
        kb = pl.program_id(1)
        k, vv = k_ref[...], v_ref[...]
        dk_acc[...] = jnp.zeros_like(dk_acc)
        dv_acc[...] = jnp.zeros_like(dv_acc)

        def step(qb, diag):
            q, dob = _blk(q_ref, qb), _blk(do_ref, qb)
            st = _dot(k, q, "nt")
            if diag:
                st = jnp.where(_diag_mask(st.shape, 1), st, NEG)
            pt = jnp.exp2(st - lser_ref[qb])
            dst = pt * (_dot(vv, dob, "nt") - dlr_ref[qb])
            dv_acc[...] += _dot(pt, dob)
            dk_acc[...] += _dot(dst, q)

        def loop_body(qb, carry):
            step(qb, False)
            return carry

        step(kb, True)
        lax.fori_loop(kb + 1, nq, loop_body, 0)
        dk_ref[...] = (dk_acc[...] * LN2).astype(dk_ref.dtype)
        dv_ref[...] = dv_acc[...].astype(dv_ref.dtype)

    kmap = lambda h, j: (j, h)
    rowq = pl.BlockSpec((None, nq, 1, ATT_BQ), lambda h, j: (h, 0, 0, 0))
    dk, dv = pl.pallas_call(
        dkv_body, out_shape=(_sds((t, B_HEADS * QK_CAT), BF16), _sds((t, B_HEADS * HD), BF16)),
        grid=(B_HEADS, nq),
        in_specs=[pl.BlockSpec((ATT_BQ, QK_CAT), kmap), pl.BlockSpec((ATT_BQ, HD), kmap),
                  pl.BlockSpec((t, QK_CAT), lambda h, j: (0, h)), pl.BlockSpec((t, HD), lambda h, j: (0, h)),
                  rowq, rowq],
        out_specs=(pl.BlockSpec((ATT_BQ, QK_CAT), kmap), pl.BlockSpec((ATT_BQ, HD), kmap)),
        scratch_shapes=[pltpu.VMEM((ATT_BQ, QK_CAT), F32), pltpu.VMEM((ATT_BQ, HD), F32)], name=name + "_dkv",
        compiler_params=_cparams(("parallel", "arbitrary")))(kc, v, qc, do, lse_row, delta_row)
    return dq, dk, dv


def _mem_fwd(hx, col, mkv, name):
    t = hx.shape[0]
    (o,) = _rowcall(_memattn, [hx, mkv, mkv],
                    [_rows(TM, MEM_W, col), pl.BlockSpec((N_MEM, MEM_W), lambda i: (0, 0)),
                     pl.BlockSpec((N_MEM, MEM_W), lambda i: (0, 1))],
                    [_sds((t, MEM_W), BF16)], [_rows(TM, MEM_W)], (t // TM,), name)
    return o


def _mem_bwd(hx, col, mkv, do, do_col, name):
    t = hx.shape[0]
    dq, dk, dv = _rowcall(_vjp_fn(_memattn, 3, (0, 1, 2)), [hx, mkv, mkv, do],
                          [_rows(TM, MEM_W, col), pl.BlockSpec((N_MEM, MEM_W), lambda i: (0, 0)),
                           pl.BlockSpec((N_MEM, MEM_W), lambda i: (0, 1)), _rows(TM, MEM_W, do_col)],
                          [_sds((t, MEM_W), BF16), _sds((N_MEM, MEM_W), F32), _sds((N_MEM, MEM_W), F32)],
                          [_rows(TM, MEM_W), _shared((N_MEM, MEM_W)), _shared((N_MEM, MEM_W))],
                          (t // TM,), name, n_acc=2)
    return dq, jnp.concatenate([dk, dv], axis=1)


def _a_in_ext(w):
    nb = 4 * A_WIDTH
    ba = jnp.pad(w[:, nb:nb + 2 * A_HEADS], ((0, 0), (0, 128 - 2 * A_HEADS)))
    return jnp.concatenate([w[:, :nb], w[:, nb + 2 * A_HEADS:], ba], axis=1)


def _swap_halves(w):
    return jnp.concatenate([w[..., QK_ROPE // 2:], w[..., :QK_ROPE // 2]], axis=-1)


def _uq_ext(w):
    w = w.reshape(Q_LORA, B_HEADS, QK_NOPE + QK_ROPE)
    nope, rope = w[..., :QK_NOPE], w[..., QK_NOPE:]
    z64 = jnp.zeros((Q_LORA, B_HEADS, QK_CAT - QK_NOPE - QK_ROPE), w.dtype)
    z128 = jnp.zeros((Q_LORA, B_HEADS, QK_NOPE), w.dtype)
    a = jnp.concatenate([nope, rope, z64], axis=-1).reshape(Q_LORA, B_HEADS * QK_CAT)
    b = jnp.concatenate([z128, _swap_halves(rope), z64], axis=-1).reshape(Q_LORA, B_HEADS * QK_CAT)
    return jnp.concatenate([a, b], axis=1)


def _dkv_ext(w):
    ckv, kr = w[:, :KV_LORA], w[:, KV_LORA:]
    z128 = jnp.zeros((D, QK_NOPE), w.dtype)
    z64 = jnp.zeros((D, QK_CAT - QK_NOPE - QK_ROPE), w.dtype)
    return jnp.concatenate([ckv, z128, kr, z64, z128, _swap_halves(kr), z64], axis=1)


def _ukv_ext(w):
    w = w.reshape(KV_LORA, B_HEADS, QK_NOPE + HD)
    kn, vv = w[..., :QK_NOPE], w[..., QK_NOPE:]
    z = jnp.zeros((KV_LORA, B_HEADS, QK_CAT - QK_NOPE), w.dtype)
    a = jnp.concatenate([kn, z], axis=-1).reshape(KV_LORA, B_HEADS * QK_CAT)
    return jnp.concatenate([a, vv.reshape(KV_LORA, B_HEADS * HD)], axis=1)


def _ext_and_back(fn, w):
    ext, back = jax.vjp(fn, w.astype(F32))
    return ext.astype(BF16), lambda g: back(g.astype(F32))[0]


def _rope_tables(pos_col):
    t = pos_col.shape[0]
    inv = (ROPE_THETA ** (-np.arange(0, QK_ROPE, 2, dtype=np.float32) / QK_ROPE)).astype(np.float32)
    inv_row = np.zeros((1, QK_CAT), np.float32)
    inv_row[0, QK_NOPE:QK_NOPE + QK_ROPE] = np.concatenate([inv, inv])
    sign = np.zeros((1, QK_CAT), np.float32)
    sign[0, QK_NOPE:QK_NOPE + QK_ROPE // 2] = -1.0
    sign[0, QK_NOPE + QK_ROPE // 2:QK_NOPE + QK_ROPE] = 1.0
    is_rope = np.abs(sign)
    is_nope = np.zeros((1, QK_CAT), np.float32)
    is_nope[0, :QK_NOPE] = 1.0

    def fn(p, inv_b, sign_b, rope_b, nope_b):
        ang = p.astype(F32) * inv_b
        return jnp.cos(ang) * rope_b + nope_b, jnp.sin(ang) * sign_b

    consts = [jnp.asarray(a) for a in (inv_row, sign, is_rope, is_nope)]
    return _rowcall(fn, [pos_col] + consts, [_rows(TM, 1)] + [_shared((1, QK_CAT))] * 4,
                    [_sds((t, QK_CAT), F32)] * 2, [_rows(TM, QK_CAT)] * 2, (t // TM,), "rope_tables")


def _local_step(x, mem, pos, target, w, layer_start, mixer_start, mixer_done, layer_done):
    t = x.shape[0]
    g = {}
    th = min(TMH, t)
    head6 = (t // th, A_HEADS)

    mem_n = _rmsnorm_fwd_small(mem, w["mem_norm"])
    rope_c, rope_s = _rope_tables(pos.reshape(t, 1))
    mkv = None

    saved = []
    for l in range(DEPTH):
        sv = {}
        layer_start(l, x)
        x, sv["ffn1"] = _ffn_fwd(x, w["ffn1_norm"][l], w["ffn1_w_gu"][l], w["ffn1_w_down"][l], "ffn1")
        sv["x1"] = x
        mixer_start(l, x)
        if mkv is None:
            mkv = [_mm(mem_n, w["w_mem_kv"][i], "nn", BF16, f"mkv{i}") for i in range(DEPTH)]
        n2 = _rmsnorm_fwd(x, w["mix_norm"][l], "mix_norm")
        sv["n2"] = n2
        if l < N_A:
            ha = _mm(n2, w["a_w_in"][l], "nn", BF16, "a_in")
            yc3 = _conv_fwd(ha, w["a_conv"][l], "a_conv")
            blk3 = pl.BlockSpec((3, th, HD), lambda i, h: (0, i, h))
            (qkv3,) = _rowcall(lambda b: jnp.stack(_gdn_prep(b[0].astype(F32), b[1].astype(F32), b[2].astype(F32))),
                               [yc3], [blk3], [_sds((3, t, A_WIDTH), BF16)], [blk3], head6, "a_prep")
            (bg,) = _rowcall(_gates, [ha, _pad128(w["a_A_log"][l], A_HEADS), _pad128(w["a_dt_bias"][l], A_HEADS)],
                             [_rows(TM, 128, A_BA_BLK), _shared((1, 128)), _shared((1, 128))],
                             [_sds((t, 128), F32)], [_rows(TM, 128)], (t // TM,), "a_gates")
            o_gdn, states, pinvs = _gdn_fwd(qkv3, bg, "a_gdn")
            (o_a,) = _rowcall(_outnorm_gate, [o_gdn, ha, w["a_out_norm"][l].reshape(1, HD)],
                              [pl.BlockSpec((th, HD), lambda i, h: (i, h)),
                               pl.BlockSpec((th, HD), lambda i, h: (i, 3 * A_HEADS + h)), _shared((1, HD))],
                              [_sds((t, A_WIDTH), BF16)], [pl.BlockSpec((th, HD), lambda i, h: (i, h))],
                              head6, "a_outnorm")
            o_m = _mem_fwd(ha, A_MQ_BLK, mkv[l], "mem_attn_a")
            sv.update(ha=ha, yc3=yc3, qkv3=qkv3, bg=bg, states=states, pinvs=pinvs, o_gdn=o_gdn)
            cat = jnp.concatenate([o_a, o_m], axis=1)
        else:
            j = l - N_A
            hb = _mm(n2, w["b_w_in"][j], "nn", BF16, "b_in")
            cqn = _rmsnorm_fwd(hb, w["b_q_norm"][j], "b_qnorm", 0, Q_LORA)
            qq = _mm(cqn, w["b_w_uq"][j], "nn", BF16, "b_uq")
            (qc,) = _rowcall(_rope_mix, [qq, qq, rope_c, rope_s],
                             [pl.BlockSpec((th, QK_CAT), lambda i, h: (i, h)),
                              pl.BlockSpec((th, QK_CAT), lambda i, h: (i, B_HEADS + h)),
                              pl.BlockSpec((th, QK_CAT), lambda i, h: (i, 0)),
                              pl.BlockSpec((th, QK_CAT), lambda i, h: (i, 0))],
                             [_sds((t, B_HEADS * QK_CAT), BF16)], [pl.BlockSpec((th, QK_CAT), lambda i, h: (i, h))],
                             head6, "b_qrope")
            o_b, lse, lse_row = _att_fwd(qc, kcat, vmla, "b_attn")
            o_m = _mem_fwd(hb, 1, mkv[l], "mem_attn_b")
            sv.update(hb=hb, cqn=cqn, qc=qc, o_b=o_b, lse=(lse, lse_row))
            cat = jnp.concatenate([o_b, o_m], axis=1)
        sv["cat"] = cat
        x = _mm(cat, w["w_out"][l], "nn", F32, "w_out", res=x)
        x, sv["ffn2"] = _ffn_fwd(x, w["ffn2_norm"][l], w["ffn2_w_gu"][l], w["ffn2_w_down"][l], "ffn2")
        saved.append(sv)
        if l == N_A - 1:
            x_kv = x
            nkv = _rmsnorm_fwd(x, w["kv_in_norm"], "kv_in_norm")
            ckr = _mm(nkv, w["w_dkv"], "nn", BF16, "kv_down")
            ckv_n = _rmsnorm_fwd(ckr, w["kv_lat_norm"], "kv_lat_norm", 0, KV_LORA)
            kvu = _mm(ckv_n, w["w_ukv"], "nn", BF16, "kv_up")
            vmla = kvu[:, B_HEADS * QK_CAT:]
            (kcat,) = _rowcall(_kcat, [kvu, ckr, ckr, rope_c, rope_s],
                               [pl.BlockSpec((th, QK_CAT), lambda i, h: (i, h)),
                                pl.BlockSpec((th, QK_CAT), lambda i, h: (i, 1)),
                                pl.BlockSpec((th, QK_CAT), lambda i, h: (i, 2)),
                                pl.BlockSpec((th, QK_CAT), lambda i, h: (i, 0)),
                                pl.BlockSpec((th, QK_CAT), lambda i, h: (i, 0))],
                               [_sds((t, B_HEADS * QK_CAT), BF16)],
                               [pl.BlockSpec((th, QK_CAT), lambda i, h: (i, h))], head6, "kv_cat")

    def loss_fn(xb, gb, tb):
        def f(xx, gg):
            e = _rms(xx, gg) - tb
            return 0.5 * jnp.sum(jnp.mean(e * e, axis=-1, keepdims=True), axis=0, keepdims=True)
        val, vjp = jax.vjp(f, xb, gb)
        dx, dg = vjp(jnp.ones((1, 1), F32))
        return dx, dg, val * jnp.ones((1, 128), F32)

    d, dfin, loss = _rowcall(loss_fn, [x, w["final_norm"].reshape(1, D), target],
                             [_rows(TM, D), _shared((1, D)), _rows(TM, D)],
                             [_sds((t, D), F32), _sds((1, D), F32), _sds((1, 128), F32)],
                             [_rows(TM, D), _shared((1, D)), _shared((1, 128))], (t // TM,), "loss_head", n_acc=2)
    g["final_norm"] = dfin[0]
    loss = loss[0, 0]

    for name in ("ffn1_norm", "ffn1_w_gu", "ffn1_w_down", "mix_norm", "ffn2_norm", "ffn2_w_gu", "ffn2_w_down",
                 "w_out", "w_mem_kv"):
        g[name] = [None] * DEPTH
    for name in ("a_w_in", "a_conv", "a_A_log", "a_dt_bias", "a_out_norm"):
        g[name] = [None] * N_A
    for name in ("b_w_in", "b_q_norm", "b_w_uq"):
        g[name] = [None] * N_B
    dmkv = [None] * DEPTH
    dkcat = []
    dvmla = []

    for l in reversed(range(DEPTH)):
        sv = saved[l]
        if l == N_A - 1:
            kq = pl.BlockSpec((th, QK_CAT), lambda i, h: (i, h))
            tab = pl.BlockSpec((th, QK_CAT), lambda i, h: (i, 0))

            def dk_fn(c, s, d0, d1):
                dk = d0.astype(F32) + d1.astype(F32)
                return dk, dk * c, dk * s

            dkn, dkr_h, dkrs_h = _rowcall(dk_fn, [rope_c, rope_s, dkcat[0], dkcat[1]], [tab, tab, kq, kq],
                                          [_sds((t, B_HEADS * QK_CAT), BF16)] + [_sds((B_HEADS, t, QK_CAT), BF16)] * 2,
                                          [kq] + [pl.BlockSpec((None, th, QK_CAT), lambda i, h: (h, i, 0))] * 2,
                                          head6, "kv_dcat")

            def sum6(a, b):
                return jnp.sum(a.astype(F32), axis=0), jnp.sum(b.astype(F32), axis=0)

            h6 = pl.BlockSpec((B_HEADS, TM, QK_CAT), lambda i: (0, i, 0))
            dkr, dkrs = _rowcall(sum6, [dkr_h, dkrs_h], [h6, h6], [_sds((t, QK_CAT), BF16)] * 2,
                                 [_rows(TM, QK_CAT)] * 2, (t // TM,), "kv_dkr")

            def addv(a, b):
                return a.astype(F32) + b.astype(F32)

            (dv,) = _rowcall(addv, dvmla, [_rows(TM, B_HEADS * HD)] * 2, [_sds((t, B_HEADS * HD), BF16)],
                             [_rows(TM, B_HEADS * HD)], (t // TM,), "kv_dv")
            dkvu = jnp.concatenate([dkn, dv], axis=1)
            g["w_ukv"] = _mm(ckv_n, dkvu, "tn", F32, "kv_up_dw")
            dckv_n = _mm(dkvu, w["w_ukv"], "nt", BF16, "kv_up_dx")

            def lat_bwd(cb, gb, dnb):
                return _vjp_fn(_rms, 2, (0, 1))(cb, gb, dnb)

            dckv, g["kv_lat_norm"] = _rowcall(lat_bwd, [ckr, w["kv_lat_norm"].reshape(1, KV_LORA), dckv_n],
                                              [_rows(TM, KV_LORA), _shared((1, KV_LORA)), _rows(TM, KV_LORA)],
                                              [_sds((t, KV_LORA), BF16), _sds((1, KV_LORA), F32)],
                                              [_rows(TM, KV_LORA), _shared((1, KV_LORA))], (t // TM,),
                                              "kv_lat_dnorm", n_acc=1)
            g["kv_lat_norm"] = g["kv_lat_norm"][0]
            dckr = jnp.concatenate([dckv, dkr, dkrs], axis=1)
            g["w_dkv"] = _mm(nkv, dckr, "tn", F32, "kv_down_dw")
            dnkv = _mm(dckr, w["w_dkv"], "nt", BF16, "kv_down_dx")
            d, g["kv_in_norm"] = _rmsnorm_bwd(x_kv, w["kv_in_norm"], dnkv, d, "kv_in_dnorm")

        d, g["ffn2_norm"][l], g["ffn2_w_gu"][l], g["ffn2_w_down"][l] = _ffn_bwd(
            d, sv["ffn2"], w["ffn2_norm"][l], w["ffn2_w_gu"][l], w["ffn2_w_down"][l], "ffn2b")
        g["w_out"][l] = _mm(sv["cat"], d, "tn", F32, "w_out_dw")
        dcat = _mm(d, w["w_out"][l], "nt", BF16, "w_out_dx")
        if l < N_A:
            ha, yc3, qkv3, states, o_gdn = sv["ha"], sv["yc3"], sv["qkv3"], sv["states"], sv["o_gdn"]
            dmq, dmkv[l] = _mem_bwd(ha, A_MQ_BLK, mkv[l], dcat, 3, "mem_attn_a_bwd")
            hblk = pl.BlockSpec((th, HD), lambda i, h: (i, h))
            do_gdn, dgate, dgain = _rowcall(
                _vjp_fn(_outnorm_gate, 3, (0, 1, 2)), [o_gdn, ha, w["a_out_norm"][l].reshape(1, HD), dcat],
                [hblk, pl.BlockSpec((th, HD), lambda i, h: (i, 3 * A_HEADS + h)), _shared((1, HD)), hblk],
                [_sds((t, A_WIDTH), BF16), _sds((t, A_WIDTH), BF16), _sds((1, HD), F32)],
                [hblk, hblk, _shared((1, HD))], head6, "a_outnorm_bwd", n_acc=1)
            g["a_out_norm"][l] = dgain[0]
            dqkv3, dbg6 = _gdn_bwd(qkv3, sv["bg"], states, sv["pinvs"], do_gdn, "a_gdn_bwd")

            def dgates(bab, alb, dtb, d6):
                return _vjp_fn(_gates, 3, (0, 1, 2))(bab, alb, dtb, jnp.sum(d6, axis=0))

            dba, dalog, ddt = _rowcall(
                dgates, [ha, _pad128(w["a_A_log"][l], A_HEADS), _pad128(w["a_dt_bias"][l], A_HEADS), dbg6],
                [_rows(TM, 128, A_BA_BLK), _shared((1, 128)), _shared((1, 128)),
                 pl.BlockSpec((A_HEADS, TM, 128), lambda i: (0, i, 0))],
                [_sds((t, 128), BF16), _sds((1, 128), F32), _sds((1, 128), F32)],
                [_rows(TM, 128), _shared((1, 128)), _shared((1, 128))], (t // TM,), "a_gates_bwd", n_acc=2)
            g["a_A_log"][l] = dalog[0, A_HEADS:2 * A_HEADS]
            g["a_dt_bias"][l] = ddt[0, A_HEADS:2 * A_HEADS]
            blk3 = pl.BlockSpec((3, th, HD), lambda i, h: (0, i, h))
            def dprep(b, db):
                return jnp.stack(_vjp_fn(_gdn_prep, 3, (0, 1, 2))(b[0], b[1], b[2], db[0], db[1], db[2]))

            (dyc3,) = _rowcall(dprep, [yc3, dqkv3], [blk3, blk3],
                               [_sds((3, t, A_WIDTH), BF16)], [blk3], head6, "a_prep_bwd")
            dqkv_in, g["a_conv"][l] = _conv_bwd(ha, w["a_conv"][l], dyc3, "a_conv_bwd")
            dha = jnp.concatenate([dqkv_in, dgate, dmq, dba], axis=1)
            g["a_w_in"][l] = _mm(sv["n2"], dha, "tn", F32, "a_in_dw")
            dn2 = _mm(dha, w["a_w_in"][l], "nt", BF16, "a_in_dx")
        else:
            j = l - N_A
            hb, cqn, qc, o_b, lse = sv["hb"], sv["cqn"], sv["qc"], sv["o_b"], sv["lse"]
            dmq, dmkv[l] = _mem_bwd(hb, 1, mkv[l], dcat, 3, "mem_attn_b_bwd")
            dqc, dkc, dvv = _att_bwd(qc, kcat, vmla, o_b, lse[0], lse[1], dcat, "b_attn_bwd")
            dkcat.append(dkc)
            dvmla.append(dvv)
            kq = pl.BlockSpec((th, QK_CAT), lambda i, h: (i, h))
            tab = pl.BlockSpec((th, QK_CAT), lambda i, h: (i, 0))

            def dq_fn(c, s, dq):
                dq = dq.astype(F32)
                return jnp.stack([dq * c, dq * s])

            (dqq,) = _rowcall(dq_fn, [rope_c, rope_s, dqc], [tab, tab, kq],
                              [_sds((2, t, B_HEADS * QK_CAT), BF16)],
                              [pl.BlockSpec((2, th, QK_CAT), lambda i, h: (0, i, h))], head6, "b_qrope_bwd")
            dqq = jnp.concatenate([dqq[0], dqq[1]], axis=1)
            g["b_w_uq"][j] = _mm(cqn, dqq, "tn", F32, "b_uq_dw")
            dcqn = _mm(dqq, w["b_w_uq"][j], "nt", BF16, "b_uq_dx")
            dcq, dqg = _rowcall(_vjp_fn(_rms, 2, (0, 1)), [hb, w["b_q_norm"][j].reshape(1, Q_LORA), dcqn],
                                [_rows(TM, Q_LORA), _shared((1, Q_LORA)), _rows(TM, Q_LORA)],
                                [_sds((t, Q_LORA), BF16), _sds((1, Q_LORA), F32)],
                                [_rows(TM, Q_LORA), _shared((1, Q_LORA))], (t // TM,), "b_qnorm_bwd", n_acc=1)
            g["b_q_norm"][j] = dqg[0]
            dhb = jnp.concatenate([dcq, dmq], axis=1)
            g["b_w_in"][j] = _mm(sv["n2"], dhb, "tn", F32, "b_in_dw")
            dn2 = _mm(dhb, w["b_w_in"][j], "nt", BF16, "b_in_dx")
        d, g["mix_norm"][l] = _rmsnorm_bwd(sv["x1"], w["mix_norm"][l], dn2, d, "mix_dnorm")
        d = mixer_done(l, g, d)
        d, g["ffn1_norm"][l], g["ffn1_w_gu"][l], g["ffn1_w_down"][l] = _ffn_bwd(
            d, sv["ffn1"], w["ffn1_norm"][l], w["ffn1_w_gu"][l], w["ffn1_w_down"][l], "ffn1b")
        d = layer_done(l, g, d)

    dmem_n = None
    for l in range(DEPTH):
        g["w_mem_kv"][l] = _mm(mem_n, dmkv[l], "tn", F32, f"mkv_dw{l}")
        dmem_n = _mm(dmkv[l], w["w_mem_kv"][l], "nt", F32, f"mkv_dx{l}", res=dmem_n)
    (_, gmn) = _rowcall(_vjp_fn(_rms, 2, (0, 1)), [mem, w["mem_norm"].reshape(1, D), dmem_n],
                        [_shared((N_MEM, D)), _shared((1, D)), _shared((N_MEM, D))],
                        [_sds((N_MEM, D), F32), _sds((1, D), F32)], [_shared((N_MEM, D)), _shared((1, D))],
                        (1,), "mem_dnorm")
    g["mem_norm"] = gmn[0]
    return loss, d, g


def _pad128(v, offset):
    return jnp.pad(v.astype(F32).reshape(1, -1), ((0, 0), (offset, 128 - offset - v.shape[0])))


def _rmsnorm_fwd_small(x, gain):
    r, w = x.shape
    (n,) = _rowcall(_rms, [x, gain.reshape(1, w)], [_shared((r, w)), _shared((1, w))],
                    [_sds((r, w), BF16)], [_shared((r, w))], (1,), "mem_norm")
    return n


def _exchange(srcs, gather, name):
    n = len(srcs)
    blks = [tuple(s.shape) if gather else tuple(s.shape[1:]) for s in srcs]

    def body(*refs):
        src_refs, out_refs = refs[:n], refs[n:2 * n]
        send_sems, recv_sems, local_sems = refs[2 * n:]
        x, y, c = lax.axis_index("x"), lax.axis_index("y"), lax.axis_index("c")
        me = 4 * x + 2 * y + c
        copies = []
        for k in range(1, N_DEV):
            px = (x + (k >> 2 & 1)) % 2
            py = (y + (k >> 1 & 1)) % 2
            pc = (c + (k & 1)) % 2
            peer = 4 * px + 2 * py + pc
            for a in range(n):
                cp = pltpu.make_async_remote_copy(
                    src_ref=src_refs[a] if gather else src_refs[a].at[peer], dst_ref=out_refs[a].at[me],
                    send_sem=send_sems.at[a, k - 1], recv_sem=recv_sems.at[a, k - 1],
                    device_id=(px, py, pc), device_id_type=pl.DeviceIdType.MESH)
                cp.start()
                copies.append(cp)
        for a in range(n):
            cp = pltpu.make_async_copy(src_refs[a] if gather else src_refs[a].at[me], out_refs[a].at[me],
                                       local_sems.at[a])
            cp.start()
            copies.append(cp)
        for cp in copies:
            cp.wait()

    return pl.pallas_call(
        body, out_shape=tuple(_sds((N_DEV,) + b, s.dtype) for b, s in zip(blks, srcs)),
        in_specs=[pl.BlockSpec(memory_space=pl.ANY)] * n, out_specs=tuple([pl.BlockSpec(memory_space=pl.ANY)] * n),
        scratch_shapes=[pltpu.SemaphoreType.DMA((n, N_DEV - 1)), pltpu.SemaphoreType.DMA((n, N_DEV - 1)),
                        pltpu.SemaphoreType.DMA((n,))],
        name=name)(*srcs)


_HBM = pl.BlockSpec(memory_space=pltpu.HBM)
_SEM = pl.BlockSpec(memory_space=pltpu.SEMAPHORE)
_EFFECT = pltpu.SideEffectType.DATAFLOW_SIDE_EFFECTING


def _split_copies(src_refs, land_refs, send_sems, recv_sems, gather):
    x, y, c = lax.axis_index("x"), lax.axis_index("y"), lax.axis_index("c")
    me = 4 * x + 2 * y + c
    copies = []
    for k in range(1, N_DEV):
        peer = ((x + (k >> 2 & 1)) % 2, (y + (k >> 1 & 1)) % 2, (c + (k & 1)) % 2)
        for a in range(len(src_refs)):
            i = (k - 1) * len(src_refs) + a
            src = src_refs[a] if gather else src_refs[a].at[4 * peer[0] + 2 * peer[1] + peer[2]]
            copies.append(pltpu.make_async_remote_copy(
                src_ref=src, dst_ref=land_refs[a].at[me], send_sem=send_sems[i],
                recv_sem=recv_sems[i], device_id=peer, device_id_type=pl.DeviceIdType.MESH))
    return copies


def _split_start(srcs, gather, name):
    n = len(srcs)
    srcs = [pltpu.with_memory_space_constraint(s, pltpu.HBM) for s in srcs]
    lands = [pltpu.with_memory_space_constraint(
        lax.empty(((N_DEV,) + s.shape) if gather else s.shape, s.dtype), pltpu.HBM) for s in srcs]

    ns = n * (N_DEV - 1)

    def body(*refs):
        sems = refs[2 * n:2 * n + 2 * ns]
        for cp in _split_copies(refs[:n], refs[n:2 * n], sems[:ns], sems[ns:], gather):
            cp.start()
        refs[-1][...] = jnp.zeros_like(refs[-1])

    outs = pl.pallas_call(
        body, name=name,
        out_shape=(*[pltpu.SemaphoreType.DMA(())] * (2 * ns), *[pltpu.HBM(a.shape, a.dtype) for a in srcs + lands],
                   _sds((8, 128), F32)),
        in_specs=[_HBM] * (2 * n),
        out_specs=(*[_SEM] * (2 * ns), *[_HBM] * (2 * n), pl.BlockSpec(memory_space=pltpu.VMEM)),
        input_output_aliases={i: 2 * ns + i for i in range(2 * n)},
        compiler_params=pltpu.CompilerParams(has_side_effects=_EFFECT))(*srcs, *lands)
    sems, rest = list(outs[:2 * ns]), outs[2 * ns:]
    return sems[:ns], sems[ns:], list(rest[:n]), list(rest[n:2 * n]), rest[-1]


def _split_wait(flight, after, gather, name):
    send_sems, recv_sems, srcs, lands, _ = flight
    n = len(srcs)

    ns = len(send_sems)

    def body(*refs):
        sems = refs[2 * n:2 * n + 2 * ns]
        for cp in _split_copies(refs[:n], refs[n:2 * n], sems[:ns], sems[ns:], gather):
            cp.wait_send()
            cp.wait_recv()

    outs = pl.pallas_call(
        body, name=name, out_shape=tuple(pltpu.HBM(a.shape, a.dtype) for a in srcs + lands),
        in_specs=[_HBM] * (2 * n) + [_SEM] * (2 * ns) + [pl.BlockSpec(memory_space=pl.ANY)],
        out_specs=tuple([_HBM] * (2 * n)), input_output_aliases={i: i for i in range(2 * n)},
        compiler_params=pltpu.CompilerParams(has_side_effects=_EFFECT))(*srcs, *lands, *send_sems, *recv_sems, after)
    return list(outs[n:])


def _reduce_adamw(parts, wp, mp, vp, name):
    r, cols = wp.shape
    tr = _tile_rows(r, cols)
    c1 = 1.0 - ADAM_B1 ** ADAM_STEP
    c2 = 1.0 - ADAM_B2 ** ADAM_STEP

    def fn(pb, wb, mb, vb):
        gsum = pb[0].astype(F32)
        for j in range(1, N_DEV):
            gsum = gsum + pb[j].astype(F32)
        m_new = ADAM_B1 * mb + (1.0 - ADAM_B1) * gsum
        v_new = ADAM_B2 * vb + (1.0 - ADAM_B2) * (gsum * gsum)
        delta = -ADAM_LR * ((m_new / c1) / (jnp.sqrt(v_new / c2) + ADAM_EPS) + ADAM_WD * wb)
        return gsum, delta, m_new, v_new

    row = _rows(tr, cols)
    return _rowcall(fn, [parts, wp, mp, vp],
                    [pl.BlockSpec((N_DEV, tr, cols), lambda i: (0, i, 0)), row, row, row],
                    [_sds((r, cols), F32)] * 4, [row] * 4, (r // tr,), name)


def _tile_rows(r, cols):
    for t in (512, 256, 128, 64, 32, 16):
        if r % t == 0 and t * cols <= 160 * 1024:
            return t
    return r


def _pack(arrs):
    flat = jnp.concatenate([a.reshape(-1).astype(F32) for a in arrs])
    n = flat.shape[0]
    unit = PACK_W * PACK_ROWS
    tot = -(-n // unit) * unit
    return jnp.pad(flat, (0, tot - n)).reshape(tot // PACK_W, PACK_W)


def _unpack(buf, shapes):
    out, off = [], 0
    flat = buf.reshape(-1)
    for s in shapes:
        n = int(np.prod(s))
        out.append(flat[off:off + n].reshape(s))
        off += n
    return out


def _as2d(a):
    return a.reshape(-1, a.shape[-1])


_SHARDED = ["ffn1_w_gu", "ffn1_w_down", "ffn2_w_gu", "ffn2_w_down", "w_out", "w_mem_kv", "a_w_in", "a_conv",
            "b_w_in", "b_w_uq", "w_dkv", "w_ukv"]
_COL_SHARDED = {"ffn1_w_gu", "ffn2_w_gu", "a_conv", "b_w_uq", "w_ukv"}
_LAYERED = {"ffn1_w_gu": DEPTH, "ffn1_w_down": DEPTH, "ffn2_w_gu": DEPTH, "ffn2_w_down": DEPTH, "w_out": DEPTH,
            "w_mem_kv": DEPTH, "a_w_in": N_A, "a_conv": N_A, "b_w_in": N_B, "b_w_uq": N_B}
_GATHER_FIRST = [("ffn1_w_gu", 0, 1), ("ffn1_w_down", 0, 1)]
_GATHER_MIX0 = [("ffn2_w_gu", 0, 1), ("ffn2_w_down", 0, 1), ("w_out", 0, 1), ("a_w_in", 0, 1), ("a_conv", 0, 1),
                ("w_mem_kv", 0, DEPTH)]
_GATHER_REST = [("ffn1_w_gu", 1, DEPTH), ("ffn1_w_down", 1, DEPTH), ("ffn2_w_gu", 1, DEPTH),
                ("ffn2_w_down", 1, DEPTH), ("w_out", 1, DEPTH), ("a_w_in", 1, N_A), ("a_conv", 1, N_A),
                ("b_w_in", 0, N_B), ("b_w_uq", 0, N_B), ("w_dkv", None, None), ("w_ukv", None, None)]
_SCATTER_HI = [("ffn1_w_gu", 2, DEPTH), ("ffn1_w_down", 2, DEPTH), ("ffn2_w_gu", 2, DEPTH), ("ffn2_w_down", 2, DEPTH),
               ("w_out", 2, DEPTH), ("b_w_in", 0, N_B), ("b_w_uq", 0, N_B)]
_SCATTER_MID = [("ffn1_w_gu", 1, 2), ("ffn1_w_down", 1, 2), ("ffn2_w_gu", 1, 2), ("ffn2_w_down", 1, 2),
                ("w_out", 1, 2), ("a_w_in", 1, N_A), ("a_conv", 1, N_A), ("w_dkv", None, None), ("w_ukv", None, None)]
_SCATTER_MIX0 = [("ffn2_w_gu", 0, 1), ("ffn2_w_down", 0, 1), ("w_out", 0, 1), ("a_w_in", 0, 1), ("a_conv", 0, 1)]
_SCATTER_LO = [("ffn1_w_gu", 0, 1), ("ffn1_w_down", 0, 1), ("w_mem_kv", 0, DEPTH)]
_REPLICATED = ["ffn1_norm", "mix_norm", "ffn2_norm", "mem_norm", "a_A_log", "a_dt_bias", "a_out_norm", "b_q_norm",
               "kv_in_norm", "kv_lat_norm", "final_norm"]
_WEIGHTS = ["ffn1_norm", "ffn1_w_gu", "ffn1_w_down", "mix_norm", "ffn2_norm", "ffn2_w_gu", "ffn2_w_down", "w_out",
            "mem_norm", "w_mem_kv", "a_w_in", "a_conv", "a_A_log", "a_dt_bias", "a_out_norm", "b_w_in", "b_q_norm",
            "b_w_uq", "kv_in_norm", "w_dkv", "kv_lat_norm", "w_ukv", "final_norm"]


def _full_from_shards(name, sh):
    if name in ("ffn1_w_gu", "ffn2_w_gu"):
        return sh
    if name in ("ffn1_w_down", "ffn2_w_down"):
        return sh.reshape(4, FF_SHARD, D)
    if name in _COL_SHARDED:
        return jnp.moveaxis(sh, 0, -2).reshape(sh.shape[1:-1] + (N_DEV * sh.shape[-1],))
    return sh.reshape((N_DEV * sh.shape[1],) + sh.shape[2:])


def _shards_from_full(name, full):
    if name in ("ffn1_w_gu", "ffn2_w_gu"):
        return full
    if name in ("ffn1_w_down", "ffn2_w_down"):
        return full.reshape(N_DEV, D_FF // N_DEV, D)
    if name in _COL_SHARDED:
        r, cc = full.shape
        return jnp.moveaxis(full.reshape(r, N_DEV, cc // N_DEV), 1, 0)
    return full.reshape((N_DEV, full.shape[0] // N_DEV) + full.shape[1:])


def kernel(x, mem, positions, ffn1_norm, ffn1_w_gu, ffn1_w_down, mix_norm, ffn2_norm, ffn2_w_gu, ffn2_w_down, w_out, mem_norm, w_mem_kv, a_w_in, a_conv, a_A_log, a_dt_bias, a_out_norm, b_w_in, b_q_norm, b_w_uq, kv_in_norm, w_dkv, kv_lat_norm, w_ukv, final_norm, loss_target, m_ffn1_norm, m_ffn1_w_gu, m_ffn1_w_down, m_mix_norm, m_ffn2_norm, m_ffn2_w_gu, m_ffn2_w_down, m_w_out, m_mem_norm, m_w_mem_kv, m_a_w_in, m_a_conv, m_a_A_log, m_a_dt_bias, m_a_out_norm, m_b_w_in, m_b_q_norm, m_b_w_uq, m_kv_in_norm, m_w_dkv, m_kv_lat_norm, m_w_ukv, m_final_norm, v_ffn1_norm, v_ffn1_w_gu, v_ffn1_w_down, v_mix_norm, v_ffn2_norm, v_ffn2_w_gu, v_ffn2_w_down, v_w_out, v_mem_norm, v_w_mem_kv, v_a_w_in, v_a_conv, v_a_A_log, v_a_dt_bias, v_a_out_norm, v_b_w_in, v_b_q_norm, v_b_w_uq, v_kv_in_norm, v_w_dkv, v_kv_lat_norm, v_w_ukv, v_final_norm):
    loc = dict(locals())
    wl = {n: loc[n] for n in _WEIGHTS}
    ml = {n: loc["m_" + n] for n in _WEIGHTS}
    vl = {n: loc["v_" + n] for n in _WEIGHTS}

    me = 4 * lax.axis_index("x") + 2 * lax.axis_index("y") + lax.axis_index("c")
    w = {n: wl[n] for n in _REPLICATED}
    for n in _SHARDED:
        w[n] = [None] * _LAYERED[n] if n in _LAYERED else None
    back = {}

    def src_of(item):
        n, lo, hi = item
        return (wl[n] if lo is None else wl[n][lo:hi]).astype(BF16)

    def install(items, pieces):
        for (n, lo, hi), p in zip(items, pieces):
            if lo is None:
                w[n] = _full_from_shards(n, p)
            else:
                for l in range(lo, hi):
                    w[n][l] = _full_from_shards(n, p[:, l - lo])
        for n, lo, hi in items:
            if n == "a_w_in":
                for l in range(lo, hi):
                    w[n][l], back[(n, l)] = _ext_and_back(_a_in_ext, w[n][l])
            elif n == "a_conv":
                for l in range(lo, hi):
                    w[n][l] = w[n][l].astype(F32)
            elif n == "b_w_uq":
                for l in range(lo, hi):
                    w[n][l], back[(n, l)] = _ext_and_back(_uq_ext, w[n][l])
            elif n == "w_dkv":
                w[n], back[n] = _ext_and_back(_dkv_ext, w[n])
            elif n == "w_ukv":
                w[n], back[n] = _ext_and_back(_ukv_ext, w[n])

    first = _exchange([src_of(it) for it in _GATHER_FIRST], True, "gather_first")
    later = {key: [src_of(it) for it in items] for key, items in (("mix0", _GATHER_MIX0), ("rest", _GATHER_REST))}
    first, later = lax.optimization_barrier((first, later))
    install(_GATHER_FIRST, first)
    flights = {}
    for key, items in (("mix0", _GATHER_MIX0), ("rest", _GATHER_REST)):
        srcs = later[key]
        flights[key] = (items, srcs, _split_start(srcs, True, f"gather_{key}_start"))
        w["ffn1_norm"] = w["ffn1_norm"] + flights[key][2][-1][0, 0]

    def arrive(key, stream):
        items, srcs, fl = flights[key]
        lands = _split_wait(fl, stream, True, f"gather_{key}_wait")
        install(items, [lax.dynamic_update_slice(ld, s[None], (me,) + (0,) * s.ndim) for ld, s in zip(lands, srcs)])

    def layer_start(l, stream):
        if l == 1:
            arrive("rest", stream)

    def mixer_start(l, stream):
        if l == 0:
            arrive("mix0", stream)

    def grad_src(item, g):
        n, lo, hi = item

        def one(l):
            gl = g[n] if l is None else g[n][l]
            key = n if l is None else (n, l)
            if key in back:
                gl = back[key](gl)
            return _shards_from_full(n, gl).astype(BF16)

        return one(None) if lo is None else jnp.stack([one(l) for l in range(lo, hi)], axis=1)

    sent = {}

    def depart(key, items, g, d):
        srcs = [grad_src(it, g) for it in items]
        sent[key] = (items, srcs, _split_start(srcs, False, f"scatter_{key}_start"))
        return d + sent[key][2][-1][0, 0]

    def layer_done(l, g, d):
        if l == 2:
            return depart("hi", _SCATTER_HI, g, d)
        if l == 1:
            return depart("mid", _SCATTER_MID, g, d)
        return d

    def mixer_done(l, g, d):
        return depart("mix0", _SCATTER_MIX0, g, d) if l == 0 else d

    loss, dx, g = _local_step(x[0], mem[0], positions[0], loss_target[0], w, layer_start, mixer_start,
                              mixer_done, layer_done)
    loss = lax.psum(loss, ("x", "y", "c"))

    lo_srcs = [grad_src(it, g) for it in _SCATTER_LO]
    pieces = {n: [] for n in _SHARDED}
    for (n, lo, hi), p in zip(_SCATTER_LO, _exchange(lo_srcs, False, "scatter_grads")):
        pieces[n].append((lo, p))
    for key in ("mix0", "mid", "hi"):
        items, srcs, fl = sent[key]
        lands = _split_wait(fl, dx, False, f"scatter_{key}_wait")
        for (n, lo, hi), s, ld in zip(items, srcs, lands):
            mine = lax.dynamic_slice(s, (me,) + (0,) * (s.ndim - 1), (1,) + s.shape[1:])
            pieces[n].append((lo, lax.dynamic_update_slice(ld, mine, (me,) + (0,) * (s.ndim - 1))))
    out = {}
    for n in _SHARDED:
        ps = [p for _, p in sorted(pieces[n], key=lambda e: -1 if e[0] is None else e[0])]
        p = ps[0] if len(ps) == 1 else jnp.concatenate(ps, axis=1)
        shape = wl[n].shape
        res = _reduce_adamw(p.reshape(N_DEV, -1, shape[-1]), _as2d(wl[n]), _as2d(ml[n]), _as2d(vl[n]), "adamw_" + n)
        for kind, buf in zip(("grad", "delta", "new_m", "new_v"), res):
            out[(kind, n)] = buf.reshape(shape)

    rep_shapes = [wl[n].shape for n in _REPLICATED]
    grep = [jnp.stack(g[n]) if isinstance(g[n], list) else g[n] for n in _REPLICATED]
    (rparts,) = _exchange([_pack(grep)], True, "gather_small_grads")
    res = _reduce_adamw(rparts, _pack([wl[n] for n in _REPLICATED]), _pack([ml[n] for n in _REPLICATED]),
                        _pack([vl[n] for n in _REPLICATED]), "adamw_replicated")
    for kind, buf in zip(("grad", "delta", "new_m", "new_v"), res):
        for n, a in zip(_REPLICATED, _unpack(buf, rep_shapes)):
            out[(kind, n)] = a

    return (loss, dx[None], *[out[("grad", n)] for n in _WEIGHTS], *[out[("delta", n)] for n in _WEIGHTS],
            *[out[("new_m", n)] for n in _WEIGHTS], *[out[("new_v", n)] for n in _WEIGHTS])
```
